```python
import jax, jax.numpy as jnp
from jax import lax
import numpy as np

D_MODEL = 2048
BATCH = 8
SEQ = 2048
DEPTH = 2

CHUNK = 64
EPS = 1e-6
D_POOL = D_MODEL // 2
POOL_WINDOWS = (2, 4, 8, 16)
N_POOL_GROUPS = len(POOL_WINDOWS)
POOL_GROUP = D_POOL // N_POOL_GROUPS
D_CONV = D_MODEL // 2
CONV_K = 31
D_AB_IN = D_POOL + 2 * D_CONV
D_SHORT = D_MODEL
SHORT_K = 3
D_FF = 4 * D_MODEL

N_EVEN = (DEPTH + 1) // 2
N_ODD = DEPTH // 2

kernel_name = "hybrid_pool_conformer_shortconv_trunk"


def rms_norm(x, g):
    xf = x.astype(jnp.float32)
    y = xf * lax.rsqrt(jnp.mean(xf * xf, axis=-1, keepdims=True) + EPS)
    return (y * g.astype(jnp.float32)).astype(x.dtype)


def layer_norm(x, g, b):
    xf = x.astype(jnp.float32)
    mu = jnp.mean(xf, axis=-1, keepdims=True)
    xc = xf - mu
    var = jnp.mean(xc * xc, axis=-1, keepdims=True)
    y = xc * lax.rsqrt(var + EPS) * g.astype(jnp.float32) + b.astype(jnp.float32)
    return y.astype(x.dtype)


def causal_depthwise_conv(u, w):
    k = w.shape[0]
    return lax.conv_general_dilated(
        u, w[:, None, :].astype(u.dtype), window_strides=(1,), padding=[(k - 1, 0)],
        dimension_numbers=("NWC", "WIO", "NWC"), feature_group_count=u.shape[-1])


def multiscale_pool(u, pool_w, pool_scale):
    b, t, _ = u.shape
    uf = u.astype(jnp.float32)
    csp = jnp.pad(jnp.cumsum(uf, axis=1), ((0, 0), (1, 0), (0, 0)))
    n_valid = jnp.arange(1, t + 1, dtype=jnp.float32)
    means = []
    for g, w in enumerate(POOL_WINDOWS):
        c = csp[..., g * POOL_GROUP:(g + 1) * POOL_GROUP]
        lag = jnp.pad(c, ((0, 0), (w - 1, 0), (0, 0)))[:, :t]
        cnt = jnp.minimum(n_valid, float(w))[None, :, None]
        means.append((c[:, 1:] - lag) / cnt)
    pooled = (jnp.concatenate(means, axis=-1) - uf).astype(u.dtype)
    pooled = pooled.reshape(b, t, N_POOL_GROUPS, POOL_GROUP)
    mixed = jnp.einsum("btgc,gce->btge", pooled, pool_w).reshape(b, t, D_POOL)
    return mixed * pool_scale


def pool_conformer_mixer(h, w_in, pool_w, pool_scale, conv_w, conv_b, ln_g, ln_b, w_out):
    z = jnp.einsum("btd,de->bte", h, w_in)
    u_pool = z[..., :D_POOL]
    v = z[..., D_POOL:D_POOL + D_CONV]
    gate = z[..., D_POOL + D_CONV:]
    y_pool = multiscale_pool(u_pool, pool_w, pool_scale)
    c = causal_depthwise_conv(v * jax.nn.sigmoid(gate), conv_w) + conv_b
    y_conv = jax.nn.silu(layer_norm(c, ln_g, ln_b))
    y = jnp.concatenate([y_pool, y_conv], axis=-1)
    return jnp.einsum("bte,ed->btd", y, w_out)


def short_conv_mixer(h, w_in, conv_w, w_out):
    z = jnp.einsum("btd,de->bte", h, w_in)
    b_gate = z[..., :D_SHORT]
    c_gate = z[..., D_SHORT:2 * D_SHORT]
    u = z[..., 2 * D_SHORT:]
    y = b_gate * causal_depthwise_conv(c_gate * u, conv_w)
    return jnp.einsum("bte,ed->btd", y, w_out)


def sq_relu_mlp(h, w1, w2):
    a = jax.nn.relu(jnp.einsum("btd,df->btf", h, w1))
    return jnp.einsum("btf,fd->btd", a * a, w2)


def _fwd_setup_inputs(seed: int = 0) -> dict:
    key = jax.random.key(seed)
    ks = jax.random.split(key, 20)
    f32 = jnp.float32

    def nrm(k, shape, scale):
        return jax.random.normal(k, shape, f32) * scale

    def gain(k, shape):
        return 1.0 + 0.02 * jax.random.normal(k, shape, f32)

    return {
        "x": jax.random.normal(ks[0], (BATCH, SEQ, D_MODEL), f32),
        "mix_pre_g": gain(ks[1], (DEPTH, D_MODEL)),
        "mix_post_g": gain(ks[2], (DEPTH, D_MODEL)),
        "ffn_pre_g": gain(ks[3], (DEPTH, D_MODEL)),
        "ffn_post_g": gain(ks[4], (DEPTH, D_MODEL)),
        "ab_w_in": nrm(ks[5], (N_EVEN, D_MODEL, D_AB_IN), D_MODEL ** -0.5),
        "pool_w": nrm(ks[6], (N_EVEN, N_POOL_GROUPS, POOL_GROUP, POOL_GROUP), POOL_GROUP ** -0.5),
        "pool_scale": 1.0 + 0.1 * jax.random.normal(ks[7], (N_EVEN, D_POOL), f32),
        "conv_w": nrm(ks[8], (N_EVEN, CONV_K, D_CONV), CONV_K ** -0.5),
        "conv_b": nrm(ks[9], (N_EVEN, D_CONV), 0.02),
        "conv_ln_g": gain(ks[10], (N_EVEN, D_CONV)),
        "conv_ln_b": nrm(ks[11], (N_EVEN, D_CONV), 0.02),
        "ab_w_out": nrm(ks[12], (N_EVEN, D_POOL + D_CONV, D_MODEL), (D_POOL + D_CONV) ** -0.5),
        "sc_w_in": nrm(ks[13], (N_ODD, D_MODEL, 3 * D_SHORT), D_MODEL ** -0.5),
        "sc_conv_w": nrm(ks[14], (N_ODD, SHORT_K, D_SHORT), SHORT_K ** -0.5),
        "sc_w_out": nrm(ks[15], (N_ODD, D_SHORT, D_MODEL), D_SHORT ** -0.5),
        "ffn_w1": nrm(ks[16], (DEPTH, D_MODEL, D_FF), D_MODEL ** -0.5),
        "ffn_w2": nrm(ks[17], (DEPTH, D_FF, D_MODEL), D_FF ** -0.5),
    }


def _fwd_reference(x, mix_pre_g, mix_post_g, ffn_pre_g, ffn_post_g, ab_w_in, pool_w, pool_scale,
              conv_w, conv_b, conv_ln_g, conv_ln_b, ab_w_out, sc_w_in, sc_conv_w, sc_w_out,
              ffn_w1, ffn_w2):
    for layer in range(DEPTH):
        i = layer // 2
        h = rms_norm(x, mix_pre_g[layer])
        if layer % 2 == 0:
            m = pool_conformer_mixer(h, ab_w_in[i], pool_w[i], pool_scale[i], conv_w[i], conv_b[i],
                                     conv_ln_g[i], conv_ln_b[i], ab_w_out[i])
        else:
            m = short_conv_mixer(h, sc_w_in[i], sc_conv_w[i], sc_w_out[i])
        x = x + rms_norm(m, mix_post_g[layer])
        h = rms_norm(x, ffn_pre_g[layer])
        x = x + rms_norm(sq_relu_mlp(h, ffn_w1[layer], ffn_w2[layer]), ffn_post_g[layer])
    return x


import jax as _jax
import jax.numpy as _jnp

TWIN_FORMAT = 'train_step'
FWD_PARAMS = ['x', 'mix_pre_g', 'mix_post_g', 'ffn_pre_g', 'ffn_post_g', 'ab_w_in', 'pool_w', 'pool_scale', 'conv_w', 'conv_b', 'conv_ln_g', 'conv_ln_b', 'ab_w_out', 'sc_w_in', 'sc_conv_w', 'sc_w_out', 'ffn_w1', 'ffn_w2']
TWIN_WEIGHTS = ['mix_pre_g', 'mix_post_g', 'ffn_pre_g', 'ffn_post_g', 'ab_w_in', 'pool_w', 'pool_scale', 'conv_w', 'conv_b', 'conv_ln_g', 'conv_ln_b', 'ab_w_out', 'sc_w_in', 'sc_conv_w', 'sc_w_out', 'ffn_w1', 'ffn_w2']
TWIN_DIFF_INPUT = 'x'
TWIN_INPUTS = ['x', 'mix_pre_g', 'mix_post_g', 'ffn_pre_g', 'ffn_post_g', 'ab_w_in', 'pool_w', 'pool_scale', 'conv_w', 'conv_b', 'conv_ln_g', 'conv_ln_b', 'ab_w_out', 'sc_w_in', 'sc_conv_w', 'sc_w_out', 'ffn_w1', 'ffn_w2', 'loss_target', 'm_mix_pre_g', 'm_mix_post_g', 'm_ffn_pre_g', 'm_ffn_post_g', 'm_ab_w_in', 'm_pool_w', 'm_pool_scale', 'm_conv_w', 'm_conv_b', 'm_conv_ln_g', 'm_conv_ln_b', 'm_ab_w_out', 'm_sc_w_in', 'm_sc_conv_w', 'm_sc_w_out', 'm_ffn_w1', 'm_ffn_w2', 'v_mix_pre_g', 'v_mix_post_g', 'v_ffn_pre_g', 'v_ffn_post_g', 'v_ab_w_in', 'v_pool_w', 'v_pool_scale', 'v_conv_w', 'v_conv_b', 'v_conv_ln_g', 'v_conv_ln_b', 'v_ab_w_out', 'v_sc_w_in', 'v_sc_conv_w', 'v_sc_w_out', 'v_ffn_w1', 'v_ffn_w2']
TWIN_OUTPUTS = ['loss', 'grad_x', 'grad_mix_pre_g', 'grad_mix_post_g', 'grad_ffn_pre_g', 'grad_ffn_post_g', 'grad_ab_w_in', 'grad_pool_w', 'grad_pool_scale', 'grad_conv_w', 'grad_conv_b', 'grad_conv_ln_g', 'grad_conv_ln_b', 'grad_ab_w_out', 'grad_sc_w_in', 'grad_sc_conv_w', 'grad_sc_w_out', 'grad_ffn_w1', 'grad_ffn_w2', 'delta_mix_pre_g', 'delta_mix_post_g', 'delta_ffn_pre_g', 'delta_ffn_post_g', 'delta_ab_w_in', 'delta_pool_w', 'delta_pool_scale', 'delta_conv_w', 'delta_conv_b', 'delta_conv_ln_g', 'delta_conv_ln_b', 'delta_ab_w_out', 'delta_sc_w_in', 'delta_sc_conv_w', 'delta_sc_w_out', 'delta_ffn_w1', 'delta_ffn_w2', 'new_m_mix_pre_g', 'new_m_mix_post_g', 'new_m_ffn_pre_g', 'new_m_ffn_post_g', 'new_m_ab_w_in', 'new_m_pool_w', 'new_m_pool_scale', 'new_m_conv_w', 'new_m_conv_b', 'new_m_conv_ln_g', 'new_m_conv_ln_b', 'new_m_ab_w_out', 'new_m_sc_w_in', 'new_m_sc_conv_w', 'new_m_sc_w_out', 'new_m_ffn_w1', 'new_m_ffn_w2', 'new_v_mix_pre_g', 'new_v_mix_post_g', 'new_v_ffn_pre_g', 'new_v_ffn_post_g', 'new_v_ab_w_in', 'new_v_pool_w', 'new_v_pool_scale', 'new_v_conv_w', 'new_v_conv_b', 'new_v_conv_ln_g', 'new_v_conv_ln_b', 'new_v_ab_w_out', 'new_v_sc_w_in', 'new_v_sc_conv_w', 'new_v_sc_w_out', 'new_v_ffn_w1', 'new_v_ffn_w2']
TWIN_LEAF_KINDS = {'loss': 'loss', 'grad_x': 'grad_x', 'grad_mix_pre_g': 'grad_w', 'grad_mix_post_g': 'grad_w', 'grad_ffn_pre_g': 'grad_w', 'grad_ffn_post_g': 'grad_w', 'grad_ab_w_in': 'grad_w', 'grad_pool_w': 'grad_w', 'grad_pool_scale': 'grad_w', 'grad_conv_w': 'grad_w', 'grad_conv_b': 'grad_w', 'grad_conv_ln_g': 'grad_w', 'grad_conv_ln_b': 'grad_w', 'grad_ab_w_out': 'grad_w', 'grad_sc_w_in': 'grad_w', 'grad_sc_conv_w': 'grad_w', 'grad_sc_w_out': 'grad_w', 'grad_ffn_w1': 'grad_w', 'grad_ffn_w2': 'grad_w', 'delta_mix_pre_g': 'delta_w', 'delta_mix_post_g': 'delta_w', 'delta_ffn_pre_g': 'delta_w', 'delta_ffn_post_g': 'delta_w', 'delta_ab_w_in': 'delta_w', 'delta_pool_w': 'delta_w', 'delta_pool_scale': 'delta_w', 'delta_conv_w': 'delta_w', 'delta_conv_b': 'delta_w', 'delta_conv_ln_g': 'delta_w', 'delta_conv_ln_b': 'delta_w', 'delta_ab_w_out': 'delta_w', 'delta_sc_w_in': 'delta_w', 'delta_sc_conv_w': 'delta_w', 'delta_sc_w_out': 'delta_w', 'delta_ffn_w1': 'delta_w', 'delta_ffn_w2': 'delta_w', 'new_m_mix_pre_g': 'new_m', 'new_m_mix_post_g': 'new_m', 'new_m_ffn_pre_g': 'new_m', 'new_m_ffn_post_g': 'new_m', 'new_m_ab_w_in': 'new_m', 'new_m_pool_w': 'new_m', 'new_m_pool_scale': 'new_m', 'new_m_conv_w': 'new_m', 'new_m_conv_b': 'new_m', 'new_m_conv_ln_g': 'new_m', 'new_m_conv_ln_b': 'new_m', 'new_m_ab_w_out': 'new_m', 'new_m_sc_w_in': 'new_m', 'new_m_sc_conv_w': 'new_m', 'new_m_sc_w_out': 'new_m', 'new_m_ffn_w1': 'new_m', 'new_m_ffn_w2': 'new_m', 'new_v_mix_pre_g': 'new_v', 'new_v_mix_post_g': 'new_v', 'new_v_ffn_pre_g': 'new_v', 'new_v_ffn_post_g': 'new_v', 'new_v_ab_w_in': 'new_v', 'new_v_pool_w': 'new_v', 'new_v_pool_scale': 'new_v', 'new_v_conv_w': 'new_v', 'new_v_conv_b': 'new_v', 'new_v_conv_ln_g': 'new_v', 'new_v_conv_ln_b': 'new_v', 'new_v_ab_w_out': 'new_v', 'new_v_sc_w_in': 'new_v', 'new_v_sc_conv_w': 'new_v', 'new_v_sc_w_out': 'new_v', 'new_v_ffn_w1': 'new_v', 'new_v_ffn_w2': 'new_v'}


def _forward(args):
    return _fwd_reference(*[args[k] for k in FWD_PARAMS])


def _output_shape():
    out = _jax.eval_shape(lambda: _forward(_fwd_setup_inputs(0)))
    return out.shape, out.dtype

N_MICROBATCH = 1
ADAM_LR = 0.001
ADAM_B1 = 0.9
ADAM_B2 = 0.999
ADAM_EPS = 1e-08
ADAM_WD = 0.01
ADAM_STEP = 10
PER_EXAMPLE_BATCH_AXIS = {'x': 0, 'loss_target': 0}
SHARED_INPUTS = []
_WEIGHT_DTYPES = {'mix_pre_g': _jnp.float32, 'mix_post_g': _jnp.float32, 'ffn_pre_g': _jnp.float32, 'ffn_post_g': _jnp.float32, 'ab_w_in': _jnp.float32, 'pool_w': _jnp.float32, 'pool_scale': _jnp.float32, 'conv_w': _jnp.float32, 'conv_b': _jnp.float32, 'conv_ln_g': _jnp.float32, 'conv_ln_b': _jnp.float32, 'ab_w_out': _jnp.float32, 'sc_w_in': _jnp.float32, 'sc_conv_w': _jnp.float32, 'sc_w_out': _jnp.float32, 'ffn_w1': _jnp.float32, 'ffn_w2': _jnp.float32}
MOMENT_SCALE = {'mix_pre_g': 3.319250e-01, 'mix_post_g': 7.999951e+00, 'ffn_pre_g': 5.250372e-01, 'ffn_post_g': 8.484307e+00, 'ab_w_in': 2.762175e-01, 'pool_w': 4.272522e-01, 'pool_scale': 4.873585e-01, 'conv_w': 4.130771e-01, 'conv_b': 6.377944e+00, 'conv_ln_g': 2.482224e+00, 'conv_ln_b': 3.633355e+00, 'ab_w_out': 9.796965e-01, 'sc_w_in': 1.825395e-01, 'sc_conv_w': 1.908576e-01, 'sc_w_out': 1.999144e-01, 'ffn_w1': 2.526161e-01, 'ffn_w2': 1.608655e+00}


def _to_microbatches(a, axis):
    t = _jnp.moveaxis(a, axis, 0)
    t = t.reshape((N_MICROBATCH, t.shape[0] // N_MICROBATCH) + t.shape[1:])
    return _jnp.moveaxis(t, 1, axis + 1)


def setup_inputs(seed: int = 0) -> dict:
    inp = _fwd_setup_inputs(seed)
    key = _jax.random.fold_in(_jax.random.key(seed), 7919)
    shape, _ = _output_shape()
    out = dict(inp)
    out["loss_target"] = _jax.random.normal(_jax.random.fold_in(key, 0), shape, _jnp.float32)
    for i, name in enumerate(TWIN_WEIGHTS):
        w = inp[name].astype(_jnp.float32)
        if MOMENT_SCALE is None:
            s = _jnp.sqrt(_jnp.mean(_jnp.square(w)) + 1e-30)
        else:
            s = MOMENT_SCALE[name]
        km, kv = _jax.random.split(_jax.random.fold_in(key, i + 1))
        out[name] = w
        out["m_" + name] = s * _jax.random.normal(km, w.shape, _jnp.float32)
        out["v_" + name] = (s * s) * _jax.random.uniform(kv, w.shape, _jnp.float32, 0.5, 1.5)
    if N_MICROBATCH > 1:
        for name, axis in PER_EXAMPLE_BATCH_AXIS.items():
            out[name] = _to_microbatches(out[name], axis)
    return {'x': out['x'], 'mix_pre_g': out['mix_pre_g'], 'mix_post_g': out['mix_post_g'], 'ffn_pre_g': out['ffn_pre_g'], 'ffn_post_g': out['ffn_post_g'], 'ab_w_in': out['ab_w_in'], 'pool_w': out['pool_w'], 'pool_scale': out['pool_scale'], 'conv_w': out['conv_w'], 'conv_b': out['conv_b'], 'conv_ln_g': out['conv_ln_g'], 'conv_ln_b': out['conv_ln_b'], 'ab_w_out': out['ab_w_out'], 'sc_w_in': out['sc_w_in'], 'sc_conv_w': out['sc_conv_w'], 'sc_w_out': out['sc_w_out'], 'ffn_w1': out['ffn_w1'], 'ffn_w2': out['ffn_w2'], 'loss_target': out['loss_target'], 'm_mix_pre_g': out['m_mix_pre_g'], 'm_mix_post_g': out['m_mix_post_g'], 'm_ffn_pre_g': out['m_ffn_pre_g'], 'm_ffn_post_g': out['m_ffn_post_g'], 'm_ab_w_in': out['m_ab_w_in'], 'm_pool_w': out['m_pool_w'], 'm_pool_scale': out['m_pool_scale'], 'm_conv_w': out['m_conv_w'], 'm_conv_b': out['m_conv_b'], 'm_conv_ln_g': out['m_conv_ln_g'], 'm_conv_ln_b': out['m_conv_ln_b'], 'm_ab_w_out': out['m_ab_w_out'], 'm_sc_w_in': out['m_sc_w_in'], 'm_sc_conv_w': out['m_sc_conv_w'], 'm_sc_w_out': out['m_sc_w_out'], 'm_ffn_w1': out['m_ffn_w1'], 'm_ffn_w2': out['m_ffn_w2'], 'v_mix_pre_g': out['v_mix_pre_g'], 'v_mix_post_g': out['v_mix_post_g'], 'v_ffn_pre_g': out['v_ffn_pre_g'], 'v_ffn_post_g': out['v_ffn_post_g'], 'v_ab_w_in': out['v_ab_w_in'], 'v_pool_w': out['v_pool_w'], 'v_pool_scale': out['v_pool_scale'], 'v_conv_w': out['v_conv_w'], 'v_conv_b': out['v_conv_b'], 'v_conv_ln_g': out['v_conv_ln_g'], 'v_conv_ln_b': out['v_conv_ln_b'], 'v_ab_w_out': out['v_ab_w_out'], 'v_sc_w_in': out['v_sc_w_in'], 'v_sc_conv_w': out['v_sc_conv_w'], 'v_sc_w_out': out['v_sc_w_out'], 'v_ffn_w1': out['v_ffn_w1'], 'v_ffn_w2': out['v_ffn_w2']}


def _loss(weights, diff, rest, loss_target):
    with _jax.named_scope("forward"):
        args = {**rest, TWIN_DIFF_INPUT: diff, **{k: w.astype(_WEIGHT_DTYPES[k]) for k, w in weights.items()}}
        y = _forward(args)
    with _jax.named_scope("loss_head"):
        err = _jnp.square(y.astype(_jnp.float32) - loss_target)
        return 0.5 * _jnp.sum(_jnp.mean(err, axis=-1)) if err.ndim else 0.5 * err


def _adamw(w, g, m, v):
    m = ADAM_B1 * m + (1.0 - ADAM_B1) * g
    v = ADAM_B2 * v + (1.0 - ADAM_B2) * _jnp.square(g)
    m_hat = m / (1.0 - ADAM_B1 ** ADAM_STEP)
    v_hat = v / (1.0 - ADAM_B2 ** ADAM_STEP)
    delta = -ADAM_LR * (m_hat / (_jnp.sqrt(v_hat) + ADAM_EPS) + ADAM_WD * w)
    return delta, m, v


def reference(x, mix_pre_g, mix_post_g, ffn_pre_g, ffn_post_g, ab_w_in, pool_w, pool_scale, conv_w, conv_b, conv_ln_g, conv_ln_b, ab_w_out, sc_w_in, sc_conv_w, sc_w_out, ffn_w1, ffn_w2, loss_target, m_mix_pre_g, m_mix_post_g, m_ffn_pre_g, m_ffn_post_g, m_ab_w_in, m_pool_w, m_pool_scale, m_conv_w, m_conv_b, m_conv_ln_g, m_conv_ln_b, m_ab_w_out, m_sc_w_in, m_sc_conv_w, m_sc_w_out, m_ffn_w1, m_ffn_w2, v_mix_pre_g, v_mix_post_g, v_ffn_pre_g, v_ffn_post_g, v_ab_w_in, v_pool_w, v_pool_scale, v_conv_w, v_conv_b, v_conv_ln_g, v_conv_ln_b, v_ab_w_out, v_sc_w_in, v_sc_conv_w, v_sc_w_out, v_ffn_w1, v_ffn_w2):
    given = dict(x=x, mix_pre_g=mix_pre_g, mix_post_g=mix_post_g, ffn_pre_g=ffn_pre_g, ffn_post_g=ffn_post_g, ab_w_in=ab_w_in, pool_w=pool_w, pool_scale=pool_scale, conv_w=conv_w, conv_b=conv_b, conv_ln_g=conv_ln_g, conv_ln_b=conv_ln_b, ab_w_out=ab_w_out, sc_w_in=sc_w_in, sc_conv_w=sc_conv_w, sc_w_out=sc_w_out, ffn_w1=ffn_w1, ffn_w2=ffn_w2, loss_target=loss_target, m_mix_pre_g=m_mix_pre_g, m_mix_post_g=m_mix_post_g, m_ffn_pre_g=m_ffn_pre_g, m_ffn_post_g=m_ffn_post_g, m_ab_w_in=m_ab_w_in, m_pool_w=m_pool_w, m_pool_scale=m_pool_scale, m_conv_w=m_conv_w, m_conv_b=m_conv_b, m_conv_ln_g=m_conv_ln_g, m_conv_ln_b=m_conv_ln_b, m_ab_w_out=m_ab_w_out, m_sc_w_in=m_sc_w_in, m_sc_conv_w=m_sc_conv_w, m_sc_w_out=m_sc_w_out, m_ffn_w1=m_ffn_w1, m_ffn_w2=m_ffn_w2, v_mix_pre_g=v_mix_pre_g, v_mix_post_g=v_mix_post_g, v_ffn_pre_g=v_ffn_pre_g, v_ffn_post_g=v_ffn_post_g, v_ab_w_in=v_ab_w_in, v_pool_w=v_pool_w, v_pool_scale=v_pool_scale, v_conv_w=v_conv_w, v_conv_b=v_conv_b, v_conv_ln_g=v_conv_ln_g, v_conv_ln_b=v_conv_ln_b, v_ab_w_out=v_ab_w_out, v_sc_w_in=v_sc_w_in, v_sc_conv_w=v_sc_conv_w, v_sc_w_out=v_sc_w_out, v_ffn_w1=v_ffn_w1, v_ffn_w2=v_ffn_w2)
    weights = {n: given[n] for n in TWIN_WEIGHTS}
    shared = {n: given[n] for n in SHARED_INPUTS}
    per_example = {n: given[n] for n in ['x']}
    grad_fn = _jax.value_and_grad(_loss, argnums=(0, 1))

    def one_microbatch(ex, loss_target):
        ex = dict(ex)
        diff = ex.pop(TWIN_DIFF_INPUT)
        return grad_fn(weights, diff, {**shared, **ex}, loss_target)

    if N_MICROBATCH == 1:
        loss, (grad_w, grad_x) = one_microbatch(per_example, given["loss_target"])
    else:
        def body(carry, xs):
            loss_sum, grad_sum = carry
            l_k, (gw_k, gx_k) = one_microbatch(xs[0], xs[1])
            with _jax.named_scope("update"):
                return (loss_sum + l_k, _jax.tree.map(_jnp.add, grad_sum, gw_k)), gx_k

        init = (_jnp.zeros((), _jnp.float32), _jax.tree.map(_jnp.zeros_like, weights))
        (loss, grad_w), grad_x = _jax.lax.scan(body, init, (per_example, given["loss_target"]))
    with _jax.named_scope("update"):
        delta_w, new_m, new_v = {}, {}, {}
        for n in TWIN_WEIGHTS:
            delta_w[n], new_m[n], new_v[n] = _adamw(weights[n], grad_w[n], given["m_" + n], given["v_" + n])
    return (loss, grad_x, *[grad_w[n] for n in TWIN_WEIGHTS], *[delta_w[n] for n in TWIN_WEIGHTS],
            *[new_m[n] for n in TWIN_WEIGHTS], *[new_v[n] for n in TWIN_WEIGHTS])
```

```python
import functools

import jax
import jax.numpy as jnp
from jax import lax
from jax.experimental import pallas as pl
from jax.experimental.pallas import tpu as pltpu

F32, BF16 = jnp.float32, jnp.bfloat16
EPS = 1e-6
N_GROUPS = 4
MAX_WINDOW = 16
CONV_K = 31
SHORT_K = 3
CONV_PAD = 32
SHORT_PAD = 8
ADAM_LR, ADAM_B1, ADAM_B2, ADAM_EPS, ADAM_WD, ADAM_STEP = 0.001, 0.9, 0.999, 1e-08, 0.01, 10
N_CHIPS = 4
VMEM_LIMIT_BYTES = 56 * 1024 * 1024
ROW_TILE = 256
CHUNK = 256
MESH = pl.DeviceIdType.MESH
HBM = pl.BlockSpec(memory_space=pltpu.HBM)


def _cp(*sem):
    return pltpu.CompilerParams(dimension_semantics=sem, vmem_limit_bytes=VMEM_LIMIT_BYTES)


def _sigmoid(v):
    return 1.0 / (1.0 + jnp.exp(-v))


_DIMS = {"nn": (((1,), (0,)), ((), ())), "nt": (((1,), (1,)), ((), ())), "tn": (((0,), (0,)), ((), ()))}


def _pick(n, cap, step=256):
    if n <= cap:
        return n
    return next(t for t in range(cap - cap % step, 0, -step) if n % t == 0)


def _matmul(a, b, mode, name, out_dtypes=(F32,), epilogue=None, epi=(), tm=1024, tn=1024, tk=2048):
    if mode == "tn":
        (K, M), (K2, N) = a.shape, b.shape
    elif mode == "nt":
        (M, K), (N, K2) = a.shape, b.shape
    else:
        (M, K), (K2, N) = a.shape, b.shape
    assert K == K2
    tm, tn, tk = _pick(M, tm), _pick(N, tn), _pick(K, tk)
    nk = K // tk
    a_spec = pl.BlockSpec((tk, tm), lambda i, j, k: (k, i)) if mode == "tn" else pl.BlockSpec((tm, tk), lambda i, j, k: (i, k))
    b_spec = pl.BlockSpec((tn, tk), lambda i, j, k: (j, k)) if mode == "nt" else pl.BlockSpec((tk, tn), lambda i, j, k: (k, j))
    o_spec = pl.BlockSpec((tm, tn), lambda i, j, k: (i, j))
    n_epi, n_out = len(epi), len(out_dtypes)

    def body(a_ref, b_ref, *rest):
        epi_refs, out_refs = rest[:n_epi], rest[n_epi:n_epi + n_out]
        part = lax.dot_general(a_ref[...].astype(BF16), b_ref[...].astype(BF16), _DIMS[mode], preferred_element_type=F32)

        def finish(acc):
            outs = epilogue(acc, *[r[...] for r in epi_refs]) if epilogue else (acc,)
            for o_ref, o in zip(out_refs, outs):
                o_ref[...] = o.astype(o_ref.dtype)

        if nk == 1:
            finish(part)
        else:
            acc_ref = rest[-1]
            k = pl.program_id(2)

            @pl.when(k == 0)
            def _():
                acc_ref[...] = part

            @pl.when(k > 0)
            def _():
                acc_ref[...] += part

            @pl.when(k == nk - 1)
            def _():
                finish(acc_ref[...])

    outs = pl.pallas_call(
        body, name=name, grid=(M // tm, N // tn, nk),
        in_specs=[a_spec, b_spec] + [o_spec] * n_epi,
        out_specs=[o_spec] * n_out,
        out_shape=[jax.ShapeDtypeStruct((M, N), dt) for dt in out_dtypes],
        scratch_shapes=[pltpu.VMEM((tm, tn), F32)] if nk > 1 else [],
        compiler_params=_cp("parallel", "parallel", "arbitrary"),
    )(a, b, *epi)
    return outs[0] if n_out == 1 else outs


def _rms(x, g):
    r = lax.rsqrt(jnp.mean(x * x, axis=-1, keepdims=True) + EPS)
    return x * r * g


def _rms_bwd(dy, x, g):
    r = lax.rsqrt(jnp.mean(x * x, axis=-1, keepdims=True) + EPS)
    xn = x * r
    dyg = dy * g
    dx = r * (dyg - xn * jnp.mean(dyg * xn, axis=-1, keepdims=True))
    return dx, jnp.sum(dy * xn, axis=0, keepdims=True)


def _rows(d, tr=ROW_TILE):
    return pl.BlockSpec((tr, d), lambda i: (i, 0))


def _vec(d):
    return pl.BlockSpec((1, d), lambda i: (0, 0))


def _accumulate(ref, val):
    @pl.when(pl.program_id(0) == 0)
    def _():
        ref[...] = val

    @pl.when(pl.program_id(0) > 0)
    def _():
        ref[...] += val


def _norm_fwd(x, g, name):
    T, D = x.shape

    def body(x_ref, g_ref, h_ref):
        h_ref[...] = _rms(x_ref[...], g_ref[...]).astype(BF16)

    return pl.pallas_call(body, name=name, grid=(T // ROW_TILE,), in_specs=[_rows(D), _vec(D)], out_specs=_rows(D),
                          out_shape=jax.ShapeDtypeStruct((T, D), BF16), compiler_params=_cp("parallel"))(x, g)


def _residual_norm(x, m, g_post, g_next, name):
    T, D = x.shape

    def body(x_ref, m_ref, gp_ref, gn_ref, xo_ref, h_ref):
        xo = x_ref[...] + _rms(m_ref[...], gp_ref[...])
        xo_ref[...] = xo
        h_ref[...] = _rms(xo, gn_ref[...]).astype(BF16)

    return pl.pallas_call(body, name=name, grid=(T // ROW_TILE,), in_specs=[_rows(D), _rows(D), _vec(D), _vec(D)],
                          out_specs=[_rows(D), _rows(D)],
                          out_shape=[jax.ShapeDtypeStruct((T, D), F32), jax.ShapeDtypeStruct((T, D), BF16)],
                          compiler_params=_cp("parallel"))(x, m, g_post, g_next)


def _loss_and_last_norm_bwd(x, m, g_post, target, name):
    T, D = x.shape

    def body(x_ref, m_ref, gp_ref, t_ref, dx_ref, dm_ref, dg_ref, loss_ref):
        m_val, gp = m_ref[...], gp_ref[...]
        err = x_ref[...] + _rms(m_val, gp) - t_ref[...]
        dx = err * (1.0 / D)
        dx_ref[...] = dx
        dm, dg = _rms_bwd(dx, m_val, gp)
        dm_ref[...] = dm.astype(BF16)
        _accumulate(dg_ref, dg)
        _accumulate(loss_ref, jnp.full((1, 128), 0.5 * jnp.sum(err * err) * (1.0 / D), F32))

    return pl.pallas_call(body, name=name, grid=(T // ROW_TILE,), in_specs=[_rows(D), _rows(D), _vec(D), _rows(D)],
                          out_specs=[_rows(D), _rows(D), _vec(D), _vec(128)],
                          out_shape=[jax.ShapeDtypeStruct((T, D), F32), jax.ShapeDtypeStruct((T, D), BF16),
                                     jax.ShapeDtypeStruct((1, D), F32), jax.ShapeDtypeStruct((1, 128), F32)],
                          compiler_params=_cp("arbitrary"))(x, m, g_post, target)


def _norms_bwd(dx, dh, x_in, g_pre, m_prev, g_post_prev, name):
    T, D = dx.shape
    with_prev = m_prev is not None

    def body(*refs):
        if with_prev:
            dx_ref, dh_ref, x_ref, gq_ref, m_ref, gp_ref, dxo_ref, dgq_ref, dm_ref, dgp_ref = refs
        else:
            dx_ref, dh_ref, x_ref, gq_ref, dxo_ref, dgq_ref = refs
        d_in, dgq = _rms_bwd(dh_ref[...], x_ref[...], gq_ref[...])
        dxo = dx_ref[...] + d_in
        dxo_ref[...] = dxo
        _accumulate(dgq_ref, dgq)
        if with_prev:
            dm, dgp = _rms_bwd(dxo, m_ref[...], gp_ref[...])
            dm_ref[...] = dm.astype(BF16)
            _accumulate(dgp_ref, dgp)

    ins, in_specs = [dx, dh, x_in, g_pre], [_rows(D), _rows(D), _rows(D), _vec(D)]
    out_specs = [_rows(D), _vec(D)]
    out_shape = [jax.ShapeDtypeStruct((T, D), F32), jax.ShapeDtypeStruct((1, D), F32)]
    if with_prev:
        ins += [m_prev, g_post_prev]
        in_specs += [_rows(D), _vec(D)]
        out_specs += [_rows(D), _vec(D)]
        out_shape += [jax.ShapeDtypeStruct((T, D), BF16), jax.ShapeDtypeStruct((1, D), F32)]
    return pl.pallas_call(body, name=name, grid=(T // ROW_TILE,), in_specs=in_specs, out_specs=out_specs, out_shape=out_shape,
                          compiler_params=_cp("arbitrary"))(*ins)


def _window_weights(g):
    w = 2 << g
    return w, [jnp.where(j < w, 1.0, 0.0).astype(F32) for j in range(MAX_WINDOW)]


def _valid_count(r0, rows, w):
    t = (lax.broadcasted_iota(jnp.int32, (rows, 1), 0) + (r0 + 1)).astype(F32)
    return jnp.minimum(t, w.astype(F32))


def _pool_fwd(z, pool_w, pool_scale, name):
    T = z.shape[0]
    PG = pool_w.shape[-1]
    DP = N_GROUPS * PG
    rc = min(CHUNK, T)

    def body(u_ref, pw_ref, sc_ref, pooled_ref, y_ref, pad):
        w, wts = _window_weights(pl.program_id(0))
        pad[pl.ds(0, MAX_WINDOW), :] = jnp.zeros((MAX_WINDOW, PG), F32)
        pad[pl.ds(MAX_WINDOW, T), :] = u_ref[...]
        for r0 in range(0, T, rc):
            acc = jnp.zeros((rc, PG), F32)
            for j in range(MAX_WINDOW):
                acc = acc + wts[j] * pad[pl.ds(MAX_WINDOW + r0 - j, rc), :]
            pooled = acc / _valid_count(r0, rc, w) - u_ref[pl.ds(r0, rc), :]
            pooled_ref[pl.ds(r0, rc), :] = pooled.astype(BF16)
        mixed = jnp.dot(pooled_ref[...], pw_ref[...].astype(BF16), preferred_element_type=F32)
        y_ref[...] = (mixed * sc_ref[...]).astype(BF16)

    col = lambda g: (0, g)
    return pl.pallas_call(
        body, name=name, grid=(N_GROUPS,),
        in_specs=[pl.BlockSpec((T, PG), col), pl.BlockSpec((None, PG, PG), lambda g: (g, 0, 0)), pl.BlockSpec((1, PG), col)],
        out_specs=[pl.BlockSpec((T, PG), col), pl.BlockSpec((T, PG), col)],
        out_shape=[jax.ShapeDtypeStruct((T, DP), BF16), jax.ShapeDtypeStruct((T, DP), BF16)],
        scratch_shapes=[pltpu.VMEM((T + MAX_WINDOW, PG), F32)], compiler_params=_cp("parallel"))(z, pool_w, pool_scale)


def _pool_bwd(dy, pooled, pool_w, pool_scale, name):
    T = dy.shape[0]
    PG = pool_w.shape[-1]
    DP = N_GROUPS * PG
    rc = min(CHUNK, T)

    def body(dy_ref, pooled_ref, pw_ref, sc_ref, du_ref, dpw_ref, dsc_ref, pad, dp_ref):
        w, wts = _window_weights(pl.program_id(0))
        pooled_v, pw = pooled_ref[...], pw_ref[...].astype(BF16)
        dy_v = dy_ref[...]
        mixed = jnp.dot(pooled_v, pw, preferred_element_type=F32)
        dsc_ref[...] = jnp.sum(dy_v * mixed, axis=0, keepdims=True)
        dmixed = (dy_v * sc_ref[...]).astype(BF16)
        dpw_ref[...] = lax.dot_general(pooled_v, dmixed, _DIMS["tn"], preferred_element_type=F32)
        dp_ref[...] = lax.dot_general(dmixed, pw, _DIMS["nt"], preferred_element_type=F32)
        pad[pl.ds(T, MAX_WINDOW), :] = jnp.zeros((MAX_WINDOW, PG), F32)
        for r0 in range(0, T, rc):
            pad[pl.ds(r0, rc), :] = dp_ref[pl.ds(r0, rc), :] / _valid_count(r0, rc, w)
        for r0 in range(0, T, rc):
            acc = jnp.zeros((rc, PG), F32)
            for j in range(MAX_WINDOW):
                acc = acc + wts[j] * pad[pl.ds(r0 + j, rc), :]
            du_ref[pl.ds(r0, rc), :] = (acc - dp_ref[pl.ds(r0, rc), :]).astype(BF16)

    col = lambda g: (0, g)
    return pl.pallas_call(
        body, name=name, grid=(N_GROUPS,),
        in_specs=[pl.BlockSpec((T, PG), col), pl.BlockSpec((T, PG), col), pl.BlockSpec((None, PG, PG), lambda g: (g, 0, 0)),
                  pl.BlockSpec((1, PG), col)],
        out_specs=[pl.BlockSpec((T, PG), col), pl.BlockSpec((None, PG, PG), lambda g: (g, 0, 0)), pl.BlockSpec((1, PG), col)],
        out_shape=[jax.ShapeDtypeStruct((T, DP), BF16), jax.ShapeDtypeStruct((N_GROUPS, PG, PG), F32),
                   jax.ShapeDtypeStruct((1, DP), F32)],
        scratch_shapes=[pltpu.VMEM((T + MAX_WINDOW, PG), F32), pltpu.VMEM((T, PG), F32)],
        compiler_params=_cp("parallel"))(dy, pooled, pool_w, pool_scale)


def _conv_fwd(z, conv_w, conv_b, d_pool, name, tc=128):
    T = z.shape[0]
    DC = conv_w.shape[-1]
    rc = min(CHUNK, T)
    v0, g0 = d_pool // tc, (d_pool + DC) // tc

    def body(v_ref, gt_ref, w_ref, b_ref, a_ref, c_ref, pad):
        pad[pl.ds(0, CONV_PAD), :] = jnp.zeros((CONV_PAD, tc), F32)
        for r0 in range(0, T, rc):
            a = v_ref[pl.ds(r0, rc), :] * _sigmoid(gt_ref[pl.ds(r0, rc), :])
            a_ref[pl.ds(r0, rc), :] = a
            pad[pl.ds(CONV_PAD + r0, rc), :] = a
        for r0 in range(0, T, rc):
            acc = jnp.zeros((rc, tc), F32) + b_ref[...]
            for k in range(CONV_K):
                acc = acc + w_ref[pl.ds(k, 1), :] * pad[pl.ds(CONV_PAD - (CONV_K - 1) + k + r0, rc), :]
            c_ref[pl.ds(r0, rc), :] = acc

    col = lambda j: (0, j)
    return pl.pallas_call(
        body, name=name, grid=(DC // tc,),
        in_specs=[pl.BlockSpec((T, tc), lambda j: (0, v0 + j)), pl.BlockSpec((T, tc), lambda j: (0, g0 + j)),
                  pl.BlockSpec((CONV_K, tc), col), pl.BlockSpec((1, tc), col)],
        out_specs=[pl.BlockSpec((T, tc), col), pl.BlockSpec((T, tc), col)],
        out_shape=[jax.ShapeDtypeStruct((T, DC), F32), jax.ShapeDtypeStruct((T, DC), F32)],
        scratch_shapes=[pltpu.VMEM((T + CONV_PAD, tc), F32)], compiler_params=_cp("parallel"))(z, z, conv_w, conv_b)


def _conv_bwd(dc, a, z, conv_w, d_pool, name, tc=128):
    T, DC = dc.shape
    rc = min(CHUNK, T)
    v0, g0 = d_pool // tc, (d_pool + DC) // tc

    def body(dc_ref, a_ref, v_ref, gt_ref, w_ref, dv_ref, dg_ref, dw_ref, db_ref, apad, dpad):
        apad[pl.ds(0, CONV_PAD), :] = jnp.zeros((CONV_PAD, tc), F32)
        apad[pl.ds(CONV_PAD, T), :] = a_ref[...]
        dpad[pl.ds(0, T), :] = dc_ref[...]
        dpad[pl.ds(T, CONV_PAD), :] = jnp.zeros((CONV_PAD, tc), F32)
        db_ref[...] = jnp.sum(dc_ref[...], axis=0, keepdims=True)
        for k in range(CONV_K):
            acc = jnp.zeros((8, tc), F32)
            for r0 in range(0, T, rc):
                prod = dc_ref[pl.ds(r0, rc), :] * apad[pl.ds(CONV_PAD - (CONV_K - 1) + k + r0, rc), :]
                acc = acc + jnp.sum(prod.reshape(rc // 8, 8, tc), axis=0)
            dw_ref[pl.ds(k, 1), :] = jnp.sum(acc, axis=0, keepdims=True)
        for r0 in range(0, T, rc):
            da = jnp.zeros((rc, tc), F32)
            for k in range(CONV_K):
                da = da + w_ref[pl.ds(k, 1), :] * dpad[pl.ds(r0 + (CONV_K - 1) - k, rc), :]
            sig = _sigmoid(gt_ref[pl.ds(r0, rc), :])
            dv_ref[pl.ds(r0, rc), :] = (da * sig).astype(BF16)
            dg_ref[pl.ds(r0, rc), :] = (da * v_ref[pl.ds(r0, rc), :] * sig * (1.0 - sig)).astype(BF16)

    col = lambda j: (0, j)
    return pl.pallas_call(
        body, name=name, grid=(DC // tc,),
        in_specs=[pl.BlockSpec((T, tc), col), pl.BlockSpec((T, tc), col), pl.BlockSpec((T, tc), lambda j: (0, v0 + j)),
                  pl.BlockSpec((T, tc), lambda j: (0, g0 + j)), pl.BlockSpec((CONV_K, tc), col)],
        out_specs=[pl.BlockSpec((T, tc), col), pl.BlockSpec((T, tc), col), pl.BlockSpec((CONV_K, tc), col),
                   pl.BlockSpec((1, tc), col)],
        out_shape=[jax.ShapeDtypeStruct((T, DC), BF16), jax.ShapeDtypeStruct((T, DC), BF16),
                   jax.ShapeDtypeStruct((CONV_K, DC), F32), jax.ShapeDtypeStruct((1, DC), F32)],
        scratch_shapes=[pltpu.VMEM((T + CONV_PAD, tc), F32), pltpu.VMEM((T + CONV_PAD, tc), F32)],
        compiler_params=_cp("parallel"))(dc, a, z, z, conv_w)


def _layer_norm_parts(c, g, b):
    mu = jnp.mean(c, axis=-1, keepdims=True)
    xc = c - mu
    rstd = lax.rsqrt(jnp.mean(xc * xc, axis=-1, keepdims=True) + EPS)
    xhat = xc * rstd
    return xhat, rstd, xhat * g + b


def _ln_silu_fwd(c, g, b, name):
    T, DC = c.shape

    def body(c_ref, g_ref, b_ref, y_ref):
        _, _, ln = _layer_norm_parts(c_ref[...], g_ref[...], b_ref[...])
        y_ref[...] = (ln * _sigmoid(ln)).astype(BF16)

    return pl.pallas_call(body, name=name, grid=(T // ROW_TILE,), in_specs=[_rows(DC), _vec(DC), _vec(DC)], out_specs=_rows(DC),
                          out_shape=jax.ShapeDtypeStruct((T, DC), BF16), compiler_params=_cp("parallel"))(c, g, b)


def _ln_silu_bwd(dy, c, g, b, name):
    T, DC = c.shape

    def body(dy_ref, c_ref, g_ref, b_ref, dc_ref, dg_ref, db_ref):
        gain = g_ref[...]
        xhat, rstd, ln = _layer_norm_parts(c_ref[...], gain, b_ref[...])
        s = _sigmoid(ln)
        dln = dy_ref[...] * (s * (1.0 + ln * (1.0 - s)))
        _accumulate(dg_ref, jnp.sum(dln * xhat, axis=0, keepdims=True))
        _accumulate(db_ref, jnp.sum(dln, axis=0, keepdims=True))
        dxh = dln * gain
        dc_ref[...] = rstd * (dxh - jnp.mean(dxh, axis=-1, keepdims=True) - xhat * jnp.mean(dxh * xhat, axis=-1, keepdims=True))

    return pl.pallas_call(body, name=name, grid=(T // ROW_TILE,),
                          in_specs=[pl.BlockSpec((ROW_TILE, DC), lambda i: (i, 1)), _rows(DC), _vec(DC), _vec(DC)],
                          out_specs=[_rows(DC), _vec(DC), _vec(DC)],
                          out_shape=[jax.ShapeDtypeStruct((T, DC), F32), jax.ShapeDtypeStruct((1, DC), F32),
                                     jax.ShapeDtypeStruct((1, DC), F32)],
                          compiler_params=_cp("arbitrary"))(dy, c, g, b)


def _short_specs(T, DS, tc):
    n = DS // tc
    return [pl.BlockSpec((T, tc), lambda j: (0, j)), pl.BlockSpec((T, tc), lambda j: (0, n + j)),
            pl.BlockSpec((T, tc), lambda j: (0, 2 * n + j))]


def _short_fwd(z, w, name, tc=256):
    T = z.shape[0]
    DS = w.shape[-1]
    rc = min(CHUNK, T)

    def body(b_ref, cg_ref, u_ref, w_ref, y_ref, pad):
        pad[pl.ds(0, SHORT_PAD), :] = jnp.zeros((SHORT_PAD, tc), F32)
        pad[pl.ds(SHORT_PAD, T), :] = cg_ref[...] * u_ref[...]
        for r0 in range(0, T, rc):
            r = jnp.zeros((rc, tc), F32)
            for k in range(SHORT_K):
                r = r + w_ref[pl.ds(k, 1), :] * pad[pl.ds(SHORT_PAD - (SHORT_K - 1) + k + r0, rc), :]
            y_ref[pl.ds(r0, rc), :] = (b_ref[pl.ds(r0, rc), :] * r).astype(BF16)

    col = lambda j: (0, j)
    return pl.pallas_call(body, name=name, grid=(DS // tc,), in_specs=_short_specs(T, DS, tc) + [pl.BlockSpec((SHORT_K, tc), col)],
                          out_specs=pl.BlockSpec((T, tc), col), out_shape=jax.ShapeDtypeStruct((T, DS), BF16),
                          scratch_shapes=[pltpu.VMEM((T + SHORT_PAD, tc), F32)], compiler_params=_cp("parallel"))(z, z, z, w)


def _short_bwd(dy, z, w, name, tc=256):
    T, DS = dy.shape
    rc = min(CHUNK, T)

    def body(dy_ref, b_ref, cg_ref, u_ref, w_ref, db_ref, dcg_ref, du_ref, dw_ref, qpad, rpad):
        qpad[pl.ds(0, SHORT_PAD), :] = jnp.zeros((SHORT_PAD, tc), F32)
        qpad[pl.ds(SHORT_PAD, T), :] = cg_ref[...] * u_ref[...]
        rpad[pl.ds(0, T), :] = dy_ref[...] * b_ref[...]
        rpad[pl.ds(T, SHORT_PAD), :] = jnp.zeros((SHORT_PAD, tc), F32)
        accs = [jnp.zeros((8, tc), F32) for _ in range(SHORT_K)]
        for r0 in range(0, T, rc):
            r = jnp.zeros((rc, tc), F32)
            dq = jnp.zeros((rc, tc), F32)
            dr = rpad[pl.ds(r0, rc), :]
            for k in range(SHORT_K):
                q_k = qpad[pl.ds(SHORT_PAD - (SHORT_K - 1) + k + r0, rc), :]
                r = r + w_ref[pl.ds(k, 1), :] * q_k
                dq = dq + w_ref[pl.ds(k, 1), :] * rpad[pl.ds(r0 + (SHORT_K - 1) - k, rc), :]
                accs[k] = accs[k] + jnp.sum((dr * q_k).reshape(rc // 8, 8, tc), axis=0)
            db_ref[pl.ds(r0, rc), :] = (dy_ref[pl.ds(r0, rc), :] * r).astype(BF16)
            dcg_ref[pl.ds(r0, rc), :] = (dq * u_ref[pl.ds(r0, rc), :]).astype(BF16)
            du_ref[pl.ds(r0, rc), :] = (dq * cg_ref[pl.ds(r0, rc), :]).astype(BF16)
        for k in range(SHORT_K):
            dw_ref[pl.ds(k, 1), :] = jnp.sum(accs[k], axis=0, keepdims=True)

    col = lambda j: (0, j)
    tile = pl.BlockSpec((T, tc), col)
    return pl.pallas_call(body, name=name, grid=(DS // tc,),
                          in_specs=[tile] + _short_specs(T, DS, tc) + [pl.BlockSpec((SHORT_K, tc), col)],
                          out_specs=[tile, tile, tile, pl.BlockSpec((SHORT_K, tc), col)],
                          out_shape=[jax.ShapeDtypeStruct((T, DS), BF16)] * 3 + [jax.ShapeDtypeStruct((SHORT_K, DS), F32)],
                          scratch_shapes=[pltpu.VMEM((T + SHORT_PAD, tc), F32), pltpu.VMEM((T + SHORT_PAD, tc), F32)],
                          compiler_params=_cp("parallel"))(dy, z, z, z, w)


def _tile_rows(rows, cols, n_bufs):
    budget = VMEM_LIMIT_BYTES // 2 // (2 * n_bufs * 4 * cols)
    tr = rows
    while tr > budget and tr % 16 == 0:
        tr //= 2
    return tr


def _cast_layer(w, layer, name):
    _, R, C = w.shape
    tr = _tile_rows(R, C, 2)

    def body(w_ref, o_ref):
        o_ref[...] = w_ref[...].astype(BF16)

    return pl.pallas_call(body, name=name, grid=(R // tr,), in_specs=[pl.BlockSpec((None, tr, C), lambda i: (layer, i, 0))],
                          out_specs=pl.BlockSpec((tr, C), lambda i: (i, 0)), out_shape=jax.ShapeDtypeStruct((R, C), BF16),
                          compiler_params=_cp("parallel"))(w)


def _add_pair(a, b, name):
    R, C = a.shape
    tr = _tile_rows(R, C, 3)

    def body(a_ref, b_ref, o_ref):
        o_ref[...] = (a_ref[...].astype(F32) + b_ref[...].astype(F32)).astype(BF16)

    spec = pl.BlockSpec((tr, C), lambda i: (i, 0))
    return pl.pallas_call(body, name=name, grid=(R // tr,), in_specs=[spec, spec], out_specs=spec,
                          out_shape=jax.ShapeDtypeStruct((R, C), BF16), compiler_params=_cp("parallel"))(a, b)


def _sum_chips(parts, name):
    n, R, C = parts.shape
    tr = _tile_rows(R, C, n + 1)

    def body(p_ref, o_ref):
        acc = p_ref[0].astype(F32)
        for i in range(1, n):
            acc = acc + p_ref[i].astype(F32)
        o_ref[...] = acc

    return pl.pallas_call(body, name=name, grid=(R // tr,), in_specs=[pl.BlockSpec((n, tr, C), lambda i: (0, i, 0))],
                          out_specs=pl.BlockSpec((tr, C), lambda i: (i, 0)), out_shape=jax.ShapeDtypeStruct((R, C), F32),
                          compiler_params=_cp("parallel"))(parts)


def _adamw_values(w, g, m, v):
    m = ADAM_B1 * m + (1.0 - ADAM_B1) * g
    v = ADAM_B2 * v + (1.0 - ADAM_B2) * (g * g)
    m_hat = m / (1.0 - ADAM_B1 ** ADAM_STEP)
    v_hat = v / (1.0 - ADAM_B2 ** ADAM_STEP)
    return -ADAM_LR * (m_hat / (jnp.sqrt(v_hat) + ADAM_EPS) + ADAM_WD * w), m, v


def _adamw(w, g, m, v, name, layer=0, carried=None):
    L, R, C = w.shape
    tr = _tile_rows(R, C, 8)

    def body(w_ref, g_ref, m_ref, v_ref, *rest):
        go_ref, d_ref, mo_ref, vo_ref = rest[-4:]
        g_val = g_ref[...]
        d, m_new, v_new = _adamw_values(w_ref[...], g_val, m_ref[...], v_ref[...])
        go_ref[...], d_ref[...], mo_ref[...], vo_ref[...] = g_val, d, m_new, v_new

    lay = pl.BlockSpec((None, tr, C), lambda i: (layer, i, 0))
    ins = [w, g, m, v]
    in_specs = [lay, pl.BlockSpec((tr, C), lambda i: (i, 0)), lay, lay]
    aliases = {}
    if carried is not None:
        ins += list(carried)
        in_specs += [pl.BlockSpec(memory_space=pl.ANY)] * 4
        aliases = {4 + i: i for i in range(4)}
    return pl.pallas_call(body, name=name, grid=(R // tr,), in_specs=in_specs, out_specs=[lay] * 4,
                          out_shape=[jax.ShapeDtypeStruct((L, R, C), F32)] * 4, input_output_aliases=aliases,
                          compiler_params=_cp("parallel"))(*ins)


def _aligned(v, m):
    return v if isinstance(v, int) else pl.multiple_of(v, m)


def _place():
    x, y, c = lax.axis_index("x"), lax.axis_index("y"), lax.axis_index("c")
    other_chips = [(x, 1 - y), (1 - x, y), (1 - x, 1 - y)]
    return x, y, c, 2 * x + y, other_chips


def _chip_index(chip):
    return 2 * chip[0] + chip[1]


def _shard(ref, kind, k):
    R, C = ref.shape
    if kind == "col":
        return ref.at[:, pl.ds(_aligned(k * (C // N_CHIPS), 128), C // N_CHIPS)]
    return ref.at[pl.ds(_aligned(k * (R // N_CHIPS), 16), R // N_CHIPS), :]


def _piece(ref, kind, k, h):
    R, C = ref.shape
    if kind == "col":
        return ref.at[pl.ds(_aligned(h * (R // 2), 16), R // 2), pl.ds(_aligned(k * (C // N_CHIPS), 128), C // N_CHIPS)]
    rs = R // N_CHIPS
    return ref.at[pl.ds(_aligned(k * rs + h * (rs // 2), 16), rs // 2), :]


def _compact_piece(ref, kind, k):
    R2, C = ref.shape
    if kind == "col":
        return ref.at[:, pl.ds(_aligned(k * (C // N_CHIPS), 128), C // N_CHIPS)]
    return ref.at[pl.ds(_aligned(k * (R2 // N_CHIPS), 16), R2 // N_CHIPS), :]


def _half_rows(ref, h):
    R = ref.shape[0]
    return ref.at[pl.ds(_aligned(h * (R // 2), 16), R // 2), :]


class _Copies:
    def __init__(self, send_sems, recv_sems, local_sems):
        self.send_sems, self.recv_sems, self.local_sems = send_sems, recv_sems, local_sems
        self.n_remote = self.n_local = 0
        self.started = []

    def remote(self, src, dst, device):
        k = self.n_remote
        self.n_remote += 1
        return pltpu.make_async_remote_copy(src_ref=src, dst_ref=dst, send_sem=self.send_sems.at[k], recv_sem=self.recv_sems.at[k],
                                            device_id=device, device_id_type=MESH)

    def local(self, src, dst):
        k = self.n_local
        self.n_local += 1
        return pltpu.make_async_copy(src, dst, self.local_sems.at[k])


def _comm_call(body, name, ins, out_shape, n_remote, n_local):
    return pl.pallas_call(
        body, name=name, in_specs=[HBM] * len(ins), out_specs=[HBM] * len(out_shape), out_shape=out_shape,
        scratch_shapes=[pltpu.SemaphoreType.DMA((n_remote,)), pltpu.SemaphoreType.DMA((n_remote,)),
                        pltpu.SemaphoreType.DMA((max(n_local, 1),))])(*ins)


def _gather_weights(shards, kinds, smalls, small_axes, name):
    nb, ns = len(shards), len(smalls)

    def full_shape(s, kind):
        return (s.shape[0], s.shape[1] * N_CHIPS) if kind == "col" else (s.shape[0] * N_CHIPS, s.shape[1])

    def small_full_shape(s, ax):
        return tuple(d * N_CHIPS if i == ax else d for i, d in enumerate(s.shape))

    out_shape = [jax.ShapeDtypeStruct(full_shape(s, k), s.dtype) for s, k in zip(shards, kinds)]
    out_shape += [jax.ShapeDtypeStruct(small_full_shape(s, ax), s.dtype) for s, ax in zip(smalls, small_axes)]

    def small_slot(ref, ax, k):
        n = ref.shape[ax] // N_CHIPS
        idx = [slice(None)] * len(ref.shape)
        idx[ax] = pl.ds(_aligned(k * n, 8), n)
        return ref.at[tuple(idx)]

    def body(*refs):
        ins, outs = refs[:nb + ns], refs[nb + ns:2 * (nb + ns)]
        cp = _Copies(*refs[2 * (nb + ns):])
        x, y, c, me, chips = _place()
        sibling = (x, y, 1 - c)
        sends, locals_ = [], []
        for u in range(nb):
            locals_.append(cp.local(ins[u], _shard(outs[u], kinds[u], me)))
            for chip in chips:
                sends.append(cp.remote(_half_rows(ins[u], c), _piece(outs[u], kinds[u], me, c), (*chip, c)))
        for s in range(ns):
            locals_.append(cp.local(ins[nb + s], small_slot(outs[nb + s], small_axes[s], me)))
            for chip in chips:
                sends.append(cp.remote(ins[nb + s], small_slot(outs[nb + s], small_axes[s], me), (*chip, c)))
        for d in locals_ + sends:
            d.start()
        forwards = []
        for u in range(nb):
            for r, chip in enumerate(chips):
                piece = _piece(outs[u], kinds[u], _chip_index(chip), c)
                sends[u * 3 + r].wait_recv()
                fwd = cp.remote(piece, piece, sibling)
                fwd.start()
                forwards.append(fwd)
        for s in range(ns):
            for r in range(3):
                sends[nb * 3 + s * 3 + r].wait_recv()
        for fwd in forwards:
            fwd.wait_recv()
        for d in sends + forwards:
            d.wait_send()
        for d in locals_:
            d.wait()

    return _comm_call(body, name, list(shards) + list(smalls), out_shape, 6 * nb + 3 * ns, nb + ns)


def _exchange_halves(grads, kinds, name):
    n = len(grads)
    out_shape = [jax.ShapeDtypeStruct((g.shape[0] // 2, g.shape[1]), g.dtype) for g in grads] * 2

    def body(*refs):
        ins, mine, theirs = refs[:n], refs[n:2 * n], refs[2 * n:3 * n]
        cp = _Copies(*refs[3 * n:])
        x, y, c, me, chips = _place()
        sibling = (x, y, 1 - c)
        sends, locals_ = [], []
        for u in range(n):
            if kinds[u] == "col":
                locals_.append(cp.local(_half_rows(ins[u], c), mine[u]))
                sends.append(cp.remote(_half_rows(ins[u], 1 - c), theirs[u], sibling))
            else:
                for k in range(N_CHIPS):
                    locals_.append(cp.local(_piece(ins[u], "row", k, c), _compact_piece(mine[u], "row", k)))
                    sends.append(cp.remote(_piece(ins[u], "row", k, 1 - c), _compact_piece(theirs[u], "row", k), sibling))
        for d in locals_ + sends:
            d.start()
        for d in sends:
            d.wait_recv()
        for d in sends:
            d.wait_send()
        for d in locals_:
            d.wait()

    n_copies = sum(1 if k == "col" else N_CHIPS for k in kinds)
    outs = _comm_call(body, name, list(grads), out_shape, n_copies, n_copies)
    return outs[:n], outs[n:]


def _scatter_to_owners(halves, kinds, name):
    n = len(halves)

    def part_shape(h, kind):
        return (h.shape[0], h.shape[1] // N_CHIPS) if kind == "col" else (h.shape[0] // N_CHIPS, h.shape[1])

    out_shape = [jax.ShapeDtypeStruct((N_CHIPS,) + part_shape(h, k), h.dtype) for h, k in zip(halves, kinds)]

    def body(*refs):
        ins, outs = refs[:n], refs[n:2 * n]
        cp = _Copies(*refs[2 * n:])
        x, y, c, me, chips = _place()
        sends, locals_ = [], []
        for u in range(n):
            locals_.append(cp.local(_compact_piece(ins[u], kinds[u], me), outs[u].at[me]))
            for chip in chips:
                sends.append(cp.remote(_compact_piece(ins[u], kinds[u], _chip_index(chip)), outs[u].at[me], (*chip, c)))
        for d in locals_ + sends:
            d.start()
        for d in sends:
            d.wait_recv()
        for d in sends:
            d.wait_send()
        for d in locals_:
            d.wait()

    return _comm_call(body, name, list(halves), out_shape, 3 * n, n)


def _share_with_sibling(reduced, name):
    n = len(reduced)
    out_shape = [jax.ShapeDtypeStruct((2 * q.shape[0], q.shape[1]), q.dtype) for q in reduced]

    def body(*refs):
        ins, outs = refs[:n], refs[n:2 * n]
        cp = _Copies(*refs[2 * n:])
        x, y, c, me, chips = _place()
        sends = [cp.remote(ins[u], _half_rows(outs[u], c), (x, y, 1 - c)) for u in range(n)]
        locals_ = [cp.local(ins[u], _half_rows(outs[u], c)) for u in range(n)]
        for d in locals_ + sends:
            d.start()
        for d in sends:
            d.wait_recv()
        for d in sends:
            d.wait_send()
        for d in locals_:
            d.wait()

    return _comm_call(body, name, list(reduced), out_shape, n, n)


def _all_reduce_small(packed, name):
    R, C = packed.shape
    n_dev = 2 * N_CHIPS

    def body(in_ref, out_ref, slots, send_sems, recv_sems):
        x, y, c, me, chips = _place()
        my_slot = 2 * me + c
        copies = []
        for p in range(1, n_dev):
            peer = (x ^ (p >> 2), y ^ ((p >> 1) & 1), c ^ (p & 1))
            copies.append(pltpu.make_async_remote_copy(src_ref=in_ref, dst_ref=slots.at[my_slot], send_sem=send_sems.at[p - 1],
                                                       recv_sem=recv_sems.at[p - 1], device_id=peer, device_id_type=MESH))
        for d in copies:
            d.start()
        slots[my_slot] = in_ref[...]
        for d in copies:
            d.wait_recv()
        acc = slots[0]
        for i in range(1, n_dev):
            acc = acc + slots[i]
        out_ref[...] = acc
        for d in copies:
            d.wait_send()

    return pl.pallas_call(
        body, name=name, in_specs=[pl.BlockSpec(memory_space=pltpu.VMEM)], out_specs=pl.BlockSpec(memory_space=pltpu.VMEM),
        out_shape=jax.ShapeDtypeStruct((R, C), F32),
        scratch_shapes=[pltpu.VMEM((n_dev, R, C), F32), pltpu.SemaphoreType.DMA((n_dev - 1,)), pltpu.SemaphoreType.DMA((n_dev - 1,))],
        compiler_params=pltpu.CompilerParams(vmem_limit_bytes=VMEM_LIMIT_BYTES))(packed)


def _pack(arrays, width):
    rows = []
    for a in arrays:
        flat = a.reshape(-1)
        n_rows = -(-flat.shape[0] // width)
        rows.append(jnp.pad(flat, (0, n_rows * width - flat.shape[0])).reshape(n_rows, width))
    n = sum(r.shape[0] for r in rows)
    rows.append(jnp.zeros((-n % 8, width), F32))
    return jnp.concatenate(rows, axis=0)


def _unpack(packed, shapes):
    out, r0, width = [], 0, packed.shape[1]
    for shape in shapes:
        size = 1
        for d in shape:
            size *= d
        n_rows = -(-size // width)
        out.append(packed[r0:r0 + n_rows].reshape(-1)[:size].reshape(shape))
        r0 += n_rows
    return out


def kernel(x, mix_pre_g, mix_post_g, ffn_pre_g, ffn_post_g, ab_w_in, pool_w, pool_scale, conv_w, conv_b, conv_ln_g, conv_ln_b, ab_w_out, sc_w_in, sc_conv_w, sc_w_out, ffn_w1, ffn_w2, loss_target, m_mix_pre_g, m_mix_post_g, m_ffn_pre_g, m_ffn_post_g, m_ab_w_in, m_pool_w, m_pool_scale, m_conv_w, m_conv_b, m_conv_ln_g, m_conv_ln_b, m_ab_w_out, m_sc_w_in, m_sc_conv_w, m_sc_w_out, m_ffn_w1, m_ffn_w2, v_mix_pre_g, v_mix_post_g, v_ffn_pre_g, v_ffn_post_g, v_ab_w_in, v_pool_w, v_pool_scale, v_conv_w, v_conv_b, v_conv_ln_g, v_conv_ln_b, v_ab_w_out, v_sc_w_in, v_sc_conv_w, v_sc_w_out, v_ffn_w1, v_ffn_w2):
    x0, target = x[0], loss_target[0]
    T, D = x0.shape
    DP = pool_scale.shape[-1]
    gain = lambda g, layer: g[layer][None, :]

    big = [("ab_w_in", ab_w_in, 0, "col"), ("ab_w_out", ab_w_out, 0, "row"), ("ffn_w1_0", ffn_w1, 0, "col"),
           ("ffn_w2_0", ffn_w2, 0, "row"), ("sc_w_in", sc_w_in, 0, "col"), ("sc_w_out", sc_w_out, 0, "row"),
           ("ffn_w1_1", ffn_w1, 1, "col"), ("ffn_w2_1", ffn_w2, 1, "row")]
    kinds = [b[3] for b in big]
    shards = [_cast_layer(w, layer, "cast_" + name) for name, w, layer, _ in big]
    fulls = _gather_weights(shards, kinds, [pool_w[0], conv_w[0], sc_conv_w[0]], [1, 1, 1], "gather_weights")
    w_in0, w_out0, w1_0, w2_0, w_in1, w_out1, w1_1, w2_1, pool_w_full, conv_w_full, sc_conv_w_full = fulls

    relu_sq = lambda acc: (jnp.maximum(acc, 0.0), jnp.square(jnp.maximum(acc, 0.0)))
    relu_sq_bwd = lambda acc, a: (acc * (2.0 * a.astype(F32)),)

    h0 = _norm_fwd(x0, gain(mix_pre_g, 0), "norm_in")
    z0 = _matmul(h0, w_in0, "nn", "mix0_in")
    pooled, y_pool = _pool_fwd(z0, pool_w_full, pool_scale, "pool_fwd")
    a_conv, c_conv = _conv_fwd(z0, conv_w_full, conv_b, DP, "conv_fwd")
    y_conv = _ln_silu_fwd(c_conv, conv_ln_g, conv_ln_b, "ln_silu_fwd")
    y0 = jnp.concatenate([y_pool, y_conv], axis=1)
    m0 = _matmul(y0, w_out0, "nn", "mix0_out")
    x1, h1 = _residual_norm(x0, m0, gain(mix_post_g, 0), gain(ffn_pre_g, 0), "res_mix0")
    a0, a0sq = _matmul(h1, w1_0, "nn", "ffn0_up", out_dtypes=(BF16, BF16), epilogue=relu_sq)
    f0 = _matmul(a0sq, w2_0, "nn", "ffn0_down")
    x2, h2 = _residual_norm(x1, f0, gain(ffn_post_g, 0), gain(mix_pre_g, 1), "res_ffn0")
    z1 = _matmul(h2, w_in1, "nn", "mix1_in")
    y1 = _short_fwd(z1, sc_conv_w_full, "short_fwd")
    m1 = _matmul(y1, w_out1, "nn", "mix1_out")
    x3, h3 = _residual_norm(x2, m1, gain(mix_post_g, 1), gain(ffn_pre_g, 1), "res_mix1")
    a1, a1sq = _matmul(h3, w1_1, "nn", "ffn1_up", out_dtypes=(BF16, BF16), epilogue=relu_sq)
    f1 = _matmul(a1sq, w2_1, "nn", "ffn1_down")

    dx, df1, d_ffn_post_1, loss_row = _loss_and_last_norm_bwd(x3, f1, gain(ffn_post_g, 1), target, "loss")
    g_w2_1 = _matmul(a1sq, df1, "tn", "ffn1_down_dw", out_dtypes=(BF16,))
    dz = _matmul(df1, w2_1, "nt", "ffn1_down_dx", out_dtypes=(BF16,), epilogue=relu_sq_bwd, epi=(a1,))
    g_w1_1 = _matmul(h3, dz, "tn", "ffn1_up_dw", out_dtypes=(BF16,))
    dh = _matmul(dz, w1_1, "nt", "ffn1_up_dx")
    dx, d_ffn_pre_1, dm1, d_mix_post_1 = _norms_bwd(dx, dh, x3, gain(ffn_pre_g, 1), m1, gain(mix_post_g, 1), "norms_bwd3")

    g_w_out1 = _matmul(y1, dm1, "tn", "mix1_out_dw", out_dtypes=(BF16,))
    dy1 = _matmul(dm1, w_out1, "nt", "mix1_out_dx")
    db, dcg, du, d_sc_conv_w = _short_bwd(dy1, z1, sc_conv_w_full, "short_bwd")
    dz1 = jnp.concatenate([db, dcg, du], axis=1)
    g_w_in1 = _matmul(h2, dz1, "tn", "mix1_in_dw", out_dtypes=(BF16,))
    dh = _matmul(dz1, w_in1, "nt", "mix1_in_dx")
    dx, d_mix_pre_1, df0, d_ffn_post_0 = _norms_bwd(dx, dh, x2, gain(mix_pre_g, 1), f0, gain(ffn_post_g, 0), "norms_bwd2")

    g_w2_0 = _matmul(a0sq, df0, "tn", "ffn0_down_dw", out_dtypes=(BF16,))
    dz = _matmul(df0, w2_0, "nt", "ffn0_down_dx", out_dtypes=(BF16,), epilogue=relu_sq_bwd, epi=(a0,))
    g_w1_0 = _matmul(h1, dz, "tn", "ffn0_up_dw", out_dtypes=(BF16,))
    dh = _matmul(dz, w1_0, "nt", "ffn0_up_dx")
    dx, d_ffn_pre_0, dm0, d_mix_post_0 = _norms_bwd(dx, dh, x1, gain(ffn_pre_g, 0), m0, gain(mix_post_g, 0), "norms_bwd1")

    g_w_out0 = _matmul(y0, dm0, "tn", "mix0_out_dw", out_dtypes=(BF16,))
    dy0 = _matmul(dm0, w_out0, "nt", "mix0_out_dx")
    du_pool, d_pool_w, d_pool_scale = _pool_bwd(dy0, pooled, pool_w_full, pool_scale, "pool_bwd")
    dc, d_ln_g, d_ln_b = _ln_silu_bwd(dy0, c_conv, conv_ln_g, conv_ln_b, "ln_silu_bwd")
    dv, dgate, d_conv_w, d_conv_b = _conv_bwd(dc, a_conv, z0, conv_w_full, DP, "conv_bwd")
    dz0 = jnp.concatenate([du_pool, dv, dgate], axis=1)
    g_w_in0 = _matmul(h0, dz0, "tn", "mix0_in_dw", out_dtypes=(BF16,))
    dh = _matmul(dz0, w_in0, "nt", "mix0_in_dx")
    grad_x, d_mix_pre_0 = _norms_bwd(dx, dh, x0, gain(mix_pre_g, 0), None, None, "norms_bwd0")

    loss = lax.psum(loss_row[0, 0], ("x", "y", "c"))

    grads = [g_w_in0, g_w_out0, g_w1_0, g_w2_0, g_w_in1, g_w_out1, g_w1_1, g_w2_1]
    mine, theirs = _exchange_halves(grads, kinds, "grads_to_sibling")
    chip_sums = [_add_pair(a, b, "chip_sum_" + big[u][0]) for u, (a, b) in enumerate(zip(mine, theirs))]
    slots = _scatter_to_owners(chip_sums, kinds, "grads_to_owners")
    reduced = [_sum_chips(s, "reduce_" + big[u][0]) for u, s in enumerate(slots)]
    g_big = _share_with_sibling(reduced, "grads_share")

    small_grads = [jnp.concatenate([d_mix_pre_0, d_mix_pre_1], 0), jnp.concatenate([d_mix_post_0, d_mix_post_1], 0),
                   jnp.concatenate([d_ffn_pre_0, d_ffn_pre_1], 0), jnp.concatenate([d_ffn_post_0, d_ffn_post_1], 0),
                   d_pool_scale, d_conv_b, d_ln_g, d_ln_b, d_pool_w, d_conv_w, d_sc_conv_w]
    small_shapes = [a.shape for a in small_grads]
    summed = _unpack(_all_reduce_small(_pack(small_grads, D), "small_grads_sum"), small_shapes)
    (g_mix_pre, g_mix_post, g_ffn_pre, g_ffn_post, g_pool_scale, g_conv_b, g_ln_g, g_ln_b, g_pool_w_full, g_conv_w_full,
     g_sc_conv_w_full) = summed
    chip = 2 * lax.axis_index("x") + lax.axis_index("y")
    own = lambda a, ax: lax.dynamic_slice_in_dim(a, chip * (a.shape[ax] // N_CHIPS), a.shape[ax] // N_CHIPS, axis=ax)
    g_pool_w, g_conv_w, g_sc_conv_w = own(g_pool_w_full, 1), own(g_conv_w_full, 1), own(g_sc_conv_w_full, 1)

    def small_update(w, g, m, v, name):
        shape = w.shape
        as3 = lambda a: a.reshape((1, -1, shape[-1]))
        outs = _adamw(as3(w), g.reshape((-1, shape[-1])), as3(m), as3(v), "adamw_" + name)
        return [o.reshape(shape) for o in outs]

    upd = {}
    upd["mix_pre_g"] = small_update(mix_pre_g, g_mix_pre, m_mix_pre_g, v_mix_pre_g, "mix_pre_g")
    upd["mix_post_g"] = small_update(mix_post_g, g_mix_post, m_mix_post_g, v_mix_post_g, "mix_post_g")
    upd["ffn_pre_g"] = small_update(ffn_pre_g, g_ffn_pre, m_ffn_pre_g, v_ffn_pre_g, "ffn_pre_g")
    upd["ffn_post_g"] = small_update(ffn_post_g, g_ffn_post, m_ffn_post_g, v_ffn_post_g, "ffn_post_g")
    upd["pool_w"] = small_update(pool_w, g_pool_w, m_pool_w, v_pool_w, "pool_w")
    upd["pool_scale"] = small_update(pool_scale, g_pool_scale, m_pool_scale, v_pool_scale, "pool_scale")
    upd["conv_w"] = small_update(conv_w, g_conv_w, m_conv_w, v_conv_w, "conv_w")
    upd["conv_b"] = small_update(conv_b, g_conv_b, m_conv_b, v_conv_b, "conv_b")
    upd["conv_ln_g"] = small_update(conv_ln_g, g_ln_g, m_conv_ln_g, v_conv_ln_g, "conv_ln_g")
    upd["conv_ln_b"] = small_update(conv_ln_b, g_ln_b, m_conv_ln_b, v_conv_ln_b, "conv_ln_b")
    upd["sc_conv_w"] = small_update(sc_conv_w, g_sc_conv_w, m_sc_conv_w, v_sc_conv_w, "sc_conv_w")
    upd["ab_w_in"] = _adamw(ab_w_in, g_big[0], m_ab_w_in, v_ab_w_in, "adamw_ab_w_in")
    upd["ab_w_out"] = _adamw(ab_w_out, g_big[1], m_ab_w_out, v_ab_w_out, "adamw_ab_w_out")
    upd["sc_w_in"] = _adamw(sc_w_in, g_big[4], m_sc_w_in, v_sc_w_in, "adamw_sc_w_in")
    upd["sc_w_out"] = _adamw(sc_w_out, g_big[5], m_sc_w_out, v_sc_w_out, "adamw_sc_w_out")
    first = _adamw(ffn_w1, g_big[6], m_ffn_w1, v_ffn_w1, "adamw_ffn_w1_1", layer=1)
    upd["ffn_w1"] = _adamw(ffn_w1, g_big[2], m_ffn_w1, v_ffn_w1, "adamw_ffn_w1_0", layer=0, carried=first)
    first = _adamw(ffn_w2, g_big[7], m_ffn_w2, v_ffn_w2, "adamw_ffn_w2_1", layer=1)
    upd["ffn_w2"] = _adamw(ffn_w2, g_big[3], m_ffn_w2, v_ffn_w2, "adamw_ffn_w2_0", layer=0, carried=first)

    order = ["mix_pre_g", "mix_post_g", "ffn_pre_g", "ffn_post_g", "ab_w_in", "pool_w", "pool_scale", "conv_w", "conv_b",
             "conv_ln_g", "conv_ln_b", "ab_w_out", "sc_w_in", "sc_conv_w", "sc_w_out", "ffn_w1", "ffn_w2"]
    out = [loss, grad_x[None]]
    for part in range(4):
        out += [upd[n][part] for n in order]
    return tuple(out)
```

```python
import functools

import jax
import jax.numpy as jnp
from jax import lax
from jax.experimental import pallas as pl
from jax.experimental.pallas import tpu as pltpu

F32, BF16 = jnp.float32, jnp.bfloat16
EPS = 1e-6
N_GROUPS = 4
MAX_WINDOW = 16
CONV_K = 31
SHORT_K = 3
CONV_PAD = 32
SHORT_PAD = 8
ADAM_LR, ADAM_B1, ADAM_B2, ADAM_EPS, ADAM_WD, ADAM_STEP = 0.001, 0.9, 0.999, 1e-08, 0.01, 10
N_CHIPS = 4
VMEM_LIMIT_BYTES = 56 * 1024 * 1024
ROW_TILE = 256
CHUNK = 256
MESH = pl.DeviceIdType.MESH
HBM = pl.BlockSpec(memory_space=pltpu.HBM)


def _cp(*sem):
    return pltpu.CompilerParams(dimension_semantics=sem, vmem_limit_bytes=VMEM_LIMIT_BYTES)


def _sigmoid(v):
    return 1.0 / (1.0 + jnp.exp(-v))


_DIMS = {"nn": (((1,), (0,)), ((), ())), "nt": (((1,), (1,)), ((), ())), "tn": (((0,), (0,)), ((), ()))}


def _pick(n, cap, step=256):
    if n <= cap:
        return n
    return next(t for t in range(cap - cap % step, 0, -step) if n % t == 0)


def _matmul(a, b, mode, name, out_dtypes=(F32,), epilogue=None, epi=(), tm=1024, tn=1024, tk=2048):
    if mode == "tn":
        (K, M), (K2, N) = a.shape, b.shape
    elif mode == "nt":
        (M, K), (N, K2) = a.shape, b.shape
    else:
        (M, K), (K2, N) = a.shape, b.shape
    assert K == K2
    tm, tn, tk = _pick(M, tm), _pick(N, tn), _pick(K, tk)
    nk = K // tk
    a_spec = pl.BlockSpec((tk, tm), lambda i, j, k: (k, i)) if mode == "tn" else pl.BlockSpec((tm, tk), lambda i, j, k: (i, k))
    b_spec = pl.BlockSpec((tn, tk), lambda i, j, k: (j, k)) if mode == "nt" else pl.BlockSpec((tk, tn), lambda i, j, k: (k, j))
    o_spec = pl.BlockSpec((tm, tn), lambda i, j, k: (i, j))
    n_epi, n_out = len(epi), len(out_dtypes)

    def body(a_ref, b_ref, *rest):
        epi_refs, out_refs = rest[:n_epi], rest[n_epi:n_epi + n_out]
        part = lax.dot_general(a_ref[...].astype(BF16), b_ref[...].astype(BF16), _DIMS[mode], preferred_element_type=F32)

        def finish(acc):
            outs = epilogue(acc, *[r[...] for r in epi_refs]) if epilogue else (acc,)
            for o_ref, o in zip(out_refs, outs):
                o_ref[...] = o.astype(o_ref.dtype)

        if nk == 1:
            finish(part)
        else:
            acc_ref = rest[-1]
            k = pl.program_id(2)

            @pl.when(k == 0)
            def _():
                acc_ref[...] = part

            @pl.when(k > 0)
            def _():
                acc_ref[...] += part

            @pl.when(k == nk - 1)
            def _():
                finish(acc_ref[...])

    outs = pl.pallas_call(
        body, name=name, grid=(M // tm, N // tn, nk),
        in_specs=[a_spec, b_spec] + [o_spec] * n_epi,
        out_specs=[o_spec] * n_out,
        out_shape=[jax.ShapeDtypeStruct((M, N), dt) for dt in out_dtypes],
        scratch_shapes=[pltpu.VMEM((tm, tn), F32)] if nk > 1 else [],
        compiler_params=_cp("parallel", "parallel", "arbitrary"),
    )(a, b, *epi)
    return outs[0] if n_out == 1 else outs


def _rms(x, g):
    r = lax.rsqrt(jnp.mean(x * x, axis=-1, keepdims=True) + EPS)
    return x * r * g


def _rms_bwd(dy, x, g):
    r = lax.rsqrt(jnp.mean(x * x, axis=-1, keepdims=True) + EPS)
    xn = x * r
    dyg = dy * g
    dx = r * (dyg - xn * jnp.mean(dyg * xn, axis=-1, keepdims=True))
    return dx, jnp.sum(dy * xn, axis=0, keepdims=True)


def _rows(d, tr=ROW_TILE):
    return pl.BlockSpec((tr, d), lambda i: (i, 0))


def _vec(d):
    return pl.BlockSpec((1, d), lambda i: (0, 0))


def _accumulate(ref, val):
    @pl.when(pl.program_id(0) == 0)
    def _():
        ref[...] = val

    @pl.when(pl.program_id(0) > 0)
    def _():
        ref[...] += val


def _norm_fwd(x, g, name):
    T, D = x.shape

    def body(x_ref, g_ref, h_ref):
        h_ref[...] = _rms(x_ref[...], g_ref[...]).astype(BF16)

    return pl.pallas_call(body, name=name, grid=(T // ROW_TILE,), in_specs=[_rows(D), _vec(D)], out_specs=_rows(D),
                          out_shape=jax.ShapeDtypeStruct((T, D), BF16), compiler_params=_cp("parallel"))(x, g)


def _residual_norm(x, m, g_post, g_next, name):
    T, D = x.shape

    def body(x_ref, m_ref, gp_ref, gn_ref, xo_ref, h_ref):
        xo = x_ref[...] + _rms(m_ref[...], gp_ref[...])
        xo_ref[...] = xo
        h_ref[...] = _rms(xo, gn_ref[...]).astype(BF16)

    return pl.pallas_call(body, name=name, grid=(T // ROW_TILE,), in_specs=[_rows(D), _rows(D), _vec(D), _vec(D)],
                          out_specs=[_rows(D), _rows(D)],
                          out_shape=[jax.ShapeDtypeStruct((T, D), F32), jax.ShapeDtypeStruct((T, D), BF16)],
                          compiler_params=_cp("parallel"))(x, m, g_post, g_next)


def _loss_and_last_norm_bwd(x, m, g_post, target, name):
    T, D = x.shape

    def body(x_ref, m_ref, gp_ref, t_ref, dx_ref, dm_ref, dg_ref, loss_ref):
        m_val, gp = m_ref[...], gp_ref[...]
        err = x_ref[...] + _rms(m_val, gp) - t_ref[...]
        dx = err * (1.0 / D)
        dx_ref[...] = dx
        dm, dg = _rms_bwd(dx, m_val, gp)
        dm_ref[...] = dm.astype(BF16)
        _accumulate(dg_ref, dg)
        _accumulate(loss_ref, jnp.full((1, 128), 0.5 * jnp.sum(err * err) * (1.0 / D), F32))

    return pl.pallas_call(body, name=name, grid=(T // ROW_TILE,), in_specs=[_rows(D), _rows(D), _vec(D), _rows(D)],
                          out_specs=[_rows(D), _rows(D), _vec(D), _vec(128)],
                          out_shape=[jax.ShapeDtypeStruct((T, D), F32), jax.ShapeDtypeStruct((T, D), BF16),
                                     jax.ShapeDtypeStruct((1, D), F32), jax.ShapeDtypeStruct((1, 128), F32)],
                          compiler_params=_cp("arbitrary"))(x, m, g_post, target)


def _norms_bwd(dx, dh, x_in, g_pre, m_prev, g_post_prev, name):
    T, D = dx.shape
    with_prev = m_prev is not None

    def body(*refs):
        if with_prev:
            dx_ref, dh_ref, x_ref, gq_ref, m_ref, gp_ref, dxo_ref, dgq_ref, dm_ref, dgp_ref = refs
        else:
            dx_ref, dh_ref, x_ref, gq_ref, dxo_ref, dgq_ref = refs
        d_in, dgq = _rms_bwd(dh_ref[...], x_ref[...], gq_ref[...])
        dxo = dx_ref[...] + d_in
        dxo_ref[...] = dxo
        _accumulate(dgq_ref, dgq)
        if with_prev:
            dm, dgp = _rms_bwd(dxo, m_ref[...], gp_ref[...])
            dm_ref[...] = dm.astype(BF16)
            _accumulate(dgp_ref, dgp)

    ins, in_specs = [dx, dh, x_in, g_pre], [_rows(D), _rows(D), _rows(D), _vec(D)]
    out_specs = [_rows(D), _vec(D)]
    out_shape = [jax.ShapeDtypeStruct((T, D), F32), jax.ShapeDtypeStruct((1, D), F32)]
    if with_prev:
        ins += [m_prev, g_post_prev]
        in_specs += [_rows(D), _vec(D)]
        out_specs += [_rows(D), _vec(D)]
        out_shape += [jax.ShapeDtypeStruct((T, D), BF16), jax.ShapeDtypeStruct((1, D), F32)]
    return pl.pallas_call(body, name=name, grid=(T // ROW_TILE,), in_specs=in_specs, out_specs=out_specs, out_shape=out_shape,
                          compiler_params=_cp("arbitrary"))(*ins)


def _window_weights(g):
    w = 2 << g
    return w, [jnp.where(j < w, 1.0, 0.0).astype(F32) for j in range(MAX_WINDOW)]


def _valid_count(r0, rows, w):
    t = (lax.broadcasted_iota(jnp.int32, (rows, 1), 0) + (r0 + 1)).astype(F32)
    return jnp.minimum(t, w.astype(F32))


def _pool_fwd(z, pool_w, pool_scale, name):
    T = z.shape[0]
    PG = pool_w.shape[-1]
    DP = N_GROUPS * PG
    rc = min(CHUNK, T)

    def body(u_ref, pw_ref, sc_ref, pooled_ref, y_ref, pad):
        w, wts = _window_weights(pl.program_id(0))
        pad[pl.ds(0, MAX_WINDOW), :] = jnp.zeros((MAX_WINDOW, PG), F32)
        pad[pl.ds(MAX_WINDOW, T), :] = u_ref[...]
        for r0 in range(0, T, rc):
            acc = jnp.zeros((rc, PG), F32)
            for j in range(MAX_WINDOW):
                acc = acc + wts[j] * pad[pl.ds(MAX_WINDOW + r0 - j, rc), :]
            pooled = acc / _valid_count(r0, rc, w) - u_ref[pl.ds(r0, rc), :]
            pooled_ref[pl.ds(r0, rc), :] = pooled.astype(BF16)
        mixed = jnp.dot(pooled_ref[...], pw_ref[...].astype(BF16), preferred_element_type=F32)
        y_ref[...] = (mixed * sc_ref[...]).astype(BF16)

    col = lambda g: (0, g)
    return pl.pallas_call(
        body, name=name, grid=(N_GROUPS,),
        in_specs=[pl.BlockSpec((T, PG), col), pl.BlockSpec((None, PG, PG), lambda g: (g, 0, 0)), pl.BlockSpec((1, PG), col)],
        out_specs=[pl.BlockSpec((T, PG), col), pl.BlockSpec((T, PG), col)],
        out_shape=[jax.ShapeDtypeStruct((T, DP), BF16), jax.ShapeDtypeStruct((T, DP), BF16)],
        scratch_shapes=[pltpu.VMEM((T + MAX_WINDOW, PG), F32)], compiler_params=_cp("parallel"))(z, pool_w, pool_scale)


def _pool_bwd(dy, pooled, pool_w, pool_scale, name):
    T = dy.shape[0]
    PG = pool_w.shape[-1]
    DP = N_GROUPS * PG
    rc = min(CHUNK, T)

    def body(dy_ref, pooled_ref, pw_ref, sc_ref, du_ref, dpw_ref, dsc_ref, pad, dp_ref):
        w, wts = _window_weights(pl.program_id(0))
        pooled_v, pw = pooled_ref[...], pw_ref[...].astype(BF16)
        dy_v = dy_ref[...]
        mixed = jnp.dot(pooled_v, pw, preferred_element_type=F32)
        dsc_ref[...] = jnp.sum(dy_v * mixed, axis=0, keepdims=True)
        dmixed = (dy_v * sc_ref[...]).astype(BF16)
        dpw_ref[...] = lax.dot_general(pooled_v, dmixed, _DIMS["tn"], preferred_element_type=F32)
        dp_ref[...] = lax.dot_general(dmixed, pw, _DIMS["nt"], preferred_element_type=F32)
        pad[pl.ds(T, MAX_WINDOW), :] = jnp.zeros((MAX_WINDOW, PG), F32)
        for r0 in range(0, T, rc):
            pad[pl.ds(r0, rc), :] = dp_ref[pl.ds(r0, rc), :] / _valid_count(r0, rc, w)
        for r0 in range(0, T, rc):
            acc = jnp.zeros((rc, PG), F32)
            for j in range(MAX_WINDOW):
                acc = acc + wts[j] * pad[pl.ds(r0 + j, rc), :]
            du_ref[pl.ds(r0, rc), :] = (acc - dp_ref[pl.ds(r0, rc), :]).astype(BF16)

    col = lambda g: (0, g)
    return pl.pallas_call(
        body, name=name, grid=(N_GROUPS,),
        in_specs=[pl.BlockSpec((T, PG), col), pl.BlockSpec((T, PG), col), pl.BlockSpec((None, PG, PG), lambda g: (g, 0, 0)),
                  pl.BlockSpec((1, PG), col)],
        out_specs=[pl.BlockSpec((T, PG), col), pl.BlockSpec((None, PG, PG), lambda g: (g, 0, 0)), pl.BlockSpec((1, PG), col)],
        out_shape=[jax.ShapeDtypeStruct((T, DP), BF16), jax.ShapeDtypeStruct((N_GROUPS, PG, PG), F32),
                   jax.ShapeDtypeStruct((1, DP), F32)],
        scratch_shapes=[pltpu.VMEM((T + MAX_WINDOW, PG), F32), pltpu.VMEM((T, PG), F32)],
        compiler_params=_cp("parallel"))(dy, pooled, pool_w, pool_scale)


def _conv_fwd(z, conv_w, conv_b, d_pool, name, tc=128):
    T = z.shape[0]
    DC = conv_w.shape[-1]
    rc = min(CHUNK, T)
    v0, g0 = d_pool // tc, (d_pool + DC) // tc

    def body(v_ref, gt_ref, w_ref, b_ref, a_ref, c_ref, pad):
        pad[pl.ds(0, CONV_PAD), :] = jnp.zeros((CONV_PAD, tc), F32)
        for r0 in range(0, T, rc):
            a = v_ref[pl.ds(r0, rc), :] * _sigmoid(gt_ref[pl.ds(r0, rc), :])
            a_ref[pl.ds(r0, rc), :] = a
            pad[pl.ds(CONV_PAD + r0, rc), :] = a
        for r0 in range(0, T, rc):
            acc = jnp.zeros((rc, tc), F32) + b_ref[...]
            for k in range(CONV_K):
                acc = acc + w_ref[pl.ds(k, 1), :] * pad[pl.ds(CONV_PAD - (CONV_K - 1) + k + r0, rc), :]
            c_ref[pl.ds(r0, rc), :] = acc

    col = lambda j: (0, j)
    return pl.pallas_call(
        body, name=name, grid=(DC // tc,),
        in_specs=[pl.BlockSpec((T, tc), lambda j: (0, v0 + j)), pl.BlockSpec((T, tc), lambda j: (0, g0 + j)),
                  pl.BlockSpec((CONV_K, tc), col), pl.BlockSpec((1, tc), col)],
        out_specs=[pl.BlockSpec((T, tc), col), pl.BlockSpec((T, tc), col)],
        out_shape=[jax.ShapeDtypeStruct((T, DC), F32), jax.ShapeDtypeStruct((T, DC), F32)],
        scratch_shapes=[pltpu.VMEM((T + CONV_PAD, tc), F32)], compiler_params=_cp("parallel"))(z, z, conv_w, conv_b)


def _conv_bwd(dc, a, z, conv_w, d_pool, name, tc=128):
    T, DC = dc.shape
    rc = min(CHUNK, T)
    v0, g0 = d_pool // tc, (d_pool + DC) // tc

    def body(dc_ref, a_ref, v_ref, gt_ref, w_ref, dv_ref, dg_ref, dw_ref, db_ref, apad, dpad):
        apad[pl.ds(0, CONV_PAD), :] = jnp.zeros((CONV_PAD, tc), F32)
        apad[pl.ds(CONV_PAD, T), :] = a_ref[...]
        dpad[pl.ds(0, T), :] = dc_ref[...]
        dpad[pl.ds(T, CONV_PAD), :] = jnp.zeros((CONV_PAD, tc), F32)
        db_ref[...] = jnp.sum(dc_ref[...], axis=0, keepdims=True)
        for k in range(CONV_K):
            acc = jnp.zeros((8, tc), F32)
            for r0 in range(0, T, rc):
                prod = dc_ref[pl.ds(r0, rc), :] * apad[pl.ds(CONV_PAD - (CONV_K - 1) + k + r0, rc), :]
                acc = acc + jnp.sum(prod.reshape(rc // 8, 8, tc), axis=0)
            dw_ref[pl.ds(k, 1), :] = jnp.sum(acc, axis=0, keepdims=True)
        for r0 in range(0, T, rc):
            da = jnp.zeros((rc, tc), F32)
            for k in range(CONV_K):
                da = da + w_ref[pl.ds(k, 1), :] * dpad[pl.ds(r0 + (CONV_K - 1) - k, rc), :]
            sig = _sigmoid(gt_ref[pl.ds(r0, rc), :])
            dv_ref[pl.ds(r0, rc), :] = (da * sig).astype(BF16)
            dg_ref[pl.ds(r0, rc), :] = (da * v_ref[pl.ds(r0, rc), :] * sig * (1.0 - sig)).astype(BF16)

    col = lambda j: (0, j)
    return pl.pallas_call(
        body, name=name, grid=(DC // tc,),
        in_specs=[pl.BlockSpec((T, tc), col), pl.BlockSpec((T, tc), col), pl.BlockSpec((T, tc), lambda j: (0, v0 + j)),
                  pl.BlockSpec((T, tc), lambda j: (0, g0 + j)), pl.BlockSpec((CONV_K, tc), col)],
        out_specs=[pl.BlockSpec((T, tc), col), pl.BlockSpec((T, tc), col), pl.BlockSpec((CONV_K, tc), col),
                   pl.BlockSpec((1, tc), col)],
        out_shape=[jax.ShapeDtypeStruct((T, DC), BF16), jax.ShapeDtypeStruct((T, DC), BF16),
                   jax.ShapeDtypeStruct((CONV_K, DC), F32), jax.ShapeDtypeStruct((1, DC), F32)],
        scratch_shapes=[pltpu.VMEM((T + CONV_PAD, tc), F32), pltpu.VMEM((T + CONV_PAD, tc), F32)],
        compiler_params=_cp("parallel"))(dc, a, z, z, conv_w)


def _layer_norm_parts(c, g, b):
    mu = jnp.mean(c, axis=-1, keepdims=True)
    xc = c - mu
    rstd = lax.rsqrt(jnp.mean(xc * xc, axis=-1, keepdims=True) + EPS)
    xhat = xc * rstd
    return xhat, rstd, xhat * g + b


def _ln_silu_fwd(c, g, b, name):
    T, DC = c.shape

    def body(c_ref, g_ref, b_ref, y_ref):
        _, _, ln = _layer_norm_parts(c_ref[...], g_ref[...], b_ref[...])
        y_ref[...] = (ln * _sigmoid(ln)).astype(BF16)

    return pl.pallas_call(body, name=name, grid=(T // ROW_TILE,), in_specs=[_rows(DC), _vec(DC), _vec(DC)], out_specs=_rows(DC),
                          out_shape=jax.ShapeDtypeStruct((T, DC), BF16), compiler_params=_cp("parallel"))(c, g, b)


def _ln_silu_bwd(dy, c, g, b, name):
    T, DC = c.shape

    def body(dy_ref, c_ref, g_ref, b_ref, dc_ref, dg_ref, db_ref):
        gain = g_ref[...]
        xhat, rstd, ln = _layer_norm_parts(c_ref[...], gain, b_ref[...])
        s = _sigmoid(ln)
        dln = dy_ref[...] * (s * (1.0 + ln * (1.0 - s)))
        _accumulate(dg_ref, jnp.sum(dln * xhat, axis=0, keepdims=True))
        _accumulate(db_ref, jnp.sum(dln, axis=0, keepdims=True))
        dxh = dln * gain
        dc_ref[...] = rstd * (dxh - jnp.mean(dxh, axis=-1, keepdims=True) - xhat * jnp.mean(dxh * xhat, axis=-1, keepdims=True))

    return pl.pallas_call(body, name=name, grid=(T // ROW_TILE,),
                          in_specs=[pl.BlockSpec((ROW_TILE, DC), lambda i: (i, 1)), _rows(DC), _vec(DC), _vec(DC)],
                          out_specs=[_rows(DC), _vec(DC), _vec(DC)],
                          out_shape=[jax.ShapeDtypeStruct((T, DC), F32), jax.ShapeDtypeStruct((1, DC), F32),
                                     jax.ShapeDtypeStruct((1, DC), F32)],
                          compiler_params=_cp("arbitrary"))(dy, c, g, b)


def _short_specs(T, DS, tc):
    n = DS // tc
    return [pl.BlockSpec((T, tc), lambda j: (0, j)), pl.BlockSpec((T, tc), lambda j: (0, n + j)),
            pl.BlockSpec((T, tc), lambda j: (0, 2 * n + j))]


def _short_fwd(z, w, name, tc=256):
    T = z.shape[0]
    DS = w.shape[-1]
    rc = min(CHUNK, T)

    def body(b_ref, cg_ref, u_ref, w_ref, y_ref, pad):
        pad[pl.ds(0, SHORT_PAD), :] = jnp.zeros((SHORT_PAD, tc), F32)
        pad[pl.ds(SHORT_PAD, T), :] = cg_ref[...] * u_ref[...]
        for r0 in range(0, T, rc):
            r = jnp.zeros((rc, tc), F32)
            for k in range(SHORT_K):
                r = r + w_ref[pl.ds(k, 1), :] * pad[pl.ds(SHORT_PAD - (SHORT_K - 1) + k + r0, rc), :]
            y_ref[pl.ds(r0, rc), :] = (b_ref[pl.ds(r0, rc), :] * r).astype(BF16)

    col = lambda j: (0, j)
    return pl.pallas_call(body, name=name, grid=(DS // tc,), in_specs=_short_specs(T, DS, tc) + [pl.BlockSpec((SHORT_K, tc), col)],
                          out_specs=pl.BlockSpec((T, tc), col), out_shape=jax.ShapeDtypeStruct((T, DS), BF16),
                          scratch_shapes=[pltpu.VMEM((T + SHORT_PAD, tc), F32)], compiler_params=_cp("parallel"))(z, z, z, w)


def _short_bwd(dy, z, w, name, tc=256):
    T, DS = dy.shape
    rc = min(CHUNK, T)

    def body(dy_ref, b_ref, cg_ref, u_ref, w_ref, db_ref, dcg_ref, du_ref, dw_ref, qpad, rpad):
        qpad[pl.ds(0, SHORT_PAD), :] = jnp.zeros((SHORT_PAD, tc), F32)
        qpad[pl.ds(SHORT_PAD, T), :] = cg_ref[...] * u_ref[...]
        rpad[pl.ds(0, T), :] = dy_ref[...] * b_ref[...]
        rpad[pl.ds(T, SHORT_PAD), :] = jnp.zeros((SHORT_PAD, tc), F32)
        accs = [jnp.zeros((8, tc), F32) for _ in range(SHORT_K)]
        for r0 in range(0, T, rc):
            r = jnp.zeros((rc, tc), F32)
            dq = jnp.zeros((rc, tc), F32)
            dr = rpad[pl.ds(r0, rc), :]
            for k in range(SHORT_K):
                q_k = qpad[pl.ds(SHORT_PAD - (SHORT_K - 1) + k + r0, rc), :]
                r = r + w_ref[pl.ds(k, 1), :] * q_k
                dq = dq + w_ref[pl.ds(k, 1), :] * rpad[pl.ds(r0 + (SHORT_K - 1) - k, rc), :]
                accs[k] = accs[k] + jnp.sum((dr * q_k).reshape(rc // 8, 8, tc), axis=0)
            db_ref[pl.ds(r0, rc), :] = (dy_ref[pl.ds(r0, rc), :] * r).astype(BF16)
            dcg_ref[pl.ds(r0, rc), :] = (dq * u_ref[pl.ds(r0, rc), :]).astype(BF16)
            du_ref[pl.ds(r0, rc), :] = (dq * cg_ref[pl.ds(r0, rc), :]).astype(BF16)
        for k in range(SHORT_K):
            dw_ref[pl.ds(k, 1), :] = jnp.sum(accs[k], axis=0, keepdims=True)

    col = lambda j: (0, j)
    tile = pl.BlockSpec((T, tc), col)
    return pl.pallas_call(body, name=name, grid=(DS // tc,),
                          in_specs=[tile] + _short_specs(T, DS, tc) + [pl.BlockSpec((SHORT_K, tc), col)],
                          out_specs=[tile, tile, tile, pl.BlockSpec((SHORT_K, tc), col)],
                          out_shape=[jax.ShapeDtypeStruct((T, DS), BF16)] * 3 + [jax.ShapeDtypeStruct((SHORT_K, DS), F32)],
                          scratch_shapes=[pltpu.VMEM((T + SHORT_PAD, tc), F32), pltpu.VMEM((T + SHORT_PAD, tc), F32)],
                          compiler_params=_cp("parallel"))(dy, z, z, z, w)


def _tile_rows(rows, cols, n_bufs):
    budget = VMEM_LIMIT_BYTES // 2 // (2 * n_bufs * 4 * cols)
    tr = rows
    while tr > budget and tr % 16 == 0:
        tr //= 2
    return tr


def _placed_call(body, name, place, grid, in_specs, out_specs, out_shape, ins):
    grid_spec = pltpu.PrefetchScalarGridSpec(num_scalar_prefetch=1, grid=grid, in_specs=in_specs, out_specs=out_specs)
    return pl.pallas_call(body, name=name, grid_spec=grid_spec, out_shape=out_shape, compiler_params=_cp("parallel"))(place, *ins)


def _cast_into_full(w, layer, kind, place, name):
    _, R, C = w.shape
    tr = _tile_rows(R, C, 2)
    nb = R // tr
    if kind == "col":
        full, out_spec = (R, C * N_CHIPS), pl.BlockSpec((tr, C), lambda i, s: (i, s[0]))
    else:
        full, out_spec = (R * N_CHIPS, C), pl.BlockSpec((tr, C), lambda i, s: (s[0] * nb + i, 0))

    def body(s_ref, w_ref, o_ref):
        o_ref[...] = w_ref[...].astype(BF16)

    return _placed_call(body, name, place, (nb,), [pl.BlockSpec((None, tr, C), lambda i, s: (layer, i, 0))], out_spec,
                        jax.ShapeDtypeStruct(full, BF16), [w])


def _add_pair(grad, theirs, kind, place, name):
    R, C = grad.shape
    piece_rows = R // 2 if kind == "col" else R // N_CHIPS // 2
    tr = _tile_rows(piece_rows, C, 3)
    nb = piece_rows // tr
    if kind == "col":
        g_spec = pl.BlockSpec((tr, C), lambda i, s: (s[1] * nb + i, 0))
    else:
        g_spec = pl.BlockSpec((tr, C), lambda i, s: ((2 * (i // nb) + s[1]) * nb + i % nb, 0))
    flat = pl.BlockSpec((tr, C), lambda i, s: (i, 0))

    def body(s_ref, a_ref, b_ref, o_ref):
        o_ref[...] = (a_ref[...].astype(F32) + b_ref[...].astype(F32)).astype(BF16)

    return _placed_call(body, name, place, (R // 2 // tr,), [g_spec, flat], flat, jax.ShapeDtypeStruct((R // 2, C), BF16),
                        [grad, theirs])


def _sum_chips(chip_sum, arrived, kind, place, name):
    _, H, W = arrived.shape
    tr = _tile_rows(H, W, 6)
    nb = H // tr
    if kind == "col":
        own_spec = pl.BlockSpec((tr, W), lambda i, s: (i, s[0]))
    else:
        own_spec = pl.BlockSpec((tr, W), lambda i, s: (s[0] * nb + i, 0))

    def body(s_ref, p_ref, r_ref, o_ref):
        acc = p_ref[...].astype(F32)
        for i in range(N_CHIPS - 1):
            acc = acc + r_ref[i].astype(F32)
        o_ref[...] = acc

    return _placed_call(body, name, place, (nb,), [own_spec, pl.BlockSpec((N_CHIPS - 1, tr, W), lambda i, s: (0, i, 0))],
                        pl.BlockSpec((tr, W), lambda i, s: (s[1] * nb + i, 0)), jax.ShapeDtypeStruct((2 * H, W), F32),
                        [chip_sum, arrived])


def _adamw_values(w, g, m, v):
    m = ADAM_B1 * m + (1.0 - ADAM_B1) * g
    v = ADAM_B2 * v + (1.0 - ADAM_B2) * (g * g)
    m_hat = m / (1.0 - ADAM_B1 ** ADAM_STEP)
    v_hat = v / (1.0 - ADAM_B2 ** ADAM_STEP)
    return -ADAM_LR * (m_hat / (jnp.sqrt(v_hat) + ADAM_EPS) + ADAM_WD * w), m, v


def _adamw(w, g, m, v, name, layer=0, carried=None):
    L, R, C = w.shape
    tr = _tile_rows(R, C, 8)

    def body(w_ref, g_ref, m_ref, v_ref, *rest):
        go_ref, d_ref, mo_ref, vo_ref = rest[-4:]
        g_val = g_ref[...]
        d, m_new, v_new = _adamw_values(w_ref[...], g_val, m_ref[...], v_ref[...])
        go_ref[...], d_ref[...], mo_ref[...], vo_ref[...] = g_val, d, m_new, v_new

    lay = pl.BlockSpec((None, tr, C), lambda i: (layer, i, 0))
    ins = [w, g, m, v]
    in_specs = [lay, pl.BlockSpec((tr, C), lambda i: (i, 0)), lay, lay]
    aliases = {}
    if carried is not None:
        ins += list(carried)
        in_specs += [pl.BlockSpec(memory_space=pl.ANY)] * 4
        aliases = {4 + i: i for i in range(4)}
    return pl.pallas_call(body, name=name, grid=(R // tr,), in_specs=in_specs, out_specs=[lay] * 4,
                          out_shape=[jax.ShapeDtypeStruct((L, R, C), F32)] * 4, input_output_aliases=aliases,
                          compiler_params=_cp("parallel"))(*ins)


def _aligned(v, m):
    return v if isinstance(v, int) else pl.multiple_of(v, m)


def _place():
    x, y, c = lax.axis_index("x"), lax.axis_index("y"), lax.axis_index("c")
    other_chips = [(x, 1 - y), (1 - x, y), (1 - x, 1 - y)]
    return x, y, c, 2 * x + y, other_chips


def _chip_index(chip):
    return 2 * chip[0] + chip[1]


def _shard(ref, kind, k):
    R, C = ref.shape
    if kind == "col":
        return ref.at[:, pl.ds(_aligned(k * (C // N_CHIPS), 128), C // N_CHIPS)]
    return ref.at[pl.ds(_aligned(k * (R // N_CHIPS), 16), R // N_CHIPS), :]


def _piece(ref, kind, k, h):
    R, C = ref.shape
    if kind == "col":
        return ref.at[pl.ds(_aligned(h * (R // 2), 16), R // 2), pl.ds(_aligned(k * (C // N_CHIPS), 128), C // N_CHIPS)]
    rs = R // N_CHIPS
    return ref.at[pl.ds(_aligned(k * rs + h * (rs // 2), 16), rs // 2), :]


def _compact_piece(ref, kind, k):
    R2, C = ref.shape
    if kind == "col":
        return ref.at[:, pl.ds(_aligned(k * (C // N_CHIPS), 128), C // N_CHIPS)]
    return ref.at[pl.ds(_aligned(k * (R2 // N_CHIPS), 16), R2 // N_CHIPS), :]


def _half_rows(ref, h):
    R = ref.shape[0]
    return ref.at[pl.ds(_aligned(h * (R // 2), 16), R // 2), :]


class _Copies:
    def __init__(self, send_sems, recv_sems):
        self.send_sems, self.recv_sems = send_sems, recv_sems
        self.n_remote = 0

    def remote(self, src, dst, device):
        k = self.n_remote
        self.n_remote += 1
        return pltpu.make_async_remote_copy(src_ref=src, dst_ref=dst, send_sem=self.send_sems.at[k], recv_sem=self.recv_sems.at[k],
                                            device_id=device, device_id_type=MESH)


class _Job:
    def __init__(self, ins, out_shape, aliases, n_remote, build):
        self.ins, self.out_shape, self.aliases, self.n_remote, self.build = list(ins), list(out_shape), dict(aliases), n_remote, build


def _merge_jobs(jobs):
    ins, out_shape, aliases, spans = [], [], {}, []
    for job in jobs:
        spans.append((len(ins), len(job.ins), len(out_shape), len(job.out_shape)))
        aliases.update({len(ins) + i: len(out_shape) + o for i, o in job.aliases.items()})
        ins += job.ins
        out_shape += job.out_shape

    def build(in_refs, out_refs, cp):
        copies = []
        for job, (i0, ni, o0, no) in zip(jobs, spans):
            copies += job.build(in_refs[i0:i0 + ni], out_refs[o0:o0 + no], cp)
        return copies

    return _Job(ins, out_shape, aliases, sum(j.n_remote for j in jobs), build)


def _run_jobs(jobs, name):
    job = _merge_jobs(jobs)
    n_in, n_out = len(job.ins), len(job.out_shape)

    def body(*refs):
        copies = job.build(refs[:n_in], refs[n_in:n_in + n_out], _Copies(*refs[n_in + n_out:]))
        for d in copies:
            d.start()
        for d in copies:
            d.wait_recv()
        for d in copies:
            d.wait_send()

    outs = pl.pallas_call(
        body, name=name, in_specs=[HBM] * n_in, out_specs=[HBM] * n_out, out_shape=job.out_shape,
        input_output_aliases=job.aliases,
        scratch_shapes=[pltpu.SemaphoreType.DMA((job.n_remote,)), pltpu.SemaphoreType.DMA((job.n_remote,))])(*job.ins)
    split, o0 = [], 0
    for j in jobs:
        split.append(outs[o0:o0 + len(j.out_shape)])
        o0 += len(j.out_shape)
    return split


def _in_place(arrays):
    return [jax.ShapeDtypeStruct(a.shape, a.dtype) for a in arrays], {u: u for u in range(len(arrays))}


def _gather_ici_job(fulls, kinds):
    def build(in_refs, out_refs, cp):
        x, y, c, me, chips = _place()
        copies = []
        for ref, kind in zip(out_refs, kinds):
            mine = _piece(ref, kind, me, c)
            copies += [cp.remote(mine, mine, (*chip, c)) for chip in chips]
        return copies

    return _Job(fulls, *_in_place(fulls), 3 * len(fulls), build)


def _gather_forward_job(fulls, kinds):
    def build(in_refs, out_refs, cp):
        x, y, c, me, chips = _place()
        copies = []
        for ref, kind in zip(out_refs, kinds):
            for chip in chips:
                piece = _piece(ref, kind, _chip_index(chip), c)
                copies.append(cp.remote(piece, piece, (x, y, 1 - c)))
        return copies

    return _Job(fulls, *_in_place(fulls), 3 * len(fulls), build)


def _gather_small_job(fulls, axes):
    def build(in_refs, out_refs, cp):
        x, y, c, me, chips = _place()
        copies = []
        for ref, ax in zip(out_refs, axes):
            n = ref.shape[ax] // N_CHIPS
            idx = [slice(None)] * len(ref.shape)
            idx[ax] = pl.ds(_aligned(me * n, n), n)
            mine = ref.at[tuple(idx)]
            copies += [cp.remote(mine, mine, (*chip, c)) for chip in chips]
        return copies

    return _Job(fulls, *_in_place(fulls), 3 * len(fulls), build)


def _exchange_halves_job(grads, kinds):
    def build(in_refs, out_refs, cp):
        x, y, c, me, chips = _place()
        copies = []
        for src, dst, kind in zip(in_refs, out_refs, kinds):
            if kind == "col":
                copies.append(cp.remote(_half_rows(src, 1 - c), dst, (x, y, 1 - c)))
            else:
                copies += [cp.remote(_piece(src, "row", k, 1 - c), _compact_piece(dst, "row", k), (x, y, 1 - c))
                           for k in range(N_CHIPS)]
        return copies

    out_shape = [jax.ShapeDtypeStruct((g.shape[0] // 2, g.shape[1]), g.dtype) for g in grads]
    return _Job(grads, out_shape, {}, sum(1 if k == "col" else N_CHIPS for k in kinds), build)


def _scatter_job(halves, kinds):
    def build(in_refs, out_refs, cp):
        x, y, c, me, chips = _place()
        copies = []
        for src, dst, kind in zip(in_refs, out_refs, kinds):
            copies += [cp.remote(_compact_piece(src, kind, _chip_index(chip)), dst.at[r], (*chip, c))
                       for r, chip in enumerate(chips)]
        return copies

    def part_shape(h, kind):
        return (h.shape[0], h.shape[1] // N_CHIPS) if kind == "col" else (h.shape[0] // N_CHIPS, h.shape[1])

    out_shape = [jax.ShapeDtypeStruct((N_CHIPS - 1,) + part_shape(h, k), h.dtype) for h, k in zip(halves, kinds)]
    return _Job(halves, out_shape, {}, 3 * len(halves), build)


def _share_job(shards):
    def build(in_refs, out_refs, cp):
        x, y, c, me, chips = _place()
        copies = []
        for ref in out_refs:
            mine = _half_rows(ref, c)
            copies.append(cp.remote(mine, mine, (x, y, 1 - c)))
        return copies

    return _Job(shards, *_in_place(shards), len(shards), build)


def _all_reduce_small(packed, name):
    R, C = packed.shape
    n_dev = 2 * N_CHIPS

    def body(in_ref, out_ref, slots, send_sems, recv_sems):
        x, y, c, me, chips = _place()
        my_slot = 2 * me + c
        copies = []
        for p in range(1, n_dev):
            peer = (x ^ (p >> 2), y ^ ((p >> 1) & 1), c ^ (p & 1))
            copies.append(pltpu.make_async_remote_copy(src_ref=in_ref, dst_ref=slots.at[my_slot], send_sem=send_sems.at[p - 1],
                                                       recv_sem=recv_sems.at[p - 1], device_id=peer, device_id_type=MESH))
        for d in copies:
            d.start()
        slots[my_slot] = in_ref[...]
        for d in copies:
            d.wait_recv()
        acc = slots[0]
        for i in range(1, n_dev):
            acc = acc + slots[i]
        out_ref[...] = acc
        for d in copies:
            d.wait_send()

    return pl.pallas_call(
        body, name=name, in_specs=[pl.BlockSpec(memory_space=pltpu.VMEM)], out_specs=pl.BlockSpec(memory_space=pltpu.VMEM),
        out_shape=jax.ShapeDtypeStruct((R, C), F32),
        scratch_shapes=[pltpu.VMEM((n_dev, R, C), F32), pltpu.SemaphoreType.DMA((n_dev - 1,)), pltpu.SemaphoreType.DMA((n_dev - 1,))],
        compiler_params=pltpu.CompilerParams(vmem_limit_bytes=VMEM_LIMIT_BYTES))(packed)


def _pack(arrays, width):
    rows = []
    for a in arrays:
        flat = a.reshape(-1)
        n_rows = -(-flat.shape[0] // width)
        rows.append(jnp.pad(flat, (0, n_rows * width - flat.shape[0])).reshape(n_rows, width))
    n = sum(r.shape[0] for r in rows)
    rows.append(jnp.zeros((-n % 8, width), F32))
    return jnp.concatenate(rows, axis=0)


def _unpack(packed, shapes):
    out, r0, width = [], 0, packed.shape[1]
    for shape in shapes:
        size = 1
        for d in shape:
            size *= d
        n_rows = -(-size // width)
        out.append(packed[r0:r0 + n_rows].reshape(-1)[:size].reshape(shape))
        r0 += n_rows
    return out


def kernel(x, mix_pre_g, mix_post_g, ffn_pre_g, ffn_post_g, ab_w_in, pool_w, pool_scale, conv_w, conv_b, conv_ln_g, conv_ln_b, ab_w_out, sc_w_in, sc_conv_w, sc_w_out, ffn_w1, ffn_w2, loss_target, m_mix_pre_g, m_mix_post_g, m_ffn_pre_g, m_ffn_post_g, m_ab_w_in, m_pool_w, m_pool_scale, m_conv_w, m_conv_b, m_conv_ln_g, m_conv_ln_b, m_ab_w_out, m_sc_w_in, m_sc_conv_w, m_sc_w_out, m_ffn_w1, m_ffn_w2, v_mix_pre_g, v_mix_post_g, v_ffn_pre_g, v_ffn_post_g, v_ab_w_in, v_pool_w, v_pool_scale, v_conv_w, v_conv_b, v_conv_ln_g, v_conv_ln_b, v_ab_w_out, v_sc_w_in, v_sc_conv_w, v_sc_w_out, v_ffn_w1, v_ffn_w2):
    x0, target = x[0], loss_target[0]
    T, D = x0.shape
    DP = pool_scale.shape[-1]
    gain = lambda g, layer: g[layer][None, :]

    big = [("ab_w_in", ab_w_in, 0, "col"), ("ab_w_out", ab_w_out, 0, "row"), ("ffn_w1_0", ffn_w1, 0, "col"),
           ("ffn_w2_0", ffn_w2, 0, "row"), ("sc_w_in", sc_w_in, 0, "col"), ("sc_w_out", sc_w_out, 0, "row"),
           ("ffn_w1_1", ffn_w1, 1, "col"), ("ffn_w2_1", ffn_w2, 1, "row")]
    kinds = [b[3] for b in big]
    chip = 2 * lax.axis_index("x") + lax.axis_index("y")
    place = jnp.stack([chip, lax.axis_index("c")]).astype(jnp.int32)
    fulls = [_cast_into_full(w, layer, kind, place, "cast_" + name) for name, w, layer, kind in big]

    def own_in_zeros(shard, ax):
        full = jnp.zeros(tuple(d * N_CHIPS if i == ax else d for i, d in enumerate(shard.shape)), shard.dtype)
        return lax.dynamic_update_slice_in_dim(full, shard, chip * shard.shape[ax], axis=ax)

    smalls = [own_in_zeros(pool_w[0], 1), own_in_zeros(conv_w[0], 1), own_in_zeros(sc_conv_w[0], 1)]
    fulls, smalls = _run_jobs([_gather_ici_job(fulls, kinds), _gather_small_job(smalls, [1, 1, 1])], "gather_weights")
    (fulls,) = _run_jobs([_gather_forward_job(fulls, kinds)], "gather_forward")
    w_in0, w_out0, w1_0, w2_0, w_in1, w_out1, w1_1, w2_1 = fulls
    pool_w_full, conv_w_full, sc_conv_w_full = smalls

    relu_sq = lambda acc: (jnp.maximum(acc, 0.0), jnp.square(jnp.maximum(acc, 0.0)))
    relu_sq_bwd = lambda acc, a: (acc * (2.0 * a.astype(F32)),)

    h0 = _norm_fwd(x0, gain(mix_pre_g, 0), "norm_in")
    z0 = _matmul(h0, w_in0, "nn", "mix0_in")
    pooled, y_pool = _pool_fwd(z0, pool_w_full, pool_scale, "pool_fwd")
    a_conv, c_conv = _conv_fwd(z0, conv_w_full, conv_b, DP, "conv_fwd")
    y_conv = _ln_silu_fwd(c_conv, conv_ln_g, conv_ln_b, "ln_silu_fwd")
    y0 = jnp.concatenate([y_pool, y_conv], axis=1)
    m0 = _matmul(y0, w_out0, "nn", "mix0_out")
    x1, h1 = _residual_norm(x0, m0, gain(mix_post_g, 0), gain(ffn_pre_g, 0), "res_mix0")
    a0, a0sq = _matmul(h1, w1_0, "nn", "ffn0_up", out_dtypes=(BF16, BF16), epilogue=relu_sq)
    f0 = _matmul(a0sq, w2_0, "nn", "ffn0_down")
    x2, h2 = _residual_norm(x1, f0, gain(ffn_post_g, 0), gain(mix_pre_g, 1), "res_ffn0")
    z1 = _matmul(h2, w_in1, "nn", "mix1_in")
    y1 = _short_fwd(z1, sc_conv_w_full, "short_fwd")
    m1 = _matmul(y1, w_out1, "nn", "mix1_out")
    x3, h3 = _residual_norm(x2, m1, gain(mix_post_g, 1), gain(ffn_pre_g, 1), "res_mix1")
    a1, a1sq = _matmul(h3, w1_1, "nn", "ffn1_up", out_dtypes=(BF16, BF16), epilogue=relu_sq)
    f1 = _matmul(a1sq, w2_1, "nn", "ffn1_down")

    dx, df1, d_ffn_post_1, loss_row = _loss_and_last_norm_bwd(x3, f1, gain(ffn_post_g, 1), target, "loss")
    g_w2_1 = _matmul(a1sq, df1, "tn", "ffn1_down_dw", out_dtypes=(BF16,))
    dz = _matmul(df1, w2_1, "nt", "ffn1_down_dx", out_dtypes=(BF16,), epilogue=relu_sq_bwd, epi=(a1,))
    g_w1_1 = _matmul(h3, dz, "tn", "ffn1_up_dw", out_dtypes=(BF16,))
    dh = _matmul(dz, w1_1, "nt", "ffn1_up_dx")
    dx, d_ffn_pre_1, dm1, d_mix_post_1 = _norms_bwd(dx, dh, x3, gain(ffn_pre_g, 1), m1, gain(mix_post_g, 1), "norms_bwd3")

    g_w_out1 = _matmul(y1, dm1, "tn", "mix1_out_dw", out_dtypes=(BF16,))
    dy1 = _matmul(dm1, w_out1, "nt", "mix1_out_dx")
    db, dcg, du, d_sc_conv_w = _short_bwd(dy1, z1, sc_conv_w_full, "short_bwd")
    dz1 = jnp.concatenate([db, dcg, du], axis=1)
    g_w_in1 = _matmul(h2, dz1, "tn", "mix1_in_dw", out_dtypes=(BF16,))
    dh = _matmul(dz1, w_in1, "nt", "mix1_in_dx")
    dx, d_mix_pre_1, df0, d_ffn_post_0 = _norms_bwd(dx, dh, x2, gain(mix_pre_g, 1), f0, gain(ffn_post_g, 0), "norms_bwd2")

    g_w2_0 = _matmul(a0sq, df0, "tn", "ffn0_down_dw", out_dtypes=(BF16,))
    dz = _matmul(df0, w2_0, "nt", "ffn0_down_dx", out_dtypes=(BF16,), epilogue=relu_sq_bwd, epi=(a0,))
    g_w1_0 = _matmul(h1, dz, "tn", "ffn0_up_dw", out_dtypes=(BF16,))
    dh = _matmul(dz, w1_0, "nt", "ffn0_up_dx")
    dx, d_ffn_pre_0, dm0, d_mix_post_0 = _norms_bwd(dx, dh, x1, gain(ffn_pre_g, 0), m0, gain(mix_post_g, 0), "norms_bwd1")

    g_w_out0 = _matmul(y0, dm0, "tn", "mix0_out_dw", out_dtypes=(BF16,))
    dy0 = _matmul(dm0, w_out0, "nt", "mix0_out_dx")
    du_pool, d_pool_w, d_pool_scale = _pool_bwd(dy0, pooled, pool_w_full, pool_scale, "pool_bwd")
    dc, d_ln_g, d_ln_b = _ln_silu_bwd(dy0, c_conv, conv_ln_g, conv_ln_b, "ln_silu_bwd")
    dv, dgate, d_conv_w, d_conv_b = _conv_bwd(dc, a_conv, z0, conv_w_full, DP, "conv_bwd")
    dz0 = jnp.concatenate([du_pool, dv, dgate], axis=1)
    g_w_in0 = _matmul(h0, dz0, "tn", "mix0_in_dw", out_dtypes=(BF16,))
    dh = _matmul(dz0, w_in0, "nt", "mix0_in_dx")
    grad_x, d_mix_pre_0 = _norms_bwd(dx, dh, x0, gain(mix_pre_g, 0), None, None, "norms_bwd0")

    loss = lax.psum(loss_row[0, 0], ("x", "y", "c"))

    grads = [g_w_in0, g_w_out0, g_w1_0, g_w2_0, g_w_in1, g_w_out1, g_w1_1, g_w2_1]
    (theirs,) = _run_jobs([_exchange_halves_job(grads, kinds)], "grads_to_sibling")
    chip_sums = [_add_pair(g, t, kinds[u], place, "chip_sum_" + big[u][0]) for u, (g, t) in enumerate(zip(grads, theirs))]
    (arrived,) = _run_jobs([_scatter_job(chip_sums, kinds)], "grads_to_owners")
    reduced = [_sum_chips(p, a, kinds[u], place, "reduce_" + big[u][0]) for u, (p, a) in enumerate(zip(chip_sums, arrived))]
    (g_big,) = _run_jobs([_share_job(reduced)], "grads_share")

    small_grads = [jnp.concatenate([d_mix_pre_0, d_mix_pre_1], 0), jnp.concatenate([d_mix_post_0, d_mix_post_1], 0),
                   jnp.concatenate([d_ffn_pre_0, d_ffn_pre_1], 0), jnp.concatenate([d_ffn_post_0, d_ffn_post_1], 0),
                   d_pool_scale, d_conv_b, d_ln_g, d_ln_b, d_pool_w, d_conv_w, d_sc_conv_w]
    small_shapes = [a.shape for a in small_grads]
    summed = _unpack(_all_reduce_small(_pack(small_grads, D), "small_grads_sum"), small_shapes)
    (g_mix_pre, g_mix_post, g_ffn_pre, g_ffn_post, g_pool_scale, g_conv_b, g_ln_g, g_ln_b, g_pool_w_full, g_conv_w_full,
     g_sc_conv_w_full) = summed
    own = lambda a, ax: lax.dynamic_slice_in_dim(a, chip * (a.shape[ax] // N_CHIPS), a.shape[ax] // N_CHIPS, axis=ax)
    g_pool_w, g_conv_w, g_sc_conv_w = own(g_pool_w_full, 1), own(g_conv_w_full, 1), own(g_sc_conv_w_full, 1)

    def small_update(w, g, m, v, name):
        shape = w.shape
        as3 = lambda a: a.reshape((1, -1, shape[-1]))
        outs = _adamw(as3(w), g.reshape((-1, shape[-1])), as3(m), as3(v), "adamw_" + name)
        return [o.reshape(shape) for o in outs]

    upd = {}
    upd["mix_pre_g"] = small_update(mix_pre_g, g_mix_pre, m_mix_pre_g, v_mix_pre_g, "mix_pre_g")
    upd["mix_post_g"] = small_update(mix_post_g, g_mix_post, m_mix_post_g, v_mix_post_g, "mix_post_g")
    upd["ffn_pre_g"] = small_update(ffn_pre_g, g_ffn_pre, m_ffn_pre_g, v_ffn_pre_g, "ffn_pre_g")
    upd["ffn_post_g"] = small_update(ffn_post_g, g_ffn_post, m_ffn_post_g, v_ffn_post_g, "ffn_post_g")
    upd["pool_w"] = small_update(pool_w, g_pool_w, m_pool_w, v_pool_w, "pool_w")
    upd["pool_scale"] = small_update(pool_scale, g_pool_scale, m_pool_scale, v_pool_scale, "pool_scale")
    upd["conv_w"] = small_update(conv_w, g_conv_w, m_conv_w, v_conv_w, "conv_w")
    upd["conv_b"] = small_update(conv_b, g_conv_b, m_conv_b, v_conv_b, "conv_b")
    upd["conv_ln_g"] = small_update(conv_ln_g, g_ln_g, m_conv_ln_g, v_conv_ln_g, "conv_ln_g")
    upd["conv_ln_b"] = small_update(conv_ln_b, g_ln_b, m_conv_ln_b, v_conv_ln_b, "conv_ln_b")
    upd["sc_conv_w"] = small_update(sc_conv_w, g_sc_conv_w, m_sc_conv_w, v_sc_conv_w, "sc_conv_w")
    upd["ab_w_in"] = _adamw(ab_w_in, g_big[0], m_ab_w_in, v_ab_w_in, "adamw_ab_w_in")
    upd["ab_w_out"] = _adamw(ab_w_out, g_big[1], m_ab_w_out, v_ab_w_out, "adamw_ab_w_out")
    upd["sc_w_in"] = _adamw(sc_w_in, g_big[4], m_sc_w_in, v_sc_w_in, "adamw_sc_w_in")
    upd["sc_w_out"] = _adamw(sc_w_out, g_big[5], m_sc_w_out, v_sc_w_out, "adamw_sc_w_out")
    first = _adamw(ffn_w1, g_big[6], m_ffn_w1, v_ffn_w1, "adamw_ffn_w1_1", layer=1)
    upd["ffn_w1"] = _adamw(ffn_w1, g_big[2], m_ffn_w1, v_ffn_w1, "adamw_ffn_w1_0", layer=0, carried=first)
    first = _adamw(ffn_w2, g_big[7], m_ffn_w2, v_ffn_w2, "adamw_ffn_w2_1", layer=1)
    upd["ffn_w2"] = _adamw(ffn_w2, g_big[3], m_ffn_w2, v_ffn_w2, "adamw_ffn_w2_0", layer=0, carried=first)

    order = ["mix_pre_g", "mix_post_g", "ffn_pre_g", "ffn_post_g", "ab_w_in", "pool_w", "pool_scale", "conv_w", "conv_b",
             "conv_ln_g", "conv_ln_b", "ab_w_out", "sc_w_in", "sc_conv_w", "sc_w_out", "ffn_w1", "ffn_w2"]
    out = [loss, grad_x[None]]
    for part in range(4):
        out += [upd[n][part] for n in order]
    return tuple(out)
```

```python
import jax
import jax.numpy as jnp
from jax import lax
from jax.experimental import pallas as pl
from jax.experimental.pallas import tpu as pltpu

F32, BF16 = jnp.float32, jnp.bfloat16
EPS = 1e-6
N_GROUPS = 4
MAX_WINDOW = 16
CONV_K = 31
SHORT_K = 3
CONV_PAD = 32
SHORT_PAD = 8
ADAM_LR, ADAM_B1, ADAM_B2, ADAM_EPS, ADAM_WD, ADAM_STEP = 0.001, 0.9, 0.999, 1e-08, 0.01, 10
N_CHIPS = 4
VMEM_LIMIT_BYTES = 56 * 1024 * 1024
ROW_TILE = 256
CHUNK = 256
MESH = pl.DeviceIdType.MESH
HBM = pl.BlockSpec(memory_space=pltpu.HBM)


def _cp(*sem):
    return pltpu.CompilerParams(dimension_semantics=sem, vmem_limit_bytes=VMEM_LIMIT_BYTES)


def _sigmoid(v):
    return 1.0 / (1.0 + jnp.exp(-v))


_DIMS = {"nn": (((1,), (0,)), ((), ())), "nt": (((1,), (1,)), ((), ())), "tn": (((0,), (0,)), ((), ()))}


def _pick(n, cap, step=256):
    if n <= cap:
        return n
    return next(t for t in range(cap - cap % step, 0, -step) if n % t == 0)


def _matmul(a, b, mode, name, out_dtypes=(F32,), epilogue=None, epi=(), jobs=(), tm=1024, tn=1024, tk=2048):
    if mode == "tn":
        (K, M), (K2, N) = a.shape, b.shape
    elif mode == "nt":
        (M, K), (N, K2) = a.shape, b.shape
    else:
        (M, K), (K2, N) = a.shape, b.shape
    assert K == K2
    tm, tn, tk = _pick(M, tm), _pick(N, tn), _pick(K, tk)
    nk = K // tk
    a_spec = pl.BlockSpec((tk, tm), lambda i, j, k: (k, i)) if mode == "tn" else pl.BlockSpec((tm, tk), lambda i, j, k: (i, k))
    b_spec = pl.BlockSpec((tn, tk), lambda i, j, k: (j, k)) if mode == "nt" else pl.BlockSpec((tk, tn), lambda i, j, k: (k, j))
    o_spec = pl.BlockSpec((tm, tn), lambda i, j, k: (i, j))
    n_epi, n_out = len(epi), len(out_dtypes)
    beside = _Beside(jobs)
    grid = (M // tm, N // tn, nk)

    def body(a_ref, b_ref, *rest):
        epi_refs, job_ins = rest[:n_epi], rest[n_epi:n_epi + beside.n_in]
        out_refs = rest[n_epi + beside.n_in:n_epi + beside.n_in + n_out]
        job_outs = rest[n_epi + beside.n_in + n_out:n_epi + beside.n_in + n_out + beside.n_out]
        scratch = rest[n_epi + beside.n_in + n_out + beside.n_out:]
        beside.start(job_ins, job_outs, scratch, grid)
        part = lax.dot_general(a_ref[...].astype(BF16), b_ref[...].astype(BF16), _DIMS[mode], preferred_element_type=F32)

        def finish(acc):
            outs = epilogue(acc, *[r[...] for r in epi_refs]) if epilogue else (acc,)
            for o_ref, o in zip(out_refs, outs):
                o_ref[...] = o.astype(o_ref.dtype)

        if nk == 1:
            finish(part)
        else:
            acc_ref = scratch[0]
            k = pl.program_id(2)

            @pl.when(k == 0)
            def _():
                acc_ref[...] = part

            @pl.when(k > 0)
            def _():
                acc_ref[...] += part

            @pl.when(k == nk - 1)
            def _():
                finish(acc_ref[...])

        beside.finish(job_ins, job_outs, scratch, grid)

    outs = pl.pallas_call(
        body, name=name, grid=grid,
        in_specs=[a_spec, b_spec] + [o_spec] * n_epi + [HBM] * beside.n_in,
        out_specs=[o_spec] * n_out + [HBM] * beside.n_out,
        out_shape=[jax.ShapeDtypeStruct((M, N), dt) for dt in out_dtypes] + beside.out_shape,
        input_output_aliases=beside.aliases(2 + n_epi, n_out),
        scratch_shapes=([pltpu.VMEM((tm, tn), F32)] if nk > 1 else []) + beside.scratch(),
        compiler_params=_cp(*(["arbitrary"] * 3 if jobs else ["parallel", "parallel", "arbitrary"])),
    )(a, b, *epi, *beside.ins)
    result = outs[0] if n_out == 1 else outs[:n_out]
    return (result, beside.split(outs[n_out:])) if jobs else result


def _rms(x, g):
    r = lax.rsqrt(jnp.mean(x * x, axis=-1, keepdims=True) + EPS)
    return x * r * g


def _rms_bwd(dy, x, g):
    r = lax.rsqrt(jnp.mean(x * x, axis=-1, keepdims=True) + EPS)
    xn = x * r
    dyg = dy * g
    dx = r * (dyg - xn * jnp.mean(dyg * xn, axis=-1, keepdims=True))
    return dx, jnp.sum(dy * xn, axis=0, keepdims=True)


def _rows(d, tr=ROW_TILE):
    return pl.BlockSpec((tr, d), lambda i: (i, 0))


def _vec(d):
    return pl.BlockSpec((1, d), lambda i: (0, 0))


def _accumulate(ref, val):
    @pl.when(pl.program_id(0) == 0)
    def _():
        ref[...] = val

    @pl.when(pl.program_id(0) > 0)
    def _():
        ref[...] += val


def _norm_fwd(x, g, name):
    T, D = x.shape

    def body(x_ref, g_ref, h_ref):
        h_ref[...] = _rms(x_ref[...], g_ref[...]).astype(BF16)

    return pl.pallas_call(body, name=name, grid=(T // ROW_TILE,), in_specs=[_rows(D), _vec(D)], out_specs=_rows(D),
                          out_shape=jax.ShapeDtypeStruct((T, D), BF16), compiler_params=_cp("parallel"))(x, g)


def _residual_norm(x, m, g_post, g_next, name):
    T, D = x.shape

    def body(x_ref, m_ref, gp_ref, gn_ref, xo_ref, h_ref):
        xo = x_ref[...] + _rms(m_ref[...], gp_ref[...])
        xo_ref[...] = xo
        h_ref[...] = _rms(xo, gn_ref[...]).astype(BF16)

    return pl.pallas_call(body, name=name, grid=(T // ROW_TILE,), in_specs=[_rows(D), _rows(D), _vec(D), _vec(D)],
                          out_specs=[_rows(D), _rows(D)],
                          out_shape=[jax.ShapeDtypeStruct((T, D), F32), jax.ShapeDtypeStruct((T, D), BF16)],
                          compiler_params=_cp("parallel"))(x, m, g_post, g_next)


def _loss_and_last_norm_bwd(x, m, g_post, target, name):
    T, D = x.shape

    def body(x_ref, m_ref, gp_ref, t_ref, dx_ref, dm_ref, dg_ref, loss_ref):
        m_val, gp = m_ref[...], gp_ref[...]
        err = x_ref[...] + _rms(m_val, gp) - t_ref[...]
        dx = err * (1.0 / D)
        dx_ref[...] = dx
        dm, dg = _rms_bwd(dx, m_val, gp)
        dm_ref[...] = dm.astype(BF16)
        _accumulate(dg_ref, dg)
        _accumulate(loss_ref, jnp.full((1, 128), 0.5 * jnp.sum(err * err) * (1.0 / D), F32))

    return pl.pallas_call(body, name=name, grid=(T // ROW_TILE,), in_specs=[_rows(D), _rows(D), _vec(D), _rows(D)],
                          out_specs=[_rows(D), _rows(D), _vec(D), _vec(128)],
                          out_shape=[jax.ShapeDtypeStruct((T, D), F32), jax.ShapeDtypeStruct((T, D), BF16),
                                     jax.ShapeDtypeStruct((1, D), F32), jax.ShapeDtypeStruct((1, 128), F32)],
                          compiler_params=_cp("arbitrary"))(x, m, g_post, target)


def _norms_bwd(dx, dh, x_in, g_pre, m_prev, g_post_prev, name):
    T, D = dx.shape
    with_prev = m_prev is not None

    def body(*refs):
        if with_prev:
            dx_ref, dh_ref, x_ref, gq_ref, m_ref, gp_ref, dxo_ref, dgq_ref, dm_ref, dgp_ref = refs
        else:
            dx_ref, dh_ref, x_ref, gq_ref, dxo_ref, dgq_ref = refs
        d_in, dgq = _rms_bwd(dh_ref[...], x_ref[...], gq_ref[...])
        dxo = dx_ref[...] + d_in
        dxo_ref[...] = dxo
        _accumulate(dgq_ref, dgq)
        if with_prev:
            dm, dgp = _rms_bwd(dxo, m_ref[...], gp_ref[...])
            dm_ref[...] = dm.astype(BF16)
            _accumulate(dgp_ref, dgp)

    ins, in_specs = [dx, dh, x_in, g_pre], [_rows(D), _rows(D), _rows(D), _vec(D)]
    out_specs = [_rows(D), _vec(D)]
    out_shape = [jax.ShapeDtypeStruct((T, D), F32), jax.ShapeDtypeStruct((1, D), F32)]
    if with_prev:
        ins += [m_prev, g_post_prev]
        in_specs += [_rows(D), _vec(D)]
        out_specs += [_rows(D), _vec(D)]
        out_shape += [jax.ShapeDtypeStruct((T, D), BF16), jax.ShapeDtypeStruct((1, D), F32)]
    return pl.pallas_call(body, name=name, grid=(T // ROW_TILE,), in_specs=in_specs, out_specs=out_specs, out_shape=out_shape,
                          compiler_params=_cp("arbitrary"))(*ins)


def _window_weights(g):
    w = 2 << g
    return w, [jnp.where(j < w, 1.0, 0.0).astype(F32) for j in range(MAX_WINDOW)]


def _valid_count(r0, rows, w):
    t = (lax.broadcasted_iota(jnp.int32, (rows, 1), 0) + (r0 + 1)).astype(F32)
    return jnp.minimum(t, w.astype(F32))


def _pool_fwd(z, pool_w, pool_scale, name):
    T = z.shape[0]
    PG = pool_w.shape[-1]
    DP = N_GROUPS * PG
    rc = min(CHUNK, T)

    def body(u_ref, pw_ref, sc_ref, pooled_ref, y_ref, pad):
        w, wts = _window_weights(pl.program_id(0))
        pad[pl.ds(0, MAX_WINDOW), :] = jnp.zeros((MAX_WINDOW, PG), F32)
        pad[pl.ds(MAX_WINDOW, T), :] = u_ref[...]
        for r0 in range(0, T, rc):
            acc = jnp.zeros((rc, PG), F32)
            for j in range(MAX_WINDOW):
                acc = acc + wts[j] * pad[pl.ds(MAX_WINDOW + r0 - j, rc), :]
            pooled = acc / _valid_count(r0, rc, w) - u_ref[pl.ds(r0, rc), :]
            pooled_ref[pl.ds(r0, rc), :] = pooled.astype(BF16)
        mixed = jnp.dot(pooled_ref[...], pw_ref[...].astype(BF16), preferred_element_type=F32)
        y_ref[...] = (mixed * sc_ref[...]).astype(BF16)

    col = lambda g: (0, g)
    return pl.pallas_call(
        body, name=name, grid=(N_GROUPS,),
        in_specs=[pl.BlockSpec((T, PG), col), pl.BlockSpec((None, PG, PG), lambda g: (g, 0, 0)), pl.BlockSpec((1, PG), col)],
        out_specs=[pl.BlockSpec((T, PG), col), pl.BlockSpec((T, PG), col)],
        out_shape=[jax.ShapeDtypeStruct((T, DP), BF16), jax.ShapeDtypeStruct((T, DP), BF16)],
        scratch_shapes=[pltpu.VMEM((T + MAX_WINDOW, PG), F32)], compiler_params=_cp("parallel"))(z, pool_w, pool_scale)


def _pool_bwd(dy, pooled, pool_w, pool_scale, name):
    T = dy.shape[0]
    PG = pool_w.shape[-1]
    DP = N_GROUPS * PG
    rc = min(CHUNK, T)

    def body(dy_ref, pooled_ref, pw_ref, sc_ref, du_ref, dpw_ref, dsc_ref, pad, dp_ref):
        w, wts = _window_weights(pl.program_id(0))
        pooled_v, pw = pooled_ref[...], pw_ref[...].astype(BF16)
        dy_v = dy_ref[...]
        mixed = jnp.dot(pooled_v, pw, preferred_element_type=F32)
        dsc_ref[...] = jnp.sum(dy_v * mixed, axis=0, keepdims=True)
        dmixed = (dy_v * sc_ref[...]).astype(BF16)
        dpw_ref[...] = lax.dot_general(pooled_v, dmixed, _DIMS["tn"], preferred_element_type=F32)
        dp_ref[...] = lax.dot_general(dmixed, pw, _DIMS["nt"], preferred_element_type=F32)
        pad[pl.ds(T, MAX_WINDOW), :] = jnp.zeros((MAX_WINDOW, PG), F32)
        for r0 in range(0, T, rc):
            pad[pl.ds(r0, rc), :] = dp_ref[pl.ds(r0, rc), :] / _valid_count(r0, rc, w)
        for r0 in range(0, T, rc):
            acc = jnp.zeros((rc, PG), F32)
            for j in range(MAX_WINDOW):
                acc = acc + wts[j] * pad[pl.ds(r0 + j, rc), :]
            du_ref[pl.ds(r0, rc), :] = (acc - dp_ref[pl.ds(r0, rc), :]).astype(BF16)

    col = lambda g: (0, g)
    return pl.pallas_call(
        body, name=name, grid=(N_GROUPS,),
        in_specs=[pl.BlockSpec((T, PG), col), pl.BlockSpec((T, PG), col), pl.BlockSpec((None, PG, PG), lambda g: (g, 0, 0)),
                  pl.BlockSpec((1, PG), col)],
        out_specs=[pl.BlockSpec((T, PG), col), pl.BlockSpec((None, PG, PG), lambda g: (g, 0, 0)), pl.BlockSpec((1, PG), col)],
        out_shape=[jax.ShapeDtypeStruct((T, DP), BF16), jax.ShapeDtypeStruct((N_GROUPS, PG, PG), F32),
                   jax.ShapeDtypeStruct((1, DP), F32)],
        scratch_shapes=[pltpu.VMEM((T + MAX_WINDOW, PG), F32), pltpu.VMEM((T, PG), F32)],
        compiler_params=_cp("parallel"))(dy, pooled, pool_w, pool_scale)


def _conv_fwd(z, conv_w, conv_b, d_pool, name, tc=128):
    T = z.shape[0]
    DC = conv_w.shape[-1]
    rc = min(CHUNK, T)
    v0, g0 = d_pool // tc, (d_pool + DC) // tc

    def body(v_ref, gt_ref, w_ref, b_ref, a_ref, c_ref, pad):
        pad[pl.ds(0, CONV_PAD), :] = jnp.zeros((CONV_PAD, tc), F32)
        for r0 in range(0, T, rc):
            a = v_ref[pl.ds(r0, rc), :] * _sigmoid(gt_ref[pl.ds(r0, rc), :])
            a_ref[pl.ds(r0, rc), :] = a
            pad[pl.ds(CONV_PAD + r0, rc), :] = a
        for r0 in range(0, T, rc):
            acc = jnp.zeros((rc, tc), F32) + b_ref[...]
            for k in range(CONV_K):
                acc = acc + w_ref[pl.ds(k, 1), :] * pad[pl.ds(CONV_PAD - (CONV_K - 1) + k + r0, rc), :]
            c_ref[pl.ds(r0, rc), :] = acc

    col = lambda j: (0, j)
    return pl.pallas_call(
        body, name=name, grid=(DC // tc,),
        in_specs=[pl.BlockSpec((T, tc), lambda j: (0, v0 + j)), pl.BlockSpec((T, tc), lambda j: (0, g0 + j)),
                  pl.BlockSpec((CONV_K, tc), col), pl.BlockSpec((1, tc), col)],
        out_specs=[pl.BlockSpec((T, tc), col), pl.BlockSpec((T, tc), col)],
        out_shape=[jax.ShapeDtypeStruct((T, DC), F32), jax.ShapeDtypeStruct((T, DC), F32)],
        scratch_shapes=[pltpu.VMEM((T + CONV_PAD, tc), F32)], compiler_params=_cp("parallel"))(z, z, conv_w, conv_b)


def _conv_bwd(dc, a, z, conv_w, d_pool, name, tc=128):
    T, DC = dc.shape
    rc = min(CHUNK, T)
    v0, g0 = d_pool // tc, (d_pool + DC) // tc

    def body(dc_ref, a_ref, v_ref, gt_ref, w_ref, dv_ref, dg_ref, dw_ref, db_ref, apad, dpad):
        apad[pl.ds(0, CONV_PAD), :] = jnp.zeros((CONV_PAD, tc), F32)
        apad[pl.ds(CONV_PAD, T), :] = a_ref[...]
        dpad[pl.ds(0, T), :] = dc_ref[...]
        dpad[pl.ds(T, CONV_PAD), :] = jnp.zeros((CONV_PAD, tc), F32)
        db_ref[...] = jnp.sum(dc_ref[...], axis=0, keepdims=True)
        for k in range(CONV_K):
            acc = jnp.zeros((8, tc), F32)
            for r0 in range(0, T, rc):
                prod = dc_ref[pl.ds(r0, rc), :] * apad[pl.ds(CONV_PAD - (CONV_K - 1) + k + r0, rc), :]
                acc = acc + jnp.sum(prod.reshape(rc // 8, 8, tc), axis=0)
            dw_ref[pl.ds(k, 1), :] = jnp.sum(acc, axis=0, keepdims=True)
        for r0 in range(0, T, rc):
            da = jnp.zeros((rc, tc), F32)
            for k in range(CONV_K):
                da = da + w_ref[pl.ds(k, 1), :] * dpad[pl.ds(r0 + (CONV_K - 1) - k, rc), :]
            sig = _sigmoid(gt_ref[pl.ds(r0, rc), :])
            dv_ref[pl.ds(r0, rc), :] = (da * sig).astype(BF16)
            dg_ref[pl.ds(r0, rc), :] = (da * v_ref[pl.ds(r0, rc), :] * sig * (1.0 - sig)).astype(BF16)

    col = lambda j: (0, j)
    return pl.pallas_call(
        body, name=name, grid=(DC // tc,),
        in_specs=[pl.BlockSpec((T, tc), col), pl.BlockSpec((T, tc), col), pl.BlockSpec((T, tc), lambda j: (0, v0 + j)),
                  pl.BlockSpec((T, tc), lambda j: (0, g0 + j)), pl.BlockSpec((CONV_K, tc), col)],
        out_specs=[pl.BlockSpec((T, tc), col), pl.BlockSpec((T, tc), col), pl.BlockSpec((CONV_K, tc), col),
                   pl.BlockSpec((1, tc), col)],
        out_shape=[jax.ShapeDtypeStruct((T, DC), BF16), jax.ShapeDtypeStruct((T, DC), BF16),
                   jax.ShapeDtypeStruct((CONV_K, DC), F32), jax.ShapeDtypeStruct((1, DC), F32)],
        scratch_shapes=[pltpu.VMEM((T + CONV_PAD, tc), F32), pltpu.VMEM((T + CONV_PAD, tc), F32)],
        compiler_params=_cp("parallel"))(dc, a, z, z, conv_w)


def _layer_norm_parts(c, g, b):
    mu = jnp.mean(c, axis=-1, keepdims=True)
    xc = c - mu
    rstd = lax.rsqrt(jnp.mean(xc * xc, axis=-1, keepdims=True) + EPS)
    xhat = xc * rstd
    return xhat, rstd, xhat * g + b


def _ln_silu_fwd(c, g, b, name):
    T, DC = c.shape

    def body(c_ref, g_ref, b_ref, y_ref):
        _, _, ln = _layer_norm_parts(c_ref[...], g_ref[...], b_ref[...])
        y_ref[...] = (ln * _sigmoid(ln)).astype(BF16)

    return pl.pallas_call(body, name=name, grid=(T // ROW_TILE,), in_specs=[_rows(DC), _vec(DC), _vec(DC)], out_specs=_rows(DC),
                          out_shape=jax.ShapeDtypeStruct((T, DC), BF16), compiler_params=_cp("parallel"))(c, g, b)


def _ln_silu_bwd(dy, c, g, b, name):
    T, DC = c.shape

    def body(dy_ref, c_ref, g_ref, b_ref, dc_ref, dg_ref, db_ref):
        gain = g_ref[...]
        xhat, rstd, ln = _layer_norm_parts(c_ref[...], gain, b_ref[...])
        s = _sigmoid(ln)
        dln = dy_ref[...] * (s * (1.0 + ln * (1.0 - s)))
        _accumulate(dg_ref, jnp.sum(dln * xhat, axis=0, keepdims=True))
        _accumulate(db_ref, jnp.sum(dln, axis=0, keepdims=True))
        dxh = dln * gain
        dc_ref[...] = rstd * (dxh - jnp.mean(dxh, axis=-1, keepdims=True) - xhat * jnp.mean(dxh * xhat, axis=-1, keepdims=True))

    return pl.pallas_call(body, name=name, grid=(T // ROW_TILE,),
                          in_specs=[pl.BlockSpec((ROW_TILE, DC), lambda i: (i, 1)), _rows(DC), _vec(DC), _vec(DC)],
                          out_specs=[_rows(DC), _vec(DC), _vec(DC)],
                          out_shape=[jax.ShapeDtypeStruct((T, DC), F32), jax.ShapeDtypeStruct((1, DC), F32),
                                     jax.ShapeDtypeStruct((1, DC), F32)],
                          compiler_params=_cp("arbitrary"))(dy, c, g, b)


def _short_specs(T, DS, tc):
    n = DS // tc
    return [pl.BlockSpec((T, tc), lambda j: (0, j)), pl.BlockSpec((T, tc), lambda j: (0, n + j)),
            pl.BlockSpec((T, tc), lambda j: (0, 2 * n + j))]


def _short_fwd(z, w, name, tc=256):
    T = z.shape[0]
    DS = w.shape[-1]
    rc = min(CHUNK, T)

    def body(b_ref, cg_ref, u_ref, w_ref, y_ref, pad):
        pad[pl.ds(0, SHORT_PAD), :] = jnp.zeros((SHORT_PAD, tc), F32)
        pad[pl.ds(SHORT_PAD, T), :] = cg_ref[...] * u_ref[...]
        for r0 in range(0, T, rc):
            r = jnp.zeros((rc, tc), F32)
            for k in range(SHORT_K):
                r = r + w_ref[pl.ds(k, 1), :] * pad[pl.ds(SHORT_PAD - (SHORT_K - 1) + k + r0, rc), :]
            y_ref[pl.ds(r0, rc), :] = (b_ref[pl.ds(r0, rc), :] * r).astype(BF16)

    col = lambda j: (0, j)
    return pl.pallas_call(body, name=name, grid=(DS // tc,), in_specs=_short_specs(T, DS, tc) + [pl.BlockSpec((SHORT_K, tc), col)],
                          out_specs=pl.BlockSpec((T, tc), col), out_shape=jax.ShapeDtypeStruct((T, DS), BF16),
                          scratch_shapes=[pltpu.VMEM((T + SHORT_PAD, tc), F32)], compiler_params=_cp("parallel"))(z, z, z, w)


def _short_bwd(dy, z, w, name, tc=256):
    T, DS = dy.shape
    rc = min(CHUNK, T)

    def body(dy_ref, b_ref, cg_ref, u_ref, w_ref, db_ref, dcg_ref, du_ref, dw_ref, qpad, rpad):
        qpad[pl.ds(0, SHORT_PAD), :] = jnp.zeros((SHORT_PAD, tc), F32)
        qpad[pl.ds(SHORT_PAD, T), :] = cg_ref[...] * u_ref[...]
        rpad[pl.ds(0, T), :] = dy_ref[...] * b_ref[...]
        rpad[pl.ds(T, SHORT_PAD), :] = jnp.zeros((SHORT_PAD, tc), F32)
        accs = [jnp.zeros((8, tc), F32) for _ in range(SHORT_K)]
        for r0 in range(0, T, rc):
            r = jnp.zeros((rc, tc), F32)
            dq = jnp.zeros((rc, tc), F32)
            dr = rpad[pl.ds(r0, rc), :]
            for k in range(SHORT_K):
                q_k = qpad[pl.ds(SHORT_PAD - (SHORT_K - 1) + k + r0, rc), :]
                r = r + w_ref[pl.ds(k, 1), :] * q_k
                dq = dq + w_ref[pl.ds(k, 1), :] * rpad[pl.ds(r0 + (SHORT_K - 1) - k, rc), :]
                accs[k] = accs[k] + jnp.sum((dr * q_k).reshape(rc // 8, 8, tc), axis=0)
            db_ref[pl.ds(r0, rc), :] = (dy_ref[pl.ds(r0, rc), :] * r).astype(BF16)
            dcg_ref[pl.ds(r0, rc), :] = (dq * u_ref[pl.ds(r0, rc), :]).astype(BF16)
            du_ref[pl.ds(r0, rc), :] = (dq * cg_ref[pl.ds(r0, rc), :]).astype(BF16)
        for k in range(SHORT_K):
            dw_ref[pl.ds(k, 1), :] = jnp.sum(accs[k], axis=0, keepdims=True)

    col = lambda j: (0, j)
    tile = pl.BlockSpec((T, tc), col)
    return pl.pallas_call(body, name=name, grid=(DS // tc,),
                          in_specs=[tile] + _short_specs(T, DS, tc) + [pl.BlockSpec((SHORT_K, tc), col)],
                          out_specs=[tile, tile, tile, pl.BlockSpec((SHORT_K, tc), col)],
                          out_shape=[jax.ShapeDtypeStruct((T, DS), BF16)] * 3 + [jax.ShapeDtypeStruct((SHORT_K, DS), F32)],
                          scratch_shapes=[pltpu.VMEM((T + SHORT_PAD, tc), F32), pltpu.VMEM((T + SHORT_PAD, tc), F32)],
                          compiler_params=_cp("parallel"))(dy, z, z, z, w)


def _tile_rows(rows, cols, n_bufs):
    budget = VMEM_LIMIT_BYTES // 2 // (2 * n_bufs * 4 * cols)
    tr = rows
    while tr > budget and tr % 16 == 0:
        tr //= 2
    return tr


def _placed_call(body, name, place, grid, in_specs, out_specs, out_shape, ins):
    grid_spec = pltpu.PrefetchScalarGridSpec(num_scalar_prefetch=1, grid=grid, in_specs=in_specs, out_specs=out_specs)
    return pl.pallas_call(body, name=name, grid_spec=grid_spec, out_shape=out_shape, compiler_params=_cp("parallel"))(place, *ins)


def _cast_into_full(w, layer, kind, place, name):
    _, R, C = w.shape
    tr = _tile_rows(R, C, 2)
    nb = R // tr
    if kind == "col":
        full, out_spec = (R, C * N_CHIPS), pl.BlockSpec((tr, C), lambda i, s: (i, s[0]))
    else:
        full, out_spec = (R * N_CHIPS, C), pl.BlockSpec((tr, C), lambda i, s: (s[0] * nb + i, 0))

    def body(s_ref, w_ref, o_ref):
        o_ref[...] = w_ref[...].astype(BF16)

    return _placed_call(body, name, place, (nb,), [pl.BlockSpec((None, tr, C), lambda i, s: (layer, i, 0))], out_spec,
                        jax.ShapeDtypeStruct(full, BF16), [w])


def _add_pair(grad, theirs, kind, place, name):
    R, C = grad.shape
    piece_rows = R // 2 if kind == "col" else R // N_CHIPS // 2
    tr = _tile_rows(piece_rows, C, 3)
    nb = piece_rows // tr
    if kind == "col":
        g_spec = pl.BlockSpec((tr, C), lambda i, s: (s[1] * nb + i, 0))
    else:
        g_spec = pl.BlockSpec((tr, C), lambda i, s: ((2 * (i // nb) + s[1]) * nb + i % nb, 0))
    flat = pl.BlockSpec((tr, C), lambda i, s: (i, 0))

    def body(s_ref, a_ref, b_ref, o_ref):
        o_ref[...] = (a_ref[...].astype(F32) + b_ref[...].astype(F32)).astype(BF16)

    return _placed_call(body, name, place, (R // 2 // tr,), [g_spec, flat], flat, jax.ShapeDtypeStruct((R // 2, C), BF16),
                        [grad, theirs])


def _sum_chips(chip_sum, arrived, kind, place, name):
    _, H, W = arrived.shape
    tr = _tile_rows(H, W, 6)
    nb = H // tr
    if kind == "col":
        own_spec = pl.BlockSpec((tr, W), lambda i, s: (i, s[0]))
    else:
        own_spec = pl.BlockSpec((tr, W), lambda i, s: (s[0] * nb + i, 0))

    def body(s_ref, p_ref, r_ref, o_ref):
        acc = p_ref[...].astype(F32)
        for i in range(N_CHIPS - 1):
            acc = acc + r_ref[i].astype(F32)
        o_ref[...] = acc

    return _placed_call(body, name, place, (nb,), [own_spec, pl.BlockSpec((N_CHIPS - 1, tr, W), lambda i, s: (0, i, 0))],
                        pl.BlockSpec((tr, W), lambda i, s: (s[1] * nb + i, 0)), jax.ShapeDtypeStruct((2 * H, W), F32),
                        [chip_sum, arrived])


def _adamw_values(w, g, m, v):
    m = ADAM_B1 * m + (1.0 - ADAM_B1) * g
    v = ADAM_B2 * v + (1.0 - ADAM_B2) * (g * g)
    m_hat = m / (1.0 - ADAM_B1 ** ADAM_STEP)
    v_hat = v / (1.0 - ADAM_B2 ** ADAM_STEP)
    return -ADAM_LR * (m_hat / (jnp.sqrt(v_hat) + ADAM_EPS) + ADAM_WD * w), m, v


def _adamw(w, g, m, v, name, layer=0, carried=None, jobs=()):
    L, R, C = w.shape
    tr = _tile_rows(R, C, 8)
    beside = _Beside(jobs)
    n_carried = 0 if carried is None else 4
    grid = (R // tr,)

    def body(w_ref, g_ref, m_ref, v_ref, *rest):
        job_ins = rest[n_carried:n_carried + beside.n_in]
        go_ref, d_ref, mo_ref, vo_ref = rest[n_carried + beside.n_in:n_carried + beside.n_in + 4]
        job_outs = rest[n_carried + beside.n_in + 4:n_carried + beside.n_in + 4 + beside.n_out]
        scratch = rest[n_carried + beside.n_in + 4 + beside.n_out:]
        beside.start(job_ins, job_outs, scratch, grid)
        g_val = g_ref[...]
        d, m_new, v_new = _adamw_values(w_ref[...], g_val, m_ref[...], v_ref[...])
        go_ref[...], d_ref[...], mo_ref[...], vo_ref[...] = g_val, d, m_new, v_new
        beside.finish(job_ins, job_outs, scratch, grid)

    lay = pl.BlockSpec((None, tr, C), lambda i: (layer, i, 0))
    ins = [w, g, m, v]
    in_specs = [lay, pl.BlockSpec((tr, C), lambda i: (i, 0)), lay, lay]
    aliases = {}
    if carried is not None:
        ins += list(carried)
        in_specs += [pl.BlockSpec(memory_space=pl.ANY)] * 4
        aliases = {4 + i: i for i in range(4)}
    aliases.update(beside.aliases(4 + n_carried, 4))
    outs = pl.pallas_call(body, name=name, grid=grid, in_specs=in_specs + [HBM] * beside.n_in,
                          out_specs=[lay] * 4 + [HBM] * beside.n_out,
                          out_shape=[jax.ShapeDtypeStruct((L, R, C), F32)] * 4 + beside.out_shape, input_output_aliases=aliases,
                          scratch_shapes=beside.scratch(),
                          compiler_params=_cp("arbitrary" if jobs else "parallel"))(*ins, *beside.ins)
    return (outs[:4], beside.split(outs[4:])) if jobs else outs


def _aligned(v, m):
    return v if isinstance(v, int) else pl.multiple_of(v, m)


def _place():
    x, y, c = lax.axis_index("x"), lax.axis_index("y"), lax.axis_index("c")
    other_chips = [(x, 1 - y), (1 - x, y), (1 - x, 1 - y)]
    return x, y, c, 2 * x + y, other_chips


def _chip_index(chip):
    return 2 * chip[0] + chip[1]


def _piece(ref, kind, k, h):
    R, C = ref.shape
    if kind == "col":
        return ref.at[pl.ds(_aligned(h * (R // 2), 16), R // 2), pl.ds(_aligned(k * (C // N_CHIPS), 128), C // N_CHIPS)]
    rs = R // N_CHIPS
    return ref.at[pl.ds(_aligned(k * rs + h * (rs // 2), 16), rs // 2), :]


def _compact_piece(ref, kind, k):
    R2, C = ref.shape
    if kind == "col":
        return ref.at[:, pl.ds(_aligned(k * (C // N_CHIPS), 128), C // N_CHIPS)]
    return ref.at[pl.ds(_aligned(k * (R2 // N_CHIPS), 16), R2 // N_CHIPS), :]


def _half_rows(ref, h):
    R = ref.shape[0]
    return ref.at[pl.ds(_aligned(h * (R // 2), 16), R // 2), :]


class _Copies:
    def __init__(self, send_sems, recv_sems):
        self.send_sems, self.recv_sems = send_sems, recv_sems
        self.n_remote = 0

    def remote(self, src, dst, device):
        k = self.n_remote
        self.n_remote += 1
        return pltpu.make_async_remote_copy(src_ref=src, dst_ref=dst, send_sem=self.send_sems.at[k], recv_sem=self.recv_sems.at[k],
                                            device_id=device, device_id_type=MESH)


class _Job:
    def __init__(self, ins, out_shape, aliases, n_remote, build):
        self.ins, self.out_shape, self.aliases, self.n_remote, self.build = list(ins), list(out_shape), dict(aliases), n_remote, build


def _merge_jobs(jobs):
    ins, out_shape, aliases, spans = [], [], {}, []
    for job in jobs:
        spans.append((len(ins), len(job.ins), len(out_shape), len(job.out_shape)))
        aliases.update({len(ins) + i: len(out_shape) + o for i, o in job.aliases.items()})
        ins += job.ins
        out_shape += job.out_shape

    def build(in_refs, out_refs, cp):
        copies = []
        for job, (i0, ni, o0, no) in zip(jobs, spans):
            copies += job.build(in_refs[i0:i0 + ni], out_refs[o0:o0 + no], cp)
        return copies

    return _Job(ins, out_shape, aliases, sum(j.n_remote for j in jobs), build)


def _run_jobs(jobs, name):
    job = _merge_jobs(jobs)
    n_in, n_out = len(job.ins), len(job.out_shape)

    def body(*refs):
        copies = job.build(refs[:n_in], refs[n_in:n_in + n_out], _Copies(*refs[n_in + n_out:]))
        for d in copies:
            d.start()
        for d in copies:
            d.wait_recv()
        for d in copies:
            d.wait_send()

    outs = pl.pallas_call(
        body, name=name, in_specs=[HBM] * n_in, out_specs=[HBM] * n_out, out_shape=job.out_shape,
        input_output_aliases=job.aliases,
        scratch_shapes=[pltpu.SemaphoreType.DMA((job.n_remote,)), pltpu.SemaphoreType.DMA((job.n_remote,))])(*job.ins)
    split, o0 = [], 0
    for j in jobs:
        split.append(outs[o0:o0 + len(j.out_shape)])
        o0 += len(j.out_shape)
    return split


class _Beside:
    def __init__(self, jobs):
        self.jobs = list(jobs)
        job = self.job = _merge_jobs(self.jobs) if self.jobs else None
        self.ins = job.ins if job else []
        self.out_shape = job.out_shape if job else []
        self.n_in, self.n_out = len(self.ins), len(self.out_shape)

    def scratch(self):
        n = self.job.n_remote if self.job else 0
        return [pltpu.SemaphoreType.DMA((n,)), pltpu.SemaphoreType.DMA((n,))] if n else []

    def aliases(self, first_in, first_out):
        return {first_in + i: first_out + o for i, o in self.job.aliases.items()} if self.job else {}

    def _at(self, grid, which):
        cond = None
        for axis, n in enumerate(grid):
            here = pl.program_id(axis) == (0 if which == "first" else n - 1)
            cond = here if cond is None else jnp.logical_and(cond, here)
        return cond

    def start(self, in_refs, out_refs, scratch, grid):
        if self.job:
            @pl.when(self._at(grid, "first"))
            def _():
                for d in self.job.build(in_refs, out_refs, _Copies(*scratch[-2:])):
                    d.start()

    def finish(self, in_refs, out_refs, scratch, grid):
        if self.job:
            @pl.when(self._at(grid, "last"))
            def _():
                copies = self.job.build(in_refs, out_refs, _Copies(*scratch[-2:]))
                for d in copies:
                    d.wait_recv()
                for d in copies:
                    d.wait_send()

    def split(self, outs):
        per_job, o0 = [], 0
        for j in self.jobs:
            per_job.append(list(outs[o0:o0 + len(j.out_shape)]))
            o0 += len(j.out_shape)
        return per_job


def _in_place(arrays):
    return [jax.ShapeDtypeStruct(a.shape, a.dtype) for a in arrays], {u: u for u in range(len(arrays))}


def _gather_ici_job(fulls, kinds, part=0, n_parts=1):
    def build(in_refs, out_refs, cp):
        x, y, c, me, chips = _place()
        copies = []
        for ref, kind in zip(out_refs, kinds):
            mine = _piece(ref, kind, me, c)
            h = mine.shape[0] // n_parts
            mine = mine.at[pl.ds(part * h, h), :]
            copies += [cp.remote(mine, mine, (*chip, c)) for chip in chips]
        return copies

    return _Job(fulls, *_in_place(fulls), 3 * len(fulls), build)


def _gather_forward_job(fulls, kinds):
    def build(in_refs, out_refs, cp):
        x, y, c, me, chips = _place()
        copies = []
        for ref, kind in zip(out_refs, kinds):
            for chip in chips:
                piece = _piece(ref, kind, _chip_index(chip), c)
                copies.append(cp.remote(piece, piece, (x, y, 1 - c)))
        return copies

    return _Job(fulls, *_in_place(fulls), 3 * len(fulls), build)


def _gather_small_job(fulls, axes):
    def build(in_refs, out_refs, cp):
        x, y, c, me, chips = _place()
        copies = []
        for ref, ax in zip(out_refs, axes):
            n = ref.shape[ax] // N_CHIPS
            idx = [slice(None)] * len(ref.shape)
            idx[ax] = pl.ds(_aligned(me * n, n), n)
            mine = ref.at[tuple(idx)]
            copies += [cp.remote(mine, mine, (*chip, c)) for chip in chips]
        return copies

    return _Job(fulls, *_in_place(fulls), 3 * len(fulls), build)


def _exchange_halves_job(grads, kinds):
    def build(in_refs, out_refs, cp):
        x, y, c, me, chips = _place()
        copies = []
        for src, dst, kind in zip(in_refs, out_refs, kinds):
            if kind == "col":
                copies.append(cp.remote(_half_rows(src, 1 - c), dst, (x, y, 1 - c)))
            else:
                copies += [cp.remote(_piece(src, "row", k, 1 - c), _compact_piece(dst, "row", k), (x, y, 1 - c))
                           for k in range(N_CHIPS)]
        return copies

    out_shape = [jax.ShapeDtypeStruct((g.shape[0] // 2, g.shape[1]), g.dtype) for g in grads]
    return _Job(grads, out_shape, {}, sum(1 if k == "col" else N_CHIPS for k in kinds), build)


def _scatter_job(halves, kinds, part=0, n_parts=1, into=None):
    n = len(halves)

    def rows(ref):
        h = ref.shape[0] // n_parts
        return ref.at[pl.ds(part * h, h), :]

    def build(in_refs, out_refs, cp):
        x, y, c, me, chips = _place()
        copies = []
        for src, dst, kind in zip(in_refs[:n], out_refs, kinds):
            copies += [cp.remote(rows(_compact_piece(src, kind, _chip_index(chip))), rows(dst.at[r]), (*chip, c))
                       for r, chip in enumerate(chips)]
        return copies

    def part_shape(h, kind):
        return (h.shape[0], h.shape[1] // N_CHIPS) if kind == "col" else (h.shape[0] // N_CHIPS, h.shape[1])

    out_shape = [jax.ShapeDtypeStruct((N_CHIPS - 1,) + part_shape(h, k), h.dtype) for h, k in zip(halves, kinds)]
    ins, aliases = list(halves), {}
    if into is not None:
        ins, aliases = ins + list(into), {n + u: u for u in range(n)}
    return _Job(ins, out_shape, aliases, 3 * n, build)


def _share_job(shards):
    def build(in_refs, out_refs, cp):
        x, y, c, me, chips = _place()
        copies = []
        for ref in out_refs:
            mine = _half_rows(ref, c)
            copies.append(cp.remote(mine, mine, (x, y, 1 - c)))
        return copies

    return _Job(shards, *_in_place(shards), len(shards), build)


def _all_reduce_small(packed, name):
    R, C = packed.shape
    n_dev = 2 * N_CHIPS

    def body(in_ref, out_ref, slots, send_sems, recv_sems):
        x, y, c, me, chips = _place()
        my_slot = 2 * me + c
        copies = []
        for p in range(1, n_dev):
            peer = (x ^ (p >> 2), y ^ ((p >> 1) & 1), c ^ (p & 1))
            copies.append(pltpu.make_async_remote_copy(src_ref=in_ref, dst_ref=slots.at[my_slot], send_sem=send_sems.at[p - 1],
                                                       recv_sem=recv_sems.at[p - 1], device_id=peer, device_id_type=MESH))
        for d in copies:
            d.start()
        slots[my_slot] = in_ref[...]
        for d in copies:
            d.wait_recv()
        acc = slots[0]
        for i in range(1, n_dev):
            acc = acc + slots[i]
        out_ref[...] = acc
        for d in copies:
            d.wait_send()

    return pl.pallas_call(
        body, name=name, in_specs=[pl.BlockSpec(memory_space=pltpu.VMEM)], out_specs=pl.BlockSpec(memory_space=pltpu.VMEM),
        out_shape=jax.ShapeDtypeStruct((R, C), F32),
        scratch_shapes=[pltpu.VMEM((n_dev, R, C), F32), pltpu.SemaphoreType.DMA((n_dev - 1,)), pltpu.SemaphoreType.DMA((n_dev - 1,))],
        compiler_params=pltpu.CompilerParams(vmem_limit_bytes=VMEM_LIMIT_BYTES))(packed)


def _pack(arrays, width):
    rows = []
    for a in arrays:
        flat = a.reshape(-1)
        n_rows = -(-flat.shape[0] // width)
        rows.append(jnp.pad(flat, (0, n_rows * width - flat.shape[0])).reshape(n_rows, width))
    n = sum(r.shape[0] for r in rows)
    rows.append(jnp.zeros((-n % 8, width), F32))
    return jnp.concatenate(rows, axis=0)


def _unpack(packed, shapes):
    out, r0, width = [], 0, packed.shape[1]
    for shape in shapes:
        size = 1
        for d in shape:
            size *= d
        n_rows = -(-size // width)
        out.append(packed[r0:r0 + n_rows].reshape(-1)[:size].reshape(shape))
        r0 += n_rows
    return out


def kernel(x, mix_pre_g, mix_post_g, ffn_pre_g, ffn_post_g, ab_w_in, pool_w, pool_scale, conv_w, conv_b, conv_ln_g, conv_ln_b, ab_w_out, sc_w_in, sc_conv_w, sc_w_out, ffn_w1, ffn_w2, loss_target, m_mix_pre_g, m_mix_post_g, m_ffn_pre_g, m_ffn_post_g, m_ab_w_in, m_pool_w, m_pool_scale, m_conv_w, m_conv_b, m_conv_ln_g, m_conv_ln_b, m_ab_w_out, m_sc_w_in, m_sc_conv_w, m_sc_w_out, m_ffn_w1, m_ffn_w2, v_mix_pre_g, v_mix_post_g, v_ffn_pre_g, v_ffn_post_g, v_ab_w_in, v_pool_w, v_pool_scale, v_conv_w, v_conv_b, v_conv_ln_g, v_conv_ln_b, v_ab_w_out, v_sc_w_in, v_sc_conv_w, v_sc_w_out, v_ffn_w1, v_ffn_w2):
    x0, target = x[0], loss_target[0]
    T, D = x0.shape
    DP = pool_scale.shape[-1]
    gain = lambda g, layer: g[layer][None, :]

    big = [("ab_w_in", ab_w_in, 0, "col"), ("ab_w_out", ab_w_out, 0, "row"), ("ffn_w1_0", ffn_w1, 0, "col"),
           ("ffn_w2_0", ffn_w2, 0, "row"), ("sc_w_in", sc_w_in, 0, "col"), ("sc_w_out", sc_w_out, 0, "row"),
           ("ffn_w1_1", ffn_w1, 1, "col"), ("ffn_w2_1", ffn_w2, 1, "row")]
    kinds = [b[3] for b in big]
    chip = 2 * lax.axis_index("x") + lax.axis_index("y")
    place = jnp.stack([chip, lax.axis_index("c")]).astype(jnp.int32)
    W = [_cast_into_full(w, layer, kind, place, "cast_" + name) for name, w, layer, kind in big]

    def own_in_zeros(shard, ax):
        full = jnp.zeros(tuple(d * N_CHIPS if i == ax else d for i, d in enumerate(shard.shape)), shard.dtype)
        return lax.dynamic_update_slice_in_dim(full, shard, chip * shard.shape[ax], axis=ax)

    smalls = [own_in_zeros(pool_w[0], 1), own_in_zeros(conv_w[0], 1), own_in_zeros(sc_conv_w[0], 1)]
    relu_sq = lambda acc: (jnp.maximum(acc, 0.0), jnp.square(jnp.maximum(acc, 0.0)))
    relu_sq_bwd = lambda acc, a: (acc * (2.0 * a.astype(F32)),)

    I = lambda u, part=0, n_parts=1: _gather_ici_job([W[u]], [kinds[u]], part, n_parts)
    Fw = lambda u: _gather_forward_job([W[u]], [kinds[u]])
    (W[0],), smalls = _run_jobs([I(0), _gather_small_job(smalls, [1, 1, 1])], "gather_a")
    pool_w_full, conv_w_full, sc_conv_w_full = smalls
    h0 = _norm_fwd(x0, gain(mix_pre_g, 0), "norm_in")
    (W[0],), (W[1],) = _run_jobs([Fw(0), I(1)], "gather_b")
    z0, ((W[1],), (W[2],)) = _matmul(h0, W[0], "nn", "mix0_in", jobs=[Fw(1), I(2, 0, 2)])
    pooled, y_pool = _pool_fwd(z0, pool_w_full, pool_scale, "pool_fwd")
    a_conv, c_conv = _conv_fwd(z0, conv_w_full, conv_b, DP, "conv_fwd")
    y_conv = _ln_silu_fwd(c_conv, conv_ln_g, conv_ln_b, "ln_silu_fwd")
    y0 = jnp.concatenate([y_pool, y_conv], axis=1)
    m0, ((W[2],),) = _matmul(y0, W[1], "nn", "mix0_out", jobs=[I(2, 1, 2)])
    x1, h1 = _residual_norm(x0, m0, gain(mix_post_g, 0), gain(ffn_pre_g, 0), "res_mix0")
    (W[2],), (W[3],) = _run_jobs([Fw(2), I(3, 0, 2)], "gather_c")
    (a0, a0sq), ((W[3],),) = _matmul(h1, W[2], "nn", "ffn0_up", out_dtypes=(BF16, BF16), epilogue=relu_sq, jobs=[I(3, 1, 2)])
    (W[3],), (W[4],) = _run_jobs([Fw(3), I(4, 0, 2)], "gather_d")
    f0, ((W[4],), (W[5],)) = _matmul(a0sq, W[3], "nn", "ffn0_down", jobs=[I(4, 1, 2), I(5)])
    x2, h2 = _residual_norm(x1, f0, gain(ffn_post_g, 0), gain(mix_pre_g, 1), "res_ffn0")
    (W[4],), (W[5],), (W[6],) = _run_jobs([Fw(4), Fw(5), I(6, 0, 2)], "gather_e")
    z1, ((W[6],),) = _matmul(h2, W[4], "nn", "mix1_in", jobs=[I(6, 1, 2)])
    y1 = _short_fwd(z1, sc_conv_w_full, "short_fwd")
    m1, ((W[6],), (W[7],)) = _matmul(y1, W[5], "nn", "mix1_out", jobs=[Fw(6), I(7, 0, 2)])
    x3, h3 = _residual_norm(x2, m1, gain(mix_post_g, 1), gain(ffn_pre_g, 1), "res_mix1")
    (a1, a1sq), ((W[7],),) = _matmul(h3, W[6], "nn", "ffn1_up", out_dtypes=(BF16, BF16), epilogue=relu_sq, jobs=[I(7, 1, 2)])
    ((W[7],),) = _run_jobs([Fw(7)], "gather_f")
    f1 = _matmul(a1sq, W[7], "nn", "ffn1_down")
    w_in0, w_out0, w1_0, w2_0, w_in1, w_out1, w1_1, w2_1 = W

    dx, df1, d_ffn_post_1, loss_row = _loss_and_last_norm_bwd(x3, f1, gain(ffn_post_g, 1), target, "loss")
    chip_sum = lambda u, g, t: _add_pair(g, t, kinds[u], place, "chip_sum_" + big[u][0])
    reduce = lambda u, p, arrived: _sum_chips(p, arrived, kinds[u], place, "reduce_" + big[u][0])
    X = lambda u, g: _exchange_halves_job([g], [kinds[u]])
    S = lambda u, p, part=0, n_parts=1, into=None: _scatter_job([p], [kinds[u]], part, n_parts, into)
    H = lambda q: _share_job([q])

    g7 = _matmul(a1sq, df1, "tn", "ffn1_down_dw", out_dtypes=(BF16,))
    dz, ((t7,),) = _matmul(df1, w2_1, "nt", "ffn1_down_dx", out_dtypes=(BF16,), epilogue=relu_sq_bwd, epi=(a1,), jobs=[X(7, g7)])
    p7 = chip_sum(7, g7, t7)
    g6, (arr7,) = _matmul(h3, dz, "tn", "ffn1_up_dw", out_dtypes=(BF16,), jobs=[S(7, p7, 0, 2)])
    dh, (arr7, (t6,)) = _matmul(dz, w1_1, "nt", "ffn1_up_dx", jobs=[S(7, p7, 1, 2, arr7), X(6, g6)])
    p6, q7 = chip_sum(6, g6, t6), reduce(7, p7, arr7[0])
    dx, d_ffn_pre_1, dm1, d_mix_post_1 = _norms_bwd(dx, dh, x3, gain(ffn_pre_g, 1), m1, gain(mix_post_g, 1), "norms_bwd3")

    g5, ((gr7,),) = _matmul(y1, dm1, "tn", "mix1_out_dw", out_dtypes=(BF16,), jobs=[H(q7)])
    dy1, (arr6,) = _matmul(dm1, w_out1, "nt", "mix1_out_dx", jobs=[S(6, p6, 0, 2)])
    db, dcg, du, d_sc_conv_w = _short_bwd(dy1, z1, sc_conv_w_full, "short_bwd")
    dz1 = jnp.concatenate([db, dcg, du], axis=1)
    g4, (arr6, (t5,)) = _matmul(h2, dz1, "tn", "mix1_in_dw", out_dtypes=(BF16,), jobs=[S(6, p6, 1, 2, arr6), X(5, g5)])
    p5, q6 = chip_sum(5, g5, t5), reduce(6, p6, arr6[0])
    dh, (arr5, (gr6,), (t4,)) = _matmul(dz1, w_in1, "nt", "mix1_in_dx", jobs=[S(5, p5), H(q6), X(4, g4)])
    p4, q5 = chip_sum(4, g4, t4), reduce(5, p5, arr5[0])
    dx, d_mix_pre_1, df0, d_ffn_post_0 = _norms_bwd(dx, dh, x2, gain(mix_pre_g, 1), f0, gain(ffn_post_g, 0), "norms_bwd2")

    g3, (arr4, (gr5,)) = _matmul(a0sq, df0, "tn", "ffn0_down_dw", out_dtypes=(BF16,), jobs=[S(4, p4, 0, 2), H(q5)])
    dz, (arr4, (t3,)) = _matmul(df0, w2_0, "nt", "ffn0_down_dx", out_dtypes=(BF16,), epilogue=relu_sq_bwd, epi=(a0,),
                                jobs=[S(4, p4, 1, 2, arr4), X(3, g3)])
    p3, q4 = chip_sum(3, g3, t3), reduce(4, p4, arr4[0])
    g2, (arr3, (gr4,)) = _matmul(h1, dz, "tn", "ffn0_up_dw", out_dtypes=(BF16,), jobs=[S(3, p3, 0, 2), H(q4)])
    dh, (arr3, (t2,)) = _matmul(dz, w1_0, "nt", "ffn0_up_dx", jobs=[S(3, p3, 1, 2, arr3), X(2, g2)])
    p2, q3 = chip_sum(2, g2, t2), reduce(3, p3, arr3[0])
    dx, d_ffn_pre_0, dm0, d_mix_post_0 = _norms_bwd(dx, dh, x1, gain(ffn_pre_g, 0), m0, gain(mix_post_g, 0), "norms_bwd1")

    g1, ((gr3,),) = _matmul(y0, dm0, "tn", "mix0_out_dw", out_dtypes=(BF16,), jobs=[H(q3)])
    dy0, (arr2,) = _matmul(dm0, w_out0, "nt", "mix0_out_dx", jobs=[S(2, p2, 0, 2)])
    du_pool, d_pool_w, d_pool_scale = _pool_bwd(dy0, pooled, pool_w_full, pool_scale, "pool_bwd")
    dc, d_ln_g, d_ln_b = _ln_silu_bwd(dy0, c_conv, conv_ln_g, conv_ln_b, "ln_silu_bwd")
    dv, dgate, d_conv_w, d_conv_b = _conv_bwd(dc, a_conv, z0, conv_w_full, DP, "conv_bwd")
    dz0 = jnp.concatenate([du_pool, dv, dgate], axis=1)
    g0, (arr2, (t1,)) = _matmul(h0, dz0, "tn", "mix0_in_dw", out_dtypes=(BF16,), jobs=[S(2, p2, 1, 2, arr2), X(1, g1)])
    p1, q2 = chip_sum(1, g1, t1), reduce(2, p2, arr2[0])
    dh, (arr1, (gr2,), (t0,)) = _matmul(dz0, w_in0, "nt", "mix0_in_dx", jobs=[S(1, p1), H(q2), X(0, g0)])
    p0, q1 = chip_sum(0, g0, t0), reduce(1, p1, arr1[0])
    grad_x, d_mix_pre_0 = _norms_bwd(dx, dh, x0, gain(mix_pre_g, 0), None, None, "norms_bwd0")

    loss = lax.psum(loss_row[0, 0], ("x", "y", "c"))

    upd = {}
    w2_first, (arr0, (gr1,)) = _adamw(ffn_w2, gr7, m_ffn_w2, v_ffn_w2, "adamw_ffn_w2_1", layer=1, jobs=[S(0, p0, 0, 2), H(q1)])
    w1_first, (arr0,) = _adamw(ffn_w1, gr6, m_ffn_w1, v_ffn_w1, "adamw_ffn_w1_1", layer=1, jobs=[S(0, p0, 1, 2, arr0)])
    q0 = reduce(0, p0, arr0[0])
    upd["ffn_w2"], ((gr0,),) = _adamw(ffn_w2, gr3, m_ffn_w2, v_ffn_w2, "adamw_ffn_w2_0", layer=0, carried=w2_first, jobs=[H(q0)])
    upd["ffn_w1"] = _adamw(ffn_w1, gr2, m_ffn_w1, v_ffn_w1, "adamw_ffn_w1_0", layer=0, carried=w1_first)
    upd["sc_w_in"] = _adamw(sc_w_in, gr4, m_sc_w_in, v_sc_w_in, "adamw_sc_w_in")
    upd["sc_w_out"] = _adamw(sc_w_out, gr5, m_sc_w_out, v_sc_w_out, "adamw_sc_w_out")
    upd["ab_w_out"] = _adamw(ab_w_out, gr1, m_ab_w_out, v_ab_w_out, "adamw_ab_w_out")
    upd["ab_w_in"] = _adamw(ab_w_in, gr0, m_ab_w_in, v_ab_w_in, "adamw_ab_w_in")

    small_grads = [jnp.concatenate([d_mix_pre_0, d_mix_pre_1], 0), jnp.concatenate([d_mix_post_0, d_mix_post_1], 0),
                   jnp.concatenate([d_ffn_pre_0, d_ffn_pre_1], 0), jnp.concatenate([d_ffn_post_0, d_ffn_post_1], 0),
                   d_pool_scale, d_conv_b, d_ln_g, d_ln_b, d_pool_w, d_conv_w, d_sc_conv_w]
    small_shapes = [a.shape for a in small_grads]
    summed = _unpack(_all_reduce_small(_pack(small_grads, D), "small_grads_sum"), small_shapes)
    (g_mix_pre, g_mix_post, g_ffn_pre, g_ffn_post, g_pool_scale, g_conv_b, g_ln_g, g_ln_b, g_pool_w_full, g_conv_w_full,
     g_sc_conv_w_full) = summed
    own = lambda a, ax: lax.dynamic_slice_in_dim(a, chip * (a.shape[ax] // N_CHIPS), a.shape[ax] // N_CHIPS, axis=ax)
    g_pool_w, g_conv_w, g_sc_conv_w = own(g_pool_w_full, 1), own(g_conv_w_full, 1), own(g_sc_conv_w_full, 1)

    def small_update(w, g, m, v, name):
        shape = w.shape
        as3 = lambda a: a.reshape((1, -1, shape[-1]))
        outs = _adamw(as3(w), g.reshape((-1, shape[-1])), as3(m), as3(v), "adamw_" + name)
        return [o.reshape(shape) for o in outs]

    upd["mix_pre_g"] = small_update(mix_pre_g, g_mix_pre, m_mix_pre_g, v_mix_pre_g, "mix_pre_g")
    upd["mix_post_g"] = small_update(mix_post_g, g_mix_post, m_mix_post_g, v_mix_post_g, "mix_post_g")
    upd["ffn_pre_g"] = small_update(ffn_pre_g, g_ffn_pre, m_ffn_pre_g, v_ffn_pre_g, "ffn_pre_g")
    upd["ffn_post_g"] = small_update(ffn_post_g, g_ffn_post, m_ffn_post_g, v_ffn_post_g, "ffn_post_g")
    upd["pool_w"] = small_update(pool_w, g_pool_w, m_pool_w, v_pool_w, "pool_w")
    upd["pool_scale"] = small_update(pool_scale, g_pool_scale, m_pool_scale, v_pool_scale, "pool_scale")
    upd["conv_w"] = small_update(conv_w, g_conv_w, m_conv_w, v_conv_w, "conv_w")
    upd["conv_b"] = small_update(conv_b, g_conv_b, m_conv_b, v_conv_b, "conv_b")
    upd["conv_ln_g"] = small_update(conv_ln_g, g_ln_g, m_conv_ln_g, v_conv_ln_g, "conv_ln_g")
    upd["conv_ln_b"] = small_update(conv_ln_b, g_ln_b, m_conv_ln_b, v_conv_ln_b, "conv_ln_b")
    upd["sc_conv_w"] = small_update(sc_conv_w, g_sc_conv_w, m_sc_conv_w, v_sc_conv_w, "sc_conv_w")

    order = ["mix_pre_g", "mix_post_g", "ffn_pre_g", "ffn_post_g", "ab_w_in", "pool_w", "pool_scale", "conv_w", "conv_b",
             "conv_ln_g", "conv_ln_b", "ab_w_out", "sc_w_in", "sc_conv_w", "sc_w_out", "ffn_w1", "ffn_w2"]
    out = [loss, grad_x[None]]
    for part in range(4):
        out += [upd[n][part] for n in order]
    return tuple(out)
```

```python
import jax
import jax.numpy as jnp
from jax import lax
from jax.experimental import pallas as pl
from jax.experimental.pallas import tpu as pltpu

F32, BF16 = jnp.float32, jnp.bfloat16
EPS = 1e-6
N_GROUPS = 4
MAX_WINDOW = 16
CONV_K = 31
SHORT_K = 3
CONV_PAD = 32
SHORT_PAD = 8
ADAM_LR, ADAM_B1, ADAM_B2, ADAM_EPS, ADAM_WD, ADAM_STEP = 0.001, 0.9, 0.999, 1e-08, 0.01, 10
N_CHIPS = 4
VMEM_LIMIT_BYTES = 56 * 1024 * 1024
ROW_TILE = 256
CHUNK = 256
MESH = pl.DeviceIdType.MESH
HBM = pl.BlockSpec(memory_space=pltpu.HBM)


def _cp(*sem):
    return pltpu.CompilerParams(dimension_semantics=sem, vmem_limit_bytes=VMEM_LIMIT_BYTES)


def _sigmoid(v):
    return 1.0 / (1.0 + jnp.exp(-v))


def _call(body, jobs, prefetch=None, **kw):
    beside = _Beside(jobs)
    grid = kw["grid"]
    single = not isinstance(kw["out_shape"], (list, tuple))
    in_specs, scratch = list(kw["in_specs"]), list(kw.get("scratch_shapes", ()))
    out_shape = [kw["out_shape"]] if single else list(kw["out_shape"])
    out_specs = [kw["out_specs"]] if single else list(kw["out_specs"])
    n_pre = 0 if prefetch is None else 1
    n_in, n_out, n_scr = len(in_specs), len(out_shape), len(scratch)

    def wrapped(*refs):
        pre, refs = refs[:n_pre], refs[n_pre:]
        ins, job_ins = refs[:n_in], refs[n_in:n_in + beside.n_in]
        outs = refs[n_in + beside.n_in:n_in + beside.n_in + n_out]
        job_outs = refs[n_in + beside.n_in + n_out:n_in + beside.n_in + n_out + beside.n_out]
        scr = refs[n_in + beside.n_in + n_out + beside.n_out:]
        beside.start(job_ins, job_outs, scr, grid)
        body(*pre, *ins, *outs, *scr[:n_scr])
        beside.finish(job_ins, job_outs, scr, grid)

    aliases = {n_pre + i: o for i, o in kw.get("input_output_aliases", {}).items()}
    aliases.update(beside.aliases(n_pre + n_in, n_out))
    specs = dict(grid=grid, in_specs=in_specs + [HBM] * beside.n_in, out_specs=out_specs + [HBM] * beside.n_out)
    if prefetch is None:
        specs["scratch_shapes"] = scratch + beside.scratch()
    else:
        specs = dict(grid_spec=pltpu.PrefetchScalarGridSpec(num_scalar_prefetch=1, scratch_shapes=scratch + beside.scratch(), **specs))
    sem = ["arbitrary"] * len(grid) if jobs else kw["sem"]
    call = pl.pallas_call(wrapped, name=kw["name"], out_shape=out_shape + beside.out_shape, input_output_aliases=aliases,
                          compiler_params=_cp(*sem), **specs)

    def run(*args):
        outs = call(*([prefetch] * n_pre), *args, *beside.ins)
        own = outs[0] if single else list(outs[:n_out])
        return (own, beside.split(outs[n_out:])) if jobs else own

    return run


_DIMS = {"nn": (((1,), (0,)), ((), ())), "nt": (((1,), (1,)), ((), ())), "tn": (((0,), (0,)), ((), ()))}


def _pick(n, cap, step=256):
    if n <= cap:
        return n
    return next(t for t in range(cap - cap % step, 0, -step) if n % t == 0)


def _matmul(a, b, mode, name, out_dtypes=(F32,), epilogue=None, epi=(), jobs=(), tm=1024, tn=1024, tk=2048):
    if mode == "tn":
        (K, M), (K2, N) = a.shape, b.shape
    elif mode == "nt":
        (M, K), (N, K2) = a.shape, b.shape
    else:
        (M, K), (K2, N) = a.shape, b.shape
    assert K == K2
    tm, tn, tk = _pick(M, tm), _pick(N, tn), _pick(K, tk)
    nk = K // tk
    a_spec = pl.BlockSpec((tk, tm), lambda i, j, k: (k, i)) if mode == "tn" else pl.BlockSpec((tm, tk), lambda i, j, k: (i, k))
    b_spec = pl.BlockSpec((tn, tk), lambda i, j, k: (j, k)) if mode == "nt" else pl.BlockSpec((tk, tn), lambda i, j, k: (k, j))
    o_spec = pl.BlockSpec((tm, tn), lambda i, j, k: (i, j))
    n_epi, n_out = len(epi), len(out_dtypes)

    def body(a_ref, b_ref, *rest):
        epi_refs, out_refs, scratch = rest[:n_epi], rest[n_epi:n_epi + n_out], rest[n_epi + n_out:]
        part = lax.dot_general(a_ref[...].astype(BF16), b_ref[...].astype(BF16), _DIMS[mode], preferred_element_type=F32)

        def finish(acc):
            outs = epilogue(acc, *[r[...] for r in epi_refs]) if epilogue else (acc,)
            for o_ref, o in zip(out_refs, outs):
                o_ref[...] = o.astype(o_ref.dtype)

        if nk == 1:
            finish(part)
        else:
            acc_ref = scratch[0]
            k = pl.program_id(2)

            @pl.when(k == 0)
            def _():
                acc_ref[...] = part

            @pl.when(k > 0)
            def _():
                acc_ref[...] += part

            @pl.when(k == nk - 1)
            def _():
                finish(acc_ref[...])

    got = _call(
        body, jobs, name=name, grid=(M // tm, N // tn, nk),
        in_specs=[a_spec, b_spec] + [o_spec] * n_epi, out_specs=[o_spec] * n_out,
        out_shape=[jax.ShapeDtypeStruct((M, N), dt) for dt in out_dtypes],
        scratch_shapes=[pltpu.VMEM((tm, tn), F32)] if nk > 1 else [],
        sem=("parallel", "parallel", "arbitrary"))(a, b, *epi)
    outs, job_outs = got if jobs else (got, None)
    result = outs[0] if n_out == 1 else outs
    return (result, job_outs) if jobs else result


def _rms(x, g):
    r = lax.rsqrt(jnp.mean(x * x, axis=-1, keepdims=True) + EPS)
    return x * r * g


def _rms_bwd(dy, x, g):
    r = lax.rsqrt(jnp.mean(x * x, axis=-1, keepdims=True) + EPS)
    xn = x * r
    dyg = dy * g
    dx = r * (dyg - xn * jnp.mean(dyg * xn, axis=-1, keepdims=True))
    return dx, jnp.sum(dy * xn, axis=0, keepdims=True)


def _rows(d, tr=ROW_TILE):
    return pl.BlockSpec((tr, d), lambda i: (i, 0))


def _vec(d):
    return pl.BlockSpec((1, d), lambda i: (0, 0))


def _accumulate(ref, val):
    @pl.when(pl.program_id(0) == 0)
    def _():
        ref[...] = val

    @pl.when(pl.program_id(0) > 0)
    def _():
        ref[...] += val


def _norm_fwd(x, g, name, jobs=()):
    T, D = x.shape

    def body(x_ref, g_ref, h_ref):
        h_ref[...] = _rms(x_ref[...], g_ref[...]).astype(BF16)

    return _call(body, jobs, name=name,grid=(T // ROW_TILE,), in_specs=[_rows(D), _vec(D)], out_specs=_rows(D),
                          out_shape=jax.ShapeDtypeStruct((T, D), BF16), sem=("parallel",))(x, g)


def _residual_norm(x, m, g_post, g_next, name, jobs=()):
    T, D = x.shape

    def body(x_ref, m_ref, gp_ref, gn_ref, xo_ref, h_ref):
        xo = x_ref[...] + _rms(m_ref[...], gp_ref[...])
        xo_ref[...] = xo
        h_ref[...] = _rms(xo, gn_ref[...]).astype(BF16)

    return _call(body, jobs, name=name,grid=(T // ROW_TILE,), in_specs=[_rows(D), _rows(D), _vec(D), _vec(D)],
                          out_specs=[_rows(D), _rows(D)],
                          out_shape=[jax.ShapeDtypeStruct((T, D), F32), jax.ShapeDtypeStruct((T, D), BF16)],
                          sem=("parallel",))(x, m, g_post, g_next)


def _loss_and_last_norm_bwd(x, m, g_post, target, name, jobs=()):
    T, D = x.shape

    def body(x_ref, m_ref, gp_ref, t_ref, dx_ref, dm_ref, dg_ref, loss_ref):
        m_val, gp = m_ref[...], gp_ref[...]
        err = x_ref[...] + _rms(m_val, gp) - t_ref[...]
        dx = err * (1.0 / D)
        dx_ref[...] = dx
        dm, dg = _rms_bwd(dx, m_val, gp)
        dm_ref[...] = dm.astype(BF16)
        _accumulate(dg_ref, dg)
        _accumulate(loss_ref, jnp.full((1, 128), 0.5 * jnp.sum(err * err) * (1.0 / D), F32))

    return _call(body, jobs, name=name,grid=(T // ROW_TILE,), in_specs=[_rows(D), _rows(D), _vec(D), _rows(D)],
                          out_specs=[_rows(D), _rows(D), _vec(D), _vec(128)],
                          out_shape=[jax.ShapeDtypeStruct((T, D), F32), jax.ShapeDtypeStruct((T, D), BF16),
                                     jax.ShapeDtypeStruct((1, D), F32), jax.ShapeDtypeStruct((1, 128), F32)],
                          sem=("arbitrary",))(x, m, g_post, target)


def _norms_bwd(dx, dh, x_in, g_pre, m_prev, g_post_prev, name, jobs=()):
    T, D = dx.shape
    with_prev = m_prev is not None

    def body(*refs):
        if with_prev:
            dx_ref, dh_ref, x_ref, gq_ref, m_ref, gp_ref, dxo_ref, dgq_ref, dm_ref, dgp_ref = refs
        else:
            dx_ref, dh_ref, x_ref, gq_ref, dxo_ref, dgq_ref = refs
        d_in, dgq = _rms_bwd(dh_ref[...], x_ref[...], gq_ref[...])
        dxo = dx_ref[...] + d_in
        dxo_ref[...] = dxo
        _accumulate(dgq_ref, dgq)
        if with_prev:
            dm, dgp = _rms_bwd(dxo, m_ref[...], gp_ref[...])
            dm_ref[...] = dm.astype(BF16)
            _accumulate(dgp_ref, dgp)

    ins, in_specs = [dx, dh, x_in, g_pre], [_rows(D), _rows(D), _rows(D), _vec(D)]
    out_specs = [_rows(D), _vec(D)]
    out_shape = [jax.ShapeDtypeStruct((T, D), F32), jax.ShapeDtypeStruct((1, D), F32)]
    if with_prev:
        ins += [m_prev, g_post_prev]
        in_specs += [_rows(D), _vec(D)]
        out_specs += [_rows(D), _vec(D)]
        out_shape += [jax.ShapeDtypeStruct((T, D), BF16), jax.ShapeDtypeStruct((1, D), F32)]
    return _call(body, jobs, name=name,grid=(T // ROW_TILE,), in_specs=in_specs, out_specs=out_specs, out_shape=out_shape,
                          sem=("arbitrary",))(*ins)


def _window_weights(g):
    w = 2 << g
    return w, [jnp.where(j < w, 1.0, 0.0).astype(F32) for j in range(MAX_WINDOW)]


def _valid_count(r0, rows, w):
    t = (lax.broadcasted_iota(jnp.int32, (rows, 1), 0) + (r0 + 1)).astype(F32)
    return jnp.minimum(t, w.astype(F32))


def _pool_fwd(z, pool_w, pool_scale, name, jobs=()):
    T = z.shape[0]
    PG = pool_w.shape[-1]
    DP = N_GROUPS * PG
    rc = min(CHUNK, T)

    def body(u_ref, pw_ref, sc_ref, pooled_ref, y_ref, pad):
        w, wts = _window_weights(pl.program_id(0))
        pad[pl.ds(0, MAX_WINDOW), :] = jnp.zeros((MAX_WINDOW, PG), F32)
        pad[pl.ds(MAX_WINDOW, T), :] = u_ref[...]
        for r0 in range(0, T, rc):
            acc = jnp.zeros((rc, PG), F32)
            for j in range(MAX_WINDOW):
                acc = acc + wts[j] * pad[pl.ds(MAX_WINDOW + r0 - j, rc), :]
            pooled = acc / _valid_count(r0, rc, w) - u_ref[pl.ds(r0, rc), :]
            pooled_ref[pl.ds(r0, rc), :] = pooled.astype(BF16)
        mixed = jnp.dot(pooled_ref[...], pw_ref[...].astype(BF16), preferred_element_type=F32)
        y_ref[...] = (mixed * sc_ref[...]).astype(BF16)

    col = lambda g: (0, g)
    return _call(
        body, jobs, name=name,grid=(N_GROUPS,),
        in_specs=[pl.BlockSpec((T, PG), col), pl.BlockSpec((None, PG, PG), lambda g: (g, 0, 0)), pl.BlockSpec((1, PG), col)],
        out_specs=[pl.BlockSpec((T, PG), col), pl.BlockSpec((T, PG), col)],
        out_shape=[jax.ShapeDtypeStruct((T, DP), BF16), jax.ShapeDtypeStruct((T, DP), BF16)],
        scratch_shapes=[pltpu.VMEM((T + MAX_WINDOW, PG), F32)], sem=("parallel",))(z, pool_w, pool_scale)


def _pool_bwd(dy, pooled, pool_w, pool_scale, name, jobs=()):
    T = dy.shape[0]
    PG = pool_w.shape[-1]
    DP = N_GROUPS * PG
    rc = min(CHUNK, T)

    def body(dy_ref, pooled_ref, pw_ref, sc_ref, du_ref, dpw_ref, dsc_ref, pad, dp_ref):
        w, wts = _window_weights(pl.program_id(0))
        pooled_v, pw = pooled_ref[...], pw_ref[...].astype(BF16)
        dy_v = dy_ref[...]
        mixed = jnp.dot(pooled_v, pw, preferred_element_type=F32)
        dsc_ref[...] = jnp.sum(dy_v * mixed, axis=0, keepdims=True)
        dmixed = (dy_v * sc_ref[...]).astype(BF16)
        dpw_ref[...] = lax.dot_general(pooled_v, dmixed, _DIMS["tn"], preferred_element_type=F32)
        dp_ref[...] = lax.dot_general(dmixed, pw, _DIMS["nt"], preferred_element_type=F32)
        pad[pl.ds(T, MAX_WINDOW), :] = jnp.zeros((MAX_WINDOW, PG), F32)
        for r0 in range(0, T, rc):
            pad[pl.ds(r0, rc), :] = dp_ref[pl.ds(r0, rc), :] / _valid_count(r0, rc, w)
        for r0 in range(0, T, rc):
            acc = jnp.zeros((rc, PG), F32)
            for j in range(MAX_WINDOW):
                acc = acc + wts[j] * pad[pl.ds(r0 + j, rc), :]
            du_ref[pl.ds(r0, rc), :] = (acc - dp_ref[pl.ds(r0, rc), :]).astype(BF16)

    col = lambda g: (0, g)
    return _call(
        body, jobs, name=name,grid=(N_GROUPS,),
        in_specs=[pl.BlockSpec((T, PG), col), pl.BlockSpec((T, PG), col), pl.BlockSpec((None, PG, PG), lambda g: (g, 0, 0)),
                  pl.BlockSpec((1, PG), col)],
        out_specs=[pl.BlockSpec((T, PG), col), pl.BlockSpec((None, PG, PG), lambda g: (g, 0, 0)), pl.BlockSpec((1, PG), col)],
        out_shape=[jax.ShapeDtypeStruct((T, DP), BF16), jax.ShapeDtypeStruct((N_GROUPS, PG, PG), F32),
                   jax.ShapeDtypeStruct((1, DP), F32)],
        scratch_shapes=[pltpu.VMEM((T + MAX_WINDOW, PG), F32), pltpu.VMEM((T, PG), F32)],
        sem=("parallel",))(dy, pooled, pool_w, pool_scale)


def _conv_fwd(z, conv_w, conv_b, d_pool, name, tc=128, jobs=()):
    T = z.shape[0]
    DC = conv_w.shape[-1]
    rc = min(CHUNK, T)
    v0, g0 = d_pool // tc, (d_pool + DC) // tc

    def body(v_ref, gt_ref, w_ref, b_ref, a_ref, c_ref, pad):
        pad[pl.ds(0, CONV_PAD), :] = jnp.zeros((CONV_PAD, tc), F32)
        for r0 in range(0, T, rc):
            a = v_ref[pl.ds(r0, rc), :] * _sigmoid(gt_ref[pl.ds(r0, rc), :])
            a_ref[pl.ds(r0, rc), :] = a
            pad[pl.ds(CONV_PAD + r0, rc), :] = a
        for r0 in range(0, T, rc):
            acc = jnp.zeros((rc, tc), F32) + b_ref[...]
            for k in range(CONV_K):
                acc = acc + w_ref[pl.ds(k, 1), :] * pad[pl.ds(CONV_PAD - (CONV_K - 1) + k + r0, rc), :]
            c_ref[pl.ds(r0, rc), :] = acc

    col = lambda j: (0, j)
    return _call(
        body, jobs, name=name,grid=(DC // tc,),
        in_specs=[pl.BlockSpec((T, tc), lambda j: (0, v0 + j)), pl.BlockSpec((T, tc), lambda j: (0, g0 + j)),
                  pl.BlockSpec((CONV_K, tc), col), pl.BlockSpec((1, tc), col)],
        out_specs=[pl.BlockSpec((T, tc), col), pl.BlockSpec((T, tc), col)],
        out_shape=[jax.ShapeDtypeStruct((T, DC), F32), jax.ShapeDtypeStruct((T, DC), F32)],
        scratch_shapes=[pltpu.VMEM((T + CONV_PAD, tc), F32)], sem=("parallel",))(z, z, conv_w, conv_b)


def _conv_bwd(dc, a, z, conv_w, d_pool, name, tc=128, jobs=()):
    T, DC = dc.shape
    rc = min(CHUNK, T)
    v0, g0 = d_pool // tc, (d_pool + DC) // tc

    def body(dc_ref, a_ref, v_ref, gt_ref, w_ref, dv_ref, dg_ref, dw_ref, db_ref, apad, dpad):
        apad[pl.ds(0, CONV_PAD), :] = jnp.zeros((CONV_PAD, tc), F32)
        apad[pl.ds(CONV_PAD, T), :] = a_ref[...]
        dpad[pl.ds(0, T), :] = dc_ref[...]
        dpad[pl.ds(T, CONV_PAD), :] = jnp.zeros((CONV_PAD, tc), F32)
        db_ref[...] = jnp.sum(dc_ref[...], axis=0, keepdims=True)
        for k in range(CONV_K):
            acc = jnp.zeros((8, tc), F32)
            for r0 in range(0, T, rc):
                prod = dc_ref[pl.ds(r0, rc), :] * apad[pl.ds(CONV_PAD - (CONV_K - 1) + k + r0, rc), :]
                acc = acc + jnp.sum(prod.reshape(rc // 8, 8, tc), axis=0)
            dw_ref[pl.ds(k, 1), :] = jnp.sum(acc, axis=0, keepdims=True)
        for r0 in range(0, T, rc):
            da = jnp.zeros((rc, tc), F32)
            for k in range(CONV_K):
                da = da + w_ref[pl.ds(k, 1), :] * dpad[pl.ds(r0 + (CONV_K - 1) - k, rc), :]
            sig = _sigmoid(gt_ref[pl.ds(r0, rc), :])
            dv_ref[pl.ds(r0, rc), :] = (da * sig).astype(BF16)
            dg_ref[pl.ds(r0, rc), :] = (da * v_ref[pl.ds(r0, rc), :] * sig * (1.0 - sig)).astype(BF16)

    col = lambda j: (0, j)
    return _call(
        body, jobs, name=name,grid=(DC // tc,),
        in_specs=[pl.BlockSpec((T, tc), col), pl.BlockSpec((T, tc), col), pl.BlockSpec((T, tc), lambda j: (0, v0 + j)),
                  pl.BlockSpec((T, tc), lambda j: (0, g0 + j)), pl.BlockSpec((CONV_K, tc), col)],
        out_specs=[pl.BlockSpec((T, tc), col), pl.BlockSpec((T, tc), col), pl.BlockSpec((CONV_K, tc), col),
                   pl.BlockSpec((1, tc), col)],
        out_shape=[jax.ShapeDtypeStruct((T, DC), BF16), jax.ShapeDtypeStruct((T, DC), BF16),
                   jax.ShapeDtypeStruct((CONV_K, DC), F32), jax.ShapeDtypeStruct((1, DC), F32)],
        scratch_shapes=[pltpu.VMEM((T + CONV_PAD, tc), F32), pltpu.VMEM((T + CONV_PAD, tc), F32)],
        sem=("parallel",))(dc, a, z, z, conv_w)


def _layer_norm_parts(c, g, b):
    mu = jnp.mean(c, axis=-1, keepdims=True)
    xc = c - mu
    rstd = lax.rsqrt(jnp.mean(xc * xc, axis=-1, keepdims=True) + EPS)
    xhat = xc * rstd
    return xhat, rstd, xhat * g + b


def _ln_silu_fwd(c, g, b, name, jobs=()):
    T, DC = c.shape

    def body(c_ref, g_ref, b_ref, y_ref):
        _, _, ln = _layer_norm_parts(c_ref[...], g_ref[...], b_ref[...])
        y_ref[...] = (ln * _sigmoid(ln)).astype(BF16)

    return _call(body, jobs, name=name,grid=(T // ROW_TILE,), in_specs=[_rows(DC), _vec(DC), _vec(DC)], out_specs=_rows(DC),
                          out_shape=jax.ShapeDtypeStruct((T, DC), BF16), sem=("parallel",))(c, g, b)


def _ln_silu_bwd(dy, c, g, b, name, jobs=()):
    T, DC = c.shape

    def body(dy_ref, c_ref, g_ref, b_ref, dc_ref, dg_ref, db_ref):
        gain = g_ref[...]
        xhat, rstd, ln = _layer_norm_parts(c_ref[...], gain, b_ref[...])
        s = _sigmoid(ln)
        dln = dy_ref[...] * (s * (1.0 + ln * (1.0 - s)))
        _accumulate(dg_ref, jnp.sum(dln * xhat, axis=0, keepdims=True))
        _accumulate(db_ref, jnp.sum(dln, axis=0, keepdims=True))
        dxh = dln * gain
        dc_ref[...] = rstd * (dxh - jnp.mean(dxh, axis=-1, keepdims=True) - xhat * jnp.mean(dxh * xhat, axis=-1, keepdims=True))

    return _call(body, jobs, name=name,grid=(T // ROW_TILE,),
                          in_specs=[pl.BlockSpec((ROW_TILE, DC), lambda i: (i, 1)), _rows(DC), _vec(DC), _vec(DC)],
                          out_specs=[_rows(DC), _vec(DC), _vec(DC)],
                          out_shape=[jax.ShapeDtypeStruct((T, DC), F32), jax.ShapeDtypeStruct((1, DC), F32),
                                     jax.ShapeDtypeStruct((1, DC), F32)],
                          sem=("arbitrary",))(dy, c, g, b)


def _short_specs(T, DS, tc):
    n = DS // tc
    return [pl.BlockSpec((T, tc), lambda j: (0, j)), pl.BlockSpec((T, tc), lambda j: (0, n + j)),
            pl.BlockSpec((T, tc), lambda j: (0, 2 * n + j))]


def _short_fwd(z, w, name, tc=256, jobs=()):
    T = z.shape[0]
    DS = w.shape[-1]
    rc = min(CHUNK, T)

    def body(b_ref, cg_ref, u_ref, w_ref, y_ref, pad):
        pad[pl.ds(0, SHORT_PAD), :] = jnp.zeros((SHORT_PAD, tc), F32)
        pad[pl.ds(SHORT_PAD, T), :] = cg_ref[...] * u_ref[...]
        for r0 in range(0, T, rc):
            r = jnp.zeros((rc, tc), F32)
            for k in range(SHORT_K):
                r = r + w_ref[pl.ds(k, 1), :] * pad[pl.ds(SHORT_PAD - (SHORT_K - 1) + k + r0, rc), :]
            y_ref[pl.ds(r0, rc), :] = (b_ref[pl.ds(r0, rc), :] * r).astype(BF16)

    col = lambda j: (0, j)
    return _call(body, jobs, name=name,grid=(DS // tc,), in_specs=_short_specs(T, DS, tc) + [pl.BlockSpec((SHORT_K, tc), col)],
                          out_specs=pl.BlockSpec((T, tc), col), out_shape=jax.ShapeDtypeStruct((T, DS), BF16),
                          scratch_shapes=[pltpu.VMEM((T + SHORT_PAD, tc), F32)], sem=("parallel",))(z, z, z, w)


def _short_bwd(dy, z, w, name, tc=256, jobs=()):
    T, DS = dy.shape
    rc = min(CHUNK, T)

    def body(dy_ref, b_ref, cg_ref, u_ref, w_ref, db_ref, dcg_ref, du_ref, dw_ref, qpad, rpad):
        qpad[pl.ds(0, SHORT_PAD), :] = jnp.zeros((SHORT_PAD, tc), F32)
        qpad[pl.ds(SHORT_PAD, T), :] = cg_ref[...] * u_ref[...]
        rpad[pl.ds(0, T), :] = dy_ref[...] * b_ref[...]
        rpad[pl.ds(T, SHORT_PAD), :] = jnp.zeros((SHORT_PAD, tc), F32)
        accs = [jnp.zeros((8, tc), F32) for _ in range(SHORT_K)]
        for r0 in range(0, T, rc):
            r = jnp.zeros((rc, tc), F32)
            dq = jnp.zeros((rc, tc), F32)
            dr = rpad[pl.ds(r0, rc), :]
            for k in range(SHORT_K):
                q_k = qpad[pl.ds(SHORT_PAD - (SHORT_K - 1) + k + r0, rc), :]
                r = r + w_ref[pl.ds(k, 1), :] * q_k
                dq = dq + w_ref[pl.ds(k, 1), :] * rpad[pl.ds(r0 + (SHORT_K - 1) - k, rc), :]
                accs[k] = accs[k] + jnp.sum((dr * q_k).reshape(rc // 8, 8, tc), axis=0)
            db_ref[pl.ds(r0, rc), :] = (dy_ref[pl.ds(r0, rc), :] * r).astype(BF16)
            dcg_ref[pl.ds(r0, rc), :] = (dq * u_ref[pl.ds(r0, rc), :]).astype(BF16)
            du_ref[pl.ds(r0, rc), :] = (dq * cg_ref[pl.ds(r0, rc), :]).astype(BF16)
        for k in range(SHORT_K):
            dw_ref[pl.ds(k, 1), :] = jnp.sum(accs[k], axis=0, keepdims=True)

    col = lambda j: (0, j)
    tile = pl.BlockSpec((T, tc), col)
    return _call(body, jobs, name=name,grid=(DS // tc,),
                          in_specs=[tile] + _short_specs(T, DS, tc) + [pl.BlockSpec((SHORT_K, tc), col)],
                          out_specs=[tile, tile, tile, pl.BlockSpec((SHORT_K, tc), col)],
                          out_shape=[jax.ShapeDtypeStruct((T, DS), BF16)] * 3 + [jax.ShapeDtypeStruct((SHORT_K, DS), F32)],
                          scratch_shapes=[pltpu.VMEM((T + SHORT_PAD, tc), F32), pltpu.VMEM((T + SHORT_PAD, tc), F32)],
                          sem=("parallel",))(dy, z, z, z, w)


def _tile_rows(rows, cols, n_bufs):
    budget = VMEM_LIMIT_BYTES // 2 // (2 * n_bufs * 4 * cols)
    tr = rows
    while tr > budget and tr % 16 == 0:
        tr //= 2
    return tr


def _placed_call(body, name, place, grid, in_specs, out_specs, out_shape, ins, jobs=()):
    return _call(body, jobs, prefetch=place, name=name, grid=grid, in_specs=in_specs, out_specs=out_specs, out_shape=out_shape,
                 sem=("parallel",))(*ins)


def _cast_into_full(w, layer, kind, place, name, jobs=()):
    _, R, C = w.shape
    tr = _tile_rows(R, C, 2)
    nb = R // tr
    if kind == "col":
        full, out_spec = (R, C * N_CHIPS), pl.BlockSpec((tr, C), lambda i, s: (i, s[0]))
    else:
        full, out_spec = (R * N_CHIPS, C), pl.BlockSpec((tr, C), lambda i, s: (s[0] * nb + i, 0))

    def body(s_ref, w_ref, o_ref):
        o_ref[...] = w_ref[...].astype(BF16)

    return _placed_call(body, name, place, (nb,), [pl.BlockSpec((None, tr, C), lambda i, s: (layer, i, 0))], out_spec,
                        jax.ShapeDtypeStruct(full, BF16), [w], jobs)


def _add_pair(grad, theirs, kind, place, name, jobs=()):
    R, C = grad.shape
    piece_rows = R // 2 if kind == "col" else R // N_CHIPS // 2
    tr = _tile_rows(piece_rows, C, 3)
    nb = piece_rows // tr
    if kind == "col":
        g_spec = pl.BlockSpec((tr, C), lambda i, s: (s[1] * nb + i, 0))
    else:
        g_spec = pl.BlockSpec((tr, C), lambda i, s: ((2 * (i // nb) + s[1]) * nb + i % nb, 0))
    flat = pl.BlockSpec((tr, C), lambda i, s: (i, 0))

    def body(s_ref, a_ref, b_ref, o_ref):
        o_ref[...] = (a_ref[...].astype(F32) + b_ref[...].astype(F32)).astype(BF16)

    return _placed_call(body, name, place, (R // 2 // tr,), [g_spec, flat], flat, jax.ShapeDtypeStruct((R // 2, C), BF16),
                        [grad, theirs], jobs)


def _sum_chips(chip_sum, arrived, kind, place, name, jobs=()):
    _, H, W = arrived.shape
    tr = _tile_rows(H, W, 6)
    nb = H // tr
    if kind == "col":
        own_spec = pl.BlockSpec((tr, W), lambda i, s: (i, s[0]))
    else:
        own_spec = pl.BlockSpec((tr, W), lambda i, s: (s[0] * nb + i, 0))

    def body(s_ref, p_ref, r_ref, o_ref):
        acc = p_ref[...].astype(F32)
        for i in range(N_CHIPS - 1):
            acc = acc + r_ref[i].astype(F32)
        o_ref[...] = acc

    return _placed_call(body, name, place, (nb,), [own_spec, pl.BlockSpec((N_CHIPS - 1, tr, W), lambda i, s: (0, i, 0))],
                        pl.BlockSpec((tr, W), lambda i, s: (s[1] * nb + i, 0)), jax.ShapeDtypeStruct((2 * H, W), F32),
                        [chip_sum, arrived], jobs)


def _adamw_values(w, g, m, v):
    m = ADAM_B1 * m + (1.0 - ADAM_B1) * g
    v = ADAM_B2 * v + (1.0 - ADAM_B2) * (g * g)
    m_hat = m / (1.0 - ADAM_B1 ** ADAM_STEP)
    v_hat = v / (1.0 - ADAM_B2 ** ADAM_STEP)
    return -ADAM_LR * (m_hat / (jnp.sqrt(v_hat) + ADAM_EPS) + ADAM_WD * w), m, v


def _adamw(w, g, m, v, name, layer=0, carried=None):
    L, R, C = w.shape
    tr = _tile_rows(R, C, 8)

    def body(w_ref, g_ref, m_ref, v_ref, *rest):
        go_ref, d_ref, mo_ref, vo_ref = rest[-4:]
        g_val = g_ref[...]
        d, m_new, v_new = _adamw_values(w_ref[...], g_val, m_ref[...], v_ref[...])
        go_ref[...], d_ref[...], mo_ref[...], vo_ref[...] = g_val, d, m_new, v_new

    lay = pl.BlockSpec((None, tr, C), lambda i: (layer, i, 0))
    ins = [w, g, m, v]
    in_specs = [lay, pl.BlockSpec((tr, C), lambda i: (i, 0)), lay, lay]
    aliases = {}
    if carried is not None:
        ins += list(carried)
        in_specs += [pl.BlockSpec(memory_space=pl.ANY)] * 4
        aliases = {4 + i: i for i in range(4)}
    return _call(body, (), name=name, grid=(R // tr,), in_specs=in_specs, out_specs=[lay] * 4,
                 out_shape=[jax.ShapeDtypeStruct((L, R, C), F32)] * 4, input_output_aliases=aliases, sem=("parallel",))(*ins)


def _aligned(v, m):
    return v if isinstance(v, int) else pl.multiple_of(v, m)


def _place():
    x, y, c = lax.axis_index("x"), lax.axis_index("y"), lax.axis_index("c")
    other_chips = [(x, 1 - y), (1 - x, y), (1 - x, 1 - y)]
    return x, y, c, 2 * x + y, other_chips


def _chip_index(chip):
    return 2 * chip[0] + chip[1]


def _piece(ref, kind, k, h):
    R, C = ref.shape
    if kind == "col":
        return ref.at[pl.ds(_aligned(h * (R // 2), 16), R // 2), pl.ds(_aligned(k * (C // N_CHIPS), 128), C // N_CHIPS)]
    rs = R // N_CHIPS
    return ref.at[pl.ds(_aligned(k * rs + h * (rs // 2), 16), rs // 2), :]


def _compact_piece(ref, kind, k):
    R2, C = ref.shape
    if kind == "col":
        return ref.at[:, pl.ds(_aligned(k * (C // N_CHIPS), 128), C // N_CHIPS)]
    return ref.at[pl.ds(_aligned(k * (R2 // N_CHIPS), 16), R2 // N_CHIPS), :]


def _half_rows(ref, h):
    R = ref.shape[0]
    return ref.at[pl.ds(_aligned(h * (R // 2), 16), R // 2), :]


class _Copies:
    def __init__(self, send_sems, recv_sems):
        self.send_sems, self.recv_sems = send_sems, recv_sems
        self.n_remote = 0

    def remote(self, src, dst, device):
        k = self.n_remote
        self.n_remote += 1
        return pltpu.make_async_remote_copy(src_ref=src, dst_ref=dst, send_sem=self.send_sems.at[k], recv_sem=self.recv_sems.at[k],
                                            device_id=device, device_id_type=MESH)


class _Job:
    def __init__(self, ins, out_shape, aliases, n_remote, build):
        self.ins, self.out_shape, self.aliases, self.n_remote, self.build = list(ins), list(out_shape), dict(aliases), n_remote, build


def _merge_jobs(jobs):
    ins, out_shape, aliases, spans = [], [], {}, []
    for job in jobs:
        spans.append((len(ins), len(job.ins), len(out_shape), len(job.out_shape)))
        aliases.update({len(ins) + i: len(out_shape) + o for i, o in job.aliases.items()})
        ins += job.ins
        out_shape += job.out_shape

    def build(in_refs, out_refs, cp):
        copies = []
        for job, (i0, ni, o0, no) in zip(jobs, spans):
            copies += job.build(in_refs[i0:i0 + ni], out_refs[o0:o0 + no], cp)
        return copies

    return _Job(ins, out_shape, aliases, sum(j.n_remote for j in jobs), build)


def _run_jobs(jobs, name):
    job = _merge_jobs(jobs)
    n_in, n_out = len(job.ins), len(job.out_shape)

    def body(*refs):
        copies = job.build(refs[:n_in], refs[n_in:n_in + n_out], _Copies(*refs[n_in + n_out:]))
        for d in copies:
            d.start()
        for d in copies:
            d.wait_recv()
        for d in copies:
            d.wait_send()

    outs = pl.pallas_call(
        body, name=name, in_specs=[HBM] * n_in, out_specs=[HBM] * n_out, out_shape=job.out_shape,
        input_output_aliases=job.aliases,
        scratch_shapes=[pltpu.SemaphoreType.DMA((job.n_remote,)), pltpu.SemaphoreType.DMA((job.n_remote,))])(*job.ins)
    split, o0 = [], 0
    for j in jobs:
        split.append(outs[o0:o0 + len(j.out_shape)])
        o0 += len(j.out_shape)
    return split


class _Beside:
    def __init__(self, jobs):
        self.jobs = list(jobs)
        job = self.job = _merge_jobs(self.jobs) if self.jobs else None
        self.ins = job.ins if job else []
        self.out_shape = job.out_shape if job else []
        self.n_in, self.n_out = len(self.ins), len(self.out_shape)

    def scratch(self):
        n = self.job.n_remote if self.job else 0
        return [pltpu.SemaphoreType.DMA((n,)), pltpu.SemaphoreType.DMA((n,))] if n else []

    def aliases(self, first_in, first_out):
        return {first_in + i: first_out + o for i, o in self.job.aliases.items()} if self.job else {}

    def _at(self, grid, which):
        cond = None
        for axis, n in enumerate(grid):
            here = pl.program_id(axis) == (0 if which == "first" else n - 1)
            cond = here if cond is None else jnp.logical_and(cond, here)
        return cond

    def start(self, in_refs, out_refs, scratch, grid):
        if self.job:
            @pl.when(self._at(grid, "first"))
            def _():
                for d in self.job.build(in_refs, out_refs, _Copies(*scratch[-2:])):
                    d.start()

    def finish(self, in_refs, out_refs, scratch, grid):
        if self.job:
            @pl.when(self._at(grid, "last"))
            def _():
                copies = self.job.build(in_refs, out_refs, _Copies(*scratch[-2:]))
                for d in copies:
                    d.wait_recv()
                for d in copies:
                    d.wait_send()

    def split(self, outs):
        per_job, o0 = [], 0
        for j in self.jobs:
            per_job.append(list(outs[o0:o0 + len(j.out_shape)]))
            o0 += len(j.out_shape)
        return per_job


def _in_place(arrays):
    return [jax.ShapeDtypeStruct(a.shape, a.dtype) for a in arrays], {u: u for u in range(len(arrays))}


def _rows_part(ref, part, n_parts):
    h = ref.shape[0] // n_parts
    return ref.at[pl.ds(part * h, h), :]


def _gather_job(full, kind, n_parts, ici_parts=(), forward_parts=()):
    def build(in_refs, out_refs, cp):
        x, y, c, me, chips = _place()
        (ref,) = out_refs
        copies = []
        for p in ici_parts:
            mine = _rows_part(_piece(ref, kind, me, c), p, n_parts)
            copies += [cp.remote(mine, mine, (*chip, c)) for chip in chips]
        for p in forward_parts:
            for chip in chips:
                piece = _rows_part(_piece(ref, kind, _chip_index(chip), c), p, n_parts)
                copies.append(cp.remote(piece, piece, (x, y, 1 - c)))
        return copies

    return _Job([full], *_in_place([full]), 3 * (len(ici_parts) + len(forward_parts)), build)


def _gather_small_job(fulls, axes):
    def build(in_refs, out_refs, cp):
        x, y, c, me, chips = _place()
        copies = []
        for ref, ax in zip(out_refs, axes):
            n = ref.shape[ax] // N_CHIPS
            idx = [slice(None)] * len(ref.shape)
            idx[ax] = pl.ds(_aligned(me * n, n), n)
            mine = ref.at[tuple(idx)]
            copies += [cp.remote(mine, mine, (*chip, c)) for chip in chips]
        return copies

    return _Job(fulls, *_in_place(fulls), 3 * len(fulls), build)


def _exchange_halves_job(grads, kinds):
    def build(in_refs, out_refs, cp):
        x, y, c, me, chips = _place()
        copies = []
        for src, dst, kind in zip(in_refs, out_refs, kinds):
            if kind == "col":
                copies.append(cp.remote(_half_rows(src, 1 - c), dst, (x, y, 1 - c)))
            else:
                copies += [cp.remote(_piece(src, "row", k, 1 - c), _compact_piece(dst, "row", k), (x, y, 1 - c))
                           for k in range(N_CHIPS)]
        return copies

    out_shape = [jax.ShapeDtypeStruct((g.shape[0] // 2, g.shape[1]), g.dtype) for g in grads]
    return _Job(grads, out_shape, {}, sum(1 if k == "col" else N_CHIPS for k in kinds), build)


def _scatter_job(half, kind, n_parts, parts, into=None):
    def build(in_refs, out_refs, cp):
        x, y, c, me, chips = _place()
        src, (dst,) = in_refs[0], out_refs
        copies = []
        for p in parts:
            copies += [cp.remote(_rows_part(_compact_piece(src, kind, _chip_index(chip)), p, n_parts),
                                 _rows_part(dst.at[r], p, n_parts), (*chip, c)) for r, chip in enumerate(chips)]
        return copies

    part_shape = (half.shape[0], half.shape[1] // N_CHIPS) if kind == "col" else (half.shape[0] // N_CHIPS, half.shape[1])
    out_shape = [jax.ShapeDtypeStruct((N_CHIPS - 1,) + part_shape, half.dtype)]
    ins, aliases = ([half], {}) if into is None else ([half, into], {1: 0})
    return _Job(ins, out_shape, aliases, 3 * len(parts), build)


def _share_job(shards):
    def build(in_refs, out_refs, cp):
        x, y, c, me, chips = _place()
        copies = []
        for ref in out_refs:
            mine = _half_rows(ref, c)
            copies.append(cp.remote(mine, mine, (x, y, 1 - c)))
        return copies

    return _Job(shards, *_in_place(shards), len(shards), build)


N_DEVICES = 2 * N_CHIPS


def _small_exchange_job(slots, n_parts, parts):
    def build(in_refs, out_refs, cp):
        x, y, c, me, chips = _place()
        (ref,) = out_refs
        copies = []
        for part in parts:
            mine = _rows_part(ref.at[2 * me + c], part, n_parts)
            copies += [cp.remote(mine, mine, (x ^ (p >> 2), y ^ ((p >> 1) & 1), c ^ (p & 1))) for p in range(1, N_DEVICES)]
        return copies

    return _Job([slots], *_in_place([slots]), (N_DEVICES - 1) * len(parts), build)


def _sum_slots(slots, name, jobs=()):
    n, R, C = slots.shape

    def body(s_ref, o_ref):
        acc = s_ref[0]
        for i in range(1, n):
            acc = acc + s_ref[i]
        o_ref[...] = acc

    return _call(body, jobs, name=name, grid=(1,), in_specs=[pl.BlockSpec((n, R, C), lambda i: (0, 0, 0))],
                 out_specs=pl.BlockSpec((R, C), lambda i: (0, 0)), out_shape=jax.ShapeDtypeStruct((R, C), F32),
                 sem=("arbitrary",))(slots)


def _pack(arrays, width):
    rows = []
    for a in arrays:
        flat = a.reshape(-1)
        n_rows = -(-flat.shape[0] // width)
        rows.append(jnp.pad(flat, (0, n_rows * width - flat.shape[0])).reshape(n_rows, width))
    n = sum(r.shape[0] for r in rows)
    rows.append(jnp.zeros((-n % 8, width), F32))
    return jnp.concatenate(rows, axis=0)


def _unpack(packed, shapes):
    out, r0, width = [], 0, packed.shape[1]
    for shape in shapes:
        size = 1
        for d in shape:
            size *= d
        n_rows = -(-size // width)
        out.append(packed[r0:r0 + n_rows].reshape(-1)[:size].reshape(shape))
        r0 += n_rows
    return out


class _Traffic:
    def __init__(self):
        self.tasks = []

    def add(self, priority, cost, stream, part):
        self.tasks.append((priority, cost, stream, part))
        self.tasks.sort(key=lambda t: t[0])

    def _run(self, taken, call):
        groups = []
        for _, _, stream, part in taken:
            for g in groups:
                if g[0] is stream:
                    g[1].append(part)
                    break
            else:
                groups.append((stream, [part]))
        own, outs = call([stream.job(parts) for stream, parts in groups])
        for (stream, parts), o in zip(groups, outs):
            stream.done(o, parts)
        return own

    def carry(self, fn, budget, *args, **kw):
        taken, spent = [], 0.0
        while self.tasks and (spent < budget or self.tasks[0][1] == 0.0):
            taken.append(self.tasks.pop(0))
            spent += taken[-1][1]
        if not taken:
            return fn(*args, **kw)
        return self._run(taken, lambda jobs: fn(*args, jobs=jobs, **kw))

    def flush(self, name, below=None, extra=0.0):
        wave = 0
        while any(below is None or t[0] < below for t in self.tasks):
            taken = [t for t in self.tasks if below is None or t[0] < below]
            rest = [t for t in self.tasks if not (below is None or t[0] < below)]
            spent = 0.0
            while rest and spent < extra:
                taken.append(rest.pop(0))
                spent += taken[-1][1]
            self.tasks = rest
            self._run(taken, lambda jobs: (None, _run_jobs(jobs, "%s_%d" % (name, wave))))
            wave += 1


class _GatherStream:
    def __init__(self, traffic, bufs, u, kind, n_parts, cost):
        self.traffic, self.bufs, self.u, self.kind, self.n_parts = traffic, bufs, u, kind, n_parts
        for p in range(n_parts):
            traffic.add((u, p, 0), cost / n_parts, self, ("ici", p))

    def job(self, parts):
        return _gather_job(self.bufs[self.u], self.kind, self.n_parts, [p for k, p in parts if k == "ici"],
                           [p for k, p in parts if k == "forward"])

    def done(self, outs, parts):
        self.bufs[self.u] = outs[0]
        for k, p in parts:
            if k == "ici":
                self.traffic.add((self.u, p, 1), 0.0, self, ("forward", p))


class _InPlaceStream:
    def __init__(self, traffic, priority, cost, bufs, make_job, then):
        self.bufs, self.make_job, self.then = bufs, make_job, then
        traffic.add(priority, cost, self, 0)

    def job(self, parts):
        return self.make_job(self.bufs)

    def done(self, outs, parts):
        self.then(list(outs))


class _GradStream:
    def __init__(self, traffic, u, name, kind, n_parts, cost, seq, g, place, results):
        self.traffic, self.u, self.name, self.kind, self.n_parts, self.cost, self.seq = traffic, u, name, kind, n_parts, cost, seq
        self.g, self.place, self.results, self.arrived, self.left = g, place, results, None, n_parts
        traffic.add((seq, 0, 0), 0.0, self, ("X", 0))

    def job(self, parts):
        step = parts[0][0]
        if step == "X":
            return _exchange_halves_job([self.g], [self.kind])
        if step == "S":
            return _scatter_job(self.chip_sum, self.kind, self.n_parts, [p for _, p in parts], self.arrived)
        return _share_job([self.reduced])

    def done(self, outs, parts):
        step = parts[0][0]
        if step == "X":
            self.chip_sum = self.traffic.carry(_add_pair, SIDE_KERNEL_US, self.g, outs[0], self.kind, self.place, "chip_sum_" + self.name)
            for p in range(self.n_parts):
                self.traffic.add((self.seq, 1, p), self.cost / self.n_parts, self, ("S", p))
        elif step == "S":
            self.arrived = outs[0]
            self.left -= len(parts)
            if self.left == 0:
                self.reduced = self.traffic.carry(_sum_chips, SIDE_KERNEL_US, self.chip_sum, self.arrived, self.kind, self.place,
                                                  "reduce_" + self.name)
                self.traffic.add((self.seq, 2, 0), 0.0, self, ("H", 0))
        else:
            self.results[self.u] = outs[0]


class _SmallStream:
    def __init__(self, traffic, priority, slots, n_parts, cost):
        self.slots, self.n_parts = slots, n_parts
        for p in range(n_parts):
            traffic.add(priority + (p,), cost / n_parts, self, p)

    def job(self, parts):
        return _small_exchange_job(self.slots, self.n_parts, parts)

    def done(self, outs, parts):
        self.slots = outs[0]


SIDE_KERNEL_US = 12.0


def kernel(x, mix_pre_g, mix_post_g, ffn_pre_g, ffn_post_g, ab_w_in, pool_w, pool_scale, conv_w, conv_b, conv_ln_g, conv_ln_b, ab_w_out, sc_w_in, sc_conv_w, sc_w_out, ffn_w1, ffn_w2, loss_target, m_mix_pre_g, m_mix_post_g, m_ffn_pre_g, m_ffn_post_g, m_ab_w_in, m_pool_w, m_pool_scale, m_conv_w, m_conv_b, m_conv_ln_g, m_conv_ln_b, m_ab_w_out, m_sc_w_in, m_sc_conv_w, m_sc_w_out, m_ffn_w1, m_ffn_w2, v_mix_pre_g, v_mix_post_g, v_ffn_pre_g, v_ffn_post_g, v_ab_w_in, v_pool_w, v_pool_scale, v_conv_w, v_conv_b, v_conv_ln_g, v_conv_ln_b, v_ab_w_out, v_sc_w_in, v_sc_conv_w, v_sc_w_out, v_ffn_w1, v_ffn_w2):
    x0, target = x[0], loss_target[0]
    T, D = x0.shape
    DP = pool_scale.shape[-1]
    gain = lambda g, layer: g[layer][None, :]

    big = [("ab_w_in", ab_w_in, 0, "col", 4, 67.0), ("ab_w_out", ab_w_out, 0, "row", 2, 44.0),
           ("ffn_w1_0", ffn_w1, 0, "col", 8, 177.0), ("ffn_w2_0", ffn_w2, 0, "row", 8, 177.0),
           ("sc_w_in", sc_w_in, 0, "col", 4, 133.0), ("sc_w_out", sc_w_out, 0, "row", 2, 44.0),
           ("ffn_w1_1", ffn_w1, 1, "col", 8, 177.0), ("ffn_w2_1", ffn_w2, 1, "row", 8, 177.0)]
    kinds = [b[3] for b in big]
    chip = 2 * lax.axis_index("x") + lax.axis_index("y")
    place = jnp.stack([chip, lax.axis_index("c")]).astype(jnp.int32)
    traffic = _Traffic()
    carry = traffic.carry

    def own_in_zeros(shard, ax):
        full = jnp.zeros(tuple(d * N_CHIPS if i == ax else d for i, d in enumerate(shard.shape)), shard.dtype)
        return lax.dynamic_update_slice_in_dim(full, shard, chip * shard.shape[ax], axis=ax)

    W = [None] * len(big)
    smalls = [own_in_zeros(pool_w[0], 1), own_in_zeros(conv_w[0], 1), own_in_zeros(sc_conv_w[0], 1)]
    _InPlaceStream(traffic, (0, -1), 6.0, smalls, lambda bufs: _gather_small_job(bufs, [1, 1, 1]),
                   lambda outs: smalls.__setitem__(slice(None), outs))
    for u, (name, w, layer, kind, n_parts, cost) in enumerate(big):
        W[u] = carry(_cast_into_full, 10.0, w, layer, kind, place, "cast_" + name)
        _GatherStream(traffic, W, u, kind, n_parts, cost)

    def ready(u):
        traffic.flush("gather_%d" % u, below=(u + 1,), extra=20.0)
        return W[u]

    relu_sq = lambda acc: (jnp.maximum(acc, 0.0), jnp.square(jnp.maximum(acc, 0.0)))
    relu_sq_bwd = lambda acc, a: (acc * (2.0 * a.astype(F32)),)

    h0 = carry(_norm_fwd, 12.0, x0, gain(mix_pre_g, 0), "norm_in")
    z0 = carry(_matmul, 35.0, h0, ready(0), "nn", "mix0_in")
    pool_w_full, conv_w_full, sc_conv_w_full = smalls
    pooled, y_pool = carry(_pool_fwd, 23.0, z0, pool_w_full, pool_scale, "pool_fwd")
    a_conv, c_conv = carry(_conv_fwd, 25.0, z0, conv_w_full, conv_b, DP, "conv_fwd")
    y_conv = carry(_ln_silu_fwd, 10.0, c_conv, conv_ln_g, conv_ln_b, "ln_silu_fwd")
    y0 = jnp.concatenate([y_pool, y_conv], axis=1)
    m0 = carry(_matmul, 25.0, y0, ready(1), "nn", "mix0_out")
    x1, h1 = carry(_residual_norm, 21.0, x0, m0, gain(mix_post_g, 0), gain(ffn_pre_g, 0), "res_mix0")
    a0, a0sq = carry(_matmul, 81.0, h1, ready(2), "nn", "ffn0_up", out_dtypes=(BF16, BF16), epilogue=relu_sq)
    f0 = carry(_matmul, 87.0, a0sq, ready(3), "nn", "ffn0_down")
    x2, h2 = carry(_residual_norm, 22.0, x1, f0, gain(ffn_post_g, 0), gain(mix_pre_g, 1), "res_ffn0")
    z1 = carry(_matmul, 62.0, h2, ready(4), "nn", "mix1_in")
    y1 = carry(_short_fwd, 22.0, z1, sc_conv_w_full, "short_fwd")
    m1 = carry(_matmul, 25.0, y1, ready(5), "nn", "mix1_out")
    x3, h3 = carry(_residual_norm, 21.0, x2, m1, gain(mix_post_g, 1), gain(ffn_pre_g, 1), "res_mix1")
    a1, a1sq = carry(_matmul, 81.0, h3, ready(6), "nn", "ffn1_up", out_dtypes=(BF16, BF16), epilogue=relu_sq)
    f1 = carry(_matmul, 87.0, a1sq, ready(7), "nn", "ffn1_down")
    w_in0, w_out0, w1_0, w2_0, w_in1, w_out1, w1_1, w2_1 = W

    grads_big = [None] * len(big)

    def reduce_grad(u, g, seq):
        name, _, _, kind, n_parts, cost = big[u]
        _GradStream(traffic, u, name, kind, n_parts, cost, seq, g, place, grads_big)

    dx, df1, d_ffn_post_1, loss_row = carry(_loss_and_last_norm_bwd, 30.0, x3, f1, gain(ffn_post_g, 1), target, "loss")
    reduce_grad(7, carry(_matmul, 80.0, a1sq, df1, "tn", "ffn1_down_dw", out_dtypes=(BF16,)), 0)
    dz = carry(_matmul, 82.0, df1, w2_1, "nt", "ffn1_down_dx", out_dtypes=(BF16,), epilogue=relu_sq_bwd, epi=(a1,))
    reduce_grad(6, carry(_matmul, 80.0, h3, dz, "tn", "ffn1_up_dw", out_dtypes=(BF16,)), 1)
    dh = carry(_matmul, 87.0, dz, w1_1, "nt", "ffn1_up_dx")
    dx, d_ffn_pre_1, dm1, d_mix_post_1 = carry(_norms_bwd, 36.0, dx, dh, x3, gain(ffn_pre_g, 1), m1, gain(mix_post_g, 1), "norms_bwd3")

    reduce_grad(5, carry(_matmul, 24.0, y1, dm1, "tn", "mix1_out_dw", out_dtypes=(BF16,)), 2)
    dy1 = carry(_matmul, 25.0, dm1, w_out1, "nt", "mix1_out_dx")
    db, dcg, du, d_sc_conv_w = carry(_short_bwd, 41.0, dy1, z1, sc_conv_w_full, "short_bwd")
    dz1 = jnp.concatenate([db, dcg, du], axis=1)
    reduce_grad(4, carry(_matmul, 62.0, h2, dz1, "tn", "mix1_in_dw", out_dtypes=(BF16,)), 3)
    dh = carry(_matmul, 68.0, dz1, w_in1, "nt", "mix1_in_dx")
    dx, d_mix_pre_1, df0, d_ffn_post_0 = carry(_norms_bwd, 35.0, dx, dh, x2, gain(mix_pre_g, 1), f0, gain(ffn_post_g, 0), "norms_bwd2")

    reduce_grad(3, carry(_matmul, 80.0, a0sq, df0, "tn", "ffn0_down_dw", out_dtypes=(BF16,)), 4)
    dz = carry(_matmul, 82.0, df0, w2_0, "nt", "ffn0_down_dx", out_dtypes=(BF16,), epilogue=relu_sq_bwd, epi=(a0,))
    reduce_grad(2, carry(_matmul, 80.0, h1, dz, "tn", "ffn0_up_dw", out_dtypes=(BF16,)), 5)
    dh = carry(_matmul, 87.0, dz, w1_0, "nt", "ffn0_up_dx")
    dx, d_ffn_pre_0, dm0, d_mix_post_0 = carry(_norms_bwd, 36.0, dx, dh, x1, gain(ffn_pre_g, 0), m0, gain(mix_post_g, 0), "norms_bwd1")

    reduce_grad(1, carry(_matmul, 24.0, y0, dm0, "tn", "mix0_out_dw", out_dtypes=(BF16,)), 6)
    dy0 = carry(_matmul, 25.0, dm0, w_out0, "nt", "mix0_out_dx")
    du_pool, d_pool_w, d_pool_scale = carry(_pool_bwd, 28.0, dy0, pooled, pool_w_full, pool_scale, "pool_bwd")
    dc, d_ln_g, d_ln_b = carry(_ln_silu_bwd, 15.0, dy0, c_conv, conv_ln_g, conv_ln_b, "ln_silu_bwd")
    dv, dgate, d_conv_w, d_conv_b = carry(_conv_bwd, 52.0, dc, a_conv, z0, conv_w_full, DP, "conv_bwd")
    dz0 = jnp.concatenate([du_pool, dv, dgate], axis=1)

    device_slot = 2 * chip + lax.axis_index("c")

    def in_own_slot(packed):
        return lax.dynamic_update_slice_in_dim(jnp.zeros((N_DEVICES,) + packed.shape, F32), packed[None], device_slot, axis=0)

    small_grads = [d_mix_pre_1, jnp.concatenate([d_mix_post_0, d_mix_post_1], 0),
                   jnp.concatenate([d_ffn_pre_0, d_ffn_pre_1], 0), jnp.concatenate([d_ffn_post_0, d_ffn_post_1], 0),
                   d_pool_scale, d_conv_b, d_ln_g, d_ln_b, d_pool_w, d_conv_w, d_sc_conv_w]
    small_shapes = [a.shape for a in small_grads]
    small_stream = _SmallStream(traffic, (6, 3), in_own_slot(_pack(small_grads, D)), 4, 112.0)

    reduce_grad(0, carry(_matmul, 34.0, h0, dz0, "tn", "mix0_in_dw", out_dtypes=(BF16,)), 7)
    dh = carry(_matmul, 40.0, dz0, w_in0, "nt", "mix0_in_dx")
    grad_x, d_mix_pre_0 = _norms_bwd(dx, dh, x0, gain(mix_pre_g, 0), None, None, "norms_bwd0")
    last_stream = _SmallStream(traffic, (7, 3), in_own_slot(_pack([d_mix_pre_0], D)), 1, 5.0)
    traffic.flush("tail")

    loss = lax.psum(loss_row[0, 0], ("x", "y", "c"))
    summed = _unpack(_sum_slots(small_stream.slots, "small_grads_sum"), small_shapes)
    (g_mix_pre_1, g_mix_post, g_ffn_pre, g_ffn_post, g_pool_scale, g_conv_b, g_ln_g, g_ln_b, g_pool_w_full, g_conv_w_full,
     g_sc_conv_w_full) = summed
    (g_mix_pre_0,) = _unpack(_sum_slots(last_stream.slots, "last_grad_sum"), [d_mix_pre_0.shape])
    g_mix_pre = jnp.concatenate([g_mix_pre_0, g_mix_pre_1], 0)
    own = lambda a, ax: lax.dynamic_slice_in_dim(a, chip * (a.shape[ax] // N_CHIPS), a.shape[ax] // N_CHIPS, axis=ax)
    g_pool_w, g_conv_w, g_sc_conv_w = own(g_pool_w_full, 1), own(g_conv_w_full, 1), own(g_sc_conv_w_full, 1)

    gr = grads_big
    upd = {}
    first = _adamw(ffn_w2, gr[7], m_ffn_w2, v_ffn_w2, "adamw_ffn_w2_1", layer=1)
    upd["ffn_w2"] = _adamw(ffn_w2, gr[3], m_ffn_w2, v_ffn_w2, "adamw_ffn_w2_0", layer=0, carried=first)
    first = _adamw(ffn_w1, gr[6], m_ffn_w1, v_ffn_w1, "adamw_ffn_w1_1", layer=1)
    upd["ffn_w1"] = _adamw(ffn_w1, gr[2], m_ffn_w1, v_ffn_w1, "adamw_ffn_w1_0", layer=0, carried=first)
    upd["sc_w_in"] = _adamw(sc_w_in, gr[4], m_sc_w_in, v_sc_w_in, "adamw_sc_w_in")
    upd["sc_w_out"] = _adamw(sc_w_out, gr[5], m_sc_w_out, v_sc_w_out, "adamw_sc_w_out")
    upd["ab_w_out"] = _adamw(ab_w_out, gr[1], m_ab_w_out, v_ab_w_out, "adamw_ab_w_out")
    upd["ab_w_in"] = _adamw(ab_w_in, gr[0], m_ab_w_in, v_ab_w_in, "adamw_ab_w_in")

    def small_update(w, g, m, v, name):
        shape = w.shape
        as3 = lambda a: a.reshape((1, -1, shape[-1]))
        outs = _adamw(as3(w), g.reshape((-1, shape[-1])), as3(m), as3(v), "adamw_" + name)
        return [o.reshape(shape) for o in outs]

    upd["mix_pre_g"] = small_update(mix_pre_g, g_mix_pre, m_mix_pre_g, v_mix_pre_g, "mix_pre_g")
    upd["mix_post_g"] = small_update(mix_post_g, g_mix_post, m_mix_post_g, v_mix_post_g, "mix_post_g")
    upd["ffn_pre_g"] = small_update(ffn_pre_g, g_ffn_pre, m_ffn_pre_g, v_ffn_pre_g, "ffn_pre_g")
    upd["ffn_post_g"] = small_update(ffn_post_g, g_ffn_post, m_ffn_post_g, v_ffn_post_g, "ffn_post_g")
    upd["pool_w"] = small_update(pool_w, g_pool_w, m_pool_w, v_pool_w, "pool_w")
    upd["pool_scale"] = small_update(pool_scale, g_pool_scale, m_pool_scale, v_pool_scale, "pool_scale")
    upd["conv_w"] = small_update(conv_w, g_conv_w, m_conv_w, v_conv_w, "conv_w")
    upd["conv_b"] = small_update(conv_b, g_conv_b, m_conv_b, v_conv_b, "conv_b")
    upd["conv_ln_g"] = small_update(conv_ln_g, g_ln_g, m_conv_ln_g, v_conv_ln_g, "conv_ln_g")
    upd["conv_ln_b"] = small_update(conv_ln_b, g_ln_b, m_conv_ln_b, v_conv_ln_b, "conv_ln_b")
    upd["sc_conv_w"] = small_update(sc_conv_w, g_sc_conv_w, m_sc_conv_w, v_sc_conv_w, "sc_conv_w")

    order = ["mix_pre_g", "mix_post_g", "ffn_pre_g", "ffn_post_g", "ab_w_in", "pool_w", "pool_scale", "conv_w", "conv_b",
             "conv_ln_g", "conv_ln_b", "ab_w_out", "sc_w_in", "sc_conv_w", "sc_w_out", "ffn_w1", "ffn_w2"]
    out = [loss, grad_x[None]]
    for part in range(4):
        out += [upd[n][part] for n in order]
    return tuple(out)
```

```python
import jax
import jax.numpy as jnp
from jax import lax
from jax.experimental import pallas as pl
from jax.experimental.pallas import tpu as pltpu

F32, BF16 = jnp.float32, jnp.bfloat16
EPS = 1e-6
N_GROUPS = 4
MAX_WINDOW = 16
CONV_K = 31
SHORT_K = 3
CONV_PAD = 32
SHORT_PAD = 8
ADAM_LR, ADAM_B1, ADAM_B2, ADAM_EPS, ADAM_WD, ADAM_STEP = 0.001, 0.9, 0.999, 1e-08, 0.01, 10
N_CHIPS = 4
VMEM_LIMIT_BYTES = 56 * 1024 * 1024
ROW_TILE = 256
CHUNK = 256
MESH = pl.DeviceIdType.MESH
HBM = pl.BlockSpec(memory_space=pltpu.HBM)
SEM = pl.BlockSpec(memory_space=pltpu.SEMAPHORE)
ANY = pl.BlockSpec(memory_space=pl.ANY)


def _cp(*sem):
    return pltpu.CompilerParams(dimension_semantics=sem, vmem_limit_bytes=VMEM_LIMIT_BYTES)


def _sigmoid(v):
    return 1.0 / (1.0 + jnp.exp(-v))


class _Behind:
    pending = []


def _call(body, jobs, prefetch=None, **kw):
    beside = _Beside(jobs)
    behind, _Behind.pending = _Behind.pending, []
    grid = kw["grid"]
    single = not isinstance(kw["out_shape"], (list, tuple))
    in_specs, scratch = list(kw["in_specs"]), list(kw.get("scratch_shapes", ()))
    out_shape = [kw["out_shape"]] if single else list(kw["out_shape"])
    out_specs = [kw["out_specs"]] if single else list(kw["out_specs"])
    n_pre = 0 if prefetch is None else 1
    n_own, n_out, n_scr = len(in_specs), len(out_shape), len(scratch)
    in_specs += [ANY] * len(behind)
    n_in = len(in_specs)

    def wrapped(*refs):
        pre, refs = refs[:n_pre], refs[n_pre:]
        ins, job_ins = refs[:n_own], refs[n_in:n_in + beside.n_in]
        outs = refs[n_in + beside.n_in:n_in + beside.n_in + n_out]
        job_outs = refs[n_in + beside.n_in + n_out:n_in + beside.n_in + n_out + beside.n_out]
        scr = refs[n_in + beside.n_in + n_out + beside.n_out:]
        beside.start(job_ins, job_outs, scr, grid)
        body(*pre, *ins, *outs, *scr[:n_scr])
        beside.finish(job_ins, job_outs, scr, grid)

    aliases = {n_pre + i: o for i, o in kw.get("input_output_aliases", {}).items()}
    aliases.update(beside.aliases(n_pre + n_in, n_out))
    specs = dict(grid=grid, in_specs=in_specs + [HBM] * beside.n_in, out_specs=out_specs + [HBM] * beside.n_out)
    if prefetch is None:
        specs["scratch_shapes"] = scratch + beside.scratch()
    else:
        specs = dict(grid_spec=pltpu.PrefetchScalarGridSpec(num_scalar_prefetch=1, scratch_shapes=scratch + beside.scratch(), **specs))
    sem = ["arbitrary"] * len(grid) if jobs else kw["sem"]
    call = pl.pallas_call(wrapped, name=kw["name"], out_shape=out_shape + beside.out_shape, input_output_aliases=aliases,
                          compiler_params=_cp(*sem), **specs)

    def run(*args):
        outs = call(*([prefetch] * n_pre), *args, *behind, *beside.ins)
        own = outs[0] if single else list(outs[:n_out])
        return (own, beside.split(outs[n_out:])) if jobs else own

    return run


_DIMS = {"nn": (((1,), (0,)), ((), ())), "nt": (((1,), (1,)), ((), ())), "tn": (((0,), (0,)), ((), ()))}


def _pick(n, cap, step=256):
    if n <= cap:
        return n
    return next(t for t in range(cap - cap % step, 0, -step) if n % t == 0)


def _matmul(a, b, mode, name, out_dtypes=(F32,), epilogue=None, epi=(), jobs=(), tm=1024, tn=1024, tk=2048):
    if mode == "tn":
        (K, M), (K2, N) = a.shape, b.shape
    elif mode == "nt":
        (M, K), (N, K2) = a.shape, b.shape
    else:
        (M, K), (K2, N) = a.shape, b.shape
    assert K == K2
    tm, tn, tk = _pick(M, tm), _pick(N, tn), _pick(K, tk)
    nk = K // tk
    a_spec = pl.BlockSpec((tk, tm), lambda i, j, k: (k, i)) if mode == "tn" else pl.BlockSpec((tm, tk), lambda i, j, k: (i, k))
    b_spec = pl.BlockSpec((tn, tk), lambda i, j, k: (j, k)) if mode == "nt" else pl.BlockSpec((tk, tn), lambda i, j, k: (k, j))
    o_spec = pl.BlockSpec((tm, tn), lambda i, j, k: (i, j))
    n_epi, n_out = len(epi), len(out_dtypes)

    def body(a_ref, b_ref, *rest):
        epi_refs, out_refs, scratch = rest[:n_epi], rest[n_epi:n_epi + n_out], rest[n_epi + n_out:]
        part = lax.dot_general(a_ref[...].astype(BF16), b_ref[...].astype(BF16), _DIMS[mode], preferred_element_type=F32)

        def finish(acc):
            outs = epilogue(acc, *[r[...] for r in epi_refs]) if epilogue else (acc,)
            for o_ref, o in zip(out_refs, outs):
                o_ref[...] = o.astype(o_ref.dtype)

        if nk == 1:
            finish(part)
        else:
            acc_ref = scratch[0]
            k = pl.program_id(2)

            @pl.when(k == 0)
            def _():
                acc_ref[...] = part

            @pl.when(k > 0)
            def _():
                acc_ref[...] += part

            @pl.when(k == nk - 1)
            def _():
                finish(acc_ref[...])

    got = _call(
        body, jobs, name=name, grid=(M // tm, N // tn, nk),
        in_specs=[a_spec, b_spec] + [o_spec] * n_epi, out_specs=[o_spec] * n_out,
        out_shape=[jax.ShapeDtypeStruct((M, N), dt) for dt in out_dtypes],
        scratch_shapes=[pltpu.VMEM((tm, tn), F32)] if nk > 1 else [],
        sem=("parallel", "parallel", "arbitrary"))(a, b, *epi)
    outs, job_outs = got if jobs else (got, None)
    result = outs[0] if n_out == 1 else outs
    return (result, job_outs) if jobs else result


def _rms(x, g):
    r = lax.rsqrt(jnp.mean(x * x, axis=-1, keepdims=True) + EPS)
    return x * r * g


def _rms_bwd(dy, x, g):
    r = lax.rsqrt(jnp.mean(x * x, axis=-1, keepdims=True) + EPS)
    xn = x * r
    dyg = dy * g
    dx = r * (dyg - xn * jnp.mean(dyg * xn, axis=-1, keepdims=True))
    return dx, jnp.sum(dy * xn, axis=0, keepdims=True)


def _rows(d, tr=ROW_TILE):
    return pl.BlockSpec((tr, d), lambda i: (i, 0))


def _vec(d):
    return pl.BlockSpec((1, d), lambda i: (0, 0))


def _accumulate(ref, val):
    @pl.when(pl.program_id(0) == 0)
    def _():
        ref[...] = val

    @pl.when(pl.program_id(0) > 0)
    def _():
        ref[...] += val


def _norm_fwd(x, g, name, jobs=()):
    T, D = x.shape

    def body(x_ref, g_ref, h_ref):
        h_ref[...] = _rms(x_ref[...], g_ref[...]).astype(BF16)

    return _call(body, jobs, name=name,grid=(T // ROW_TILE,), in_specs=[_rows(D), _vec(D)], out_specs=_rows(D),
                          out_shape=jax.ShapeDtypeStruct((T, D), BF16), sem=("parallel",))(x, g)


def _residual_norm(x, m, g_post, g_next, name, jobs=()):
    T, D = x.shape

    def body(x_ref, m_ref, gp_ref, gn_ref, xo_ref, h_ref):
        xo = x_ref[...] + _rms(m_ref[...], gp_ref[...])
        xo_ref[...] = xo
        h_ref[...] = _rms(xo, gn_ref[...]).astype(BF16)

    return _call(body, jobs, name=name,grid=(T // ROW_TILE,), in_specs=[_rows(D), _rows(D), _vec(D), _vec(D)],
                          out_specs=[_rows(D), _rows(D)],
                          out_shape=[jax.ShapeDtypeStruct((T, D), F32), jax.ShapeDtypeStruct((T, D), BF16)],
                          sem=("parallel",))(x, m, g_post, g_next)


def _loss_and_last_norm_bwd(x, m, g_post, target, name, jobs=()):
    T, D = x.shape

    def body(x_ref, m_ref, gp_ref, t_ref, dx_ref, dm_ref, dg_ref, loss_ref):
        m_val, gp = m_ref[...], gp_ref[...]
        err = x_ref[...] + _rms(m_val, gp) - t_ref[...]
        dx = err * (1.0 / D)
        dx_ref[...] = dx
        dm, dg = _rms_bwd(dx, m_val, gp)
        dm_ref[...] = dm.astype(BF16)
        _accumulate(dg_ref, dg)
        _accumulate(loss_ref, jnp.full((1, 128), 0.5 * jnp.sum(err * err) * (1.0 / D), F32))

    return _call(body, jobs, name=name,grid=(T // ROW_TILE,), in_specs=[_rows(D), _rows(D), _vec(D), _rows(D)],
                          out_specs=[_rows(D), _rows(D), _vec(D), _vec(128)],
                          out_shape=[jax.ShapeDtypeStruct((T, D), F32), jax.ShapeDtypeStruct((T, D), BF16),
                                     jax.ShapeDtypeStruct((1, D), F32), jax.ShapeDtypeStruct((1, 128), F32)],
                          sem=("arbitrary",))(x, m, g_post, target)


def _norms_bwd(dx, dh, x_in, g_pre, m_prev, g_post_prev, name, jobs=()):
    T, D = dx.shape
    with_prev = m_prev is not None

    def body(*refs):
        if with_prev:
            dx_ref, dh_ref, x_ref, gq_ref, m_ref, gp_ref, dxo_ref, dgq_ref, dm_ref, dgp_ref = refs
        else:
            dx_ref, dh_ref, x_ref, gq_ref, dxo_ref, dgq_ref = refs
        d_in, dgq = _rms_bwd(dh_ref[...], x_ref[...], gq_ref[...])
        dxo = dx_ref[...] + d_in
        dxo_ref[...] = dxo
        _accumulate(dgq_ref, dgq)
        if with_prev:
            dm, dgp = _rms_bwd(dxo, m_ref[...], gp_ref[...])
            dm_ref[...] = dm.astype(BF16)
            _accumulate(dgp_ref, dgp)

    ins, in_specs = [dx, dh, x_in, g_pre], [_rows(D), _rows(D), _rows(D), _vec(D)]
    out_specs = [_rows(D), _vec(D)]
    out_shape = [jax.ShapeDtypeStruct((T, D), F32), jax.ShapeDtypeStruct((1, D), F32)]
    if with_prev:
        ins += [m_prev, g_post_prev]
        in_specs += [_rows(D), _vec(D)]
        out_specs += [_rows(D), _vec(D)]
        out_shape += [jax.ShapeDtypeStruct((T, D), BF16), jax.ShapeDtypeStruct((1, D), F32)]
    return _call(body, jobs, name=name,grid=(T // ROW_TILE,), in_specs=in_specs, out_specs=out_specs, out_shape=out_shape,
                          sem=("arbitrary",))(*ins)


def _window_weights(g):
    w = 2 << g
    return w, [jnp.where(j < w, 1.0, 0.0).astype(F32) for j in range(MAX_WINDOW)]


def _valid_count(r0, rows, w):
    t = (lax.broadcasted_iota(jnp.int32, (rows, 1), 0) + (r0 + 1)).astype(F32)
    return jnp.minimum(t, w.astype(F32))


def _pool_fwd(z, pool_w, pool_scale, name, jobs=()):
    T = z.shape[0]
    PG = pool_w.shape[-1]
    DP = N_GROUPS * PG
    rc = min(CHUNK, T)

    def body(u_ref, pw_ref, sc_ref, pooled_ref, y_ref, pad):
        w, wts = _window_weights(pl.program_id(0))
        pad[pl.ds(0, MAX_WINDOW), :] = jnp.zeros((MAX_WINDOW, PG), F32)
        pad[pl.ds(MAX_WINDOW, T), :] = u_ref[...]
        for r0 in range(0, T, rc):
            acc = jnp.zeros((rc, PG), F32)
            for j in range(MAX_WINDOW):
                acc = acc + wts[j] * pad[pl.ds(MAX_WINDOW + r0 - j, rc), :]
            pooled = acc / _valid_count(r0, rc, w) - u_ref[pl.ds(r0, rc), :]
            pooled_ref[pl.ds(r0, rc), :] = pooled.astype(BF16)
        mixed = jnp.dot(pooled_ref[...], pw_ref[...].astype(BF16), preferred_element_type=F32)
        y_ref[...] = (mixed * sc_ref[...]).astype(BF16)

    col = lambda g: (0, g)
    return _call(
        body, jobs, name=name,grid=(N_GROUPS,),
        in_specs=[pl.BlockSpec((T, PG), col), pl.BlockSpec((None, PG, PG), lambda g: (g, 0, 0)), pl.BlockSpec((1, PG), col)],
        out_specs=[pl.BlockSpec((T, PG), col), pl.BlockSpec((T, PG), col)],
        out_shape=[jax.ShapeDtypeStruct((T, DP), BF16), jax.ShapeDtypeStruct((T, DP), BF16)],
        scratch_shapes=[pltpu.VMEM((T + MAX_WINDOW, PG), F32)], sem=("parallel",))(z, pool_w, pool_scale)


def _pool_bwd(dy, pooled, pool_w, pool_scale, name, jobs=()):
    T = dy.shape[0]
    PG = pool_w.shape[-1]
    DP = N_GROUPS * PG
    rc = min(CHUNK, T)

    def body(dy_ref, pooled_ref, pw_ref, sc_ref, du_ref, dpw_ref, dsc_ref, pad, dp_ref):
        w, wts = _window_weights(pl.program_id(0))
        pooled_v, pw = pooled_ref[...], pw_ref[...].astype(BF16)
        dy_v = dy_ref[...]
        mixed = jnp.dot(pooled_v, pw, preferred_element_type=F32)
        dsc_ref[...] = jnp.sum(dy_v * mixed, axis=0, keepdims=True)
        dmixed = (dy_v * sc_ref[...]).astype(BF16)
        dpw_ref[...] = lax.dot_general(pooled_v, dmixed, _DIMS["tn"], preferred_element_type=F32)
        dp_ref[...] = lax.dot_general(dmixed, pw, _DIMS["nt"], preferred_element_type=F32)
        pad[pl.ds(T, MAX_WINDOW), :] = jnp.zeros((MAX_WINDOW, PG), F32)
        for r0 in range(0, T, rc):
            pad[pl.ds(r0, rc), :] = dp_ref[pl.ds(r0, rc), :] / _valid_count(r0, rc, w)
        for r0 in range(0, T, rc):
            acc = jnp.zeros((rc, PG), F32)
            for j in range(MAX_WINDOW):
                acc = acc + wts[j] * pad[pl.ds(r0 + j, rc), :]
            du_ref[pl.ds(r0, rc), :] = (acc - dp_ref[pl.ds(r0, rc), :]).astype(BF16)

    col = lambda g: (0, g)
    return _call(
        body, jobs, name=name,grid=(N_GROUPS,),
        in_specs=[pl.BlockSpec((T, PG), col), pl.BlockSpec((T, PG), col), pl.BlockSpec((None, PG, PG), lambda g: (g, 0, 0)),
                  pl.BlockSpec((1, PG), col)],
        out_specs=[pl.BlockSpec((T, PG), col), pl.BlockSpec((None, PG, PG), lambda g: (g, 0, 0)), pl.BlockSpec((1, PG), col)],
        out_shape=[jax.ShapeDtypeStruct((T, DP), BF16), jax.ShapeDtypeStruct((N_GROUPS, PG, PG), F32),
                   jax.ShapeDtypeStruct((1, DP), F32)],
        scratch_shapes=[pltpu.VMEM((T + MAX_WINDOW, PG), F32), pltpu.VMEM((T, PG), F32)],
        sem=("parallel",))(dy, pooled, pool_w, pool_scale)


def _conv_fwd(z, conv_w, conv_b, d_pool, name, tc=128, jobs=()):
    T = z.shape[0]
    DC = conv_w.shape[-1]
    rc = min(CHUNK, T)
    v0, g0 = d_pool // tc, (d_pool + DC) // tc

    def body(v_ref, gt_ref, w_ref, b_ref, a_ref, c_ref, pad):
        pad[pl.ds(0, CONV_PAD), :] = jnp.zeros((CONV_PAD, tc), F32)
        for r0 in range(0, T, rc):
            a = v_ref[pl.ds(r0, rc), :] * _sigmoid(gt_ref[pl.ds(r0, rc), :])
            a_ref[pl.ds(r0, rc), :] = a
            pad[pl.ds(CONV_PAD + r0, rc), :] = a
        for r0 in range(0, T, rc):
            acc = jnp.zeros((rc, tc), F32) + b_ref[...]
            for k in range(CONV_K):
                acc = acc + w_ref[pl.ds(k, 1), :] * pad[pl.ds(CONV_PAD - (CONV_K - 1) + k + r0, rc), :]
            c_ref[pl.ds(r0, rc), :] = acc

    col = lambda j: (0, j)
    return _call(
        body, jobs, name=name,grid=(DC // tc,),
        in_specs=[pl.BlockSpec((T, tc), lambda j: (0, v0 + j)), pl.BlockSpec((T, tc), lambda j: (0, g0 + j)),
                  pl.BlockSpec((CONV_K, tc), col), pl.BlockSpec((1, tc), col)],
        out_specs=[pl.BlockSpec((T, tc), col), pl.BlockSpec((T, tc), col)],
        out_shape=[jax.ShapeDtypeStruct((T, DC), F32), jax.ShapeDtypeStruct((T, DC), F32)],
        scratch_shapes=[pltpu.VMEM((T + CONV_PAD, tc), F32)], sem=("parallel",))(z, z, conv_w, conv_b)


def _conv_bwd(dc, a, z, conv_w, d_pool, name, tc=128, jobs=()):
    T, DC = dc.shape
    rc = min(CHUNK, T)
    v0, g0 = d_pool // tc, (d_pool + DC) // tc

    def body(dc_ref, a_ref, v_ref, gt_ref, w_ref, dv_ref, dg_ref, dw_ref, db_ref, apad, dpad):
        apad[pl.ds(0, CONV_PAD), :] = jnp.zeros((CONV_PAD, tc), F32)
        apad[pl.ds(CONV_PAD, T), :] = a_ref[...]
        dpad[pl.ds(0, T), :] = dc_ref[...]
        dpad[pl.ds(T, CONV_PAD), :] = jnp.zeros((CONV_PAD, tc), F32)
        db_ref[...] = jnp.sum(dc_ref[...], axis=0, keepdims=True)
        for k in range(CONV_K):
            acc = jnp.zeros((8, tc), F32)
            for r0 in range(0, T, rc):
                prod = dc_ref[pl.ds(r0, rc), :] * apad[pl.ds(CONV_PAD - (CONV_K - 1) + k + r0, rc), :]
                acc = acc + jnp.sum(prod.reshape(rc // 8, 8, tc), axis=0)
            dw_ref[pl.ds(k, 1), :] = jnp.sum(acc, axis=0, keepdims=True)
        for r0 in range(0, T, rc):
            da = jnp.zeros((rc, tc), F32)
            for k in range(CONV_K):
                da = da + w_ref[pl.ds(k, 1), :] * dpad[pl.ds(r0 + (CONV_K - 1) - k, rc), :]
            sig = _sigmoid(gt_ref[pl.ds(r0, rc), :])
            dv_ref[pl.ds(r0, rc), :] = (da * sig).astype(BF16)
            dg_ref[pl.ds(r0, rc), :] = (da * v_ref[pl.ds(r0, rc), :] * sig * (1.0 - sig)).astype(BF16)

    col = lambda j: (0, j)
    return _call(
        body, jobs, name=name,grid=(DC // tc,),
        in_specs=[pl.BlockSpec((T, tc), col), pl.BlockSpec((T, tc), col), pl.BlockSpec((T, tc), lambda j: (0, v0 + j)),
                  pl.BlockSpec((T, tc), lambda j: (0, g0 + j)), pl.BlockSpec((CONV_K, tc), col)],
        out_specs=[pl.BlockSpec((T, tc), col), pl.BlockSpec((T, tc), col), pl.BlockSpec((CONV_K, tc), col),
                   pl.BlockSpec((1, tc), col)],
        out_shape=[jax.ShapeDtypeStruct((T, DC), BF16), jax.ShapeDtypeStruct((T, DC), BF16),
                   jax.ShapeDtypeStruct((CONV_K, DC), F32), jax.ShapeDtypeStruct((1, DC), F32)],
        scratch_shapes=[pltpu.VMEM((T + CONV_PAD, tc), F32), pltpu.VMEM((T + CONV_PAD, tc), F32)],
        sem=("parallel",))(dc, a, z, z, conv_w)


def _layer_norm_parts(c, g, b):
    mu = jnp.mean(c, axis=-1, keepdims=True)
    xc = c - mu
    rstd = lax.rsqrt(jnp.mean(xc * xc, axis=-1, keepdims=True) + EPS)
    xhat = xc * rstd
    return xhat, rstd, xhat * g + b


def _ln_silu_fwd(c, g, b, name, jobs=()):
    T, DC = c.shape

    def body(c_ref, g_ref, b_ref, y_ref):
        _, _, ln = _layer_norm_parts(c_ref[...], g_ref[...], b_ref[...])
        y_ref[...] = (ln * _sigmoid(ln)).astype(BF16)

    return _call(body, jobs, name=name,grid=(T // ROW_TILE,), in_specs=[_rows(DC), _vec(DC), _vec(DC)], out_specs=_rows(DC),
                          out_shape=jax.ShapeDtypeStruct((T, DC), BF16), sem=("parallel",))(c, g, b)


def _ln_silu_bwd(dy, c, g, b, name, jobs=()):
    T, DC = c.shape

    def body(dy_ref, c_ref, g_ref, b_ref, dc_ref, dg_ref, db_ref):
        gain = g_ref[...]
        xhat, rstd, ln = _layer_norm_parts(c_ref[...], gain, b_ref[...])
        s = _sigmoid(ln)
        dln = dy_ref[...] * (s * (1.0 + ln * (1.0 - s)))
        _accumulate(dg_ref, jnp.sum(dln * xhat, axis=0, keepdims=True))
        _accumulate(db_ref, jnp.sum(dln, axis=0, keepdims=True))
        dxh = dln * gain
        dc_ref[...] = rstd * (dxh - jnp.mean(dxh, axis=-1, keepdims=True) - xhat * jnp.mean(dxh * xhat, axis=-1, keepdims=True))

    return _call(body, jobs, name=name,grid=(T // ROW_TILE,),
                          in_specs=[pl.BlockSpec((ROW_TILE, DC), lambda i: (i, 1)), _rows(DC), _vec(DC), _vec(DC)],
                          out_specs=[_rows(DC), _vec(DC), _vec(DC)],
                          out_shape=[jax.ShapeDtypeStruct((T, DC), F32), jax.ShapeDtypeStruct((1, DC), F32),
                                     jax.ShapeDtypeStruct((1, DC), F32)],
                          sem=("arbitrary",))(dy, c, g, b)


def _short_specs(T, DS, tc):
    n = DS // tc
    return [pl.BlockSpec((T, tc), lambda j: (0, j)), pl.BlockSpec((T, tc), lambda j: (0, n + j)),
            pl.BlockSpec((T, tc), lambda j: (0, 2 * n + j))]


def _short_fwd(z, w, name, tc=256, jobs=()):
    T = z.shape[0]
    DS = w.shape[-1]
    rc = min(CHUNK, T)

    def body(b_ref, cg_ref, u_ref, w_ref, y_ref, pad):
        pad[pl.ds(0, SHORT_PAD), :] = jnp.zeros((SHORT_PAD, tc), F32)
        pad[pl.ds(SHORT_PAD, T), :] = cg_ref[...] * u_ref[...]
        for r0 in range(0, T, rc):
            r = jnp.zeros((rc, tc), F32)
            for k in range(SHORT_K):
                r = r + w_ref[pl.ds(k, 1), :] * pad[pl.ds(SHORT_PAD - (SHORT_K - 1) + k + r0, rc), :]
            y_ref[pl.ds(r0, rc), :] = (b_ref[pl.ds(r0, rc), :] * r).astype(BF16)

    col = lambda j: (0, j)
    return _call(body, jobs, name=name,grid=(DS // tc,), in_specs=_short_specs(T, DS, tc) + [pl.BlockSpec((SHORT_K, tc), col)],
                          out_specs=pl.BlockSpec((T, tc), col), out_shape=jax.ShapeDtypeStruct((T, DS), BF16),
                          scratch_shapes=[pltpu.VMEM((T + SHORT_PAD, tc), F32)], sem=("parallel",))(z, z, z, w)


def _short_bwd(dy, z, w, name, tc=256, jobs=()):
    T, DS = dy.shape
    rc = min(CHUNK, T)

    def body(dy_ref, b_ref, cg_ref, u_ref, w_ref, db_ref, dcg_ref, du_ref, dw_ref, qpad, rpad):
        qpad[pl.ds(0, SHORT_PAD), :] = jnp.zeros((SHORT_PAD, tc), F32)
        qpad[pl.ds(SHORT_PAD, T), :] = cg_ref[...] * u_ref[...]
        rpad[pl.ds(0, T), :] = dy_ref[...] * b_ref[...]
        rpad[pl.ds(T, SHORT_PAD), :] = jnp.zeros((SHORT_PAD, tc), F32)
        accs = [jnp.zeros((8, tc), F32) for _ in range(SHORT_K)]
        for r0 in range(0, T, rc):
            r = jnp.zeros((rc, tc), F32)
            dq = jnp.zeros((rc, tc), F32)
            dr = rpad[pl.ds(r0, rc), :]
            for k in range(SHORT_K):
                q_k = qpad[pl.ds(SHORT_PAD - (SHORT_K - 1) + k + r0, rc), :]
                r = r + w_ref[pl.ds(k, 1), :] * q_k
                dq = dq + w_ref[pl.ds(k, 1), :] * rpad[pl.ds(r0 + (SHORT_K - 1) - k, rc), :]
                accs[k] = accs[k] + jnp.sum((dr * q_k).reshape(rc // 8, 8, tc), axis=0)
            db_ref[pl.ds(r0, rc), :] = (dy_ref[pl.ds(r0, rc), :] * r).astype(BF16)
            dcg_ref[pl.ds(r0, rc), :] = (dq * u_ref[pl.ds(r0, rc), :]).astype(BF16)
            du_ref[pl.ds(r0, rc), :] = (dq * cg_ref[pl.ds(r0, rc), :]).astype(BF16)
        for k in range(SHORT_K):
            dw_ref[pl.ds(k, 1), :] = jnp.sum(accs[k], axis=0, keepdims=True)

    col = lambda j: (0, j)
    tile = pl.BlockSpec((T, tc), col)
    return _call(body, jobs, name=name,grid=(DS // tc,),
                          in_specs=[tile] + _short_specs(T, DS, tc) + [pl.BlockSpec((SHORT_K, tc), col)],
                          out_specs=[tile, tile, tile, pl.BlockSpec((SHORT_K, tc), col)],
                          out_shape=[jax.ShapeDtypeStruct((T, DS), BF16)] * 3 + [jax.ShapeDtypeStruct((SHORT_K, DS), F32)],
                          scratch_shapes=[pltpu.VMEM((T + SHORT_PAD, tc), F32), pltpu.VMEM((T + SHORT_PAD, tc), F32)],
                          sem=("parallel",))(dy, z, z, z, w)


def _tile_rows(rows, cols, n_bufs):
    budget = VMEM_LIMIT_BYTES // 2 // (2 * n_bufs * 4 * cols)
    tr = rows
    while tr > budget and tr % 16 == 0:
        tr //= 2
    return tr


def _placed_call(body, name, place, grid, in_specs, out_specs, out_shape, ins, jobs=()):
    return _call(body, jobs, prefetch=place, name=name, grid=grid, in_specs=in_specs, out_specs=out_specs, out_shape=out_shape,
                 sem=("parallel",))(*ins)


def _cast_into_full(w, layer, kind, place, name, jobs=()):
    _, R, C = w.shape
    tr = _tile_rows(R, C, 2)
    nb = R // tr
    if kind == "col":
        full, out_spec = (R, C * N_CHIPS), pl.BlockSpec((tr, C), lambda i, s: (i, s[0]))
    else:
        full, out_spec = (R * N_CHIPS, C), pl.BlockSpec((tr, C), lambda i, s: (s[0] * nb + i, 0))

    def body(s_ref, w_ref, o_ref):
        o_ref[...] = w_ref[...].astype(BF16)

    return _placed_call(body, name, place, (nb,), [pl.BlockSpec((None, tr, C), lambda i, s: (layer, i, 0))], out_spec,
                        jax.ShapeDtypeStruct(full, BF16), [w], jobs)


def _add_pair(grad, theirs, kind, place, name, jobs=()):
    R, C = grad.shape
    piece_rows = R // 2 if kind == "col" else R // N_CHIPS // 2
    tr = _tile_rows(piece_rows, C, 3)
    nb = piece_rows // tr
    if kind == "col":
        g_spec = pl.BlockSpec((tr, C), lambda i, s: (s[1] * nb + i, 0))
    else:
        g_spec = pl.BlockSpec((tr, C), lambda i, s: ((2 * (i // nb) + s[1]) * nb + i % nb, 0))
    flat = pl.BlockSpec((tr, C), lambda i, s: (i, 0))

    def body(s_ref, a_ref, b_ref, o_ref):
        o_ref[...] = (a_ref[...].astype(F32) + b_ref[...].astype(F32)).astype(BF16)

    return _placed_call(body, name, place, (R // 2 // tr,), [g_spec, flat], flat, jax.ShapeDtypeStruct((R // 2, C), BF16),
                        [grad, theirs], jobs)


def _sum_chips(chip_sum, arrived, kind, place, name, jobs=()):
    _, H, W = arrived.shape
    tr = _tile_rows(H, W, 6)
    nb = H // tr
    if kind == "col":
        own_spec = pl.BlockSpec((tr, W), lambda i, s: (i, s[0]))
    else:
        own_spec = pl.BlockSpec((tr, W), lambda i, s: (s[0] * nb + i, 0))

    def body(s_ref, p_ref, r_ref, o_ref):
        acc = p_ref[...].astype(F32)
        for i in range(N_CHIPS - 1):
            acc = acc + r_ref[i].astype(F32)
        o_ref[...] = acc

    return _placed_call(body, name, place, (nb,), [own_spec, pl.BlockSpec((N_CHIPS - 1, tr, W), lambda i, s: (0, i, 0))],
                        pl.BlockSpec((tr, W), lambda i, s: (s[1] * nb + i, 0)), jax.ShapeDtypeStruct((2 * H, W), F32),
                        [chip_sum, arrived], jobs)


def _adamw_values(w, g, m, v):
    m = ADAM_B1 * m + (1.0 - ADAM_B1) * g
    v = ADAM_B2 * v + (1.0 - ADAM_B2) * (g * g)
    m_hat = m / (1.0 - ADAM_B1 ** ADAM_STEP)
    v_hat = v / (1.0 - ADAM_B2 ** ADAM_STEP)
    return -ADAM_LR * (m_hat / (jnp.sqrt(v_hat) + ADAM_EPS) + ADAM_WD * w), m, v


def _adamw(w, g, m, v, name, layer=0, carried=None):
    L, R, C = w.shape
    tr = _tile_rows(R, C, 8)

    def body(w_ref, g_ref, m_ref, v_ref, *rest):
        go_ref, d_ref, mo_ref, vo_ref = rest[-4:]
        g_val = g_ref[...]
        d, m_new, v_new = _adamw_values(w_ref[...], g_val, m_ref[...], v_ref[...])
        go_ref[...], d_ref[...], mo_ref[...], vo_ref[...] = g_val, d, m_new, v_new

    lay = pl.BlockSpec((None, tr, C), lambda i: (layer, i, 0))
    ins = [w, g, m, v]
    in_specs = [lay, pl.BlockSpec((tr, C), lambda i: (i, 0)), lay, lay]
    aliases = {}
    if carried is not None:
        ins += list(carried)
        in_specs += [pl.BlockSpec(memory_space=pl.ANY)] * 4
        aliases = {4 + i: i for i in range(4)}
    return _call(body, (), name=name, grid=(R // tr,), in_specs=in_specs, out_specs=[lay] * 4,
                 out_shape=[jax.ShapeDtypeStruct((L, R, C), F32)] * 4, input_output_aliases=aliases, sem=("parallel",))(*ins)


def _aligned(v, m):
    return v if isinstance(v, int) else pl.multiple_of(v, m)


def _place():
    x, y, c = lax.axis_index("x"), lax.axis_index("y"), lax.axis_index("c")
    other_chips = [(x, 1 - y), (1 - x, y), (1 - x, 1 - y)]
    return x, y, c, 2 * x + y, other_chips


def _chip_index(chip):
    return 2 * chip[0] + chip[1]


def _piece(ref, kind, k, h):
    R, C = ref.shape
    if kind == "col":
        return ref.at[pl.ds(_aligned(h * (R // 2), 16), R // 2), pl.ds(_aligned(k * (C // N_CHIPS), 128), C // N_CHIPS)]
    rs = R // N_CHIPS
    return ref.at[pl.ds(_aligned(k * rs + h * (rs // 2), 16), rs // 2), :]


def _compact_piece(ref, kind, k):
    R2, C = ref.shape
    if kind == "col":
        return ref.at[:, pl.ds(_aligned(k * (C // N_CHIPS), 128), C // N_CHIPS)]
    return ref.at[pl.ds(_aligned(k * (R2 // N_CHIPS), 16), R2 // N_CHIPS), :]


def _half_rows(ref, h):
    R = ref.shape[0]
    return ref.at[pl.ds(_aligned(h * (R // 2), 16), R // 2), :]


class _Copies:
    def __init__(self, send_sems, recv_sems):
        self.send_sems, self.recv_sems = send_sems, recv_sems
        self.n_remote = 0

    def remote(self, src, dst, device):
        k = self.n_remote
        self.n_remote += 1
        return pltpu.make_async_remote_copy(src_ref=src, dst_ref=dst, send_sem=self.send_sems.at[k], recv_sem=self.recv_sems.at[k],
                                            device_id=device, device_id_type=MESH)


class _Job:
    def __init__(self, ins, out_shape, aliases, n_remote, build):
        self.ins, self.out_shape, self.aliases, self.n_remote, self.build = list(ins), list(out_shape), dict(aliases), n_remote, build


def _merge_jobs(jobs):
    ins, out_shape, aliases, spans = [], [], {}, []
    for job in jobs:
        spans.append((len(ins), len(job.ins), len(out_shape), len(job.out_shape)))
        aliases.update({len(ins) + i: len(out_shape) + o for i, o in job.aliases.items()})
        ins += job.ins
        out_shape += job.out_shape

    def build(in_refs, out_refs, cp):
        copies = []
        for job, (i0, ni, o0, no) in zip(jobs, spans):
            copies += job.build(in_refs[i0:i0 + ni], out_refs[o0:o0 + no], cp)
        return copies

    return _Job(ins, out_shape, aliases, sum(j.n_remote for j in jobs), build)


def _run_jobs(jobs, name):
    job = _merge_jobs(jobs)
    n_in, n_out = len(job.ins), len(job.out_shape)

    def body(*refs):
        copies = job.build(refs[:n_in], refs[n_in:n_in + n_out], _Copies(*refs[n_in + n_out:]))
        for d in copies:
            d.start()
        for d in copies:
            d.wait_recv()
        for d in copies:
            d.wait_send()

    outs = pl.pallas_call(
        body, name=name, in_specs=[HBM] * n_in, out_specs=[HBM] * n_out, out_shape=job.out_shape,
        input_output_aliases=job.aliases,
        scratch_shapes=[pltpu.SemaphoreType.DMA((job.n_remote,)), pltpu.SemaphoreType.DMA((job.n_remote,))])(*job.ins)
    split, o0 = [], 0
    for j in jobs:
        split.append(outs[o0:o0 + len(j.out_shape)])
        o0 += len(j.out_shape)
    return split


class _Flying:
    def __init__(self, job, send_sems, recv_sems, bufs, token):
        self.job, self.send_sems, self.recv_sems, self.bufs, self.token = job, send_sems, recv_sems, bufs, token


def _job_refs(job, buf_refs):
    n_out = len(job.out_shape)
    kept = [i for i in range(len(job.ins)) if i not in job.aliases]
    ins = [buf_refs[job.aliases[i]] if i in job.aliases else buf_refs[n_out + kept.index(i)] for i in range(len(job.ins))]
    return ins, list(buf_refs[:n_out])


def _start_job(job, name, after=()):
    n_in, n_out, n_after = len(job.ins), len(job.out_shape), len(after)
    kept = [i for i in range(n_in) if i not in job.aliases]
    n_bufs = n_out + len(kept)

    def body(*refs):
        in_refs, out_refs = refs[:n_in], refs[n_in + n_after:n_in + n_after + n_out]
        send_sems, recv_sems, token = refs[n_in + n_after + n_bufs:]
        for d in job.build(in_refs, out_refs, _Copies(send_sems, recv_sems)):
            d.start()
        token[...] = jnp.zeros_like(token)

    aliases = dict(job.aliases)
    aliases.update({i: n_out + k for k, i in enumerate(kept)})
    sems = pltpu.SemaphoreType.DMA((job.n_remote,))
    outs = pl.pallas_call(
        body, name=name, in_specs=[HBM] * n_in + [ANY] * n_after,
        out_specs=[HBM] * n_bufs + [SEM, SEM, pl.BlockSpec(memory_space=pltpu.VMEM)],
        out_shape=job.out_shape + [jax.ShapeDtypeStruct(job.ins[i].shape, job.ins[i].dtype) for i in kept]
        + [sems, sems, jax.ShapeDtypeStruct((8, 128), F32)],
        input_output_aliases=aliases,
        compiler_params=pltpu.CompilerParams(has_side_effects=pltpu.SideEffectType.DATAFLOW_SIDE_EFFECTING))(*job.ins, *after)
    return _Flying(job, outs[n_bufs], outs[n_bufs + 1], list(outs[:n_bufs]), outs[n_bufs + 2])


def _wait_job(flying, name, after=()):
    job, n_bufs, n_after = flying.job, len(flying.bufs), len(after)

    def body(*refs):
        in_refs, out_refs = _job_refs(job, refs[:n_bufs])
        send_sems, recv_sems = refs[n_bufs:n_bufs + 2]
        copies = job.build(in_refs, out_refs, _Copies(send_sems, recv_sems))
        for d in copies:
            d.wait_send()
        for d in copies:
            d.wait_recv()

    outs = pl.pallas_call(
        body, name=name, in_specs=[HBM] * n_bufs + [SEM, SEM] + [ANY] * n_after, out_specs=[HBM] * n_bufs,
        out_shape=[jax.ShapeDtypeStruct(b.shape, b.dtype) for b in flying.bufs],
        input_output_aliases={i: i for i in range(n_bufs)},
        compiler_params=pltpu.CompilerParams(has_side_effects=pltpu.SideEffectType.DATAFLOW_SIDE_EFFECTING))(
            *flying.bufs, flying.send_sems, flying.recv_sems, *after)
    return list(outs[:len(job.out_shape)])


class _Beside:
    def __init__(self, jobs):
        self.jobs = list(jobs)
        job = self.job = _merge_jobs(self.jobs) if self.jobs else None
        self.ins = job.ins if job else []
        self.out_shape = job.out_shape if job else []
        self.n_in, self.n_out = len(self.ins), len(self.out_shape)

    def scratch(self):
        n = self.job.n_remote if self.job else 0
        return [pltpu.SemaphoreType.DMA((n,)), pltpu.SemaphoreType.DMA((n,))] if n else []

    def aliases(self, first_in, first_out):
        return {first_in + i: first_out + o for i, o in self.job.aliases.items()} if self.job else {}

    def _at(self, grid, which):
        cond = None
        for axis, n in enumerate(grid):
            here = pl.program_id(axis) == (0 if which == "first" else n - 1)
            cond = here if cond is None else jnp.logical_and(cond, here)
        return cond

    def start(self, in_refs, out_refs, scratch, grid):
        if self.job:
            @pl.when(self._at(grid, "first"))
            def _():
                for d in self.job.build(in_refs, out_refs, _Copies(*scratch[-2:])):
                    d.start()

    def finish(self, in_refs, out_refs, scratch, grid):
        if self.job:
            @pl.when(self._at(grid, "last"))
            def _():
                copies = self.job.build(in_refs, out_refs, _Copies(*scratch[-2:]))
                for d in copies:
                    d.wait_recv()
                for d in copies:
                    d.wait_send()

    def split(self, outs):
        per_job, o0 = [], 0
        for j in self.jobs:
            per_job.append(list(outs[o0:o0 + len(j.out_shape)]))
            o0 += len(j.out_shape)
        return per_job


def _in_place(arrays):
    return [jax.ShapeDtypeStruct(a.shape, a.dtype) for a in arrays], {u: u for u in range(len(arrays))}


def _rows_part(ref, part, n_parts):
    h = ref.shape[0] // n_parts
    return ref.at[pl.ds(part * h, h), :]


def _gather_job(full, kind, n_parts, ici_parts=(), forward_parts=()):
    def build(in_refs, out_refs, cp):
        x, y, c, me, chips = _place()
        (ref,) = out_refs
        copies = []
        for p in ici_parts:
            mine = _rows_part(_piece(ref, kind, me, c), p, n_parts)
            copies += [cp.remote(mine, mine, (*chip, c)) for chip in chips]
        for p in forward_parts:
            for chip in chips:
                piece = _rows_part(_piece(ref, kind, _chip_index(chip), c), p, n_parts)
                copies.append(cp.remote(piece, piece, (x, y, 1 - c)))
        return copies

    return _Job([full], *_in_place([full]), 3 * (len(ici_parts) + len(forward_parts)), build)


def _gather_small_job(fulls, axes):
    def build(in_refs, out_refs, cp):
        x, y, c, me, chips = _place()
        copies = []
        for ref, ax in zip(out_refs, axes):
            n = ref.shape[ax] // N_CHIPS
            idx = [slice(None)] * len(ref.shape)
            idx[ax] = pl.ds(_aligned(me * n, n), n)
            mine = ref.at[tuple(idx)]
            copies += [cp.remote(mine, mine, (*chip, c)) for chip in chips]
        return copies

    return _Job(fulls, *_in_place(fulls), 3 * len(fulls), build)


def _exchange_halves_job(grads, kinds):
    def build(in_refs, out_refs, cp):
        x, y, c, me, chips = _place()
        copies = []
        for src, dst, kind in zip(in_refs, out_refs, kinds):
            if kind == "col":
                copies.append(cp.remote(_half_rows(src, 1 - c), dst, (x, y, 1 - c)))
            else:
                copies += [cp.remote(_piece(src, "row", k, 1 - c), _compact_piece(dst, "row", k), (x, y, 1 - c))
                           for k in range(N_CHIPS)]
        return copies

    out_shape = [jax.ShapeDtypeStruct((g.shape[0] // 2, g.shape[1]), g.dtype) for g in grads]
    return _Job(grads, out_shape, {}, sum(1 if k == "col" else N_CHIPS for k in kinds), build)


def _scatter_job(half, kind, n_parts, parts, into=None):
    def build(in_refs, out_refs, cp):
        x, y, c, me, chips = _place()
        src, (dst,) = in_refs[0], out_refs
        copies = []
        for p in parts:
            copies += [cp.remote(_rows_part(_compact_piece(src, kind, _chip_index(chip)), p, n_parts),
                                 _rows_part(dst.at[r], p, n_parts), (*chip, c)) for r, chip in enumerate(chips)]
        return copies

    part_shape = (half.shape[0], half.shape[1] // N_CHIPS) if kind == "col" else (half.shape[0] // N_CHIPS, half.shape[1])
    out_shape = [jax.ShapeDtypeStruct((N_CHIPS - 1,) + part_shape, half.dtype)]
    ins, aliases = ([half], {}) if into is None else ([half, into], {1: 0})
    return _Job(ins, out_shape, aliases, 3 * len(parts), build)


def _share_job(shards):
    def build(in_refs, out_refs, cp):
        x, y, c, me, chips = _place()
        copies = []
        for ref in out_refs:
            mine = _half_rows(ref, c)
            copies.append(cp.remote(mine, mine, (x, y, 1 - c)))
        return copies

    return _Job(shards, *_in_place(shards), len(shards), build)


N_DEVICES = 2 * N_CHIPS


def _small_exchange_job(slots, n_parts, parts):
    def build(in_refs, out_refs, cp):
        x, y, c, me, chips = _place()
        (ref,) = out_refs
        copies = []
        for part in parts:
            mine = _rows_part(ref.at[2 * me + c], part, n_parts)
            copies += [cp.remote(mine, mine, (x ^ (p >> 2), y ^ ((p >> 1) & 1), c ^ (p & 1))) for p in range(1, N_DEVICES)]
        return copies

    return _Job([slots], *_in_place([slots]), (N_DEVICES - 1) * len(parts), build)


def _sum_slots(slots, name, jobs=()):
    n, R, C = slots.shape

    def body(s_ref, o_ref):
        acc = s_ref[0]
        for i in range(1, n):
            acc = acc + s_ref[i]
        o_ref[...] = acc

    return _call(body, jobs, name=name, grid=(1,), in_specs=[pl.BlockSpec((n, R, C), lambda i: (0, 0, 0))],
                 out_specs=pl.BlockSpec((R, C), lambda i: (0, 0)), out_shape=jax.ShapeDtypeStruct((R, C), F32),
                 sem=("arbitrary",))(slots)


def _pack(arrays, width):
    rows = []
    for a in arrays:
        flat = a.reshape(-1)
        n_rows = -(-flat.shape[0] // width)
        rows.append(jnp.pad(flat, (0, n_rows * width - flat.shape[0])).reshape(n_rows, width))
    n = sum(r.shape[0] for r in rows)
    rows.append(jnp.zeros((-n % 8, width), F32))
    return jnp.concatenate(rows, axis=0)


def _unpack(packed, shapes):
    out, r0, width = [], 0, packed.shape[1]
    for shape in shapes:
        size = 1
        for d in shape:
            size *= d
        n_rows = -(-size // width)
        out.append(packed[r0:r0 + n_rows].reshape(-1)[:size].reshape(shape))
        r0 += n_rows
    return out


class _Traffic:
    def __init__(self):
        self.tasks = []

    def add(self, priority, cost, stream, part):
        self.tasks.append((priority, cost, stream, part))
        self.tasks.sort(key=lambda t: t[0])

    def _run(self, taken, call):
        groups = []
        for _, _, stream, part in taken:
            for g in groups:
                if g[0] is stream:
                    g[1].append(part)
                    break
            else:
                groups.append((stream, [part]))
        own, outs = call([stream.job(parts) for stream, parts in groups])
        for (stream, parts), o in zip(groups, outs):
            stream.done(o, parts)
        return own

    def carry(self, fn, budget, *args, **kw):
        taken, spent = [], 0.0
        while budget >= MIN_CARRIER_US and self.tasks and (spent < budget or self.tasks[0][1] == 0.0):
            taken.append(self.tasks.pop(0))
            spent += taken[-1][1]
        if not taken:
            return fn(*args, **kw)
        return self._run(taken, lambda jobs: fn(*args, jobs=jobs, **kw))

    def flush(self, name, below=None, extra=0.0):
        wave = 0
        while any(below is None or t[0] < below for t in self.tasks):
            taken = [t for t in self.tasks if below is None or t[0] < below]
            rest = [t for t in self.tasks if not (below is None or t[0] < below)]
            spent = 0.0
            while rest and spent < extra:
                taken.append(rest.pop(0))
                spent += taken[-1][1]
            self.tasks = rest
            self._run(taken, lambda jobs: (None, _run_jobs(jobs, "%s_%d" % (name, wave))))
            wave += 1


class _GradStream:
    def __init__(self, traffic, u, name, kind, n_parts, cost, seq, g, place, results):
        self.traffic, self.u, self.name, self.kind, self.n_parts, self.cost, self.seq = traffic, u, name, kind, n_parts, cost, seq
        self.g, self.place, self.results, self.arrived, self.left = g, place, results, None, n_parts
        traffic.add((seq, 0, 0), 0.0, self, ("X", 0))

    def job(self, parts):
        step = parts[0][0]
        if step == "X":
            return _exchange_halves_job([self.g], [self.kind])
        if step == "S":
            return _scatter_job(self.chip_sum, self.kind, self.n_parts, [p for _, p in parts], self.arrived)
        return _share_job([self.reduced])

    def done(self, outs, parts):
        step = parts[0][0]
        if step == "X":
            self.chip_sum = self.traffic.carry(_add_pair, SIDE_KERNEL_US, self.g, outs[0], self.kind, self.place, "chip_sum_" + self.name)
            for p in range(self.n_parts):
                self.traffic.add((self.seq, 1, p), self.cost / self.n_parts, self, ("S", p))
        elif step == "S":
            self.arrived = outs[0]
            self.left -= len(parts)
            if self.left == 0:
                self.reduced = self.traffic.carry(_sum_chips, SIDE_KERNEL_US, self.chip_sum, self.arrived, self.kind, self.place,
                                                  "reduce_" + self.name)
                self.traffic.add((self.seq, 2, 0), 0.0, self, ("H", 0))
        else:
            self.results[self.u] = outs[0]


class _SmallStream:
    def __init__(self, traffic, priority, slots, n_parts, cost):
        self.slots, self.n_parts = slots, n_parts
        for p in range(n_parts):
            traffic.add(priority + (p,), cost / n_parts, self, p)

    def job(self, parts):
        return _small_exchange_job(self.slots, self.n_parts, parts)

    def done(self, outs, parts):
        self.slots = outs[0]


SIDE_KERNEL_US = 12.0
MIN_CARRIER_US = 30.0


def kernel(x, mix_pre_g, mix_post_g, ffn_pre_g, ffn_post_g, ab_w_in, pool_w, pool_scale, conv_w, conv_b, conv_ln_g, conv_ln_b, ab_w_out, sc_w_in, sc_conv_w, sc_w_out, ffn_w1, ffn_w2, loss_target, m_mix_pre_g, m_mix_post_g, m_ffn_pre_g, m_ffn_post_g, m_ab_w_in, m_pool_w, m_pool_scale, m_conv_w, m_conv_b, m_conv_ln_g, m_conv_ln_b, m_ab_w_out, m_sc_w_in, m_sc_conv_w, m_sc_w_out, m_ffn_w1, m_ffn_w2, v_mix_pre_g, v_mix_post_g, v_ffn_pre_g, v_ffn_post_g, v_ab_w_in, v_pool_w, v_pool_scale, v_conv_w, v_conv_b, v_conv_ln_g, v_conv_ln_b, v_ab_w_out, v_sc_w_in, v_sc_conv_w, v_sc_w_out, v_ffn_w1, v_ffn_w2):
    x0, target = x[0], loss_target[0]
    T, D = x0.shape
    DP = pool_scale.shape[-1]
    gain = lambda g, layer: g[layer][None, :]

    big = [("ab_w_in", ab_w_in, 0, "col", 4, 67.0), ("ab_w_out", ab_w_out, 0, "row", 2, 44.0),
           ("ffn_w1_0", ffn_w1, 0, "col", 8, 177.0), ("ffn_w2_0", ffn_w2, 0, "row", 8, 177.0),
           ("sc_w_in", sc_w_in, 0, "col", 4, 133.0), ("sc_w_out", sc_w_out, 0, "row", 2, 44.0),
           ("ffn_w1_1", ffn_w1, 1, "col", 8, 177.0), ("ffn_w2_1", ffn_w2, 1, "row", 8, 177.0)]
    kinds = [b[3] for b in big]
    chip = 2 * lax.axis_index("x") + lax.axis_index("y")
    place = jnp.stack([chip, lax.axis_index("c")]).astype(jnp.int32)
    traffic = _Traffic()
    carry = traffic.carry

    def own_in_zeros(shard, ax):
        full = jnp.zeros(tuple(d * N_CHIPS if i == ax else d for i, d in enumerate(shard.shape)), shard.dtype)
        return lax.dynamic_update_slice_in_dim(full, shard, chip * shard.shape[ax], axis=ax)

    W = [_cast_into_full(w, layer, kind, place, "cast_" + name) for name, w, layer, kind, _, _ in big]
    smalls = [own_in_zeros(pool_w[0], 1), own_in_zeros(conv_w[0], 1), own_in_zeros(sc_conv_w[0], 1)]
    flying = {}

    def start_gather(u, after=()):
        if u < len(big):
            flying[u] = _start_job(_gather_job(W[u], kinds[u], 1, ici_parts=[0]), "gather_start_%d" % u, after)

    def ready(u, after):
        (W[u],) = _wait_job(flying[u], "gather_wait_%d" % u, [after])
        start_gather(u + 3, [W[u]])
        onward = _start_job(_gather_job(W[u], kinds[u], 1, forward_parts=[0]), "forward_start_%d" % u)
        (W[u],) = _wait_job(onward, "forward_wait_%d" % u, [onward.token])
        return W[u]

    small_flight = _start_job(_gather_small_job(smalls, [1, 1, 1]), "gather_start_small")
    start_gather(0, [small_flight.token])
    start_gather(1, [flying[0].token])
    start_gather(2, [flying[1].token])
    _Behind.pending = [f.token for f in flying.values()] + [small_flight.token]

    relu_sq = lambda acc: (jnp.maximum(acc, 0.0), jnp.square(jnp.maximum(acc, 0.0)))
    relu_sq_bwd = lambda acc, a: (acc * (2.0 * a.astype(F32)),)

    h0 = _norm_fwd(x0, gain(mix_pre_g, 0), "norm_in")
    pool_w_full, conv_w_full, sc_conv_w_full = _wait_job(small_flight, "gather_wait_small", [h0])
    z0 = _matmul(h0, ready(0, h0), "nn", "mix0_in")
    pooled, y_pool = _pool_fwd(z0, pool_w_full, pool_scale, "pool_fwd")
    a_conv, c_conv = _conv_fwd(z0, conv_w_full, conv_b, DP, "conv_fwd")
    y_conv = _ln_silu_fwd(c_conv, conv_ln_g, conv_ln_b, "ln_silu_fwd")
    y0 = jnp.concatenate([y_pool, y_conv], axis=1)
    m0 = _matmul(y0, ready(1, y0), "nn", "mix0_out")
    x1, h1 = _residual_norm(x0, m0, gain(mix_post_g, 0), gain(ffn_pre_g, 0), "res_mix0")
    a0, a0sq = _matmul(h1, ready(2, h1), "nn", "ffn0_up", out_dtypes=(BF16, BF16), epilogue=relu_sq)
    f0 = _matmul(a0sq, ready(3, a0sq), "nn", "ffn0_down")
    x2, h2 = _residual_norm(x1, f0, gain(ffn_post_g, 0), gain(mix_pre_g, 1), "res_ffn0")
    z1 = _matmul(h2, ready(4, h2), "nn", "mix1_in")
    y1 = _short_fwd(z1, sc_conv_w_full, "short_fwd")
    m1 = _matmul(y1, ready(5, y1), "nn", "mix1_out")
    x3, h3 = _residual_norm(x2, m1, gain(mix_post_g, 1), gain(ffn_pre_g, 1), "res_mix1")
    a1, a1sq = _matmul(h3, ready(6, h3), "nn", "ffn1_up", out_dtypes=(BF16, BF16), epilogue=relu_sq)
    f1 = _matmul(a1sq, ready(7, a1sq), "nn", "ffn1_down")
    w_in0, w_out0, w1_0, w2_0, w_in1, w_out1, w1_1, w2_1 = W

    grads_big = [None] * len(big)

    def reduce_grad(u, g, seq):
        name, _, _, kind, n_parts, cost = big[u]
        _GradStream(traffic, u, name, kind, n_parts, cost, seq, g, place, grads_big)

    dx, df1, d_ffn_post_1, loss_row = carry(_loss_and_last_norm_bwd, 30.0, x3, f1, gain(ffn_post_g, 1), target, "loss")
    reduce_grad(7, carry(_matmul, 80.0, a1sq, df1, "tn", "ffn1_down_dw", out_dtypes=(BF16,)), 0)
    dz = carry(_matmul, 82.0, df1, w2_1, "nt", "ffn1_down_dx", out_dtypes=(BF16,), epilogue=relu_sq_bwd, epi=(a1,))
    reduce_grad(6, carry(_matmul, 80.0, h3, dz, "tn", "ffn1_up_dw", out_dtypes=(BF16,)), 1)
    dh = carry(_matmul, 87.0, dz, w1_1, "nt", "ffn1_up_dx")
    dx, d_ffn_pre_1, dm1, d_mix_post_1 = carry(_norms_bwd, 36.0, dx, dh, x3, gain(ffn_pre_g, 1), m1, gain(mix_post_g, 1), "norms_bwd3")

    reduce_grad(5, carry(_matmul, 24.0, y1, dm1, "tn", "mix1_out_dw", out_dtypes=(BF16,)), 2)
    dy1 = carry(_matmul, 25.0, dm1, w_out1, "nt", "mix1_out_dx")
    db, dcg, du, d_sc_conv_w = carry(_short_bwd, 41.0, dy1, z1, sc_conv_w_full, "short_bwd")
    dz1 = jnp.concatenate([db, dcg, du], axis=1)
    reduce_grad(4, carry(_matmul, 62.0, h2, dz1, "tn", "mix1_in_dw", out_dtypes=(BF16,)), 3)
    dh = carry(_matmul, 68.0, dz1, w_in1, "nt", "mix1_in_dx")
    dx, d_mix_pre_1, df0, d_ffn_post_0 = carry(_norms_bwd, 35.0, dx, dh, x2, gain(mix_pre_g, 1), f0, gain(ffn_post_g, 0), "norms_bwd2")

    reduce_grad(3, carry(_matmul, 80.0, a0sq, df0, "tn", "ffn0_down_dw", out_dtypes=(BF16,)), 4)
    dz = carry(_matmul, 82.0, df0, w2_0, "nt", "ffn0_down_dx", out_dtypes=(BF16,), epilogue=relu_sq_bwd, epi=(a0,))
    reduce_grad(2, carry(_matmul, 80.0, h1, dz, "tn", "ffn0_up_dw", out_dtypes=(BF16,)), 5)
    dh = carry(_matmul, 87.0, dz, w1_0, "nt", "ffn0_up_dx")
    dx, d_ffn_pre_0, dm0, d_mix_post_0 = carry(_norms_bwd, 36.0, dx, dh, x1, gain(ffn_pre_g, 0), m0, gain(mix_post_g, 0), "norms_bwd1")

    reduce_grad(1, carry(_matmul, 24.0, y0, dm0, "tn", "mix0_out_dw", out_dtypes=(BF16,)), 6)
    dy0 = carry(_matmul, 25.0, dm0, w_out0, "nt", "mix0_out_dx")
    du_pool, d_pool_w, d_pool_scale = carry(_pool_bwd, 28.0, dy0, pooled, pool_w_full, pool_scale, "pool_bwd")
    dc, d_ln_g, d_ln_b = carry(_ln_silu_bwd, 15.0, dy0, c_conv, conv_ln_g, conv_ln_b, "ln_silu_bwd")
    dv, dgate, d_conv_w, d_conv_b = carry(_conv_bwd, 52.0, dc, a_conv, z0, conv_w_full, DP, "conv_bwd")
    dz0 = jnp.concatenate([du_pool, dv, dgate], axis=1)

    device_slot = 2 * chip + lax.axis_index("c")

    def in_own_slot(packed):
        return lax.dynamic_update_slice_in_dim(jnp.zeros((N_DEVICES,) + packed.shape, F32), packed[None], device_slot, axis=0)

    small_grads = [d_mix_pre_1, jnp.concatenate([d_mix_post_0, d_mix_post_1], 0),
                   jnp.concatenate([d_ffn_pre_0, d_ffn_pre_1], 0), jnp.concatenate([d_ffn_post_0, d_ffn_post_1], 0),
                   d_pool_scale, d_conv_b, d_ln_g, d_ln_b, d_pool_w, d_conv_w, d_sc_conv_w]
    small_shapes = [a.shape for a in small_grads]
    small_stream = _SmallStream(traffic, (6, 3), in_own_slot(_pack(small_grads, D)), 4, 112.0)

    reduce_grad(0, carry(_matmul, 34.0, h0, dz0, "tn", "mix0_in_dw", out_dtypes=(BF16,)), 7)
    dh = carry(_matmul, 40.0, dz0, w_in0, "nt", "mix0_in_dx")
    grad_x, d_mix_pre_0 = _norms_bwd(dx, dh, x0, gain(mix_pre_g, 0), None, None, "norms_bwd0")
    last_stream = _SmallStream(traffic, (7, 3), in_own_slot(_pack([d_mix_pre_0], D)), 1, 5.0)
    traffic.flush("tail")

    loss = lax.psum(loss_row[0, 0], ("x", "y", "c"))
    summed = _unpack(_sum_slots(small_stream.slots, "small_grads_sum"), small_shapes)
    (g_mix_pre_1, g_mix_post, g_ffn_pre, g_ffn_post, g_pool_scale, g_conv_b, g_ln_g, g_ln_b, g_pool_w_full, g_conv_w_full,
     g_sc_conv_w_full) = summed
    (g_mix_pre_0,) = _unpack(_sum_slots(last_stream.slots, "last_grad_sum"), [d_mix_pre_0.shape])
    g_mix_pre = jnp.concatenate([g_mix_pre_0, g_mix_pre_1], 0)
    own = lambda a, ax: lax.dynamic_slice_in_dim(a, chip * (a.shape[ax] // N_CHIPS), a.shape[ax] // N_CHIPS, axis=ax)
    g_pool_w, g_conv_w, g_sc_conv_w = own(g_pool_w_full, 1), own(g_conv_w_full, 1), own(g_sc_conv_w_full, 1)

    gr = grads_big
    upd = {}
    first = _adamw(ffn_w2, gr[7], m_ffn_w2, v_ffn_w2, "adamw_ffn_w2_1", layer=1)
    upd["ffn_w2"] = _adamw(ffn_w2, gr[3], m_ffn_w2, v_ffn_w2, "adamw_ffn_w2_0", layer=0, carried=first)
    first = _adamw(ffn_w1, gr[6], m_ffn_w1, v_ffn_w1, "adamw_ffn_w1_1", layer=1)
    upd["ffn_w1"] = _adamw(ffn_w1, gr[2], m_ffn_w1, v_ffn_w1, "adamw_ffn_w1_0", layer=0, carried=first)
    upd["sc_w_in"] = _adamw(sc_w_in, gr[4], m_sc_w_in, v_sc_w_in, "adamw_sc_w_in")
    upd["sc_w_out"] = _adamw(sc_w_out, gr[5], m_sc_w_out, v_sc_w_out, "adamw_sc_w_out")
    upd["ab_w_out"] = _adamw(ab_w_out, gr[1], m_ab_w_out, v_ab_w_out, "adamw_ab_w_out")
    upd["ab_w_in"] = _adamw(ab_w_in, gr[0], m_ab_w_in, v_ab_w_in, "adamw_ab_w_in")

    def small_update(w, g, m, v, name):
        shape = w.shape
        as3 = lambda a: a.reshape((1, -1, shape[-1]))
        outs = _adamw(as3(w), g.reshape((-1, shape[-1])), as3(m), as3(v), "adamw_" + name)
        return [o.reshape(shape) for o in outs]

    upd["mix_pre_g"] = small_update(mix_pre_g, g_mix_pre, m_mix_pre_g, v_mix_pre_g, "mix_pre_g")
    upd["mix_post_g"] = small_update(mix_post_g, g_mix_post, m_mix_post_g, v_mix_post_g, "mix_post_g")
    upd["ffn_pre_g"] = small_update(ffn_pre_g, g_ffn_pre, m_ffn_pre_g, v_ffn_pre_g, "ffn_pre_g")
    upd["ffn_post_g"] = small_update(ffn_post_g, g_ffn_post, m_ffn_post_g, v_ffn_post_g, "ffn_post_g")
    upd["pool_w"] = small_update(pool_w, g_pool_w, m_pool_w, v_pool_w, "pool_w")
    upd["pool_scale"] = small_update(pool_scale, g_pool_scale, m_pool_scale, v_pool_scale, "pool_scale")
    upd["conv_w"] = small_update(conv_w, g_conv_w, m_conv_w, v_conv_w, "conv_w")
    upd["conv_b"] = small_update(conv_b, g_conv_b, m_conv_b, v_conv_b, "conv_b")
    upd["conv_ln_g"] = small_update(conv_ln_g, g_ln_g, m_conv_ln_g, v_conv_ln_g, "conv_ln_g")
    upd["conv_ln_b"] = small_update(conv_ln_b, g_ln_b, m_conv_ln_b, v_conv_ln_b, "conv_ln_b")
    upd["sc_conv_w"] = small_update(sc_conv_w, g_sc_conv_w, m_sc_conv_w, v_sc_conv_w, "sc_conv_w")

    order = ["mix_pre_g", "mix_post_g", "ffn_pre_g", "ffn_post_g", "ab_w_in", "pool_w", "pool_scale", "conv_w", "conv_b",
             "conv_ln_g", "conv_ln_b", "ab_w_out", "sc_w_in", "sc_conv_w", "sc_w_out", "ffn_w1", "ffn_w2"]
    out = [loss, grad_x[None]]
    for part in range(4):
        out += [upd[n][part] for n in order]
    return tuple(out)
```

```python
import jax
import jax.numpy as jnp
from jax import lax
from jax.experimental import pallas as pl
from jax.experimental.pallas import tpu as pltpu

F32, BF16 = jnp.float32, jnp.bfloat16
EPS = 1e-6
N_GROUPS = 4
MAX_WINDOW = 16
CONV_K = 31
SHORT_K = 3
CONV_PAD = 32
SHORT_PAD = 8
ADAM_LR, ADAM_B1, ADAM_B2, ADAM_EPS, ADAM_WD, ADAM_STEP = 0.001, 0.9, 0.999, 1e-08, 0.01, 10
N_CHIPS = 4
VMEM_LIMIT_BYTES = 56 * 1024 * 1024
ROW_TILE = 256
CHUNK = 256
MESH = pl.DeviceIdType.MESH
HBM = pl.BlockSpec(memory_space=pltpu.HBM)
SEM = pl.BlockSpec(memory_space=pltpu.SEMAPHORE)
ANY = pl.BlockSpec(memory_space=pl.ANY)


def _cp(*sem):
    return pltpu.CompilerParams(dimension_semantics=sem, vmem_limit_bytes=VMEM_LIMIT_BYTES)


def _sigmoid(v):
    return 1.0 / (1.0 + jnp.exp(-v))


class _Behind:
    pending = []


def _call(body, jobs, prefetch=None, **kw):
    beside = _Beside(jobs)
    behind, _Behind.pending = _Behind.pending, []
    grid = kw["grid"]
    single = not isinstance(kw["out_shape"], (list, tuple))
    in_specs, scratch = list(kw["in_specs"]), list(kw.get("scratch_shapes", ()))
    out_shape = [kw["out_shape"]] if single else list(kw["out_shape"])
    out_specs = [kw["out_specs"]] if single else list(kw["out_specs"])
    n_pre = 0 if prefetch is None else 1
    n_own, n_out, n_scr = len(in_specs), len(out_shape), len(scratch)
    in_specs += [ANY] * len(behind)
    n_in = len(in_specs)

    def wrapped(*refs):
        pre, refs = refs[:n_pre], refs[n_pre:]
        ins, job_ins = refs[:n_own], refs[n_in:n_in + beside.n_in]
        outs = refs[n_in + beside.n_in:n_in + beside.n_in + n_out]
        job_outs = refs[n_in + beside.n_in + n_out:n_in + beside.n_in + n_out + beside.n_out]
        scr = refs[n_in + beside.n_in + n_out + beside.n_out:]
        beside.start(job_ins, job_outs, scr, grid)
        body(*pre, *ins, *outs, *scr[:n_scr])
        beside.finish(job_ins, job_outs, scr, grid)

    aliases = {n_pre + i: o for i, o in kw.get("input_output_aliases", {}).items()}
    aliases.update(beside.aliases(n_pre + n_in, n_out))
    specs = dict(grid=grid, in_specs=in_specs + [HBM] * beside.n_in, out_specs=out_specs + [HBM] * beside.n_out)
    if prefetch is None:
        specs["scratch_shapes"] = scratch + beside.scratch()
    else:
        specs = dict(grid_spec=pltpu.PrefetchScalarGridSpec(num_scalar_prefetch=1, scratch_shapes=scratch + beside.scratch(), **specs))
    sem = ["arbitrary"] * len(grid) if jobs else kw["sem"]
    call = pl.pallas_call(wrapped, name=kw["name"], out_shape=out_shape + beside.out_shape, input_output_aliases=aliases,
                          compiler_params=_cp(*sem), **specs)

    def run(*args):
        outs = call(*([prefetch] * n_pre), *args, *behind, *beside.ins)
        own = outs[0] if single else list(outs[:n_out])
        return (own, beside.split(outs[n_out:])) if jobs else own

    return run


_DIMS = {"nn": (((1,), (0,)), ((), ())), "nt": (((1,), (1,)), ((), ())), "tn": (((0,), (0,)), ((), ()))}


def _pick(n, cap, step=256):
    if n <= cap:
        return n
    return next(t for t in range(cap - cap % step, 0, -step) if n % t == 0)


def _matmul(a, b, mode, name, out_dtypes=(F32,), epilogue=None, epi=(), jobs=(), tm=1024, tn=1024, tk=2048):
    if mode == "tn":
        (K, M), (K2, N) = a.shape, b.shape
    elif mode == "nt":
        (M, K), (N, K2) = a.shape, b.shape
    else:
        (M, K), (K2, N) = a.shape, b.shape
    assert K == K2
    tm, tn, tk = _pick(M, tm), _pick(N, tn), _pick(K, tk)
    nk = K // tk
    a_spec = pl.BlockSpec((tk, tm), lambda i, j, k: (k, i)) if mode == "tn" else pl.BlockSpec((tm, tk), lambda i, j, k: (i, k))
    b_spec = pl.BlockSpec((tn, tk), lambda i, j, k: (j, k)) if mode == "nt" else pl.BlockSpec((tk, tn), lambda i, j, k: (k, j))
    o_spec = pl.BlockSpec((tm, tn), lambda i, j, k: (i, j))
    n_epi, n_out = len(epi), len(out_dtypes)

    def body(a_ref, b_ref, *rest):
        epi_refs, out_refs, scratch = rest[:n_epi], rest[n_epi:n_epi + n_out], rest[n_epi + n_out:]
        part = lax.dot_general(a_ref[...].astype(BF16), b_ref[...].astype(BF16), _DIMS[mode], preferred_element_type=F32)

        def finish(acc):
            outs = epilogue(acc, *[r[...] for r in epi_refs]) if epilogue else (acc,)
            for o_ref, o in zip(out_refs, outs):
                o_ref[...] = o.astype(o_ref.dtype)

        if nk == 1:
            finish(part)
        else:
            acc_ref = scratch[0]
            k = pl.program_id(2)

            @pl.when(k == 0)
            def _():
                acc_ref[...] = part

            @pl.when(k > 0)
            def _():
                acc_ref[...] += part

            @pl.when(k == nk - 1)
            def _():
                finish(acc_ref[...])

    got = _call(
        body, jobs, name=name, grid=(M // tm, N // tn, nk),
        in_specs=[a_spec, b_spec] + [o_spec] * n_epi, out_specs=[o_spec] * n_out,
        out_shape=[jax.ShapeDtypeStruct((M, N), dt) for dt in out_dtypes],
        scratch_shapes=[pltpu.VMEM((tm, tn), F32)] if nk > 1 else [],
        sem=("parallel", "parallel", "arbitrary"))(a, b, *epi)
    outs, job_outs = got if jobs else (got, None)
    result = outs[0] if n_out == 1 else outs
    return (result, job_outs) if jobs else result


def _rms(x, g):
    r = lax.rsqrt(jnp.mean(x * x, axis=-1, keepdims=True) + EPS)
    return x * r * g


def _rms_bwd(dy, x, g):
    r = lax.rsqrt(jnp.mean(x * x, axis=-1, keepdims=True) + EPS)
    xn = x * r
    dyg = dy * g
    dx = r * (dyg - xn * jnp.mean(dyg * xn, axis=-1, keepdims=True))
    return dx, jnp.sum(dy * xn, axis=0, keepdims=True)


def _rows(d, tr=ROW_TILE):
    return pl.BlockSpec((tr, d), lambda i: (i, 0))


def _vec(d):
    return pl.BlockSpec((1, d), lambda i: (0, 0))


def _accumulate(ref, val):
    @pl.when(pl.program_id(0) == 0)
    def _():
        ref[...] = val

    @pl.when(pl.program_id(0) > 0)
    def _():
        ref[...] += val


def _norm_fwd(x, g, name, jobs=()):
    T, D = x.shape

    def body(x_ref, g_ref, h_ref):
        h_ref[...] = _rms(x_ref[...], g_ref[...]).astype(BF16)

    return _call(body, jobs, name=name,grid=(T // ROW_TILE,), in_specs=[_rows(D), _vec(D)], out_specs=_rows(D),
                          out_shape=jax.ShapeDtypeStruct((T, D), BF16), sem=("parallel",))(x, g)


def _residual_norm(x, m, g_post, g_next, name, jobs=()):
    T, D = x.shape

    def body(x_ref, m_ref, gp_ref, gn_ref, xo_ref, h_ref):
        xo = x_ref[...] + _rms(m_ref[...], gp_ref[...])
        xo_ref[...] = xo
        h_ref[...] = _rms(xo, gn_ref[...]).astype(BF16)

    return _call(body, jobs, name=name,grid=(T // ROW_TILE,), in_specs=[_rows(D), _rows(D), _vec(D), _vec(D)],
                          out_specs=[_rows(D), _rows(D)],
                          out_shape=[jax.ShapeDtypeStruct((T, D), F32), jax.ShapeDtypeStruct((T, D), BF16)],
                          sem=("parallel",))(x, m, g_post, g_next)


def _loss_and_last_norm_bwd(x, m, g_post, target, name, jobs=()):
    T, D = x.shape

    def body(x_ref, m_ref, gp_ref, t_ref, dx_ref, dm_ref, dg_ref, loss_ref):
        m_val, gp = m_ref[...], gp_ref[...]
        err = x_ref[...] + _rms(m_val, gp) - t_ref[...]
        dx = err * (1.0 / D)
        dx_ref[...] = dx
        dm, dg = _rms_bwd(dx, m_val, gp)
        dm_ref[...] = dm.astype(BF16)
        _accumulate(dg_ref, dg)
        _accumulate(loss_ref, jnp.full((1, 128), 0.5 * jnp.sum(err * err) * (1.0 / D), F32))

    return _call(body, jobs, name=name,grid=(T // ROW_TILE,), in_specs=[_rows(D), _rows(D), _vec(D), _rows(D)],
                          out_specs=[_rows(D), _rows(D), _vec(D), _vec(128)],
                          out_shape=[jax.ShapeDtypeStruct((T, D), F32), jax.ShapeDtypeStruct((T, D), BF16),
                                     jax.ShapeDtypeStruct((1, D), F32), jax.ShapeDtypeStruct((1, 128), F32)],
                          sem=("arbitrary",))(x, m, g_post, target)


def _norms_bwd(dx, dh, x_in, g_pre, m_prev, g_post_prev, name, jobs=()):
    T, D = dx.shape
    with_prev = m_prev is not None

    def body(*refs):
        if with_prev:
            dx_ref, dh_ref, x_ref, gq_ref, m_ref, gp_ref, dxo_ref, dgq_ref, dm_ref, dgp_ref = refs
        else:
            dx_ref, dh_ref, x_ref, gq_ref, dxo_ref, dgq_ref = refs
        d_in, dgq = _rms_bwd(dh_ref[...], x_ref[...], gq_ref[...])
        dxo = dx_ref[...] + d_in
        dxo_ref[...] = dxo
        _accumulate(dgq_ref, dgq)
        if with_prev:
            dm, dgp = _rms_bwd(dxo, m_ref[...], gp_ref[...])
            dm_ref[...] = dm.astype(BF16)
            _accumulate(dgp_ref, dgp)

    ins, in_specs = [dx, dh, x_in, g_pre], [_rows(D), _rows(D), _rows(D), _vec(D)]
    out_specs = [_rows(D), _vec(D)]
    out_shape = [jax.ShapeDtypeStruct((T, D), F32), jax.ShapeDtypeStruct((1, D), F32)]
    if with_prev:
        ins += [m_prev, g_post_prev]
        in_specs += [_rows(D), _vec(D)]
        out_specs += [_rows(D), _vec(D)]
        out_shape += [jax.ShapeDtypeStruct((T, D), BF16), jax.ShapeDtypeStruct((1, D), F32)]
    return _call(body, jobs, name=name,grid=(T // ROW_TILE,), in_specs=in_specs, out_specs=out_specs, out_shape=out_shape,
                          sem=("arbitrary",))(*ins)


def _window_weights(g):
    w = 2 << g
    return w, [jnp.where(j < w, 1.0, 0.0).astype(F32) for j in range(MAX_WINDOW)]


def _valid_count(r0, rows, w):
    t = (lax.broadcasted_iota(jnp.int32, (rows, 1), 0) + (r0 + 1)).astype(F32)
    return jnp.minimum(t, w.astype(F32))


def _pool_fwd(z, pool_w, pool_scale, name, jobs=()):
    T = z.shape[0]
    PG = pool_w.shape[-1]
    DP = N_GROUPS * PG
    rc = min(CHUNK, T)

    def body(u_ref, pw_ref, sc_ref, pooled_ref, y_ref, pad):
        w, wts = _window_weights(pl.program_id(0))
        pad[pl.ds(0, MAX_WINDOW), :] = jnp.zeros((MAX_WINDOW, PG), F32)
        pad[pl.ds(MAX_WINDOW, T), :] = u_ref[...]
        for r0 in range(0, T, rc):
            acc = jnp.zeros((rc, PG), F32)
            for j in range(MAX_WINDOW):
                acc = acc + wts[j] * pad[pl.ds(MAX_WINDOW + r0 - j, rc), :]
            pooled = acc / _valid_count(r0, rc, w) - u_ref[pl.ds(r0, rc), :]
            pooled_ref[pl.ds(r0, rc), :] = pooled.astype(BF16)
        mixed = jnp.dot(pooled_ref[...], pw_ref[...].astype(BF16), preferred_element_type=F32)
        y_ref[...] = (mixed * sc_ref[...]).astype(BF16)

    col = lambda g: (0, g)
    return _call(
        body, jobs, name=name,grid=(N_GROUPS,),
        in_specs=[pl.BlockSpec((T, PG), col), pl.BlockSpec((None, PG, PG), lambda g: (g, 0, 0)), pl.BlockSpec((1, PG), col)],
        out_specs=[pl.BlockSpec((T, PG), col), pl.BlockSpec((T, PG), col)],
        out_shape=[jax.ShapeDtypeStruct((T, DP), BF16), jax.ShapeDtypeStruct((T, DP), BF16)],
        scratch_shapes=[pltpu.VMEM((T + MAX_WINDOW, PG), F32)], sem=("parallel",))(z, pool_w, pool_scale)


def _pool_bwd(dy, pooled, pool_w, pool_scale, name, jobs=()):
    T = dy.shape[0]
    PG = pool_w.shape[-1]
    DP = N_GROUPS * PG
    rc = min(CHUNK, T)

    def body(dy_ref, pooled_ref, pw_ref, sc_ref, du_ref, dpw_ref, dsc_ref, pad, dp_ref):
        w, wts = _window_weights(pl.program_id(0))
        pooled_v, pw = pooled_ref[...], pw_ref[...].astype(BF16)
        dy_v = dy_ref[...]
        mixed = jnp.dot(pooled_v, pw, preferred_element_type=F32)
        dsc_ref[...] = jnp.sum(dy_v * mixed, axis=0, keepdims=True)
        dmixed = (dy_v * sc_ref[...]).astype(BF16)
        dpw_ref[...] = lax.dot_general(pooled_v, dmixed, _DIMS["tn"], preferred_element_type=F32)
        dp_ref[...] = lax.dot_general(dmixed, pw, _DIMS["nt"], preferred_element_type=F32)
        pad[pl.ds(T, MAX_WINDOW), :] = jnp.zeros((MAX_WINDOW, PG), F32)
        for r0 in range(0, T, rc):
            pad[pl.ds(r0, rc), :] = dp_ref[pl.ds(r0, rc), :] / _valid_count(r0, rc, w)
        for r0 in range(0, T, rc):
            acc = jnp.zeros((rc, PG), F32)
            for j in range(MAX_WINDOW):
                acc = acc + wts[j] * pad[pl.ds(r0 + j, rc), :]
            du_ref[pl.ds(r0, rc), :] = (acc - dp_ref[pl.ds(r0, rc), :]).astype(BF16)

    col = lambda g: (0, g)
    return _call(
        body, jobs, name=name,grid=(N_GROUPS,),
        in_specs=[pl.BlockSpec((T, PG), col), pl.BlockSpec((T, PG), col), pl.BlockSpec((None, PG, PG), lambda g: (g, 0, 0)),
                  pl.BlockSpec((1, PG), col)],
        out_specs=[pl.BlockSpec((T, PG), col), pl.BlockSpec((None, PG, PG), lambda g: (g, 0, 0)), pl.BlockSpec((1, PG), col)],
        out_shape=[jax.ShapeDtypeStruct((T, DP), BF16), jax.ShapeDtypeStruct((N_GROUPS, PG, PG), F32),
                   jax.ShapeDtypeStruct((1, DP), F32)],
        scratch_shapes=[pltpu.VMEM((T + MAX_WINDOW, PG), F32), pltpu.VMEM((T, PG), F32)],
        sem=("parallel",))(dy, pooled, pool_w, pool_scale)


def _conv_fwd(z, conv_w, conv_b, d_pool, name, tc=128, jobs=()):
    T = z.shape[0]
    DC = conv_w.shape[-1]
    rc = min(CHUNK, T)
    v0, g0 = d_pool // tc, (d_pool + DC) // tc

    def body(v_ref, gt_ref, w_ref, b_ref, a_ref, c_ref, pad):
        pad[pl.ds(0, CONV_PAD), :] = jnp.zeros((CONV_PAD, tc), F32)
        for r0 in range(0, T, rc):
            a = v_ref[pl.ds(r0, rc), :] * _sigmoid(gt_ref[pl.ds(r0, rc), :])
            a_ref[pl.ds(r0, rc), :] = a
            pad[pl.ds(CONV_PAD + r0, rc), :] = a
        for r0 in range(0, T, rc):
            acc = jnp.zeros((rc, tc), F32) + b_ref[...]
            for k in range(CONV_K):
                acc = acc + w_ref[pl.ds(k, 1), :] * pad[pl.ds(CONV_PAD - (CONV_K - 1) + k + r0, rc), :]
            c_ref[pl.ds(r0, rc), :] = acc

    col = lambda j: (0, j)
    return _call(
        body, jobs, name=name,grid=(DC // tc,),
        in_specs=[pl.BlockSpec((T, tc), lambda j: (0, v0 + j)), pl.BlockSpec((T, tc), lambda j: (0, g0 + j)),
                  pl.BlockSpec((CONV_K, tc), col), pl.BlockSpec((1, tc), col)],
        out_specs=[pl.BlockSpec((T, tc), col), pl.BlockSpec((T, tc), col)],
        out_shape=[jax.ShapeDtypeStruct((T, DC), F32), jax.ShapeDtypeStruct((T, DC), F32)],
        scratch_shapes=[pltpu.VMEM((T + CONV_PAD, tc), F32)], sem=("parallel",))(z, z, conv_w, conv_b)


def _conv_bwd(dc, a, z, conv_w, d_pool, name, tc=128, jobs=()):
    T, DC = dc.shape
    rc = min(CHUNK, T)
    v0, g0 = d_pool // tc, (d_pool + DC) // tc

    def body(dc_ref, a_ref, v_ref, gt_ref, w_ref, dv_ref, dg_ref, dw_ref, db_ref, apad, dpad):
        apad[pl.ds(0, CONV_PAD), :] = jnp.zeros((CONV_PAD, tc), F32)
        apad[pl.ds(CONV_PAD, T), :] = a_ref[...]
        dpad[pl.ds(0, T), :] = dc_ref[...]
        dpad[pl.ds(T, CONV_PAD), :] = jnp.zeros((CONV_PAD, tc), F32)
        db_ref[...] = jnp.sum(dc_ref[...], axis=0, keepdims=True)
        for k in range(CONV_K):
            acc = jnp.zeros((8, tc), F32)
            for r0 in range(0, T, rc):
                prod = dc_ref[pl.ds(r0, rc), :] * apad[pl.ds(CONV_PAD - (CONV_K - 1) + k + r0, rc), :]
                acc = acc + jnp.sum(prod.reshape(rc // 8, 8, tc), axis=0)
            dw_ref[pl.ds(k, 1), :] = jnp.sum(acc, axis=0, keepdims=True)
        for r0 in range(0, T, rc):
            da = jnp.zeros((rc, tc), F32)
            for k in range(CONV_K):
                da = da + w_ref[pl.ds(k, 1), :] * dpad[pl.ds(r0 + (CONV_K - 1) - k, rc), :]
            sig = _sigmoid(gt_ref[pl.ds(r0, rc), :])
            dv_ref[pl.ds(r0, rc), :] = (da * sig).astype(BF16)
            dg_ref[pl.ds(r0, rc), :] = (da * v_ref[pl.ds(r0, rc), :] * sig * (1.0 - sig)).astype(BF16)

    col = lambda j: (0, j)
    return _call(
        body, jobs, name=name,grid=(DC // tc,),
        in_specs=[pl.BlockSpec((T, tc), col), pl.BlockSpec((T, tc), col), pl.BlockSpec((T, tc), lambda j: (0, v0 + j)),
                  pl.BlockSpec((T, tc), lambda j: (0, g0 + j)), pl.BlockSpec((CONV_K, tc), col)],
        out_specs=[pl.BlockSpec((T, tc), col), pl.BlockSpec((T, tc), col), pl.BlockSpec((CONV_K, tc), col),
                   pl.BlockSpec((1, tc), col)],
        out_shape=[jax.ShapeDtypeStruct((T, DC), BF16), jax.ShapeDtypeStruct((T, DC), BF16),
                   jax.ShapeDtypeStruct((CONV_K, DC), F32), jax.ShapeDtypeStruct((1, DC), F32)],
        scratch_shapes=[pltpu.VMEM((T + CONV_PAD, tc), F32), pltpu.VMEM((T + CONV_PAD, tc), F32)],
        sem=("parallel",))(dc, a, z, z, conv_w)


def _layer_norm_parts(c, g, b):
    mu = jnp.mean(c, axis=-1, keepdims=True)
    xc = c - mu
    rstd = lax.rsqrt(jnp.mean(xc * xc, axis=-1, keepdims=True) + EPS)
    xhat = xc * rstd
    return xhat, rstd, xhat * g + b


def _ln_silu_fwd(c, g, b, name, jobs=()):
    T, DC = c.shape

    def body(c_ref, g_ref, b_ref, y_ref):
        _, _, ln = _layer_norm_parts(c_ref[...], g_ref[...], b_ref[...])
        y_ref[...] = (ln * _sigmoid(ln)).astype(BF16)

    return _call(body, jobs, name=name,grid=(T // ROW_TILE,), in_specs=[_rows(DC), _vec(DC), _vec(DC)], out_specs=_rows(DC),
                          out_shape=jax.ShapeDtypeStruct((T, DC), BF16), sem=("parallel",))(c, g, b)


def _ln_silu_bwd(dy, c, g, b, name, jobs=()):
    T, DC = c.shape

    def body(dy_ref, c_ref, g_ref, b_ref, dc_ref, dg_ref, db_ref):
        gain = g_ref[...]
        xhat, rstd, ln = _layer_norm_parts(c_ref[...], gain, b_ref[...])
        s = _sigmoid(ln)
        dln = dy_ref[...] * (s * (1.0 + ln * (1.0 - s)))
        _accumulate(dg_ref, jnp.sum(dln * xhat, axis=0, keepdims=True))
        _accumulate(db_ref, jnp.sum(dln, axis=0, keepdims=True))
        dxh = dln * gain
        dc_ref[...] = rstd * (dxh - jnp.mean(dxh, axis=-1, keepdims=True) - xhat * jnp.mean(dxh * xhat, axis=-1, keepdims=True))

    return _call(body, jobs, name=name,grid=(T // ROW_TILE,),
                          in_specs=[pl.BlockSpec((ROW_TILE, DC), lambda i: (i, 1)), _rows(DC), _vec(DC), _vec(DC)],
                          out_specs=[_rows(DC), _vec(DC), _vec(DC)],
                          out_shape=[jax.ShapeDtypeStruct((T, DC), F32), jax.ShapeDtypeStruct((1, DC), F32),
                                     jax.ShapeDtypeStruct((1, DC), F32)],
                          sem=("arbitrary",))(dy, c, g, b)


def _short_specs(T, DS, tc):
    n = DS // tc
    return [pl.BlockSpec((T, tc), lambda j: (0, j)), pl.BlockSpec((T, tc), lambda j: (0, n + j)),
            pl.BlockSpec((T, tc), lambda j: (0, 2 * n + j))]


def _short_fwd(z, w, name, tc=256, jobs=()):
    T = z.shape[0]
    DS = w.shape[-1]
    rc = min(CHUNK, T)

    def body(b_ref, cg_ref, u_ref, w_ref, y_ref, pad):
        pad[pl.ds(0, SHORT_PAD), :] = jnp.zeros((SHORT_PAD, tc), F32)
        pad[pl.ds(SHORT_PAD, T), :] = cg_ref[...] * u_ref[...]
        for r0 in range(0, T, rc):
            r = jnp.zeros((rc, tc), F32)
            for k in range(SHORT_K):
                r = r + w_ref[pl.ds(k, 1), :] * pad[pl.ds(SHORT_PAD - (SHORT_K - 1) + k + r0, rc), :]
            y_ref[pl.ds(r0, rc), :] = (b_ref[pl.ds(r0, rc), :] * r).astype(BF16)

    col = lambda j: (0, j)
    return _call(body, jobs, name=name,grid=(DS // tc,), in_specs=_short_specs(T, DS, tc) + [pl.BlockSpec((SHORT_K, tc), col)],
                          out_specs=pl.BlockSpec((T, tc), col), out_shape=jax.ShapeDtypeStruct((T, DS), BF16),
                          scratch_shapes=[pltpu.VMEM((T + SHORT_PAD, tc), F32)], sem=("parallel",))(z, z, z, w)


def _short_bwd(dy, z, w, name, tc=256, jobs=()):
    T, DS = dy.shape
    rc = min(CHUNK, T)

    def body(dy_ref, b_ref, cg_ref, u_ref, w_ref, db_ref, dcg_ref, du_ref, dw_ref, qpad, rpad):
        qpad[pl.ds(0, SHORT_PAD), :] = jnp.zeros((SHORT_PAD, tc), F32)
        qpad[pl.ds(SHORT_PAD, T), :] = cg_ref[...] * u_ref[...]
        rpad[pl.ds(0, T), :] = dy_ref[...] * b_ref[...]
        rpad[pl.ds(T, SHORT_PAD), :] = jnp.zeros((SHORT_PAD, tc), F32)
        accs = [jnp.zeros((8, tc), F32) for _ in range(SHORT_K)]
        for r0 in range(0, T, rc):
            r = jnp.zeros((rc, tc), F32)
            dq = jnp.zeros((rc, tc), F32)
            dr = rpad[pl.ds(r0, rc), :]
            for k in range(SHORT_K):
                q_k = qpad[pl.ds(SHORT_PAD - (SHORT_K - 1) + k + r0, rc), :]
                r = r + w_ref[pl.ds(k, 1), :] * q_k
                dq = dq + w_ref[pl.ds(k, 1), :] * rpad[pl.ds(r0 + (SHORT_K - 1) - k, rc), :]
                accs[k] = accs[k] + jnp.sum((dr * q_k).reshape(rc // 8, 8, tc), axis=0)
            db_ref[pl.ds(r0, rc), :] = (dy_ref[pl.ds(r0, rc), :] * r).astype(BF16)
            dcg_ref[pl.ds(r0, rc), :] = (dq * u_ref[pl.ds(r0, rc), :]).astype(BF16)
            du_ref[pl.ds(r0, rc), :] = (dq * cg_ref[pl.ds(r0, rc), :]).astype(BF16)
        for k in range(SHORT_K):
            dw_ref[pl.ds(k, 1), :] = jnp.sum(accs[k], axis=0, keepdims=True)

    col = lambda j: (0, j)
    tile = pl.BlockSpec((T, tc), col)
    return _call(body, jobs, name=name,grid=(DS // tc,),
                          in_specs=[tile] + _short_specs(T, DS, tc) + [pl.BlockSpec((SHORT_K, tc), col)],
                          out_specs=[tile, tile, tile, pl.BlockSpec((SHORT_K, tc), col)],
                          out_shape=[jax.ShapeDtypeStruct((T, DS), BF16)] * 3 + [jax.ShapeDtypeStruct((SHORT_K, DS), F32)],
                          scratch_shapes=[pltpu.VMEM((T + SHORT_PAD, tc), F32), pltpu.VMEM((T + SHORT_PAD, tc), F32)],
                          sem=("parallel",))(dy, z, z, z, w)


def _tile_rows(rows, cols, n_bufs):
    budget = VMEM_LIMIT_BYTES // 2 // (2 * n_bufs * 4 * cols)
    tr = rows
    while tr > budget and tr % 16 == 0:
        tr //= 2
    return tr


def _placed_call(body, name, place, grid, in_specs, out_specs, out_shape, ins, jobs=()):
    return _call(body, jobs, prefetch=place, name=name, grid=grid, in_specs=in_specs, out_specs=out_specs, out_shape=out_shape,
                 sem=("parallel",))(*ins)


def _cast_into_full(w, layer, kind, place, name, jobs=()):
    _, R, C = w.shape
    tr = _tile_rows(R, C, 2)
    nb = R // tr
    if kind == "col":
        full, out_spec = (R, C * N_CHIPS), pl.BlockSpec((tr, C), lambda i, s: (i, s[0]))
    else:
        full, out_spec = (R * N_CHIPS, C), pl.BlockSpec((tr, C), lambda i, s: (s[0] * nb + i, 0))

    def body(s_ref, w_ref, o_ref):
        o_ref[...] = w_ref[...].astype(BF16)

    return _placed_call(body, name, place, (nb,), [pl.BlockSpec((None, tr, C), lambda i, s: (layer, i, 0))], out_spec,
                        jax.ShapeDtypeStruct(full, BF16), [w], jobs)


def _add_pair(grad, theirs, kind, place, name, jobs=()):
    R, C = grad.shape
    piece_rows = R // 2 if kind == "col" else R // N_CHIPS // 2
    tr = _tile_rows(piece_rows, C, 3)
    nb = piece_rows // tr
    if kind == "col":
        g_spec = pl.BlockSpec((tr, C), lambda i, s: (s[1] * nb + i, 0))
    else:
        g_spec = pl.BlockSpec((tr, C), lambda i, s: ((2 * (i // nb) + s[1]) * nb + i % nb, 0))
    flat = pl.BlockSpec((tr, C), lambda i, s: (i, 0))

    def body(s_ref, a_ref, b_ref, o_ref):
        o_ref[...] = (a_ref[...].astype(F32) + b_ref[...].astype(F32)).astype(BF16)

    return _placed_call(body, name, place, (R // 2 // tr,), [g_spec, flat], flat, jax.ShapeDtypeStruct((R // 2, C), BF16),
                        [grad, theirs], jobs)


def _sum_chips(chip_sum, arrived, kind, place, name, jobs=()):
    _, H, W = arrived.shape
    tr = _tile_rows(H, W, 6)
    nb = H // tr
    if kind == "col":
        own_spec = pl.BlockSpec((tr, W), lambda i, s: (i, s[0]))
    else:
        own_spec = pl.BlockSpec((tr, W), lambda i, s: (s[0] * nb + i, 0))

    def body(s_ref, p_ref, r_ref, o_ref):
        acc = p_ref[...].astype(F32)
        for i in range(N_CHIPS - 1):
            acc = acc + r_ref[i].astype(F32)
        o_ref[...] = acc

    return _placed_call(body, name, place, (nb,), [own_spec, pl.BlockSpec((N_CHIPS - 1, tr, W), lambda i, s: (0, i, 0))],
                        pl.BlockSpec((tr, W), lambda i, s: (s[1] * nb + i, 0)), jax.ShapeDtypeStruct((2 * H, W), F32),
                        [chip_sum, arrived], jobs)


def _adamw_values(w, g, m, v):
    m = ADAM_B1 * m + (1.0 - ADAM_B1) * g
    v = ADAM_B2 * v + (1.0 - ADAM_B2) * (g * g)
    m_hat = m / (1.0 - ADAM_B1 ** ADAM_STEP)
    v_hat = v / (1.0 - ADAM_B2 ** ADAM_STEP)
    return -ADAM_LR * (m_hat / (jnp.sqrt(v_hat) + ADAM_EPS) + ADAM_WD * w), m, v


def _adamw(w, g, m, v, name, layer=0, carried=None):
    L, R, C = w.shape
    tr = _tile_rows(R, C, 8)

    def body(w_ref, g_ref, m_ref, v_ref, *rest):
        go_ref, d_ref, mo_ref, vo_ref = rest[-4:]
        g_val = g_ref[...]
        d, m_new, v_new = _adamw_values(w_ref[...], g_val, m_ref[...], v_ref[...])
        go_ref[...], d_ref[...], mo_ref[...], vo_ref[...] = g_val, d, m_new, v_new

    lay = pl.BlockSpec((None, tr, C), lambda i: (layer, i, 0))
    ins = [w, g, m, v]
    in_specs = [lay, pl.BlockSpec((tr, C), lambda i: (i, 0)), lay, lay]
    aliases = {}
    if carried is not None:
        ins += list(carried)
        in_specs += [pl.BlockSpec(memory_space=pl.ANY)] * 4
        aliases = {4 + i: i for i in range(4)}
    return _call(body, (), name=name, grid=(R // tr,), in_specs=in_specs, out_specs=[lay] * 4,
                 out_shape=[jax.ShapeDtypeStruct((L, R, C), F32)] * 4, input_output_aliases=aliases, sem=("parallel",))(*ins)


def _aligned(v, m):
    return v if isinstance(v, int) else pl.multiple_of(v, m)


def _place():
    x, y, c = lax.axis_index("x"), lax.axis_index("y"), lax.axis_index("c")
    other_chips = [(x, 1 - y), (1 - x, y), (1 - x, 1 - y)]
    return x, y, c, 2 * x + y, other_chips


def _chip_index(chip):
    return 2 * chip[0] + chip[1]


def _piece(ref, kind, k, h):
    R, C = ref.shape
    if kind == "col":
        return ref.at[pl.ds(_aligned(h * (R // 2), 16), R // 2), pl.ds(_aligned(k * (C // N_CHIPS), 128), C // N_CHIPS)]
    rs = R // N_CHIPS
    return ref.at[pl.ds(_aligned(k * rs + h * (rs // 2), 16), rs // 2), :]


def _compact_piece(ref, kind, k):
    R2, C = ref.shape
    if kind == "col":
        return ref.at[:, pl.ds(_aligned(k * (C // N_CHIPS), 128), C // N_CHIPS)]
    return ref.at[pl.ds(_aligned(k * (R2 // N_CHIPS), 16), R2 // N_CHIPS), :]


def _half_rows(ref, h):
    R = ref.shape[0]
    return ref.at[pl.ds(_aligned(h * (R // 2), 16), R // 2), :]


class _Copies:
    def __init__(self, send_sems, recv_sems):
        self.send_sems, self.recv_sems = send_sems, recv_sems
        self.n_remote = 0

    def remote(self, src, dst, device):
        k = self.n_remote
        self.n_remote += 1
        return pltpu.make_async_remote_copy(src_ref=src, dst_ref=dst, send_sem=self.send_sems.at[k], recv_sem=self.recv_sems.at[k],
                                            device_id=device, device_id_type=MESH)


class _Job:
    def __init__(self, ins, out_shape, aliases, n_remote, build):
        self.ins, self.out_shape, self.aliases, self.n_remote, self.build = list(ins), list(out_shape), dict(aliases), n_remote, build


def _merge_jobs(jobs):
    ins, out_shape, aliases, spans = [], [], {}, []
    for job in jobs:
        spans.append((len(ins), len(job.ins), len(out_shape), len(job.out_shape)))
        aliases.update({len(ins) + i: len(out_shape) + o for i, o in job.aliases.items()})
        ins += job.ins
        out_shape += job.out_shape

    def build(in_refs, out_refs, cp):
        copies = []
        for job, (i0, ni, o0, no) in zip(jobs, spans):
            copies += job.build(in_refs[i0:i0 + ni], out_refs[o0:o0 + no], cp)
        return copies

    return _Job(ins, out_shape, aliases, sum(j.n_remote for j in jobs), build)


def _run_jobs(jobs, name):
    job = _merge_jobs(jobs)
    n_in, n_out = len(job.ins), len(job.out_shape)

    def body(*refs):
        copies = job.build(refs[:n_in], refs[n_in:n_in + n_out], _Copies(*refs[n_in + n_out:]))
        for d in copies:
            d.start()
        for d in copies:
            d.wait_recv()
        for d in copies:
            d.wait_send()

    outs = pl.pallas_call(
        body, name=name, in_specs=[HBM] * n_in, out_specs=[HBM] * n_out, out_shape=job.out_shape,
        input_output_aliases=job.aliases,
        scratch_shapes=[pltpu.SemaphoreType.DMA((job.n_remote,)), pltpu.SemaphoreType.DMA((job.n_remote,))])(*job.ins)
    split, o0 = [], 0
    for j in jobs:
        split.append(outs[o0:o0 + len(j.out_shape)])
        o0 += len(j.out_shape)
    return split


class _Flying:
    def __init__(self, job, send_sems, recv_sems, bufs, token):
        self.job, self.send_sems, self.recv_sems, self.bufs, self.token = job, send_sems, recv_sems, bufs, token


def _job_refs(job, buf_refs):
    n_out = len(job.out_shape)
    kept = [i for i in range(len(job.ins)) if i not in job.aliases]
    ins = [buf_refs[job.aliases[i]] if i in job.aliases else buf_refs[n_out + kept.index(i)] for i in range(len(job.ins))]
    return ins, list(buf_refs[:n_out])


def _start_job(job, name, after=()):
    n_in, n_out, n_after = len(job.ins), len(job.out_shape), len(after)
    kept = [i for i in range(n_in) if i not in job.aliases]
    n_bufs = n_out + len(kept)

    def body(*refs):
        in_refs, out_refs = refs[:n_in], refs[n_in + n_after:n_in + n_after + n_out]
        send_sems, recv_sems, token = refs[n_in + n_after + n_bufs:]
        for d in job.build(in_refs, out_refs, _Copies(send_sems, recv_sems)):
            d.start()
        token[...] = jnp.zeros_like(token)

    aliases = dict(job.aliases)
    aliases.update({i: n_out + k for k, i in enumerate(kept)})
    sems = pltpu.SemaphoreType.DMA((job.n_remote,))
    outs = pl.pallas_call(
        body, name=name, in_specs=[HBM] * n_in + [ANY] * n_after,
        out_specs=[HBM] * n_bufs + [SEM, SEM, pl.BlockSpec(memory_space=pltpu.VMEM)],
        out_shape=job.out_shape + [jax.ShapeDtypeStruct(job.ins[i].shape, job.ins[i].dtype) for i in kept]
        + [sems, sems, jax.ShapeDtypeStruct((8, 128), F32)],
        input_output_aliases=aliases,
        compiler_params=pltpu.CompilerParams(has_side_effects=pltpu.SideEffectType.DATAFLOW_SIDE_EFFECTING))(*job.ins, *after)
    return _Flying(job, outs[n_bufs], outs[n_bufs + 1], list(outs[:n_bufs]), outs[n_bufs + 2])


def _wait_job(flying, name, after=()):
    job, n_bufs, n_after = flying.job, len(flying.bufs), len(after)

    def body(*refs):
        in_refs, out_refs = _job_refs(job, refs[:n_bufs])
        send_sems, recv_sems = refs[n_bufs:n_bufs + 2]
        copies = job.build(in_refs, out_refs, _Copies(send_sems, recv_sems))
        for d in copies:
            d.wait_send()
        for d in copies:
            d.wait_recv()

    outs = pl.pallas_call(
        body, name=name, in_specs=[HBM] * n_bufs + [SEM, SEM] + [ANY] * n_after, out_specs=[HBM] * n_bufs,
        out_shape=[jax.ShapeDtypeStruct(b.shape, b.dtype) for b in flying.bufs],
        input_output_aliases={i: i for i in range(n_bufs)},
        compiler_params=pltpu.CompilerParams(has_side_effects=pltpu.SideEffectType.DATAFLOW_SIDE_EFFECTING))(
            *flying.bufs, flying.send_sems, flying.recv_sems, *after)
    return list(outs[:len(job.out_shape)]), list(outs[len(job.out_shape):])


class _Beside:
    def __init__(self, jobs):
        self.jobs = list(jobs)
        job = self.job = _merge_jobs(self.jobs) if self.jobs else None
        self.ins = job.ins if job else []
        self.out_shape = job.out_shape if job else []
        self.n_in, self.n_out = len(self.ins), len(self.out_shape)

    def scratch(self):
        n = self.job.n_remote if self.job else 0
        return [pltpu.SemaphoreType.DMA((n,)), pltpu.SemaphoreType.DMA((n,))] if n else []

    def aliases(self, first_in, first_out):
        return {first_in + i: first_out + o for i, o in self.job.aliases.items()} if self.job else {}

    def _at(self, grid, which):
        cond = None
        for axis, n in enumerate(grid):
            here = pl.program_id(axis) == (0 if which == "first" else n - 1)
            cond = here if cond is None else jnp.logical_and(cond, here)
        return cond

    def start(self, in_refs, out_refs, scratch, grid):
        if self.job:
            @pl.when(self._at(grid, "first"))
            def _():
                for d in self.job.build(in_refs, out_refs, _Copies(*scratch[-2:])):
                    d.start()

    def finish(self, in_refs, out_refs, scratch, grid):
        if self.job:
            @pl.when(self._at(grid, "last"))
            def _():
                copies = self.job.build(in_refs, out_refs, _Copies(*scratch[-2:]))
                for d in copies:
                    d.wait_recv()
                for d in copies:
                    d.wait_send()

    def split(self, outs):
        per_job, o0 = [], 0
        for j in self.jobs:
            per_job.append(list(outs[o0:o0 + len(j.out_shape)]))
            o0 += len(j.out_shape)
        return per_job


def _in_place(arrays):
    return [jax.ShapeDtypeStruct(a.shape, a.dtype) for a in arrays], {u: u for u in range(len(arrays))}


def _rows_part(ref, part, n_parts):
    h = ref.shape[0] // n_parts
    return ref.at[pl.ds(part * h, h), :]


def _gather_job(full, kind, n_parts, ici_parts=(), forward_parts=()):
    def build(in_refs, out_refs, cp):
        x, y, c, me, chips = _place()
        (ref,) = out_refs
        copies = []
        for p in ici_parts:
            mine = _rows_part(_piece(ref, kind, me, c), p, n_parts)
            copies += [cp.remote(mine, mine, (*chip, c)) for chip in chips]
        for p in forward_parts:
            for chip in chips:
                piece = _rows_part(_piece(ref, kind, _chip_index(chip), c), p, n_parts)
                copies.append(cp.remote(piece, piece, (x, y, 1 - c)))
        return copies

    return _Job([full], *_in_place([full]), 3 * (len(ici_parts) + len(forward_parts)), build)


def _gather_small_job(fulls, axes):
    def build(in_refs, out_refs, cp):
        x, y, c, me, chips = _place()
        copies = []
        for ref, ax in zip(out_refs, axes):
            n = ref.shape[ax] // N_CHIPS
            idx = [slice(None)] * len(ref.shape)
            idx[ax] = pl.ds(_aligned(me * n, n), n)
            mine = ref.at[tuple(idx)]
            copies += [cp.remote(mine, mine, (*chip, c)) for chip in chips]
        return copies

    return _Job(fulls, *_in_place(fulls), 3 * len(fulls), build)


def _exchange_halves_job(grads, kinds):
    def build(in_refs, out_refs, cp):
        x, y, c, me, chips = _place()
        copies = []
        for src, dst, kind in zip(in_refs, out_refs, kinds):
            if kind == "col":
                copies.append(cp.remote(_half_rows(src, 1 - c), dst, (x, y, 1 - c)))
            else:
                copies += [cp.remote(_piece(src, "row", k, 1 - c), _compact_piece(dst, "row", k), (x, y, 1 - c))
                           for k in range(N_CHIPS)]
        return copies

    out_shape = [jax.ShapeDtypeStruct((g.shape[0] // 2, g.shape[1]), g.dtype) for g in grads]
    return _Job(grads, out_shape, {}, sum(1 if k == "col" else N_CHIPS for k in kinds), build)


def _scatter_job(half, kind, n_parts, parts, into=None):
    def build(in_refs, out_refs, cp):
        x, y, c, me, chips = _place()
        src, (dst,) = in_refs[0], out_refs
        copies = []
        for p in parts:
            copies += [cp.remote(_rows_part(_compact_piece(src, kind, _chip_index(chip)), p, n_parts),
                                 _rows_part(dst.at[r], p, n_parts), (*chip, c)) for r, chip in enumerate(chips)]
        return copies

    part_shape = (half.shape[0], half.shape[1] // N_CHIPS) if kind == "col" else (half.shape[0] // N_CHIPS, half.shape[1])
    out_shape = [jax.ShapeDtypeStruct((N_CHIPS - 1,) + part_shape, half.dtype)]
    ins, aliases = ([half], {}) if into is None else ([half, into], {1: 0})
    return _Job(ins, out_shape, aliases, 3 * len(parts), build)


def _share_job(shards):
    def build(in_refs, out_refs, cp):
        x, y, c, me, chips = _place()
        copies = []
        for ref in out_refs:
            mine = _half_rows(ref, c)
            copies.append(cp.remote(mine, mine, (x, y, 1 - c)))
        return copies

    return _Job(shards, *_in_place(shards), len(shards), build)


N_DEVICES = 2 * N_CHIPS


def _small_exchange_job(slots, n_parts, parts):
    def build(in_refs, out_refs, cp):
        x, y, c, me, chips = _place()
        (ref,) = out_refs
        copies = []
        for part in parts:
            mine = _rows_part(ref.at[2 * me + c], part, n_parts)
            copies += [cp.remote(mine, mine, (x ^ (p >> 2), y ^ ((p >> 1) & 1), c ^ (p & 1))) for p in range(1, N_DEVICES)]
        return copies

    return _Job([slots], *_in_place([slots]), (N_DEVICES - 1) * len(parts), build)


def _sum_slots(slots, name, jobs=()):
    n, R, C = slots.shape

    def body(s_ref, o_ref):
        acc = s_ref[0]
        for i in range(1, n):
            acc = acc + s_ref[i]
        o_ref[...] = acc

    return _call(body, jobs, name=name, grid=(1,), in_specs=[pl.BlockSpec((n, R, C), lambda i: (0, 0, 0))],
                 out_specs=pl.BlockSpec((R, C), lambda i: (0, 0)), out_shape=jax.ShapeDtypeStruct((R, C), F32),
                 sem=("arbitrary",))(slots)


def _pack(arrays, width):
    rows = []
    for a in arrays:
        flat = a.reshape(-1)
        n_rows = -(-flat.shape[0] // width)
        rows.append(jnp.pad(flat, (0, n_rows * width - flat.shape[0])).reshape(n_rows, width))
    n = sum(r.shape[0] for r in rows)
    rows.append(jnp.zeros((-n % 8, width), F32))
    return jnp.concatenate(rows, axis=0)


def _unpack(packed, shapes):
    out, r0, width = [], 0, packed.shape[1]
    for shape in shapes:
        size = 1
        for d in shape:
            size *= d
        n_rows = -(-size // width)
        out.append(packed[r0:r0 + n_rows].reshape(-1)[:size].reshape(shape))
        r0 += n_rows
    return out


class _Backlog:
    def __init__(self):
        self.now, self.free, self.flights, self.last = 0.0, {"ici": 0.0, "d2d": 0.0}, [], None

    def run(self, fn, us, *args, **kw):
        out = fn(*args, **kw)
        self.now += us
        self.last = out[0] if isinstance(out, (list, tuple)) else out
        self.poll()
        return out

    def start(self, job, name, link, cost, done):
        flying = _start_job(job, name + "_start")
        _Behind.pending.append(flying.token)
        ends = max(self.now, self.free[link]) + cost
        self.free[link] = ends
        self.flights.append((ends + LANDING_SLACK_US, name, flying, done))
        self.flights.sort(key=lambda f: f[0])

    def poll(self, block=False):
        while self.flights and (block or self.flights[0][0] <= self.now):
            ends, name, flying, done = self.flights.pop(0)
            self.now, block = max(self.now, ends), False
            done(*_wait_job(flying, name + "_wait", [self.last]))


class _GradStream:
    def __init__(self, backlog, u, name, kind, cost, g, place, results):
        self.backlog, self.u, self.name, self.kind, self.cost, self.place, self.results = backlog, u, name, kind, cost, place, results
        backlog.start(_exchange_halves_job([g], [kind]), "to_sibling_" + name, "d2d", D2D_SHARE * cost, self.exchanged)

    def exchanged(self, outs, kept):
        chip_sum = self.backlog.run(_add_pair, SIDE_KERNEL_US, kept[0], outs[0], self.kind, self.place, "chip_sum_" + self.name)
        self.backlog.start(_scatter_job(chip_sum, self.kind, 1, [0]), "to_owners_" + self.name, "ici", self.cost, self.scattered)

    def scattered(self, outs, kept):
        reduced = self.backlog.run(_sum_chips, SIDE_KERNEL_US, kept[0], outs[0], self.kind, self.place, "reduce_" + self.name)
        self.backlog.start(_share_job([reduced]), "share_" + self.name, "d2d", D2D_SHARE * self.cost, self.shared)

    def shared(self, outs, kept):
        self.results[self.u] = outs[0]


SIDE_KERNEL_US = 12.0
D2D_SHARE = 0.15
LANDING_SLACK_US = 5.0


def kernel(x, mix_pre_g, mix_post_g, ffn_pre_g, ffn_post_g, ab_w_in, pool_w, pool_scale, conv_w, conv_b, conv_ln_g, conv_ln_b, ab_w_out, sc_w_in, sc_conv_w, sc_w_out, ffn_w1, ffn_w2, loss_target, m_mix_pre_g, m_mix_post_g, m_ffn_pre_g, m_ffn_post_g, m_ab_w_in, m_pool_w, m_pool_scale, m_conv_w, m_conv_b, m_conv_ln_g, m_conv_ln_b, m_ab_w_out, m_sc_w_in, m_sc_conv_w, m_sc_w_out, m_ffn_w1, m_ffn_w2, v_mix_pre_g, v_mix_post_g, v_ffn_pre_g, v_ffn_post_g, v_ab_w_in, v_pool_w, v_pool_scale, v_conv_w, v_conv_b, v_conv_ln_g, v_conv_ln_b, v_ab_w_out, v_sc_w_in, v_sc_conv_w, v_sc_w_out, v_ffn_w1, v_ffn_w2):
    x0, target = x[0], loss_target[0]
    T, D = x0.shape
    DP = pool_scale.shape[-1]
    gain = lambda g, layer: g[layer][None, :]

    big = [("ab_w_in", ab_w_in, 0, "col", 4, 67.0), ("ab_w_out", ab_w_out, 0, "row", 2, 44.0),
           ("ffn_w1_0", ffn_w1, 0, "col", 8, 177.0), ("ffn_w2_0", ffn_w2, 0, "row", 8, 177.0),
           ("sc_w_in", sc_w_in, 0, "col", 4, 133.0), ("sc_w_out", sc_w_out, 0, "row", 2, 44.0),
           ("ffn_w1_1", ffn_w1, 1, "col", 8, 177.0), ("ffn_w2_1", ffn_w2, 1, "row", 8, 177.0)]
    kinds = [b[3] for b in big]
    chip = 2 * lax.axis_index("x") + lax.axis_index("y")
    place = jnp.stack([chip, lax.axis_index("c")]).astype(jnp.int32)

    def own_in_zeros(shard, ax):
        full = jnp.zeros(tuple(d * N_CHIPS if i == ax else d for i, d in enumerate(shard.shape)), shard.dtype)
        return lax.dynamic_update_slice_in_dim(full, shard, chip * shard.shape[ax], axis=ax)

    W = [_cast_into_full(w, layer, kind, place, "cast_" + name) for name, w, layer, kind, _, _ in big]
    smalls = [own_in_zeros(pool_w[0], 1), own_in_zeros(conv_w[0], 1), own_in_zeros(sc_conv_w[0], 1)]
    flying = {}

    def start_gather(u, after=()):
        if u < len(big):
            flying[u] = _start_job(_gather_job(W[u], kinds[u], 1, ici_parts=[0]), "gather_start_%d" % u, after)

    def ready(u, after):
        (W[u],), _ = _wait_job(flying[u], "gather_wait_%d" % u, [after])
        start_gather(u + 3, [W[u]])
        onward = _start_job(_gather_job(W[u], kinds[u], 1, forward_parts=[0]), "forward_start_%d" % u)
        (W[u],), _ = _wait_job(onward, "forward_wait_%d" % u, [onward.token])
        return W[u]

    small_flight = _start_job(_gather_small_job(smalls, [1, 1, 1]), "gather_start_small")
    start_gather(0, [small_flight.token])
    start_gather(1, [flying[0].token])
    start_gather(2, [flying[1].token])
    _Behind.pending = [f.token for f in flying.values()] + [small_flight.token]

    relu_sq = lambda acc: (jnp.maximum(acc, 0.0), jnp.square(jnp.maximum(acc, 0.0)))
    relu_sq_bwd = lambda acc, a: (acc * (2.0 * a.astype(F32)),)

    h0 = _norm_fwd(x0, gain(mix_pre_g, 0), "norm_in")
    (pool_w_full, conv_w_full, sc_conv_w_full), _ = _wait_job(small_flight, "gather_wait_small", [h0])
    z0 = _matmul(h0, ready(0, h0), "nn", "mix0_in")
    pooled, y_pool = _pool_fwd(z0, pool_w_full, pool_scale, "pool_fwd")
    a_conv, c_conv = _conv_fwd(z0, conv_w_full, conv_b, DP, "conv_fwd")
    y_conv = _ln_silu_fwd(c_conv, conv_ln_g, conv_ln_b, "ln_silu_fwd")
    y0 = jnp.concatenate([y_pool, y_conv], axis=1)
    m0 = _matmul(y0, ready(1, y0), "nn", "mix0_out")
    x1, h1 = _residual_norm(x0, m0, gain(mix_post_g, 0), gain(ffn_pre_g, 0), "res_mix0")
    a0, a0sq = _matmul(h1, ready(2, h1), "nn", "ffn0_up", out_dtypes=(BF16, BF16), epilogue=relu_sq)
    f0 = _matmul(a0sq, ready(3, a0sq), "nn", "ffn0_down")
    x2, h2 = _residual_norm(x1, f0, gain(ffn_post_g, 0), gain(mix_pre_g, 1), "res_ffn0")
    z1 = _matmul(h2, ready(4, h2), "nn", "mix1_in")
    y1 = _short_fwd(z1, sc_conv_w_full, "short_fwd")
    m1 = _matmul(y1, ready(5, y1), "nn", "mix1_out")
    x3, h3 = _residual_norm(x2, m1, gain(mix_post_g, 1), gain(ffn_pre_g, 1), "res_mix1")
    a1, a1sq = _matmul(h3, ready(6, h3), "nn", "ffn1_up", out_dtypes=(BF16, BF16), epilogue=relu_sq)
    f1 = _matmul(a1sq, ready(7, a1sq), "nn", "ffn1_down")
    w_in0, w_out0, w1_0, w2_0, w_in1, w_out1, w1_1, w2_1 = W

    grads_big = [None] * len(big)
    backlog = _Backlog()
    run = backlog.run

    def reduce_grad(u, g):
        name, _, _, kind, _, cost = big[u]
        _GradStream(backlog, u, name, kind, cost, g, place, grads_big)

    dx, df1, d_ffn_post_1, loss_row = run(_loss_and_last_norm_bwd, 30.0, x3, f1, gain(ffn_post_g, 1), target, "loss")
    reduce_grad(7, run(_matmul, 80.0, a1sq, df1, "tn", "ffn1_down_dw", out_dtypes=(BF16,)))
    dz = run(_matmul, 82.0, df1, w2_1, "nt", "ffn1_down_dx", out_dtypes=(BF16,), epilogue=relu_sq_bwd, epi=(a1,))
    reduce_grad(6, run(_matmul, 80.0, h3, dz, "tn", "ffn1_up_dw", out_dtypes=(BF16,)))
    dh = run(_matmul, 87.0, dz, w1_1, "nt", "ffn1_up_dx")
    dx, d_ffn_pre_1, dm1, d_mix_post_1 = run(_norms_bwd, 36.0, dx, dh, x3, gain(ffn_pre_g, 1), m1, gain(mix_post_g, 1), "norms_bwd3")

    reduce_grad(5, run(_matmul, 24.0, y1, dm1, "tn", "mix1_out_dw", out_dtypes=(BF16,)))
    dy1 = run(_matmul, 25.0, dm1, w_out1, "nt", "mix1_out_dx")
    db, dcg, du, d_sc_conv_w = run(_short_bwd, 41.0, dy1, z1, sc_conv_w_full, "short_bwd")
    dz1 = jnp.concatenate([db, dcg, du], axis=1)
    reduce_grad(4, run(_matmul, 62.0, h2, dz1, "tn", "mix1_in_dw", out_dtypes=(BF16,)))
    dh = run(_matmul, 68.0, dz1, w_in1, "nt", "mix1_in_dx")
    dx, d_mix_pre_1, df0, d_ffn_post_0 = run(_norms_bwd, 35.0, dx, dh, x2, gain(mix_pre_g, 1), f0, gain(ffn_post_g, 0), "norms_bwd2")

    reduce_grad(3, run(_matmul, 80.0, a0sq, df0, "tn", "ffn0_down_dw", out_dtypes=(BF16,)))
    dz = run(_matmul, 82.0, df0, w2_0, "nt", "ffn0_down_dx", out_dtypes=(BF16,), epilogue=relu_sq_bwd, epi=(a0,))
    reduce_grad(2, run(_matmul, 80.0, h1, dz, "tn", "ffn0_up_dw", out_dtypes=(BF16,)))
    dh = run(_matmul, 87.0, dz, w1_0, "nt", "ffn0_up_dx")
    dx, d_ffn_pre_0, dm0, d_mix_post_0 = run(_norms_bwd, 36.0, dx, dh, x1, gain(ffn_pre_g, 0), m0, gain(mix_post_g, 0), "norms_bwd1")

    reduce_grad(1, run(_matmul, 24.0, y0, dm0, "tn", "mix0_out_dw", out_dtypes=(BF16,)))
    dy0 = run(_matmul, 25.0, dm0, w_out0, "nt", "mix0_out_dx")
    du_pool, d_pool_w, d_pool_scale = run(_pool_bwd, 28.0, dy0, pooled, pool_w_full, pool_scale, "pool_bwd")
    dc, d_ln_g, d_ln_b = run(_ln_silu_bwd, 15.0, dy0, c_conv, conv_ln_g, conv_ln_b, "ln_silu_bwd")
    dv, dgate, d_conv_w, d_conv_b = run(_conv_bwd, 52.0, dc, a_conv, z0, conv_w_full, DP, "conv_bwd")
    dz0 = jnp.concatenate([du_pool, dv, dgate], axis=1)

    device_slot = 2 * chip + lax.axis_index("c")
    small_sums = {}

    def exchange_small(key, arrays, cost):
        slots = lax.dynamic_update_slice_in_dim(jnp.zeros((N_DEVICES,) + _pack(arrays, D).shape, F32), _pack(arrays, D)[None],
                                                device_slot, axis=0)
        backlog.start(_small_exchange_job(slots, 1, [0]), "small_grads_" + key, "ici", cost,
                      lambda outs, kept: small_sums.__setitem__(key, _unpack(_sum_slots(outs[0], "small_grads_sum_" + key),
                                                                             [a.shape for a in arrays])))

    exchange_small("most", [d_mix_pre_1, jnp.concatenate([d_mix_post_0, d_mix_post_1], 0),
                            jnp.concatenate([d_ffn_pre_0, d_ffn_pre_1], 0), jnp.concatenate([d_ffn_post_0, d_ffn_post_1], 0),
                            d_pool_scale, d_conv_b, d_ln_g, d_ln_b, d_pool_w, d_conv_w, d_sc_conv_w], 112.0)
    reduce_grad(0, run(_matmul, 34.0, h0, dz0, "tn", "mix0_in_dw", out_dtypes=(BF16,)))
    dh = run(_matmul, 40.0, dz0, w_in0, "nt", "mix0_in_dx")
    grad_x, d_mix_pre_0 = run(_norms_bwd, 26.0, dx, dh, x0, gain(mix_pre_g, 0), None, None, "norms_bwd0")
    exchange_small("last", [d_mix_pre_0], 5.0)
    loss = lax.psum(loss_row[0, 0], ("x", "y", "c"))

    upd, gr, first = {}, grads_big, {}

    def keep(where, key, outs):
        where[key] = outs
        return outs

    adamw_big = [
        (7, lambda: keep(first, "ffn_w2", _adamw(ffn_w2, gr[7], m_ffn_w2, v_ffn_w2, "adamw_ffn_w2_1", layer=1)), 46.0),
        (6, lambda: keep(first, "ffn_w1", _adamw(ffn_w1, gr[6], m_ffn_w1, v_ffn_w1, "adamw_ffn_w1_1", layer=1)), 46.0),
        (5, lambda: keep(upd, "sc_w_out", _adamw(sc_w_out, gr[5], m_sc_w_out, v_sc_w_out, "adamw_sc_w_out")), 14.0),
        (4, lambda: keep(upd, "sc_w_in", _adamw(sc_w_in, gr[4], m_sc_w_in, v_sc_w_in, "adamw_sc_w_in")), 35.0),
        (3, lambda: keep(upd, "ffn_w2", _adamw(ffn_w2, gr[3], m_ffn_w2, v_ffn_w2, "adamw_ffn_w2_0", layer=0,
                                               carried=first["ffn_w2"])), 46.0),
        (2, lambda: keep(upd, "ffn_w1", _adamw(ffn_w1, gr[2], m_ffn_w1, v_ffn_w1, "adamw_ffn_w1_0", layer=0,
                                               carried=first["ffn_w1"])), 46.0),
        (1, lambda: keep(upd, "ab_w_out", _adamw(ab_w_out, gr[1], m_ab_w_out, v_ab_w_out, "adamw_ab_w_out")), 14.0),
        (0, lambda: keep(upd, "ab_w_in", _adamw(ab_w_in, gr[0], m_ab_w_in, v_ab_w_in, "adamw_ab_w_in")), 19.0)]
    while adamw_big or backlog.flights:
        due = [a for a in adamw_big if gr[a[0]] is not None]
        if due:
            adamw_big.remove(due[0])
            backlog.run(due[0][1], due[0][2])
        else:
            backlog.poll(block=True)

    (g_mix_pre_1, g_mix_post, g_ffn_pre, g_ffn_post, g_pool_scale, g_conv_b, g_ln_g, g_ln_b, g_pool_w_full, g_conv_w_full,
     g_sc_conv_w_full) = small_sums["most"]
    g_mix_pre = jnp.concatenate([small_sums["last"][0], g_mix_pre_1], 0)
    own = lambda a, ax: lax.dynamic_slice_in_dim(a, chip * (a.shape[ax] // N_CHIPS), a.shape[ax] // N_CHIPS, axis=ax)
    g_pool_w, g_conv_w, g_sc_conv_w = own(g_pool_w_full, 1), own(g_conv_w_full, 1), own(g_sc_conv_w_full, 1)

    def small_update(w, g, m, v, name):
        shape = w.shape
        as3 = lambda a: a.reshape((1, -1, shape[-1]))
        outs = _adamw(as3(w), g.reshape((-1, shape[-1])), as3(m), as3(v), "adamw_" + name)
        return [o.reshape(shape) for o in outs]

    upd["mix_pre_g"] = small_update(mix_pre_g, g_mix_pre, m_mix_pre_g, v_mix_pre_g, "mix_pre_g")
    upd["mix_post_g"] = small_update(mix_post_g, g_mix_post, m_mix_post_g, v_mix_post_g, "mix_post_g")
    upd["ffn_pre_g"] = small_update(ffn_pre_g, g_ffn_pre, m_ffn_pre_g, v_ffn_pre_g, "ffn_pre_g")
    upd["ffn_post_g"] = small_update(ffn_post_g, g_ffn_post, m_ffn_post_g, v_ffn_post_g, "ffn_post_g")
    upd["pool_w"] = small_update(pool_w, g_pool_w, m_pool_w, v_pool_w, "pool_w")
    upd["pool_scale"] = small_update(pool_scale, g_pool_scale, m_pool_scale, v_pool_scale, "pool_scale")
    upd["conv_w"] = small_update(conv_w, g_conv_w, m_conv_w, v_conv_w, "conv_w")
    upd["conv_b"] = small_update(conv_b, g_conv_b, m_conv_b, v_conv_b, "conv_b")
    upd["conv_ln_g"] = small_update(conv_ln_g, g_ln_g, m_conv_ln_g, v_conv_ln_g, "conv_ln_g")
    upd["conv_ln_b"] = small_update(conv_ln_b, g_ln_b, m_conv_ln_b, v_conv_ln_b, "conv_ln_b")
    upd["sc_conv_w"] = small_update(sc_conv_w, g_sc_conv_w, m_sc_conv_w, v_sc_conv_w, "sc_conv_w")

    order = ["mix_pre_g", "mix_post_g", "ffn_pre_g", "ffn_post_g", "ab_w_in", "pool_w", "pool_scale", "conv_w", "conv_b",
             "conv_ln_g", "conv_ln_b", "ab_w_out", "sc_w_in", "sc_conv_w", "sc_w_out", "ffn_w1", "ffn_w2"]
    out = [loss, grad_x[None]]
    for part in range(4):
        out += [upd[n][part] for n in order]
    return tuple(out)
```

```python
import jax
import jax.numpy as jnp
from jax import lax
from jax.experimental import pallas as pl
from jax.experimental.pallas import tpu as pltpu

F32, BF16 = jnp.float32, jnp.bfloat16
EPS = 1e-6
N_GROUPS = 4
MAX_WINDOW = 16
CONV_K = 31
SHORT_K = 3
CONV_PAD = 32
SHORT_PAD = 8
ADAM_LR, ADAM_B1, ADAM_B2, ADAM_EPS, ADAM_WD, ADAM_STEP = 0.001, 0.9, 0.999, 1e-08, 0.01, 10
N_CHIPS = 4
VMEM_LIMIT_BYTES = 56 * 1024 * 1024
ROW_TILE = 256
CHUNK = 256
MESH = pl.DeviceIdType.MESH
HBM = pl.BlockSpec(memory_space=pltpu.HBM)
SEM = pl.BlockSpec(memory_space=pltpu.SEMAPHORE)
ANY = pl.BlockSpec(memory_space=pl.ANY)


def _cp(*sem):
    return pltpu.CompilerParams(dimension_semantics=sem, vmem_limit_bytes=VMEM_LIMIT_BYTES)


def _sigmoid(v):
    return 1.0 / (1.0 + jnp.exp(-v))


class _Behind:
    pending = []


def _call(body, jobs, prefetch=None, **kw):
    beside = _Beside(jobs)
    behind, _Behind.pending = _Behind.pending, []
    grid = kw["grid"]
    single = not isinstance(kw["out_shape"], (list, tuple))
    in_specs, scratch = list(kw["in_specs"]), list(kw.get("scratch_shapes", ()))
    out_shape = [kw["out_shape"]] if single else list(kw["out_shape"])
    out_specs = [kw["out_specs"]] if single else list(kw["out_specs"])
    n_pre = 0 if prefetch is None else 1
    n_own, n_out, n_scr = len(in_specs), len(out_shape), len(scratch)
    in_specs += [ANY] * len(behind)
    n_in = len(in_specs)

    def wrapped(*refs):
        pre, refs = refs[:n_pre], refs[n_pre:]
        ins, job_ins = refs[:n_own], refs[n_in:n_in + beside.n_in]
        outs = refs[n_in + beside.n_in:n_in + beside.n_in + n_out]
        job_outs = refs[n_in + beside.n_in + n_out:n_in + beside.n_in + n_out + beside.n_out]
        scr = refs[n_in + beside.n_in + n_out + beside.n_out:]
        beside.start(job_ins, job_outs, scr, grid)
        body(*pre, *ins, *outs, *scr[:n_scr])
        beside.finish(job_ins, job_outs, scr, grid)

    aliases = {n_pre + i: o for i, o in kw.get("input_output_aliases", {}).items()}
    aliases.update(beside.aliases(n_pre + n_in, n_out))
    specs = dict(grid=grid, in_specs=in_specs + [HBM] * beside.n_in, out_specs=out_specs + [HBM] * beside.n_out)
    if prefetch is None:
        specs["scratch_shapes"] = scratch + beside.scratch()
    else:
        specs = dict(grid_spec=pltpu.PrefetchScalarGridSpec(num_scalar_prefetch=1, scratch_shapes=scratch + beside.scratch(), **specs))
    sem = ["arbitrary"] * len(grid) if jobs else kw["sem"]
    call = pl.pallas_call(wrapped, name=kw["name"], out_shape=out_shape + beside.out_shape, input_output_aliases=aliases,
                          compiler_params=_cp(*sem), **specs)

    def run(*args):
        outs = call(*([prefetch] * n_pre), *args, *behind, *beside.ins)
        own = outs[0] if single else list(outs[:n_out])
        return (own, beside.split(outs[n_out:])) if jobs else own

    return run


_DIMS = {"nn": (((1,), (0,)), ((), ())), "nt": (((1,), (1,)), ((), ())), "tn": (((0,), (0,)), ((), ()))}


def _pick(n, cap, step=256):
    if n <= cap:
        return n
    return next(t for t in range(cap - cap % step, 0, -step) if n % t == 0)


def _matmul(a, b, mode, name, out_dtypes=(F32,), epilogue=None, epi=(), jobs=(), tm=1024, tn=1024, tk=2048):
    if mode == "tn":
        (K, M), (K2, N) = a.shape, b.shape
    elif mode == "nt":
        (M, K), (N, K2) = a.shape, b.shape
    else:
        (M, K), (K2, N) = a.shape, b.shape
    assert K == K2
    tm, tn, tk = _pick(M, tm), _pick(N, tn), _pick(K, tk)
    nk = K // tk
    a_spec = pl.BlockSpec((tk, tm), lambda i, j, k: (k, i)) if mode == "tn" else pl.BlockSpec((tm, tk), lambda i, j, k: (i, k))
    b_spec = pl.BlockSpec((tn, tk), lambda i, j, k: (j, k)) if mode == "nt" else pl.BlockSpec((tk, tn), lambda i, j, k: (k, j))
    o_spec = pl.BlockSpec((tm, tn), lambda i, j, k: (i, j))
    n_epi, n_out = len(epi), len(out_dtypes)

    def body(a_ref, b_ref, *rest):
        epi_refs, out_refs, scratch = rest[:n_epi], rest[n_epi:n_epi + n_out], rest[n_epi + n_out:]
        part = lax.dot_general(a_ref[...].astype(BF16), b_ref[...].astype(BF16), _DIMS[mode], preferred_element_type=F32)

        def finish(acc):
            outs = epilogue(acc, *[r[...] for r in epi_refs]) if epilogue else (acc,)
            for o_ref, o in zip(out_refs, outs):
                o_ref[...] = o.astype(o_ref.dtype)

        if nk == 1:
            finish(part)
        else:
            acc_ref = scratch[0]
            k = pl.program_id(2)

            @pl.when(k == 0)
            def _():
                acc_ref[...] = part

            @pl.when(k > 0)
            def _():
                acc_ref[...] += part

            @pl.when(k == nk - 1)
            def _():
                finish(acc_ref[...])

    got = _call(
        body, jobs, name=name, grid=(M // tm, N // tn, nk),
        in_specs=[a_spec, b_spec] + [o_spec] * n_epi, out_specs=[o_spec] * n_out,
        out_shape=[jax.ShapeDtypeStruct((M, N), dt) for dt in out_dtypes],
        scratch_shapes=[pltpu.VMEM((tm, tn), F32)] if nk > 1 else [],
        sem=("parallel", "parallel", "arbitrary"))(a, b, *epi)
    outs, job_outs = got if jobs else (got, None)
    result = outs[0] if n_out == 1 else outs
    return (result, job_outs) if jobs else result


def _rms(x, g):
    r = lax.rsqrt(jnp.mean(x * x, axis=-1, keepdims=True) + EPS)
    return x * r * g


def _rms_bwd(dy, x, g):
    r = lax.rsqrt(jnp.mean(x * x, axis=-1, keepdims=True) + EPS)
    xn = x * r
    dyg = dy * g
    dx = r * (dyg - xn * jnp.mean(dyg * xn, axis=-1, keepdims=True))
    return dx, jnp.sum(dy * xn, axis=0, keepdims=True)


def _rows(d, tr=ROW_TILE):
    return pl.BlockSpec((tr, d), lambda i: (i, 0))


def _vec(d):
    return pl.BlockSpec((1, d), lambda i: (0, 0))


def _accumulate(ref, val):
    @pl.when(pl.program_id(0) == 0)
    def _():
        ref[...] = val

    @pl.when(pl.program_id(0) > 0)
    def _():
        ref[...] += val


def _norm_fwd(x, g, name, jobs=()):
    T, D = x.shape

    def body(x_ref, g_ref, h_ref):
        h_ref[...] = _rms(x_ref[...], g_ref[...]).astype(BF16)

    return _call(body, jobs, name=name,grid=(T // ROW_TILE,), in_specs=[_rows(D), _vec(D)], out_specs=_rows(D),
                          out_shape=jax.ShapeDtypeStruct((T, D), BF16), sem=("parallel",))(x, g)


def _residual_norm(x, m, g_post, g_next, name, jobs=()):
    T, D = x.shape

    def body(x_ref, m_ref, gp_ref, gn_ref, xo_ref, h_ref):
        xo = x_ref[...] + _rms(m_ref[...], gp_ref[...])
        xo_ref[...] = xo
        h_ref[...] = _rms(xo, gn_ref[...]).astype(BF16)

    return _call(body, jobs, name=name,grid=(T // ROW_TILE,), in_specs=[_rows(D), _rows(D), _vec(D), _vec(D)],
                          out_specs=[_rows(D), _rows(D)],
                          out_shape=[jax.ShapeDtypeStruct((T, D), F32), jax.ShapeDtypeStruct((T, D), BF16)],
                          sem=("parallel",))(x, m, g_post, g_next)


def _loss_and_last_norm_bwd(x, m, g_post, target, name, jobs=()):
    T, D = x.shape

    def body(x_ref, m_ref, gp_ref, t_ref, dx_ref, dm_ref, dg_ref, loss_ref):
        m_val, gp = m_ref[...], gp_ref[...]
        err = x_ref[...] + _rms(m_val, gp) - t_ref[...]
        dx = err * (1.0 / D)
        dx_ref[...] = dx
        dm, dg = _rms_bwd(dx, m_val, gp)
        dm_ref[...] = dm.astype(BF16)
        _accumulate(dg_ref, dg)
        _accumulate(loss_ref, jnp.full((1, 128), 0.5 * jnp.sum(err * err) * (1.0 / D), F32))

    return _call(body, jobs, name=name,grid=(T // ROW_TILE,), in_specs=[_rows(D), _rows(D), _vec(D), _rows(D)],
                          out_specs=[_rows(D), _rows(D), _vec(D), _vec(128)],
                          out_shape=[jax.ShapeDtypeStruct((T, D), F32), jax.ShapeDtypeStruct((T, D), BF16),
                                     jax.ShapeDtypeStruct((1, D), F32), jax.ShapeDtypeStruct((1, 128), F32)],
                          sem=("arbitrary",))(x, m, g_post, target)


def _norms_bwd(dx, dh, x_in, g_pre, m_prev, g_post_prev, name, jobs=()):
    T, D = dx.shape
    with_prev = m_prev is not None

    def body(*refs):
        if with_prev:
            dx_ref, dh_ref, x_ref, gq_ref, m_ref, gp_ref, dxo_ref, dgq_ref, dm_ref, dgp_ref = refs
        else:
            dx_ref, dh_ref, x_ref, gq_ref, dxo_ref, dgq_ref = refs
        d_in, dgq = _rms_bwd(dh_ref[...], x_ref[...], gq_ref[...])
        dxo = dx_ref[...] + d_in
        dxo_ref[...] = dxo
        _accumulate(dgq_ref, dgq)
        if with_prev:
            dm, dgp = _rms_bwd(dxo, m_ref[...], gp_ref[...])
            dm_ref[...] = dm.astype(BF16)
            _accumulate(dgp_ref, dgp)

    ins, in_specs = [dx, dh, x_in, g_pre], [_rows(D), _rows(D), _rows(D), _vec(D)]
    out_specs = [_rows(D), _vec(D)]
    out_shape = [jax.ShapeDtypeStruct((T, D), F32), jax.ShapeDtypeStruct((1, D), F32)]
    if with_prev:
        ins += [m_prev, g_post_prev]
        in_specs += [_rows(D), _vec(D)]
        out_specs += [_rows(D), _vec(D)]
        out_shape += [jax.ShapeDtypeStruct((T, D), BF16), jax.ShapeDtypeStruct((1, D), F32)]
    return _call(body, jobs, name=name,grid=(T // ROW_TILE,), in_specs=in_specs, out_specs=out_specs, out_shape=out_shape,
                          sem=("arbitrary",))(*ins)


def _window_weights(g):
    w = 2 << g
    return w, [jnp.where(j < w, 1.0, 0.0).astype(F32) for j in range(MAX_WINDOW)]


def _valid_count(r0, rows, w):
    t = (lax.broadcasted_iota(jnp.int32, (rows, 1), 0) + (r0 + 1)).astype(F32)
    return jnp.minimum(t, w.astype(F32))


def _pool_fwd(z, pool_w, pool_scale, name, jobs=()):
    T = z.shape[0]
    PG = pool_w.shape[-1]
    DP = N_GROUPS * PG
    rc = min(CHUNK, T)

    def body(u_ref, pw_ref, sc_ref, pooled_ref, y_ref, pad):
        w, wts = _window_weights(pl.program_id(0))
        pad[pl.ds(0, MAX_WINDOW), :] = jnp.zeros((MAX_WINDOW, PG), F32)
        pad[pl.ds(MAX_WINDOW, T), :] = u_ref[...]
        for r0 in range(0, T, rc):
            acc = jnp.zeros((rc, PG), F32)
            for j in range(MAX_WINDOW):
                acc = acc + wts[j] * pad[pl.ds(MAX_WINDOW + r0 - j, rc), :]
            pooled = acc / _valid_count(r0, rc, w) - u_ref[pl.ds(r0, rc), :]
            pooled_ref[pl.ds(r0, rc), :] = pooled.astype(BF16)
        mixed = jnp.dot(pooled_ref[...], pw_ref[...].astype(BF16), preferred_element_type=F32)
        y_ref[...] = (mixed * sc_ref[...]).astype(BF16)

    col = lambda g: (0, g)
    return _call(
        body, jobs, name=name,grid=(N_GROUPS,),
        in_specs=[pl.BlockSpec((T, PG), col), pl.BlockSpec((None, PG, PG), lambda g: (g, 0, 0)), pl.BlockSpec((1, PG), col)],
        out_specs=[pl.BlockSpec((T, PG), col), pl.BlockSpec((T, PG), col)],
        out_shape=[jax.ShapeDtypeStruct((T, DP), BF16), jax.ShapeDtypeStruct((T, DP), BF16)],
        scratch_shapes=[pltpu.VMEM((T + MAX_WINDOW, PG), F32)], sem=("parallel",))(z, pool_w, pool_scale)


def _pool_bwd(dy, pooled, pool_w, pool_scale, name, jobs=()):
    T = dy.shape[0]
    PG = pool_w.shape[-1]
    DP = N_GROUPS * PG
    rc = min(CHUNK, T)

    def body(dy_ref, pooled_ref, pw_ref, sc_ref, du_ref, dpw_ref, dsc_ref, pad, dp_ref):
        w, wts = _window_weights(pl.program_id(0))
        pooled_v, pw = pooled_ref[...], pw_ref[...].astype(BF16)
        dy_v = dy_ref[...]
        mixed = jnp.dot(pooled_v, pw, preferred_element_type=F32)
        dsc_ref[...] = jnp.sum(dy_v * mixed, axis=0, keepdims=True)
        dmixed = (dy_v * sc_ref[...]).astype(BF16)
        dpw_ref[...] = lax.dot_general(pooled_v, dmixed, _DIMS["tn"], preferred_element_type=F32)
        dp_ref[...] = lax.dot_general(dmixed, pw, _DIMS["nt"], preferred_element_type=F32)
        pad[pl.ds(T, MAX_WINDOW), :] = jnp.zeros((MAX_WINDOW, PG), F32)
        for r0 in range(0, T, rc):
            pad[pl.ds(r0, rc), :] = dp_ref[pl.ds(r0, rc), :] / _valid_count(r0, rc, w)
        for r0 in range(0, T, rc):
            acc = jnp.zeros((rc, PG), F32)
            for j in range(MAX_WINDOW):
                acc = acc + wts[j] * pad[pl.ds(r0 + j, rc), :]
            du_ref[pl.ds(r0, rc), :] = (acc - dp_ref[pl.ds(r0, rc), :]).astype(BF16)

    col = lambda g: (0, g)
    return _call(
        body, jobs, name=name,grid=(N_GROUPS,),
        in_specs=[pl.BlockSpec((T, PG), col), pl.BlockSpec((T, PG), col), pl.BlockSpec((None, PG, PG), lambda g: (g, 0, 0)),
                  pl.BlockSpec((1, PG), col)],
        out_specs=[pl.BlockSpec((T, PG), col), pl.BlockSpec((None, PG, PG), lambda g: (g, 0, 0)), pl.BlockSpec((1, PG), col)],
        out_shape=[jax.ShapeDtypeStruct((T, DP), BF16), jax.ShapeDtypeStruct((N_GROUPS, PG, PG), F32),
                   jax.ShapeDtypeStruct((1, DP), F32)],
        scratch_shapes=[pltpu.VMEM((T + MAX_WINDOW, PG), F32), pltpu.VMEM((T, PG), F32)],
        sem=("parallel",))(dy, pooled, pool_w, pool_scale)


def _conv_fwd(z, conv_w, conv_b, d_pool, name, tc=128, jobs=()):
    T = z.shape[0]
    DC = conv_w.shape[-1]
    rc = min(CHUNK, T)
    v0, g0 = d_pool // tc, (d_pool + DC) // tc

    def body(v_ref, gt_ref, w_ref, b_ref, a_ref, c_ref, pad):
        pad[pl.ds(0, CONV_PAD), :] = jnp.zeros((CONV_PAD, tc), F32)
        for r0 in range(0, T, rc):
            a = v_ref[pl.ds(r0, rc), :] * _sigmoid(gt_ref[pl.ds(r0, rc), :])
            a_ref[pl.ds(r0, rc), :] = a
            pad[pl.ds(CONV_PAD + r0, rc), :] = a
        for r0 in range(0, T, rc):
            acc = jnp.zeros((rc, tc), F32) + b_ref[...]
            for k in range(CONV_K):
                acc = acc + w_ref[pl.ds(k, 1), :] * pad[pl.ds(CONV_PAD - (CONV_K - 1) + k + r0, rc), :]
            c_ref[pl.ds(r0, rc), :] = acc

    col = lambda j: (0, j)
    return _call(
        body, jobs, name=name,grid=(DC // tc,),
        in_specs=[pl.BlockSpec((T, tc), lambda j: (0, v0 + j)), pl.BlockSpec((T, tc), lambda j: (0, g0 + j)),
                  pl.BlockSpec((CONV_K, tc), col), pl.BlockSpec((1, tc), col)],
        out_specs=[pl.BlockSpec((T, tc), col), pl.BlockSpec((T, tc), col)],
        out_shape=[jax.ShapeDtypeStruct((T, DC), F32), jax.ShapeDtypeStruct((T, DC), F32)],
        scratch_shapes=[pltpu.VMEM((T + CONV_PAD, tc), F32)], sem=("parallel",))(z, z, conv_w, conv_b)


def _conv_bwd(dc, a, z, conv_w, d_pool, name, tc=128, jobs=()):
    T, DC = dc.shape
    rc = min(CHUNK, T)
    v0, g0 = d_pool // tc, (d_pool + DC) // tc

    def body(dc_ref, a_ref, v_ref, gt_ref, w_ref, dv_ref, dg_ref, dw_ref, db_ref, apad, dpad):
        apad[pl.ds(0, CONV_PAD), :] = jnp.zeros((CONV_PAD, tc), F32)
        apad[pl.ds(CONV_PAD, T), :] = a_ref[...]
        dpad[pl.ds(0, T), :] = dc_ref[...]
        dpad[pl.ds(T, CONV_PAD), :] = jnp.zeros((CONV_PAD, tc), F32)
        db_ref[...] = jnp.sum(dc_ref[...], axis=0, keepdims=True)
        for k in range(CONV_K):
            acc = jnp.zeros((8, tc), F32)
            for r0 in range(0, T, rc):
                prod = dc_ref[pl.ds(r0, rc), :] * apad[pl.ds(CONV_PAD - (CONV_K - 1) + k + r0, rc), :]
                acc = acc + jnp.sum(prod.reshape(rc // 8, 8, tc), axis=0)
            dw_ref[pl.ds(k, 1), :] = jnp.sum(acc, axis=0, keepdims=True)
        for r0 in range(0, T, rc):
            da = jnp.zeros((rc, tc), F32)
            for k in range(CONV_K):
                da = da + w_ref[pl.ds(k, 1), :] * dpad[pl.ds(r0 + (CONV_K - 1) - k, rc), :]
            sig = _sigmoid(gt_ref[pl.ds(r0, rc), :])
            dv_ref[pl.ds(r0, rc), :] = (da * sig).astype(BF16)
            dg_ref[pl.ds(r0, rc), :] = (da * v_ref[pl.ds(r0, rc), :] * sig * (1.0 - sig)).astype(BF16)

    col = lambda j: (0, j)
    return _call(
        body, jobs, name=name,grid=(DC // tc,),
        in_specs=[pl.BlockSpec((T, tc), col), pl.BlockSpec((T, tc), col), pl.BlockSpec((T, tc), lambda j: (0, v0 + j)),
                  pl.BlockSpec((T, tc), lambda j: (0, g0 + j)), pl.BlockSpec((CONV_K, tc), col)],
        out_specs=[pl.BlockSpec((T, tc), col), pl.BlockSpec((T, tc), col), pl.BlockSpec((CONV_K, tc), col),
                   pl.BlockSpec((1, tc), col)],
        out_shape=[jax.ShapeDtypeStruct((T, DC), BF16), jax.ShapeDtypeStruct((T, DC), BF16),
                   jax.ShapeDtypeStruct((CONV_K, DC), F32), jax.ShapeDtypeStruct((1, DC), F32)],
        scratch_shapes=[pltpu.VMEM((T + CONV_PAD, tc), F32), pltpu.VMEM((T + CONV_PAD, tc), F32)],
        sem=("parallel",))(dc, a, z, z, conv_w)


def _layer_norm_parts(c, g, b):
    mu = jnp.mean(c, axis=-1, keepdims=True)
    xc = c - mu
    rstd = lax.rsqrt(jnp.mean(xc * xc, axis=-1, keepdims=True) + EPS)
    xhat = xc * rstd
    return xhat, rstd, xhat * g + b


def _ln_silu_fwd(c, g, b, name, jobs=()):
    T, DC = c.shape

    def body(c_ref, g_ref, b_ref, y_ref):
        _, _, ln = _layer_norm_parts(c_ref[...], g_ref[...], b_ref[...])
        y_ref[...] = (ln * _sigmoid(ln)).astype(BF16)

    return _call(body, jobs, name=name,grid=(T // ROW_TILE,), in_specs=[_rows(DC), _vec(DC), _vec(DC)], out_specs=_rows(DC),
                          out_shape=jax.ShapeDtypeStruct((T, DC), BF16), sem=("parallel",))(c, g, b)


def _ln_silu_bwd(dy, c, g, b, name, jobs=()):
    T, DC = c.shape

    def body(dy_ref, c_ref, g_ref, b_ref, dc_ref, dg_ref, db_ref):
        gain = g_ref[...]
        xhat, rstd, ln = _layer_norm_parts(c_ref[...], gain, b_ref[...])
        s = _sigmoid(ln)
        dln = dy_ref[...] * (s * (1.0 + ln * (1.0 - s)))
        _accumulate(dg_ref, jnp.sum(dln * xhat, axis=0, keepdims=True))
        _accumulate(db_ref, jnp.sum(dln, axis=0, keepdims=True))
        dxh = dln * gain
        dc_ref[...] = rstd * (dxh - jnp.mean(dxh, axis=-1, keepdims=True) - xhat * jnp.mean(dxh * xhat, axis=-1, keepdims=True))

    return _call(body, jobs, name=name,grid=(T // ROW_TILE,),
                          in_specs=[pl.BlockSpec((ROW_TILE, DC), lambda i: (i, 1)), _rows(DC), _vec(DC), _vec(DC)],
                          out_specs=[_rows(DC), _vec(DC), _vec(DC)],
                          out_shape=[jax.ShapeDtypeStruct((T, DC), F32), jax.ShapeDtypeStruct((1, DC), F32),
                                     jax.ShapeDtypeStruct((1, DC), F32)],
                          sem=("arbitrary",))(dy, c, g, b)


def _short_specs(T, DS, tc):
    n = DS // tc
    return [pl.BlockSpec((T, tc), lambda j: (0, j)), pl.BlockSpec((T, tc), lambda j: (0, n + j)),
            pl.BlockSpec((T, tc), lambda j: (0, 2 * n + j))]


def _short_fwd(z, w, name, tc=256, jobs=()):
    T = z.shape[0]
    DS = w.shape[-1]
    rc = min(CHUNK, T)

    def body(b_ref, cg_ref, u_ref, w_ref, y_ref, pad):
        pad[pl.ds(0, SHORT_PAD), :] = jnp.zeros((SHORT_PAD, tc), F32)
        pad[pl.ds(SHORT_PAD, T), :] = cg_ref[...] * u_ref[...]
        for r0 in range(0, T, rc):
            r = jnp.zeros((rc, tc), F32)
            for k in range(SHORT_K):
                r = r + w_ref[pl.ds(k, 1), :] * pad[pl.ds(SHORT_PAD - (SHORT_K - 1) + k + r0, rc), :]
            y_ref[pl.ds(r0, rc), :] = (b_ref[pl.ds(r0, rc), :] * r).astype(BF16)

    col = lambda j: (0, j)
    return _call(body, jobs, name=name,grid=(DS // tc,), in_specs=_short_specs(T, DS, tc) + [pl.BlockSpec((SHORT_K, tc), col)],
                          out_specs=pl.BlockSpec((T, tc), col), out_shape=jax.ShapeDtypeStruct((T, DS), BF16),
                          scratch_shapes=[pltpu.VMEM((T + SHORT_PAD, tc), F32)], sem=("parallel",))(z, z, z, w)


def _short_bwd(dy, z, w, name, tc=256, jobs=()):
    T, DS = dy.shape
    rc = min(CHUNK, T)

    def body(dy_ref, b_ref, cg_ref, u_ref, w_ref, db_ref, dcg_ref, du_ref, dw_ref, qpad, rpad):
        qpad[pl.ds(0, SHORT_PAD), :] = jnp.zeros((SHORT_PAD, tc), F32)
        qpad[pl.ds(SHORT_PAD, T), :] = cg_ref[...] * u_ref[...]
        rpad[pl.ds(0, T), :] = dy_ref[...] * b_ref[...]
        rpad[pl.ds(T, SHORT_PAD), :] = jnp.zeros((SHORT_PAD, tc), F32)
        accs = [jnp.zeros((8, tc), F32) for _ in range(SHORT_K)]
        for r0 in range(0, T, rc):
            r = jnp.zeros((rc, tc), F32)
            dq = jnp.zeros((rc, tc), F32)
            dr = rpad[pl.ds(r0, rc), :]
            for k in range(SHORT_K):
                q_k = qpad[pl.ds(SHORT_PAD - (SHORT_K - 1) + k + r0, rc), :]
                r = r + w_ref[pl.ds(k, 1), :] * q_k
                dq = dq + w_ref[pl.ds(k, 1), :] * rpad[pl.ds(r0 + (SHORT_K - 1) - k, rc), :]
                accs[k] = accs[k] + jnp.sum((dr * q_k).reshape(rc // 8, 8, tc), axis=0)
            db_ref[pl.ds(r0, rc), :] = (dy_ref[pl.ds(r0, rc), :] * r).astype(BF16)
            dcg_ref[pl.ds(r0, rc), :] = (dq * u_ref[pl.ds(r0, rc), :]).astype(BF16)
            du_ref[pl.ds(r0, rc), :] = (dq * cg_ref[pl.ds(r0, rc), :]).astype(BF16)
        for k in range(SHORT_K):
            dw_ref[pl.ds(k, 1), :] = jnp.sum(accs[k], axis=0, keepdims=True)

    col = lambda j: (0, j)
    tile = pl.BlockSpec((T, tc), col)
    return _call(body, jobs, name=name,grid=(DS // tc,),
                          in_specs=[tile] + _short_specs(T, DS, tc) + [pl.BlockSpec((SHORT_K, tc), col)],
                          out_specs=[tile, tile, tile, pl.BlockSpec((SHORT_K, tc), col)],
                          out_shape=[jax.ShapeDtypeStruct((T, DS), BF16)] * 3 + [jax.ShapeDtypeStruct((SHORT_K, DS), F32)],
                          scratch_shapes=[pltpu.VMEM((T + SHORT_PAD, tc), F32), pltpu.VMEM((T + SHORT_PAD, tc), F32)],
                          sem=("parallel",))(dy, z, z, z, w)


def _tile_rows(rows, cols, n_bufs):
    budget = VMEM_LIMIT_BYTES // 2 // (2 * n_bufs * 4 * cols)
    tr = rows
    while tr > budget and tr % 16 == 0:
        tr //= 2
    return tr


def _placed_call(body, name, place, grid, in_specs, out_specs, out_shape, ins, jobs=()):
    return _call(body, jobs, prefetch=place, name=name, grid=grid, in_specs=in_specs, out_specs=out_specs, out_shape=out_shape,
                 sem=("parallel",))(*ins)


def _cast_into_full(w, layer, kind, place, name, jobs=()):
    _, R, C = w.shape
    tr = _tile_rows(R, C, 2)
    nb = R // tr
    if kind == "col":
        full, out_spec = (R, C * N_CHIPS), pl.BlockSpec((tr, C), lambda i, s: (i, s[0]))
    else:
        full, out_spec = (R * N_CHIPS, C), pl.BlockSpec((tr, C), lambda i, s: (s[0] * nb + i, 0))

    def body(s_ref, w_ref, o_ref):
        o_ref[...] = w_ref[...].astype(BF16)

    return _placed_call(body, name, place, (nb,), [pl.BlockSpec((None, tr, C), lambda i, s: (layer, i, 0))], out_spec,
                        jax.ShapeDtypeStruct(full, BF16), [w], jobs)


def _add_pair(grad, theirs, kind, place, name, jobs=()):
    R, C = grad.shape
    piece_rows = R // 2 if kind == "col" else R // N_CHIPS // 2
    tr = _tile_rows(piece_rows, C, 3)
    nb = piece_rows // tr
    if kind == "col":
        g_spec = pl.BlockSpec((tr, C), lambda i, s: (s[1] * nb + i, 0))
    else:
        g_spec = pl.BlockSpec((tr, C), lambda i, s: ((2 * (i // nb) + s[1]) * nb + i % nb, 0))
    flat = pl.BlockSpec((tr, C), lambda i, s: (i, 0))

    def body(s_ref, a_ref, b_ref, o_ref):
        o_ref[...] = (a_ref[...].astype(F32) + b_ref[...].astype(F32)).astype(BF16)

    return _placed_call(body, name, place, (R // 2 // tr,), [g_spec, flat], flat, jax.ShapeDtypeStruct((R // 2, C), BF16),
                        [grad, theirs], jobs)


def _sum_chips(chip_sum, arrived, kind, place, name, jobs=()):
    _, H, W = arrived.shape
    tr = _tile_rows(H, W, 6)
    nb = H // tr
    if kind == "col":
        own_spec = pl.BlockSpec((tr, W), lambda i, s: (i, s[0]))
    else:
        own_spec = pl.BlockSpec((tr, W), lambda i, s: (s[0] * nb + i, 0))

    def body(s_ref, p_ref, r_ref, o_ref):
        acc = p_ref[...].astype(F32)
        for i in range(N_CHIPS - 1):
            acc = acc + r_ref[i].astype(F32)
        o_ref[...] = acc

    return _placed_call(body, name, place, (nb,), [own_spec, pl.BlockSpec((N_CHIPS - 1, tr, W), lambda i, s: (0, i, 0))],
                        pl.BlockSpec((tr, W), lambda i, s: (s[1] * nb + i, 0)), jax.ShapeDtypeStruct((2 * H, W), F32),
                        [chip_sum, arrived], jobs)


def _adamw_values(w, g, m, v):
    m = ADAM_B1 * m + (1.0 - ADAM_B1) * g
    v = ADAM_B2 * v + (1.0 - ADAM_B2) * (g * g)
    m_hat = m / (1.0 - ADAM_B1 ** ADAM_STEP)
    v_hat = v / (1.0 - ADAM_B2 ** ADAM_STEP)
    return -ADAM_LR * (m_hat / (jnp.sqrt(v_hat) + ADAM_EPS) + ADAM_WD * w), m, v


def _adamw(w, g, m, v, name, layer=0, carried=None):
    L, R, C = w.shape
    tr = _tile_rows(R, C, 8)

    def body(w_ref, g_ref, m_ref, v_ref, *rest):
        go_ref, d_ref, mo_ref, vo_ref = rest[-4:]
        g_val = g_ref[...]
        d, m_new, v_new = _adamw_values(w_ref[...], g_val, m_ref[...], v_ref[...])
        go_ref[...], d_ref[...], mo_ref[...], vo_ref[...] = g_val, d, m_new, v_new

    lay = pl.BlockSpec((None, tr, C), lambda i: (layer, i, 0))
    ins = [w, g, m, v]
    in_specs = [lay, pl.BlockSpec((tr, C), lambda i: (i, 0)), lay, lay]
    aliases = {}
    if carried is not None:
        ins += list(carried)
        in_specs += [pl.BlockSpec(memory_space=pl.ANY)] * 4
        aliases = {4 + i: i for i in range(4)}
    return _call(body, (), name=name, grid=(R // tr,), in_specs=in_specs, out_specs=[lay] * 4,
                 out_shape=[jax.ShapeDtypeStruct((L, R, C), F32)] * 4, input_output_aliases=aliases, sem=("parallel",))(*ins)


def _aligned(v, m):
    return v if isinstance(v, int) else pl.multiple_of(v, m)


def _place():
    x, y, c = lax.axis_index("x"), lax.axis_index("y"), lax.axis_index("c")
    other_chips = [(x, 1 - y), (1 - x, y), (1 - x, 1 - y)]
    return x, y, c, 2 * x + y, other_chips


def _chip_index(chip):
    return 2 * chip[0] + chip[1]


def _piece(ref, kind, k, h):
    R, C = ref.shape
    if kind == "col":
        return ref.at[pl.ds(_aligned(h * (R // 2), 16), R // 2), pl.ds(_aligned(k * (C // N_CHIPS), 128), C // N_CHIPS)]
    rs = R // N_CHIPS
    return ref.at[pl.ds(_aligned(k * rs + h * (rs // 2), 16), rs // 2), :]


def _compact_piece(ref, kind, k):
    R2, C = ref.shape
    if kind == "col":
        return ref.at[:, pl.ds(_aligned(k * (C // N_CHIPS), 128), C // N_CHIPS)]
    return ref.at[pl.ds(_aligned(k * (R2 // N_CHIPS), 16), R2 // N_CHIPS), :]


def _half_rows(ref, h):
    R = ref.shape[0]
    return ref.at[pl.ds(_aligned(h * (R // 2), 16), R // 2), :]


class _Copies:
    def __init__(self, send_sems, recv_sems):
        self.send_sems, self.recv_sems = send_sems, recv_sems
        self.n_remote = 0

    def remote(self, src, dst, device):
        k = self.n_remote
        self.n_remote += 1
        return pltpu.make_async_remote_copy(src_ref=src, dst_ref=dst, send_sem=self.send_sems.at[k], recv_sem=self.recv_sems.at[k],
                                            device_id=device, device_id_type=MESH)


class _Job:
    def __init__(self, ins, out_shape, aliases, n_remote, build):
        self.ins, self.out_shape, self.aliases, self.n_remote, self.build = list(ins), list(out_shape), dict(aliases), n_remote, build


def _merge_jobs(jobs):
    ins, out_shape, aliases, spans = [], [], {}, []
    for job in jobs:
        spans.append((len(ins), len(job.ins), len(out_shape), len(job.out_shape)))
        aliases.update({len(ins) + i: len(out_shape) + o for i, o in job.aliases.items()})
        ins += job.ins
        out_shape += job.out_shape

    def build(in_refs, out_refs, cp):
        copies = []
        for job, (i0, ni, o0, no) in zip(jobs, spans):
            copies += job.build(in_refs[i0:i0 + ni], out_refs[o0:o0 + no], cp)
        return copies

    return _Job(ins, out_shape, aliases, sum(j.n_remote for j in jobs), build)


def _run_jobs(jobs, name):
    job = _merge_jobs(jobs)
    n_in, n_out = len(job.ins), len(job.out_shape)

    def body(*refs):
        copies = job.build(refs[:n_in], refs[n_in:n_in + n_out], _Copies(*refs[n_in + n_out:]))
        for d in copies:
            d.start()
        for d in copies:
            d.wait_recv()
        for d in copies:
            d.wait_send()

    outs = pl.pallas_call(
        body, name=name, in_specs=[HBM] * n_in, out_specs=[HBM] * n_out, out_shape=job.out_shape,
        input_output_aliases=job.aliases,
        scratch_shapes=[pltpu.SemaphoreType.DMA((job.n_remote,)), pltpu.SemaphoreType.DMA((job.n_remote,))])(*job.ins)
    split, o0 = [], 0
    for j in jobs:
        split.append(outs[o0:o0 + len(j.out_shape)])
        o0 += len(j.out_shape)
    return split


class _Flying:
    def __init__(self, job, send_sems, recv_sems, bufs, token):
        self.job, self.send_sems, self.recv_sems, self.bufs, self.token = job, send_sems, recv_sems, bufs, token


def _job_refs(job, buf_refs):
    n_out = len(job.out_shape)
    kept = [i for i in range(len(job.ins)) if i not in job.aliases]
    ins = [buf_refs[job.aliases[i]] if i in job.aliases else buf_refs[n_out + kept.index(i)] for i in range(len(job.ins))]
    return ins, list(buf_refs[:n_out])


def _start_job(job, name, after=()):
    n_in, n_out, n_after = len(job.ins), len(job.out_shape), len(after)
    kept = [i for i in range(n_in) if i not in job.aliases]
    n_bufs = n_out + len(kept)

    def body(*refs):
        in_refs, out_refs = refs[:n_in], refs[n_in + n_after:n_in + n_after + n_out]
        send_sems, recv_sems, token = refs[n_in + n_after + n_bufs:]
        for d in job.build(in_refs, out_refs, _Copies(send_sems, recv_sems)):
            d.start()
        token[...] = jnp.zeros_like(token)

    aliases = dict(job.aliases)
    aliases.update({i: n_out + k for k, i in enumerate(kept)})
    sems = pltpu.SemaphoreType.DMA((job.n_remote,))
    outs = pl.pallas_call(
        body, name=name, in_specs=[HBM] * n_in + [ANY] * n_after,
        out_specs=[HBM] * n_bufs + [SEM, SEM, pl.BlockSpec(memory_space=pltpu.VMEM)],
        out_shape=job.out_shape + [jax.ShapeDtypeStruct(job.ins[i].shape, job.ins[i].dtype) for i in kept]
        + [sems, sems, jax.ShapeDtypeStruct((8, 128), F32)],
        input_output_aliases=aliases,
        compiler_params=pltpu.CompilerParams(has_side_effects=pltpu.SideEffectType.DATAFLOW_SIDE_EFFECTING))(*job.ins, *after)
    return _Flying(job, outs[n_bufs], outs[n_bufs + 1], list(outs[:n_bufs]), outs[n_bufs + 2])


def _wait_job(flying, name, after=()):
    job, n_bufs, n_after = flying.job, len(flying.bufs), len(after)

    def body(*refs):
        in_refs, out_refs = _job_refs(job, refs[:n_bufs])
        send_sems, recv_sems = refs[n_bufs:n_bufs + 2]
        copies = job.build(in_refs, out_refs, _Copies(send_sems, recv_sems))
        for d in copies:
            d.wait_send()
        for d in copies:
            d.wait_recv()

    outs = pl.pallas_call(
        body, name=name, in_specs=[HBM] * n_bufs + [SEM, SEM] + [ANY] * n_after, out_specs=[HBM] * n_bufs,
        out_shape=[jax.ShapeDtypeStruct(b.shape, b.dtype) for b in flying.bufs],
        input_output_aliases={i: i for i in range(n_bufs)},
        compiler_params=pltpu.CompilerParams(has_side_effects=pltpu.SideEffectType.DATAFLOW_SIDE_EFFECTING))(
            *flying.bufs, flying.send_sems, flying.recv_sems, *after)
    return list(outs[:len(job.out_shape)]), list(outs[len(job.out_shape):])


class _Beside:
    def __init__(self, jobs):
        self.jobs = list(jobs)
        job = self.job = _merge_jobs(self.jobs) if self.jobs else None
        self.ins = job.ins if job else []
        self.out_shape = job.out_shape if job else []
        self.n_in, self.n_out = len(self.ins), len(self.out_shape)

    def scratch(self):
        n = self.job.n_remote if self.job else 0
        return [pltpu.SemaphoreType.DMA((n,)), pltpu.SemaphoreType.DMA((n,))] if n else []

    def aliases(self, first_in, first_out):
        return {first_in + i: first_out + o for i, o in self.job.aliases.items()} if self.job else {}

    def _at(self, grid, which):
        cond = None
        for axis, n in enumerate(grid):
            here = pl.program_id(axis) == (0 if which == "first" else n - 1)
            cond = here if cond is None else jnp.logical_and(cond, here)
        return cond

    def start(self, in_refs, out_refs, scratch, grid):
        if self.job:
            @pl.when(self._at(grid, "first"))
            def _():
                for d in self.job.build(in_refs, out_refs, _Copies(*scratch[-2:])):
                    d.start()

    def finish(self, in_refs, out_refs, scratch, grid):
        if self.job:
            @pl.when(self._at(grid, "last"))
            def _():
                copies = self.job.build(in_refs, out_refs, _Copies(*scratch[-2:]))
                for d in copies:
                    d.wait_recv()
                for d in copies:
                    d.wait_send()

    def split(self, outs):
        per_job, o0 = [], 0
        for j in self.jobs:
            per_job.append(list(outs[o0:o0 + len(j.out_shape)]))
            o0 += len(j.out_shape)
        return per_job


def _in_place(arrays):
    return [jax.ShapeDtypeStruct(a.shape, a.dtype) for a in arrays], {u: u for u in range(len(arrays))}


def _rows_part(ref, part, n_parts):
    h = ref.shape[0] // n_parts
    return ref.at[pl.ds(part * h, h), :]


def _gather_job(full, kind, stage):
    def build(in_refs, out_refs, cp):
        x, y, c, me, (y_nbr, x_nbr, diagonal) = _place()
        (ref,) = out_refs
        sibling = (x, y, 1 - c)
        if stage == 1:
            mine = _piece(ref, kind, me, c)
            return [cp.remote(mine, mine, (*y_nbr, c)), cp.remote(mine, mine, (*x_nbr, c))]
        from_y, from_x = _piece(ref, kind, _chip_index(y_nbr), c), _piece(ref, kind, _chip_index(x_nbr), c)
        if stage == 2:
            relay_0, relay_1 = _rows_part(from_x, 0, 2), _rows_part(from_y, 1, 2)
            return [cp.remote(relay_0, relay_0, (*y_nbr, c)), cp.remote(relay_1, relay_1, (*x_nbr, c)),
                    cp.remote(from_y, from_y, sibling), cp.remote(from_x, from_x, sibling)]
        from_diagonal = _piece(ref, kind, _chip_index(diagonal), c)
        return [cp.remote(from_diagonal, from_diagonal, sibling)]

    return _Job([full], *_in_place([full]), {1: 2, 2: 4, 3: 1}[stage], build)


def _gather_small_job(fulls, axes):
    def build(in_refs, out_refs, cp):
        x, y, c, me, chips = _place()
        copies = []
        for ref, ax in zip(out_refs, axes):
            n = ref.shape[ax] // N_CHIPS
            idx = [slice(None)] * len(ref.shape)
            idx[ax] = pl.ds(_aligned(me * n, n), n)
            mine = ref.at[tuple(idx)]
            copies += [cp.remote(mine, mine, (*chip, c)) for chip in chips]
        return copies

    return _Job(fulls, *_in_place(fulls), 3 * len(fulls), build)


def _exchange_halves_job(grads, kinds):
    def build(in_refs, out_refs, cp):
        x, y, c, me, chips = _place()
        copies = []
        for src, dst, kind in zip(in_refs, out_refs, kinds):
            if kind == "col":
                copies.append(cp.remote(_half_rows(src, 1 - c), dst, (x, y, 1 - c)))
            else:
                copies += [cp.remote(_piece(src, "row", k, 1 - c), _compact_piece(dst, "row", k), (x, y, 1 - c))
                           for k in range(N_CHIPS)]
        return copies

    out_shape = [jax.ShapeDtypeStruct((g.shape[0] // 2, g.shape[1]), g.dtype) for g in grads]
    return _Job(grads, out_shape, {}, sum(1 if k == "col" else N_CHIPS for k in kinds), build)


def _scatter_job(half, kind, n_parts, parts, into=None):
    def build(in_refs, out_refs, cp):
        x, y, c, me, chips = _place()
        src, (dst,) = in_refs[0], out_refs
        copies = []
        for p in parts:
            copies += [cp.remote(_rows_part(_compact_piece(src, kind, _chip_index(chip)), p, n_parts),
                                 _rows_part(dst.at[r], p, n_parts), (*chip, c)) for r, chip in enumerate(chips)]
        return copies

    part_shape = (half.shape[0], half.shape[1] // N_CHIPS) if kind == "col" else (half.shape[0] // N_CHIPS, half.shape[1])
    out_shape = [jax.ShapeDtypeStruct((N_CHIPS - 1,) + part_shape, half.dtype)]
    ins, aliases = ([half], {}) if into is None else ([half, into], {1: 0})
    return _Job(ins, out_shape, aliases, 3 * len(parts), build)


def _share_job(shards):
    def build(in_refs, out_refs, cp):
        x, y, c, me, chips = _place()
        copies = []
        for ref in out_refs:
            mine = _half_rows(ref, c)
            copies.append(cp.remote(mine, mine, (x, y, 1 - c)))
        return copies

    return _Job(shards, *_in_place(shards), len(shards), build)


N_DEVICES = 2 * N_CHIPS


def _small_exchange_job(slots, n_parts, parts):
    def build(in_refs, out_refs, cp):
        x, y, c, me, chips = _place()
        (ref,) = out_refs
        copies = []
        for part in parts:
            mine = _rows_part(ref.at[2 * me + c], part, n_parts)
            copies += [cp.remote(mine, mine, (x ^ (p >> 2), y ^ ((p >> 1) & 1), c ^ (p & 1))) for p in range(1, N_DEVICES)]
        return copies

    return _Job([slots], *_in_place([slots]), (N_DEVICES - 1) * len(parts), build)


def _sum_slots(slots, name, jobs=()):
    n, R, C = slots.shape

    def body(s_ref, o_ref):
        acc = s_ref[0]
        for i in range(1, n):
            acc = acc + s_ref[i]
        o_ref[...] = acc

    return _call(body, jobs, name=name, grid=(1,), in_specs=[pl.BlockSpec((n, R, C), lambda i: (0, 0, 0))],
                 out_specs=pl.BlockSpec((R, C), lambda i: (0, 0)), out_shape=jax.ShapeDtypeStruct((R, C), F32),
                 sem=("arbitrary",))(slots)


def _pack(arrays, width):
    rows = []
    for a in arrays:
        flat = a.reshape(-1)
        n_rows = -(-flat.shape[0] // width)
        rows.append(jnp.pad(flat, (0, n_rows * width - flat.shape[0])).reshape(n_rows, width))
    n = sum(r.shape[0] for r in rows)
    rows.append(jnp.zeros((-n % 8, width), F32))
    return jnp.concatenate(rows, axis=0)


def _unpack(packed, shapes):
    out, r0, width = [], 0, packed.shape[1]
    for shape in shapes:
        size = 1
        for d in shape:
            size *= d
        n_rows = -(-size // width)
        out.append(packed[r0:r0 + n_rows].reshape(-1)[:size].reshape(shape))
        r0 += n_rows
    return out


class _Backlog:
    def __init__(self):
        self.now, self.free, self.flights, self.last = 0.0, {"ici": 0.0, "d2d": 0.0}, [], None

    def run(self, fn, us, *args, **kw):
        out = fn(*args, **kw)
        self.now += us
        self.last = out[0] if isinstance(out, (list, tuple)) else out
        self.poll()
        return out

    def start(self, job, name, link, cost, done):
        flying = _start_job(job, name + "_start")
        _Behind.pending.append(flying.token)
        ends = max(self.now, self.free[link]) + cost
        self.free[link] = ends
        self.flights.append((ends + LANDING_SLACK_US, name, flying, done))
        self.flights.sort(key=lambda f: f[0])

    def poll(self, block=False):
        while self.flights and (block or self.flights[0][0] <= self.now):
            ends, name, flying, done = self.flights.pop(0)
            self.now, block = max(self.now, ends), False
            done(*_wait_job(flying, name + "_wait", [self.last]))


class _GradStream:
    def __init__(self, backlog, u, name, kind, cost, g, place, results):
        self.backlog, self.u, self.name, self.kind, self.cost, self.place, self.results = backlog, u, name, kind, cost, place, results
        backlog.start(_exchange_halves_job([g], [kind]), "to_sibling_" + name, "d2d", D2D_SHARE * cost, self.exchanged)

    def exchanged(self, outs, kept):
        chip_sum = self.backlog.run(_add_pair, SIDE_KERNEL_US, kept[0], outs[0], self.kind, self.place, "chip_sum_" + self.name)
        self.backlog.start(_scatter_job(chip_sum, self.kind, 1, [0]), "to_owners_" + self.name, "ici", self.cost, self.scattered)

    def scattered(self, outs, kept):
        reduced = self.backlog.run(_sum_chips, SIDE_KERNEL_US, kept[0], outs[0], self.kind, self.place, "reduce_" + self.name)
        self.backlog.start(_share_job([reduced]), "share_" + self.name, "d2d", D2D_SHARE * self.cost, self.shared)

    def shared(self, outs, kept):
        self.results[self.u] = outs[0]


SIDE_KERNEL_US = 12.0
D2D_SHARE = 0.15
LANDING_SLACK_US = 5.0


def kernel(x, mix_pre_g, mix_post_g, ffn_pre_g, ffn_post_g, ab_w_in, pool_w, pool_scale, conv_w, conv_b, conv_ln_g, conv_ln_b, ab_w_out, sc_w_in, sc_conv_w, sc_w_out, ffn_w1, ffn_w2, loss_target, m_mix_pre_g, m_mix_post_g, m_ffn_pre_g, m_ffn_post_g, m_ab_w_in, m_pool_w, m_pool_scale, m_conv_w, m_conv_b, m_conv_ln_g, m_conv_ln_b, m_ab_w_out, m_sc_w_in, m_sc_conv_w, m_sc_w_out, m_ffn_w1, m_ffn_w2, v_mix_pre_g, v_mix_post_g, v_ffn_pre_g, v_ffn_post_g, v_ab_w_in, v_pool_w, v_pool_scale, v_conv_w, v_conv_b, v_conv_ln_g, v_conv_ln_b, v_ab_w_out, v_sc_w_in, v_sc_conv_w, v_sc_w_out, v_ffn_w1, v_ffn_w2):
    x0, target = x[0], loss_target[0]
    T, D = x0.shape
    DP = pool_scale.shape[-1]
    gain = lambda g, layer: g[layer][None, :]

    big = [("ab_w_in", ab_w_in, 0, "col", 4, 67.0), ("ab_w_out", ab_w_out, 0, "row", 2, 44.0),
           ("ffn_w1_0", ffn_w1, 0, "col", 8, 177.0), ("ffn_w2_0", ffn_w2, 0, "row", 8, 177.0),
           ("sc_w_in", sc_w_in, 0, "col", 4, 133.0), ("sc_w_out", sc_w_out, 0, "row", 2, 44.0),
           ("ffn_w1_1", ffn_w1, 1, "col", 8, 177.0), ("ffn_w2_1", ffn_w2, 1, "row", 8, 177.0)]
    kinds = [b[3] for b in big]
    chip = 2 * lax.axis_index("x") + lax.axis_index("y")
    place = jnp.stack([chip, lax.axis_index("c")]).astype(jnp.int32)

    def own_in_zeros(shard, ax):
        full = jnp.zeros(tuple(d * N_CHIPS if i == ax else d for i, d in enumerate(shard.shape)), shard.dtype)
        return lax.dynamic_update_slice_in_dim(full, shard, chip * shard.shape[ax], axis=ax)

    W = [_cast_into_full(w, layer, kind, place, "cast_" + name) for name, w, layer, kind, _, _ in big]
    smalls = [own_in_zeros(pool_w[0], 1), own_in_zeros(conv_w[0], 1), own_in_zeros(sc_conv_w[0], 1)]
    flying = {}

    def start_gather(u, after=()):
        if u < len(big):
            flying[u] = _start_job(_gather_job(W[u], kinds[u], 1), "gather_start_%d" % u, after)

    def ready(u, after):
        (W[u],), _ = _wait_job(flying[u], "gather_wait_%d" % u, [after])
        relay = _start_job(_gather_job(W[u], kinds[u], 2), "relay_start_%d" % u)
        start_gather(u + 2, [relay.token])
        (W[u],), _ = _wait_job(relay, "relay_wait_%d" % u, [flying[u + 2].token if u + 2 < len(big) else relay.token])
        onward = _start_job(_gather_job(W[u], kinds[u], 3), "forward_start_%d" % u)
        (W[u],), _ = _wait_job(onward, "forward_wait_%d" % u, [onward.token])
        return W[u]

    small_flight = _start_job(_gather_small_job(smalls, [1, 1, 1]), "gather_start_small")
    start_gather(0, [small_flight.token])
    start_gather(1, [flying[0].token])
    _Behind.pending = [f.token for f in flying.values()] + [small_flight.token]

    relu_sq = lambda acc: (jnp.maximum(acc, 0.0), jnp.square(jnp.maximum(acc, 0.0)))
    relu_sq_bwd = lambda acc, a: (acc * (2.0 * a.astype(F32)),)

    h0 = _norm_fwd(x0, gain(mix_pre_g, 0), "norm_in")
    (pool_w_full, conv_w_full, sc_conv_w_full), _ = _wait_job(small_flight, "gather_wait_small", [h0])
    z0 = _matmul(h0, ready(0, h0), "nn", "mix0_in")
    pooled, y_pool = _pool_fwd(z0, pool_w_full, pool_scale, "pool_fwd")
    a_conv, c_conv = _conv_fwd(z0, conv_w_full, conv_b, DP, "conv_fwd")
    y_conv = _ln_silu_fwd(c_conv, conv_ln_g, conv_ln_b, "ln_silu_fwd")
    y0 = jnp.concatenate([y_pool, y_conv], axis=1)
    m0 = _matmul(y0, ready(1, y0), "nn", "mix0_out")
    x1, h1 = _residual_norm(x0, m0, gain(mix_post_g, 0), gain(ffn_pre_g, 0), "res_mix0")
    a0, a0sq = _matmul(h1, ready(2, h1), "nn", "ffn0_up", out_dtypes=(BF16, BF16), epilogue=relu_sq)
    f0 = _matmul(a0sq, ready(3, a0sq), "nn", "ffn0_down")
    x2, h2 = _residual_norm(x1, f0, gain(ffn_post_g, 0), gain(mix_pre_g, 1), "res_ffn0")
    z1 = _matmul(h2, ready(4, h2), "nn", "mix1_in")
    y1 = _short_fwd(z1, sc_conv_w_full, "short_fwd")
    m1 = _matmul(y1, ready(5, y1), "nn", "mix1_out")
    x3, h3 = _residual_norm(x2, m1, gain(mix_post_g, 1), gain(ffn_pre_g, 1), "res_mix1")
    a1, a1sq = _matmul(h3, ready(6, h3), "nn", "ffn1_up", out_dtypes=(BF16, BF16), epilogue=relu_sq)
    f1 = _matmul(a1sq, ready(7, a1sq), "nn", "ffn1_down")
    w_in0, w_out0, w1_0, w2_0, w_in1, w_out1, w1_1, w2_1 = W

    grads_big = [None] * len(big)
    backlog = _Backlog()
    run = backlog.run

    def reduce_grad(u, g):
        name, _, _, kind, _, cost = big[u]
        _GradStream(backlog, u, name, kind, cost, g, place, grads_big)

    dx, df1, d_ffn_post_1, loss_row = run(_loss_and_last_norm_bwd, 30.0, x3, f1, gain(ffn_post_g, 1), target, "loss")
    reduce_grad(7, run(_matmul, 80.0, a1sq, df1, "tn", "ffn1_down_dw", out_dtypes=(BF16,)))
    dz = run(_matmul, 82.0, df1, w2_1, "nt", "ffn1_down_dx", out_dtypes=(BF16,), epilogue=relu_sq_bwd, epi=(a1,))
    reduce_grad(6, run(_matmul, 80.0, h3, dz, "tn", "ffn1_up_dw", out_dtypes=(BF16,)))
    dh = run(_matmul, 87.0, dz, w1_1, "nt", "ffn1_up_dx")
    dx, d_ffn_pre_1, dm1, d_mix_post_1 = run(_norms_bwd, 36.0, dx, dh, x3, gain(ffn_pre_g, 1), m1, gain(mix_post_g, 1), "norms_bwd3")

    reduce_grad(5, run(_matmul, 24.0, y1, dm1, "tn", "mix1_out_dw", out_dtypes=(BF16,)))
    dy1 = run(_matmul, 25.0, dm1, w_out1, "nt", "mix1_out_dx")
    db, dcg, du, d_sc_conv_w = run(_short_bwd, 41.0, dy1, z1, sc_conv_w_full, "short_bwd")
    dz1 = jnp.concatenate([db, dcg, du], axis=1)
    reduce_grad(4, run(_matmul, 62.0, h2, dz1, "tn", "mix1_in_dw", out_dtypes=(BF16,)))
    dh = run(_matmul, 68.0, dz1, w_in1, "nt", "mix1_in_dx")
    dx, d_mix_pre_1, df0, d_ffn_post_0 = run(_norms_bwd, 35.0, dx, dh, x2, gain(mix_pre_g, 1), f0, gain(ffn_post_g, 0), "norms_bwd2")

    reduce_grad(3, run(_matmul, 80.0, a0sq, df0, "tn", "ffn0_down_dw", out_dtypes=(BF16,)))
    dz = run(_matmul, 82.0, df0, w2_0, "nt", "ffn0_down_dx", out_dtypes=(BF16,), epilogue=relu_sq_bwd, epi=(a0,))
    reduce_grad(2, run(_matmul, 80.0, h1, dz, "tn", "ffn0_up_dw", out_dtypes=(BF16,)))
    dh = run(_matmul, 87.0, dz, w1_0, "nt", "ffn0_up_dx")
    dx, d_ffn_pre_0, dm0, d_mix_post_0 = run(_norms_bwd, 36.0, dx, dh, x1, gain(ffn_pre_g, 0), m0, gain(mix_post_g, 0), "norms_bwd1")

    reduce_grad(1, run(_matmul, 24.0, y0, dm0, "tn", "mix0_out_dw", out_dtypes=(BF16,)))
    dy0 = run(_matmul, 25.0, dm0, w_out0, "nt", "mix0_out_dx")
    du_pool, d_pool_w, d_pool_scale = run(_pool_bwd, 28.0, dy0, pooled, pool_w_full, pool_scale, "pool_bwd")
    dc, d_ln_g, d_ln_b = run(_ln_silu_bwd, 15.0, dy0, c_conv, conv_ln_g, conv_ln_b, "ln_silu_bwd")
    dv, dgate, d_conv_w, d_conv_b = run(_conv_bwd, 52.0, dc, a_conv, z0, conv_w_full, DP, "conv_bwd")
    dz0 = jnp.concatenate([du_pool, dv, dgate], axis=1)

    device_slot = 2 * chip + lax.axis_index("c")
    small_sums = {}

    def exchange_small(key, arrays, cost):
        slots = lax.dynamic_update_slice_in_dim(jnp.zeros((N_DEVICES,) + _pack(arrays, D).shape, F32), _pack(arrays, D)[None],
                                                device_slot, axis=0)
        backlog.start(_small_exchange_job(slots, 1, [0]), "small_grads_" + key, "ici", cost,
                      lambda outs, kept: small_sums.__setitem__(key, _unpack(_sum_slots(outs[0], "small_grads_sum_" + key),
                                                                             [a.shape for a in arrays])))

    exchange_small("most", [d_mix_pre_1, jnp.concatenate([d_mix_post_0, d_mix_post_1], 0),
                            jnp.concatenate([d_ffn_pre_0, d_ffn_pre_1], 0), jnp.concatenate([d_ffn_post_0, d_ffn_post_1], 0),
                            d_pool_scale, d_conv_b, d_ln_g, d_ln_b, d_pool_w, d_conv_w, d_sc_conv_w], 112.0)
    reduce_grad(0, run(_matmul, 34.0, h0, dz0, "tn", "mix0_in_dw", out_dtypes=(BF16,)))
    dh = run(_matmul, 40.0, dz0, w_in0, "nt", "mix0_in_dx")
    grad_x, d_mix_pre_0 = run(_norms_bwd, 26.0, dx, dh, x0, gain(mix_pre_g, 0), None, None, "norms_bwd0")
    exchange_small("last", [d_mix_pre_0], 5.0)
    loss = lax.psum(loss_row[0, 0], ("x", "y", "c"))

    upd, gr, first = {}, grads_big, {}

    def keep(where, key, outs):
        where[key] = outs
        return outs

    adamw_big = [
        (7, lambda: keep(first, "ffn_w2", _adamw(ffn_w2, gr[7], m_ffn_w2, v_ffn_w2, "adamw_ffn_w2_1", layer=1)), 46.0),
        (6, lambda: keep(first, "ffn_w1", _adamw(ffn_w1, gr[6], m_ffn_w1, v_ffn_w1, "adamw_ffn_w1_1", layer=1)), 46.0),
        (5, lambda: keep(upd, "sc_w_out", _adamw(sc_w_out, gr[5], m_sc_w_out, v_sc_w_out, "adamw_sc_w_out")), 14.0),
        (4, lambda: keep(upd, "sc_w_in", _adamw(sc_w_in, gr[4], m_sc_w_in, v_sc_w_in, "adamw_sc_w_in")), 35.0),
        (3, lambda: keep(upd, "ffn_w2", _adamw(ffn_w2, gr[3], m_ffn_w2, v_ffn_w2, "adamw_ffn_w2_0", layer=0,
                                               carried=first["ffn_w2"])), 46.0),
        (2, lambda: keep(upd, "ffn_w1", _adamw(ffn_w1, gr[2], m_ffn_w1, v_ffn_w1, "adamw_ffn_w1_0", layer=0,
                                               carried=first["ffn_w1"])), 46.0),
        (1, lambda: keep(upd, "ab_w_out", _adamw(ab_w_out, gr[1], m_ab_w_out, v_ab_w_out, "adamw_ab_w_out")), 14.0),
        (0, lambda: keep(upd, "ab_w_in", _adamw(ab_w_in, gr[0], m_ab_w_in, v_ab_w_in, "adamw_ab_w_in")), 19.0)]
    while adamw_big or backlog.flights:
        due = [a for a in adamw_big if gr[a[0]] is not None]
        if due:
            adamw_big.remove(due[0])
            backlog.run(due[0][1], due[0][2])
        else:
            backlog.poll(block=True)

    (g_mix_pre_1, g_mix_post, g_ffn_pre, g_ffn_post, g_pool_scale, g_conv_b, g_ln_g, g_ln_b, g_pool_w_full, g_conv_w_full,
     g_sc_conv_w_full) = small_sums["most"]
    g_mix_pre = jnp.concatenate([small_sums["last"][0], g_mix_pre_1], 0)
    own = lambda a, ax: lax.dynamic_slice_in_dim(a, chip * (a.shape[ax] // N_CHIPS), a.shape[ax] // N_CHIPS, axis=ax)
    g_pool_w, g_conv_w, g_sc_conv_w = own(g_pool_w_full, 1), own(g_conv_w_full, 1), own(g_sc_conv_w_full, 1)

    def small_update(w, g, m, v, name):
        shape = w.shape
        as3 = lambda a: a.reshape((1, -1, shape[-1]))
        outs = _adamw(as3(w), g.reshape((-1, shape[-1])), as3(m), as3(v), "adamw_" + name)
        return [o.reshape(shape) for o in outs]

    upd["mix_pre_g"] = small_update(mix_pre_g, g_mix_pre, m_mix_pre_g, v_mix_pre_g, "mix_pre_g")
    upd["mix_post_g"] = small_update(mix_post_g, g_mix_post, m_mix_post_g, v_mix_post_g, "mix_post_g")
    upd["ffn_pre_g"] = small_update(ffn_pre_g, g_ffn_pre, m_ffn_pre_g, v_ffn_pre_g, "ffn_pre_g")
    upd["ffn_post_g"] = small_update(ffn_post_g, g_ffn_post, m_ffn_post_g, v_ffn_post_g, "ffn_post_g")
    upd["pool_w"] = small_update(pool_w, g_pool_w, m_pool_w, v_pool_w, "pool_w")
    upd["pool_scale"] = small_update(pool_scale, g_pool_scale, m_pool_scale, v_pool_scale, "pool_scale")
    upd["conv_w"] = small_update(conv_w, g_conv_w, m_conv_w, v_conv_w, "conv_w")
    upd["conv_b"] = small_update(conv_b, g_conv_b, m_conv_b, v_conv_b, "conv_b")
    upd["conv_ln_g"] = small_update(conv_ln_g, g_ln_g, m_conv_ln_g, v_conv_ln_g, "conv_ln_g")
    upd["conv_ln_b"] = small_update(conv_ln_b, g_ln_b, m_conv_ln_b, v_conv_ln_b, "conv_ln_b")
    upd["sc_conv_w"] = small_update(sc_conv_w, g_sc_conv_w, m_sc_conv_w, v_sc_conv_w, "sc_conv_w")

    order = ["mix_pre_g", "mix_post_g", "ffn_pre_g", "ffn_post_g", "ab_w_in", "pool_w", "pool_scale", "conv_w", "conv_b",
             "conv_ln_g", "conv_ln_b", "ab_w_out", "sc_w_in", "sc_conv_w", "sc_w_out", "ffn_w1", "ffn_w2"]
    out = [loss, grad_x[None]]
    for part in range(4):
        out += [upd[n][part] for n in order]
    return tuple(out)
```

```python
import jax
import jax.numpy as jnp
from jax import lax
from jax.experimental import pallas as pl
from jax.experimental.pallas import tpu as pltpu

F32, BF16 = jnp.float32, jnp.bfloat16
EPS = 1e-6
N_GROUPS = 4
MAX_WINDOW = 16
CONV_K = 31
SHORT_K = 3
CONV_PAD = 32
SHORT_PAD = 8
ADAM_LR, ADAM_B1, ADAM_B2, ADAM_EPS, ADAM_WD, ADAM_STEP = 0.001, 0.9, 0.999, 1e-08, 0.01, 10
N_CHIPS = 4
VMEM_LIMIT_BYTES = 56 * 1024 * 1024
ROW_TILE = 256
CHUNK = 256
MESH = pl.DeviceIdType.MESH
HBM = pl.BlockSpec(memory_space=pltpu.HBM)
SEM = pl.BlockSpec(memory_space=pltpu.SEMAPHORE)
ANY = pl.BlockSpec(memory_space=pl.ANY)


def _cp(*sem):
    return pltpu.CompilerParams(dimension_semantics=sem, vmem_limit_bytes=VMEM_LIMIT_BYTES)


def _sigmoid(v):
    return 1.0 / (1.0 + jnp.exp(-v))


class _Behind:
    pending = []


def _call(body, jobs, prefetch=None, **kw):
    beside = _Beside(jobs)
    behind, _Behind.pending = _Behind.pending, []
    grid = kw["grid"]
    single = not isinstance(kw["out_shape"], (list, tuple))
    in_specs, scratch = list(kw["in_specs"]), list(kw.get("scratch_shapes", ()))
    out_shape = [kw["out_shape"]] if single else list(kw["out_shape"])
    out_specs = [kw["out_specs"]] if single else list(kw["out_specs"])
    n_pre = 0 if prefetch is None else 1
    n_own, n_out, n_scr = len(in_specs), len(out_shape), len(scratch)
    in_specs += [ANY] * len(behind)
    n_in = len(in_specs)

    def wrapped(*refs):
        pre, refs = refs[:n_pre], refs[n_pre:]
        ins, job_ins = refs[:n_own], refs[n_in:n_in + beside.n_in]
        outs = refs[n_in + beside.n_in:n_in + beside.n_in + n_out]
        job_outs = refs[n_in + beside.n_in + n_out:n_in + beside.n_in + n_out + beside.n_out]
        scr = refs[n_in + beside.n_in + n_out + beside.n_out:]
        beside.start(job_ins, job_outs, scr, grid)
        body(*pre, *ins, *outs, *scr[:n_scr])
        beside.finish(job_ins, job_outs, scr, grid)

    aliases = {n_pre + i: o for i, o in kw.get("input_output_aliases", {}).items()}
    aliases.update(beside.aliases(n_pre + n_in, n_out))
    specs = dict(grid=grid, in_specs=in_specs + [HBM] * beside.n_in, out_specs=out_specs + [HBM] * beside.n_out)
    if prefetch is None:
        specs["scratch_shapes"] = scratch + beside.scratch()
    else:
        specs = dict(grid_spec=pltpu.PrefetchScalarGridSpec(num_scalar_prefetch=1, scratch_shapes=scratch + beside.scratch(), **specs))
    sem = ["arbitrary"] * len(grid) if jobs else kw["sem"]
    call = pl.pallas_call(wrapped, name=kw["name"], out_shape=out_shape + beside.out_shape, input_output_aliases=aliases,
                          compiler_params=_cp(*sem), **specs)

    def run(*args):
        outs = call(*([prefetch] * n_pre), *args, *behind, *beside.ins)
        own = outs[0] if single else list(outs[:n_out])
        return (own, beside.split(outs[n_out:])) if jobs else own

    return run


_DIMS = {"nn": (((1,), (0,)), ((), ())), "nt": (((1,), (1,)), ((), ())), "tn": (((0,), (0,)), ((), ()))}


def _pick(n, cap, step=256):
    if n <= cap:
        return n
    return next(t for t in range(cap - cap % step, 0, -step) if n % t == 0)


def _matmul(a, b, mode, name, out_dtypes=(F32,), epilogue=None, epi=(), jobs=(), tm=1024, tn=1024, tk=2048):
    if mode == "tn":
        (K, M), (K2, N) = a.shape, b.shape
    elif mode == "nt":
        (M, K), (N, K2) = a.shape, b.shape
    else:
        (M, K), (K2, N) = a.shape, b.shape
    assert K == K2
    tm, tn, tk = _pick(M, tm), _pick(N, tn), _pick(K, tk)
    nk = K // tk
    a_spec = pl.BlockSpec((tk, tm), lambda i, j, k: (k, i)) if mode == "tn" else pl.BlockSpec((tm, tk), lambda i, j, k: (i, k))
    b_spec = pl.BlockSpec((tn, tk), lambda i, j, k: (j, k)) if mode == "nt" else pl.BlockSpec((tk, tn), lambda i, j, k: (k, j))
    o_spec = pl.BlockSpec((tm, tn), lambda i, j, k: (i, j))
    n_epi, n_out = len(epi), len(out_dtypes)

    def body(a_ref, b_ref, *rest):
        epi_refs, out_refs, scratch = rest[:n_epi], rest[n_epi:n_epi + n_out], rest[n_epi + n_out:]
        part = lax.dot_general(a_ref[...].astype(BF16), b_ref[...].astype(BF16), _DIMS[mode], preferred_element_type=F32)

        def finish(acc):
            outs = epilogue(acc, *[r[...] for r in epi_refs]) if epilogue else (acc,)
            for o_ref, o in zip(out_refs, outs):
                o_ref[...] = o.astype(o_ref.dtype)

        if nk == 1:
            finish(part)
        else:
            acc_ref = scratch[0]
            k = pl.program_id(2)

            @pl.when(k == 0)
            def _():
                acc_ref[...] = part

            @pl.when(k > 0)
            def _():
                acc_ref[...] += part

            @pl.when(k == nk - 1)
            def _():
                finish(acc_ref[...])

    got = _call(
        body, jobs, name=name, grid=(M // tm, N // tn, nk),
        in_specs=[a_spec, b_spec] + [o_spec] * n_epi, out_specs=[o_spec] * n_out,
        out_shape=[jax.ShapeDtypeStruct((M, N), dt) for dt in out_dtypes],
        scratch_shapes=[pltpu.VMEM((tm, tn), F32)] if nk > 1 else [],
        sem=("parallel", "parallel", "arbitrary"))(a, b, *epi)
    outs, job_outs = got if jobs else (got, None)
    result = outs[0] if n_out == 1 else outs
    return (result, job_outs) if jobs else result


def _rms(x, g):
    r = lax.rsqrt(jnp.mean(x * x, axis=-1, keepdims=True) + EPS)
    return x * r * g


def _rms_bwd(dy, x, g):
    r = lax.rsqrt(jnp.mean(x * x, axis=-1, keepdims=True) + EPS)
    xn = x * r
    dyg = dy * g
    dx = r * (dyg - xn * jnp.mean(dyg * xn, axis=-1, keepdims=True))
    return dx, jnp.sum(dy * xn, axis=0, keepdims=True)


def _rows(d, tr=ROW_TILE):
    return pl.BlockSpec((tr, d), lambda i: (i, 0))


def _vec(d):
    return pl.BlockSpec((1, d), lambda i: (0, 0))


def _accumulate(ref, val):
    @pl.when(pl.program_id(0) == 0)
    def _():
        ref[...] = val

    @pl.when(pl.program_id(0) > 0)
    def _():
        ref[...] += val


def _norm_fwd(x, g, name, jobs=()):
    T, D = x.shape

    def body(x_ref, g_ref, h_ref):
        h_ref[...] = _rms(x_ref[...], g_ref[...]).astype(BF16)

    return _call(body, jobs, name=name,grid=(T // ROW_TILE,), in_specs=[_rows(D), _vec(D)], out_specs=_rows(D),
                          out_shape=jax.ShapeDtypeStruct((T, D), BF16), sem=("parallel",))(x, g)


def _residual_norm(x, m, g_post, g_next, name, jobs=()):
    T, D = x.shape

    def body(x_ref, m_ref, gp_ref, gn_ref, xo_ref, h_ref):
        xo = x_ref[...] + _rms(m_ref[...], gp_ref[...])
        xo_ref[...] = xo
        h_ref[...] = _rms(xo, gn_ref[...]).astype(BF16)

    return _call(body, jobs, name=name,grid=(T // ROW_TILE,), in_specs=[_rows(D), _rows(D), _vec(D), _vec(D)],
                          out_specs=[_rows(D), _rows(D)],
                          out_shape=[jax.ShapeDtypeStruct((T, D), F32), jax.ShapeDtypeStruct((T, D), BF16)],
                          sem=("parallel",))(x, m, g_post, g_next)


def _loss_and_last_norm_bwd(x, m, g_post, target, name, jobs=()):
    T, D = x.shape

    def body(x_ref, m_ref, gp_ref, t_ref, dx_ref, dm_ref, dg_ref, loss_ref):
        m_val, gp = m_ref[...], gp_ref[...]
        err = x_ref[...] + _rms(m_val, gp) - t_ref[...]
        dx = err * (1.0 / D)
        dx_ref[...] = dx
        dm, dg = _rms_bwd(dx, m_val, gp)
        dm_ref[...] = dm.astype(BF16)
        _accumulate(dg_ref, dg)
        _accumulate(loss_ref, jnp.full((1, 128), 0.5 * jnp.sum(err * err) * (1.0 / D), F32))

    return _call(body, jobs, name=name,grid=(T // ROW_TILE,), in_specs=[_rows(D), _rows(D), _vec(D), _rows(D)],
                          out_specs=[_rows(D), _rows(D), _vec(D), _vec(128)],
                          out_shape=[jax.ShapeDtypeStruct((T, D), F32), jax.ShapeDtypeStruct((T, D), BF16),
                                     jax.ShapeDtypeStruct((1, D), F32), jax.ShapeDtypeStruct((1, 128), F32)],
                          sem=("arbitrary",))(x, m, g_post, target)


def _norms_bwd(dx, dh, x_in, g_pre, m_prev, g_post_prev, name, jobs=()):
    T, D = dx.shape
    with_prev = m_prev is not None

    def body(*refs):
        if with_prev:
            dx_ref, dh_ref, x_ref, gq_ref, m_ref, gp_ref, dxo_ref, dgq_ref, dm_ref, dgp_ref = refs
        else:
            dx_ref, dh_ref, x_ref, gq_ref, dxo_ref, dgq_ref = refs
        d_in, dgq = _rms_bwd(dh_ref[...], x_ref[...], gq_ref[...])
        dxo = dx_ref[...] + d_in
        dxo_ref[...] = dxo
        _accumulate(dgq_ref, dgq)
        if with_prev:
            dm, dgp = _rms_bwd(dxo, m_ref[...], gp_ref[...])
            dm_ref[...] = dm.astype(BF16)
            _accumulate(dgp_ref, dgp)

    ins, in_specs = [dx, dh, x_in, g_pre], [_rows(D), _rows(D), _rows(D), _vec(D)]
    out_specs = [_rows(D), _vec(D)]
    out_shape = [jax.ShapeDtypeStruct((T, D), F32), jax.ShapeDtypeStruct((1, D), F32)]
    if with_prev:
        ins += [m_prev, g_post_prev]
        in_specs += [_rows(D), _vec(D)]
        out_specs += [_rows(D), _vec(D)]
        out_shape += [jax.ShapeDtypeStruct((T, D), BF16), jax.ShapeDtypeStruct((1, D), F32)]
    return _call(body, jobs, name=name,grid=(T // ROW_TILE,), in_specs=in_specs, out_specs=out_specs, out_shape=out_shape,
                          sem=("arbitrary",))(*ins)


def _window_weights(g):
    w = 2 << g
    return w, [jnp.where(j < w, 1.0, 0.0).astype(F32) for j in range(MAX_WINDOW)]


def _valid_count(r0, rows, w):
    t = (lax.broadcasted_iota(jnp.int32, (rows, 1), 0) + (r0 + 1)).astype(F32)
    return jnp.minimum(t, w.astype(F32))


def _pool_fwd(z, pool_w, pool_scale, name, jobs=()):
    T = z.shape[0]
    PG = pool_w.shape[-1]
    DP = N_GROUPS * PG
    rc = min(CHUNK, T)

    def body(u_ref, pw_ref, sc_ref, pooled_ref, y_ref, pad):
        w, wts = _window_weights(pl.program_id(0))
        pad[pl.ds(0, MAX_WINDOW), :] = jnp.zeros((MAX_WINDOW, PG), F32)
        pad[pl.ds(MAX_WINDOW, T), :] = u_ref[...]
        for r0 in range(0, T, rc):
            acc = jnp.zeros((rc, PG), F32)
            for j in range(MAX_WINDOW):
                acc = acc + wts[j] * pad[pl.ds(MAX_WINDOW + r0 - j, rc), :]
            pooled = acc / _valid_count(r0, rc, w) - u_ref[pl.ds(r0, rc), :]
            pooled_ref[pl.ds(r0, rc), :] = pooled.astype(BF16)
        mixed = jnp.dot(pooled_ref[...], pw_ref[...].astype(BF16), preferred_element_type=F32)
        y_ref[...] = (mixed * sc_ref[...]).astype(BF16)

    col = lambda g: (0, g)
    return _call(
        body, jobs, name=name,grid=(N_GROUPS,),
        in_specs=[pl.BlockSpec((T, PG), col), pl.BlockSpec((None, PG, PG), lambda g: (g, 0, 0)), pl.BlockSpec((1, PG), col)],
        out_specs=[pl.BlockSpec((T, PG), col), pl.BlockSpec((T, PG), col)],
        out_shape=[jax.ShapeDtypeStruct((T, DP), BF16), jax.ShapeDtypeStruct((T, DP), BF16)],
        scratch_shapes=[pltpu.VMEM((T + MAX_WINDOW, PG), F32)], sem=("parallel",))(z, pool_w, pool_scale)


def _pool_bwd(dy, pooled, pool_w, pool_scale, name, jobs=()):
    T = dy.shape[0]
    PG = pool_w.shape[-1]
    DP = N_GROUPS * PG
    rc = min(CHUNK, T)

    def body(dy_ref, pooled_ref, pw_ref, sc_ref, du_ref, dpw_ref, dsc_ref, pad, dp_ref):
        w, wts = _window_weights(pl.program_id(0))
        pooled_v, pw = pooled_ref[...], pw_ref[...].astype(BF16)
        dy_v = dy_ref[...]
        mixed = jnp.dot(pooled_v, pw, preferred_element_type=F32)
        dsc_ref[...] = jnp.sum(dy_v * mixed, axis=0, keepdims=True)
        dmixed = (dy_v * sc_ref[...]).astype(BF16)
        dpw_ref[...] = lax.dot_general(pooled_v, dmixed, _DIMS["tn"], preferred_element_type=F32)
        dp_ref[...] = lax.dot_general(dmixed, pw, _DIMS["nt"], preferred_element_type=F32)
        pad[pl.ds(T, MAX_WINDOW), :] = jnp.zeros((MAX_WINDOW, PG), F32)
        for r0 in range(0, T, rc):
            pad[pl.ds(r0, rc), :] = dp_ref[pl.ds(r0, rc), :] / _valid_count(r0, rc, w)
        for r0 in range(0, T, rc):
            acc = jnp.zeros((rc, PG), F32)
            for j in range(MAX_WINDOW):
                acc = acc + wts[j] * pad[pl.ds(r0 + j, rc), :]
            du_ref[pl.ds(r0, rc), :] = (acc - dp_ref[pl.ds(r0, rc), :]).astype(BF16)

    col = lambda g: (0, g)
    return _call(
        body, jobs, name=name,grid=(N_GROUPS,),
        in_specs=[pl.BlockSpec((T, PG), col), pl.BlockSpec((T, PG), col), pl.BlockSpec((None, PG, PG), lambda g: (g, 0, 0)),
                  pl.BlockSpec((1, PG), col)],
        out_specs=[pl.BlockSpec((T, PG), col), pl.BlockSpec((None, PG, PG), lambda g: (g, 0, 0)), pl.BlockSpec((1, PG), col)],
        out_shape=[jax.ShapeDtypeStruct((T, DP), BF16), jax.ShapeDtypeStruct((N_GROUPS, PG, PG), F32),
                   jax.ShapeDtypeStruct((1, DP), F32)],
        scratch_shapes=[pltpu.VMEM((T + MAX_WINDOW, PG), F32), pltpu.VMEM((T, PG), F32)],
        sem=("parallel",))(dy, pooled, pool_w, pool_scale)


def _conv_fwd(z, conv_w, conv_b, d_pool, name, tc=128, jobs=()):
    T = z.shape[0]
    DC = conv_w.shape[-1]
    rc = min(CHUNK, T)
    v0, g0 = d_pool // tc, (d_pool + DC) // tc

    def body(v_ref, gt_ref, w_ref, b_ref, a_ref, c_ref, pad):
        pad[pl.ds(0, CONV_PAD), :] = jnp.zeros((CONV_PAD, tc), F32)
        for r0 in range(0, T, rc):
            a = v_ref[pl.ds(r0, rc), :] * _sigmoid(gt_ref[pl.ds(r0, rc), :])
            a_ref[pl.ds(r0, rc), :] = a
            pad[pl.ds(CONV_PAD + r0, rc), :] = a
        for r0 in range(0, T, rc):
            acc = jnp.zeros((rc, tc), F32) + b_ref[...]
            for k in range(CONV_K):
                acc = acc + w_ref[pl.ds(k, 1), :] * pad[pl.ds(CONV_PAD - (CONV_K - 1) + k + r0, rc), :]
            c_ref[pl.ds(r0, rc), :] = acc

    col = lambda j: (0, j)
    return _call(
        body, jobs, name=name,grid=(DC // tc,),
        in_specs=[pl.BlockSpec((T, tc), lambda j: (0, v0 + j)), pl.BlockSpec((T, tc), lambda j: (0, g0 + j)),
                  pl.BlockSpec((CONV_K, tc), col), pl.BlockSpec((1, tc), col)],
        out_specs=[pl.BlockSpec((T, tc), col), pl.BlockSpec((T, tc), col)],
        out_shape=[jax.ShapeDtypeStruct((T, DC), F32), jax.ShapeDtypeStruct((T, DC), F32)],
        scratch_shapes=[pltpu.VMEM((T + CONV_PAD, tc), F32)], sem=("parallel",))(z, z, conv_w, conv_b)


def _conv_bwd(dc, a, z, conv_w, d_pool, name, tc=128, jobs=()):
    T, DC = dc.shape
    rc = min(CHUNK, T)
    v0, g0 = d_pool // tc, (d_pool + DC) // tc

    def body(dc_ref, a_ref, v_ref, gt_ref, w_ref, dv_ref, dg_ref, dw_ref, db_ref, apad, dpad):
        apad[pl.ds(0, CONV_PAD), :] = jnp.zeros((CONV_PAD, tc), F32)
        apad[pl.ds(CONV_PAD, T), :] = a_ref[...]
        dpad[pl.ds(0, T), :] = dc_ref[...]
        dpad[pl.ds(T, CONV_PAD), :] = jnp.zeros((CONV_PAD, tc), F32)
        db_ref[...] = jnp.sum(dc_ref[...], axis=0, keepdims=True)
        for k in range(CONV_K):
            acc = jnp.zeros((8, tc), F32)
            for r0 in range(0, T, rc):
                prod = dc_ref[pl.ds(r0, rc), :] * apad[pl.ds(CONV_PAD - (CONV_K - 1) + k + r0, rc), :]
                acc = acc + jnp.sum(prod.reshape(rc // 8, 8, tc), axis=0)
            dw_ref[pl.ds(k, 1), :] = jnp.sum(acc, axis=0, keepdims=True)
        for r0 in range(0, T, rc):
            da = jnp.zeros((rc, tc), F32)
            for k in range(CONV_K):
                da = da + w_ref[pl.ds(k, 1), :] * dpad[pl.ds(r0 + (CONV_K - 1) - k, rc), :]
            sig = _sigmoid(gt_ref[pl.ds(r0, rc), :])
            dv_ref[pl.ds(r0, rc), :] = (da * sig).astype(BF16)
            dg_ref[pl.ds(r0, rc), :] = (da * v_ref[pl.ds(r0, rc), :] * sig * (1.0 - sig)).astype(BF16)

    col = lambda j: (0, j)
    return _call(
        body, jobs, name=name,grid=(DC // tc,),
        in_specs=[pl.BlockSpec((T, tc), col), pl.BlockSpec((T, tc), col), pl.BlockSpec((T, tc), lambda j: (0, v0 + j)),
                  pl.BlockSpec((T, tc), lambda j: (0, g0 + j)), pl.BlockSpec((CONV_K, tc), col)],
        out_specs=[pl.BlockSpec((T, tc), col), pl.BlockSpec((T, tc), col), pl.BlockSpec((CONV_K, tc), col),
                   pl.BlockSpec((1, tc), col)],
        out_shape=[jax.ShapeDtypeStruct((T, DC), BF16), jax.ShapeDtypeStruct((T, DC), BF16),
                   jax.ShapeDtypeStruct((CONV_K, DC), F32), jax.ShapeDtypeStruct((1, DC), F32)],
        scratch_shapes=[pltpu.VMEM((T + CONV_PAD, tc), F32), pltpu.VMEM((T + CONV_PAD, tc), F32)],
        sem=("parallel",))(dc, a, z, z, conv_w)


def _layer_norm_parts(c, g, b):
    mu = jnp.mean(c, axis=-1, keepdims=True)
    xc = c - mu
    rstd = lax.rsqrt(jnp.mean(xc * xc, axis=-1, keepdims=True) + EPS)
    xhat = xc * rstd
    return xhat, rstd, xhat * g + b


def _ln_silu_fwd(c, g, b, name, jobs=()):
    T, DC = c.shape

    def body(c_ref, g_ref, b_ref, y_ref):
        _, _, ln = _layer_norm_parts(c_ref[...], g_ref[...], b_ref[...])
        y_ref[...] = (ln * _sigmoid(ln)).astype(BF16)

    return _call(body, jobs, name=name,grid=(T // ROW_TILE,), in_specs=[_rows(DC), _vec(DC), _vec(DC)], out_specs=_rows(DC),
                          out_shape=jax.ShapeDtypeStruct((T, DC), BF16), sem=("parallel",))(c, g, b)


def _ln_silu_bwd(dy, c, g, b, name, jobs=()):
    T, DC = c.shape

    def body(dy_ref, c_ref, g_ref, b_ref, dc_ref, dg_ref, db_ref):
        gain = g_ref[...]
        xhat, rstd, ln = _layer_norm_parts(c_ref[...], gain, b_ref[...])
        s = _sigmoid(ln)
        dln = dy_ref[...] * (s * (1.0 + ln * (1.0 - s)))
        _accumulate(dg_ref, jnp.sum(dln * xhat, axis=0, keepdims=True))
        _accumulate(db_ref, jnp.sum(dln, axis=0, keepdims=True))
        dxh = dln * gain
        dc_ref[...] = rstd * (dxh - jnp.mean(dxh, axis=-1, keepdims=True) - xhat * jnp.mean(dxh * xhat, axis=-1, keepdims=True))

    return _call(body, jobs, name=name,grid=(T // ROW_TILE,),
                          in_specs=[pl.BlockSpec((ROW_TILE, DC), lambda i: (i, 1)), _rows(DC), _vec(DC), _vec(DC)],
                          out_specs=[_rows(DC), _vec(DC), _vec(DC)],
                          out_shape=[jax.ShapeDtypeStruct((T, DC), F32), jax.ShapeDtypeStruct((1, DC), F32),
                                     jax.ShapeDtypeStruct((1, DC), F32)],
                          sem=("arbitrary",))(dy, c, g, b)


def _short_specs(T, DS, tc):
    n = DS // tc
    return [pl.BlockSpec((T, tc), lambda j: (0, j)), pl.BlockSpec((T, tc), lambda j: (0, n + j)),
            pl.BlockSpec((T, tc), lambda j: (0, 2 * n + j))]


def _short_fwd(z, w, name, tc=256, jobs=()):
    T = z.shape[0]
    DS = w.shape[-1]
    rc = min(CHUNK, T)

    def body(b_ref, cg_ref, u_ref, w_ref, y_ref, pad):
        pad[pl.ds(0, SHORT_PAD), :] = jnp.zeros((SHORT_PAD, tc), F32)
        pad[pl.ds(SHORT_PAD, T), :] = cg_ref[...] * u_ref[...]
        for r0 in range(0, T, rc):
            r = jnp.zeros((rc, tc), F32)
            for k in range(SHORT_K):
                r = r + w_ref[pl.ds(k, 1), :] * pad[pl.ds(SHORT_PAD - (SHORT_K - 1) + k + r0, rc), :]
            y_ref[pl.ds(r0, rc), :] = (b_ref[pl.ds(r0, rc), :] * r).astype(BF16)

    col = lambda j: (0, j)
    return _call(body, jobs, name=name,grid=(DS // tc,), in_specs=_short_specs(T, DS, tc) + [pl.BlockSpec((SHORT_K, tc), col)],
                          out_specs=pl.BlockSpec((T, tc), col), out_shape=jax.ShapeDtypeStruct((T, DS), BF16),
                          scratch_shapes=[pltpu.VMEM((T + SHORT_PAD, tc), F32)], sem=("parallel",))(z, z, z, w)


def _short_bwd(dy, z, w, name, tc=256, jobs=()):
    T, DS = dy.shape
    rc = min(CHUNK, T)

    def body(dy_ref, b_ref, cg_ref, u_ref, w_ref, db_ref, dcg_ref, du_ref, dw_ref, qpad, rpad):
        qpad[pl.ds(0, SHORT_PAD), :] = jnp.zeros((SHORT_PAD, tc), F32)
        qpad[pl.ds(SHORT_PAD, T), :] = cg_ref[...] * u_ref[...]
        rpad[pl.ds(0, T), :] = dy_ref[...] * b_ref[...]
        rpad[pl.ds(T, SHORT_PAD), :] = jnp.zeros((SHORT_PAD, tc), F32)
        accs = [jnp.zeros((8, tc), F32) for _ in range(SHORT_K)]
        for r0 in range(0, T, rc):
            r = jnp.zeros((rc, tc), F32)
            dq = jnp.zeros((rc, tc), F32)
            dr = rpad[pl.ds(r0, rc), :]
            for k in range(SHORT_K):
                q_k = qpad[pl.ds(SHORT_PAD - (SHORT_K - 1) + k + r0, rc), :]
                r = r + w_ref[pl.ds(k, 1), :] * q_k
                dq = dq + w_ref[pl.ds(k, 1), :] * rpad[pl.ds(r0 + (SHORT_K - 1) - k, rc), :]
                accs[k] = accs[k] + jnp.sum((dr * q_k).reshape(rc // 8, 8, tc), axis=0)
            db_ref[pl.ds(r0, rc), :] = (dy_ref[pl.ds(r0, rc), :] * r).astype(BF16)
            dcg_ref[pl.ds(r0, rc), :] = (dq * u_ref[pl.ds(r0, rc), :]).astype(BF16)
            du_ref[pl.ds(r0, rc), :] = (dq * cg_ref[pl.ds(r0, rc), :]).astype(BF16)
        for k in range(SHORT_K):
            dw_ref[pl.ds(k, 1), :] = jnp.sum(accs[k], axis=0, keepdims=True)

    col = lambda j: (0, j)
    tile = pl.BlockSpec((T, tc), col)
    return _call(body, jobs, name=name,grid=(DS // tc,),
                          in_specs=[tile] + _short_specs(T, DS, tc) + [pl.BlockSpec((SHORT_K, tc), col)],
                          out_specs=[tile, tile, tile, pl.BlockSpec((SHORT_K, tc), col)],
                          out_shape=[jax.ShapeDtypeStruct((T, DS), BF16)] * 3 + [jax.ShapeDtypeStruct((SHORT_K, DS), F32)],
                          scratch_shapes=[pltpu.VMEM((T + SHORT_PAD, tc), F32), pltpu.VMEM((T + SHORT_PAD, tc), F32)],
                          sem=("parallel",))(dy, z, z, z, w)


def _tile_rows(rows, cols, n_bufs):
    budget = VMEM_LIMIT_BYTES // 2 // (2 * n_bufs * 4 * cols)
    tr = rows
    while tr > budget and tr % 16 == 0:
        tr //= 2
    return tr


def _placed_call(body, name, place, grid, in_specs, out_specs, out_shape, ins, jobs=()):
    return _call(body, jobs, prefetch=place, name=name, grid=grid, in_specs=in_specs, out_specs=out_specs, out_shape=out_shape,
                 sem=("parallel",))(*ins)


def _cast_into_full(w, layer, kind, place, name, jobs=()):
    _, R, C = w.shape
    tr = _tile_rows(R, C, 2)
    nb = R // tr
    if kind == "col":
        full, out_spec = (R, C * N_CHIPS), pl.BlockSpec((tr, C), lambda i, s: (i, s[0]))
    else:
        full, out_spec = (R * N_CHIPS, C), pl.BlockSpec((tr, C), lambda i, s: (s[0] * nb + i, 0))

    def body(s_ref, w_ref, o_ref):
        o_ref[...] = w_ref[...].astype(BF16)

    return _placed_call(body, name, place, (nb,), [pl.BlockSpec((None, tr, C), lambda i, s: (layer, i, 0))], out_spec,
                        jax.ShapeDtypeStruct(full, BF16), [w], jobs)


def _add_pair(grad, theirs, kind, place, name, jobs=()):
    R, C = grad.shape
    piece_rows = R // 2 if kind == "col" else R // N_CHIPS // 2
    tr = _tile_rows(piece_rows, C, 3)
    nb = piece_rows // tr
    if kind == "col":
        g_spec = pl.BlockSpec((tr, C), lambda i, s: (s[1] * nb + i, 0))
    else:
        g_spec = pl.BlockSpec((tr, C), lambda i, s: ((2 * (i // nb) + s[1]) * nb + i % nb, 0))
    flat = pl.BlockSpec((tr, C), lambda i, s: (i, 0))

    def body(s_ref, a_ref, b_ref, o_ref):
        o_ref[...] = (a_ref[...].astype(F32) + b_ref[...].astype(F32)).astype(BF16)

    return _placed_call(body, name, place, (R // 2 // tr,), [g_spec, flat], flat, jax.ShapeDtypeStruct((R // 2, C), BF16),
                        [grad, theirs], jobs)


def _sum_chips(chip_sum, arrived, kind, place, name, jobs=()):
    _, H, W = arrived.shape
    tr = _tile_rows(H, W, 6)
    nb = H // tr
    if kind == "col":
        own_spec = pl.BlockSpec((tr, W), lambda i, s: (i, s[0]))
    else:
        own_spec = pl.BlockSpec((tr, W), lambda i, s: (s[0] * nb + i, 0))

    def body(s_ref, p_ref, r_ref, o_ref):
        acc = p_ref[...].astype(F32)
        for i in range(N_CHIPS - 1):
            acc = acc + r_ref[i].astype(F32)
        o_ref[...] = acc

    return _placed_call(body, name, place, (nb,), [own_spec, pl.BlockSpec((N_CHIPS - 1, tr, W), lambda i, s: (0, i, 0))],
                        pl.BlockSpec((tr, W), lambda i, s: (s[1] * nb + i, 0)), jax.ShapeDtypeStruct((2 * H, W), F32),
                        [chip_sum, arrived], jobs)


def _adamw_values(w, g, m, v):
    m = ADAM_B1 * m + (1.0 - ADAM_B1) * g
    v = ADAM_B2 * v + (1.0 - ADAM_B2) * (g * g)
    m_hat = m / (1.0 - ADAM_B1 ** ADAM_STEP)
    v_hat = v / (1.0 - ADAM_B2 ** ADAM_STEP)
    return -ADAM_LR * (m_hat / (jnp.sqrt(v_hat) + ADAM_EPS) + ADAM_WD * w), m, v


def _adamw(w, g, m, v, name, layer=0, carried=None):
    L, R, C = w.shape
    tr = _tile_rows(R, C, 8)

    def body(w_ref, g_ref, m_ref, v_ref, *rest):
        go_ref, d_ref, mo_ref, vo_ref = rest[-4:]
        g_val = g_ref[...]
        d, m_new, v_new = _adamw_values(w_ref[...], g_val, m_ref[...], v_ref[...])
        go_ref[...], d_ref[...], mo_ref[...], vo_ref[...] = g_val, d, m_new, v_new

    lay = pl.BlockSpec((None, tr, C), lambda i: (layer, i, 0))
    ins = [w, g, m, v]
    in_specs = [lay, pl.BlockSpec((tr, C), lambda i: (i, 0)), lay, lay]
    aliases = {}
    if carried is not None:
        ins += list(carried)
        in_specs += [pl.BlockSpec(memory_space=pl.ANY)] * 4
        aliases = {4 + i: i for i in range(4)}
    return _call(body, (), name=name, grid=(R // tr,), in_specs=in_specs, out_specs=[lay] * 4,
                 out_shape=[jax.ShapeDtypeStruct((L, R, C), F32)] * 4, input_output_aliases=aliases, sem=("parallel",))(*ins)


def _aligned(v, m):
    return v if isinstance(v, int) else pl.multiple_of(v, m)


def _place():
    x, y, c = lax.axis_index("x"), lax.axis_index("y"), lax.axis_index("c")
    other_chips = [(x, 1 - y), (1 - x, y), (1 - x, 1 - y)]
    return x, y, c, 2 * x + y, other_chips


def _chip_index(chip):
    return 2 * chip[0] + chip[1]


def _piece(ref, kind, k, h):
    R, C = ref.shape
    if kind == "col":
        return ref.at[pl.ds(_aligned(h * (R // 2), 16), R // 2), pl.ds(_aligned(k * (C // N_CHIPS), 128), C // N_CHIPS)]
    rs = R // N_CHIPS
    return ref.at[pl.ds(_aligned(k * rs + h * (rs // 2), 16), rs // 2), :]


def _compact_piece(ref, kind, k):
    R2, C = ref.shape
    if kind == "col":
        return ref.at[:, pl.ds(_aligned(k * (C // N_CHIPS), 128), C // N_CHIPS)]
    return ref.at[pl.ds(_aligned(k * (R2 // N_CHIPS), 16), R2 // N_CHIPS), :]


def _half_rows(ref, h):
    R = ref.shape[0]
    return ref.at[pl.ds(_aligned(h * (R // 2), 16), R // 2), :]


class _Copies:
    def __init__(self, send_sems, recv_sems):
        self.send_sems, self.recv_sems = send_sems, recv_sems
        self.n_remote = 0

    def remote(self, src, dst, device):
        k = self.n_remote
        self.n_remote += 1
        return pltpu.make_async_remote_copy(src_ref=src, dst_ref=dst, send_sem=self.send_sems.at[k], recv_sem=self.recv_sems.at[k],
                                            device_id=device, device_id_type=MESH)


class _Job:
    def __init__(self, ins, out_shape, aliases, n_remote, build):
        self.ins, self.out_shape, self.aliases, self.n_remote, self.build = list(ins), list(out_shape), dict(aliases), n_remote, build


def _merge_jobs(jobs):
    ins, out_shape, aliases, spans = [], [], {}, []
    for job in jobs:
        spans.append((len(ins), len(job.ins), len(out_shape), len(job.out_shape)))
        aliases.update({len(ins) + i: len(out_shape) + o for i, o in job.aliases.items()})
        ins += job.ins
        out_shape += job.out_shape

    def build(in_refs, out_refs, cp):
        copies = []
        for job, (i0, ni, o0, no) in zip(jobs, spans):
            copies += job.build(in_refs[i0:i0 + ni], out_refs[o0:o0 + no], cp)
        return copies

    return _Job(ins, out_shape, aliases, sum(j.n_remote for j in jobs), build)


def _run_jobs(jobs, name):
    job = _merge_jobs(jobs)
    n_in, n_out = len(job.ins), len(job.out_shape)

    def body(*refs):
        copies = job.build(refs[:n_in], refs[n_in:n_in + n_out], _Copies(*refs[n_in + n_out:]))
        for d in copies:
            d.start()
        for d in copies:
            d.wait_recv()
        for d in copies:
            d.wait_send()

    outs = pl.pallas_call(
        body, name=name, in_specs=[HBM] * n_in, out_specs=[HBM] * n_out, out_shape=job.out_shape,
        input_output_aliases=job.aliases,
        scratch_shapes=[pltpu.SemaphoreType.DMA((job.n_remote,)), pltpu.SemaphoreType.DMA((job.n_remote,))])(*job.ins)
    split, o0 = [], 0
    for j in jobs:
        split.append(outs[o0:o0 + len(j.out_shape)])
        o0 += len(j.out_shape)
    return split


class _Flying:
    def __init__(self, job, send_sems, recv_sems, bufs, token):
        self.job, self.send_sems, self.recv_sems, self.bufs, self.token = job, send_sems, recv_sems, bufs, token


def _job_refs(job, buf_refs):
    n_out = len(job.out_shape)
    kept = [i for i in range(len(job.ins)) if i not in job.aliases]
    ins = [buf_refs[job.aliases[i]] if i in job.aliases else buf_refs[n_out + kept.index(i)] for i in range(len(job.ins))]
    return ins, list(buf_refs[:n_out])


def _start_job(job, name, after=()):
    n_in, n_out, n_after = len(job.ins), len(job.out_shape), len(after)
    kept = [i for i in range(n_in) if i not in job.aliases]
    n_bufs = n_out + len(kept)

    def body(*refs):
        in_refs, out_refs = refs[:n_in], refs[n_in + n_after:n_in + n_after + n_out]
        send_sems, recv_sems, token = refs[n_in + n_after + n_bufs:]
        for d in job.build(in_refs, out_refs, _Copies(send_sems, recv_sems)):
            d.start()
        token[...] = jnp.zeros_like(token)

    aliases = dict(job.aliases)
    aliases.update({i: n_out + k for k, i in enumerate(kept)})
    sems = pltpu.SemaphoreType.DMA((job.n_remote,))
    outs = pl.pallas_call(
        body, name=name, in_specs=[HBM] * n_in + [ANY] * n_after,
        out_specs=[HBM] * n_bufs + [SEM, SEM, pl.BlockSpec(memory_space=pltpu.VMEM)],
        out_shape=job.out_shape + [jax.ShapeDtypeStruct(job.ins[i].shape, job.ins[i].dtype) for i in kept]
        + [sems, sems, jax.ShapeDtypeStruct((8, 128), F32)],
        input_output_aliases=aliases,
        compiler_params=pltpu.CompilerParams(has_side_effects=pltpu.SideEffectType.DATAFLOW_SIDE_EFFECTING))(*job.ins, *after)
    return _Flying(job, outs[n_bufs], outs[n_bufs + 1], list(outs[:n_bufs]), outs[n_bufs + 2])


def _wait_job(flying, name, after=()):
    job, n_bufs, n_after = flying.job, len(flying.bufs), len(after)

    def body(*refs):
        in_refs, out_refs = _job_refs(job, refs[:n_bufs])
        send_sems, recv_sems = refs[n_bufs:n_bufs + 2]
        copies = job.build(in_refs, out_refs, _Copies(send_sems, recv_sems))
        for d in copies:
            d.wait_send()
        for d in copies:
            d.wait_recv()

    outs = pl.pallas_call(
        body, name=name, in_specs=[HBM] * n_bufs + [SEM, SEM] + [ANY] * n_after, out_specs=[HBM] * n_bufs,
        out_shape=[jax.ShapeDtypeStruct(b.shape, b.dtype) for b in flying.bufs],
        input_output_aliases={i: i for i in range(n_bufs)},
        compiler_params=pltpu.CompilerParams(has_side_effects=pltpu.SideEffectType.DATAFLOW_SIDE_EFFECTING))(
            *flying.bufs, flying.send_sems, flying.recv_sems, *after)
    return list(outs[:len(job.out_shape)]), list(outs[len(job.out_shape):])


class _Beside:
    def __init__(self, jobs):
        self.jobs = list(jobs)
        job = self.job = _merge_jobs(self.jobs) if self.jobs else None
        self.ins = job.ins if job else []
        self.out_shape = job.out_shape if job else []
        self.n_in, self.n_out = len(self.ins), len(self.out_shape)

    def scratch(self):
        n = self.job.n_remote if self.job else 0
        return [pltpu.SemaphoreType.DMA((n,)), pltpu.SemaphoreType.DMA((n,))] if n else []

    def aliases(self, first_in, first_out):
        return {first_in + i: first_out + o for i, o in self.job.aliases.items()} if self.job else {}

    def _at(self, grid, which):
        cond = None
        for axis, n in enumerate(grid):
            here = pl.program_id(axis) == (0 if which == "first" else n - 1)
            cond = here if cond is None else jnp.logical_and(cond, here)
        return cond

    def start(self, in_refs, out_refs, scratch, grid):
        if self.job:
            @pl.when(self._at(grid, "first"))
            def _():
                for d in self.job.build(in_refs, out_refs, _Copies(*scratch[-2:])):
                    d.start()

    def finish(self, in_refs, out_refs, scratch, grid):
        if self.job:
            @pl.when(self._at(grid, "last"))
            def _():
                copies = self.job.build(in_refs, out_refs, _Copies(*scratch[-2:]))
                for d in copies:
                    d.wait_recv()
                for d in copies:
                    d.wait_send()

    def split(self, outs):
        per_job, o0 = [], 0
        for j in self.jobs:
            per_job.append(list(outs[o0:o0 + len(j.out_shape)]))
            o0 += len(j.out_shape)
        return per_job


def _in_place(arrays):
    return [jax.ShapeDtypeStruct(a.shape, a.dtype) for a in arrays], {u: u for u in range(len(arrays))}


def _rows_part(ref, part, n_parts):
    h = ref.shape[0] // n_parts
    return ref.at[pl.ds(part * h, h), :]


def _gather_job(full, kind, stage):
    def build(in_refs, out_refs, cp):
        x, y, c, me, (y_nbr, x_nbr, diagonal) = _place()
        (ref,) = out_refs
        sibling = (x, y, 1 - c)
        if stage == 1:
            mine = _piece(ref, kind, me, c)
            return [cp.remote(mine, mine, (*y_nbr, c)), cp.remote(mine, mine, (*x_nbr, c))]
        from_y, from_x = _piece(ref, kind, _chip_index(y_nbr), c), _piece(ref, kind, _chip_index(x_nbr), c)
        if stage == 2:
            relay_0, relay_1 = _rows_part(from_x, 0, 2), _rows_part(from_y, 1, 2)
            return [cp.remote(relay_0, relay_0, (*y_nbr, c)), cp.remote(relay_1, relay_1, (*x_nbr, c)),
                    cp.remote(from_y, from_y, sibling), cp.remote(from_x, from_x, sibling)]
        from_diagonal = _piece(ref, kind, _chip_index(diagonal), c)
        return [cp.remote(from_diagonal, from_diagonal, sibling)]

    return _Job([full], *_in_place([full]), {1: 2, 2: 4, 3: 1}[stage], build)


def _gather_small_job(fulls, axes):
    def build(in_refs, out_refs, cp):
        x, y, c, me, chips = _place()
        copies = []
        for ref, ax in zip(out_refs, axes):
            n = ref.shape[ax] // N_CHIPS
            idx = [slice(None)] * len(ref.shape)
            idx[ax] = pl.ds(_aligned(me * n, n), n)
            mine = ref.at[tuple(idx)]
            copies += [cp.remote(mine, mine, (*chip, c)) for chip in chips]
        return copies

    return _Job(fulls, *_in_place(fulls), 3 * len(fulls), build)


def _exchange_halves_job(grads, kinds):
    def build(in_refs, out_refs, cp):
        x, y, c, me, chips = _place()
        copies = []
        for src, dst, kind in zip(in_refs, out_refs, kinds):
            if kind == "col":
                copies.append(cp.remote(_half_rows(src, 1 - c), dst, (x, y, 1 - c)))
            else:
                copies += [cp.remote(_piece(src, "row", k, 1 - c), _compact_piece(dst, "row", k), (x, y, 1 - c))
                           for k in range(N_CHIPS)]
        return copies

    out_shape = [jax.ShapeDtypeStruct((g.shape[0] // 2, g.shape[1]), g.dtype) for g in grads]
    return _Job(grads, out_shape, {}, sum(1 if k == "col" else N_CHIPS for k in kinds), build)


def _scatter_job(half, kind, n_parts, parts, into=None):
    def build(in_refs, out_refs, cp):
        x, y, c, me, chips = _place()
        src, (dst,) = in_refs[0], out_refs
        copies = []
        for p in parts:
            copies += [cp.remote(_rows_part(_compact_piece(src, kind, _chip_index(chip)), p, n_parts),
                                 _rows_part(dst.at[r], p, n_parts), (*chip, c)) for r, chip in enumerate(chips)]
        return copies

    part_shape = (half.shape[0], half.shape[1] // N_CHIPS) if kind == "col" else (half.shape[0] // N_CHIPS, half.shape[1])
    out_shape = [jax.ShapeDtypeStruct((N_CHIPS - 1,) + part_shape, half.dtype)]
    ins, aliases = ([half], {}) if into is None else ([half, into], {1: 0})
    return _Job(ins, out_shape, aliases, 3 * len(parts), build)


def _share_job(shards):
    def build(in_refs, out_refs, cp):
        x, y, c, me, chips = _place()
        copies = []
        for ref in out_refs:
            mine = _half_rows(ref, c)
            copies.append(cp.remote(mine, mine, (x, y, 1 - c)))
        return copies

    return _Job(shards, *_in_place(shards), len(shards), build)


N_DEVICES = 2 * N_CHIPS


def _small_exchange_job(slots, n_parts, parts):
    def build(in_refs, out_refs, cp):
        x, y, c, me, chips = _place()
        (ref,) = out_refs
        copies = []
        for part in parts:
            mine = _rows_part(ref.at[2 * me + c], part, n_parts)
            copies += [cp.remote(mine, mine, (x ^ (p >> 2), y ^ ((p >> 1) & 1), c ^ (p & 1))) for p in range(1, N_DEVICES)]
        return copies

    return _Job([slots], *_in_place([slots]), (N_DEVICES - 1) * len(parts), build)


def _sum_slots(slots, name, jobs=()):
    n, R, C = slots.shape

    def body(s_ref, o_ref):
        acc = s_ref[0]
        for i in range(1, n):
            acc = acc + s_ref[i]
        o_ref[...] = acc

    return _call(body, jobs, name=name, grid=(1,), in_specs=[pl.BlockSpec((n, R, C), lambda i: (0, 0, 0))],
                 out_specs=pl.BlockSpec((R, C), lambda i: (0, 0)), out_shape=jax.ShapeDtypeStruct((R, C), F32),
                 sem=("arbitrary",))(slots)


def _pack(arrays, width):
    rows = []
    for a in arrays:
        flat = a.reshape(-1)
        n_rows = -(-flat.shape[0] // width)
        rows.append(jnp.pad(flat, (0, n_rows * width - flat.shape[0])).reshape(n_rows, width))
    n = sum(r.shape[0] for r in rows)
    rows.append(jnp.zeros((-n % 8, width), F32))
    return jnp.concatenate(rows, axis=0)


def _unpack(packed, shapes):
    out, r0, width = [], 0, packed.shape[1]
    for shape in shapes:
        size = 1
        for d in shape:
            size *= d
        n_rows = -(-size // width)
        out.append(packed[r0:r0 + n_rows].reshape(-1)[:size].reshape(shape))
        r0 += n_rows
    return out


class _Backlog:
    def __init__(self):
        self.now, self.free, self.flights, self.last = 0.0, {"ici": 0.0, "d2d": 0.0}, [], None

    def run(self, fn, us, *args, **kw):
        out = fn(*args, **kw)
        self.now += us
        self.last = out[0] if isinstance(out, (list, tuple)) else out
        self.poll()
        return out

    def start(self, job, name, link, cost, done):
        flying = _start_job(job, name + "_start")
        _Behind.pending.append(flying.token)
        ends = max(self.now, self.free[link]) + cost
        self.free[link] = ends
        self.flights.append((ends + LANDING_SLACK_US, name, flying, done))
        self.flights.sort(key=lambda f: f[0])

    def poll(self, block=False):
        while self.flights and (block or self.flights[0][0] <= self.now):
            ends, name, flying, done = self.flights.pop(0)
            self.now, block = max(self.now, ends), False
            done(*_wait_job(flying, name + "_wait", [self.last]))


class _GradStream:
    def __init__(self, backlog, u, name, kind, cost, g, place, results):
        self.backlog, self.u, self.name, self.kind, self.cost, self.place, self.results = backlog, u, name, kind, cost, place, results
        backlog.start(_exchange_halves_job([g], [kind]), "to_sibling_" + name, "d2d", D2D_SHARE * cost, self.exchanged)

    def exchanged(self, outs, kept):
        chip_sum = self.backlog.run(_add_pair, SIDE_KERNEL_US, kept[0], outs[0], self.kind, self.place, "chip_sum_" + self.name)
        self.backlog.start(_scatter_job(chip_sum, self.kind, 1, [0]), "to_owners_" + self.name, "ici", self.cost, self.scattered)

    def scattered(self, outs, kept):
        reduced = self.backlog.run(_sum_chips, SIDE_KERNEL_US, kept[0], outs[0], self.kind, self.place, "reduce_" + self.name)
        self.backlog.start(_share_job([reduced]), "share_" + self.name, "d2d", D2D_SHARE * self.cost, self.shared)

    def shared(self, outs, kept):
        self.results[self.u] = outs[0]


SIDE_KERNEL_US = 12.0
D2D_SHARE = 0.15
LANDING_SLACK_US = 5.0


def kernel(x, mix_pre_g, mix_post_g, ffn_pre_g, ffn_post_g, ab_w_in, pool_w, pool_scale, conv_w, conv_b, conv_ln_g, conv_ln_b, ab_w_out, sc_w_in, sc_conv_w, sc_w_out, ffn_w1, ffn_w2, loss_target, m_mix_pre_g, m_mix_post_g, m_ffn_pre_g, m_ffn_post_g, m_ab_w_in, m_pool_w, m_pool_scale, m_conv_w, m_conv_b, m_conv_ln_g, m_conv_ln_b, m_ab_w_out, m_sc_w_in, m_sc_conv_w, m_sc_w_out, m_ffn_w1, m_ffn_w2, v_mix_pre_g, v_mix_post_g, v_ffn_pre_g, v_ffn_post_g, v_ab_w_in, v_pool_w, v_pool_scale, v_conv_w, v_conv_b, v_conv_ln_g, v_conv_ln_b, v_ab_w_out, v_sc_w_in, v_sc_conv_w, v_sc_w_out, v_ffn_w1, v_ffn_w2):
    x0, target = x[0], loss_target[0]
    T, D = x0.shape
    DP = pool_scale.shape[-1]
    gain = lambda g, layer: g[layer][None, :]

    big = [("ab_w_in", ab_w_in, 0, "col", 4, 67.0), ("ab_w_out", ab_w_out, 0, "row", 2, 44.0),
           ("ffn_w1_0", ffn_w1, 0, "col", 8, 177.0), ("ffn_w2_0", ffn_w2, 0, "row", 8, 177.0),
           ("sc_w_in", sc_w_in, 0, "col", 4, 133.0), ("sc_w_out", sc_w_out, 0, "row", 2, 44.0),
           ("ffn_w1_1", ffn_w1, 1, "col", 8, 177.0), ("ffn_w2_1", ffn_w2, 1, "row", 8, 177.0)]
    kinds = [b[3] for b in big]
    chip = 2 * lax.axis_index("x") + lax.axis_index("y")
    place = jnp.stack([chip, lax.axis_index("c")]).astype(jnp.int32)

    def own_in_zeros(shard, ax):
        full = jnp.zeros(tuple(d * N_CHIPS if i == ax else d for i, d in enumerate(shard.shape)), shard.dtype)
        return lax.dynamic_update_slice_in_dim(full, shard, chip * shard.shape[ax], axis=ax)

    W = [_cast_into_full(w, layer, kind, place, "cast_" + name) for name, w, layer, kind, _, _ in big]
    smalls = [own_in_zeros(pool_w[0], 1), own_in_zeros(conv_w[0], 1), own_in_zeros(sc_conv_w[0], 1)]
    flying = {}

    def start_gather(u, after=()):
        if u < len(big):
            flying[u] = _start_job(_gather_job(W[u], kinds[u], 1), "gather_start_%d" % u, after)

    first_use = [1, 3, 5, 6, 8, 10, 12, 13]
    relays, onwards = {}, {}

    def gather_steps(slot, after):
        for u in range(len(big)):
            if max(first_use[u] - 2, 0) == slot:
                (W[u],), _ = _wait_job(flying[u], "gather_wait_%d" % u, [after])
                relays[u] = _start_job(_gather_job(W[u], kinds[u], 2), "relay_start_%d" % u)
                start_gather(u + 2, [relays[u].token])
                _Behind.pending += [relays[u].token] + ([flying[u + 2].token] if u + 2 < len(big) else [])
        for u in range(len(big)):
            if max(first_use[u] - 1, 0) == slot:
                (W[u],), _ = _wait_job(relays[u], "relay_wait_%d" % u, [after])
                onwards[u] = _start_job(_gather_job(W[u], kinds[u], 3), "forward_start_%d" % u)
                _Behind.pending.append(onwards[u].token)
        for u in range(len(big)):
            if first_use[u] == slot:
                (W[u],), _ = _wait_job(onwards[u], "forward_wait_%d" % u, [after])

    small_flight = _start_job(_gather_small_job(smalls, [1, 1, 1]), "gather_start_small")
    start_gather(0, [small_flight.token])
    start_gather(1, [flying[0].token])

    relu_sq = lambda acc: (jnp.maximum(acc, 0.0), jnp.square(jnp.maximum(acc, 0.0)))
    relu_sq_bwd = lambda acc, a: (acc * (2.0 * a.astype(F32)),)

    _Behind.pending = [flying[1].token]
    h0 = _norm_fwd(x0, gain(mix_pre_g, 0), "norm_in")
    (pool_w_full, conv_w_full, sc_conv_w_full), _ = _wait_job(small_flight, "gather_wait_small", [h0])
    gather_steps(0, h0)
    gather_steps(1, h0)
    z0 = _matmul(h0, W[0], "nn", "mix0_in")
    gather_steps(2, z0)
    pooled, y_pool = _pool_fwd(z0, pool_w_full, pool_scale, "pool_fwd")
    a_conv, c_conv = _conv_fwd(z0, conv_w_full, conv_b, DP, "conv_fwd")
    y_conv = _ln_silu_fwd(c_conv, conv_ln_g, conv_ln_b, "ln_silu_fwd")
    y0 = jnp.concatenate([y_pool, y_conv], axis=1)
    gather_steps(3, y0)
    m0 = _matmul(y0, W[1], "nn", "mix0_out")
    gather_steps(4, m0)
    x1, h1 = _residual_norm(x0, m0, gain(mix_post_g, 0), gain(ffn_pre_g, 0), "res_mix0")
    gather_steps(5, h1)
    a0, a0sq = _matmul(h1, W[2], "nn", "ffn0_up", out_dtypes=(BF16, BF16), epilogue=relu_sq)
    gather_steps(6, a0sq)
    f0 = _matmul(a0sq, W[3], "nn", "ffn0_down")
    gather_steps(7, f0)
    x2, h2 = _residual_norm(x1, f0, gain(ffn_post_g, 0), gain(mix_pre_g, 1), "res_ffn0")
    gather_steps(8, h2)
    z1 = _matmul(h2, W[4], "nn", "mix1_in")
    gather_steps(9, z1)
    y1 = _short_fwd(z1, sc_conv_w_full, "short_fwd")
    gather_steps(10, y1)
    m1 = _matmul(y1, W[5], "nn", "mix1_out")
    gather_steps(11, m1)
    x3, h3 = _residual_norm(x2, m1, gain(mix_post_g, 1), gain(ffn_pre_g, 1), "res_mix1")
    gather_steps(12, h3)
    a1, a1sq = _matmul(h3, W[6], "nn", "ffn1_up", out_dtypes=(BF16, BF16), epilogue=relu_sq)
    gather_steps(13, a1sq)
    f1 = _matmul(a1sq, W[7], "nn", "ffn1_down")
    w_in0, w_out0, w1_0, w2_0, w_in1, w_out1, w1_1, w2_1 = W

    grads_big = [None] * len(big)
    backlog = _Backlog()
    run = backlog.run

    def reduce_grad(u, g):
        name, _, _, kind, _, cost = big[u]
        _GradStream(backlog, u, name, kind, cost, g, place, grads_big)

    dx, df1, d_ffn_post_1, loss_row = run(_loss_and_last_norm_bwd, 30.0, x3, f1, gain(ffn_post_g, 1), target, "loss")
    reduce_grad(7, run(_matmul, 80.0, a1sq, df1, "tn", "ffn1_down_dw", out_dtypes=(BF16,)))
    dz = run(_matmul, 82.0, df1, w2_1, "nt", "ffn1_down_dx", out_dtypes=(BF16,), epilogue=relu_sq_bwd, epi=(a1,))
    reduce_grad(6, run(_matmul, 80.0, h3, dz, "tn", "ffn1_up_dw", out_dtypes=(BF16,)))
    dh = run(_matmul, 87.0, dz, w1_1, "nt", "ffn1_up_dx")
    dx, d_ffn_pre_1, dm1, d_mix_post_1 = run(_norms_bwd, 36.0, dx, dh, x3, gain(ffn_pre_g, 1), m1, gain(mix_post_g, 1), "norms_bwd3")

    reduce_grad(5, run(_matmul, 24.0, y1, dm1, "tn", "mix1_out_dw", out_dtypes=(BF16,)))
    dy1 = run(_matmul, 25.0, dm1, w_out1, "nt", "mix1_out_dx")
    db, dcg, du, d_sc_conv_w = run(_short_bwd, 41.0, dy1, z1, sc_conv_w_full, "short_bwd")
    dz1 = jnp.concatenate([db, dcg, du], axis=1)
    reduce_grad(4, run(_matmul, 62.0, h2, dz1, "tn", "mix1_in_dw", out_dtypes=(BF16,)))
    dh = run(_matmul, 68.0, dz1, w_in1, "nt", "mix1_in_dx")
    dx, d_mix_pre_1, df0, d_ffn_post_0 = run(_norms_bwd, 35.0, dx, dh, x2, gain(mix_pre_g, 1), f0, gain(ffn_post_g, 0), "norms_bwd2")

    reduce_grad(3, run(_matmul, 80.0, a0sq, df0, "tn", "ffn0_down_dw", out_dtypes=(BF16,)))
    dz = run(_matmul, 82.0, df0, w2_0, "nt", "ffn0_down_dx", out_dtypes=(BF16,), epilogue=relu_sq_bwd, epi=(a0,))
    reduce_grad(2, run(_matmul, 80.0, h1, dz, "tn", "ffn0_up_dw", out_dtypes=(BF16,)))
    dh = run(_matmul, 87.0, dz, w1_0, "nt", "ffn0_up_dx")
    dx, d_ffn_pre_0, dm0, d_mix_post_0 = run(_norms_bwd, 36.0, dx, dh, x1, gain(ffn_pre_g, 0), m0, gain(mix_post_g, 0), "norms_bwd1")

    reduce_grad(1, run(_matmul, 24.0, y0, dm0, "tn", "mix0_out_dw", out_dtypes=(BF16,)))
    dy0 = run(_matmul, 25.0, dm0, w_out0, "nt", "mix0_out_dx")
    du_pool, d_pool_w, d_pool_scale = run(_pool_bwd, 28.0, dy0, pooled, pool_w_full, pool_scale, "pool_bwd")
    dc, d_ln_g, d_ln_b = run(_ln_silu_bwd, 15.0, dy0, c_conv, conv_ln_g, conv_ln_b, "ln_silu_bwd")
    dv, dgate, d_conv_w, d_conv_b = run(_conv_bwd, 52.0, dc, a_conv, z0, conv_w_full, DP, "conv_bwd")
    dz0 = jnp.concatenate([du_pool, dv, dgate], axis=1)

    device_slot = 2 * chip + lax.axis_index("c")
    small_sums = {}

    def exchange_small(key, arrays, cost):
        slots = lax.dynamic_update_slice_in_dim(jnp.zeros((N_DEVICES,) + _pack(arrays, D).shape, F32), _pack(arrays, D)[None],
                                                device_slot, axis=0)
        backlog.start(_small_exchange_job(slots, 1, [0]), "small_grads_" + key, "ici", cost,
                      lambda outs, kept: small_sums.__setitem__(key, _unpack(_sum_slots(outs[0], "small_grads_sum_" + key),
                                                                             [a.shape for a in arrays])))

    exchange_small("most", [d_mix_pre_1, jnp.concatenate([d_mix_post_0, d_mix_post_1], 0),
                            jnp.concatenate([d_ffn_pre_0, d_ffn_pre_1], 0), jnp.concatenate([d_ffn_post_0, d_ffn_post_1], 0),
                            d_pool_scale, d_conv_b, d_ln_g, d_ln_b, d_pool_w, d_conv_w, d_sc_conv_w], 112.0)
    reduce_grad(0, run(_matmul, 34.0, h0, dz0, "tn", "mix0_in_dw", out_dtypes=(BF16,)))
    dh = run(_matmul, 40.0, dz0, w_in0, "nt", "mix0_in_dx")
    grad_x, d_mix_pre_0 = run(_norms_bwd, 26.0, dx, dh, x0, gain(mix_pre_g, 0), None, None, "norms_bwd0")
    exchange_small("last", [d_mix_pre_0], 5.0)
    loss = lax.psum(loss_row[0, 0], ("x", "y", "c"))

    upd, gr, first = {}, grads_big, {}

    def keep(where, key, outs):
        where[key] = outs
        return outs

    adamw_big = [
        (7, lambda: keep(first, "ffn_w2", _adamw(ffn_w2, gr[7], m_ffn_w2, v_ffn_w2, "adamw_ffn_w2_1", layer=1)), 46.0),
        (6, lambda: keep(first, "ffn_w1", _adamw(ffn_w1, gr[6], m_ffn_w1, v_ffn_w1, "adamw_ffn_w1_1", layer=1)), 46.0),
        (5, lambda: keep(upd, "sc_w_out", _adamw(sc_w_out, gr[5], m_sc_w_out, v_sc_w_out, "adamw_sc_w_out")), 14.0),
        (4, lambda: keep(upd, "sc_w_in", _adamw(sc_w_in, gr[4], m_sc_w_in, v_sc_w_in, "adamw_sc_w_in")), 35.0),
        (3, lambda: keep(upd, "ffn_w2", _adamw(ffn_w2, gr[3], m_ffn_w2, v_ffn_w2, "adamw_ffn_w2_0", layer=0,
                                               carried=first["ffn_w2"])), 46.0),
        (2, lambda: keep(upd, "ffn_w1", _adamw(ffn_w1, gr[2], m_ffn_w1, v_ffn_w1, "adamw_ffn_w1_0", layer=0,
                                               carried=first["ffn_w1"])), 46.0),
        (1, lambda: keep(upd, "ab_w_out", _adamw(ab_w_out, gr[1], m_ab_w_out, v_ab_w_out, "adamw_ab_w_out")), 14.0),
        (0, lambda: keep(upd, "ab_w_in", _adamw(ab_w_in, gr[0], m_ab_w_in, v_ab_w_in, "adamw_ab_w_in")), 19.0)]
    while adamw_big or backlog.flights:
        due = [a for a in adamw_big if gr[a[0]] is not None]
        if due:
            adamw_big.remove(due[0])
            backlog.run(due[0][1], due[0][2])
        else:
            backlog.poll(block=True)

    (g_mix_pre_1, g_mix_post, g_ffn_pre, g_ffn_post, g_pool_scale, g_conv_b, g_ln_g, g_ln_b, g_pool_w_full, g_conv_w_full,
     g_sc_conv_w_full) = small_sums["most"]
    g_mix_pre = jnp.concatenate([small_sums["last"][0], g_mix_pre_1], 0)
    own = lambda a, ax: lax.dynamic_slice_in_dim(a, chip * (a.shape[ax] // N_CHIPS), a.shape[ax] // N_CHIPS, axis=ax)
    g_pool_w, g_conv_w, g_sc_conv_w = own(g_pool_w_full, 1), own(g_conv_w_full, 1), own(g_sc_conv_w_full, 1)

    def small_update(w, g, m, v, name):
        shape = w.shape
        as3 = lambda a: a.reshape((1, -1, shape[-1]))
        outs = _adamw(as3(w), g.reshape((-1, shape[-1])), as3(m), as3(v), "adamw_" + name)
        return [o.reshape(shape) for o in outs]

    upd["mix_pre_g"] = small_update(mix_pre_g, g_mix_pre, m_mix_pre_g, v_mix_pre_g, "mix_pre_g")
    upd["mix_post_g"] = small_update(mix_post_g, g_mix_post, m_mix_post_g, v_mix_post_g, "mix_post_g")
    upd["ffn_pre_g"] = small_update(ffn_pre_g, g_ffn_pre, m_ffn_pre_g, v_ffn_pre_g, "ffn_pre_g")
    upd["ffn_post_g"] = small_update(ffn_post_g, g_ffn_post, m_ffn_post_g, v_ffn_post_g, "ffn_post_g")
    upd["pool_w"] = small_update(pool_w, g_pool_w, m_pool_w, v_pool_w, "pool_w")
    upd["pool_scale"] = small_update(pool_scale, g_pool_scale, m_pool_scale, v_pool_scale, "pool_scale")
    upd["conv_w"] = small_update(conv_w, g_conv_w, m_conv_w, v_conv_w, "conv_w")
    upd["conv_b"] = small_update(conv_b, g_conv_b, m_conv_b, v_conv_b, "conv_b")
    upd["conv_ln_g"] = small_update(conv_ln_g, g_ln_g, m_conv_ln_g, v_conv_ln_g, "conv_ln_g")
    upd["conv_ln_b"] = small_update(conv_ln_b, g_ln_b, m_conv_ln_b, v_conv_ln_b, "conv_ln_b")
    upd["sc_conv_w"] = small_update(sc_conv_w, g_sc_conv_w, m_sc_conv_w, v_sc_conv_w, "sc_conv_w")

    order = ["mix_pre_g", "mix_post_g", "ffn_pre_g", "ffn_post_g", "ab_w_in", "pool_w", "pool_scale", "conv_w", "conv_b",
             "conv_ln_g", "conv_ln_b", "ab_w_out", "sc_w_in", "sc_conv_w", "sc_w_out", "ffn_w1", "ffn_w2"]
    out = [loss, grad_x[None]]
    for part in range(4):
        out += [upd[n][part] for n in order]
    return tuple(out)
```

```python
import jax
import jax.numpy as jnp
from jax import lax
from jax.experimental import pallas as pl
from jax.experimental.pallas import tpu as pltpu

F32, BF16 = jnp.float32, jnp.bfloat16
EPS = 1e-6
N_GROUPS = 4
MAX_WINDOW = 16
CONV_K = 31
SHORT_K = 3
CONV_PAD = 32
SHORT_PAD = 8
ADAM_LR, ADAM_B1, ADAM_B2, ADAM_EPS, ADAM_WD, ADAM_STEP = 0.001, 0.9, 0.999, 1e-08, 0.01, 10
N_CHIPS = 4
VMEM_LIMIT_BYTES = 56 * 1024 * 1024
ROW_TILE = 256
CHUNK = 256
LONG_K_TILE = 4096
MESH = pl.DeviceIdType.MESH
HBM = pl.BlockSpec(memory_space=pltpu.HBM)
SEM = pl.BlockSpec(memory_space=pltpu.SEMAPHORE)
ANY = pl.BlockSpec(memory_space=pl.ANY)


def _cp(*sem):
    return pltpu.CompilerParams(dimension_semantics=sem, vmem_limit_bytes=VMEM_LIMIT_BYTES)


def _sigmoid(v):
    return 1.0 / (1.0 + jnp.exp(-v))


class _Behind:
    pending = []


def _call(body, prefetch=None, **kw):
    behind, _Behind.pending = _Behind.pending, []
    single = not isinstance(kw["out_shape"], (list, tuple))
    in_specs, scratch = list(kw["in_specs"]), list(kw.get("scratch_shapes", ()))
    out_shape = [kw["out_shape"]] if single else list(kw["out_shape"])
    out_specs = [kw["out_specs"]] if single else list(kw["out_specs"])
    n_pre = 0 if prefetch is None else 1
    n_own, n_behind = len(in_specs), len(behind)

    def wrapped(*refs):
        body(*refs[:n_pre + n_own], *refs[n_pre + n_own + n_behind:])

    specs = dict(grid=kw["grid"], in_specs=in_specs + [ANY] * n_behind, out_specs=out_specs)
    if prefetch is None:
        specs["scratch_shapes"] = scratch
    else:
        specs = dict(grid_spec=pltpu.PrefetchScalarGridSpec(num_scalar_prefetch=1, scratch_shapes=scratch, **specs))
    aliases = {n_pre + i: o for i, o in kw.get("input_output_aliases", {}).items()}
    call = pl.pallas_call(wrapped, name=kw["name"], out_shape=out_shape, input_output_aliases=aliases,
                          compiler_params=_cp(*kw["sem"]), **specs)

    def run(*args):
        outs = call(*([prefetch] * n_pre), *args, *behind)
        return outs[0] if single else list(outs)

    return run


_DIMS = {"nn": (((1,), (0,)), ((), ())), "nt": (((1,), (1,)), ((), ())), "tn": (((0,), (0,)), ((), ()))}


def _pick(n, cap, step=256):
    if n <= cap:
        return n
    return next(t for t in range(cap - cap % step, 0, -step) if n % t == 0)


def _matmul(a, b, mode, name, out_dtypes=(F32,), epilogue=None, epi=(), tm=1024, tn=1024, tk=2048):
    if mode == "tn":
        (K, M), (K2, N) = a.shape, b.shape
    elif mode == "nt":
        (M, K), (N, K2) = a.shape, b.shape
    else:
        (M, K), (K2, N) = a.shape, b.shape
    assert K == K2
    tm, tn, tk = _pick(M, tm), _pick(N, tn), _pick(K, tk)
    nk = K // tk
    a_spec = pl.BlockSpec((tk, tm), lambda i, j, k: (k, i)) if mode == "tn" else pl.BlockSpec((tm, tk), lambda i, j, k: (i, k))
    b_spec = pl.BlockSpec((tn, tk), lambda i, j, k: (j, k)) if mode == "nt" else pl.BlockSpec((tk, tn), lambda i, j, k: (k, j))
    o_spec = pl.BlockSpec((tm, tn), lambda i, j, k: (i, j))
    n_epi, n_out = len(epi), len(out_dtypes)

    def body(a_ref, b_ref, *rest):
        epi_refs, out_refs, scratch = rest[:n_epi], rest[n_epi:n_epi + n_out], rest[n_epi + n_out:]
        part = lax.dot_general(a_ref[...].astype(BF16), b_ref[...].astype(BF16), _DIMS[mode], preferred_element_type=F32)

        def finish(acc):
            outs = epilogue(acc, *[r[...] for r in epi_refs]) if epilogue else (acc,)
            for o_ref, o in zip(out_refs, outs):
                o_ref[...] = o.astype(o_ref.dtype)

        if nk == 1:
            finish(part)
        else:
            acc_ref = scratch[0]
            k = pl.program_id(2)

            @pl.when(k == 0)
            def _():
                acc_ref[...] = part

            @pl.when(k > 0)
            def _():
                acc_ref[...] += part

            @pl.when(k == nk - 1)
            def _():
                finish(acc_ref[...])

    outs = _call(
        body, name=name, grid=(M // tm, N // tn, nk),
        in_specs=[a_spec, b_spec] + [o_spec] * n_epi, out_specs=[o_spec] * n_out,
        out_shape=[jax.ShapeDtypeStruct((M, N), dt) for dt in out_dtypes],
        scratch_shapes=[pltpu.VMEM((tm, tn), F32)] if nk > 1 else [],
        sem=("parallel", "parallel", "arbitrary"))(a, b, *epi)
    return outs[0] if n_out == 1 else outs


def _rms(x, g):
    r = lax.rsqrt(jnp.mean(x * x, axis=-1, keepdims=True) + EPS)
    return x * r * g


def _rms_bwd(dy, x, g):
    r = lax.rsqrt(jnp.mean(x * x, axis=-1, keepdims=True) + EPS)
    xn = x * r
    dyg = dy * g
    dx = r * (dyg - xn * jnp.mean(dyg * xn, axis=-1, keepdims=True))
    return dx, jnp.sum(dy * xn, axis=0, keepdims=True)


def _rows(d, tr=ROW_TILE):
    return pl.BlockSpec((tr, d), lambda i: (i, 0))


def _vec(d):
    return pl.BlockSpec((1, d), lambda i: (0, 0))


def _accumulate(ref, val):
    @pl.when(pl.program_id(0) == 0)
    def _():
        ref[...] = val

    @pl.when(pl.program_id(0) > 0)
    def _():
        ref[...] += val


def _norm_fwd(x, g, name):
    T, D = x.shape

    def body(x_ref, g_ref, h_ref):
        h_ref[...] = _rms(x_ref[...], g_ref[...]).astype(BF16)

    return _call(body,name=name,grid=(T // ROW_TILE,), in_specs=[_rows(D), _vec(D)], out_specs=_rows(D),
                          out_shape=jax.ShapeDtypeStruct((T, D), BF16), sem=("parallel",))(x, g)


def _residual_norm(x, m, g_post, g_next, name):
    T, D = x.shape

    def body(x_ref, m_ref, gp_ref, gn_ref, xo_ref, h_ref):
        xo = x_ref[...] + _rms(m_ref[...], gp_ref[...])
        xo_ref[...] = xo
        h_ref[...] = _rms(xo, gn_ref[...]).astype(BF16)

    return _call(body,name=name,grid=(T // ROW_TILE,), in_specs=[_rows(D), _rows(D), _vec(D), _vec(D)],
                          out_specs=[_rows(D), _rows(D)],
                          out_shape=[jax.ShapeDtypeStruct((T, D), F32), jax.ShapeDtypeStruct((T, D), BF16)],
                          sem=("parallel",))(x, m, g_post, g_next)


def _loss_and_last_norm_bwd(x, m, g_post, target, name):
    T, D = x.shape

    def body(x_ref, m_ref, gp_ref, t_ref, dx_ref, dm_ref, dg_ref, loss_ref):
        m_val, gp = m_ref[...], gp_ref[...]
        err = x_ref[...] + _rms(m_val, gp) - t_ref[...]
        dx = err * (1.0 / D)
        dx_ref[...] = dx
        dm, dg = _rms_bwd(dx, m_val, gp)
        dm_ref[...] = dm.astype(BF16)
        _accumulate(dg_ref, dg)
        _accumulate(loss_ref, jnp.full((1, 128), 0.5 * jnp.sum(err * err) * (1.0 / D), F32))

    return _call(body,name=name,grid=(T // ROW_TILE,), in_specs=[_rows(D), _rows(D), _vec(D), _rows(D)],
                          out_specs=[_rows(D), _rows(D), _vec(D), _vec(128)],
                          out_shape=[jax.ShapeDtypeStruct((T, D), F32), jax.ShapeDtypeStruct((T, D), BF16),
                                     jax.ShapeDtypeStruct((1, D), F32), jax.ShapeDtypeStruct((1, 128), F32)],
                          sem=("arbitrary",))(x, m, g_post, target)


def _norms_bwd(dx, dh, x_in, g_pre, m_prev, g_post_prev, name):
    T, D = dx.shape
    with_prev = m_prev is not None

    def body(*refs):
        if with_prev:
            dx_ref, dh_ref, x_ref, gq_ref, m_ref, gp_ref, dxo_ref, dgq_ref, dm_ref, dgp_ref = refs
        else:
            dx_ref, dh_ref, x_ref, gq_ref, dxo_ref, dgq_ref = refs
        d_in, dgq = _rms_bwd(dh_ref[...], x_ref[...], gq_ref[...])
        dxo = dx_ref[...] + d_in
        dxo_ref[...] = dxo
        _accumulate(dgq_ref, dgq)
        if with_prev:
            dm, dgp = _rms_bwd(dxo, m_ref[...], gp_ref[...])
            dm_ref[...] = dm.astype(BF16)
            _accumulate(dgp_ref, dgp)

    ins, in_specs = [dx, dh, x_in, g_pre], [_rows(D), _rows(D), _rows(D), _vec(D)]
    out_specs = [_rows(D), _vec(D)]
    out_shape = [jax.ShapeDtypeStruct((T, D), F32), jax.ShapeDtypeStruct((1, D), F32)]
    if with_prev:
        ins += [m_prev, g_post_prev]
        in_specs += [_rows(D), _vec(D)]
        out_specs += [_rows(D), _vec(D)]
        out_shape += [jax.ShapeDtypeStruct((T, D), BF16), jax.ShapeDtypeStruct((1, D), F32)]
    return _call(body,name=name,grid=(T // ROW_TILE,), in_specs=in_specs, out_specs=out_specs, out_shape=out_shape,
                          sem=("arbitrary",))(*ins)


def _window_weights(g):
    w = 2 << g
    return w, [jnp.where(j < w, 1.0, 0.0).astype(F32) for j in range(MAX_WINDOW)]


def _valid_count(r0, rows, w):
    t = (lax.broadcasted_iota(jnp.int32, (rows, 1), 0) + (r0 + 1)).astype(F32)
    return jnp.minimum(t, w.astype(F32))


def _pool_fwd(z, pool_w, pool_scale, name):
    T = z.shape[0]
    PG = pool_w.shape[-1]
    DP = N_GROUPS * PG
    rc = min(CHUNK, T)

    def body(u_ref, pw_ref, sc_ref, pooled_ref, y_ref, pad):
        w, wts = _window_weights(pl.program_id(0))
        pad[pl.ds(0, MAX_WINDOW), :] = jnp.zeros((MAX_WINDOW, PG), F32)
        pad[pl.ds(MAX_WINDOW, T), :] = u_ref[...]
        for r0 in range(0, T, rc):
            acc = jnp.zeros((rc, PG), F32)
            for j in range(MAX_WINDOW):
                acc = acc + wts[j] * pad[pl.ds(MAX_WINDOW + r0 - j, rc), :]
            pooled = acc / _valid_count(r0, rc, w) - u_ref[pl.ds(r0, rc), :]
            pooled_ref[pl.ds(r0, rc), :] = pooled.astype(BF16)
        mixed = jnp.dot(pooled_ref[...], pw_ref[...].astype(BF16), preferred_element_type=F32)
        y_ref[...] = (mixed * sc_ref[...]).astype(BF16)

    col = lambda g: (0, g)
    return _call(
        body, name=name,grid=(N_GROUPS,),
        in_specs=[pl.BlockSpec((T, PG), col), pl.BlockSpec((None, PG, PG), lambda g: (g, 0, 0)), pl.BlockSpec((1, PG), col)],
        out_specs=[pl.BlockSpec((T, PG), col), pl.BlockSpec((T, PG), col)],
        out_shape=[jax.ShapeDtypeStruct((T, DP), BF16), jax.ShapeDtypeStruct((T, DP), BF16)],
        scratch_shapes=[pltpu.VMEM((T + MAX_WINDOW, PG), F32)], sem=("parallel",))(z, pool_w, pool_scale)


def _pool_bwd(dy, pooled, pool_w, pool_scale, name):
    T = dy.shape[0]
    PG = pool_w.shape[-1]
    DP = N_GROUPS * PG
    rc = min(CHUNK, T)

    def body(dy_ref, pooled_ref, pw_ref, sc_ref, du_ref, dpw_ref, dsc_ref, pad, dp_ref):
        w, wts = _window_weights(pl.program_id(0))
        pooled_v, pw = pooled_ref[...], pw_ref[...].astype(BF16)
        dy_v = dy_ref[...]
        mixed = jnp.dot(pooled_v, pw, preferred_element_type=F32)
        dsc_ref[...] = jnp.sum(dy_v * mixed, axis=0, keepdims=True)
        dmixed = (dy_v * sc_ref[...]).astype(BF16)
        dpw_ref[...] = lax.dot_general(pooled_v, dmixed, _DIMS["tn"], preferred_element_type=F32)
        dp_ref[...] = lax.dot_general(dmixed, pw, _DIMS["nt"], preferred_element_type=F32)
        pad[pl.ds(T, MAX_WINDOW), :] = jnp.zeros((MAX_WINDOW, PG), F32)
        for r0 in range(0, T, rc):
            pad[pl.ds(r0, rc), :] = dp_ref[pl.ds(r0, rc), :] / _valid_count(r0, rc, w)
        for r0 in range(0, T, rc):
            acc = jnp.zeros((rc, PG), F32)
            for j in range(MAX_WINDOW):
                acc = acc + wts[j] * pad[pl.ds(r0 + j, rc), :]
            du_ref[pl.ds(r0, rc), :] = (acc - dp_ref[pl.ds(r0, rc), :]).astype(BF16)

    col = lambda g: (0, g)
    return _call(
        body, name=name,grid=(N_GROUPS,),
        in_specs=[pl.BlockSpec((T, PG), col), pl.BlockSpec((T, PG), col), pl.BlockSpec((None, PG, PG), lambda g: (g, 0, 0)),
                  pl.BlockSpec((1, PG), col)],
        out_specs=[pl.BlockSpec((T, PG), col), pl.BlockSpec((None, PG, PG), lambda g: (g, 0, 0)), pl.BlockSpec((1, PG), col)],
        out_shape=[jax.ShapeDtypeStruct((T, DP), BF16), jax.ShapeDtypeStruct((N_GROUPS, PG, PG), F32),
                   jax.ShapeDtypeStruct((1, DP), F32)],
        scratch_shapes=[pltpu.VMEM((T + MAX_WINDOW, PG), F32), pltpu.VMEM((T, PG), F32)],
        sem=("parallel",))(dy, pooled, pool_w, pool_scale)


def _conv_fwd(z, conv_w, conv_b, d_pool, name, tc=128):
    T = z.shape[0]
    DC = conv_w.shape[-1]
    rc = min(CHUNK, T)
    v0, g0 = d_pool // tc, (d_pool + DC) // tc

    def body(v_ref, gt_ref, w_ref, b_ref, a_ref, c_ref, pad):
        pad[pl.ds(0, CONV_PAD), :] = jnp.zeros((CONV_PAD, tc), F32)
        for r0 in range(0, T, rc):
            a = v_ref[pl.ds(r0, rc), :] * _sigmoid(gt_ref[pl.ds(r0, rc), :])
            a_ref[pl.ds(r0, rc), :] = a
            pad[pl.ds(CONV_PAD + r0, rc), :] = a
        for r0 in range(0, T, rc):
            acc = jnp.zeros((rc, tc), F32) + b_ref[...]
            for k in range(CONV_K):
                acc = acc + w_ref[pl.ds(k, 1), :] * pad[pl.ds(CONV_PAD - (CONV_K - 1) + k + r0, rc), :]
            c_ref[pl.ds(r0, rc), :] = acc

    col = lambda j: (0, j)
    return _call(
        body, name=name,grid=(DC // tc,),
        in_specs=[pl.BlockSpec((T, tc), lambda j: (0, v0 + j)), pl.BlockSpec((T, tc), lambda j: (0, g0 + j)),
                  pl.BlockSpec((CONV_K, tc), col), pl.BlockSpec((1, tc), col)],
        out_specs=[pl.BlockSpec((T, tc), col), pl.BlockSpec((T, tc), col)],
        out_shape=[jax.ShapeDtypeStruct((T, DC), F32), jax.ShapeDtypeStruct((T, DC), F32)],
        scratch_shapes=[pltpu.VMEM((T + CONV_PAD, tc), F32)], sem=("parallel",))(z, z, conv_w, conv_b)


def _conv_bwd(dc, a, z, conv_w, d_pool, name, tc=128):
    T, DC = dc.shape
    rc = min(CHUNK, T)
    v0, g0 = d_pool // tc, (d_pool + DC) // tc

    def body(dc_ref, a_ref, v_ref, gt_ref, w_ref, dv_ref, dg_ref, dw_ref, db_ref, apad, dpad):
        apad[pl.ds(0, CONV_PAD), :] = jnp.zeros((CONV_PAD, tc), F32)
        apad[pl.ds(CONV_PAD, T), :] = a_ref[...]
        dpad[pl.ds(0, T), :] = dc_ref[...]
        dpad[pl.ds(T, CONV_PAD), :] = jnp.zeros((CONV_PAD, tc), F32)
        db_ref[...] = jnp.sum(dc_ref[...], axis=0, keepdims=True)
        for k in range(CONV_K):
            acc = jnp.zeros((8, tc), F32)
            for r0 in range(0, T, rc):
                prod = dc_ref[pl.ds(r0, rc), :] * apad[pl.ds(CONV_PAD - (CONV_K - 1) + k + r0, rc), :]
                acc = acc + jnp.sum(prod.reshape(rc // 8, 8, tc), axis=0)
            dw_ref[pl.ds(k, 1), :] = jnp.sum(acc, axis=0, keepdims=True)
        for r0 in range(0, T, rc):
            da = jnp.zeros((rc, tc), F32)
            for k in range(CONV_K):
                da = da + w_ref[pl.ds(k, 1), :] * dpad[pl.ds(r0 + (CONV_K - 1) - k, rc), :]
            sig = _sigmoid(gt_ref[pl.ds(r0, rc), :])
            dv_ref[pl.ds(r0, rc), :] = (da * sig).astype(BF16)
            dg_ref[pl.ds(r0, rc), :] = (da * v_ref[pl.ds(r0, rc), :] * sig * (1.0 - sig)).astype(BF16)

    col = lambda j: (0, j)
    return _call(
        body, name=name,grid=(DC // tc,),
        in_specs=[pl.BlockSpec((T, tc), col), pl.BlockSpec((T, tc), col), pl.BlockSpec((T, tc), lambda j: (0, v0 + j)),
                  pl.BlockSpec((T, tc), lambda j: (0, g0 + j)), pl.BlockSpec((CONV_K, tc), col)],
        out_specs=[pl.BlockSpec((T, tc), col), pl.BlockSpec((T, tc), col), pl.BlockSpec((CONV_K, tc), col),
                   pl.BlockSpec((1, tc), col)],
        out_shape=[jax.ShapeDtypeStruct((T, DC), BF16), jax.ShapeDtypeStruct((T, DC), BF16),
                   jax.ShapeDtypeStruct((CONV_K, DC), F32), jax.ShapeDtypeStruct((1, DC), F32)],
        scratch_shapes=[pltpu.VMEM((T + CONV_PAD, tc), F32), pltpu.VMEM((T + CONV_PAD, tc), F32)],
        sem=("parallel",))(dc, a, z, z, conv_w)


def _layer_norm_parts(c, g, b):
    mu = jnp.mean(c, axis=-1, keepdims=True)
    xc = c - mu
    rstd = lax.rsqrt(jnp.mean(xc * xc, axis=-1, keepdims=True) + EPS)
    xhat = xc * rstd
    return xhat, rstd, xhat * g + b


def _ln_silu_fwd(c, g, b, name):
    T, DC = c.shape

    def body(c_ref, g_ref, b_ref, y_ref):
        _, _, ln = _layer_norm_parts(c_ref[...], g_ref[...], b_ref[...])
        y_ref[...] = (ln * _sigmoid(ln)).astype(BF16)

    return _call(body,name=name,grid=(T // ROW_TILE,), in_specs=[_rows(DC), _vec(DC), _vec(DC)], out_specs=_rows(DC),
                          out_shape=jax.ShapeDtypeStruct((T, DC), BF16), sem=("parallel",))(c, g, b)


def _ln_silu_bwd(dy, c, g, b, name):
    T, DC = c.shape

    def body(dy_ref, c_ref, g_ref, b_ref, dc_ref, dg_ref, db_ref):
        gain = g_ref[...]
        xhat, rstd, ln = _layer_norm_parts(c_ref[...], gain, b_ref[...])
        s = _sigmoid(ln)
        dln = dy_ref[...] * (s * (1.0 + ln * (1.0 - s)))
        _accumulate(dg_ref, jnp.sum(dln * xhat, axis=0, keepdims=True))
        _accumulate(db_ref, jnp.sum(dln, axis=0, keepdims=True))
        dxh = dln * gain
        dc_ref[...] = rstd * (dxh - jnp.mean(dxh, axis=-1, keepdims=True) - xhat * jnp.mean(dxh * xhat, axis=-1, keepdims=True))

    return _call(body,name=name,grid=(T // ROW_TILE,),
                          in_specs=[pl.BlockSpec((ROW_TILE, DC), lambda i: (i, 1)), _rows(DC), _vec(DC), _vec(DC)],
                          out_specs=[_rows(DC), _vec(DC), _vec(DC)],
                          out_shape=[jax.ShapeDtypeStruct((T, DC), F32), jax.ShapeDtypeStruct((1, DC), F32),
                                     jax.ShapeDtypeStruct((1, DC), F32)],
                          sem=("arbitrary",))(dy, c, g, b)


def _short_specs(T, DS, tc):
    n = DS // tc
    return [pl.BlockSpec((T, tc), lambda j: (0, j)), pl.BlockSpec((T, tc), lambda j: (0, n + j)),
            pl.BlockSpec((T, tc), lambda j: (0, 2 * n + j))]


def _short_fwd(z, w, name, tc=256):
    T = z.shape[0]
    DS = w.shape[-1]
    rc = min(CHUNK, T)

    def body(b_ref, cg_ref, u_ref, w_ref, y_ref, pad):
        pad[pl.ds(0, SHORT_PAD), :] = jnp.zeros((SHORT_PAD, tc), F32)
        pad[pl.ds(SHORT_PAD, T), :] = cg_ref[...] * u_ref[...]
        for r0 in range(0, T, rc):
            r = jnp.zeros((rc, tc), F32)
            for k in range(SHORT_K):
                r = r + w_ref[pl.ds(k, 1), :] * pad[pl.ds(SHORT_PAD - (SHORT_K - 1) + k + r0, rc), :]
            y_ref[pl.ds(r0, rc), :] = (b_ref[pl.ds(r0, rc), :] * r).astype(BF16)

    col = lambda j: (0, j)
    return _call(body,name=name,grid=(DS // tc,), in_specs=_short_specs(T, DS, tc) + [pl.BlockSpec((SHORT_K, tc), col)],
                          out_specs=pl.BlockSpec((T, tc), col), out_shape=jax.ShapeDtypeStruct((T, DS), BF16),
                          scratch_shapes=[pltpu.VMEM((T + SHORT_PAD, tc), F32)], sem=("parallel",))(z, z, z, w)


def _short_bwd(dy, z, w, name, tc=256):
    T, DS = dy.shape
    rc = min(CHUNK, T)

    def body(dy_ref, b_ref, cg_ref, u_ref, w_ref, db_ref, dcg_ref, du_ref, dw_ref, qpad, rpad):
        qpad[pl.ds(0, SHORT_PAD), :] = jnp.zeros((SHORT_PAD, tc), F32)
        qpad[pl.ds(SHORT_PAD, T), :] = cg_ref[...] * u_ref[...]
        rpad[pl.ds(0, T), :] = dy_ref[...] * b_ref[...]
        rpad[pl.ds(T, SHORT_PAD), :] = jnp.zeros((SHORT_PAD, tc), F32)
        accs = [jnp.zeros((8, tc), F32) for _ in range(SHORT_K)]
        for r0 in range(0, T, rc):
            r = jnp.zeros((rc, tc), F32)
            dq = jnp.zeros((rc, tc), F32)
            dr = rpad[pl.ds(r0, rc), :]
            for k in range(SHORT_K):
                q_k = qpad[pl.ds(SHORT_PAD - (SHORT_K - 1) + k + r0, rc), :]
                r = r + w_ref[pl.ds(k, 1), :] * q_k
                dq = dq + w_ref[pl.ds(k, 1), :] * rpad[pl.ds(r0 + (SHORT_K - 1) - k, rc), :]
                accs[k] = accs[k] + jnp.sum((dr * q_k).reshape(rc // 8, 8, tc), axis=0)
            db_ref[pl.ds(r0, rc), :] = (dy_ref[pl.ds(r0, rc), :] * r).astype(BF16)
            dcg_ref[pl.ds(r0, rc), :] = (dq * u_ref[pl.ds(r0, rc), :]).astype(BF16)
            du_ref[pl.ds(r0, rc), :] = (dq * cg_ref[pl.ds(r0, rc), :]).astype(BF16)
        for k in range(SHORT_K):
            dw_ref[pl.ds(k, 1), :] = jnp.sum(accs[k], axis=0, keepdims=True)

    col = lambda j: (0, j)
    tile = pl.BlockSpec((T, tc), col)
    return _call(body,name=name,grid=(DS // tc,),
                          in_specs=[tile] + _short_specs(T, DS, tc) + [pl.BlockSpec((SHORT_K, tc), col)],
                          out_specs=[tile, tile, tile, pl.BlockSpec((SHORT_K, tc), col)],
                          out_shape=[jax.ShapeDtypeStruct((T, DS), BF16)] * 3 + [jax.ShapeDtypeStruct((SHORT_K, DS), F32)],
                          scratch_shapes=[pltpu.VMEM((T + SHORT_PAD, tc), F32), pltpu.VMEM((T + SHORT_PAD, tc), F32)],
                          sem=("parallel",))(dy, z, z, z, w)


def _tile_rows(rows, cols, n_bufs):
    budget = VMEM_LIMIT_BYTES * 3 // 4 // (2 * n_bufs * 4 * cols)
    tr = rows
    while tr > budget and tr % 16 == 0:
        tr //= 2
    return tr


def _placed_call(body, name, place, grid, in_specs, out_specs, out_shape, ins):
    return _call(body,prefetch=place, name=name, grid=grid, in_specs=in_specs, out_specs=out_specs, out_shape=out_shape,
                 sem=("parallel",))(*ins)


def _cast_into_full(w, layer, kind, place, name):
    _, R, C = w.shape
    tr = _tile_rows(R, C, 2)
    nb = R // tr
    if kind == "col":
        full, out_spec = (R, C * N_CHIPS), pl.BlockSpec((tr, C), lambda i, s: (i, s[0]))
    else:
        full, out_spec = (R * N_CHIPS, C), pl.BlockSpec((tr, C), lambda i, s: (s[0] * nb + i, 0))

    def body(s_ref, w_ref, o_ref):
        o_ref[...] = w_ref[...].astype(BF16)

    return _placed_call(body, name, place, (nb,), [pl.BlockSpec((None, tr, C), lambda i, s: (layer, i, 0))], out_spec,
                        jax.ShapeDtypeStruct(full, BF16), [w])


def _add_pair(grad, theirs, kind, place, name):
    R, C = grad.shape
    piece_rows = R // 2 if kind == "col" else R // N_CHIPS // 2
    tr = _tile_rows(piece_rows, C, 3)
    nb = piece_rows // tr
    if kind == "col":
        g_spec = pl.BlockSpec((tr, C), lambda i, s: (s[1] * nb + i, 0))
    else:
        g_spec = pl.BlockSpec((tr, C), lambda i, s: ((2 * (i // nb) + s[1]) * nb + i % nb, 0))
    flat = pl.BlockSpec((tr, C), lambda i, s: (i, 0))

    def body(s_ref, a_ref, b_ref, o_ref):
        o_ref[...] = (a_ref[...].astype(F32) + b_ref[...].astype(F32)).astype(BF16)

    return _placed_call(body, name, place, (R // 2 // tr,), [g_spec, flat], flat, jax.ShapeDtypeStruct((R // 2, C), BF16),
                        [grad, theirs])


def _sum_chips(chip_sum, arrived, kind, place, name):
    _, H, W = arrived.shape
    tr = _tile_rows(H, W, 6)
    nb = H // tr
    if kind == "col":
        own_spec = pl.BlockSpec((tr, W), lambda i, s: (i, s[0]))
    else:
        own_spec = pl.BlockSpec((tr, W), lambda i, s: (s[0] * nb + i, 0))

    def body(s_ref, p_ref, r_ref, o_ref):
        acc = p_ref[...].astype(F32)
        for i in range(N_CHIPS - 1):
            acc = acc + r_ref[i].astype(F32)
        o_ref[...] = acc

    return _placed_call(body, name, place, (nb,), [own_spec, pl.BlockSpec((N_CHIPS - 1, tr, W), lambda i, s: (0, i, 0))],
                        pl.BlockSpec((tr, W), lambda i, s: (s[1] * nb + i, 0)), jax.ShapeDtypeStruct((2 * H, W), F32),
                        [chip_sum, arrived])


def _adamw_values(w, g, m, v):
    m = ADAM_B1 * m + (1.0 - ADAM_B1) * g
    v = ADAM_B2 * v + (1.0 - ADAM_B2) * (g * g)
    m_hat = m / (1.0 - ADAM_B1 ** ADAM_STEP)
    v_hat = v / (1.0 - ADAM_B2 ** ADAM_STEP)
    return -ADAM_LR * (m_hat / (jnp.sqrt(v_hat) + ADAM_EPS) + ADAM_WD * w), m, v


def _adamw(w, g, m, v, name, layer=0, carried=None):
    L, R, C = w.shape
    tr = _tile_rows(R, C, 8)

    def body(w_ref, g_ref, m_ref, v_ref, *rest):
        go_ref, d_ref, mo_ref, vo_ref = rest[-4:]
        g_val = g_ref[...]
        d, m_new, v_new = _adamw_values(w_ref[...], g_val, m_ref[...], v_ref[...])
        go_ref[...], d_ref[...], mo_ref[...], vo_ref[...] = g_val, d, m_new, v_new

    lay = pl.BlockSpec((None, tr, C), lambda i: (layer, i, 0))
    ins = [w, g, m, v]
    in_specs = [lay, pl.BlockSpec((tr, C), lambda i: (i, 0)), lay, lay]
    aliases = {}
    if carried is not None:
        ins += list(carried)
        in_specs += [pl.BlockSpec(memory_space=pl.ANY)] * 4
        aliases = {4 + i: i for i in range(4)}
    return _call(body, name=name, grid=(R // tr,), in_specs=in_specs, out_specs=[lay] * 4,
                 out_shape=[jax.ShapeDtypeStruct((L, R, C), F32)] * 4, input_output_aliases=aliases, sem=("parallel",))(*ins)


def _aligned(v, m):
    return v if isinstance(v, int) else pl.multiple_of(v, m)


def _place():
    x, y, c = lax.axis_index("x"), lax.axis_index("y"), lax.axis_index("c")
    other_chips = [(x, 1 - y), (1 - x, y), (1 - x, 1 - y)]
    return x, y, c, 2 * x + y, other_chips


def _chip_index(chip):
    return 2 * chip[0] + chip[1]


def _piece(ref, kind, k, h):
    R, C = ref.shape
    if kind == "col":
        return ref.at[pl.ds(_aligned(h * (R // 2), 16), R // 2), pl.ds(_aligned(k * (C // N_CHIPS), 128), C // N_CHIPS)]
    rs = R // N_CHIPS
    return ref.at[pl.ds(_aligned(k * rs + h * (rs // 2), 16), rs // 2), :]


def _compact_piece(ref, kind, k):
    R2, C = ref.shape
    if kind == "col":
        return ref.at[:, pl.ds(_aligned(k * (C // N_CHIPS), 128), C // N_CHIPS)]
    return ref.at[pl.ds(_aligned(k * (R2 // N_CHIPS), 16), R2 // N_CHIPS), :]


def _half_rows(ref, h):
    R = ref.shape[0]
    return ref.at[pl.ds(_aligned(h * (R // 2), 16), R // 2), :]


class _Copies:
    def __init__(self, send_sems, recv_sems):
        self.send_sems, self.recv_sems = send_sems, recv_sems
        self.n_remote = 0

    def remote(self, src, dst, device):
        k = self.n_remote
        self.n_remote += 1
        return pltpu.make_async_remote_copy(src_ref=src, dst_ref=dst, send_sem=self.send_sems.at[k], recv_sem=self.recv_sems.at[k],
                                            device_id=device, device_id_type=MESH)


class _Job:
    def __init__(self, ins, out_shape, aliases, n_remote, build):
        self.ins, self.out_shape, self.aliases, self.n_remote, self.build = list(ins), list(out_shape), dict(aliases), n_remote, build


class _Flying:
    def __init__(self, job, send_sems, recv_sems, bufs, token):
        self.job, self.send_sems, self.recv_sems, self.bufs, self.token = job, send_sems, recv_sems, bufs, token


def _job_refs(job, buf_refs):
    n_out = len(job.out_shape)
    kept = [i for i in range(len(job.ins)) if i not in job.aliases]
    ins = [buf_refs[job.aliases[i]] if i in job.aliases else buf_refs[n_out + kept.index(i)] for i in range(len(job.ins))]
    return ins, list(buf_refs[:n_out])


def _start_job(job, name, after=()):
    n_in, n_out, n_after = len(job.ins), len(job.out_shape), len(after)
    kept = [i for i in range(n_in) if i not in job.aliases]
    n_bufs = n_out + len(kept)

    def body(*refs):
        in_refs, out_refs = refs[:n_in], refs[n_in + n_after:n_in + n_after + n_out]
        send_sems, recv_sems, token = refs[n_in + n_after + n_bufs:]
        for d in job.build(in_refs, out_refs, _Copies(send_sems, recv_sems)):
            d.start()
        token[...] = jnp.zeros_like(token)

    aliases = dict(job.aliases)
    aliases.update({i: n_out + k for k, i in enumerate(kept)})
    sems = pltpu.SemaphoreType.DMA((job.n_remote,))
    outs = pl.pallas_call(
        body, name=name, in_specs=[HBM] * n_in + [ANY] * n_after,
        out_specs=[HBM] * n_bufs + [SEM, SEM, pl.BlockSpec(memory_space=pltpu.VMEM)],
        out_shape=job.out_shape + [jax.ShapeDtypeStruct(job.ins[i].shape, job.ins[i].dtype) for i in kept]
        + [sems, sems, jax.ShapeDtypeStruct((8, 128), F32)],
        input_output_aliases=aliases,
        compiler_params=pltpu.CompilerParams(has_side_effects=pltpu.SideEffectType.DATAFLOW_SIDE_EFFECTING))(*job.ins, *after)
    return _Flying(job, outs[n_bufs], outs[n_bufs + 1], list(outs[:n_bufs]), outs[n_bufs + 2])


def _wait_job(flying, name, after=()):
    job, n_bufs, n_after = flying.job, len(flying.bufs), len(after)

    def body(*refs):
        in_refs, out_refs = _job_refs(job, refs[:n_bufs])
        send_sems, recv_sems = refs[n_bufs:n_bufs + 2]
        copies = job.build(in_refs, out_refs, _Copies(send_sems, recv_sems))
        for d in copies:
            d.wait_send()
        for d in copies:
            d.wait_recv()

    outs = pl.pallas_call(
        body, name=name, in_specs=[HBM] * n_bufs + [SEM, SEM] + [ANY] * n_after, out_specs=[HBM] * n_bufs,
        out_shape=[jax.ShapeDtypeStruct(b.shape, b.dtype) for b in flying.bufs],
        input_output_aliases={i: i for i in range(n_bufs)},
        compiler_params=pltpu.CompilerParams(has_side_effects=pltpu.SideEffectType.DATAFLOW_SIDE_EFFECTING))(
            *flying.bufs, flying.send_sems, flying.recv_sems, *after)
    return list(outs[:len(job.out_shape)]), list(outs[len(job.out_shape):])


def _in_place(arrays):
    return [jax.ShapeDtypeStruct(a.shape, a.dtype) for a in arrays], {u: u for u in range(len(arrays))}


def _rows_part(ref, part, n_parts):
    h = ref.shape[0] // n_parts
    return ref.at[pl.ds(part * h, h), :]


def _gather_job(full, kind, stage):
    def build(in_refs, out_refs, cp):
        x, y, c, me, (y_nbr, x_nbr, diagonal) = _place()
        (ref,) = out_refs
        sibling = (x, y, 1 - c)
        if stage == 1:
            mine = _piece(ref, kind, me, c)
            return [cp.remote(mine, mine, (*y_nbr, c)), cp.remote(mine, mine, (*x_nbr, c))]
        from_y, from_x = _piece(ref, kind, _chip_index(y_nbr), c), _piece(ref, kind, _chip_index(x_nbr), c)
        if stage == 2:
            relay_0, relay_1 = _rows_part(from_x, 0, 2), _rows_part(from_y, 1, 2)
            return [cp.remote(relay_0, relay_0, (*y_nbr, c)), cp.remote(relay_1, relay_1, (*x_nbr, c)),
                    cp.remote(from_y, from_y, sibling), cp.remote(from_x, from_x, sibling)]
        from_diagonal = _piece(ref, kind, _chip_index(diagonal), c)
        return [cp.remote(from_diagonal, from_diagonal, sibling)]

    return _Job([full], *_in_place([full]), {1: 2, 2: 4, 3: 1}[stage], build)


def _gather_small_job(fulls, axes):
    def build(in_refs, out_refs, cp):
        x, y, c, me, chips = _place()
        copies = []
        for ref, ax in zip(out_refs, axes):
            n = ref.shape[ax] // N_CHIPS
            idx = [slice(None)] * len(ref.shape)
            idx[ax] = pl.ds(_aligned(me * n, n), n)
            mine = ref.at[tuple(idx)]
            copies += [cp.remote(mine, mine, (*chip, c)) for chip in chips]
        return copies

    return _Job(fulls, *_in_place(fulls), 3 * len(fulls), build)


def _exchange_halves_job(grads, kinds):
    def build(in_refs, out_refs, cp):
        x, y, c, me, chips = _place()
        copies = []
        for src, dst, kind in zip(in_refs, out_refs, kinds):
            if kind == "col":
                copies.append(cp.remote(_half_rows(src, 1 - c), dst, (x, y, 1 - c)))
            else:
                copies += [cp.remote(_piece(src, "row", k, 1 - c), _compact_piece(dst, "row", k), (x, y, 1 - c))
                           for k in range(N_CHIPS)]
        return copies

    out_shape = [jax.ShapeDtypeStruct((g.shape[0] // 2, g.shape[1]), g.dtype) for g in grads]
    return _Job(grads, out_shape, {}, sum(1 if k == "col" else N_CHIPS for k in kinds), build)


def _scatter_job(half, kind, n_parts, parts, into=None):
    def build(in_refs, out_refs, cp):
        x, y, c, me, chips = _place()
        src, (dst,) = in_refs[0], out_refs
        copies = []
        for p in parts:
            copies += [cp.remote(_rows_part(_compact_piece(src, kind, _chip_index(chip)), p, n_parts),
                                 _rows_part(dst.at[r], p, n_parts), (*chip, c)) for r, chip in enumerate(chips)]
        return copies

    part_shape = (half.shape[0], half.shape[1] // N_CHIPS) if kind == "col" else (half.shape[0] // N_CHIPS, half.shape[1])
    out_shape = [jax.ShapeDtypeStruct((N_CHIPS - 1,) + part_shape, half.dtype)]
    ins, aliases = ([half], {}) if into is None else ([half, into], {1: 0})
    return _Job(ins, out_shape, aliases, 3 * len(parts), build)


def _share_job(shards):
    def build(in_refs, out_refs, cp):
        x, y, c, me, chips = _place()
        copies = []
        for ref in out_refs:
            mine = _half_rows(ref, c)
            copies.append(cp.remote(mine, mine, (x, y, 1 - c)))
        return copies

    return _Job(shards, *_in_place(shards), len(shards), build)


N_DEVICES = 2 * N_CHIPS


def _small_exchange_job(slots, n_parts, parts):
    def build(in_refs, out_refs, cp):
        x, y, c, me, chips = _place()
        (ref,) = out_refs
        copies = []
        for part in parts:
            mine = _rows_part(ref.at[2 * me + c], part, n_parts)
            copies += [cp.remote(mine, mine, (x ^ (p >> 2), y ^ ((p >> 1) & 1), c ^ (p & 1))) for p in range(1, N_DEVICES)]
        return copies

    return _Job([slots], *_in_place([slots]), (N_DEVICES - 1) * len(parts), build)


def _sum_slots(slots, name):
    n, R, C = slots.shape

    def body(s_ref, o_ref):
        acc = s_ref[0]
        for i in range(1, n):
            acc = acc + s_ref[i]
        o_ref[...] = acc

    return _call(body,name=name, grid=(1,), in_specs=[pl.BlockSpec((n, R, C), lambda i: (0, 0, 0))],
                 out_specs=pl.BlockSpec((R, C), lambda i: (0, 0)), out_shape=jax.ShapeDtypeStruct((R, C), F32),
                 sem=("arbitrary",))(slots)


def _pack(arrays, width):
    rows = []
    for a in arrays:
        flat = a.reshape(-1)
        n_rows = -(-flat.shape[0] // width)
        rows.append(jnp.pad(flat, (0, n_rows * width - flat.shape[0])).reshape(n_rows, width))
    n = sum(r.shape[0] for r in rows)
    rows.append(jnp.zeros((-n % 8, width), F32))
    return jnp.concatenate(rows, axis=0)


def _unpack(packed, shapes):
    out, r0, width = [], 0, packed.shape[1]
    for shape in shapes:
        size = 1
        for d in shape:
            size *= d
        n_rows = -(-size // width)
        out.append(packed[r0:r0 + n_rows].reshape(-1)[:size].reshape(shape))
        r0 += n_rows
    return out


class _Backlog:
    def __init__(self):
        self.now, self.free, self.flights, self.last = 0.0, {"ici": 0.0, "d2d": 0.0}, [], None

    def run(self, fn, us, *args, **kw):
        out = fn(*args, **kw)
        self.now += us
        self.last = out[0] if isinstance(out, (list, tuple)) else out
        self.poll()
        return out

    def start(self, job, name, link, cost, done):
        flying = _start_job(job, name + "_start")
        _Behind.pending.append(flying.token)
        ends = max(self.now, self.free[link]) + cost
        self.free[link] = ends
        self.flights.append((ends + LANDING_SLACK_US, name, flying, done))
        self.flights.sort(key=lambda f: f[0])

    def poll(self, block=False):
        while self.flights and (block or self.flights[0][0] <= self.now):
            ends, name, flying, done = self.flights.pop(0)
            self.now, block = max(self.now, ends), False
            done(*_wait_job(flying, name + "_wait", [self.last]))


class _GradStream:
    def __init__(self, backlog, u, name, kind, cost, g, place, results):
        self.backlog, self.u, self.name, self.kind, self.cost, self.place, self.results = backlog, u, name, kind, cost, place, results
        backlog.start(_exchange_halves_job([g], [kind]), "to_sibling_" + name, "d2d", D2D_SHARE * cost, self.exchanged)

    def exchanged(self, outs, kept):
        chip_sum = self.backlog.run(_add_pair, SIDE_KERNEL_US, kept[0], outs[0], self.kind, self.place, "chip_sum_" + self.name)
        self.backlog.start(_scatter_job(chip_sum, self.kind, 1, [0]), "to_owners_" + self.name, "ici", self.cost, self.scattered)

    def scattered(self, outs, kept):
        reduced = self.backlog.run(_sum_chips, SIDE_KERNEL_US, kept[0], outs[0], self.kind, self.place, "reduce_" + self.name)
        self.backlog.start(_share_job([reduced]), "share_" + self.name, "d2d", D2D_SHARE * self.cost, self.shared)

    def shared(self, outs, kept):
        self.results[self.u] = outs[0]


SIDE_KERNEL_US = 12.0
D2D_SHARE = 0.15
LANDING_SLACK_US = 5.0


def kernel(x, mix_pre_g, mix_post_g, ffn_pre_g, ffn_post_g, ab_w_in, pool_w, pool_scale, conv_w, conv_b, conv_ln_g, conv_ln_b, ab_w_out, sc_w_in, sc_conv_w, sc_w_out, ffn_w1, ffn_w2, loss_target, m_mix_pre_g, m_mix_post_g, m_ffn_pre_g, m_ffn_post_g, m_ab_w_in, m_pool_w, m_pool_scale, m_conv_w, m_conv_b, m_conv_ln_g, m_conv_ln_b, m_ab_w_out, m_sc_w_in, m_sc_conv_w, m_sc_w_out, m_ffn_w1, m_ffn_w2, v_mix_pre_g, v_mix_post_g, v_ffn_pre_g, v_ffn_post_g, v_ab_w_in, v_pool_w, v_pool_scale, v_conv_w, v_conv_b, v_conv_ln_g, v_conv_ln_b, v_ab_w_out, v_sc_w_in, v_sc_conv_w, v_sc_w_out, v_ffn_w1, v_ffn_w2):
    x0, target = x[0], loss_target[0]
    T, D = x0.shape
    DP = pool_scale.shape[-1]
    gain = lambda g, layer: g[layer][None, :]

    big = [("ab_w_in", ab_w_in, 0, "col", 4, 67.0), ("ab_w_out", ab_w_out, 0, "row", 2, 44.0),
           ("ffn_w1_0", ffn_w1, 0, "col", 8, 177.0), ("ffn_w2_0", ffn_w2, 0, "row", 8, 177.0),
           ("sc_w_in", sc_w_in, 0, "col", 4, 133.0), ("sc_w_out", sc_w_out, 0, "row", 2, 44.0),
           ("ffn_w1_1", ffn_w1, 1, "col", 8, 177.0), ("ffn_w2_1", ffn_w2, 1, "row", 8, 177.0)]
    kinds = [b[3] for b in big]
    chip = 2 * lax.axis_index("x") + lax.axis_index("y")
    place = jnp.stack([chip, lax.axis_index("c")]).astype(jnp.int32)

    def own_in_zeros(shard, ax):
        full = jnp.zeros(tuple(d * N_CHIPS if i == ax else d for i, d in enumerate(shard.shape)), shard.dtype)
        return lax.dynamic_update_slice_in_dim(full, shard, chip * shard.shape[ax], axis=ax)

    W = [_cast_into_full(w, layer, kind, place, "cast_" + name) for name, w, layer, kind, _, _ in big]
    smalls = [own_in_zeros(pool_w[0], 1), own_in_zeros(conv_w[0], 1), own_in_zeros(sc_conv_w[0], 1)]
    flying = {}

    def start_gather(u, after=()):
        if u < len(big):
            flying[u] = _start_job(_gather_job(W[u], kinds[u], 1), "gather_start_%d" % u, after)

    first_use = [1, 3, 5, 6, 8, 10, 12, 13]
    relays, onwards = {}, {}

    def gather_steps(slot, after):
        for u in range(len(big)):
            if max(first_use[u] - 2, 0) == slot:
                (W[u],), _ = _wait_job(flying[u], "gather_wait_%d" % u, [after])
                relays[u] = _start_job(_gather_job(W[u], kinds[u], 2), "relay_start_%d" % u)
                start_gather(u + 2, [relays[u].token])
                _Behind.pending += [relays[u].token] + ([flying[u + 2].token] if u + 2 < len(big) else [])
        for u in range(len(big)):
            if max(first_use[u] - 1, 0) == slot:
                (W[u],), _ = _wait_job(relays[u], "relay_wait_%d" % u, [after])
                onwards[u] = _start_job(_gather_job(W[u], kinds[u], 3), "forward_start_%d" % u)
                _Behind.pending.append(onwards[u].token)
        for u in range(len(big)):
            if first_use[u] == slot:
                (W[u],), _ = _wait_job(onwards[u], "forward_wait_%d" % u, [after])

    small_flight = _start_job(_gather_small_job(smalls, [1, 1, 1]), "gather_start_small")
    start_gather(0, [small_flight.token])
    start_gather(1, [flying[0].token])

    relu_sq = lambda acc: (jnp.maximum(acc, 0.0), jnp.square(jnp.maximum(acc, 0.0)))
    relu_sq_bwd = lambda acc, a: (acc * (2.0 * a.astype(F32)),)

    _Behind.pending = [flying[1].token]
    h0 = _norm_fwd(x0, gain(mix_pre_g, 0), "norm_in")
    (pool_w_full, conv_w_full, sc_conv_w_full), _ = _wait_job(small_flight, "gather_wait_small", [h0])
    gather_steps(0, h0)
    gather_steps(1, h0)
    z0 = _matmul(h0, W[0], "nn", "mix0_in")
    gather_steps(2, z0)
    pooled, y_pool = _pool_fwd(z0, pool_w_full, pool_scale, "pool_fwd")
    a_conv, c_conv = _conv_fwd(z0, conv_w_full, conv_b, DP, "conv_fwd")
    y_conv = _ln_silu_fwd(c_conv, conv_ln_g, conv_ln_b, "ln_silu_fwd")
    y0 = jnp.concatenate([y_pool, y_conv], axis=1)
    gather_steps(3, y0)
    m0 = _matmul(y0, W[1], "nn", "mix0_out")
    gather_steps(4, m0)
    x1, h1 = _residual_norm(x0, m0, gain(mix_post_g, 0), gain(ffn_pre_g, 0), "res_mix0")
    gather_steps(5, h1)
    a0, a0sq = _matmul(h1, W[2], "nn", "ffn0_up", out_dtypes=(BF16, BF16), epilogue=relu_sq)
    gather_steps(6, a0sq)
    f0 = _matmul(a0sq, W[3], "nn", "ffn0_down", tk=LONG_K_TILE)
    gather_steps(7, f0)
    x2, h2 = _residual_norm(x1, f0, gain(ffn_post_g, 0), gain(mix_pre_g, 1), "res_ffn0")
    gather_steps(8, h2)
    z1 = _matmul(h2, W[4], "nn", "mix1_in")
    gather_steps(9, z1)
    y1 = _short_fwd(z1, sc_conv_w_full, "short_fwd")
    gather_steps(10, y1)
    m1 = _matmul(y1, W[5], "nn", "mix1_out")
    gather_steps(11, m1)
    x3, h3 = _residual_norm(x2, m1, gain(mix_post_g, 1), gain(ffn_pre_g, 1), "res_mix1")
    gather_steps(12, h3)
    a1, a1sq = _matmul(h3, W[6], "nn", "ffn1_up", out_dtypes=(BF16, BF16), epilogue=relu_sq)
    gather_steps(13, a1sq)
    f1 = _matmul(a1sq, W[7], "nn", "ffn1_down", tk=LONG_K_TILE)
    w_in0, w_out0, w1_0, w2_0, w_in1, w_out1, w1_1, w2_1 = W

    grads_big = [None] * len(big)
    backlog = _Backlog()
    run = backlog.run

    def reduce_grad(u, g):
        name, _, _, kind, _, cost = big[u]
        _GradStream(backlog, u, name, kind, cost, g, place, grads_big)

    dx, df1, d_ffn_post_1, loss_row = run(_loss_and_last_norm_bwd, 30.0, x3, f1, gain(ffn_post_g, 1), target, "loss")
    reduce_grad(7, run(_matmul, 80.0, a1sq, df1, "tn", "ffn1_down_dw", out_dtypes=(BF16,)))
    dz = run(_matmul, 82.0, df1, w2_1, "nt", "ffn1_down_dx", out_dtypes=(BF16,), epilogue=relu_sq_bwd, epi=(a1,))
    reduce_grad(6, run(_matmul, 80.0, h3, dz, "tn", "ffn1_up_dw", out_dtypes=(BF16,)))
    dh = run(_matmul, 87.0, dz, w1_1, "nt", "ffn1_up_dx", tk=LONG_K_TILE)
    dx, d_ffn_pre_1, dm1, d_mix_post_1 = run(_norms_bwd, 36.0, dx, dh, x3, gain(ffn_pre_g, 1), m1, gain(mix_post_g, 1), "norms_bwd3")

    reduce_grad(5, run(_matmul, 24.0, y1, dm1, "tn", "mix1_out_dw", out_dtypes=(BF16,)))
    dy1 = run(_matmul, 25.0, dm1, w_out1, "nt", "mix1_out_dx")
    db, dcg, du, d_sc_conv_w = run(_short_bwd, 41.0, dy1, z1, sc_conv_w_full, "short_bwd")
    dz1 = jnp.concatenate([db, dcg, du], axis=1)
    reduce_grad(4, run(_matmul, 62.0, h2, dz1, "tn", "mix1_in_dw", out_dtypes=(BF16,)))
    dh = run(_matmul, 68.0, dz1, w_in1, "nt", "mix1_in_dx")
    dx, d_mix_pre_1, df0, d_ffn_post_0 = run(_norms_bwd, 35.0, dx, dh, x2, gain(mix_pre_g, 1), f0, gain(ffn_post_g, 0), "norms_bwd2")

    reduce_grad(3, run(_matmul, 80.0, a0sq, df0, "tn", "ffn0_down_dw", out_dtypes=(BF16,)))
    dz = run(_matmul, 82.0, df0, w2_0, "nt", "ffn0_down_dx", out_dtypes=(BF16,), epilogue=relu_sq_bwd, epi=(a0,))
    reduce_grad(2, run(_matmul, 80.0, h1, dz, "tn", "ffn0_up_dw", out_dtypes=(BF16,)))
    dh = run(_matmul, 87.0, dz, w1_0, "nt", "ffn0_up_dx", tk=LONG_K_TILE)
    dx, d_ffn_pre_0, dm0, d_mix_post_0 = run(_norms_bwd, 36.0, dx, dh, x1, gain(ffn_pre_g, 0), m0, gain(mix_post_g, 0), "norms_bwd1")

    reduce_grad(1, run(_matmul, 24.0, y0, dm0, "tn", "mix0_out_dw", out_dtypes=(BF16,)))
    dy0 = run(_matmul, 25.0, dm0, w_out0, "nt", "mix0_out_dx")
    du_pool, d_pool_w, d_pool_scale = run(_pool_bwd, 28.0, dy0, pooled, pool_w_full, pool_scale, "pool_bwd")
    dc, d_ln_g, d_ln_b = run(_ln_silu_bwd, 15.0, dy0, c_conv, conv_ln_g, conv_ln_b, "ln_silu_bwd")
    dv, dgate, d_conv_w, d_conv_b = run(_conv_bwd, 52.0, dc, a_conv, z0, conv_w_full, DP, "conv_bwd")
    dz0 = jnp.concatenate([du_pool, dv, dgate], axis=1)

    device_slot = 2 * chip + lax.axis_index("c")
    small_sums = {}

    def exchange_small(key, arrays, cost):
        slots = lax.dynamic_update_slice_in_dim(jnp.zeros((N_DEVICES,) + _pack(arrays, D).shape, F32), _pack(arrays, D)[None],
                                                device_slot, axis=0)
        backlog.start(_small_exchange_job(slots, 1, [0]), "small_grads_" + key, "ici", cost,
                      lambda outs, kept: small_sums.__setitem__(key, _unpack(_sum_slots(outs[0], "small_grads_sum_" + key),
                                                                             [a.shape for a in arrays])))

    exchange_small("most", [d_mix_pre_1, jnp.concatenate([d_mix_post_0, d_mix_post_1], 0),
                            jnp.concatenate([d_ffn_pre_0, d_ffn_pre_1], 0), jnp.concatenate([d_ffn_post_0, d_ffn_post_1], 0),
                            d_pool_scale, d_conv_b, d_ln_g, d_ln_b, d_pool_w, d_conv_w, d_sc_conv_w], 112.0)
    reduce_grad(0, run(_matmul, 34.0, h0, dz0, "tn", "mix0_in_dw", out_dtypes=(BF16,)))
    dh = run(_matmul, 40.0, dz0, w_in0, "nt", "mix0_in_dx")
    grad_x, d_mix_pre_0 = run(_norms_bwd, 26.0, dx, dh, x0, gain(mix_pre_g, 0), None, None, "norms_bwd0")
    exchange_small("last", [d_mix_pre_0], 5.0)
    loss = lax.psum(loss_row[0, 0], ("x", "y", "c"))

    upd, gr, first = {}, grads_big, {}

    def keep(where, key, outs):
        where[key] = outs
        return outs

    adamw_big = [
        (7, lambda: keep(first, "ffn_w2", _adamw(ffn_w2, gr[7], m_ffn_w2, v_ffn_w2, "adamw_ffn_w2_1", layer=1)), 46.0),
        (6, lambda: keep(first, "ffn_w1", _adamw(ffn_w1, gr[6], m_ffn_w1, v_ffn_w1, "adamw_ffn_w1_1", layer=1)), 46.0),
        (5, lambda: keep(upd, "sc_w_out", _adamw(sc_w_out, gr[5], m_sc_w_out, v_sc_w_out, "adamw_sc_w_out")), 14.0),
        (4, lambda: keep(upd, "sc_w_in", _adamw(sc_w_in, gr[4], m_sc_w_in, v_sc_w_in, "adamw_sc_w_in")), 35.0),
        (3, lambda: keep(upd, "ffn_w2", _adamw(ffn_w2, gr[3], m_ffn_w2, v_ffn_w2, "adamw_ffn_w2_0", layer=0,
                                               carried=first["ffn_w2"])), 46.0),
        (2, lambda: keep(upd, "ffn_w1", _adamw(ffn_w1, gr[2], m_ffn_w1, v_ffn_w1, "adamw_ffn_w1_0", layer=0,
                                               carried=first["ffn_w1"])), 46.0),
        (1, lambda: keep(upd, "ab_w_out", _adamw(ab_w_out, gr[1], m_ab_w_out, v_ab_w_out, "adamw_ab_w_out")), 14.0),
        (0, lambda: keep(upd, "ab_w_in", _adamw(ab_w_in, gr[0], m_ab_w_in, v_ab_w_in, "adamw_ab_w_in")), 19.0)]
    while adamw_big or backlog.flights:
        due = [a for a in adamw_big if gr[a[0]] is not None]
        if due:
            adamw_big.remove(due[0])
            backlog.run(due[0][1], due[0][2])
        else:
            backlog.poll(block=True)

    (g_mix_pre_1, g_mix_post, g_ffn_pre, g_ffn_post, g_pool_scale, g_conv_b, g_ln_g, g_ln_b, g_pool_w_full, g_conv_w_full,
     g_sc_conv_w_full) = small_sums["most"]
    g_mix_pre = jnp.concatenate([small_sums["last"][0], g_mix_pre_1], 0)
    own = lambda a, ax: lax.dynamic_slice_in_dim(a, chip * (a.shape[ax] // N_CHIPS), a.shape[ax] // N_CHIPS, axis=ax)
    g_pool_w, g_conv_w, g_sc_conv_w = own(g_pool_w_full, 1), own(g_conv_w_full, 1), own(g_sc_conv_w_full, 1)

    def small_update(w, g, m, v, name):
        shape = w.shape
        as3 = lambda a: a.reshape((1, -1, shape[-1]))
        outs = _adamw(as3(w), g.reshape((-1, shape[-1])), as3(m), as3(v), "adamw_" + name)
        return [o.reshape(shape) for o in outs]

    upd["mix_pre_g"] = small_update(mix_pre_g, g_mix_pre, m_mix_pre_g, v_mix_pre_g, "mix_pre_g")
    upd["mix_post_g"] = small_update(mix_post_g, g_mix_post, m_mix_post_g, v_mix_post_g, "mix_post_g")
    upd["ffn_pre_g"] = small_update(ffn_pre_g, g_ffn_pre, m_ffn_pre_g, v_ffn_pre_g, "ffn_pre_g")
    upd["ffn_post_g"] = small_update(ffn_post_g, g_ffn_post, m_ffn_post_g, v_ffn_post_g, "ffn_post_g")
    upd["pool_w"] = small_update(pool_w, g_pool_w, m_pool_w, v_pool_w, "pool_w")
    upd["pool_scale"] = small_update(pool_scale, g_pool_scale, m_pool_scale, v_pool_scale, "pool_scale")
    upd["conv_w"] = small_update(conv_w, g_conv_w, m_conv_w, v_conv_w, "conv_w")
    upd["conv_b"] = small_update(conv_b, g_conv_b, m_conv_b, v_conv_b, "conv_b")
    upd["conv_ln_g"] = small_update(conv_ln_g, g_ln_g, m_conv_ln_g, v_conv_ln_g, "conv_ln_g")
    upd["conv_ln_b"] = small_update(conv_ln_b, g_ln_b, m_conv_ln_b, v_conv_ln_b, "conv_ln_b")
    upd["sc_conv_w"] = small_update(sc_conv_w, g_sc_conv_w, m_sc_conv_w, v_sc_conv_w, "sc_conv_w")

    order = ["mix_pre_g", "mix_post_g", "ffn_pre_g", "ffn_post_g", "ab_w_in", "pool_w", "pool_scale", "conv_w", "conv_b",
             "conv_ln_g", "conv_ln_b", "ab_w_out", "sc_w_in", "sc_conv_w", "sc_w_out", "ffn_w1", "ffn_w2"]
    out = [loss, grad_x[None]]
    for part in range(4):
        out += [upd[n][part] for n in order]
    return tuple(out)
```

```python
import jax
import jax.numpy as jnp
from jax import lax
from jax.experimental import pallas as pl
from jax.experimental.pallas import tpu as pltpu

F32, BF16 = jnp.float32, jnp.bfloat16
EPS = 1e-6
N_GROUPS = 4
MAX_WINDOW = 16
CONV_K = 31
SHORT_K = 3
CONV_PAD = 32
SHORT_PAD = 8
ADAM_LR, ADAM_B1, ADAM_B2, ADAM_EPS, ADAM_WD, ADAM_STEP = 0.001, 0.9, 0.999, 1e-08, 0.01, 10
N_CHIPS = 4
VMEM_LIMIT_BYTES = 56 * 1024 * 1024
ROW_TILE = 256
CHUNK = 256
LONG_K_TILE = 4096
MESH = pl.DeviceIdType.MESH
HBM = pl.BlockSpec(memory_space=pltpu.HBM)
SEM = pl.BlockSpec(memory_space=pltpu.SEMAPHORE)
ANY = pl.BlockSpec(memory_space=pl.ANY)


def _cp(*sem):
    return pltpu.CompilerParams(dimension_semantics=sem, vmem_limit_bytes=VMEM_LIMIT_BYTES)


def _sigmoid(v):
    return 1.0 / (1.0 + jnp.exp(-v))


class _Behind:
    pending = []


def _call(body, prefetch=None, **kw):
    behind, _Behind.pending = _Behind.pending, []
    single = not isinstance(kw["out_shape"], (list, tuple))
    in_specs, scratch = list(kw["in_specs"]), list(kw.get("scratch_shapes", ()))
    out_shape = [kw["out_shape"]] if single else list(kw["out_shape"])
    out_specs = [kw["out_specs"]] if single else list(kw["out_specs"])
    n_pre = 0 if prefetch is None else 1
    n_own, n_behind = len(in_specs), len(behind)

    def wrapped(*refs):
        body(*refs[:n_pre + n_own], *refs[n_pre + n_own + n_behind:])

    specs = dict(grid=kw["grid"], in_specs=in_specs + [ANY] * n_behind, out_specs=out_specs)
    if prefetch is None:
        specs["scratch_shapes"] = scratch
    else:
        specs = dict(grid_spec=pltpu.PrefetchScalarGridSpec(num_scalar_prefetch=1, scratch_shapes=scratch, **specs))
    aliases = {n_pre + i: o for i, o in kw.get("input_output_aliases", {}).items()}
    call = pl.pallas_call(wrapped, name=kw["name"], out_shape=out_shape, input_output_aliases=aliases,
                          compiler_params=_cp(*kw["sem"]), **specs)

    def run(*args):
        outs = call(*([prefetch] * n_pre), *args, *behind)
        return outs[0] if single else list(outs)

    return run


_DIMS = {"nn": (((1,), (0,)), ((), ())), "nt": (((1,), (1,)), ((), ())), "tn": (((0,), (0,)), ((), ()))}


def _pick(n, cap, step=256):
    if n <= cap:
        return n
    return next(t for t in range(cap - cap % step, 0, -step) if n % t == 0)


def _matmul(a, b, mode, name, out_dtypes=(F32,), epilogue=None, epi=(), tm=1024, tn=1024, tk=2048):
    if mode == "tn":
        (K, M), (K2, N) = a.shape, b.shape
    elif mode == "nt":
        (M, K), (N, K2) = a.shape, b.shape
    else:
        (M, K), (K2, N) = a.shape, b.shape
    assert K == K2
    tm, tn, tk = _pick(M, tm), _pick(N, tn), _pick(K, tk)
    nk = K // tk
    a_spec = pl.BlockSpec((tk, tm), lambda i, j, k: (k, i)) if mode == "tn" else pl.BlockSpec((tm, tk), lambda i, j, k: (i, k))
    b_spec = pl.BlockSpec((tn, tk), lambda i, j, k: (j, k)) if mode == "nt" else pl.BlockSpec((tk, tn), lambda i, j, k: (k, j))
    o_spec = pl.BlockSpec((tm, tn), lambda i, j, k: (i, j))
    n_epi, n_out = len(epi), len(out_dtypes)

    def body(a_ref, b_ref, *rest):
        epi_refs, out_refs, scratch = rest[:n_epi], rest[n_epi:n_epi + n_out], rest[n_epi + n_out:]
        part = lax.dot_general(a_ref[...].astype(BF16), b_ref[...].astype(BF16), _DIMS[mode], preferred_element_type=F32)

        def finish(acc):
            outs = epilogue(acc, *[r[...] for r in epi_refs]) if epilogue else (acc,)
            for o_ref, o in zip(out_refs, outs):
                o_ref[...] = o.astype(o_ref.dtype)

        if nk == 1:
            finish(part)
        else:
            acc_ref = scratch[0]
            k = pl.program_id(2)

            @pl.when(k == 0)
            def _():
                acc_ref[...] = part

            @pl.when(k > 0)
            def _():
                acc_ref[...] += part

            @pl.when(k == nk - 1)
            def _():
                finish(acc_ref[...])

    outs = _call(
        body, name=name, grid=(M // tm, N // tn, nk),
        in_specs=[a_spec, b_spec] + [o_spec] * n_epi, out_specs=[o_spec] * n_out,
        out_shape=[jax.ShapeDtypeStruct((M, N), dt) for dt in out_dtypes],
        scratch_shapes=[pltpu.VMEM((tm, tn), F32)] if nk > 1 else [],
        sem=("parallel", "parallel", "arbitrary"))(a, b, *epi)
    return outs[0] if n_out == 1 else outs


def _rms(x, g):
    r = lax.rsqrt(jnp.mean(x * x, axis=-1, keepdims=True) + EPS)
    return x * r * g


def _rms_bwd(dy, x, g):
    r = lax.rsqrt(jnp.mean(x * x, axis=-1, keepdims=True) + EPS)
    xn = x * r
    dyg = dy * g
    dx = r * (dyg - xn * jnp.mean(dyg * xn, axis=-1, keepdims=True))
    return dx, jnp.sum(dy * xn, axis=0, keepdims=True)


def _rows(d, tr=ROW_TILE):
    return pl.BlockSpec((tr, d), lambda i: (i, 0))


def _vec(d):
    return pl.BlockSpec((1, d), lambda i: (0, 0))


def _accumulate(ref, val):
    @pl.when(pl.program_id(0) == 0)
    def _():
        ref[...] = val

    @pl.when(pl.program_id(0) > 0)
    def _():
        ref[...] += val


def _norm_fwd(x, g, name):
    T, D = x.shape

    def body(x_ref, g_ref, h_ref):
        h_ref[...] = _rms(x_ref[...], g_ref[...]).astype(BF16)

    return _call(body,name=name,grid=(T // ROW_TILE,), in_specs=[_rows(D), _vec(D)], out_specs=_rows(D),
                          out_shape=jax.ShapeDtypeStruct((T, D), BF16), sem=("parallel",))(x, g)


def _residual_norm(x, m, g_post, g_next, name):
    T, D = x.shape

    def body(x_ref, m_ref, gp_ref, gn_ref, xo_ref, h_ref):
        xo = x_ref[...] + _rms(m_ref[...], gp_ref[...])
        xo_ref[...] = xo
        h_ref[...] = _rms(xo, gn_ref[...]).astype(BF16)

    return _call(body,name=name,grid=(T // ROW_TILE,), in_specs=[_rows(D), _rows(D), _vec(D), _vec(D)],
                          out_specs=[_rows(D), _rows(D)],
                          out_shape=[jax.ShapeDtypeStruct((T, D), F32), jax.ShapeDtypeStruct((T, D), BF16)],
                          sem=("parallel",))(x, m, g_post, g_next)


def _loss_and_last_norm_bwd(x, m, g_post, target, name):
    T, D = x.shape

    def body(x_ref, m_ref, gp_ref, t_ref, dx_ref, dm_ref, dg_ref, loss_ref):
        m_val, gp = m_ref[...], gp_ref[...]
        err = x_ref[...] + _rms(m_val, gp) - t_ref[...]
        dx = err * (1.0 / D)
        dx_ref[...] = dx
        dm, dg = _rms_bwd(dx, m_val, gp)
        dm_ref[...] = dm.astype(BF16)
        _accumulate(dg_ref, dg)
        _accumulate(loss_ref, jnp.full((1, 128), 0.5 * jnp.sum(err * err) * (1.0 / D), F32))

    return _call(body,name=name,grid=(T // ROW_TILE,), in_specs=[_rows(D), _rows(D), _vec(D), _rows(D)],
                          out_specs=[_rows(D), _rows(D), _vec(D), _vec(128)],
                          out_shape=[jax.ShapeDtypeStruct((T, D), F32), jax.ShapeDtypeStruct((T, D), BF16),
                                     jax.ShapeDtypeStruct((1, D), F32), jax.ShapeDtypeStruct((1, 128), F32)],
                          sem=("arbitrary",))(x, m, g_post, target)


def _norms_bwd(dx, dh, x_in, g_pre, m_prev, g_post_prev, name):
    T, D = dx.shape
    with_prev = m_prev is not None

    def body(*refs):
        if with_prev:
            dx_ref, dh_ref, x_ref, gq_ref, m_ref, gp_ref, dxo_ref, dgq_ref, dm_ref, dgp_ref = refs
        else:
            dx_ref, dh_ref, x_ref, gq_ref, dxo_ref, dgq_ref = refs
        d_in, dgq = _rms_bwd(dh_ref[...], x_ref[...], gq_ref[...])
        dxo = dx_ref[...] + d_in
        dxo_ref[...] = dxo
        _accumulate(dgq_ref, dgq)
        if with_prev:
            dm, dgp = _rms_bwd(dxo, m_ref[...], gp_ref[...])
            dm_ref[...] = dm.astype(BF16)
            _accumulate(dgp_ref, dgp)

    ins, in_specs = [dx, dh, x_in, g_pre], [_rows(D), _rows(D), _rows(D), _vec(D)]
    out_specs = [_rows(D), _vec(D)]
    out_shape = [jax.ShapeDtypeStruct((T, D), F32), jax.ShapeDtypeStruct((1, D), F32)]
    if with_prev:
        ins += [m_prev, g_post_prev]
        in_specs += [_rows(D), _vec(D)]
        out_specs += [_rows(D), _vec(D)]
        out_shape += [jax.ShapeDtypeStruct((T, D), BF16), jax.ShapeDtypeStruct((1, D), F32)]
    return _call(body,name=name,grid=(T // ROW_TILE,), in_specs=in_specs, out_specs=out_specs, out_shape=out_shape,
                          sem=("arbitrary",))(*ins)


def _window_weights(g):
    w = 2 << g
    return w, [jnp.where(j < w, 1.0, 0.0).astype(F32) for j in range(MAX_WINDOW)]


def _valid_count(r0, rows, w):
    t = (lax.broadcasted_iota(jnp.int32, (rows, 1), 0) + (r0 + 1)).astype(F32)
    return jnp.minimum(t, w.astype(F32))


def _pool_fwd(z, pool_w, pool_scale, name):
    T = z.shape[0]
    PG = pool_w.shape[-1]
    DP = N_GROUPS * PG
    rc = min(CHUNK, T)

    def body(u_ref, pw_ref, sc_ref, pooled_ref, y_ref, pad):
        w, wts = _window_weights(pl.program_id(0))
        pad[pl.ds(0, MAX_WINDOW), :] = jnp.zeros((MAX_WINDOW, PG), F32)
        pad[pl.ds(MAX_WINDOW, T), :] = u_ref[...]
        for r0 in range(0, T, rc):
            acc = jnp.zeros((rc, PG), F32)
            for j in range(MAX_WINDOW):
                acc = acc + wts[j] * pad[pl.ds(MAX_WINDOW + r0 - j, rc), :]
            pooled = acc / _valid_count(r0, rc, w) - u_ref[pl.ds(r0, rc), :]
            pooled_ref[pl.ds(r0, rc), :] = pooled.astype(BF16)
        mixed = jnp.dot(pooled_ref[...], pw_ref[...].astype(BF16), preferred_element_type=F32)
        y_ref[...] = (mixed * sc_ref[...]).astype(BF16)

    col = lambda g: (0, g)
    return _call(
        body, name=name,grid=(N_GROUPS,),
        in_specs=[pl.BlockSpec((T, PG), col), pl.BlockSpec((None, PG, PG), lambda g: (g, 0, 0)), pl.BlockSpec((1, PG), col)],
        out_specs=[pl.BlockSpec((T, PG), col), pl.BlockSpec((T, PG), col)],
        out_shape=[jax.ShapeDtypeStruct((T, DP), BF16), jax.ShapeDtypeStruct((T, DP), BF16)],
        scratch_shapes=[pltpu.VMEM((T + MAX_WINDOW, PG), F32)], sem=("parallel",))(z, pool_w, pool_scale)


def _pool_bwd(dy, pooled, pool_w, pool_scale, name):
    T = dy.shape[0]
    PG = pool_w.shape[-1]
    DP = N_GROUPS * PG
    rc = min(CHUNK, T)

    def body(dy_ref, pooled_ref, pw_ref, sc_ref, du_ref, dpw_ref, dsc_ref, pad, dp_ref):
        w, wts = _window_weights(pl.program_id(0))
        pooled_v, pw = pooled_ref[...], pw_ref[...].astype(BF16)
        dy_v = dy_ref[...]
        mixed = jnp.dot(pooled_v, pw, preferred_element_type=F32)
        dsc_ref[...] = jnp.sum(dy_v * mixed, axis=0, keepdims=True)
        dmixed = (dy_v * sc_ref[...]).astype(BF16)
        dpw_ref[...] = lax.dot_general(pooled_v, dmixed, _DIMS["tn"], preferred_element_type=F32)
        dp_ref[...] = lax.dot_general(dmixed, pw, _DIMS["nt"], preferred_element_type=F32)
        pad[pl.ds(T, MAX_WINDOW), :] = jnp.zeros((MAX_WINDOW, PG), F32)
        for r0 in range(0, T, rc):
            pad[pl.ds(r0, rc), :] = dp_ref[pl.ds(r0, rc), :] / _valid_count(r0, rc, w)
        for r0 in range(0, T, rc):
            acc = jnp.zeros((rc, PG), F32)
            for j in range(MAX_WINDOW):
                acc = acc + wts[j] * pad[pl.ds(r0 + j, rc), :]
            du_ref[pl.ds(r0, rc), :] = (acc - dp_ref[pl.ds(r0, rc), :]).astype(BF16)

    col = lambda g: (0, g)
    return _call(
        body, name=name,grid=(N_GROUPS,),
        in_specs=[pl.BlockSpec((T, PG), col), pl.BlockSpec((T, PG), col), pl.BlockSpec((None, PG, PG), lambda g: (g, 0, 0)),
                  pl.BlockSpec((1, PG), col)],
        out_specs=[pl.BlockSpec((T, PG), col), pl.BlockSpec((None, PG, PG), lambda g: (g, 0, 0)), pl.BlockSpec((1, PG), col)],
        out_shape=[jax.ShapeDtypeStruct((T, DP), BF16), jax.ShapeDtypeStruct((N_GROUPS, PG, PG), F32),
                   jax.ShapeDtypeStruct((1, DP), F32)],
        scratch_shapes=[pltpu.VMEM((T + MAX_WINDOW, PG), F32), pltpu.VMEM((T, PG), F32)],
        sem=("parallel",))(dy, pooled, pool_w, pool_scale)


def _conv_fwd(z, conv_w, conv_b, d_pool, name, tc=128):
    T = z.shape[0]
    DC = conv_w.shape[-1]
    rc = min(CHUNK, T)
    v0, g0 = d_pool // tc, (d_pool + DC) // tc

    def body(v_ref, gt_ref, w_ref, b_ref, a_ref, c_ref, pad):
        pad[pl.ds(0, CONV_PAD), :] = jnp.zeros((CONV_PAD, tc), F32)
        for r0 in range(0, T, rc):
            a = v_ref[pl.ds(r0, rc), :] * _sigmoid(gt_ref[pl.ds(r0, rc), :])
            a_ref[pl.ds(r0, rc), :] = a
            pad[pl.ds(CONV_PAD + r0, rc), :] = a
        for r0 in range(0, T, rc):
            acc = jnp.zeros((rc, tc), F32) + b_ref[...]
            for k in range(CONV_K):
                acc = acc + w_ref[pl.ds(k, 1), :] * pad[pl.ds(CONV_PAD - (CONV_K - 1) + k + r0, rc), :]
            c_ref[pl.ds(r0, rc), :] = acc

    col = lambda j: (0, j)
    return _call(
        body, name=name,grid=(DC // tc,),
        in_specs=[pl.BlockSpec((T, tc), lambda j: (0, v0 + j)), pl.BlockSpec((T, tc), lambda j: (0, g0 + j)),
                  pl.BlockSpec((CONV_K, tc), col), pl.BlockSpec((1, tc), col)],
        out_specs=[pl.BlockSpec((T, tc), col), pl.BlockSpec((T, tc), col)],
        out_shape=[jax.ShapeDtypeStruct((T, DC), F32), jax.ShapeDtypeStruct((T, DC), F32)],
        scratch_shapes=[pltpu.VMEM((T + CONV_PAD, tc), F32)], sem=("parallel",))(z, z, conv_w, conv_b)


def _conv_bwd(dc, a, z, conv_w, d_pool, name, tc=128):
    T, DC = dc.shape
    rc = min(CHUNK, T)
    v0, g0 = d_pool // tc, (d_pool + DC) // tc

    def body(dc_ref, a_ref, v_ref, gt_ref, w_ref, dv_ref, dg_ref, dw_ref, db_ref, apad, dpad):
        apad[pl.ds(0, CONV_PAD), :] = jnp.zeros((CONV_PAD, tc), F32)
        apad[pl.ds(CONV_PAD, T), :] = a_ref[...]
        dpad[pl.ds(0, T), :] = dc_ref[...]
        dpad[pl.ds(T, CONV_PAD), :] = jnp.zeros((CONV_PAD, tc), F32)
        db_ref[...] = jnp.sum(dc_ref[...], axis=0, keepdims=True)
        for k in range(CONV_K):
            acc = jnp.zeros((8, tc), F32)
            for r0 in range(0, T, rc):
                prod = dc_ref[pl.ds(r0, rc), :] * apad[pl.ds(CONV_PAD - (CONV_K - 1) + k + r0, rc), :]
                acc = acc + jnp.sum(prod.reshape(rc // 8, 8, tc), axis=0)
            dw_ref[pl.ds(k, 1), :] = jnp.sum(acc, axis=0, keepdims=True)
        for r0 in range(0, T, rc):
            da = jnp.zeros((rc, tc), F32)
            for k in range(CONV_K):
                da = da + w_ref[pl.ds(k, 1), :] * dpad[pl.ds(r0 + (CONV_K - 1) - k, rc), :]
            sig = _sigmoid(gt_ref[pl.ds(r0, rc), :])
            dv_ref[pl.ds(r0, rc), :] = (da * sig).astype(BF16)
            dg_ref[pl.ds(r0, rc), :] = (da * v_ref[pl.ds(r0, rc), :] * sig * (1.0 - sig)).astype(BF16)

    col = lambda j: (0, j)
    return _call(
        body, name=name,grid=(DC // tc,),
        in_specs=[pl.BlockSpec((T, tc), col), pl.BlockSpec((T, tc), col), pl.BlockSpec((T, tc), lambda j: (0, v0 + j)),
                  pl.BlockSpec((T, tc), lambda j: (0, g0 + j)), pl.BlockSpec((CONV_K, tc), col)],
        out_specs=[pl.BlockSpec((T, tc), col), pl.BlockSpec((T, tc), col), pl.BlockSpec((CONV_K, tc), col),
                   pl.BlockSpec((1, tc), col)],
        out_shape=[jax.ShapeDtypeStruct((T, DC), BF16), jax.ShapeDtypeStruct((T, DC), BF16),
                   jax.ShapeDtypeStruct((CONV_K, DC), F32), jax.ShapeDtypeStruct((1, DC), F32)],
        scratch_shapes=[pltpu.VMEM((T + CONV_PAD, tc), F32), pltpu.VMEM((T + CONV_PAD, tc), F32)],
        sem=("parallel",))(dc, a, z, z, conv_w)


def _layer_norm_parts(c, g, b):
    mu = jnp.mean(c, axis=-1, keepdims=True)
    xc = c - mu
    rstd = lax.rsqrt(jnp.mean(xc * xc, axis=-1, keepdims=True) + EPS)
    xhat = xc * rstd
    return xhat, rstd, xhat * g + b


def _ln_silu_fwd(c, g, b, name):
    T, DC = c.shape

    def body(c_ref, g_ref, b_ref, y_ref):
        _, _, ln = _layer_norm_parts(c_ref[...], g_ref[...], b_ref[...])
        y_ref[...] = (ln * _sigmoid(ln)).astype(BF16)

    return _call(body,name=name,grid=(T // ROW_TILE,), in_specs=[_rows(DC), _vec(DC), _vec(DC)], out_specs=_rows(DC),
                          out_shape=jax.ShapeDtypeStruct((T, DC), BF16), sem=("parallel",))(c, g, b)


def _ln_silu_bwd(dy, c, g, b, name):
    T, DC = c.shape

    def body(dy_ref, c_ref, g_ref, b_ref, dc_ref, dg_ref, db_ref):
        gain = g_ref[...]
        xhat, rstd, ln = _layer_norm_parts(c_ref[...], gain, b_ref[...])
        s = _sigmoid(ln)
        dln = dy_ref[...] * (s * (1.0 + ln * (1.0 - s)))
        _accumulate(dg_ref, jnp.sum(dln * xhat, axis=0, keepdims=True))
        _accumulate(db_ref, jnp.sum(dln, axis=0, keepdims=True))
        dxh = dln * gain
        dc_ref[...] = rstd * (dxh - jnp.mean(dxh, axis=-1, keepdims=True) - xhat * jnp.mean(dxh * xhat, axis=-1, keepdims=True))

    return _call(body,name=name,grid=(T // ROW_TILE,),
                          in_specs=[pl.BlockSpec((ROW_TILE, DC), lambda i: (i, 1)), _rows(DC), _vec(DC), _vec(DC)],
                          out_specs=[_rows(DC), _vec(DC), _vec(DC)],
                          out_shape=[jax.ShapeDtypeStruct((T, DC), F32), jax.ShapeDtypeStruct((1, DC), F32),
                                     jax.ShapeDtypeStruct((1, DC), F32)],
                          sem=("arbitrary",))(dy, c, g, b)


def _short_specs(T, DS, tc):
    n = DS // tc
    return [pl.BlockSpec((T, tc), lambda j: (0, j)), pl.BlockSpec((T, tc), lambda j: (0, n + j)),
            pl.BlockSpec((T, tc), lambda j: (0, 2 * n + j))]


def _short_fwd(z, w, name, tc=256):
    T = z.shape[0]
    DS = w.shape[-1]
    rc = min(CHUNK, T)

    def body(b_ref, cg_ref, u_ref, w_ref, y_ref, pad):
        pad[pl.ds(0, SHORT_PAD), :] = jnp.zeros((SHORT_PAD, tc), F32)
        pad[pl.ds(SHORT_PAD, T), :] = cg_ref[...] * u_ref[...]
        for r0 in range(0, T, rc):
            r = jnp.zeros((rc, tc), F32)
            for k in range(SHORT_K):
                r = r + w_ref[pl.ds(k, 1), :] * pad[pl.ds(SHORT_PAD - (SHORT_K - 1) + k + r0, rc), :]
            y_ref[pl.ds(r0, rc), :] = (b_ref[pl.ds(r0, rc), :] * r).astype(BF16)

    col = lambda j: (0, j)
    return _call(body,name=name,grid=(DS // tc,), in_specs=_short_specs(T, DS, tc) + [pl.BlockSpec((SHORT_K, tc), col)],
                          out_specs=pl.BlockSpec((T, tc), col), out_shape=jax.ShapeDtypeStruct((T, DS), BF16),
                          scratch_shapes=[pltpu.VMEM((T + SHORT_PAD, tc), F32)], sem=("parallel",))(z, z, z, w)


def _short_bwd(dy, z, w, name, tc=256):
    T, DS = dy.shape
    rc = min(CHUNK, T)

    def body(dy_ref, b_ref, cg_ref, u_ref, w_ref, db_ref, dcg_ref, du_ref, dw_ref, qpad, rpad):
        qpad[pl.ds(0, SHORT_PAD), :] = jnp.zeros((SHORT_PAD, tc), F32)
        qpad[pl.ds(SHORT_PAD, T), :] = cg_ref[...] * u_ref[...]
        rpad[pl.ds(0, T), :] = dy_ref[...] * b_ref[...]
        rpad[pl.ds(T, SHORT_PAD), :] = jnp.zeros((SHORT_PAD, tc), F32)
        accs = [jnp.zeros((8, tc), F32) for _ in range(SHORT_K)]
        for r0 in range(0, T, rc):
            r = jnp.zeros((rc, tc), F32)
            dq = jnp.zeros((rc, tc), F32)
            dr = rpad[pl.ds(r0, rc), :]
            for k in range(SHORT_K):
                q_k = qpad[pl.ds(SHORT_PAD - (SHORT_K - 1) + k + r0, rc), :]
                r = r + w_ref[pl.ds(k, 1), :] * q_k
                dq = dq + w_ref[pl.ds(k, 1), :] * rpad[pl.ds(r0 + (SHORT_K - 1) - k, rc), :]
                accs[k] = accs[k] + jnp.sum((dr * q_k).reshape(rc // 8, 8, tc), axis=0)
            db_ref[pl.ds(r0, rc), :] = (dy_ref[pl.ds(r0, rc), :] * r).astype(BF16)
            dcg_ref[pl.ds(r0, rc), :] = (dq * u_ref[pl.ds(r0, rc), :]).astype(BF16)
            du_ref[pl.ds(r0, rc), :] = (dq * cg_ref[pl.ds(r0, rc), :]).astype(BF16)
        for k in range(SHORT_K):
            dw_ref[pl.ds(k, 1), :] = jnp.sum(accs[k], axis=0, keepdims=True)

    col = lambda j: (0, j)
    tile = pl.BlockSpec((T, tc), col)
    return _call(body,name=name,grid=(DS // tc,),
                          in_specs=[tile] + _short_specs(T, DS, tc) + [pl.BlockSpec((SHORT_K, tc), col)],
                          out_specs=[tile, tile, tile, pl.BlockSpec((SHORT_K, tc), col)],
                          out_shape=[jax.ShapeDtypeStruct((T, DS), BF16)] * 3 + [jax.ShapeDtypeStruct((SHORT_K, DS), F32)],
                          scratch_shapes=[pltpu.VMEM((T + SHORT_PAD, tc), F32), pltpu.VMEM((T + SHORT_PAD, tc), F32)],
                          sem=("parallel",))(dy, z, z, z, w)


def _tile_rows(rows, cols, n_bufs):
    budget = VMEM_LIMIT_BYTES * 3 // 4 // (2 * n_bufs * 4 * cols)
    tr = rows
    while tr > budget and tr % 16 == 0:
        tr //= 2
    return tr


def _placed_call(body, name, place, grid, in_specs, out_specs, out_shape, ins):
    return _call(body,prefetch=place, name=name, grid=grid, in_specs=in_specs, out_specs=out_specs, out_shape=out_shape,
                 sem=("parallel",))(*ins)


def _cast_into_full(w, layer, kind, place, name):
    _, R, C = w.shape
    tr = _tile_rows(R, C, 2)
    nb = R // tr
    if kind == "col":
        full, out_spec = (R, C * N_CHIPS), pl.BlockSpec((tr, C), lambda i, s: (i, s[0]))
    else:
        full, out_spec = (R * N_CHIPS, C), pl.BlockSpec((tr, C), lambda i, s: (s[0] * nb + i, 0))

    def body(s_ref, w_ref, o_ref):
        o_ref[...] = w_ref[...].astype(BF16)

    return _placed_call(body, name, place, (nb,), [pl.BlockSpec((None, tr, C), lambda i, s: (layer, i, 0))], out_spec,
                        jax.ShapeDtypeStruct(full, BF16), [w])


def _add_pair(grad, theirs, kind, place, name):
    R, C = grad.shape
    piece_rows = R // 2 if kind == "col" else R // N_CHIPS // 2
    tr = _tile_rows(piece_rows, C, 3)
    nb = piece_rows // tr
    if kind == "col":
        g_spec = pl.BlockSpec((tr, C), lambda i, s: (s[1] * nb + i, 0))
    else:
        g_spec = pl.BlockSpec((tr, C), lambda i, s: ((2 * (i // nb) + s[1]) * nb + i % nb, 0))
    flat = pl.BlockSpec((tr, C), lambda i, s: (i, 0))

    def body(s_ref, a_ref, b_ref, o_ref):
        o_ref[...] = (a_ref[...].astype(F32) + b_ref[...].astype(F32)).astype(BF16)

    return _placed_call(body, name, place, (R // 2 // tr,), [g_spec, flat], flat, jax.ShapeDtypeStruct((R // 2, C), BF16),
                        [grad, theirs])


def _sum_chips(chip_sum, arrived, kind, place, name):
    _, H, W = arrived.shape
    tr = _tile_rows(H, W, 6)
    nb = H // tr
    if kind == "col":
        own_spec = pl.BlockSpec((tr, W), lambda i, s: (i, s[0]))
    else:
        own_spec = pl.BlockSpec((tr, W), lambda i, s: (s[0] * nb + i, 0))

    def body(s_ref, p_ref, r_ref, o_ref):
        acc = p_ref[...].astype(F32)
        for i in range(N_CHIPS - 1):
            acc = acc + r_ref[i].astype(F32)
        o_ref[...] = acc

    return _placed_call(body, name, place, (nb,), [own_spec, pl.BlockSpec((N_CHIPS - 1, tr, W), lambda i, s: (0, i, 0))],
                        pl.BlockSpec((tr, W), lambda i, s: (s[1] * nb + i, 0)), jax.ShapeDtypeStruct((2 * H, W), F32),
                        [chip_sum, arrived])


def _adamw_values(w, g, m, v):
    m = ADAM_B1 * m + (1.0 - ADAM_B1) * g
    v = ADAM_B2 * v + (1.0 - ADAM_B2) * (g * g)
    m_hat = m / (1.0 - ADAM_B1 ** ADAM_STEP)
    v_hat = v / (1.0 - ADAM_B2 ** ADAM_STEP)
    return -ADAM_LR * (m_hat / (jnp.sqrt(v_hat) + ADAM_EPS) + ADAM_WD * w), m, v


def _adamw(w, g, m, v, name, layer=0, carried=None):
    L, R, C = w.shape
    tr = _tile_rows(R, C, 8)

    def body(w_ref, g_ref, m_ref, v_ref, *rest):
        go_ref, d_ref, mo_ref, vo_ref = rest[-4:]
        g_val = g_ref[...]
        d, m_new, v_new = _adamw_values(w_ref[...], g_val, m_ref[...], v_ref[...])
        go_ref[...], d_ref[...], mo_ref[...], vo_ref[...] = g_val, d, m_new, v_new

    lay = pl.BlockSpec((None, tr, C), lambda i: (layer, i, 0))
    ins = [w, g, m, v]
    in_specs = [lay, pl.BlockSpec((tr, C), lambda i: (i, 0)), lay, lay]
    aliases = {}
    if carried is not None:
        ins += list(carried)
        in_specs += [pl.BlockSpec(memory_space=pl.ANY)] * 4
        aliases = {4 + i: i for i in range(4)}
    return _call(body, name=name, grid=(R // tr,), in_specs=in_specs, out_specs=[lay] * 4,
                 out_shape=[jax.ShapeDtypeStruct((L, R, C), F32)] * 4, input_output_aliases=aliases, sem=("parallel",))(*ins)


def _aligned(v, m):
    return v if isinstance(v, int) else pl.multiple_of(v, m)


def _place():
    x, y, c = lax.axis_index("x"), lax.axis_index("y"), lax.axis_index("c")
    other_chips = [(x, 1 - y), (1 - x, y), (1 - x, 1 - y)]
    return x, y, c, 2 * x + y, other_chips


def _chip_index(chip):
    return 2 * chip[0] + chip[1]


def _piece(ref, kind, k, h):
    R, C = ref.shape
    if kind == "col":
        return ref.at[pl.ds(_aligned(h * (R // 2), 16), R // 2), pl.ds(_aligned(k * (C // N_CHIPS), 128), C // N_CHIPS)]
    rs = R // N_CHIPS
    return ref.at[pl.ds(_aligned(k * rs + h * (rs // 2), 16), rs // 2), :]


def _compact_piece(ref, kind, k):
    R2, C = ref.shape
    if kind == "col":
        return ref.at[:, pl.ds(_aligned(k * (C // N_CHIPS), 128), C // N_CHIPS)]
    return ref.at[pl.ds(_aligned(k * (R2 // N_CHIPS), 16), R2 // N_CHIPS), :]


def _half_rows(ref, h):
    R = ref.shape[0]
    return ref.at[pl.ds(_aligned(h * (R // 2), 16), R // 2), :]


class _Copies:
    def __init__(self, send_sems, recv_sems):
        self.send_sems, self.recv_sems = send_sems, recv_sems
        self.n_remote = 0

    def remote(self, src, dst, device):
        k = self.n_remote
        self.n_remote += 1
        return pltpu.make_async_remote_copy(src_ref=src, dst_ref=dst, send_sem=self.send_sems.at[k], recv_sem=self.recv_sems.at[k],
                                            device_id=device, device_id_type=MESH)


class _Job:
    def __init__(self, ins, out_shape, aliases, n_remote, build):
        self.ins, self.out_shape, self.aliases, self.n_remote, self.build = list(ins), list(out_shape), dict(aliases), n_remote, build


class _Flying:
    def __init__(self, job, send_sems, recv_sems, bufs, token):
        self.job, self.send_sems, self.recv_sems, self.bufs, self.token = job, send_sems, recv_sems, bufs, token


def _job_refs(job, buf_refs):
    n_out = len(job.out_shape)
    kept = [i for i in range(len(job.ins)) if i not in job.aliases]
    ins = [buf_refs[job.aliases[i]] if i in job.aliases else buf_refs[n_out + kept.index(i)] for i in range(len(job.ins))]
    return ins, list(buf_refs[:n_out])


def _start_job(job, name, after=()):
    n_in, n_out, n_after = len(job.ins), len(job.out_shape), len(after)
    kept = [i for i in range(n_in) if i not in job.aliases]
    n_bufs = n_out + len(kept)

    def body(*refs):
        in_refs, out_refs = refs[:n_in], refs[n_in + n_after:n_in + n_after + n_out]
        send_sems, recv_sems, token = refs[n_in + n_after + n_bufs:]
        for d in job.build(in_refs, out_refs, _Copies(send_sems, recv_sems)):
            d.start()
        token[...] = jnp.zeros_like(token)

    aliases = dict(job.aliases)
    aliases.update({i: n_out + k for k, i in enumerate(kept)})
    sems = pltpu.SemaphoreType.DMA((job.n_remote,))
    outs = pl.pallas_call(
        body, name=name, in_specs=[HBM] * n_in + [ANY] * n_after,
        out_specs=[HBM] * n_bufs + [SEM, SEM, pl.BlockSpec(memory_space=pltpu.VMEM)],
        out_shape=job.out_shape + [jax.ShapeDtypeStruct(job.ins[i].shape, job.ins[i].dtype) for i in kept]
        + [sems, sems, jax.ShapeDtypeStruct((8, 128), F32)],
        input_output_aliases=aliases,
        compiler_params=pltpu.CompilerParams(has_side_effects=pltpu.SideEffectType.DATAFLOW_SIDE_EFFECTING))(*job.ins, *after)
    return _Flying(job, outs[n_bufs], outs[n_bufs + 1], list(outs[:n_bufs]), outs[n_bufs + 2])


def _wait_job(flying, name, after=()):
    job, n_bufs, n_after = flying.job, len(flying.bufs), len(after)

    def body(*refs):
        in_refs, out_refs = _job_refs(job, refs[:n_bufs])
        send_sems, recv_sems = refs[n_bufs:n_bufs + 2]
        copies = job.build(in_refs, out_refs, _Copies(send_sems, recv_sems))
        for d in copies:
            d.wait_send()
        for d in copies:
            d.wait_recv()

    outs = pl.pallas_call(
        body, name=name, in_specs=[HBM] * n_bufs + [SEM, SEM] + [ANY] * n_after, out_specs=[HBM] * n_bufs,
        out_shape=[jax.ShapeDtypeStruct(b.shape, b.dtype) for b in flying.bufs],
        input_output_aliases={i: i for i in range(n_bufs)},
        compiler_params=pltpu.CompilerParams(has_side_effects=pltpu.SideEffectType.DATAFLOW_SIDE_EFFECTING))(
            *flying.bufs, flying.send_sems, flying.recv_sems, *after)
    return list(outs[:len(job.out_shape)]), list(outs[len(job.out_shape):])


def _in_place(arrays):
    return [jax.ShapeDtypeStruct(a.shape, a.dtype) for a in arrays], {u: u for u in range(len(arrays))}


def _rows_part(ref, part, n_parts):
    h = ref.shape[0] // n_parts
    return ref.at[pl.ds(part * h, h), :]


def _gather_job(full, kind, stage):
    def build(in_refs, out_refs, cp):
        x, y, c, me, (y_nbr, x_nbr, diagonal) = _place()
        (ref,) = out_refs
        sibling = (x, y, 1 - c)
        if stage == 1:
            mine = _piece(ref, kind, me, c)
            return [cp.remote(mine, mine, (*y_nbr, c)), cp.remote(mine, mine, (*x_nbr, c))]
        from_y, from_x = _piece(ref, kind, _chip_index(y_nbr), c), _piece(ref, kind, _chip_index(x_nbr), c)
        if stage == 2:
            relay_0, relay_1 = _rows_part(from_x, 0, 2), _rows_part(from_y, 1, 2)
            return [cp.remote(relay_0, relay_0, (*y_nbr, c)), cp.remote(relay_1, relay_1, (*x_nbr, c)),
                    cp.remote(from_y, from_y, sibling), cp.remote(from_x, from_x, sibling)]
        from_diagonal = _piece(ref, kind, _chip_index(diagonal), c)
        return [cp.remote(from_diagonal, from_diagonal, sibling)]

    return _Job([full], *_in_place([full]), {1: 2, 2: 4, 3: 1}[stage], build)


def _gather_small_job(fulls, axes):
    def build(in_refs, out_refs, cp):
        x, y, c, me, chips = _place()
        copies = []
        for ref, ax in zip(out_refs, axes):
            n = ref.shape[ax] // N_CHIPS
            idx = [slice(None)] * len(ref.shape)
            idx[ax] = pl.ds(_aligned(me * n, n), n)
            mine = ref.at[tuple(idx)]
            copies += [cp.remote(mine, mine, (*chip, c)) for chip in chips]
        return copies

    return _Job(fulls, *_in_place(fulls), 3 * len(fulls), build)


def _exchange_halves_job(grads, kinds):
    def build(in_refs, out_refs, cp):
        x, y, c, me, chips = _place()
        copies = []
        for src, dst, kind in zip(in_refs, out_refs, kinds):
            if kind == "col":
                copies.append(cp.remote(_half_rows(src, 1 - c), dst, (x, y, 1 - c)))
            else:
                copies += [cp.remote(_piece(src, "row", k, 1 - c), _compact_piece(dst, "row", k), (x, y, 1 - c))
                           for k in range(N_CHIPS)]
        return copies

    out_shape = [jax.ShapeDtypeStruct((g.shape[0] // 2, g.shape[1]), g.dtype) for g in grads]
    return _Job(grads, out_shape, {}, sum(1 if k == "col" else N_CHIPS for k in kinds), build)


def _scatter_job(half, kind, n_parts, parts, into=None):
    def build(in_refs, out_refs, cp):
        x, y, c, me, chips = _place()
        src, (dst,) = in_refs[0], out_refs
        copies = []
        for p in parts:
            copies += [cp.remote(_rows_part(_compact_piece(src, kind, _chip_index(chip)), p, n_parts),
                                 _rows_part(dst.at[r], p, n_parts), (*chip, c)) for r, chip in enumerate(chips)]
        return copies

    part_shape = (half.shape[0], half.shape[1] // N_CHIPS) if kind == "col" else (half.shape[0] // N_CHIPS, half.shape[1])
    out_shape = [jax.ShapeDtypeStruct((N_CHIPS - 1,) + part_shape, half.dtype)]
    ins, aliases = ([half], {}) if into is None else ([half, into], {1: 0})
    return _Job(ins, out_shape, aliases, 3 * len(parts), build)


def _share_job(shards):
    def build(in_refs, out_refs, cp):
        x, y, c, me, chips = _place()
        copies = []
        for ref in out_refs:
            mine = _half_rows(ref, c)
            copies.append(cp.remote(mine, mine, (x, y, 1 - c)))
        return copies

    return _Job(shards, *_in_place(shards), len(shards), build)


N_DEVICES = 2 * N_CHIPS


def _small_exchange_job(slots, n_parts, parts):
    def build(in_refs, out_refs, cp):
        x, y, c, me, chips = _place()
        (ref,) = out_refs
        copies = []
        for part in parts:
            mine = _rows_part(ref.at[2 * me + c], part, n_parts)
            copies += [cp.remote(mine, mine, (x ^ (p >> 2), y ^ ((p >> 1) & 1), c ^ (p & 1))) for p in range(1, N_DEVICES)]
        return copies

    return _Job([slots], *_in_place([slots]), (N_DEVICES - 1) * len(parts), build)


def _sum_slots(slots, name):
    n, R, C = slots.shape

    def body(s_ref, o_ref):
        acc = s_ref[0]
        for i in range(1, n):
            acc = acc + s_ref[i]
        o_ref[...] = acc

    return _call(body,name=name, grid=(1,), in_specs=[pl.BlockSpec((n, R, C), lambda i: (0, 0, 0))],
                 out_specs=pl.BlockSpec((R, C), lambda i: (0, 0)), out_shape=jax.ShapeDtypeStruct((R, C), F32),
                 sem=("arbitrary",))(slots)


def _pack(arrays, width):
    rows = []
    for a in arrays:
        flat = a.reshape(-1)
        n_rows = -(-flat.shape[0] // width)
        rows.append(jnp.pad(flat, (0, n_rows * width - flat.shape[0])).reshape(n_rows, width))
    n = sum(r.shape[0] for r in rows)
    rows.append(jnp.zeros((-n % 8, width), F32))
    return jnp.concatenate(rows, axis=0)


def _unpack(packed, shapes):
    out, r0, width = [], 0, packed.shape[1]
    for shape in shapes:
        size = 1
        for d in shape:
            size *= d
        n_rows = -(-size // width)
        out.append(packed[r0:r0 + n_rows].reshape(-1)[:size].reshape(shape))
        r0 += n_rows
    return out


class _Backlog:
    def __init__(self, first):
        self.now, self.free, self.flights, self.last, self.chain = 0.0, {"ici": 0.0, "d2d": 0.0}, [], first, []

    def run(self, fn, us, *args, **kw):
        out = fn(*args, **kw)
        self.now += us
        self.last = out[0] if isinstance(out, (list, tuple)) else out
        self.poll()
        return out

    def start(self, job, name, link, cost, done):
        flying = _start_job(job, name + "_start", self.chain)
        self.chain = [flying.token]
        _Behind.pending.append(flying.token)
        ends = max(self.now, self.free[link]) + cost
        self.free[link] = ends
        self.flights.append((ends + LANDING_SLACK_US, name, flying, done))
        self.flights.sort(key=lambda f: f[0])

    def poll(self, block=False):
        while self.flights and (block or self.flights[0][0] <= self.now):
            ends, name, flying, done = self.flights.pop(0)
            self.now, block = max(self.now, ends), False
            done(*_wait_job(flying, name + "_wait", [self.last] + self.chain))


class _GatherStream:
    def __init__(self, backlog, bufs, kinds, costs):
        self.backlog, self.bufs, self.kinds, self.costs, self.begun, self.complete = backlog, bufs, kinds, costs, 0, set()
        self.begin()

    def begin(self):
        u, self.begun = self.begun, self.begun + 1
        self.backlog.start(_gather_job(self.bufs[u], self.kinds[u], 1), "gather_%d" % u, "ici", 0.5 * self.costs[u],
                           lambda outs, kept: self.arrived(u, outs[0]))

    def arrived(self, u, buf):
        self.bufs[u] = buf
        self.backlog.start(_gather_job(buf, self.kinds[u], 2), "relay_%d" % u, "ici", 0.25 * self.costs[u],
                           lambda outs, kept: self.relayed(u, outs[0]))
        while self.begun <= min(u + (1 if u < STARVED_WEIGHTS else 2), len(self.bufs) - 1):
            self.begin()

    def relayed(self, u, buf):
        self.bufs[u] = buf
        self.backlog.start(_gather_job(buf, self.kinds[u], 3), "handon_%d" % u, "d2d", D2D_SHARE * self.costs[u],
                           lambda outs, kept: self.handed(u, outs[0]))

    def handed(self, u, buf):
        self.bufs[u] = buf
        self.complete.add(u)

    def ready(self, u):
        while u not in self.complete:
            assert self.backlog.flights, "weight %d is not on its way" % u
            self.backlog.poll(block=True)
        return self.bufs[u]


class _GradStream:
    def __init__(self, backlog, u, name, kind, cost, g, place, results):
        self.backlog, self.u, self.name, self.kind, self.cost, self.place, self.results = backlog, u, name, kind, cost, place, results
        backlog.start(_exchange_halves_job([g], [kind]), "to_sibling_" + name, "d2d", D2D_SHARE * cost, self.exchanged)

    def exchanged(self, outs, kept):
        chip_sum = self.backlog.run(_add_pair, SIDE_KERNEL_US, kept[0], outs[0], self.kind, self.place, "chip_sum_" + self.name)
        self.backlog.start(_scatter_job(chip_sum, self.kind, 1, [0]), "to_owners_" + self.name, "ici", self.cost, self.scattered)

    def scattered(self, outs, kept):
        reduced = self.backlog.run(_sum_chips, SIDE_KERNEL_US, kept[0], outs[0], self.kind, self.place, "reduce_" + self.name)
        self.backlog.start(_share_job([reduced]), "share_" + self.name, "d2d", D2D_SHARE * self.cost, self.shared)

    def shared(self, outs, kept):
        self.results[self.u] = outs[0]


SIDE_KERNEL_US = 12.0
D2D_SHARE = 0.15
LANDING_SLACK_US = 5.0
STARVED_WEIGHTS = 4


def kernel(x, mix_pre_g, mix_post_g, ffn_pre_g, ffn_post_g, ab_w_in, pool_w, pool_scale, conv_w, conv_b, conv_ln_g, conv_ln_b, ab_w_out, sc_w_in, sc_conv_w, sc_w_out, ffn_w1, ffn_w2, loss_target, m_mix_pre_g, m_mix_post_g, m_ffn_pre_g, m_ffn_post_g, m_ab_w_in, m_pool_w, m_pool_scale, m_conv_w, m_conv_b, m_conv_ln_g, m_conv_ln_b, m_ab_w_out, m_sc_w_in, m_sc_conv_w, m_sc_w_out, m_ffn_w1, m_ffn_w2, v_mix_pre_g, v_mix_post_g, v_ffn_pre_g, v_ffn_post_g, v_ab_w_in, v_pool_w, v_pool_scale, v_conv_w, v_conv_b, v_conv_ln_g, v_conv_ln_b, v_ab_w_out, v_sc_w_in, v_sc_conv_w, v_sc_w_out, v_ffn_w1, v_ffn_w2):
    x0, target = x[0], loss_target[0]
    T, D = x0.shape
    DP = pool_scale.shape[-1]
    gain = lambda g, layer: g[layer][None, :]

    big = [("ab_w_in", ab_w_in, 0, "col", 4, 67.0), ("ab_w_out", ab_w_out, 0, "row", 2, 44.0),
           ("ffn_w1_0", ffn_w1, 0, "col", 8, 177.0), ("ffn_w2_0", ffn_w2, 0, "row", 8, 177.0),
           ("sc_w_in", sc_w_in, 0, "col", 4, 133.0), ("sc_w_out", sc_w_out, 0, "row", 2, 44.0),
           ("ffn_w1_1", ffn_w1, 1, "col", 8, 177.0), ("ffn_w2_1", ffn_w2, 1, "row", 8, 177.0)]
    kinds = [b[3] for b in big]
    chip = 2 * lax.axis_index("x") + lax.axis_index("y")
    place = jnp.stack([chip, lax.axis_index("c")]).astype(jnp.int32)

    def own_in_zeros(shard, ax):
        full = jnp.zeros(tuple(d * N_CHIPS if i == ax else d for i, d in enumerate(shard.shape)), shard.dtype)
        return lax.dynamic_update_slice_in_dim(full, shard, chip * shard.shape[ax], axis=ax)

    W = [_cast_into_full(w, layer, kind, place, "cast_" + name) for name, w, layer, kind, _, _ in big]
    smalls = [own_in_zeros(pool_w[0], 1), own_in_zeros(conv_w[0], 1), own_in_zeros(sc_conv_w[0], 1)]
    backlog = _Backlog(x0)
    run = backlog.run
    small_weights = []
    backlog.start(_gather_small_job(smalls, [1, 1, 1]), "gather_small", "ici", 6.0, lambda outs, kept: small_weights.extend(outs))
    gather = _GatherStream(backlog, W, kinds, [b[5] for b in big])

    relu_sq = lambda acc: (jnp.maximum(acc, 0.0), jnp.square(jnp.maximum(acc, 0.0)))
    relu_sq_bwd = lambda acc, a: (acc * (2.0 * a.astype(F32)),)

    h0 = run(_norm_fwd, 12.0, x0, gain(mix_pre_g, 0), "norm_in")
    z0 = run(_matmul, 35.0, h0, gather.ready(0), "nn", "mix0_in")
    while not small_weights:
        backlog.poll(block=True)
    pool_w_full, conv_w_full, sc_conv_w_full = small_weights
    pooled, y_pool = run(_pool_fwd, 23.0, z0, pool_w_full, pool_scale, "pool_fwd")
    a_conv, c_conv = run(_conv_fwd, 25.0, z0, conv_w_full, conv_b, DP, "conv_fwd")
    y_conv = run(_ln_silu_fwd, 10.0, c_conv, conv_ln_g, conv_ln_b, "ln_silu_fwd")
    y0 = jnp.concatenate([y_pool, y_conv], axis=1)
    m0 = run(_matmul, 25.0, y0, gather.ready(1), "nn", "mix0_out")
    x1, h1 = run(_residual_norm, 21.0, x0, m0, gain(mix_post_g, 0), gain(ffn_pre_g, 0), "res_mix0")
    a0, a0sq = run(_matmul, 81.0, h1, gather.ready(2), "nn", "ffn0_up", out_dtypes=(BF16, BF16), epilogue=relu_sq)
    f0 = run(_matmul, 84.0, a0sq, gather.ready(3), "nn", "ffn0_down", tk=LONG_K_TILE)
    x2, h2 = run(_residual_norm, 22.0, x1, f0, gain(ffn_post_g, 0), gain(mix_pre_g, 1), "res_ffn0")
    z1 = run(_matmul, 62.0, h2, gather.ready(4), "nn", "mix1_in")
    y1 = run(_short_fwd, 22.0, z1, sc_conv_w_full, "short_fwd")
    m1 = run(_matmul, 25.0, y1, gather.ready(5), "nn", "mix1_out")
    x3, h3 = run(_residual_norm, 21.0, x2, m1, gain(mix_post_g, 1), gain(ffn_pre_g, 1), "res_mix1")
    a1, a1sq = run(_matmul, 81.0, h3, gather.ready(6), "nn", "ffn1_up", out_dtypes=(BF16, BF16), epilogue=relu_sq)
    f1 = run(_matmul, 84.0, a1sq, gather.ready(7), "nn", "ffn1_down", tk=LONG_K_TILE)
    w_in0, w_out0, w1_0, w2_0, w_in1, w_out1, w1_1, w2_1 = W

    grads_big = [None] * len(big)


    def reduce_grad(u, g):
        name, _, _, kind, _, cost = big[u]
        _GradStream(backlog, u, name, kind, cost, g, place, grads_big)

    dx, df1, d_ffn_post_1, loss_row = run(_loss_and_last_norm_bwd, 30.0, x3, f1, gain(ffn_post_g, 1), target, "loss")
    reduce_grad(7, run(_matmul, 80.0, a1sq, df1, "tn", "ffn1_down_dw", out_dtypes=(BF16,)))
    dz = run(_matmul, 82.0, df1, w2_1, "nt", "ffn1_down_dx", out_dtypes=(BF16,), epilogue=relu_sq_bwd, epi=(a1,))
    reduce_grad(6, run(_matmul, 80.0, h3, dz, "tn", "ffn1_up_dw", out_dtypes=(BF16,)))
    dh = run(_matmul, 87.0, dz, w1_1, "nt", "ffn1_up_dx", tk=LONG_K_TILE)
    dx, d_ffn_pre_1, dm1, d_mix_post_1 = run(_norms_bwd, 36.0, dx, dh, x3, gain(ffn_pre_g, 1), m1, gain(mix_post_g, 1), "norms_bwd3")

    reduce_grad(5, run(_matmul, 24.0, y1, dm1, "tn", "mix1_out_dw", out_dtypes=(BF16,)))
    dy1 = run(_matmul, 25.0, dm1, w_out1, "nt", "mix1_out_dx")
    db, dcg, du, d_sc_conv_w = run(_short_bwd, 41.0, dy1, z1, sc_conv_w_full, "short_bwd")
    dz1 = jnp.concatenate([db, dcg, du], axis=1)
    reduce_grad(4, run(_matmul, 62.0, h2, dz1, "tn", "mix1_in_dw", out_dtypes=(BF16,)))
    dh = run(_matmul, 68.0, dz1, w_in1, "nt", "mix1_in_dx")
    dx, d_mix_pre_1, df0, d_ffn_post_0 = run(_norms_bwd, 35.0, dx, dh, x2, gain(mix_pre_g, 1), f0, gain(ffn_post_g, 0), "norms_bwd2")

    reduce_grad(3, run(_matmul, 80.0, a0sq, df0, "tn", "ffn0_down_dw", out_dtypes=(BF16,)))
    dz = run(_matmul, 82.0, df0, w2_0, "nt", "ffn0_down_dx", out_dtypes=(BF16,), epilogue=relu_sq_bwd, epi=(a0,))
    reduce_grad(2, run(_matmul, 80.0, h1, dz, "tn", "ffn0_up_dw", out_dtypes=(BF16,)))
    dh = run(_matmul, 87.0, dz, w1_0, "nt", "ffn0_up_dx", tk=LONG_K_TILE)
    dx, d_ffn_pre_0, dm0, d_mix_post_0 = run(_norms_bwd, 36.0, dx, dh, x1, gain(ffn_pre_g, 0), m0, gain(mix_post_g, 0), "norms_bwd1")

    reduce_grad(1, run(_matmul, 24.0, y0, dm0, "tn", "mix0_out_dw", out_dtypes=(BF16,)))
    dy0 = run(_matmul, 25.0, dm0, w_out0, "nt", "mix0_out_dx")
    du_pool, d_pool_w, d_pool_scale = run(_pool_bwd, 28.0, dy0, pooled, pool_w_full, pool_scale, "pool_bwd")
    dc, d_ln_g, d_ln_b = run(_ln_silu_bwd, 15.0, dy0, c_conv, conv_ln_g, conv_ln_b, "ln_silu_bwd")
    dv, dgate, d_conv_w, d_conv_b = run(_conv_bwd, 52.0, dc, a_conv, z0, conv_w_full, DP, "conv_bwd")
    dz0 = jnp.concatenate([du_pool, dv, dgate], axis=1)

    device_slot = 2 * chip + lax.axis_index("c")
    small_sums = {}

    def exchange_small(key, arrays, cost):
        slots = lax.dynamic_update_slice_in_dim(jnp.zeros((N_DEVICES,) + _pack(arrays, D).shape, F32), _pack(arrays, D)[None],
                                                device_slot, axis=0)
        backlog.start(_small_exchange_job(slots, 1, [0]), "small_grads_" + key, "ici", cost,
                      lambda outs, kept: small_sums.__setitem__(key, _unpack(_sum_slots(outs[0], "small_grads_sum_" + key),
                                                                             [a.shape for a in arrays])))

    exchange_small("most", [d_mix_pre_1, jnp.concatenate([d_mix_post_0, d_mix_post_1], 0),
                            jnp.concatenate([d_ffn_pre_0, d_ffn_pre_1], 0), jnp.concatenate([d_ffn_post_0, d_ffn_post_1], 0),
                            d_pool_scale, d_conv_b, d_ln_g, d_ln_b, d_pool_w, d_conv_w, d_sc_conv_w], 112.0)
    reduce_grad(0, run(_matmul, 34.0, h0, dz0, "tn", "mix0_in_dw", out_dtypes=(BF16,)))
    dh = run(_matmul, 40.0, dz0, w_in0, "nt", "mix0_in_dx")
    grad_x, d_mix_pre_0 = run(_norms_bwd, 26.0, dx, dh, x0, gain(mix_pre_g, 0), None, None, "norms_bwd0")
    exchange_small("last", [d_mix_pre_0], 5.0)
    loss = lax.psum(loss_row[0, 0], ("x", "y", "c"))

    upd, gr, first = {}, grads_big, {}

    def keep(where, key, outs):
        where[key] = outs
        return outs

    adamw_big = [
        (7, lambda: keep(first, "ffn_w2", _adamw(ffn_w2, gr[7], m_ffn_w2, v_ffn_w2, "adamw_ffn_w2_1", layer=1)), 46.0),
        (6, lambda: keep(first, "ffn_w1", _adamw(ffn_w1, gr[6], m_ffn_w1, v_ffn_w1, "adamw_ffn_w1_1", layer=1)), 46.0),
        (5, lambda: keep(upd, "sc_w_out", _adamw(sc_w_out, gr[5], m_sc_w_out, v_sc_w_out, "adamw_sc_w_out")), 14.0),
        (4, lambda: keep(upd, "sc_w_in", _adamw(sc_w_in, gr[4], m_sc_w_in, v_sc_w_in, "adamw_sc_w_in")), 35.0),
        (3, lambda: keep(upd, "ffn_w2", _adamw(ffn_w2, gr[3], m_ffn_w2, v_ffn_w2, "adamw_ffn_w2_0", layer=0,
                                               carried=first["ffn_w2"])), 46.0),
        (2, lambda: keep(upd, "ffn_w1", _adamw(ffn_w1, gr[2], m_ffn_w1, v_ffn_w1, "adamw_ffn_w1_0", layer=0,
                                               carried=first["ffn_w1"])), 46.0),
        (1, lambda: keep(upd, "ab_w_out", _adamw(ab_w_out, gr[1], m_ab_w_out, v_ab_w_out, "adamw_ab_w_out")), 14.0),
        (0, lambda: keep(upd, "ab_w_in", _adamw(ab_w_in, gr[0], m_ab_w_in, v_ab_w_in, "adamw_ab_w_in")), 19.0)]
    while adamw_big or backlog.flights:
        due = [a for a in adamw_big if gr[a[0]] is not None]
        if due:
            adamw_big.remove(due[0])
            backlog.run(due[0][1], due[0][2])
        else:
            backlog.poll(block=True)

    (g_mix_pre_1, g_mix_post, g_ffn_pre, g_ffn_post, g_pool_scale, g_conv_b, g_ln_g, g_ln_b, g_pool_w_full, g_conv_w_full,
     g_sc_conv_w_full) = small_sums["most"]
    g_mix_pre = jnp.concatenate([small_sums["last"][0], g_mix_pre_1], 0)
    own = lambda a, ax: lax.dynamic_slice_in_dim(a, chip * (a.shape[ax] // N_CHIPS), a.shape[ax] // N_CHIPS, axis=ax)
    g_pool_w, g_conv_w, g_sc_conv_w = own(g_pool_w_full, 1), own(g_conv_w_full, 1), own(g_sc_conv_w_full, 1)

    def small_update(w, g, m, v, name):
        shape = w.shape
        as3 = lambda a: a.reshape((1, -1, shape[-1]))
        outs = _adamw(as3(w), g.reshape((-1, shape[-1])), as3(m), as3(v), "adamw_" + name)
        return [o.reshape(shape) for o in outs]

    upd["mix_pre_g"] = small_update(mix_pre_g, g_mix_pre, m_mix_pre_g, v_mix_pre_g, "mix_pre_g")
    upd["mix_post_g"] = small_update(mix_post_g, g_mix_post, m_mix_post_g, v_mix_post_g, "mix_post_g")
    upd["ffn_pre_g"] = small_update(ffn_pre_g, g_ffn_pre, m_ffn_pre_g, v_ffn_pre_g, "ffn_pre_g")
    upd["ffn_post_g"] = small_update(ffn_post_g, g_ffn_post, m_ffn_post_g, v_ffn_post_g, "ffn_post_g")
    upd["pool_w"] = small_update(pool_w, g_pool_w, m_pool_w, v_pool_w, "pool_w")
    upd["pool_scale"] = small_update(pool_scale, g_pool_scale, m_pool_scale, v_pool_scale, "pool_scale")
    upd["conv_w"] = small_update(conv_w, g_conv_w, m_conv_w, v_conv_w, "conv_w")
    upd["conv_b"] = small_update(conv_b, g_conv_b, m_conv_b, v_conv_b, "conv_b")
    upd["conv_ln_g"] = small_update(conv_ln_g, g_ln_g, m_conv_ln_g, v_conv_ln_g, "conv_ln_g")
    upd["conv_ln_b"] = small_update(conv_ln_b, g_ln_b, m_conv_ln_b, v_conv_ln_b, "conv_ln_b")
    upd["sc_conv_w"] = small_update(sc_conv_w, g_sc_conv_w, m_sc_conv_w, v_sc_conv_w, "sc_conv_w")

    order = ["mix_pre_g", "mix_post_g", "ffn_pre_g", "ffn_post_g", "ab_w_in", "pool_w", "pool_scale", "conv_w", "conv_b",
             "conv_ln_g", "conv_ln_b", "ab_w_out", "sc_w_in", "sc_conv_w", "sc_w_out", "ffn_w1", "ffn_w2"]
    out = [loss, grad_x[None]]
    for part in range(4):
        out += [upd[n][part] for n in order]
    return tuple(out)
```

```python
import jax
import jax.numpy as jnp
from jax import lax
from jax.experimental import pallas as pl
from jax.experimental.pallas import tpu as pltpu

F32, BF16 = jnp.float32, jnp.bfloat16
EPS = 1e-6
N_GROUPS = 4
MAX_WINDOW = 16
CONV_K = 31
SHORT_K = 3
CONV_PAD = 32
SHORT_PAD = 8
ADAM_LR, ADAM_B1, ADAM_B2, ADAM_EPS, ADAM_WD, ADAM_STEP = 0.001, 0.9, 0.999, 1e-08, 0.01, 10
N_CHIPS = 4
VMEM_LIMIT_BYTES = 56 * 1024 * 1024
ROW_TILE = 256
CHUNK = 256
LONG_K_TILE = 4096
MESH = pl.DeviceIdType.MESH
HBM = pl.BlockSpec(memory_space=pltpu.HBM)
SEM = pl.BlockSpec(memory_space=pltpu.SEMAPHORE)
ANY = pl.BlockSpec(memory_space=pl.ANY)


def _cp(*sem):
    return pltpu.CompilerParams(dimension_semantics=sem, vmem_limit_bytes=VMEM_LIMIT_BYTES)


def _sigmoid(v):
    return 1.0 / (1.0 + jnp.exp(-v))


class _Behind:
    pending = []


def _call(body, prefetch=None, **kw):
    behind, _Behind.pending = _Behind.pending, []
    single = not isinstance(kw["out_shape"], (list, tuple))
    in_specs, scratch = list(kw["in_specs"]), list(kw.get("scratch_shapes", ()))
    out_shape = [kw["out_shape"]] if single else list(kw["out_shape"])
    out_specs = [kw["out_specs"]] if single else list(kw["out_specs"])
    n_pre = 0 if prefetch is None else 1
    n_own, n_behind = len(in_specs), len(behind)

    def wrapped(*refs):
        body(*refs[:n_pre + n_own], *refs[n_pre + n_own + n_behind:])

    specs = dict(grid=kw["grid"], in_specs=in_specs + [ANY] * n_behind, out_specs=out_specs)
    if prefetch is None:
        specs["scratch_shapes"] = scratch
    else:
        specs = dict(grid_spec=pltpu.PrefetchScalarGridSpec(num_scalar_prefetch=1, scratch_shapes=scratch, **specs))
    aliases = {n_pre + i: o for i, o in kw.get("input_output_aliases", {}).items()}
    call = pl.pallas_call(wrapped, name=kw["name"], out_shape=out_shape, input_output_aliases=aliases,
                          compiler_params=_cp(*kw["sem"]), **specs)

    def run(*args):
        outs = call(*([prefetch] * n_pre), *args, *behind)
        return outs[0] if single else list(outs)

    return run


_DIMS = {"nn": (((1,), (0,)), ((), ())), "nt": (((1,), (1,)), ((), ())), "tn": (((0,), (0,)), ((), ()))}


def _pick(n, cap, step=256):
    if n <= cap:
        return n
    return next(t for t in range(cap - cap % step, 0, -step) if n % t == 0)


def _matmul(a, b, mode, name, out_dtypes=(F32,), epilogue=None, epi=(), tm=1024, tn=1024, tk=2048):
    if mode == "tn":
        (K, M), (K2, N) = a.shape, b.shape
    elif mode == "nt":
        (M, K), (N, K2) = a.shape, b.shape
    else:
        (M, K), (K2, N) = a.shape, b.shape
    assert K == K2
    tm, tn, tk = _pick(M, tm), _pick(N, tn), _pick(K, tk)
    nk = K // tk
    a_spec = pl.BlockSpec((tk, tm), lambda i, j, k: (k, i)) if mode == "tn" else pl.BlockSpec((tm, tk), lambda i, j, k: (i, k))
    b_spec = pl.BlockSpec((tn, tk), lambda i, j, k: (j, k)) if mode == "nt" else pl.BlockSpec((tk, tn), lambda i, j, k: (k, j))
    o_spec = pl.BlockSpec((tm, tn), lambda i, j, k: (i, j))
    n_epi, n_out = len(epi), len(out_dtypes)

    def body(a_ref, b_ref, *rest):
        epi_refs, out_refs, scratch = rest[:n_epi], rest[n_epi:n_epi + n_out], rest[n_epi + n_out:]
        part = lax.dot_general(a_ref[...].astype(BF16), b_ref[...].astype(BF16), _DIMS[mode], preferred_element_type=F32)

        def finish(acc):
            outs = epilogue(acc, *[r[...] for r in epi_refs]) if epilogue else (acc,)
            for o_ref, o in zip(out_refs, outs):
                o_ref[...] = o.astype(o_ref.dtype)

        if nk == 1:
            finish(part)
        else:
            acc_ref = scratch[0]
            k = pl.program_id(2)

            @pl.when(k == 0)
            def _():
                acc_ref[...] = part

            @pl.when(k > 0)
            def _():
                acc_ref[...] += part

            @pl.when(k == nk - 1)
            def _():
                finish(acc_ref[...])

    outs = _call(
        body, name=name, grid=(M // tm, N // tn, nk),
        in_specs=[a_spec, b_spec] + [o_spec] * n_epi, out_specs=[o_spec] * n_out,
        out_shape=[jax.ShapeDtypeStruct((M, N), dt) for dt in out_dtypes],
        scratch_shapes=[pltpu.VMEM((tm, tn), F32)] if nk > 1 else [],
        sem=("parallel", "parallel", "arbitrary"))(a, b, *epi)
    return outs[0] if n_out == 1 else outs


def _rms(x, g):
    r = lax.rsqrt(jnp.mean(x * x, axis=-1, keepdims=True) + EPS)
    return x * r * g


def _rms_bwd(dy, x, g):
    r = lax.rsqrt(jnp.mean(x * x, axis=-1, keepdims=True) + EPS)
    xn = x * r
    dyg = dy * g
    dx = r * (dyg - xn * jnp.mean(dyg * xn, axis=-1, keepdims=True))
    return dx, jnp.sum(dy * xn, axis=0, keepdims=True)


def _rows(d, tr=ROW_TILE):
    return pl.BlockSpec((tr, d), lambda i: (i, 0))


def _vec(d):
    return pl.BlockSpec((1, d), lambda i: (0, 0))


def _accumulate(ref, val):
    @pl.when(pl.program_id(0) == 0)
    def _():
        ref[...] = val

    @pl.when(pl.program_id(0) > 0)
    def _():
        ref[...] += val


def _norm_fwd(x, g, name):
    T, D = x.shape

    def body(x_ref, g_ref, h_ref):
        h_ref[...] = _rms(x_ref[...], g_ref[...]).astype(BF16)

    return _call(body,name=name,grid=(T // ROW_TILE,), in_specs=[_rows(D), _vec(D)], out_specs=_rows(D),
                          out_shape=jax.ShapeDtypeStruct((T, D), BF16), sem=("parallel",))(x, g)


def _residual_norm(x, m, g_post, g_next, name):
    T, D = x.shape

    def body(x_ref, m_ref, gp_ref, gn_ref, xo_ref, h_ref):
        xo = x_ref[...] + _rms(m_ref[...], gp_ref[...])
        xo_ref[...] = xo
        h_ref[...] = _rms(xo, gn_ref[...]).astype(BF16)

    return _call(body,name=name,grid=(T // ROW_TILE,), in_specs=[_rows(D), _rows(D), _vec(D), _vec(D)],
                          out_specs=[_rows(D), _rows(D)],
                          out_shape=[jax.ShapeDtypeStruct((T, D), F32), jax.ShapeDtypeStruct((T, D), BF16)],
                          sem=("parallel",))(x, m, g_post, g_next)


def _loss_and_last_norm_bwd(x, m, g_post, target, name):
    T, D = x.shape

    def body(x_ref, m_ref, gp_ref, t_ref, dx_ref, dm_ref, dg_ref, loss_ref):
        m_val, gp = m_ref[...], gp_ref[...]
        err = x_ref[...] + _rms(m_val, gp) - t_ref[...]
        dx = err * (1.0 / D)
        dx_ref[...] = dx
        dm, dg = _rms_bwd(dx, m_val, gp)
        dm_ref[...] = dm.astype(BF16)
        _accumulate(dg_ref, dg)
        _accumulate(loss_ref, jnp.full((1, 128), 0.5 * jnp.sum(err * err) * (1.0 / D), F32))

    return _call(body,name=name,grid=(T // ROW_TILE,), in_specs=[_rows(D), _rows(D), _vec(D), _rows(D)],
                          out_specs=[_rows(D), _rows(D), _vec(D), _vec(128)],
                          out_shape=[jax.ShapeDtypeStruct((T, D), F32), jax.ShapeDtypeStruct((T, D), BF16),
                                     jax.ShapeDtypeStruct((1, D), F32), jax.ShapeDtypeStruct((1, 128), F32)],
                          sem=("arbitrary",))(x, m, g_post, target)


def _norms_bwd(dx, dh, x_in, g_pre, m_prev, g_post_prev, name):
    T, D = dx.shape
    with_prev = m_prev is not None

    def body(*refs):
        if with_prev:
            dx_ref, dh_ref, x_ref, gq_ref, m_ref, gp_ref, dxo_ref, dgq_ref, dm_ref, dgp_ref = refs
        else:
            dx_ref, dh_ref, x_ref, gq_ref, dxo_ref, dgq_ref = refs
        d_in, dgq = _rms_bwd(dh_ref[...], x_ref[...], gq_ref[...])
        dxo = dx_ref[...] + d_in
        dxo_ref[...] = dxo
        _accumulate(dgq_ref, dgq)
        if with_prev:
            dm, dgp = _rms_bwd(dxo, m_ref[...], gp_ref[...])
            dm_ref[...] = dm.astype(BF16)
            _accumulate(dgp_ref, dgp)

    ins, in_specs = [dx, dh, x_in, g_pre], [_rows(D), _rows(D), _rows(D), _vec(D)]
    out_specs = [_rows(D), _vec(D)]
    out_shape = [jax.ShapeDtypeStruct((T, D), F32), jax.ShapeDtypeStruct((1, D), F32)]
    if with_prev:
        ins += [m_prev, g_post_prev]
        in_specs += [_rows(D), _vec(D)]
        out_specs += [_rows(D), _vec(D)]
        out_shape += [jax.ShapeDtypeStruct((T, D), BF16), jax.ShapeDtypeStruct((1, D), F32)]
    return _call(body,name=name,grid=(T // ROW_TILE,), in_specs=in_specs, out_specs=out_specs, out_shape=out_shape,
                          sem=("arbitrary",))(*ins)


def _window_weights(g):
    w = 2 << g
    return w, [jnp.where(j < w, 1.0, 0.0).astype(F32) for j in range(MAX_WINDOW)]


def _valid_count(r0, rows, w):
    t = (lax.broadcasted_iota(jnp.int32, (rows, 1), 0) + (r0 + 1)).astype(F32)
    return jnp.minimum(t, w.astype(F32))


def _pool_fwd(z, pool_w, pool_scale, name):
    T = z.shape[0]
    PG = pool_w.shape[-1]
    DP = N_GROUPS * PG
    rc = min(CHUNK, T)

    def body(u_ref, pw_ref, sc_ref, pooled_ref, y_ref, pad):
        w, wts = _window_weights(pl.program_id(0))
        pad[pl.ds(0, MAX_WINDOW), :] = jnp.zeros((MAX_WINDOW, PG), F32)
        pad[pl.ds(MAX_WINDOW, T), :] = u_ref[...]
        for r0 in range(0, T, rc):
            acc = jnp.zeros((rc, PG), F32)
            for j in range(MAX_WINDOW):
                acc = acc + wts[j] * pad[pl.ds(MAX_WINDOW + r0 - j, rc), :]
            pooled = acc / _valid_count(r0, rc, w) - u_ref[pl.ds(r0, rc), :]
            pooled_ref[pl.ds(r0, rc), :] = pooled.astype(BF16)
        mixed = jnp.dot(pooled_ref[...], pw_ref[...].astype(BF16), preferred_element_type=F32)
        y_ref[...] = (mixed * sc_ref[...]).astype(BF16)

    col = lambda g: (0, g)
    return _call(
        body, name=name,grid=(N_GROUPS,),
        in_specs=[pl.BlockSpec((T, PG), col), pl.BlockSpec((None, PG, PG), lambda g: (g, 0, 0)), pl.BlockSpec((1, PG), col)],
        out_specs=[pl.BlockSpec((T, PG), col), pl.BlockSpec((T, PG), col)],
        out_shape=[jax.ShapeDtypeStruct((T, DP), BF16), jax.ShapeDtypeStruct((T, DP), BF16)],
        scratch_shapes=[pltpu.VMEM((T + MAX_WINDOW, PG), F32)], sem=("parallel",))(z, pool_w, pool_scale)


def _pool_bwd(dy, pooled, pool_w, pool_scale, name):
    T = dy.shape[0]
    PG = pool_w.shape[-1]
    DP = N_GROUPS * PG
    rc = min(CHUNK, T)

    def body(dy_ref, pooled_ref, pw_ref, sc_ref, du_ref, dpw_ref, dsc_ref, pad, dp_ref):
        w, wts = _window_weights(pl.program_id(0))
        pooled_v, pw = pooled_ref[...], pw_ref[...].astype(BF16)
        dy_v = dy_ref[...]
        mixed = jnp.dot(pooled_v, pw, preferred_element_type=F32)
        dsc_ref[...] = jnp.sum(dy_v * mixed, axis=0, keepdims=True)
        dmixed = (dy_v * sc_ref[...]).astype(BF16)
        dpw_ref[...] = lax.dot_general(pooled_v, dmixed, _DIMS["tn"], preferred_element_type=F32)
        dp_ref[...] = lax.dot_general(dmixed, pw, _DIMS["nt"], preferred_element_type=F32)
        pad[pl.ds(T, MAX_WINDOW), :] = jnp.zeros((MAX_WINDOW, PG), F32)
        for r0 in range(0, T, rc):
            pad[pl.ds(r0, rc), :] = dp_ref[pl.ds(r0, rc), :] / _valid_count(r0, rc, w)
        for r0 in range(0, T, rc):
            acc = jnp.zeros((rc, PG), F32)
            for j in range(MAX_WINDOW):
                acc = acc + wts[j] * pad[pl.ds(r0 + j, rc), :]
            du_ref[pl.ds(r0, rc), :] = (acc - dp_ref[pl.ds(r0, rc), :]).astype(BF16)

    col = lambda g: (0, g)
    return _call(
        body, name=name,grid=(N_GROUPS,),
        in_specs=[pl.BlockSpec((T, PG), col), pl.BlockSpec((T, PG), col), pl.BlockSpec((None, PG, PG), lambda g: (g, 0, 0)),
                  pl.BlockSpec((1, PG), col)],
        out_specs=[pl.BlockSpec((T, PG), col), pl.BlockSpec((None, PG, PG), lambda g: (g, 0, 0)), pl.BlockSpec((1, PG), col)],
        out_shape=[jax.ShapeDtypeStruct((T, DP), BF16), jax.ShapeDtypeStruct((N_GROUPS, PG, PG), F32),
                   jax.ShapeDtypeStruct((1, DP), F32)],
        scratch_shapes=[pltpu.VMEM((T + MAX_WINDOW, PG), F32), pltpu.VMEM((T, PG), F32)],
        sem=("parallel",))(dy, pooled, pool_w, pool_scale)


def _conv_fwd(z, conv_w, conv_b, d_pool, name, tc=128):
    T = z.shape[0]
    DC = conv_w.shape[-1]
    rc = min(CHUNK, T)
    v0, g0 = d_pool // tc, (d_pool + DC) // tc

    def body(v_ref, gt_ref, w_ref, b_ref, a_ref, c_ref, pad):
        pad[pl.ds(0, CONV_PAD), :] = jnp.zeros((CONV_PAD, tc), F32)
        for r0 in range(0, T, rc):
            a = v_ref[pl.ds(r0, rc), :] * _sigmoid(gt_ref[pl.ds(r0, rc), :])
            a_ref[pl.ds(r0, rc), :] = a
            pad[pl.ds(CONV_PAD + r0, rc), :] = a
        for r0 in range(0, T, rc):
            acc = jnp.zeros((rc, tc), F32) + b_ref[...]
            for k in range(CONV_K):
                acc = acc + w_ref[pl.ds(k, 1), :] * pad[pl.ds(CONV_PAD - (CONV_K - 1) + k + r0, rc), :]
            c_ref[pl.ds(r0, rc), :] = acc

    col = lambda j: (0, j)
    return _call(
        body, name=name,grid=(DC // tc,),
        in_specs=[pl.BlockSpec((T, tc), lambda j: (0, v0 + j)), pl.BlockSpec((T, tc), lambda j: (0, g0 + j)),
                  pl.BlockSpec((CONV_K, tc), col), pl.BlockSpec((1, tc), col)],
        out_specs=[pl.BlockSpec((T, tc), col), pl.BlockSpec((T, tc), col)],
        out_shape=[jax.ShapeDtypeStruct((T, DC), F32), jax.ShapeDtypeStruct((T, DC), F32)],
        scratch_shapes=[pltpu.VMEM((T + CONV_PAD, tc), F32)], sem=("parallel",))(z, z, conv_w, conv_b)


def _conv_bwd(dc, a, z, conv_w, d_pool, name, tc=128):
    T, DC = dc.shape
    rc = min(CHUNK, T)
    v0, g0 = d_pool // tc, (d_pool + DC) // tc

    def body(dc_ref, a_ref, v_ref, gt_ref, w_ref, dv_ref, dg_ref, dw_ref, db_ref, apad, dpad):
        apad[pl.ds(0, CONV_PAD), :] = jnp.zeros((CONV_PAD, tc), F32)
        apad[pl.ds(CONV_PAD, T), :] = a_ref[...]
        dpad[pl.ds(0, T), :] = dc_ref[...]
        dpad[pl.ds(T, CONV_PAD), :] = jnp.zeros((CONV_PAD, tc), F32)
        db_ref[...] = jnp.sum(dc_ref[...], axis=0, keepdims=True)
        for k in range(CONV_K):
            acc = jnp.zeros((8, tc), F32)
            for r0 in range(0, T, rc):
                prod = dc_ref[pl.ds(r0, rc), :] * apad[pl.ds(CONV_PAD - (CONV_K - 1) + k + r0, rc), :]
                acc = acc + jnp.sum(prod.reshape(rc // 8, 8, tc), axis=0)
            dw_ref[pl.ds(k, 1), :] = jnp.sum(acc, axis=0, keepdims=True)
        for r0 in range(0, T, rc):
            da = jnp.zeros((rc, tc), F32)
            for k in range(CONV_K):
                da = da + w_ref[pl.ds(k, 1), :] * dpad[pl.ds(r0 + (CONV_K - 1) - k, rc), :]
            sig = _sigmoid(gt_ref[pl.ds(r0, rc), :])
            dv_ref[pl.ds(r0, rc), :] = (da * sig).astype(BF16)
            dg_ref[pl.ds(r0, rc), :] = (da * v_ref[pl.ds(r0, rc), :] * sig * (1.0 - sig)).astype(BF16)

    col = lambda j: (0, j)
    return _call(
        body, name=name,grid=(DC // tc,),
        in_specs=[pl.BlockSpec((T, tc), col), pl.BlockSpec((T, tc), col), pl.BlockSpec((T, tc), lambda j: (0, v0 + j)),
                  pl.BlockSpec((T, tc), lambda j: (0, g0 + j)), pl.BlockSpec((CONV_K, tc), col)],
        out_specs=[pl.BlockSpec((T, tc), col), pl.BlockSpec((T, tc), col), pl.BlockSpec((CONV_K, tc), col),
                   pl.BlockSpec((1, tc), col)],
        out_shape=[jax.ShapeDtypeStruct((T, DC), BF16), jax.ShapeDtypeStruct((T, DC), BF16),
                   jax.ShapeDtypeStruct((CONV_K, DC), F32), jax.ShapeDtypeStruct((1, DC), F32)],
        scratch_shapes=[pltpu.VMEM((T + CONV_PAD, tc), F32), pltpu.VMEM((T + CONV_PAD, tc), F32)],
        sem=("parallel",))(dc, a, z, z, conv_w)


def _layer_norm_parts(c, g, b):
    mu = jnp.mean(c, axis=-1, keepdims=True)
    xc = c - mu
    rstd = lax.rsqrt(jnp.mean(xc * xc, axis=-1, keepdims=True) + EPS)
    xhat = xc * rstd
    return xhat, rstd, xhat * g + b


def _ln_silu_fwd(c, g, b, name):
    T, DC = c.shape

    def body(c_ref, g_ref, b_ref, y_ref):
        _, _, ln = _layer_norm_parts(c_ref[...], g_ref[...], b_ref[...])
        y_ref[...] = (ln * _sigmoid(ln)).astype(BF16)

    return _call(body,name=name,grid=(T // ROW_TILE,), in_specs=[_rows(DC), _vec(DC), _vec(DC)], out_specs=_rows(DC),
                          out_shape=jax.ShapeDtypeStruct((T, DC), BF16), sem=("parallel",))(c, g, b)


def _ln_silu_bwd(dy, c, g, b, name):
    T, DC = c.shape

    def body(dy_ref, c_ref, g_ref, b_ref, dc_ref, dg_ref, db_ref):
        gain = g_ref[...]
        xhat, rstd, ln = _layer_norm_parts(c_ref[...], gain, b_ref[...])
        s = _sigmoid(ln)
        dln = dy_ref[...] * (s * (1.0 + ln * (1.0 - s)))
        _accumulate(dg_ref, jnp.sum(dln * xhat, axis=0, keepdims=True))
        _accumulate(db_ref, jnp.sum(dln, axis=0, keepdims=True))
        dxh = dln * gain
        dc_ref[...] = rstd * (dxh - jnp.mean(dxh, axis=-1, keepdims=True) - xhat * jnp.mean(dxh * xhat, axis=-1, keepdims=True))

    return _call(body,name=name,grid=(T // ROW_TILE,),
                          in_specs=[pl.BlockSpec((ROW_TILE, DC), lambda i: (i, 1)), _rows(DC), _vec(DC), _vec(DC)],
                          out_specs=[_rows(DC), _vec(DC), _vec(DC)],
                          out_shape=[jax.ShapeDtypeStruct((T, DC), F32), jax.ShapeDtypeStruct((1, DC), F32),
                                     jax.ShapeDtypeStruct((1, DC), F32)],
                          sem=("arbitrary",))(dy, c, g, b)


def _short_specs(T, DS, tc):
    n = DS // tc
    return [pl.BlockSpec((T, tc), lambda j: (0, j)), pl.BlockSpec((T, tc), lambda j: (0, n + j)),
            pl.BlockSpec((T, tc), lambda j: (0, 2 * n + j))]


def _short_fwd(z, w, name, tc=256):
    T = z.shape[0]
    DS = w.shape[-1]
    rc = min(CHUNK, T)

    def body(b_ref, cg_ref, u_ref, w_ref, y_ref, pad):
        pad[pl.ds(0, SHORT_PAD), :] = jnp.zeros((SHORT_PAD, tc), F32)
        pad[pl.ds(SHORT_PAD, T), :] = cg_ref[...] * u_ref[...]
        for r0 in range(0, T, rc):
            r = jnp.zeros((rc, tc), F32)
            for k in range(SHORT_K):
                r = r + w_ref[pl.ds(k, 1), :] * pad[pl.ds(SHORT_PAD - (SHORT_K - 1) + k + r0, rc), :]
            y_ref[pl.ds(r0, rc), :] = (b_ref[pl.ds(r0, rc), :] * r).astype(BF16)

    col = lambda j: (0, j)
    return _call(body,name=name,grid=(DS // tc,), in_specs=_short_specs(T, DS, tc) + [pl.BlockSpec((SHORT_K, tc), col)],
                          out_specs=pl.BlockSpec((T, tc), col), out_shape=jax.ShapeDtypeStruct((T, DS), BF16),
                          scratch_shapes=[pltpu.VMEM((T + SHORT_PAD, tc), F32)], sem=("parallel",))(z, z, z, w)


def _short_bwd(dy, z, w, name, tc=256):
    T, DS = dy.shape
    rc = min(CHUNK, T)

    def body(dy_ref, b_ref, cg_ref, u_ref, w_ref, db_ref, dcg_ref, du_ref, dw_ref, qpad, rpad):
        qpad[pl.ds(0, SHORT_PAD), :] = jnp.zeros((SHORT_PAD, tc), F32)
        qpad[pl.ds(SHORT_PAD, T), :] = cg_ref[...] * u_ref[...]
        rpad[pl.ds(0, T), :] = dy_ref[...] * b_ref[...]
        rpad[pl.ds(T, SHORT_PAD), :] = jnp.zeros((SHORT_PAD, tc), F32)
        accs = [jnp.zeros((8, tc), F32) for _ in range(SHORT_K)]
        for r0 in range(0, T, rc):
            r = jnp.zeros((rc, tc), F32)
            dq = jnp.zeros((rc, tc), F32)
            dr = rpad[pl.ds(r0, rc), :]
            for k in range(SHORT_K):
                q_k = qpad[pl.ds(SHORT_PAD - (SHORT_K - 1) + k + r0, rc), :]
                r = r + w_ref[pl.ds(k, 1), :] * q_k
                dq = dq + w_ref[pl.ds(k, 1), :] * rpad[pl.ds(r0 + (SHORT_K - 1) - k, rc), :]
                accs[k] = accs[k] + jnp.sum((dr * q_k).reshape(rc // 8, 8, tc), axis=0)
            db_ref[pl.ds(r0, rc), :] = (dy_ref[pl.ds(r0, rc), :] * r).astype(BF16)
            dcg_ref[pl.ds(r0, rc), :] = (dq * u_ref[pl.ds(r0, rc), :]).astype(BF16)
            du_ref[pl.ds(r0, rc), :] = (dq * cg_ref[pl.ds(r0, rc), :]).astype(BF16)
        for k in range(SHORT_K):
            dw_ref[pl.ds(k, 1), :] = jnp.sum(accs[k], axis=0, keepdims=True)

    col = lambda j: (0, j)
    tile = pl.BlockSpec((T, tc), col)
    return _call(body,name=name,grid=(DS // tc,),
                          in_specs=[tile] + _short_specs(T, DS, tc) + [pl.BlockSpec((SHORT_K, tc), col)],
                          out_specs=[tile, tile, tile, pl.BlockSpec((SHORT_K, tc), col)],
                          out_shape=[jax.ShapeDtypeStruct((T, DS), BF16)] * 3 + [jax.ShapeDtypeStruct((SHORT_K, DS), F32)],
                          scratch_shapes=[pltpu.VMEM((T + SHORT_PAD, tc), F32), pltpu.VMEM((T + SHORT_PAD, tc), F32)],
                          sem=("parallel",))(dy, z, z, z, w)


def _tile_rows(rows, cols, n_bufs):
    budget = VMEM_LIMIT_BYTES * 3 // 4 // (2 * n_bufs * 4 * cols)
    tr = rows
    while tr > budget and tr % 16 == 0:
        tr //= 2
    return tr


def _placed_call(body, name, place, grid, in_specs, out_specs, out_shape, ins):
    return _call(body,prefetch=place, name=name, grid=grid, in_specs=in_specs, out_specs=out_specs, out_shape=out_shape,
                 sem=("parallel",))(*ins)


def _cast_into_full(w, layer, kind, place, name):
    _, R, C = w.shape
    tr = _tile_rows(R, C, 2)
    nb = R // tr
    if kind == "col":
        full, out_spec = (R, C * N_CHIPS), pl.BlockSpec((tr, C), lambda i, s: (i, s[0]))
    else:
        full, out_spec = (R * N_CHIPS, C), pl.BlockSpec((tr, C), lambda i, s: (s[0] * nb + i, 0))

    def body(s_ref, w_ref, o_ref):
        o_ref[...] = w_ref[...].astype(BF16)

    return _placed_call(body, name, place, (nb,), [pl.BlockSpec((None, tr, C), lambda i, s: (layer, i, 0))], out_spec,
                        jax.ShapeDtypeStruct(full, BF16), [w])


def _add_pair(grad, theirs, kind, place, name):
    R, C = grad.shape
    piece_rows = R // 2 if kind == "col" else R // N_CHIPS // 2
    tr = _tile_rows(piece_rows, C, 3)
    nb = piece_rows // tr
    if kind == "col":
        g_spec = pl.BlockSpec((tr, C), lambda i, s: (s[1] * nb + i, 0))
    else:
        g_spec = pl.BlockSpec((tr, C), lambda i, s: ((2 * (i // nb) + s[1]) * nb + i % nb, 0))
    flat = pl.BlockSpec((tr, C), lambda i, s: (i, 0))

    def body(s_ref, a_ref, b_ref, o_ref):
        o_ref[...] = (a_ref[...].astype(F32) + b_ref[...].astype(F32)).astype(BF16)

    return _placed_call(body, name, place, (R // 2 // tr,), [g_spec, flat], flat, jax.ShapeDtypeStruct((R // 2, C), BF16),
                        [grad, theirs])


def _sum_chips(chip_sum, arrived, kind, place, name):
    _, H, W = arrived.shape
    tr = _tile_rows(H, W, 6)
    nb = H // tr
    if kind == "col":
        own_spec = pl.BlockSpec((tr, W), lambda i, s: (i, s[0]))
    else:
        own_spec = pl.BlockSpec((tr, W), lambda i, s: (s[0] * nb + i, 0))

    def body(s_ref, p_ref, r_ref, o_ref):
        acc = p_ref[...].astype(F32)
        for i in range(N_CHIPS - 1):
            acc = acc + r_ref[i].astype(F32)
        o_ref[...] = acc

    return _placed_call(body, name, place, (nb,), [own_spec, pl.BlockSpec((N_CHIPS - 1, tr, W), lambda i, s: (0, i, 0))],
                        pl.BlockSpec((tr, W), lambda i, s: (s[1] * nb + i, 0)), jax.ShapeDtypeStruct((2 * H, W), F32),
                        [chip_sum, arrived])


def _adamw_values(w, g, m, v):
    m = ADAM_B1 * m + (1.0 - ADAM_B1) * g
    v = ADAM_B2 * v + (1.0 - ADAM_B2) * (g * g)
    m_hat = m / (1.0 - ADAM_B1 ** ADAM_STEP)
    v_hat = v / (1.0 - ADAM_B2 ** ADAM_STEP)
    return -ADAM_LR * (m_hat / (jnp.sqrt(v_hat) + ADAM_EPS) + ADAM_WD * w), m, v


def _adamw(w, g, m, v, name, layer=0, carried=None):
    L, R, C = w.shape
    tr = _tile_rows(R, C, 8)

    def body(w_ref, g_ref, m_ref, v_ref, *rest):
        go_ref, d_ref, mo_ref, vo_ref = rest[-4:]
        g_val = g_ref[...]
        d, m_new, v_new = _adamw_values(w_ref[...], g_val, m_ref[...], v_ref[...])
        go_ref[...], d_ref[...], mo_ref[...], vo_ref[...] = g_val, d, m_new, v_new

    lay = pl.BlockSpec((None, tr, C), lambda i: (layer, i, 0))
    ins = [w, g, m, v]
    in_specs = [lay, pl.BlockSpec((tr, C), lambda i: (i, 0)), lay, lay]
    aliases = {}
    if carried is not None:
        ins += list(carried)
        in_specs += [pl.BlockSpec(memory_space=pl.ANY)] * 4
        aliases = {4 + i: i for i in range(4)}
    return _call(body, name=name, grid=(R // tr,), in_specs=in_specs, out_specs=[lay] * 4,
                 out_shape=[jax.ShapeDtypeStruct((L, R, C), F32)] * 4, input_output_aliases=aliases, sem=("parallel",))(*ins)


def _aligned(v, m):
    return v if isinstance(v, int) else pl.multiple_of(v, m)


def _place():
    x, y, c = lax.axis_index("x"), lax.axis_index("y"), lax.axis_index("c")
    other_chips = [(x, 1 - y), (1 - x, y), (1 - x, 1 - y)]
    return x, y, c, 2 * x + y, other_chips


def _chip_index(chip):
    return 2 * chip[0] + chip[1]


def _piece(ref, kind, k, h):
    R, C = ref.shape
    if kind == "col":
        return ref.at[pl.ds(_aligned(h * (R // 2), 16), R // 2), pl.ds(_aligned(k * (C // N_CHIPS), 128), C // N_CHIPS)]
    rs = R // N_CHIPS
    return ref.at[pl.ds(_aligned(k * rs + h * (rs // 2), 16), rs // 2), :]


def _compact_piece(ref, kind, k):
    R2, C = ref.shape
    if kind == "col":
        return ref.at[:, pl.ds(_aligned(k * (C // N_CHIPS), 128), C // N_CHIPS)]
    return ref.at[pl.ds(_aligned(k * (R2 // N_CHIPS), 16), R2 // N_CHIPS), :]


def _half_rows(ref, h):
    R = ref.shape[0]
    return ref.at[pl.ds(_aligned(h * (R // 2), 16), R // 2), :]


class _Copies:
    def __init__(self, send_sems, recv_sems):
        self.send_sems, self.recv_sems = send_sems, recv_sems
        self.n_remote = 0

    def remote(self, src, dst, device):
        k = self.n_remote
        self.n_remote += 1
        return pltpu.make_async_remote_copy(src_ref=src, dst_ref=dst, send_sem=self.send_sems.at[k], recv_sem=self.recv_sems.at[k],
                                            device_id=device, device_id_type=MESH)


class _Job:
    def __init__(self, ins, out_shape, aliases, n_remote, build):
        self.ins, self.out_shape, self.aliases, self.n_remote, self.build = list(ins), list(out_shape), dict(aliases), n_remote, build


class _Flying:
    def __init__(self, job, send_sems, recv_sems, bufs, token):
        self.job, self.send_sems, self.recv_sems, self.bufs, self.token = job, send_sems, recv_sems, bufs, token


def _job_refs(job, buf_refs):
    n_out = len(job.out_shape)
    kept = [i for i in range(len(job.ins)) if i not in job.aliases]
    ins = [buf_refs[job.aliases[i]] if i in job.aliases else buf_refs[n_out + kept.index(i)] for i in range(len(job.ins))]
    return ins, list(buf_refs[:n_out])


def _start_job(job, name, after=()):
    n_in, n_out, n_after = len(job.ins), len(job.out_shape), len(after)
    kept = [i for i in range(n_in) if i not in job.aliases]
    n_bufs = n_out + len(kept)

    def body(*refs):
        in_refs, out_refs = refs[:n_in], refs[n_in + n_after:n_in + n_after + n_out]
        send_sems, recv_sems, token = refs[n_in + n_after + n_bufs:]
        for d in job.build(in_refs, out_refs, _Copies(send_sems, recv_sems)):
            d.start()
        token[...] = jnp.zeros_like(token)

    aliases = dict(job.aliases)
    aliases.update({i: n_out + k for k, i in enumerate(kept)})
    sems = pltpu.SemaphoreType.DMA((job.n_remote,))
    outs = pl.pallas_call(
        body, name=name, in_specs=[HBM] * n_in + [ANY] * n_after,
        out_specs=[HBM] * n_bufs + [SEM, SEM, pl.BlockSpec(memory_space=pltpu.VMEM)],
        out_shape=job.out_shape + [jax.ShapeDtypeStruct(job.ins[i].shape, job.ins[i].dtype) for i in kept]
        + [sems, sems, jax.ShapeDtypeStruct((8, 128), F32)],
        input_output_aliases=aliases,
        compiler_params=pltpu.CompilerParams(has_side_effects=pltpu.SideEffectType.DATAFLOW_SIDE_EFFECTING))(*job.ins, *after)
    return _Flying(job, outs[n_bufs], outs[n_bufs + 1], list(outs[:n_bufs]), outs[n_bufs + 2])


def _wait_job(flying, name, after=()):
    job, n_bufs, n_after = flying.job, len(flying.bufs), len(after)

    def body(*refs):
        in_refs, out_refs = _job_refs(job, refs[:n_bufs])
        send_sems, recv_sems = refs[n_bufs:n_bufs + 2]
        copies = job.build(in_refs, out_refs, _Copies(send_sems, recv_sems))
        for d in copies:
            d.wait_send()
        for d in copies:
            d.wait_recv()

    outs = pl.pallas_call(
        body, name=name, in_specs=[HBM] * n_bufs + [SEM, SEM] + [ANY] * n_after, out_specs=[HBM] * n_bufs,
        out_shape=[jax.ShapeDtypeStruct(b.shape, b.dtype) for b in flying.bufs],
        input_output_aliases={i: i for i in range(n_bufs)},
        compiler_params=pltpu.CompilerParams(has_side_effects=pltpu.SideEffectType.DATAFLOW_SIDE_EFFECTING))(
            *flying.bufs, flying.send_sems, flying.recv_sems, *after)
    return list(outs[:len(job.out_shape)]), list(outs[len(job.out_shape):])


def _in_place(arrays):
    return [jax.ShapeDtypeStruct(a.shape, a.dtype) for a in arrays], {u: u for u in range(len(arrays))}


def _rows_part(ref, part, n_parts):
    h = ref.shape[0] // n_parts
    return ref.at[pl.ds(part * h, h), :]


def _gather_job(full, kind, stage):
    def build(in_refs, out_refs, cp):
        x, y, c, me, (y_nbr, x_nbr, diagonal) = _place()
        (ref,) = out_refs
        sibling = (x, y, 1 - c)
        if stage == 1:
            mine = _piece(ref, kind, me, c)
            return [cp.remote(mine, mine, (*y_nbr, c)), cp.remote(mine, mine, (*x_nbr, c))]
        from_y, from_x = _piece(ref, kind, _chip_index(y_nbr), c), _piece(ref, kind, _chip_index(x_nbr), c)
        if stage == 2:
            relay_0, relay_1 = _rows_part(from_x, 0, 2), _rows_part(from_y, 1, 2)
            return [cp.remote(relay_0, relay_0, (*y_nbr, c)), cp.remote(relay_1, relay_1, (*x_nbr, c)),
                    cp.remote(from_y, from_y, sibling), cp.remote(from_x, from_x, sibling)]
        from_diagonal = _piece(ref, kind, _chip_index(diagonal), c)
        return [cp.remote(from_diagonal, from_diagonal, sibling)]

    return _Job([full], *_in_place([full]), {1: 2, 2: 4, 3: 1}[stage], build)


def _gather_small_job(fulls, axes):
    def build(in_refs, out_refs, cp):
        x, y, c, me, chips = _place()
        copies = []
        for ref, ax in zip(out_refs, axes):
            n = ref.shape[ax] // N_CHIPS
            idx = [slice(None)] * len(ref.shape)
            idx[ax] = pl.ds(_aligned(me * n, n), n)
            mine = ref.at[tuple(idx)]
            copies += [cp.remote(mine, mine, (*chip, c)) for chip in chips]
        return copies

    return _Job(fulls, *_in_place(fulls), 3 * len(fulls), build)


def _exchange_halves_job(grads, kinds):
    def build(in_refs, out_refs, cp):
        x, y, c, me, chips = _place()
        copies = []
        for src, dst, kind in zip(in_refs, out_refs, kinds):
            if kind == "col":
                copies.append(cp.remote(_half_rows(src, 1 - c), dst, (x, y, 1 - c)))
            else:
                copies += [cp.remote(_piece(src, "row", k, 1 - c), _compact_piece(dst, "row", k), (x, y, 1 - c))
                           for k in range(N_CHIPS)]
        return copies

    out_shape = [jax.ShapeDtypeStruct((g.shape[0] // 2, g.shape[1]), g.dtype) for g in grads]
    return _Job(grads, out_shape, {}, sum(1 if k == "col" else N_CHIPS for k in kinds), build)


def _scatter_job(half, kind, n_parts, parts, into=None):
    def build(in_refs, out_refs, cp):
        x, y, c, me, chips = _place()
        src, (dst,) = in_refs[0], out_refs
        copies = []
        for p in parts:
            copies += [cp.remote(_rows_part(_compact_piece(src, kind, _chip_index(chip)), p, n_parts),
                                 _rows_part(dst.at[r], p, n_parts), (*chip, c)) for r, chip in enumerate(chips)]
        return copies

    part_shape = (half.shape[0], half.shape[1] // N_CHIPS) if kind == "col" else (half.shape[0] // N_CHIPS, half.shape[1])
    out_shape = [jax.ShapeDtypeStruct((N_CHIPS - 1,) + part_shape, half.dtype)]
    ins, aliases = ([half], {}) if into is None else ([half, into], {1: 0})
    return _Job(ins, out_shape, aliases, 3 * len(parts), build)


def _share_job(shards):
    def build(in_refs, out_refs, cp):
        x, y, c, me, chips = _place()
        copies = []
        for ref in out_refs:
            mine = _half_rows(ref, c)
            copies.append(cp.remote(mine, mine, (x, y, 1 - c)))
        return copies

    return _Job(shards, *_in_place(shards), len(shards), build)


N_DEVICES = 2 * N_CHIPS


def _small_exchange_job(slots, n_parts, parts):
    def build(in_refs, out_refs, cp):
        x, y, c, me, chips = _place()
        (ref,) = out_refs
        copies = []
        for part in parts:
            mine = _rows_part(ref.at[2 * me + c], part, n_parts)
            copies += [cp.remote(mine, mine, (x ^ (p >> 2), y ^ ((p >> 1) & 1), c ^ (p & 1))) for p in range(1, N_DEVICES)]
        return copies

    return _Job([slots], *_in_place([slots]), (N_DEVICES - 1) * len(parts), build)


def _in_own_slot(packed, place, name):
    R, C = packed.shape

    def body(s_ref, p_ref, o_ref):
        o_ref[...] = p_ref[...]

    return _placed_call(body, name, place, (1,), [pl.BlockSpec((R, C), lambda i, s: (0, 0))],
                        pl.BlockSpec((None, R, C), lambda i, s: (2 * s[0] + s[1], 0, 0)),
                        jax.ShapeDtypeStruct((N_DEVICES, R, C), F32), [packed])


def _sum_slots(slots, name):
    n, R, C = slots.shape

    def body(s_ref, o_ref):
        acc = s_ref[0]
        for i in range(1, n):
            acc = acc + s_ref[i]
        o_ref[...] = acc

    return _call(body,name=name, grid=(1,), in_specs=[pl.BlockSpec((n, R, C), lambda i: (0, 0, 0))],
                 out_specs=pl.BlockSpec((R, C), lambda i: (0, 0)), out_shape=jax.ShapeDtypeStruct((R, C), F32),
                 sem=("arbitrary",))(slots)


def _packed_rows(size, width):
    return -(-size // (8 * width)) * 8


def _pack(arrays, width):
    rows = []
    for a in arrays:
        flat = a.reshape(-1)
        n_rows = _packed_rows(flat.shape[0], width)
        rows.append(jnp.pad(flat, (0, n_rows * width - flat.shape[0])).reshape(n_rows, width))
    return jnp.concatenate(rows, axis=0)


def _unpack(packed, shapes):
    out, r0, width = [], 0, packed.shape[1]
    for shape in shapes:
        size = 1
        for d in shape:
            size *= d
        out.append(packed[r0:r0 + _packed_rows(size, width)].reshape(-1)[:size].reshape(shape))
        r0 += _packed_rows(size, width)
    return out


class _Backlog:
    def __init__(self, first):
        self.now, self.free, self.flights, self.last, self.chain = 0.0, {"ici": 0.0, "d2d": 0.0}, [], first, []

    def run(self, fn, us, *args, **kw):
        out = fn(*args, **kw)
        self.now += us
        self.last = out[0] if isinstance(out, (list, tuple)) else out
        self.poll()
        return out

    def start(self, job, name, link, cost, done):
        flying = _start_job(job, name + "_start", self.chain)
        self.chain = [flying.token]
        _Behind.pending.append(flying.token)
        ends = max(self.now, self.free[link]) + cost
        self.free[link] = ends
        self.flights.append((ends + LANDING_SLACK_US, name, flying, done))
        self.flights.sort(key=lambda f: f[0])

    def poll(self, block=False):
        while self.flights and (block or self.flights[0][0] <= self.now):
            ends, name, flying, done = self.flights.pop(0)
            self.now, block = max(self.now, ends), False
            done(*_wait_job(flying, name + "_wait", [self.last] + self.chain))


class _GatherStream:
    def __init__(self, backlog, bufs, kinds, costs):
        self.backlog, self.bufs, self.kinds, self.costs, self.begun, self.complete = backlog, bufs, kinds, costs, 0, set()
        self.begin()

    def begin(self):
        u, self.begun = self.begun, self.begun + 1
        self.backlog.start(_gather_job(self.bufs[u], self.kinds[u], 1), "gather_%d" % u, "ici", 0.5 * self.costs[u],
                           lambda outs, kept: self.arrived(u, outs[0]))

    def arrived(self, u, buf):
        self.bufs[u] = buf
        self.backlog.start(_gather_job(buf, self.kinds[u], 2), "relay_%d" % u, "ici", 0.25 * self.costs[u],
                           lambda outs, kept: self.relayed(u, outs[0]))
        while self.begun <= min(u + (1 if u < STARVED_WEIGHTS else 2), len(self.bufs) - 1):
            self.begin()

    def relayed(self, u, buf):
        self.bufs[u] = buf
        self.backlog.start(_gather_job(buf, self.kinds[u], 3), "handon_%d" % u, "d2d", D2D_SHARE * self.costs[u],
                           lambda outs, kept: self.handed(u, outs[0]))

    def handed(self, u, buf):
        self.bufs[u] = buf
        self.complete.add(u)

    def ready(self, u):
        while u not in self.complete:
            assert self.backlog.flights, "weight %d is not on its way" % u
            self.backlog.poll(block=True)
        return self.bufs[u]


class _GradStream:
    def __init__(self, backlog, u, name, kind, cost, g, place, results):
        self.backlog, self.u, self.name, self.kind, self.cost, self.place, self.results = backlog, u, name, kind, cost, place, results
        backlog.start(_exchange_halves_job([g], [kind]), "to_sibling_" + name, "d2d", D2D_SHARE * cost, self.exchanged)

    def exchanged(self, outs, kept):
        chip_sum = self.backlog.run(_add_pair, SIDE_KERNEL_US, kept[0], outs[0], self.kind, self.place, "chip_sum_" + self.name)
        self.backlog.start(_scatter_job(chip_sum, self.kind, 1, [0]), "to_owners_" + self.name, "ici", self.cost, self.scattered)

    def scattered(self, outs, kept):
        reduced = self.backlog.run(_sum_chips, SIDE_KERNEL_US, kept[0], outs[0], self.kind, self.place, "reduce_" + self.name)
        self.backlog.start(_share_job([reduced]), "share_" + self.name, "d2d", D2D_SHARE * self.cost, self.shared)

    def shared(self, outs, kept):
        self.results[self.u] = outs[0]


SIDE_KERNEL_US = 12.0
D2D_SHARE = 0.15
LANDING_SLACK_US = 5.0
STARVED_WEIGHTS = 8


def kernel(x, mix_pre_g, mix_post_g, ffn_pre_g, ffn_post_g, ab_w_in, pool_w, pool_scale, conv_w, conv_b, conv_ln_g, conv_ln_b, ab_w_out, sc_w_in, sc_conv_w, sc_w_out, ffn_w1, ffn_w2, loss_target, m_mix_pre_g, m_mix_post_g, m_ffn_pre_g, m_ffn_post_g, m_ab_w_in, m_pool_w, m_pool_scale, m_conv_w, m_conv_b, m_conv_ln_g, m_conv_ln_b, m_ab_w_out, m_sc_w_in, m_sc_conv_w, m_sc_w_out, m_ffn_w1, m_ffn_w2, v_mix_pre_g, v_mix_post_g, v_ffn_pre_g, v_ffn_post_g, v_ab_w_in, v_pool_w, v_pool_scale, v_conv_w, v_conv_b, v_conv_ln_g, v_conv_ln_b, v_ab_w_out, v_sc_w_in, v_sc_conv_w, v_sc_w_out, v_ffn_w1, v_ffn_w2):
    x0, target = x[0], loss_target[0]
    T, D = x0.shape
    DP = pool_scale.shape[-1]
    gain = lambda g, layer: g[layer][None, :]

    big = [("ab_w_in", ab_w_in, 0, "col", 4, 67.0), ("ab_w_out", ab_w_out, 0, "row", 2, 44.0),
           ("ffn_w1_0", ffn_w1, 0, "col", 8, 177.0), ("ffn_w2_0", ffn_w2, 0, "row", 8, 177.0),
           ("sc_w_in", sc_w_in, 0, "col", 4, 133.0), ("sc_w_out", sc_w_out, 0, "row", 2, 44.0),
           ("ffn_w1_1", ffn_w1, 1, "col", 8, 177.0), ("ffn_w2_1", ffn_w2, 1, "row", 8, 177.0)]
    kinds = [b[3] for b in big]
    chip = 2 * lax.axis_index("x") + lax.axis_index("y")
    place = jnp.stack([chip, lax.axis_index("c")]).astype(jnp.int32)

    def own_in_zeros(shard, ax):
        full = jnp.zeros(tuple(d * N_CHIPS if i == ax else d for i, d in enumerate(shard.shape)), shard.dtype)
        return lax.dynamic_update_slice_in_dim(full, shard, chip * shard.shape[ax], axis=ax)

    W = [_cast_into_full(w, layer, kind, place, "cast_" + name) for name, w, layer, kind, _, _ in big]
    smalls = [own_in_zeros(pool_w[0], 1), own_in_zeros(conv_w[0], 1), own_in_zeros(sc_conv_w[0], 1)]
    backlog = _Backlog(x0)
    run = backlog.run
    small_weights = []
    backlog.start(_gather_small_job(smalls, [1, 1, 1]), "gather_small", "ici", 6.0, lambda outs, kept: small_weights.extend(outs))
    gather = _GatherStream(backlog, W, kinds, [b[5] for b in big])

    relu_sq = lambda acc: (jnp.maximum(acc, 0.0), jnp.square(jnp.maximum(acc, 0.0)))
    relu_sq_bwd = lambda acc, a: (acc * (2.0 * a.astype(F32)),)

    h0 = run(_norm_fwd, 12.0, x0, gain(mix_pre_g, 0), "norm_in")
    z0 = run(_matmul, 35.0, h0, gather.ready(0), "nn", "mix0_in")
    while not small_weights:
        backlog.poll(block=True)
    pool_w_full, conv_w_full, sc_conv_w_full = small_weights
    pooled, y_pool = run(_pool_fwd, 23.0, z0, pool_w_full, pool_scale, "pool_fwd")
    a_conv, c_conv = run(_conv_fwd, 25.0, z0, conv_w_full, conv_b, DP, "conv_fwd")
    y_conv = run(_ln_silu_fwd, 10.0, c_conv, conv_ln_g, conv_ln_b, "ln_silu_fwd")
    y0 = jnp.concatenate([y_pool, y_conv], axis=1)
    m0 = run(_matmul, 25.0, y0, gather.ready(1), "nn", "mix0_out")
    x1, h1 = run(_residual_norm, 21.0, x0, m0, gain(mix_post_g, 0), gain(ffn_pre_g, 0), "res_mix0")
    a0, a0sq = run(_matmul, 81.0, h1, gather.ready(2), "nn", "ffn0_up", out_dtypes=(BF16, BF16), epilogue=relu_sq)
    f0 = run(_matmul, 84.0, a0sq, gather.ready(3), "nn", "ffn0_down", tk=LONG_K_TILE)
    x2, h2 = run(_residual_norm, 22.0, x1, f0, gain(ffn_post_g, 0), gain(mix_pre_g, 1), "res_ffn0")
    z1 = run(_matmul, 62.0, h2, gather.ready(4), "nn", "mix1_in")
    y1 = run(_short_fwd, 22.0, z1, sc_conv_w_full, "short_fwd")
    m1 = run(_matmul, 25.0, y1, gather.ready(5), "nn", "mix1_out")
    x3, h3 = run(_residual_norm, 21.0, x2, m1, gain(mix_post_g, 1), gain(ffn_pre_g, 1), "res_mix1")
    a1, a1sq = run(_matmul, 81.0, h3, gather.ready(6), "nn", "ffn1_up", out_dtypes=(BF16, BF16), epilogue=relu_sq)
    f1 = run(_matmul, 84.0, a1sq, gather.ready(7), "nn", "ffn1_down", tk=LONG_K_TILE)
    w_in0, w_out0, w1_0, w2_0, w_in1, w_out1, w1_1, w2_1 = W

    grads_big = [None] * len(big)


    def reduce_grad(u, g):
        name, _, _, kind, _, cost = big[u]
        _GradStream(backlog, u, name, kind, cost, g, place, grads_big)

    dx, df1, d_ffn_post_1, loss_row = run(_loss_and_last_norm_bwd, 30.0, x3, f1, gain(ffn_post_g, 1), target, "loss")
    reduce_grad(7, run(_matmul, 80.0, a1sq, df1, "tn", "ffn1_down_dw", out_dtypes=(BF16,)))
    dz = run(_matmul, 82.0, df1, w2_1, "nt", "ffn1_down_dx", out_dtypes=(BF16,), epilogue=relu_sq_bwd, epi=(a1,))
    reduce_grad(6, run(_matmul, 80.0, h3, dz, "tn", "ffn1_up_dw", out_dtypes=(BF16,)))
    dh = run(_matmul, 87.0, dz, w1_1, "nt", "ffn1_up_dx", tk=LONG_K_TILE)
    dx, d_ffn_pre_1, dm1, d_mix_post_1 = run(_norms_bwd, 36.0, dx, dh, x3, gain(ffn_pre_g, 1), m1, gain(mix_post_g, 1), "norms_bwd3")

    reduce_grad(5, run(_matmul, 24.0, y1, dm1, "tn", "mix1_out_dw", out_dtypes=(BF16,)))
    dy1 = run(_matmul, 25.0, dm1, w_out1, "nt", "mix1_out_dx")
    db, dcg, du, d_sc_conv_w = run(_short_bwd, 41.0, dy1, z1, sc_conv_w_full, "short_bwd")
    dz1 = jnp.concatenate([db, dcg, du], axis=1)
    reduce_grad(4, run(_matmul, 62.0, h2, dz1, "tn", "mix1_in_dw", out_dtypes=(BF16,)))
    dh = run(_matmul, 68.0, dz1, w_in1, "nt", "mix1_in_dx")
    dx, d_mix_pre_1, df0, d_ffn_post_0 = run(_norms_bwd, 35.0, dx, dh, x2, gain(mix_pre_g, 1), f0, gain(ffn_post_g, 0), "norms_bwd2")

    reduce_grad(3, run(_matmul, 80.0, a0sq, df0, "tn", "ffn0_down_dw", out_dtypes=(BF16,)))
    dz = run(_matmul, 82.0, df0, w2_0, "nt", "ffn0_down_dx", out_dtypes=(BF16,), epilogue=relu_sq_bwd, epi=(a0,))
    reduce_grad(2, run(_matmul, 80.0, h1, dz, "tn", "ffn0_up_dw", out_dtypes=(BF16,)))
    dh = run(_matmul, 87.0, dz, w1_0, "nt", "ffn0_up_dx", tk=LONG_K_TILE)
    dx, d_ffn_pre_0, dm0, d_mix_post_0 = run(_norms_bwd, 36.0, dx, dh, x1, gain(ffn_pre_g, 0), m0, gain(mix_post_g, 0), "norms_bwd1")

    reduce_grad(1, run(_matmul, 24.0, y0, dm0, "tn", "mix0_out_dw", out_dtypes=(BF16,)))
    dy0 = run(_matmul, 25.0, dm0, w_out0, "nt", "mix0_out_dx")
    du_pool, d_pool_w, d_pool_scale = run(_pool_bwd, 28.0, dy0, pooled, pool_w_full, pool_scale, "pool_bwd")
    dc, d_ln_g, d_ln_b = run(_ln_silu_bwd, 15.0, dy0, c_conv, conv_ln_g, conv_ln_b, "ln_silu_bwd")
    dv, dgate, d_conv_w, d_conv_b = run(_conv_bwd, 52.0, dc, a_conv, z0, conv_w_full, DP, "conv_bwd")
    dz0 = jnp.concatenate([du_pool, dv, dgate], axis=1)

    small_sums = {}

    def exchange_small(key, arrays, cost):
        slots = _in_own_slot(_pack(arrays, D), place, "small_grads_slot_" + key)
        backlog.start(_small_exchange_job(slots, 1, [0]), "small_grads_" + key, "ici", cost,
                      lambda outs, kept: small_sums.__setitem__(key, _unpack(_sum_slots(outs[0], "small_grads_sum_" + key),
                                                                             [a.shape for a in arrays])))

    exchange_small("most", [d_mix_pre_1, jnp.concatenate([d_mix_post_0, d_mix_post_1], 0),
                            jnp.concatenate([d_ffn_pre_0, d_ffn_pre_1], 0), jnp.concatenate([d_ffn_post_0, d_ffn_post_1], 0),
                            d_pool_scale, d_conv_b, d_ln_g, d_ln_b, d_pool_w, d_conv_w, d_sc_conv_w], 112.0)
    reduce_grad(0, run(_matmul, 34.0, h0, dz0, "tn", "mix0_in_dw", out_dtypes=(BF16,)))
    dh = run(_matmul, 40.0, dz0, w_in0, "nt", "mix0_in_dx")
    grad_x, d_mix_pre_0 = run(_norms_bwd, 26.0, dx, dh, x0, gain(mix_pre_g, 0), None, None, "norms_bwd0")
    exchange_small("last", [d_mix_pre_0, loss_row], 5.0)

    upd, gr, first = {}, grads_big, {}

    def keep(where, key, outs):
        where[key] = outs
        return outs

    adamw_big = [
        (7, lambda: keep(first, "ffn_w2", _adamw(ffn_w2, gr[7], m_ffn_w2, v_ffn_w2, "adamw_ffn_w2_1", layer=1)), 46.0),
        (6, lambda: keep(first, "ffn_w1", _adamw(ffn_w1, gr[6], m_ffn_w1, v_ffn_w1, "adamw_ffn_w1_1", layer=1)), 46.0),
        (5, lambda: keep(upd, "sc_w_out", _adamw(sc_w_out, gr[5], m_sc_w_out, v_sc_w_out, "adamw_sc_w_out")), 14.0),
        (4, lambda: keep(upd, "sc_w_in", _adamw(sc_w_in, gr[4], m_sc_w_in, v_sc_w_in, "adamw_sc_w_in")), 35.0),
        (3, lambda: keep(upd, "ffn_w2", _adamw(ffn_w2, gr[3], m_ffn_w2, v_ffn_w2, "adamw_ffn_w2_0", layer=0,
                                               carried=first["ffn_w2"])), 46.0),
        (2, lambda: keep(upd, "ffn_w1", _adamw(ffn_w1, gr[2], m_ffn_w1, v_ffn_w1, "adamw_ffn_w1_0", layer=0,
                                               carried=first["ffn_w1"])), 46.0),
        (1, lambda: keep(upd, "ab_w_out", _adamw(ab_w_out, gr[1], m_ab_w_out, v_ab_w_out, "adamw_ab_w_out")), 14.0),
        (0, lambda: keep(upd, "ab_w_in", _adamw(ab_w_in, gr[0], m_ab_w_in, v_ab_w_in, "adamw_ab_w_in")), 19.0)]
    while adamw_big or backlog.flights:
        due = [a for a in adamw_big if gr[a[0]] is not None]
        if due:
            adamw_big.remove(due[0])
            backlog.run(due[0][1], due[0][2])
        else:
            backlog.poll(block=True)

    (g_mix_pre_1, g_mix_post, g_ffn_pre, g_ffn_post, g_pool_scale, g_conv_b, g_ln_g, g_ln_b, g_pool_w_full, g_conv_w_full,
     g_sc_conv_w_full) = small_sums["most"]
    g_mix_pre = jnp.concatenate([small_sums["last"][0], g_mix_pre_1], 0)
    loss = small_sums["last"][1][0, 0]
    own = lambda a, ax: lax.dynamic_slice_in_dim(a, chip * (a.shape[ax] // N_CHIPS), a.shape[ax] // N_CHIPS, axis=ax)
    g_pool_w, g_conv_w, g_sc_conv_w = own(g_pool_w_full, 1), own(g_conv_w_full, 1), own(g_sc_conv_w_full, 1)

    def small_update(w, g, m, v, name):
        shape = w.shape
        as3 = lambda a: a.reshape((1, -1, shape[-1]))
        outs = _adamw(as3(w), g.reshape((-1, shape[-1])), as3(m), as3(v), "adamw_" + name)
        return [o.reshape(shape) for o in outs]

    upd["mix_pre_g"] = small_update(mix_pre_g, g_mix_pre, m_mix_pre_g, v_mix_pre_g, "mix_pre_g")
    upd["mix_post_g"] = small_update(mix_post_g, g_mix_post, m_mix_post_g, v_mix_post_g, "mix_post_g")
    upd["ffn_pre_g"] = small_update(ffn_pre_g, g_ffn_pre, m_ffn_pre_g, v_ffn_pre_g, "ffn_pre_g")
    upd["ffn_post_g"] = small_update(ffn_post_g, g_ffn_post, m_ffn_post_g, v_ffn_post_g, "ffn_post_g")
    upd["pool_w"] = small_update(pool_w, g_pool_w, m_pool_w, v_pool_w, "pool_w")
    upd["pool_scale"] = small_update(pool_scale, g_pool_scale, m_pool_scale, v_pool_scale, "pool_scale")
    upd["conv_w"] = small_update(conv_w, g_conv_w, m_conv_w, v_conv_w, "conv_w")
    upd["conv_b"] = small_update(conv_b, g_conv_b, m_conv_b, v_conv_b, "conv_b")
    upd["conv_ln_g"] = small_update(conv_ln_g, g_ln_g, m_conv_ln_g, v_conv_ln_g, "conv_ln_g")
    upd["conv_ln_b"] = small_update(conv_ln_b, g_ln_b, m_conv_ln_b, v_conv_ln_b, "conv_ln_b")
    upd["sc_conv_w"] = small_update(sc_conv_w, g_sc_conv_w, m_sc_conv_w, v_sc_conv_w, "sc_conv_w")

    order = ["mix_pre_g", "mix_post_g", "ffn_pre_g", "ffn_post_g", "ab_w_in", "pool_w", "pool_scale", "conv_w", "conv_b",
             "conv_ln_g", "conv_ln_b", "ab_w_out", "sc_w_in", "sc_conv_w", "sc_w_out", "ffn_w1", "ffn_w2"]
    out = [loss, grad_x[None]]
    for part in range(4):
        out += [upd[n][part] for n in order]
    return tuple(out)
```

```python
import jax
import jax.numpy as jnp
from jax import lax
from jax.experimental import pallas as pl
from jax.experimental.pallas import tpu as pltpu

F32, BF16 = jnp.float32, jnp.bfloat16
EPS = 1e-6
N_GROUPS = 4
MAX_WINDOW = 16
CONV_K = 31
SHORT_K = 3
CONV_PAD = 32
SHORT_PAD = 8
ADAM_LR, ADAM_B1, ADAM_B2, ADAM_EPS, ADAM_WD, ADAM_STEP = 0.001, 0.9, 0.999, 1e-08, 0.01, 10
N_CHIPS = 4
VMEM_LIMIT_BYTES = 56 * 1024 * 1024
ROW_TILE = 256
CHUNK = 256
LONG_K_TILE = 4096
MESH = pl.DeviceIdType.MESH
HBM = pl.BlockSpec(memory_space=pltpu.HBM)
SEM = pl.BlockSpec(memory_space=pltpu.SEMAPHORE)
ANY = pl.BlockSpec(memory_space=pl.ANY)


def _cp(*sem):
    return pltpu.CompilerParams(dimension_semantics=sem, vmem_limit_bytes=VMEM_LIMIT_BYTES)


def _sigmoid(v):
    return 1.0 / (1.0 + jnp.exp(-v))


class _Behind:
    pending = []


def _call(body, prefetch=None, **kw):
    behind, _Behind.pending = _Behind.pending, []
    single = not isinstance(kw["out_shape"], (list, tuple))
    in_specs, scratch = list(kw["in_specs"]), list(kw.get("scratch_shapes", ()))
    out_shape = [kw["out_shape"]] if single else list(kw["out_shape"])
    out_specs = [kw["out_specs"]] if single else list(kw["out_specs"])
    n_pre = 0 if prefetch is None else 1
    n_own, n_behind = len(in_specs), len(behind)

    def wrapped(*refs):
        body(*refs[:n_pre + n_own], *refs[n_pre + n_own + n_behind:])

    specs = dict(grid=kw["grid"], in_specs=in_specs + [ANY] * n_behind, out_specs=out_specs)
    if prefetch is None:
        specs["scratch_shapes"] = scratch
    else:
        specs = dict(grid_spec=pltpu.PrefetchScalarGridSpec(num_scalar_prefetch=1, scratch_shapes=scratch, **specs))
    aliases = {n_pre + i: o for i, o in kw.get("input_output_aliases", {}).items()}
    call = pl.pallas_call(wrapped, name=kw["name"], out_shape=out_shape, input_output_aliases=aliases,
                          compiler_params=_cp(*kw["sem"]), **specs)

    def run(*args):
        outs = call(*([prefetch] * n_pre), *args, *behind)
        return outs[0] if single else list(outs)

    return run


_DIMS = {"nn": (((1,), (0,)), ((), ())), "nt": (((1,), (1,)), ((), ())), "tn": (((0,), (0,)), ((), ()))}


def _pick(n, cap, step=256):
    if n <= cap:
        return n
    return next(t for t in range(cap - cap % step, 0, -step) if n % t == 0)


def _matmul(a, b, mode, name, out_dtypes=(F32,), epilogue=None, epi=(), tm=1024, tn=1024, tk=2048):
    if mode == "tn":
        (K, M), (K2, N) = a.shape, b.shape
    elif mode == "nt":
        (M, K), (N, K2) = a.shape, b.shape
    else:
        (M, K), (K2, N) = a.shape, b.shape
    assert K == K2
    tm, tn, tk = _pick(M, tm), _pick(N, tn), _pick(K, tk)
    nk = K // tk
    a_spec = pl.BlockSpec((tk, tm), lambda i, j, k: (k, i)) if mode == "tn" else pl.BlockSpec((tm, tk), lambda i, j, k: (i, k))
    b_spec = pl.BlockSpec((tn, tk), lambda i, j, k: (j, k)) if mode == "nt" else pl.BlockSpec((tk, tn), lambda i, j, k: (k, j))
    o_spec = pl.BlockSpec((tm, tn), lambda i, j, k: (i, j))
    n_epi, n_out = len(epi), len(out_dtypes)

    def body(a_ref, b_ref, *rest):
        epi_refs, out_refs, scratch = rest[:n_epi], rest[n_epi:n_epi + n_out], rest[n_epi + n_out:]
        part = lax.dot_general(a_ref[...].astype(BF16), b_ref[...].astype(BF16), _DIMS[mode], preferred_element_type=F32)

        def finish(acc):
            outs = epilogue(acc, *[r[...] for r in epi_refs]) if epilogue else (acc,)
            for o_ref, o in zip(out_refs, outs):
                o_ref[...] = o.astype(o_ref.dtype)

        if nk == 1:
            finish(part)
        else:
            acc_ref = scratch[0]
            k = pl.program_id(2)

            @pl.when(k == 0)
            def _():
                acc_ref[...] = part

            @pl.when(k > 0)
            def _():
                acc_ref[...] += part

            @pl.when(k == nk - 1)
            def _():
                finish(acc_ref[...])

    outs = _call(
        body, name=name, grid=(M // tm, N // tn, nk),
        in_specs=[a_spec, b_spec] + [o_spec] * n_epi, out_specs=[o_spec] * n_out,
        out_shape=[jax.ShapeDtypeStruct((M, N), dt) for dt in out_dtypes],
        scratch_shapes=[pltpu.VMEM((tm, tn), F32)] if nk > 1 else [],
        sem=("parallel", "parallel", "arbitrary"))(a, b, *epi)
    return outs[0] if n_out == 1 else outs


def _rms(x, g):
    r = lax.rsqrt(jnp.mean(x * x, axis=-1, keepdims=True) + EPS)
    return x * r * g


def _rms_bwd(dy, x, g):
    r = lax.rsqrt(jnp.mean(x * x, axis=-1, keepdims=True) + EPS)
    xn = x * r
    dyg = dy * g
    dx = r * (dyg - xn * jnp.mean(dyg * xn, axis=-1, keepdims=True))
    return dx, jnp.sum(dy * xn, axis=0, keepdims=True)


def _rows(d, tr=ROW_TILE):
    return pl.BlockSpec((tr, d), lambda i: (i, 0))


def _vec(d):
    return pl.BlockSpec((1, d), lambda i: (0, 0))


def _accumulate(ref, val):
    @pl.when(pl.program_id(0) == 0)
    def _():
        ref[...] = val

    @pl.when(pl.program_id(0) > 0)
    def _():
        ref[...] += val


def _norm_fwd(x, g, name):
    T, D = x.shape

    def body(x_ref, g_ref, h_ref):
        h_ref[...] = _rms(x_ref[...], g_ref[...]).astype(BF16)

    return _call(body,name=name,grid=(T // ROW_TILE,), in_specs=[_rows(D), _vec(D)], out_specs=_rows(D),
                          out_shape=jax.ShapeDtypeStruct((T, D), BF16), sem=("parallel",))(x, g)


def _residual_norm(x, m, g_post, g_next, name):
    T, D = x.shape

    def body(x_ref, m_ref, gp_ref, gn_ref, xo_ref, h_ref):
        xo = x_ref[...] + _rms(m_ref[...], gp_ref[...])
        xo_ref[...] = xo
        h_ref[...] = _rms(xo, gn_ref[...]).astype(BF16)

    return _call(body,name=name,grid=(T // ROW_TILE,), in_specs=[_rows(D), _rows(D), _vec(D), _vec(D)],
                          out_specs=[_rows(D), _rows(D)],
                          out_shape=[jax.ShapeDtypeStruct((T, D), F32), jax.ShapeDtypeStruct((T, D), BF16)],
                          sem=("parallel",))(x, m, g_post, g_next)


def _loss_and_last_norm_bwd(x, m, g_post, target, name):
    T, D = x.shape

    def body(x_ref, m_ref, gp_ref, t_ref, dx_ref, dm_ref, dg_ref, loss_ref):
        m_val, gp = m_ref[...], gp_ref[...]
        err = x_ref[...] + _rms(m_val, gp) - t_ref[...]
        dx = err * (1.0 / D)
        dx_ref[...] = dx
        dm, dg = _rms_bwd(dx, m_val, gp)
        dm_ref[...] = dm.astype(BF16)
        _accumulate(dg_ref, dg)
        _accumulate(loss_ref, jnp.full((1, 128), 0.5 * jnp.sum(err * err) * (1.0 / D), F32))

    return _call(body,name=name,grid=(T // ROW_TILE,), in_specs=[_rows(D), _rows(D), _vec(D), _rows(D)],
                          out_specs=[_rows(D), _rows(D), _vec(D), _vec(128)],
                          out_shape=[jax.ShapeDtypeStruct((T, D), F32), jax.ShapeDtypeStruct((T, D), BF16),
                                     jax.ShapeDtypeStruct((1, D), F32), jax.ShapeDtypeStruct((1, 128), F32)],
                          sem=("arbitrary",))(x, m, g_post, target)


def _norms_bwd(dx, dh, x_in, g_pre, m_prev, g_post_prev, name):
    T, D = dx.shape
    with_prev = m_prev is not None

    def body(*refs):
        if with_prev:
            dx_ref, dh_ref, x_ref, gq_ref, m_ref, gp_ref, dxo_ref, dgq_ref, dm_ref, dgp_ref = refs
        else:
            dx_ref, dh_ref, x_ref, gq_ref, dxo_ref, dgq_ref = refs
        d_in, dgq = _rms_bwd(dh_ref[...], x_ref[...], gq_ref[...])
        dxo = dx_ref[...] + d_in
        dxo_ref[...] = dxo
        _accumulate(dgq_ref, dgq)
        if with_prev:
            dm, dgp = _rms_bwd(dxo, m_ref[...], gp_ref[...])
            dm_ref[...] = dm.astype(BF16)
            _accumulate(dgp_ref, dgp)

    ins, in_specs = [dx, dh, x_in, g_pre], [_rows(D), _rows(D), _rows(D), _vec(D)]
    out_specs = [_rows(D), _vec(D)]
    out_shape = [jax.ShapeDtypeStruct((T, D), F32), jax.ShapeDtypeStruct((1, D), F32)]
    if with_prev:
        ins += [m_prev, g_post_prev]
        in_specs += [_rows(D), _vec(D)]
        out_specs += [_rows(D), _vec(D)]
        out_shape += [jax.ShapeDtypeStruct((T, D), BF16), jax.ShapeDtypeStruct((1, D), F32)]
    return _call(body,name=name,grid=(T // ROW_TILE,), in_specs=in_specs, out_specs=out_specs, out_shape=out_shape,
                          sem=("arbitrary",))(*ins)


def _window_weights(g):
    w = 2 << g
    return w, [jnp.where(j < w, 1.0, 0.0).astype(F32) for j in range(MAX_WINDOW)]


def _valid_count(r0, rows, w):
    t = (lax.broadcasted_iota(jnp.int32, (rows, 1), 0) + (r0 + 1)).astype(F32)
    return jnp.minimum(t, w.astype(F32))


def _pool_fwd(z, pool_w, pool_scale, name):
    T = z.shape[0]
    PG = pool_w.shape[-1]
    DP = N_GROUPS * PG
    rc = min(CHUNK, T)

    def body(u_ref, pw_ref, sc_ref, pooled_ref, y_ref, pad):
        w, wts = _window_weights(pl.program_id(0))
        pad[pl.ds(0, MAX_WINDOW), :] = jnp.zeros((MAX_WINDOW, PG), F32)
        pad[pl.ds(MAX_WINDOW, T), :] = u_ref[...]
        for r0 in range(0, T, rc):
            acc = jnp.zeros((rc, PG), F32)
            for j in range(MAX_WINDOW):
                acc = acc + wts[j] * pad[pl.ds(MAX_WINDOW + r0 - j, rc), :]
            pooled = acc / _valid_count(r0, rc, w) - u_ref[pl.ds(r0, rc), :]
            pooled_ref[pl.ds(r0, rc), :] = pooled.astype(BF16)
        mixed = jnp.dot(pooled_ref[...], pw_ref[...].astype(BF16), preferred_element_type=F32)
        y_ref[...] = (mixed * sc_ref[...]).astype(BF16)

    col = lambda g: (0, g)
    return _call(
        body, name=name,grid=(N_GROUPS,),
        in_specs=[pl.BlockSpec((T, PG), col), pl.BlockSpec((None, PG, PG), lambda g: (g, 0, 0)), pl.BlockSpec((1, PG), col)],
        out_specs=[pl.BlockSpec((T, PG), col), pl.BlockSpec((T, PG), col)],
        out_shape=[jax.ShapeDtypeStruct((T, DP), BF16), jax.ShapeDtypeStruct((T, DP), BF16)],
        scratch_shapes=[pltpu.VMEM((T + MAX_WINDOW, PG), F32)], sem=("parallel",))(z, pool_w, pool_scale)


def _pool_bwd(dy, pooled, pool_w, pool_scale, name):
    T = dy.shape[0]
    PG = pool_w.shape[-1]
    DP = N_GROUPS * PG
    rc = min(CHUNK, T)

    def body(dy_ref, pooled_ref, pw_ref, sc_ref, du_ref, dpw_ref, dsc_ref, pad, dp_ref):
        w, wts = _window_weights(pl.program_id(0))
        pooled_v, pw = pooled_ref[...], pw_ref[...].astype(BF16)
        dy_v = dy_ref[...]
        mixed = jnp.dot(pooled_v, pw, preferred_element_type=F32)
        dsc_ref[...] = jnp.sum(dy_v * mixed, axis=0, keepdims=True)
        dmixed = (dy_v * sc_ref[...]).astype(BF16)
        dpw_ref[...] = lax.dot_general(pooled_v, dmixed, _DIMS["tn"], preferred_element_type=F32)
        dp_ref[...] = lax.dot_general(dmixed, pw, _DIMS["nt"], preferred_element_type=F32)
        pad[pl.ds(T, MAX_WINDOW), :] = jnp.zeros((MAX_WINDOW, PG), F32)
        for r0 in range(0, T, rc):
            pad[pl.ds(r0, rc), :] = dp_ref[pl.ds(r0, rc), :] / _valid_count(r0, rc, w)
        for r0 in range(0, T, rc):
            acc = jnp.zeros((rc, PG), F32)
            for j in range(MAX_WINDOW):
                acc = acc + wts[j] * pad[pl.ds(r0 + j, rc), :]
            du_ref[pl.ds(r0, rc), :] = (acc - dp_ref[pl.ds(r0, rc), :]).astype(BF16)

    col = lambda g: (0, g)
    return _call(
        body, name=name,grid=(N_GROUPS,),
        in_specs=[pl.BlockSpec((T, PG), col), pl.BlockSpec((T, PG), col), pl.BlockSpec((None, PG, PG), lambda g: (g, 0, 0)),
                  pl.BlockSpec((1, PG), col)],
        out_specs=[pl.BlockSpec((T, PG), col), pl.BlockSpec((None, PG, PG), lambda g: (g, 0, 0)), pl.BlockSpec((1, PG), col)],
        out_shape=[jax.ShapeDtypeStruct((T, DP), BF16), jax.ShapeDtypeStruct((N_GROUPS, PG, PG), F32),
                   jax.ShapeDtypeStruct((1, DP), F32)],
        scratch_shapes=[pltpu.VMEM((T + MAX_WINDOW, PG), F32), pltpu.VMEM((T, PG), F32)],
        sem=("parallel",))(dy, pooled, pool_w, pool_scale)


def _conv_fwd(z, conv_w, conv_b, d_pool, name, tc=128):
    T = z.shape[0]
    DC = conv_w.shape[-1]
    rc = min(CHUNK, T)
    v0, g0 = d_pool // tc, (d_pool + DC) // tc

    def body(v_ref, gt_ref, w_ref, b_ref, a_ref, c_ref, pad):
        pad[pl.ds(0, CONV_PAD), :] = jnp.zeros((CONV_PAD, tc), F32)
        for r0 in range(0, T, rc):
            a = v_ref[pl.ds(r0, rc), :] * _sigmoid(gt_ref[pl.ds(r0, rc), :])
            a_ref[pl.ds(r0, rc), :] = a
            pad[pl.ds(CONV_PAD + r0, rc), :] = a
        for r0 in range(0, T, rc):
            acc = jnp.zeros((rc, tc), F32) + b_ref[...]
            for k in range(CONV_K):
                acc = acc + w_ref[pl.ds(k, 1), :] * pad[pl.ds(CONV_PAD - (CONV_K - 1) + k + r0, rc), :]
            c_ref[pl.ds(r0, rc), :] = acc

    col = lambda j: (0, j)
    return _call(
        body, name=name,grid=(DC // tc,),
        in_specs=[pl.BlockSpec((T, tc), lambda j: (0, v0 + j)), pl.BlockSpec((T, tc), lambda j: (0, g0 + j)),
                  pl.BlockSpec((CONV_K, tc), col), pl.BlockSpec((1, tc), col)],
        out_specs=[pl.BlockSpec((T, tc), col), pl.BlockSpec((T, tc), col)],
        out_shape=[jax.ShapeDtypeStruct((T, DC), F32), jax.ShapeDtypeStruct((T, DC), F32)],
        scratch_shapes=[pltpu.VMEM((T + CONV_PAD, tc), F32)], sem=("parallel",))(z, z, conv_w, conv_b)


def _conv_bwd(dc, a, z, conv_w, d_pool, name, tc=128):
    T, DC = dc.shape
    rc = min(CHUNK, T)
    v0, g0 = d_pool // tc, (d_pool + DC) // tc

    def body(dc_ref, a_ref, v_ref, gt_ref, w_ref, dv_ref, dg_ref, dw_ref, db_ref, apad, dpad):
        apad[pl.ds(0, CONV_PAD), :] = jnp.zeros((CONV_PAD, tc), F32)
        apad[pl.ds(CONV_PAD, T), :] = a_ref[...]
        dpad[pl.ds(0, T), :] = dc_ref[...]
        dpad[pl.ds(T, CONV_PAD), :] = jnp.zeros((CONV_PAD, tc), F32)
        db_ref[...] = jnp.sum(dc_ref[...], axis=0, keepdims=True)
        for k in range(CONV_K):
            acc = jnp.zeros((8, tc), F32)
            for r0 in range(0, T, rc):
                prod = dc_ref[pl.ds(r0, rc), :] * apad[pl.ds(CONV_PAD - (CONV_K - 1) + k + r0, rc), :]
                acc = acc + jnp.sum(prod.reshape(rc // 8, 8, tc), axis=0)
            dw_ref[pl.ds(k, 1), :] = jnp.sum(acc, axis=0, keepdims=True)
        for r0 in range(0, T, rc):
            da = jnp.zeros((rc, tc), F32)
            for k in range(CONV_K):
                da = da + w_ref[pl.ds(k, 1), :] * dpad[pl.ds(r0 + (CONV_K - 1) - k, rc), :]
            sig = _sigmoid(gt_ref[pl.ds(r0, rc), :])
            dv_ref[pl.ds(r0, rc), :] = (da * sig).astype(BF16)
            dg_ref[pl.ds(r0, rc), :] = (da * v_ref[pl.ds(r0, rc), :] * sig * (1.0 - sig)).astype(BF16)

    col = lambda j: (0, j)
    return _call(
        body, name=name,grid=(DC // tc,),
        in_specs=[pl.BlockSpec((T, tc), col), pl.BlockSpec((T, tc), col), pl.BlockSpec((T, tc), lambda j: (0, v0 + j)),
                  pl.BlockSpec((T, tc), lambda j: (0, g0 + j)), pl.BlockSpec((CONV_K, tc), col)],
        out_specs=[pl.BlockSpec((T, tc), col), pl.BlockSpec((T, tc), col), pl.BlockSpec((CONV_K, tc), col),
                   pl.BlockSpec((1, tc), col)],
        out_shape=[jax.ShapeDtypeStruct((T, DC), BF16), jax.ShapeDtypeStruct((T, DC), BF16),
                   jax.ShapeDtypeStruct((CONV_K, DC), F32), jax.ShapeDtypeStruct((1, DC), F32)],
        scratch_shapes=[pltpu.VMEM((T + CONV_PAD, tc), F32), pltpu.VMEM((T + CONV_PAD, tc), F32)],
        sem=("parallel",))(dc, a, z, z, conv_w)


def _layer_norm_parts(c, g, b):
    mu = jnp.mean(c, axis=-1, keepdims=True)
    xc = c - mu
    rstd = lax.rsqrt(jnp.mean(xc * xc, axis=-1, keepdims=True) + EPS)
    xhat = xc * rstd
    return xhat, rstd, xhat * g + b


def _ln_silu_fwd(c, g, b, name):
    T, DC = c.shape

    def body(c_ref, g_ref, b_ref, y_ref):
        _, _, ln = _layer_norm_parts(c_ref[...], g_ref[...], b_ref[...])
        y_ref[...] = (ln * _sigmoid(ln)).astype(BF16)

    return _call(body,name=name,grid=(T // ROW_TILE,), in_specs=[_rows(DC), _vec(DC), _vec(DC)], out_specs=_rows(DC),
                          out_shape=jax.ShapeDtypeStruct((T, DC), BF16), sem=("parallel",))(c, g, b)


def _ln_silu_bwd(dy, c, g, b, name):
    T, DC = c.shape

    def body(dy_ref, c_ref, g_ref, b_ref, dc_ref, dg_ref, db_ref):
        gain = g_ref[...]
        xhat, rstd, ln = _layer_norm_parts(c_ref[...], gain, b_ref[...])
        s = _sigmoid(ln)
        dln = dy_ref[...] * (s * (1.0 + ln * (1.0 - s)))
        _accumulate(dg_ref, jnp.sum(dln * xhat, axis=0, keepdims=True))
        _accumulate(db_ref, jnp.sum(dln, axis=0, keepdims=True))
        dxh = dln * gain
        dc_ref[...] = rstd * (dxh - jnp.mean(dxh, axis=-1, keepdims=True) - xhat * jnp.mean(dxh * xhat, axis=-1, keepdims=True))

    return _call(body,name=name,grid=(T // ROW_TILE,),
                          in_specs=[pl.BlockSpec((ROW_TILE, DC), lambda i: (i, 1)), _rows(DC), _vec(DC), _vec(DC)],
                          out_specs=[_rows(DC), _vec(DC), _vec(DC)],
                          out_shape=[jax.ShapeDtypeStruct((T, DC), F32), jax.ShapeDtypeStruct((1, DC), F32),
                                     jax.ShapeDtypeStruct((1, DC), F32)],
                          sem=("arbitrary",))(dy, c, g, b)


def _short_specs(T, DS, tc):
    n = DS // tc
    return [pl.BlockSpec((T, tc), lambda j: (0, j)), pl.BlockSpec((T, tc), lambda j: (0, n + j)),
            pl.BlockSpec((T, tc), lambda j: (0, 2 * n + j))]


def _short_fwd(z, w, name, tc=256):
    T = z.shape[0]
    DS = w.shape[-1]
    rc = min(CHUNK, T)

    def body(b_ref, cg_ref, u_ref, w_ref, y_ref, pad):
        pad[pl.ds(0, SHORT_PAD), :] = jnp.zeros((SHORT_PAD, tc), F32)
        pad[pl.ds(SHORT_PAD, T), :] = cg_ref[...] * u_ref[...]
        for r0 in range(0, T, rc):
            r = jnp.zeros((rc, tc), F32)
            for k in range(SHORT_K):
                r = r + w_ref[pl.ds(k, 1), :] * pad[pl.ds(SHORT_PAD - (SHORT_K - 1) + k + r0, rc), :]
            y_ref[pl.ds(r0, rc), :] = (b_ref[pl.ds(r0, rc), :] * r).astype(BF16)

    col = lambda j: (0, j)
    return _call(body,name=name,grid=(DS // tc,), in_specs=_short_specs(T, DS, tc) + [pl.BlockSpec((SHORT_K, tc), col)],
                          out_specs=pl.BlockSpec((T, tc), col), out_shape=jax.ShapeDtypeStruct((T, DS), BF16),
                          scratch_shapes=[pltpu.VMEM((T + SHORT_PAD, tc), F32)], sem=("parallel",))(z, z, z, w)


def _short_bwd(dy, z, w, name, tc=256):
    T, DS = dy.shape
    rc = min(CHUNK, T)

    def body(dy_ref, b_ref, cg_ref, u_ref, w_ref, db_ref, dcg_ref, du_ref, dw_ref, qpad, rpad):
        qpad[pl.ds(0, SHORT_PAD), :] = jnp.zeros((SHORT_PAD, tc), F32)
        qpad[pl.ds(SHORT_PAD, T), :] = cg_ref[...] * u_ref[...]
        rpad[pl.ds(0, T), :] = dy_ref[...] * b_ref[...]
        rpad[pl.ds(T, SHORT_PAD), :] = jnp.zeros((SHORT_PAD, tc), F32)
        accs = [jnp.zeros((8, tc), F32) for _ in range(SHORT_K)]
        for r0 in range(0, T, rc):
            r = jnp.zeros((rc, tc), F32)
            dq = jnp.zeros((rc, tc), F32)
            dr = rpad[pl.ds(r0, rc), :]
            for k in range(SHORT_K):
                q_k = qpad[pl.ds(SHORT_PAD - (SHORT_K - 1) + k + r0, rc), :]
                r = r + w_ref[pl.ds(k, 1), :] * q_k
                dq = dq + w_ref[pl.ds(k, 1), :] * rpad[pl.ds(r0 + (SHORT_K - 1) - k, rc), :]
                accs[k] = accs[k] + jnp.sum((dr * q_k).reshape(rc // 8, 8, tc), axis=0)
            db_ref[pl.ds(r0, rc), :] = (dy_ref[pl.ds(r0, rc), :] * r).astype(BF16)
            dcg_ref[pl.ds(r0, rc), :] = (dq * u_ref[pl.ds(r0, rc), :]).astype(BF16)
            du_ref[pl.ds(r0, rc), :] = (dq * cg_ref[pl.ds(r0, rc), :]).astype(BF16)
        for k in range(SHORT_K):
            dw_ref[pl.ds(k, 1), :] = jnp.sum(accs[k], axis=0, keepdims=True)

    col = lambda j: (0, j)
    tile = pl.BlockSpec((T, tc), col)
    return _call(body,name=name,grid=(DS // tc,),
                          in_specs=[tile] + _short_specs(T, DS, tc) + [pl.BlockSpec((SHORT_K, tc), col)],
                          out_specs=[tile, tile, tile, pl.BlockSpec((SHORT_K, tc), col)],
                          out_shape=[jax.ShapeDtypeStruct((T, DS), BF16)] * 3 + [jax.ShapeDtypeStruct((SHORT_K, DS), F32)],
                          scratch_shapes=[pltpu.VMEM((T + SHORT_PAD, tc), F32), pltpu.VMEM((T + SHORT_PAD, tc), F32)],
                          sem=("parallel",))(dy, z, z, z, w)


def _tile_rows(rows, cols, n_bufs):
    budget = VMEM_LIMIT_BYTES * 3 // 4 // (2 * n_bufs * 4 * cols)
    tr = rows
    while tr > budget and tr % 16 == 0:
        tr //= 2
    return tr


def _placed_call(body, name, place, grid, in_specs, out_specs, out_shape, ins):
    return _call(body,prefetch=place, name=name, grid=grid, in_specs=in_specs, out_specs=out_specs, out_shape=out_shape,
                 sem=("parallel",))(*ins)


def _cast_into_full(w, layer, kind, place, name):
    _, R, C = w.shape
    tr = _tile_rows(R, C, 2)
    nb = R // tr
    if kind == "col":
        full, out_spec = (R, C * N_CHIPS), pl.BlockSpec((tr, C), lambda i, s: (i, s[0]))
    else:
        full, out_spec = (R * N_CHIPS, C), pl.BlockSpec((tr, C), lambda i, s: (s[0] * nb + i, 0))

    def body(s_ref, w_ref, o_ref):
        o_ref[...] = w_ref[...].astype(BF16)

    return _placed_call(body, name, place, (nb,), [pl.BlockSpec((None, tr, C), lambda i, s: (layer, i, 0))], out_spec,
                        jax.ShapeDtypeStruct(full, BF16), [w])


def _add_pair(grad, theirs, kind, place, name):
    R, C = grad.shape
    piece_rows = R // 2 if kind == "col" else R // N_CHIPS // 2
    tr = _tile_rows(piece_rows, C, 3)
    nb = piece_rows // tr
    if kind == "col":
        g_spec = pl.BlockSpec((tr, C), lambda i, s: (s[1] * nb + i, 0))
    else:
        g_spec = pl.BlockSpec((tr, C), lambda i, s: ((2 * (i // nb) + s[1]) * nb + i % nb, 0))
    flat = pl.BlockSpec((tr, C), lambda i, s: (i, 0))

    def body(s_ref, a_ref, b_ref, o_ref):
        o_ref[...] = (a_ref[...].astype(F32) + b_ref[...].astype(F32)).astype(BF16)

    return _placed_call(body, name, place, (R // 2 // tr,), [g_spec, flat], flat, jax.ShapeDtypeStruct((R // 2, C), BF16),
                        [grad, theirs])


def _sum_chips(chip_sum, arrived, kind, place, name):
    _, H, W = arrived.shape
    tr = _tile_rows(H, W, 6)
    nb = H // tr
    if kind == "col":
        own_spec = pl.BlockSpec((tr, W), lambda i, s: (i, s[0]))
    else:
        own_spec = pl.BlockSpec((tr, W), lambda i, s: (s[0] * nb + i, 0))

    def body(s_ref, p_ref, r_ref, o_ref):
        acc = p_ref[...].astype(F32)
        for i in range(N_CHIPS - 1):
            acc = acc + r_ref[i].astype(F32)
        o_ref[...] = acc

    return _placed_call(body, name, place, (nb,), [own_spec, pl.BlockSpec((N_CHIPS - 1, tr, W), lambda i, s: (0, i, 0))],
                        pl.BlockSpec((tr, W), lambda i, s: (s[1] * nb + i, 0)), jax.ShapeDtypeStruct((2 * H, W), F32),
                        [chip_sum, arrived])


def _adamw_values(w, g, m, v):
    m = ADAM_B1 * m + (1.0 - ADAM_B1) * g
    v = ADAM_B2 * v + (1.0 - ADAM_B2) * (g * g)
    m_hat = m / (1.0 - ADAM_B1 ** ADAM_STEP)
    v_hat = v / (1.0 - ADAM_B2 ** ADAM_STEP)
    return -ADAM_LR * (m_hat / (jnp.sqrt(v_hat) + ADAM_EPS) + ADAM_WD * w), m, v


def _adamw(w, g, m, v, name, layer=0, carried=None):
    L, R, C = w.shape
    tr = _tile_rows(R, C, 8)

    def body(w_ref, g_ref, m_ref, v_ref, *rest):
        go_ref, d_ref, mo_ref, vo_ref = rest[-4:]
        g_val = g_ref[...]
        d, m_new, v_new = _adamw_values(w_ref[...], g_val, m_ref[...], v_ref[...])
        go_ref[...], d_ref[...], mo_ref[...], vo_ref[...] = g_val, d, m_new, v_new

    lay = pl.BlockSpec((None, tr, C), lambda i: (layer, i, 0))
    ins = [w, g, m, v]
    in_specs = [lay, pl.BlockSpec((tr, C), lambda i: (i, 0)), lay, lay]
    aliases = {}
    if carried is not None:
        ins += list(carried)
        in_specs += [pl.BlockSpec(memory_space=pl.ANY)] * 4
        aliases = {4 + i: i for i in range(4)}
    return _call(body, name=name, grid=(R // tr,), in_specs=in_specs, out_specs=[lay] * 4,
                 out_shape=[jax.ShapeDtypeStruct((L, R, C), F32)] * 4, input_output_aliases=aliases, sem=("parallel",))(*ins)


def _aligned(v, m):
    return v if isinstance(v, int) else pl.multiple_of(v, m)


def _place():
    x, y, c = lax.axis_index("x"), lax.axis_index("y"), lax.axis_index("c")
    other_chips = [(x, 1 - y), (1 - x, y), (1 - x, 1 - y)]
    return x, y, c, 2 * x + y, other_chips


def _chip_index(chip):
    return 2 * chip[0] + chip[1]


def _piece(ref, kind, k, h):
    R, C = ref.shape
    if kind == "col":
        return ref.at[pl.ds(_aligned(h * (R // 2), 16), R // 2), pl.ds(_aligned(k * (C // N_CHIPS), 128), C // N_CHIPS)]
    rs = R // N_CHIPS
    return ref.at[pl.ds(_aligned(k * rs + h * (rs // 2), 16), rs // 2), :]


def _compact_piece(ref, kind, k):
    R2, C = ref.shape
    if kind == "col":
        return ref.at[:, pl.ds(_aligned(k * (C // N_CHIPS), 128), C // N_CHIPS)]
    return ref.at[pl.ds(_aligned(k * (R2 // N_CHIPS), 16), R2 // N_CHIPS), :]


def _half_rows(ref, h):
    R = ref.shape[0]
    return ref.at[pl.ds(_aligned(h * (R // 2), 16), R // 2), :]


class _Copies:
    def __init__(self, send_sems, recv_sems):
        self.send_sems, self.recv_sems = send_sems, recv_sems
        self.n_remote = 0

    def remote(self, src, dst, device):
        k = self.n_remote
        self.n_remote += 1
        return pltpu.make_async_remote_copy(src_ref=src, dst_ref=dst, send_sem=self.send_sems.at[k], recv_sem=self.recv_sems.at[k],
                                            device_id=device, device_id_type=MESH)


class _Job:
    def __init__(self, ins, out_shape, aliases, n_remote, build):
        self.ins, self.out_shape, self.aliases, self.n_remote, self.build = list(ins), list(out_shape), dict(aliases), n_remote, build


class _Flying:
    def __init__(self, job, send_sems, recv_sems, bufs, token):
        self.job, self.send_sems, self.recv_sems, self.bufs, self.token = job, send_sems, recv_sems, bufs, token


def _job_refs(job, buf_refs):
    n_out = len(job.out_shape)
    kept = [i for i in range(len(job.ins)) if i not in job.aliases]
    ins = [buf_refs[job.aliases[i]] if i in job.aliases else buf_refs[n_out + kept.index(i)] for i in range(len(job.ins))]
    return ins, list(buf_refs[:n_out])


def _start_jobs(jobs, name, after=()):
    n_after = len(after)
    ins, out_shape, aliases, layout = [], [], {}, []
    for job in jobs:
        kept = [i for i in range(len(job.ins)) if i not in job.aliases]
        layout.append((len(ins), len(out_shape), len(kept)))
        aliases.update({len(ins) + i: len(out_shape) + o for i, o in job.aliases.items()})
        aliases.update({len(ins) + i: len(out_shape) + len(job.out_shape) + k for k, i in enumerate(kept)})
        out_shape += job.out_shape + [jax.ShapeDtypeStruct(job.ins[i].shape, job.ins[i].dtype) for i in kept]
        ins += job.ins
    n_in, n_bufs = len(ins), len(out_shape)

    def body(*refs):
        in_refs, buf_refs = refs[:n_in], refs[n_in + n_after:n_in + n_after + n_bufs]
        sem_refs, token = refs[n_in + n_after + n_bufs:-1], refs[-1]
        for j, (job, (i0, b0, _)) in enumerate(zip(jobs, layout)):
            out_refs = buf_refs[b0:b0 + len(job.out_shape)]
            for d in job.build(in_refs[i0:i0 + len(job.ins)], out_refs, _Copies(sem_refs[2 * j], sem_refs[2 * j + 1])):
                d.start()
        token[...] = jnp.zeros_like(token)

    sems = [pltpu.SemaphoreType.DMA((job.n_remote,)) for job in jobs for _ in range(2)]
    outs = pl.pallas_call(
        body, name=name, in_specs=[HBM] * n_in + [ANY] * n_after,
        out_specs=[HBM] * n_bufs + [SEM] * len(sems) + [pl.BlockSpec(memory_space=pltpu.VMEM)],
        out_shape=out_shape + sems + [jax.ShapeDtypeStruct((8, 128), F32)], input_output_aliases=aliases,
        compiler_params=pltpu.CompilerParams(has_side_effects=pltpu.SideEffectType.DATAFLOW_SIDE_EFFECTING))(*ins, *after)
    return [_Flying(job, outs[n_bufs + 2 * j], outs[n_bufs + 2 * j + 1], list(outs[b0:b0 + len(job.out_shape) + n_kept]), outs[-1])
            for j, (job, (_, b0, n_kept)) in enumerate(zip(jobs, layout))]


def _wait_job(flying, name, after=()):
    job, n_bufs, n_after = flying.job, len(flying.bufs), len(after)

    def body(*refs):
        in_refs, out_refs = _job_refs(job, refs[:n_bufs])
        send_sems, recv_sems = refs[n_bufs:n_bufs + 2]
        copies = job.build(in_refs, out_refs, _Copies(send_sems, recv_sems))
        for d in copies:
            d.wait_send()
        for d in copies:
            d.wait_recv()

    outs = pl.pallas_call(
        body, name=name, in_specs=[HBM] * n_bufs + [SEM, SEM] + [ANY] * n_after, out_specs=[HBM] * n_bufs,
        out_shape=[jax.ShapeDtypeStruct(b.shape, b.dtype) for b in flying.bufs],
        input_output_aliases={i: i for i in range(n_bufs)},
        compiler_params=pltpu.CompilerParams(has_side_effects=pltpu.SideEffectType.DATAFLOW_SIDE_EFFECTING))(
            *flying.bufs, flying.send_sems, flying.recv_sems, *after)
    return list(outs[:len(job.out_shape)]), list(outs[len(job.out_shape):])


def _in_place(arrays):
    return [jax.ShapeDtypeStruct(a.shape, a.dtype) for a in arrays], {u: u for u in range(len(arrays))}


def _rows_part(ref, part, n_parts):
    h = ref.shape[0] // n_parts
    return ref.at[pl.ds(part * h, h), :]


def _gather_job(full, kind, stage):
    def build(in_refs, out_refs, cp):
        x, y, c, me, (y_nbr, x_nbr, diagonal) = _place()
        (ref,) = out_refs
        sibling = (x, y, 1 - c)
        if stage == 1:
            mine = _piece(ref, kind, me, c)
            return [cp.remote(mine, mine, (*y_nbr, c)), cp.remote(mine, mine, (*x_nbr, c))]
        from_y, from_x = _piece(ref, kind, _chip_index(y_nbr), c), _piece(ref, kind, _chip_index(x_nbr), c)
        if stage == 2:
            relay_0, relay_1 = _rows_part(from_x, 0, 2), _rows_part(from_y, 1, 2)
            return [cp.remote(relay_0, relay_0, (*y_nbr, c)), cp.remote(relay_1, relay_1, (*x_nbr, c)),
                    cp.remote(from_y, from_y, sibling), cp.remote(from_x, from_x, sibling)]
        from_diagonal = _piece(ref, kind, _chip_index(diagonal), c)
        return [cp.remote(from_diagonal, from_diagonal, sibling)]

    return _Job([full], *_in_place([full]), {1: 2, 2: 4, 3: 1}[stage], build)


def _gather_small_job(fulls, axes):
    def build(in_refs, out_refs, cp):
        x, y, c, me, chips = _place()
        copies = []
        for ref, ax in zip(out_refs, axes):
            n = ref.shape[ax] // N_CHIPS
            idx = [slice(None)] * len(ref.shape)
            idx[ax] = pl.ds(_aligned(me * n, n), n)
            mine = ref.at[tuple(idx)]
            copies += [cp.remote(mine, mine, (*chip, c)) for chip in chips]
        return copies

    return _Job(fulls, *_in_place(fulls), 3 * len(fulls), build)


def _exchange_halves_job(grads, kinds):
    def build(in_refs, out_refs, cp):
        x, y, c, me, chips = _place()
        copies = []
        for src, dst, kind in zip(in_refs, out_refs, kinds):
            if kind == "col":
                copies.append(cp.remote(_half_rows(src, 1 - c), dst, (x, y, 1 - c)))
            else:
                copies += [cp.remote(_piece(src, "row", k, 1 - c), _compact_piece(dst, "row", k), (x, y, 1 - c))
                           for k in range(N_CHIPS)]
        return copies

    out_shape = [jax.ShapeDtypeStruct((g.shape[0] // 2, g.shape[1]), g.dtype) for g in grads]
    return _Job(grads, out_shape, {}, sum(1 if k == "col" else N_CHIPS for k in kinds), build)


def _scatter_job(half, kind, n_parts, parts, into=None):
    def build(in_refs, out_refs, cp):
        x, y, c, me, chips = _place()
        src, (dst,) = in_refs[0], out_refs
        copies = []
        for p in parts:
            copies += [cp.remote(_rows_part(_compact_piece(src, kind, _chip_index(chip)), p, n_parts),
                                 _rows_part(dst.at[r], p, n_parts), (*chip, c)) for r, chip in enumerate(chips)]
        return copies

    part_shape = (half.shape[0], half.shape[1] // N_CHIPS) if kind == "col" else (half.shape[0] // N_CHIPS, half.shape[1])
    out_shape = [jax.ShapeDtypeStruct((N_CHIPS - 1,) + part_shape, half.dtype)]
    ins, aliases = ([half], {}) if into is None else ([half, into], {1: 0})
    return _Job(ins, out_shape, aliases, 3 * len(parts), build)


def _share_job(shards):
    def build(in_refs, out_refs, cp):
        x, y, c, me, chips = _place()
        copies = []
        for ref in out_refs:
            mine = _half_rows(ref, c)
            copies.append(cp.remote(mine, mine, (x, y, 1 - c)))
        return copies

    return _Job(shards, *_in_place(shards), len(shards), build)


N_DEVICES = 2 * N_CHIPS


def _small_exchange_job(slots, n_parts, parts):
    def build(in_refs, out_refs, cp):
        x, y, c, me, chips = _place()
        (ref,) = out_refs
        copies = []
        for part in parts:
            mine = _rows_part(ref.at[2 * me + c], part, n_parts)
            copies += [cp.remote(mine, mine, (x ^ (p >> 2), y ^ ((p >> 1) & 1), c ^ (p & 1))) for p in range(1, N_DEVICES)]
        return copies

    return _Job([slots], *_in_place([slots]), (N_DEVICES - 1) * len(parts), build)


def _in_own_slot(packed, place, name):
    R, C = packed.shape

    def body(s_ref, p_ref, o_ref):
        o_ref[...] = p_ref[...]

    return _placed_call(body, name, place, (1,), [pl.BlockSpec((R, C), lambda i, s: (0, 0))],
                        pl.BlockSpec((None, R, C), lambda i, s: (2 * s[0] + s[1], 0, 0)),
                        jax.ShapeDtypeStruct((N_DEVICES, R, C), F32), [packed])


def _sum_slots(slots, name):
    n, R, C = slots.shape

    def body(s_ref, o_ref):
        acc = s_ref[0]
        for i in range(1, n):
            acc = acc + s_ref[i]
        o_ref[...] = acc

    return _call(body,name=name, grid=(1,), in_specs=[pl.BlockSpec((n, R, C), lambda i: (0, 0, 0))],
                 out_specs=pl.BlockSpec((R, C), lambda i: (0, 0)), out_shape=jax.ShapeDtypeStruct((R, C), F32),
                 sem=("arbitrary",))(slots)


def _packed_rows(size, width):
    return -(-size // (8 * width)) * 8


def _pack(arrays, width):
    rows = []
    for a in arrays:
        flat = a.reshape(-1)
        n_rows = _packed_rows(flat.shape[0], width)
        rows.append(jnp.pad(flat, (0, n_rows * width - flat.shape[0])).reshape(n_rows, width))
    return jnp.concatenate(rows, axis=0)


def _unpack(packed, shapes):
    out, r0, width = [], 0, packed.shape[1]
    for shape in shapes:
        size = 1
        for d in shape:
            size *= d
        out.append(packed[r0:r0 + _packed_rows(size, width)].reshape(-1)[:size].reshape(shape))
        r0 += _packed_rows(size, width)
    return out


class _Backlog:
    def __init__(self, first):
        self.now, self.free, self.flights, self.last, self.chain = 0.0, {"ici": 0.0, "d2d": 0.0}, [], first, []
        self.asked, self.n_starts = [], 0

    def run(self, fn, us, *args, **kw):
        self.flush()
        out = fn(*args, **kw)
        self.now += us
        self.last = out[0] if isinstance(out, (list, tuple)) else out
        self.poll()
        return out

    def start(self, job, name, link, cost, done):
        self.asked.append((job, name, link, cost, done))

    def flush(self):
        if self.asked:
            asked, self.asked = self.asked, []
            flights = _start_jobs([a[0] for a in asked], "start_%d" % self.n_starts, self.chain)
            self.n_starts += 1
            self.chain = [flights[0].token]
            _Behind.pending.append(flights[0].token)
            for (job, name, link, cost, done), flying in zip(asked, flights):
                ends = max(self.now, self.free[link]) + cost
                self.free[link] = ends
                self.flights.append((ends + LANDING_SLACK_US, name, flying, done))
            self.flights.sort(key=lambda f: f[0])

    def poll(self, block=False):
        self.flush()
        while self.flights and (block or self.flights[0][0] <= self.now):
            ends, name, flying, done = self.flights.pop(0)
            self.now, block = max(self.now, ends), False
            done(*_wait_job(flying, name + "_wait", [self.last] + self.chain))
        self.flush()


class _GatherStream:
    def __init__(self, backlog, bufs, kinds, costs):
        self.backlog, self.bufs, self.kinds, self.costs, self.begun, self.complete = backlog, bufs, kinds, costs, 0, set()
        self.begin()

    def begin(self):
        u, self.begun = self.begun, self.begun + 1
        self.backlog.start(_gather_job(self.bufs[u], self.kinds[u], 1), "gather_%d" % u, "ici", 0.5 * self.costs[u],
                           lambda outs, kept: self.arrived(u, outs[0]))

    def arrived(self, u, buf):
        self.bufs[u] = buf
        self.backlog.start(_gather_job(buf, self.kinds[u], 2), "relay_%d" % u, "ici", 0.25 * self.costs[u],
                           lambda outs, kept: self.relayed(u, outs[0]))
        while self.begun <= min(u + (1 if u < STARVED_WEIGHTS else 2), len(self.bufs) - 1):
            self.begin()

    def relayed(self, u, buf):
        self.bufs[u] = buf
        self.backlog.start(_gather_job(buf, self.kinds[u], 3), "handon_%d" % u, "d2d", D2D_SHARE * self.costs[u],
                           lambda outs, kept: self.handed(u, outs[0]))

    def handed(self, u, buf):
        self.bufs[u] = buf
        self.complete.add(u)

    def ready(self, u):
        while u not in self.complete:
            assert self.backlog.flights or self.backlog.asked, "weight %d is not on its way" % u
            self.backlog.poll(block=True)
        return self.bufs[u]


class _GradStream:
    def __init__(self, backlog, u, name, kind, cost, g, place, results):
        self.backlog, self.u, self.name, self.kind, self.cost, self.place, self.results = backlog, u, name, kind, cost, place, results
        backlog.start(_exchange_halves_job([g], [kind]), "to_sibling_" + name, "d2d", D2D_SHARE * cost, self.exchanged)

    def exchanged(self, outs, kept):
        chip_sum = self.backlog.run(_add_pair, SIDE_KERNEL_US, kept[0], outs[0], self.kind, self.place, "chip_sum_" + self.name)
        self.backlog.start(_scatter_job(chip_sum, self.kind, 1, [0]), "to_owners_" + self.name, "ici", self.cost, self.scattered)

    def scattered(self, outs, kept):
        reduced = self.backlog.run(_sum_chips, SIDE_KERNEL_US, kept[0], outs[0], self.kind, self.place, "reduce_" + self.name)
        self.backlog.start(_share_job([reduced]), "share_" + self.name, "d2d", D2D_SHARE * self.cost, self.shared)

    def shared(self, outs, kept):
        self.results[self.u] = outs[0]


SIDE_KERNEL_US = 12.0
D2D_SHARE = 0.15
LANDING_SLACK_US = 5.0
STARVED_WEIGHTS = 8


def kernel(x, mix_pre_g, mix_post_g, ffn_pre_g, ffn_post_g, ab_w_in, pool_w, pool_scale, conv_w, conv_b, conv_ln_g, conv_ln_b, ab_w_out, sc_w_in, sc_conv_w, sc_w_out, ffn_w1, ffn_w2, loss_target, m_mix_pre_g, m_mix_post_g, m_ffn_pre_g, m_ffn_post_g, m_ab_w_in, m_pool_w, m_pool_scale, m_conv_w, m_conv_b, m_conv_ln_g, m_conv_ln_b, m_ab_w_out, m_sc_w_in, m_sc_conv_w, m_sc_w_out, m_ffn_w1, m_ffn_w2, v_mix_pre_g, v_mix_post_g, v_ffn_pre_g, v_ffn_post_g, v_ab_w_in, v_pool_w, v_pool_scale, v_conv_w, v_conv_b, v_conv_ln_g, v_conv_ln_b, v_ab_w_out, v_sc_w_in, v_sc_conv_w, v_sc_w_out, v_ffn_w1, v_ffn_w2):
    x0, target = x[0], loss_target[0]
    T, D = x0.shape
    DP = pool_scale.shape[-1]
    gain = lambda g, layer: g[layer][None, :]

    big = [("ab_w_in", ab_w_in, 0, "col", 4, 67.0), ("ab_w_out", ab_w_out, 0, "row", 2, 44.0),
           ("ffn_w1_0", ffn_w1, 0, "col", 8, 177.0), ("ffn_w2_0", ffn_w2, 0, "row", 8, 177.0),
           ("sc_w_in", sc_w_in, 0, "col", 4, 133.0), ("sc_w_out", sc_w_out, 0, "row", 2, 44.0),
           ("ffn_w1_1", ffn_w1, 1, "col", 8, 177.0), ("ffn_w2_1", ffn_w2, 1, "row", 8, 177.0)]
    kinds = [b[3] for b in big]
    chip = 2 * lax.axis_index("x") + lax.axis_index("y")
    place = jnp.stack([chip, lax.axis_index("c")]).astype(jnp.int32)

    def own_in_zeros(shard, ax):
        full = jnp.zeros(tuple(d * N_CHIPS if i == ax else d for i, d in enumerate(shard.shape)), shard.dtype)
        return lax.dynamic_update_slice_in_dim(full, shard, chip * shard.shape[ax], axis=ax)

    W = [_cast_into_full(w, layer, kind, place, "cast_" + name) for name, w, layer, kind, _, _ in big]
    smalls = [own_in_zeros(pool_w[0], 1), own_in_zeros(conv_w[0], 1), own_in_zeros(sc_conv_w[0], 1)]
    backlog = _Backlog(x0)
    run = backlog.run
    small_weights = []
    backlog.start(_gather_small_job(smalls, [1, 1, 1]), "gather_small", "ici", 6.0, lambda outs, kept: small_weights.extend(outs))
    gather = _GatherStream(backlog, W, kinds, [b[5] for b in big])

    relu_sq = lambda acc: (jnp.maximum(acc, 0.0), jnp.square(jnp.maximum(acc, 0.0)))
    relu_sq_bwd = lambda acc, a: (acc * (2.0 * a.astype(F32)),)

    cast_done = [w[:8, :128] for w in W]
    _Behind.pending += cast_done[1:4]
    h0 = run(_norm_fwd, 12.0, x0, gain(mix_pre_g, 0), "norm_in")
    w_ready = gather.ready(0)
    _Behind.pending += cast_done[4:]
    z0 = run(_matmul, 35.0, h0, w_ready, "nn", "mix0_in")
    while not small_weights:
        backlog.poll(block=True)
    pool_w_full, conv_w_full, sc_conv_w_full = small_weights
    pooled, y_pool = run(_pool_fwd, 23.0, z0, pool_w_full, pool_scale, "pool_fwd")
    a_conv, c_conv = run(_conv_fwd, 25.0, z0, conv_w_full, conv_b, DP, "conv_fwd")
    y_conv = run(_ln_silu_fwd, 10.0, c_conv, conv_ln_g, conv_ln_b, "ln_silu_fwd")
    y0 = jnp.concatenate([y_pool, y_conv], axis=1)
    m0 = run(_matmul, 25.0, y0, gather.ready(1), "nn", "mix0_out")
    x1, h1 = run(_residual_norm, 21.0, x0, m0, gain(mix_post_g, 0), gain(ffn_pre_g, 0), "res_mix0")
    a0, a0sq = run(_matmul, 81.0, h1, gather.ready(2), "nn", "ffn0_up", out_dtypes=(BF16, BF16), epilogue=relu_sq)
    f0 = run(_matmul, 84.0, a0sq, gather.ready(3), "nn", "ffn0_down", tk=LONG_K_TILE)
    x2, h2 = run(_residual_norm, 22.0, x1, f0, gain(ffn_post_g, 0), gain(mix_pre_g, 1), "res_ffn0")
    z1 = run(_matmul, 62.0, h2, gather.ready(4), "nn", "mix1_in")
    y1 = run(_short_fwd, 22.0, z1, sc_conv_w_full, "short_fwd")
    m1 = run(_matmul, 25.0, y1, gather.ready(5), "nn", "mix1_out")
    x3, h3 = run(_residual_norm, 21.0, x2, m1, gain(mix_post_g, 1), gain(ffn_pre_g, 1), "res_mix1")
    a1, a1sq = run(_matmul, 81.0, h3, gather.ready(6), "nn", "ffn1_up", out_dtypes=(BF16, BF16), epilogue=relu_sq)
    f1 = run(_matmul, 84.0, a1sq, gather.ready(7), "nn", "ffn1_down", tk=LONG_K_TILE)
    w_in0, w_out0, w1_0, w2_0, w_in1, w_out1, w1_1, w2_1 = W

    grads_big = [None] * len(big)


    def reduce_grad(u, g):
        name, _, _, kind, _, cost = big[u]
        _GradStream(backlog, u, name, kind, cost, g, place, grads_big)

    dx, df1, d_ffn_post_1, loss_row = run(_loss_and_last_norm_bwd, 30.0, x3, f1, gain(ffn_post_g, 1), target, "loss")
    reduce_grad(7, run(_matmul, 80.0, a1sq, df1, "tn", "ffn1_down_dw", out_dtypes=(BF16,)))
    dz = run(_matmul, 82.0, df1, w2_1, "nt", "ffn1_down_dx", out_dtypes=(BF16,), epilogue=relu_sq_bwd, epi=(a1,))
    reduce_grad(6, run(_matmul, 80.0, h3, dz, "tn", "ffn1_up_dw", out_dtypes=(BF16,)))
    dh = run(_matmul, 87.0, dz, w1_1, "nt", "ffn1_up_dx", tk=LONG_K_TILE)
    dx, d_ffn_pre_1, dm1, d_mix_post_1 = run(_norms_bwd, 36.0, dx, dh, x3, gain(ffn_pre_g, 1), m1, gain(mix_post_g, 1), "norms_bwd3")

    reduce_grad(5, run(_matmul, 24.0, y1, dm1, "tn", "mix1_out_dw", out_dtypes=(BF16,)))
    dy1 = run(_matmul, 25.0, dm1, w_out1, "nt", "mix1_out_dx")
    db, dcg, du, d_sc_conv_w = run(_short_bwd, 41.0, dy1, z1, sc_conv_w_full, "short_bwd")
    dz1 = jnp.concatenate([db, dcg, du], axis=1)
    reduce_grad(4, run(_matmul, 62.0, h2, dz1, "tn", "mix1_in_dw", out_dtypes=(BF16,)))
    dh = run(_matmul, 68.0, dz1, w_in1, "nt", "mix1_in_dx")
    dx, d_mix_pre_1, df0, d_ffn_post_0 = run(_norms_bwd, 35.0, dx, dh, x2, gain(mix_pre_g, 1), f0, gain(ffn_post_g, 0), "norms_bwd2")

    reduce_grad(3, run(_matmul, 80.0, a0sq, df0, "tn", "ffn0_down_dw", out_dtypes=(BF16,)))
    dz = run(_matmul, 82.0, df0, w2_0, "nt", "ffn0_down_dx", out_dtypes=(BF16,), epilogue=relu_sq_bwd, epi=(a0,))
    reduce_grad(2, run(_matmul, 80.0, h1, dz, "tn", "ffn0_up_dw", out_dtypes=(BF16,)))
    dh = run(_matmul, 87.0, dz, w1_0, "nt", "ffn0_up_dx", tk=LONG_K_TILE)
    dx, d_ffn_pre_0, dm0, d_mix_post_0 = run(_norms_bwd, 36.0, dx, dh, x1, gain(ffn_pre_g, 0), m0, gain(mix_post_g, 0), "norms_bwd1")

    reduce_grad(1, run(_matmul, 24.0, y0, dm0, "tn", "mix0_out_dw", out_dtypes=(BF16,)))
    dy0 = run(_matmul, 25.0, dm0, w_out0, "nt", "mix0_out_dx")
    du_pool, d_pool_w, d_pool_scale = run(_pool_bwd, 28.0, dy0, pooled, pool_w_full, pool_scale, "pool_bwd")
    dc, d_ln_g, d_ln_b = run(_ln_silu_bwd, 15.0, dy0, c_conv, conv_ln_g, conv_ln_b, "ln_silu_bwd")
    dv, dgate, d_conv_w, d_conv_b = run(_conv_bwd, 52.0, dc, a_conv, z0, conv_w_full, DP, "conv_bwd")
    dz0 = jnp.concatenate([du_pool, dv, dgate], axis=1)

    small_sums = {}

    def exchange_small(key, arrays, cost):
        slots = _in_own_slot(_pack(arrays, D), place, "small_grads_slot_" + key)
        backlog.start(_small_exchange_job(slots, 1, [0]), "small_grads_" + key, "ici", cost,
                      lambda outs, kept: small_sums.__setitem__(key, _unpack(_sum_slots(outs[0], "small_grads_sum_" + key),
                                                                             [a.shape for a in arrays])))

    exchange_small("most", [d_mix_pre_1, jnp.concatenate([d_mix_post_0, d_mix_post_1], 0),
                            jnp.concatenate([d_ffn_pre_0, d_ffn_pre_1], 0), jnp.concatenate([d_ffn_post_0, d_ffn_post_1], 0),
                            d_pool_scale, d_conv_b, d_ln_g, d_ln_b, d_pool_w, d_conv_w, d_sc_conv_w], 112.0)
    reduce_grad(0, run(_matmul, 34.0, h0, dz0, "tn", "mix0_in_dw", out_dtypes=(BF16,)))
    dh = run(_matmul, 40.0, dz0, w_in0, "nt", "mix0_in_dx")
    grad_x, d_mix_pre_0 = run(_norms_bwd, 26.0, dx, dh, x0, gain(mix_pre_g, 0), None, None, "norms_bwd0")
    exchange_small("last", [d_mix_pre_0, loss_row], 5.0)

    upd, gr, first = {}, grads_big, {}

    def keep(where, key, outs):
        where[key] = outs
        return outs

    adamw_big = [
        (7, lambda: keep(first, "ffn_w2", _adamw(ffn_w2, gr[7], m_ffn_w2, v_ffn_w2, "adamw_ffn_w2_1", layer=1)), 46.0),
        (6, lambda: keep(first, "ffn_w1", _adamw(ffn_w1, gr[6], m_ffn_w1, v_ffn_w1, "adamw_ffn_w1_1", layer=1)), 46.0),
        (5, lambda: keep(upd, "sc_w_out", _adamw(sc_w_out, gr[5], m_sc_w_out, v_sc_w_out, "adamw_sc_w_out")), 14.0),
        (4, lambda: keep(upd, "sc_w_in", _adamw(sc_w_in, gr[4], m_sc_w_in, v_sc_w_in, "adamw_sc_w_in")), 35.0),
        (3, lambda: keep(upd, "ffn_w2", _adamw(ffn_w2, gr[3], m_ffn_w2, v_ffn_w2, "adamw_ffn_w2_0", layer=0,
                                               carried=first["ffn_w2"])), 46.0),
        (2, lambda: keep(upd, "ffn_w1", _adamw(ffn_w1, gr[2], m_ffn_w1, v_ffn_w1, "adamw_ffn_w1_0", layer=0,
                                               carried=first["ffn_w1"])), 46.0),
        (1, lambda: keep(upd, "ab_w_out", _adamw(ab_w_out, gr[1], m_ab_w_out, v_ab_w_out, "adamw_ab_w_out")), 14.0),
        (0, lambda: keep(upd, "ab_w_in", _adamw(ab_w_in, gr[0], m_ab_w_in, v_ab_w_in, "adamw_ab_w_in")), 19.0)]
    while adamw_big or backlog.flights:
        due = [a for a in adamw_big if gr[a[0]] is not None]
        if due:
            adamw_big.remove(due[0])
            backlog.run(due[0][1], due[0][2])
        else:
            backlog.poll(block=True)

    (g_mix_pre_1, g_mix_post, g_ffn_pre, g_ffn_post, g_pool_scale, g_conv_b, g_ln_g, g_ln_b, g_pool_w_full, g_conv_w_full,
     g_sc_conv_w_full) = small_sums["most"]
    g_mix_pre = jnp.concatenate([small_sums["last"][0], g_mix_pre_1], 0)
    loss = small_sums["last"][1][0, 0]
    own = lambda a, ax: lax.dynamic_slice_in_dim(a, chip * (a.shape[ax] // N_CHIPS), a.shape[ax] // N_CHIPS, axis=ax)
    g_pool_w, g_conv_w, g_sc_conv_w = own(g_pool_w_full, 1), own(g_conv_w_full, 1), own(g_sc_conv_w_full, 1)

    def small_update(w, g, m, v, name):
        shape = w.shape
        as3 = lambda a: a.reshape((1, -1, shape[-1]))
        outs = _adamw(as3(w), g.reshape((-1, shape[-1])), as3(m), as3(v), "adamw_" + name)
        return [o.reshape(shape) for o in outs]

    upd["mix_pre_g"] = small_update(mix_pre_g, g_mix_pre, m_mix_pre_g, v_mix_pre_g, "mix_pre_g")
    upd["mix_post_g"] = small_update(mix_post_g, g_mix_post, m_mix_post_g, v_mix_post_g, "mix_post_g")
    upd["ffn_pre_g"] = small_update(ffn_pre_g, g_ffn_pre, m_ffn_pre_g, v_ffn_pre_g, "ffn_pre_g")
    upd["ffn_post_g"] = small_update(ffn_post_g, g_ffn_post, m_ffn_post_g, v_ffn_post_g, "ffn_post_g")
    upd["pool_w"] = small_update(pool_w, g_pool_w, m_pool_w, v_pool_w, "pool_w")
    upd["pool_scale"] = small_update(pool_scale, g_pool_scale, m_pool_scale, v_pool_scale, "pool_scale")
    upd["conv_w"] = small_update(conv_w, g_conv_w, m_conv_w, v_conv_w, "conv_w")
    upd["conv_b"] = small_update(conv_b, g_conv_b, m_conv_b, v_conv_b, "conv_b")
    upd["conv_ln_g"] = small_update(conv_ln_g, g_ln_g, m_conv_ln_g, v_conv_ln_g, "conv_ln_g")
    upd["conv_ln_b"] = small_update(conv_ln_b, g_ln_b, m_conv_ln_b, v_conv_ln_b, "conv_ln_b")
    upd["sc_conv_w"] = small_update(sc_conv_w, g_sc_conv_w, m_sc_conv_w, v_sc_conv_w, "sc_conv_w")

    order = ["mix_pre_g", "mix_post_g", "ffn_pre_g", "ffn_post_g", "ab_w_in", "pool_w", "pool_scale", "conv_w", "conv_b",
             "conv_ln_g", "conv_ln_b", "ab_w_out", "sc_w_in", "sc_conv_w", "sc_w_out", "ffn_w1", "ffn_w2"]
    out = [loss, grad_x[None]]
    for part in range(4):
        out += [upd[n][part] for n in order]
    return tuple(out)
```

```python
import jax
import jax.numpy as jnp
from jax import lax
from jax.experimental import pallas as pl
from jax.experimental.pallas import tpu as pltpu

F32, BF16 = jnp.float32, jnp.bfloat16
EPS = 1e-6
N_GROUPS = 4
MAX_WINDOW = 16
CONV_K = 31
SHORT_K = 3
CONV_PAD = 32
SHORT_PAD = 8
ADAM_LR, ADAM_B1, ADAM_B2, ADAM_EPS, ADAM_WD, ADAM_STEP = 0.001, 0.9, 0.999, 1e-08, 0.01, 10
N_CHIPS = 4
VMEM_LIMIT_BYTES = 56 * 1024 * 1024
ROW_TILE = 256
CHUNK = 256
LONG_K_TILE = 4096
WIDE_N_TILE = 2048
MESH = pl.DeviceIdType.MESH
HBM = pl.BlockSpec(memory_space=pltpu.HBM)
SEM = pl.BlockSpec(memory_space=pltpu.SEMAPHORE)
ANY = pl.BlockSpec(memory_space=pl.ANY)


def _cp(*sem):
    return pltpu.CompilerParams(dimension_semantics=sem, vmem_limit_bytes=VMEM_LIMIT_BYTES)


def _sigmoid(v):
    return 1.0 / (1.0 + jnp.exp(-v))


class _Behind:
    pending = []


def _call(body, prefetch=None, **kw):
    behind, _Behind.pending = _Behind.pending, []
    single = not isinstance(kw["out_shape"], (list, tuple))
    in_specs, scratch = list(kw["in_specs"]), list(kw.get("scratch_shapes", ()))
    out_shape = [kw["out_shape"]] if single else list(kw["out_shape"])
    out_specs = [kw["out_specs"]] if single else list(kw["out_specs"])
    n_pre = 0 if prefetch is None else 1
    n_own, n_behind = len(in_specs), len(behind)

    def wrapped(*refs):
        body(*refs[:n_pre + n_own], *refs[n_pre + n_own + n_behind:])

    specs = dict(grid=kw["grid"], in_specs=in_specs + [ANY] * n_behind, out_specs=out_specs)
    if prefetch is None:
        specs["scratch_shapes"] = scratch
    else:
        specs = dict(grid_spec=pltpu.PrefetchScalarGridSpec(num_scalar_prefetch=1, scratch_shapes=scratch, **specs))
    aliases = {n_pre + i: o for i, o in kw.get("input_output_aliases", {}).items()}
    call = pl.pallas_call(wrapped, name=kw["name"], out_shape=out_shape, input_output_aliases=aliases,
                          compiler_params=_cp(*kw["sem"]), **specs)

    def run(*args):
        outs = call(*([prefetch] * n_pre), *args, *behind)
        return outs[0] if single else list(outs)

    return run


_DIMS = {"nn": (((1,), (0,)), ((), ())), "nt": (((1,), (1,)), ((), ())), "tn": (((0,), (0,)), ((), ()))}


def _pick(n, cap, step=256):
    if n <= cap:
        return n
    return next(t for t in range(cap - cap % step, 0, -step) if n % t == 0)


def _matmul(a, b, mode, name, out_dtypes=(F32,), epilogue=None, epi=(), tm=1024, tn=1024, tk=2048):
    if mode == "tn":
        (K, M), (K2, N) = a.shape, b.shape
    elif mode == "nt":
        (M, K), (N, K2) = a.shape, b.shape
    else:
        (M, K), (K2, N) = a.shape, b.shape
    assert K == K2
    tm, tn, tk = _pick(M, tm), _pick(N, tn), _pick(K, tk)
    nk = K // tk
    a_spec = pl.BlockSpec((tk, tm), lambda i, j, k: (k, i)) if mode == "tn" else pl.BlockSpec((tm, tk), lambda i, j, k: (i, k))
    b_spec = pl.BlockSpec((tn, tk), lambda i, j, k: (j, k)) if mode == "nt" else pl.BlockSpec((tk, tn), lambda i, j, k: (k, j))
    o_spec = pl.BlockSpec((tm, tn), lambda i, j, k: (i, j))
    n_epi, n_out = len(epi), len(out_dtypes)

    def body(a_ref, b_ref, *rest):
        epi_refs, out_refs, scratch = rest[:n_epi], rest[n_epi:n_epi + n_out], rest[n_epi + n_out:]
        part = lax.dot_general(a_ref[...].astype(BF16), b_ref[...].astype(BF16), _DIMS[mode], preferred_element_type=F32)

        def finish(acc):
            outs = epilogue(acc, *[r[...] for r in epi_refs]) if epilogue else (acc,)
            for o_ref, o in zip(out_refs, outs):
                o_ref[...] = o.astype(o_ref.dtype)

        if nk == 1:
            finish(part)
        else:
            acc_ref = scratch[0]
            k = pl.program_id(2)

            @pl.when(k == 0)
            def _():
                acc_ref[...] = part

            @pl.when(k > 0)
            def _():
                acc_ref[...] += part

            @pl.when(k == nk - 1)
            def _():
                finish(acc_ref[...])

    outs = _call(
        body, name=name, grid=(M // tm, N // tn, nk),
        in_specs=[a_spec, b_spec] + [o_spec] * n_epi, out_specs=[o_spec] * n_out,
        out_shape=[jax.ShapeDtypeStruct((M, N), dt) for dt in out_dtypes],
        scratch_shapes=[pltpu.VMEM((tm, tn), F32)] if nk > 1 else [],
        sem=("parallel", "parallel", "arbitrary"))(a, b, *epi)
    return outs[0] if n_out == 1 else outs


def _rms(x, g):
    r = lax.rsqrt(jnp.mean(x * x, axis=-1, keepdims=True) + EPS)
    return x * r * g


def _rms_bwd(dy, x, g):
    r = lax.rsqrt(jnp.mean(x * x, axis=-1, keepdims=True) + EPS)
    xn = x * r
    dyg = dy * g
    dx = r * (dyg - xn * jnp.mean(dyg * xn, axis=-1, keepdims=True))
    return dx, jnp.sum(dy * xn, axis=0, keepdims=True)


def _rows(d, tr=ROW_TILE):
    return pl.BlockSpec((tr, d), lambda i: (i, 0))


def _vec(d):
    return pl.BlockSpec((1, d), lambda i: (0, 0))


def _accumulate(ref, val):
    @pl.when(pl.program_id(0) == 0)
    def _():
        ref[...] = val

    @pl.when(pl.program_id(0) > 0)
    def _():
        ref[...] += val


def _norm_fwd(x, g, name):
    T, D = x.shape

    def body(x_ref, g_ref, h_ref):
        h_ref[...] = _rms(x_ref[...], g_ref[...]).astype(BF16)

    return _call(body,name=name,grid=(T // ROW_TILE,), in_specs=[_rows(D), _vec(D)], out_specs=_rows(D),
                          out_shape=jax.ShapeDtypeStruct((T, D), BF16), sem=("parallel",))(x, g)


def _residual_norm(x, m, g_post, g_next, name):
    T, D = x.shape

    def body(x_ref, m_ref, gp_ref, gn_ref, xo_ref, h_ref):
        xo = x_ref[...] + _rms(m_ref[...], gp_ref[...])
        xo_ref[...] = xo
        h_ref[...] = _rms(xo, gn_ref[...]).astype(BF16)

    return _call(body,name=name,grid=(T // ROW_TILE,), in_specs=[_rows(D), _rows(D), _vec(D), _vec(D)],
                          out_specs=[_rows(D), _rows(D)],
                          out_shape=[jax.ShapeDtypeStruct((T, D), F32), jax.ShapeDtypeStruct((T, D), BF16)],
                          sem=("parallel",))(x, m, g_post, g_next)


def _loss_and_last_norm_bwd(x, m, g_post, target, name):
    T, D = x.shape

    def body(x_ref, m_ref, gp_ref, t_ref, dx_ref, dm_ref, dg_ref, loss_ref):
        m_val, gp = m_ref[...], gp_ref[...]
        err = x_ref[...] + _rms(m_val, gp) - t_ref[...]
        dx = err * (1.0 / D)
        dx_ref[...] = dx
        dm, dg = _rms_bwd(dx, m_val, gp)
        dm_ref[...] = dm.astype(BF16)
        _accumulate(dg_ref, dg)
        _accumulate(loss_ref, jnp.full((1, 128), 0.5 * jnp.sum(err * err) * (1.0 / D), F32))

    return _call(body,name=name,grid=(T // ROW_TILE,), in_specs=[_rows(D), _rows(D), _vec(D), _rows(D)],
                          out_specs=[_rows(D), _rows(D), _vec(D), _vec(128)],
                          out_shape=[jax.ShapeDtypeStruct((T, D), F32), jax.ShapeDtypeStruct((T, D), BF16),
                                     jax.ShapeDtypeStruct((1, D), F32), jax.ShapeDtypeStruct((1, 128), F32)],
                          sem=("arbitrary",))(x, m, g_post, target)


def _norms_bwd(dx, dh, x_in, g_pre, m_prev, g_post_prev, name):
    T, D = dx.shape
    with_prev = m_prev is not None

    def body(*refs):
        if with_prev:
            dx_ref, dh_ref, x_ref, gq_ref, m_ref, gp_ref, dxo_ref, dgq_ref, dm_ref, dgp_ref = refs
        else:
            dx_ref, dh_ref, x_ref, gq_ref, dxo_ref, dgq_ref = refs
        d_in, dgq = _rms_bwd(dh_ref[...], x_ref[...], gq_ref[...])
        dxo = dx_ref[...] + d_in
        dxo_ref[...] = dxo
        _accumulate(dgq_ref, dgq)
        if with_prev:
            dm, dgp = _rms_bwd(dxo, m_ref[...], gp_ref[...])
            dm_ref[...] = dm.astype(BF16)
            _accumulate(dgp_ref, dgp)

    ins, in_specs = [dx, dh, x_in, g_pre], [_rows(D), _rows(D), _rows(D), _vec(D)]
    out_specs = [_rows(D), _vec(D)]
    out_shape = [jax.ShapeDtypeStruct((T, D), F32), jax.ShapeDtypeStruct((1, D), F32)]
    if with_prev:
        ins += [m_prev, g_post_prev]
        in_specs += [_rows(D), _vec(D)]
        out_specs += [_rows(D), _vec(D)]
        out_shape += [jax.ShapeDtypeStruct((T, D), BF16), jax.ShapeDtypeStruct((1, D), F32)]
    return _call(body,name=name,grid=(T // ROW_TILE,), in_specs=in_specs, out_specs=out_specs, out_shape=out_shape,
                          sem=("arbitrary",))(*ins)


def _window_weights(g):
    w = 2 << g
    return w, [jnp.where(j < w, 1.0, 0.0).astype(F32) for j in range(MAX_WINDOW)]


def _valid_count(r0, rows, w):
    t = (lax.broadcasted_iota(jnp.int32, (rows, 1), 0) + (r0 + 1)).astype(F32)
    return jnp.minimum(t, w.astype(F32))


def _pool_fwd(z, pool_w, pool_scale, name):
    T = z.shape[0]
    PG = pool_w.shape[-1]
    DP = N_GROUPS * PG
    rc = min(CHUNK, T)

    def body(u_ref, pw_ref, sc_ref, pooled_ref, y_ref, pad):
        w, wts = _window_weights(pl.program_id(0))
        pad[pl.ds(0, MAX_WINDOW), :] = jnp.zeros((MAX_WINDOW, PG), F32)
        pad[pl.ds(MAX_WINDOW, T), :] = u_ref[...]
        for r0 in range(0, T, rc):
            acc = jnp.zeros((rc, PG), F32)
            for j in range(MAX_WINDOW):
                acc = acc + wts[j] * pad[pl.ds(MAX_WINDOW + r0 - j, rc), :]
            pooled = acc / _valid_count(r0, rc, w) - u_ref[pl.ds(r0, rc), :]
            pooled_ref[pl.ds(r0, rc), :] = pooled.astype(BF16)
        mixed = jnp.dot(pooled_ref[...], pw_ref[...].astype(BF16), preferred_element_type=F32)
        y_ref[...] = (mixed * sc_ref[...]).astype(BF16)

    col = lambda g: (0, g)
    return _call(
        body, name=name,grid=(N_GROUPS,),
        in_specs=[pl.BlockSpec((T, PG), col), pl.BlockSpec((None, PG, PG), lambda g: (g, 0, 0)), pl.BlockSpec((1, PG), col)],
        out_specs=[pl.BlockSpec((T, PG), col), pl.BlockSpec((T, PG), col)],
        out_shape=[jax.ShapeDtypeStruct((T, DP), BF16), jax.ShapeDtypeStruct((T, DP), BF16)],
        scratch_shapes=[pltpu.VMEM((T + MAX_WINDOW, PG), F32)], sem=("parallel",))(z, pool_w, pool_scale)


def _pool_bwd(dy, pooled, pool_w, pool_scale, name):
    T = dy.shape[0]
    PG = pool_w.shape[-1]
    DP = N_GROUPS * PG
    rc = min(CHUNK, T)

    def body(dy_ref, pooled_ref, pw_ref, sc_ref, du_ref, dpw_ref, dsc_ref, pad, dp_ref):
        w, wts = _window_weights(pl.program_id(0))
        pooled_v, pw = pooled_ref[...], pw_ref[...].astype(BF16)
        dy_v = dy_ref[...]
        mixed = jnp.dot(pooled_v, pw, preferred_element_type=F32)
        dsc_ref[...] = jnp.sum(dy_v * mixed, axis=0, keepdims=True)
        dmixed = (dy_v * sc_ref[...]).astype(BF16)
        dpw_ref[...] = lax.dot_general(pooled_v, dmixed, _DIMS["tn"], preferred_element_type=F32)
        dp_ref[...] = lax.dot_general(dmixed, pw, _DIMS["nt"], preferred_element_type=F32)
        pad[pl.ds(T, MAX_WINDOW), :] = jnp.zeros((MAX_WINDOW, PG), F32)
        for r0 in range(0, T, rc):
            pad[pl.ds(r0, rc), :] = dp_ref[pl.ds(r0, rc), :] / _valid_count(r0, rc, w)
        for r0 in range(0, T, rc):
            acc = jnp.zeros((rc, PG), F32)
            for j in range(MAX_WINDOW):
                acc = acc + wts[j] * pad[pl.ds(r0 + j, rc), :]
            du_ref[pl.ds(r0, rc), :] = (acc - dp_ref[pl.ds(r0, rc), :]).astype(BF16)

    col = lambda g: (0, g)
    return _call(
        body, name=name,grid=(N_GROUPS,),
        in_specs=[pl.BlockSpec((T, PG), col), pl.BlockSpec((T, PG), col), pl.BlockSpec((None, PG, PG), lambda g: (g, 0, 0)),
                  pl.BlockSpec((1, PG), col)],
        out_specs=[pl.BlockSpec((T, PG), col), pl.BlockSpec((None, PG, PG), lambda g: (g, 0, 0)), pl.BlockSpec((1, PG), col)],
        out_shape=[jax.ShapeDtypeStruct((T, DP), BF16), jax.ShapeDtypeStruct((N_GROUPS, PG, PG), F32),
                   jax.ShapeDtypeStruct((1, DP), F32)],
        scratch_shapes=[pltpu.VMEM((T + MAX_WINDOW, PG), F32), pltpu.VMEM((T, PG), F32)],
        sem=("parallel",))(dy, pooled, pool_w, pool_scale)


def _conv_fwd(z, conv_w, conv_b, d_pool, name, tc=128):
    T = z.shape[0]
    DC = conv_w.shape[-1]
    rc = min(CHUNK, T)
    v0, g0 = d_pool // tc, (d_pool + DC) // tc

    def body(v_ref, gt_ref, w_ref, b_ref, a_ref, c_ref, pad):
        pad[pl.ds(0, CONV_PAD), :] = jnp.zeros((CONV_PAD, tc), F32)
        for r0 in range(0, T, rc):
            a = v_ref[pl.ds(r0, rc), :] * _sigmoid(gt_ref[pl.ds(r0, rc), :])
            a_ref[pl.ds(r0, rc), :] = a
            pad[pl.ds(CONV_PAD + r0, rc), :] = a
        for r0 in range(0, T, rc):
            acc = jnp.zeros((rc, tc), F32) + b_ref[...]
            for k in range(CONV_K):
                acc = acc + w_ref[pl.ds(k, 1), :] * pad[pl.ds(CONV_PAD - (CONV_K - 1) + k + r0, rc), :]
            c_ref[pl.ds(r0, rc), :] = acc

    col = lambda j: (0, j)
    return _call(
        body, name=name,grid=(DC // tc,),
        in_specs=[pl.BlockSpec((T, tc), lambda j: (0, v0 + j)), pl.BlockSpec((T, tc), lambda j: (0, g0 + j)),
                  pl.BlockSpec((CONV_K, tc), col), pl.BlockSpec((1, tc), col)],
        out_specs=[pl.BlockSpec((T, tc), col), pl.BlockSpec((T, tc), col)],
        out_shape=[jax.ShapeDtypeStruct((T, DC), F32), jax.ShapeDtypeStruct((T, DC), F32)],
        scratch_shapes=[pltpu.VMEM((T + CONV_PAD, tc), F32)], sem=("parallel",))(z, z, conv_w, conv_b)


def _conv_bwd(dc, a, z, conv_w, d_pool, name, tc=128):
    T, DC = dc.shape
    rc = min(CHUNK, T)
    v0, g0 = d_pool // tc, (d_pool + DC) // tc

    def body(dc_ref, a_ref, v_ref, gt_ref, w_ref, dv_ref, dg_ref, dw_ref, db_ref, apad, dpad):
        apad[pl.ds(0, CONV_PAD), :] = jnp.zeros((CONV_PAD, tc), F32)
        apad[pl.ds(CONV_PAD, T), :] = a_ref[...]
        dpad[pl.ds(0, T), :] = dc_ref[...]
        dpad[pl.ds(T, CONV_PAD), :] = jnp.zeros((CONV_PAD, tc), F32)
        db_ref[...] = jnp.sum(dc_ref[...], axis=0, keepdims=True)
        for k in range(CONV_K):
            acc = jnp.zeros((8, tc), F32)
            for r0 in range(0, T, rc):
                prod = dc_ref[pl.ds(r0, rc), :] * apad[pl.ds(CONV_PAD - (CONV_K - 1) + k + r0, rc), :]
                acc = acc + jnp.sum(prod.reshape(rc // 8, 8, tc), axis=0)
            dw_ref[pl.ds(k, 1), :] = jnp.sum(acc, axis=0, keepdims=True)
        for r0 in range(0, T, rc):
            da = jnp.zeros((rc, tc), F32)
            for k in range(CONV_K):
                da = da + w_ref[pl.ds(k, 1), :] * dpad[pl.ds(r0 + (CONV_K - 1) - k, rc), :]
            sig = _sigmoid(gt_ref[pl.ds(r0, rc), :])
            dv_ref[pl.ds(r0, rc), :] = (da * sig).astype(BF16)
            dg_ref[pl.ds(r0, rc), :] = (da * v_ref[pl.ds(r0, rc), :] * sig * (1.0 - sig)).astype(BF16)

    col = lambda j: (0, j)
    return _call(
        body, name=name,grid=(DC // tc,),
        in_specs=[pl.BlockSpec((T, tc), col), pl.BlockSpec((T, tc), col), pl.BlockSpec((T, tc), lambda j: (0, v0 + j)),
                  pl.BlockSpec((T, tc), lambda j: (0, g0 + j)), pl.BlockSpec((CONV_K, tc), col)],
        out_specs=[pl.BlockSpec((T, tc), col), pl.BlockSpec((T, tc), col), pl.BlockSpec((CONV_K, tc), col),
                   pl.BlockSpec((1, tc), col)],
        out_shape=[jax.ShapeDtypeStruct((T, DC), BF16), jax.ShapeDtypeStruct((T, DC), BF16),
                   jax.ShapeDtypeStruct((CONV_K, DC), F32), jax.ShapeDtypeStruct((1, DC), F32)],
        scratch_shapes=[pltpu.VMEM((T + CONV_PAD, tc), F32), pltpu.VMEM((T + CONV_PAD, tc), F32)],
        sem=("parallel",))(dc, a, z, z, conv_w)


def _layer_norm_parts(c, g, b):
    mu = jnp.mean(c, axis=-1, keepdims=True)
    xc = c - mu
    rstd = lax.rsqrt(jnp.mean(xc * xc, axis=-1, keepdims=True) + EPS)
    xhat = xc * rstd
    return xhat, rstd, xhat * g + b


def _ln_silu_fwd(c, g, b, name):
    T, DC = c.shape

    def body(c_ref, g_ref, b_ref, y_ref):
        _, _, ln = _layer_norm_parts(c_ref[...], g_ref[...], b_ref[...])
        y_ref[...] = (ln * _sigmoid(ln)).astype(BF16)

    return _call(body,name=name,grid=(T // ROW_TILE,), in_specs=[_rows(DC), _vec(DC), _vec(DC)], out_specs=_rows(DC),
                          out_shape=jax.ShapeDtypeStruct((T, DC), BF16), sem=("parallel",))(c, g, b)


def _ln_silu_bwd(dy, c, g, b, name):
    T, DC = c.shape

    def body(dy_ref, c_ref, g_ref, b_ref, dc_ref, dg_ref, db_ref):
        gain = g_ref[...]
        xhat, rstd, ln = _layer_norm_parts(c_ref[...], gain, b_ref[...])
        s = _sigmoid(ln)
        dln = dy_ref[...] * (s * (1.0 + ln * (1.0 - s)))
        _accumulate(dg_ref, jnp.sum(dln * xhat, axis=0, keepdims=True))
        _accumulate(db_ref, jnp.sum(dln, axis=0, keepdims=True))
        dxh = dln * gain
        dc_ref[...] = rstd * (dxh - jnp.mean(dxh, axis=-1, keepdims=True) - xhat * jnp.mean(dxh * xhat, axis=-1, keepdims=True))

    return _call(body,name=name,grid=(T // ROW_TILE,),
                          in_specs=[pl.BlockSpec((ROW_TILE, DC), lambda i: (i, 1)), _rows(DC), _vec(DC), _vec(DC)],
                          out_specs=[_rows(DC), _vec(DC), _vec(DC)],
                          out_shape=[jax.ShapeDtypeStruct((T, DC), F32), jax.ShapeDtypeStruct((1, DC), F32),
                                     jax.ShapeDtypeStruct((1, DC), F32)],
                          sem=("arbitrary",))(dy, c, g, b)


def _short_specs(T, DS, tc):
    n = DS // tc
    return [pl.BlockSpec((T, tc), lambda j: (0, j)), pl.BlockSpec((T, tc), lambda j: (0, n + j)),
            pl.BlockSpec((T, tc), lambda j: (0, 2 * n + j))]


def _short_fwd(z, w, name, tc=256):
    T = z.shape[0]
    DS = w.shape[-1]
    rc = min(CHUNK, T)

    def body(b_ref, cg_ref, u_ref, w_ref, y_ref, pad):
        pad[pl.ds(0, SHORT_PAD), :] = jnp.zeros((SHORT_PAD, tc), F32)
        pad[pl.ds(SHORT_PAD, T), :] = cg_ref[...] * u_ref[...]
        for r0 in range(0, T, rc):
            r = jnp.zeros((rc, tc), F32)
            for k in range(SHORT_K):
                r = r + w_ref[pl.ds(k, 1), :] * pad[pl.ds(SHORT_PAD - (SHORT_K - 1) + k + r0, rc), :]
            y_ref[pl.ds(r0, rc), :] = (b_ref[pl.ds(r0, rc), :] * r).astype(BF16)

    col = lambda j: (0, j)
    return _call(body,name=name,grid=(DS // tc,), in_specs=_short_specs(T, DS, tc) + [pl.BlockSpec((SHORT_K, tc), col)],
                          out_specs=pl.BlockSpec((T, tc), col), out_shape=jax.ShapeDtypeStruct((T, DS), BF16),
                          scratch_shapes=[pltpu.VMEM((T + SHORT_PAD, tc), F32)], sem=("parallel",))(z, z, z, w)


def _short_bwd(dy, z, w, name, tc=256):
    T, DS = dy.shape
    rc = min(CHUNK, T)

    def body(dy_ref, b_ref, cg_ref, u_ref, w_ref, db_ref, dcg_ref, du_ref, dw_ref, qpad, rpad):
        qpad[pl.ds(0, SHORT_PAD), :] = jnp.zeros((SHORT_PAD, tc), F32)
        qpad[pl.ds(SHORT_PAD, T), :] = cg_ref[...] * u_ref[...]
        rpad[pl.ds(0, T), :] = dy_ref[...] * b_ref[...]
        rpad[pl.ds(T, SHORT_PAD), :] = jnp.zeros((SHORT_PAD, tc), F32)
        accs = [jnp.zeros((8, tc), F32) for _ in range(SHORT_K)]
        for r0 in range(0, T, rc):
            r = jnp.zeros((rc, tc), F32)
            dq = jnp.zeros((rc, tc), F32)
            dr = rpad[pl.ds(r0, rc), :]
            for k in range(SHORT_K):
                q_k = qpad[pl.ds(SHORT_PAD - (SHORT_K - 1) + k + r0, rc), :]
                r = r + w_ref[pl.ds(k, 1), :] * q_k
                dq = dq + w_ref[pl.ds(k, 1), :] * rpad[pl.ds(r0 + (SHORT_K - 1) - k, rc), :]
                accs[k] = accs[k] + jnp.sum((dr * q_k).reshape(rc // 8, 8, tc), axis=0)
            db_ref[pl.ds(r0, rc), :] = (dy_ref[pl.ds(r0, rc), :] * r).astype(BF16)
            dcg_ref[pl.ds(r0, rc), :] = (dq * u_ref[pl.ds(r0, rc), :]).astype(BF16)
            du_ref[pl.ds(r0, rc), :] = (dq * cg_ref[pl.ds(r0, rc), :]).astype(BF16)
        for k in range(SHORT_K):
            dw_ref[pl.ds(k, 1), :] = jnp.sum(accs[k], axis=0, keepdims=True)

    col = lambda j: (0, j)
    tile = pl.BlockSpec((T, tc), col)
    return _call(body,name=name,grid=(DS // tc,),
                          in_specs=[tile] + _short_specs(T, DS, tc) + [pl.BlockSpec((SHORT_K, tc), col)],
                          out_specs=[tile, tile, tile, pl.BlockSpec((SHORT_K, tc), col)],
                          out_shape=[jax.ShapeDtypeStruct((T, DS), BF16)] * 3 + [jax.ShapeDtypeStruct((SHORT_K, DS), F32)],
                          scratch_shapes=[pltpu.VMEM((T + SHORT_PAD, tc), F32), pltpu.VMEM((T + SHORT_PAD, tc), F32)],
                          sem=("parallel",))(dy, z, z, z, w)


def _tile_rows(rows, cols, n_bufs):
    budget = VMEM_LIMIT_BYTES * 3 // 4 // (2 * n_bufs * 4 * cols)
    tr = rows
    while tr > budget and tr % 16 == 0:
        tr //= 2
    return tr


def _placed_call(body, name, place, grid, in_specs, out_specs, out_shape, ins):
    return _call(body,prefetch=place, name=name, grid=grid, in_specs=in_specs, out_specs=out_specs, out_shape=out_shape,
                 sem=("parallel",))(*ins)


def _cast_into_full(w, layer, kind, place, name):
    _, R, C = w.shape
    tr = _tile_rows(R, C, 2)
    nb = R // tr
    if kind == "col":
        full, out_spec = (R, C * N_CHIPS), pl.BlockSpec((tr, C), lambda i, s: (i, s[0]))
    else:
        full, out_spec = (R * N_CHIPS, C), pl.BlockSpec((tr, C), lambda i, s: (s[0] * nb + i, 0))

    def body(s_ref, w_ref, o_ref):
        o_ref[...] = w_ref[...].astype(BF16)

    return _placed_call(body, name, place, (nb,), [pl.BlockSpec((None, tr, C), lambda i, s: (layer, i, 0))], out_spec,
                        jax.ShapeDtypeStruct(full, BF16), [w])


def _add_pair(grad, theirs, kind, place, name):
    R, C = grad.shape
    piece_rows = R // 2 if kind == "col" else R // N_CHIPS // 2
    tr = _tile_rows(piece_rows, C, 3)
    nb = piece_rows // tr
    if kind == "col":
        g_spec = pl.BlockSpec((tr, C), lambda i, s: (s[1] * nb + i, 0))
    else:
        g_spec = pl.BlockSpec((tr, C), lambda i, s: ((2 * (i // nb) + s[1]) * nb + i % nb, 0))
    flat = pl.BlockSpec((tr, C), lambda i, s: (i, 0))

    def body(s_ref, a_ref, b_ref, o_ref):
        o_ref[...] = (a_ref[...].astype(F32) + b_ref[...].astype(F32)).astype(BF16)

    return _placed_call(body, name, place, (R // 2 // tr,), [g_spec, flat], flat, jax.ShapeDtypeStruct((R // 2, C), BF16),
                        [grad, theirs])


def _sum_chips(chip_sum, arrived, kind, place, name):
    _, H, W = arrived.shape
    tr = _tile_rows(H, W, 6)
    nb = H // tr
    if kind == "col":
        own_spec = pl.BlockSpec((tr, W), lambda i, s: (i, s[0]))
    else:
        own_spec = pl.BlockSpec((tr, W), lambda i, s: (s[0] * nb + i, 0))

    def body(s_ref, p_ref, r_ref, o_ref):
        acc = p_ref[...].astype(F32)
        for i in range(N_CHIPS - 1):
            acc = acc + r_ref[i].astype(F32)
        o_ref[...] = acc

    return _placed_call(body, name, place, (nb,), [own_spec, pl.BlockSpec((N_CHIPS - 1, tr, W), lambda i, s: (0, i, 0))],
                        pl.BlockSpec((tr, W), lambda i, s: (s[1] * nb + i, 0)), jax.ShapeDtypeStruct((2 * H, W), F32),
                        [chip_sum, arrived])


def _adamw_values(w, g, m, v):
    m = ADAM_B1 * m + (1.0 - ADAM_B1) * g
    v = ADAM_B2 * v + (1.0 - ADAM_B2) * (g * g)
    m_hat = m / (1.0 - ADAM_B1 ** ADAM_STEP)
    v_hat = v / (1.0 - ADAM_B2 ** ADAM_STEP)
    return -ADAM_LR * (m_hat / (jnp.sqrt(v_hat) + ADAM_EPS) + ADAM_WD * w), m, v


def _adamw(w, g, m, v, name, layer=0, carried=None):
    L, R, C = w.shape
    tr = _tile_rows(R, C, 8)

    def body(w_ref, g_ref, m_ref, v_ref, *rest):
        go_ref, d_ref, mo_ref, vo_ref = rest[-4:]
        g_val = g_ref[...]
        d, m_new, v_new = _adamw_values(w_ref[...], g_val, m_ref[...], v_ref[...])
        go_ref[...], d_ref[...], mo_ref[...], vo_ref[...] = g_val, d, m_new, v_new

    lay = pl.BlockSpec((None, tr, C), lambda i: (layer, i, 0))
    ins = [w, g, m, v]
    in_specs = [lay, pl.BlockSpec((tr, C), lambda i: (i, 0)), lay, lay]
    aliases = {}
    if carried is not None:
        ins += list(carried)
        in_specs += [pl.BlockSpec(memory_space=pl.ANY)] * 4
        aliases = {4 + i: i for i in range(4)}
    return _call(body, name=name, grid=(R // tr,), in_specs=in_specs, out_specs=[lay] * 4,
                 out_shape=[jax.ShapeDtypeStruct((L, R, C), F32)] * 4, input_output_aliases=aliases, sem=("parallel",))(*ins)


def _aligned(v, m):
    return v if isinstance(v, int) else pl.multiple_of(v, m)


def _place():
    x, y, c = lax.axis_index("x"), lax.axis_index("y"), lax.axis_index("c")
    other_chips = [(x, 1 - y), (1 - x, y), (1 - x, 1 - y)]
    return x, y, c, 2 * x + y, other_chips


def _chip_index(chip):
    return 2 * chip[0] + chip[1]


def _piece(ref, kind, k, h):
    R, C = ref.shape
    if kind == "col":
        return ref.at[pl.ds(_aligned(h * (R // 2), 16), R // 2), pl.ds(_aligned(k * (C // N_CHIPS), 128), C // N_CHIPS)]
    rs = R // N_CHIPS
    return ref.at[pl.ds(_aligned(k * rs + h * (rs // 2), 16), rs // 2), :]


def _compact_piece(ref, kind, k):
    R2, C = ref.shape
    if kind == "col":
        return ref.at[:, pl.ds(_aligned(k * (C // N_CHIPS), 128), C // N_CHIPS)]
    return ref.at[pl.ds(_aligned(k * (R2 // N_CHIPS), 16), R2 // N_CHIPS), :]


def _half_rows(ref, h):
    R = ref.shape[0]
    return ref.at[pl.ds(_aligned(h * (R // 2), 16), R // 2), :]


class _Copies:
    def __init__(self, send_sems, recv_sems):
        self.send_sems, self.recv_sems = send_sems, recv_sems
        self.n_remote = 0

    def remote(self, src, dst, device):
        k = self.n_remote
        self.n_remote += 1
        return pltpu.make_async_remote_copy(src_ref=src, dst_ref=dst, send_sem=self.send_sems.at[k], recv_sem=self.recv_sems.at[k],
                                            device_id=device, device_id_type=MESH)


class _Job:
    def __init__(self, ins, out_shape, aliases, n_remote, build):
        self.ins, self.out_shape, self.aliases, self.n_remote, self.build = list(ins), list(out_shape), dict(aliases), n_remote, build


class _Flying:
    def __init__(self, job, send_sems, recv_sems, bufs, token):
        self.job, self.send_sems, self.recv_sems, self.bufs, self.token = job, send_sems, recv_sems, bufs, token


def _job_refs(job, buf_refs):
    n_out = len(job.out_shape)
    kept = [i for i in range(len(job.ins)) if i not in job.aliases]
    ins = [buf_refs[job.aliases[i]] if i in job.aliases else buf_refs[n_out + kept.index(i)] for i in range(len(job.ins))]
    return ins, list(buf_refs[:n_out])


def _start_jobs(jobs, name, after=()):
    n_after = len(after)
    ins, out_shape, aliases, layout = [], [], {}, []
    for job in jobs:
        kept = [i for i in range(len(job.ins)) if i not in job.aliases]
        layout.append((len(ins), len(out_shape), len(kept)))
        aliases.update({len(ins) + i: len(out_shape) + o for i, o in job.aliases.items()})
        aliases.update({len(ins) + i: len(out_shape) + len(job.out_shape) + k for k, i in enumerate(kept)})
        out_shape += job.out_shape + [jax.ShapeDtypeStruct(job.ins[i].shape, job.ins[i].dtype) for i in kept]
        ins += job.ins
    n_in, n_bufs = len(ins), len(out_shape)

    def body(*refs):
        in_refs, buf_refs = refs[:n_in], refs[n_in + n_after:n_in + n_after + n_bufs]
        sem_refs, token = refs[n_in + n_after + n_bufs:-1], refs[-1]
        for j, (job, (i0, b0, _)) in enumerate(zip(jobs, layout)):
            out_refs = buf_refs[b0:b0 + len(job.out_shape)]
            for d in job.build(in_refs[i0:i0 + len(job.ins)], out_refs, _Copies(sem_refs[2 * j], sem_refs[2 * j + 1])):
                d.start()
        token[...] = jnp.zeros_like(token)

    sems = [pltpu.SemaphoreType.DMA((job.n_remote,)) for job in jobs for _ in range(2)]
    outs = pl.pallas_call(
        body, name=name, in_specs=[HBM] * n_in + [ANY] * n_after,
        out_specs=[HBM] * n_bufs + [SEM] * len(sems) + [pl.BlockSpec(memory_space=pltpu.VMEM)],
        out_shape=out_shape + sems + [jax.ShapeDtypeStruct((8, 128), F32)], input_output_aliases=aliases,
        compiler_params=pltpu.CompilerParams(has_side_effects=pltpu.SideEffectType.DATAFLOW_SIDE_EFFECTING))(*ins, *after)
    return [_Flying(job, outs[n_bufs + 2 * j], outs[n_bufs + 2 * j + 1], list(outs[b0:b0 + len(job.out_shape) + n_kept]), outs[-1])
            for j, (job, (_, b0, n_kept)) in enumerate(zip(jobs, layout))]


def _wait_job(flying, name, after=()):
    job, n_bufs, n_after = flying.job, len(flying.bufs), len(after)

    def body(*refs):
        in_refs, out_refs = _job_refs(job, refs[:n_bufs])
        send_sems, recv_sems = refs[n_bufs:n_bufs + 2]
        copies = job.build(in_refs, out_refs, _Copies(send_sems, recv_sems))
        for d in copies:
            d.wait_send()
        for d in copies:
            d.wait_recv()

    outs = pl.pallas_call(
        body, name=name, in_specs=[HBM] * n_bufs + [SEM, SEM] + [ANY] * n_after, out_specs=[HBM] * n_bufs,
        out_shape=[jax.ShapeDtypeStruct(b.shape, b.dtype) for b in flying.bufs],
        input_output_aliases={i: i for i in range(n_bufs)},
        compiler_params=pltpu.CompilerParams(has_side_effects=pltpu.SideEffectType.DATAFLOW_SIDE_EFFECTING))(
            *flying.bufs, flying.send_sems, flying.recv_sems, *after)
    return list(outs[:len(job.out_shape)]), list(outs[len(job.out_shape):])


def _in_place(arrays):
    return [jax.ShapeDtypeStruct(a.shape, a.dtype) for a in arrays], {u: u for u in range(len(arrays))}


def _rows_part(ref, part, n_parts):
    h = ref.shape[0] // n_parts
    return ref.at[pl.ds(part * h, h), :]


def _gather_job(full, kind, stage):
    def build(in_refs, out_refs, cp):
        x, y, c, me, (y_nbr, x_nbr, diagonal) = _place()
        (ref,) = out_refs
        sibling = (x, y, 1 - c)
        if stage == 1:
            mine = _piece(ref, kind, me, c)
            return [cp.remote(mine, mine, (*y_nbr, c)), cp.remote(mine, mine, (*x_nbr, c))]
        from_y, from_x = _piece(ref, kind, _chip_index(y_nbr), c), _piece(ref, kind, _chip_index(x_nbr), c)
        if stage == 2:
            relay_0, relay_1 = _rows_part(from_x, 0, 2), _rows_part(from_y, 1, 2)
            return [cp.remote(relay_0, relay_0, (*y_nbr, c)), cp.remote(relay_1, relay_1, (*x_nbr, c)),
                    cp.remote(from_y, from_y, sibling), cp.remote(from_x, from_x, sibling)]
        from_diagonal = _piece(ref, kind, _chip_index(diagonal), c)
        return [cp.remote(from_diagonal, from_diagonal, sibling)]

    return _Job([full], *_in_place([full]), {1: 2, 2: 4, 3: 1}[stage], build)


def _gather_small_job(fulls, axes):
    def build(in_refs, out_refs, cp):
        x, y, c, me, chips = _place()
        copies = []
        for ref, ax in zip(out_refs, axes):
            n = ref.shape[ax] // N_CHIPS
            idx = [slice(None)] * len(ref.shape)
            idx[ax] = pl.ds(_aligned(me * n, n), n)
            mine = ref.at[tuple(idx)]
            copies += [cp.remote(mine, mine, (*chip, c)) for chip in chips]
        return copies

    return _Job(fulls, *_in_place(fulls), 3 * len(fulls), build)


def _exchange_halves_job(grads, kinds):
    def build(in_refs, out_refs, cp):
        x, y, c, me, chips = _place()
        copies = []
        for src, dst, kind in zip(in_refs, out_refs, kinds):
            if kind == "col":
                copies.append(cp.remote(_half_rows(src, 1 - c), dst, (x, y, 1 - c)))
            else:
                copies += [cp.remote(_piece(src, "row", k, 1 - c), _compact_piece(dst, "row", k), (x, y, 1 - c))
                           for k in range(N_CHIPS)]
        return copies

    out_shape = [jax.ShapeDtypeStruct((g.shape[0] // 2, g.shape[1]), g.dtype) for g in grads]
    return _Job(grads, out_shape, {}, sum(1 if k == "col" else N_CHIPS for k in kinds), build)


def _scatter_job(half, kind, n_parts, parts, into=None):
    def build(in_refs, out_refs, cp):
        x, y, c, me, chips = _place()
        src, (dst,) = in_refs[0], out_refs
        copies = []
        for p in parts:
            copies += [cp.remote(_rows_part(_compact_piece(src, kind, _chip_index(chip)), p, n_parts),
                                 _rows_part(dst.at[r], p, n_parts), (*chip, c)) for r, chip in enumerate(chips)]
        return copies

    part_shape = (half.shape[0], half.shape[1] // N_CHIPS) if kind == "col" else (half.shape[0] // N_CHIPS, half.shape[1])
    out_shape = [jax.ShapeDtypeStruct((N_CHIPS - 1,) + part_shape, half.dtype)]
    ins, aliases = ([half], {}) if into is None else ([half, into], {1: 0})
    return _Job(ins, out_shape, aliases, 3 * len(parts), build)


def _share_job(shards):
    def build(in_refs, out_refs, cp):
        x, y, c, me, chips = _place()
        copies = []
        for ref in out_refs:
            mine = _half_rows(ref, c)
            copies.append(cp.remote(mine, mine, (x, y, 1 - c)))
        return copies

    return _Job(shards, *_in_place(shards), len(shards), build)


N_DEVICES = 2 * N_CHIPS


def _small_exchange_job(slots, n_parts, parts):
    def build(in_refs, out_refs, cp):
        x, y, c, me, chips = _place()
        (ref,) = out_refs
        copies = []
        for part in parts:
            mine = _rows_part(ref.at[2 * me + c], part, n_parts)
            copies += [cp.remote(mine, mine, (x ^ (p >> 2), y ^ ((p >> 1) & 1), c ^ (p & 1))) for p in range(1, N_DEVICES)]
        return copies

    return _Job([slots], *_in_place([slots]), (N_DEVICES - 1) * len(parts), build)


def _in_own_slot(packed, place, name):
    R, C = packed.shape

    def body(s_ref, p_ref, o_ref):
        o_ref[...] = p_ref[...]

    return _placed_call(body, name, place, (1,), [pl.BlockSpec((R, C), lambda i, s: (0, 0))],
                        pl.BlockSpec((None, R, C), lambda i, s: (2 * s[0] + s[1], 0, 0)),
                        jax.ShapeDtypeStruct((N_DEVICES, R, C), F32), [packed])


def _sum_slots(slots, name):
    n, R, C = slots.shape

    def body(s_ref, o_ref):
        acc = s_ref[0]
        for i in range(1, n):
            acc = acc + s_ref[i]
        o_ref[...] = acc

    return _call(body,name=name, grid=(1,), in_specs=[pl.BlockSpec((n, R, C), lambda i: (0, 0, 0))],
                 out_specs=pl.BlockSpec((R, C), lambda i: (0, 0)), out_shape=jax.ShapeDtypeStruct((R, C), F32),
                 sem=("arbitrary",))(slots)


def _packed_rows(size, width):
    return -(-size // (8 * width)) * 8


def _pack(arrays, width):
    rows = []
    for a in arrays:
        flat = a.reshape(-1)
        n_rows = _packed_rows(flat.shape[0], width)
        rows.append(jnp.pad(flat, (0, n_rows * width - flat.shape[0])).reshape(n_rows, width))
    return jnp.concatenate(rows, axis=0)


def _unpack(packed, shapes):
    out, r0, width = [], 0, packed.shape[1]
    for shape in shapes:
        size = 1
        for d in shape:
            size *= d
        out.append(packed[r0:r0 + _packed_rows(size, width)].reshape(-1)[:size].reshape(shape))
        r0 += _packed_rows(size, width)
    return out


class _Backlog:
    def __init__(self, first):
        self.now, self.free, self.flights, self.last, self.chain = 0.0, {"ici": 0.0, "d2d": 0.0}, [], first, []
        self.asked, self.n_starts = [], 0

    def run(self, fn, us, *args, **kw):
        self.flush()
        out = fn(*args, **kw)
        self.now += us
        self.last = out[0] if isinstance(out, (list, tuple)) else out
        self.poll()
        return out

    def start(self, job, name, link, cost, done):
        self.asked.append((job, name, link, cost, done))

    def flush(self):
        if self.asked:
            asked, self.asked = self.asked, []
            flights = _start_jobs([a[0] for a in asked], "start_%d" % self.n_starts, self.chain)
            self.n_starts += 1
            self.chain = [flights[0].token]
            _Behind.pending.append(flights[0].token)
            for (job, name, link, cost, done), flying in zip(asked, flights):
                ends = max(self.now, self.free[link]) + cost
                self.free[link] = ends
                self.flights.append((ends + LANDING_SLACK_US, name, flying, done))
            self.flights.sort(key=lambda f: f[0])

    def poll(self, block=False):
        self.flush()
        while self.flights and (block or self.flights[0][0] <= self.now):
            ends, name, flying, done = self.flights.pop(0)
            self.now, block = max(self.now, ends), False
            done(*_wait_job(flying, name + "_wait", [self.last] + self.chain))
        self.flush()


class _GatherStream:
    def __init__(self, backlog, bufs, kinds, costs):
        self.backlog, self.bufs, self.kinds, self.costs, self.begun, self.complete = backlog, bufs, kinds, costs, 0, set()
        self.begin()

    def begin(self):
        u, self.begun = self.begun, self.begun + 1
        self.backlog.start(_gather_job(self.bufs[u], self.kinds[u], 1), "gather_%d" % u, "ici", 0.5 * self.costs[u],
                           lambda outs, kept: self.arrived(u, outs[0]))

    def arrived(self, u, buf):
        self.bufs[u] = buf
        self.backlog.start(_gather_job(buf, self.kinds[u], 2), "relay_%d" % u, "ici", 0.25 * self.costs[u],
                           lambda outs, kept: self.relayed(u, outs[0]))
        while self.begun <= min(u + (1 if u < STARVED_WEIGHTS else 2), len(self.bufs) - 1):
            self.begin()

    def relayed(self, u, buf):
        self.bufs[u] = buf
        self.backlog.start(_gather_job(buf, self.kinds[u], 3), "handon_%d" % u, "d2d", D2D_SHARE * self.costs[u],
                           lambda outs, kept: self.handed(u, outs[0]))

    def handed(self, u, buf):
        self.bufs[u] = buf
        self.complete.add(u)

    def ready(self, u):
        while u not in self.complete:
            assert self.backlog.flights or self.backlog.asked, "weight %d is not on its way" % u
            self.backlog.poll(block=True)
        return self.bufs[u]


class _GradStream:
    def __init__(self, backlog, u, name, kind, cost, g, place, results):
        self.backlog, self.u, self.name, self.kind, self.cost, self.place, self.results = backlog, u, name, kind, cost, place, results
        backlog.start(_exchange_halves_job([g], [kind]), "to_sibling_" + name, "d2d", D2D_SHARE * cost, self.exchanged)

    def exchanged(self, outs, kept):
        chip_sum = self.backlog.run(_add_pair, SIDE_KERNEL_US, kept[0], outs[0], self.kind, self.place, "chip_sum_" + self.name)
        self.backlog.start(_scatter_job(chip_sum, self.kind, 1, [0]), "to_owners_" + self.name, "ici", self.cost, self.scattered)

    def scattered(self, outs, kept):
        reduced = self.backlog.run(_sum_chips, SIDE_KERNEL_US, kept[0], outs[0], self.kind, self.place, "reduce_" + self.name)
        self.backlog.start(_share_job([reduced]), "share_" + self.name, "d2d", D2D_SHARE * self.cost, self.shared)

    def shared(self, outs, kept):
        self.results[self.u] = outs[0]


SIDE_KERNEL_US = 12.0
D2D_SHARE = 0.15
LANDING_SLACK_US = 5.0
STARVED_WEIGHTS = 8


def kernel(x, mix_pre_g, mix_post_g, ffn_pre_g, ffn_post_g, ab_w_in, pool_w, pool_scale, conv_w, conv_b, conv_ln_g, conv_ln_b, ab_w_out, sc_w_in, sc_conv_w, sc_w_out, ffn_w1, ffn_w2, loss_target, m_mix_pre_g, m_mix_post_g, m_ffn_pre_g, m_ffn_post_g, m_ab_w_in, m_pool_w, m_pool_scale, m_conv_w, m_conv_b, m_conv_ln_g, m_conv_ln_b, m_ab_w_out, m_sc_w_in, m_sc_conv_w, m_sc_w_out, m_ffn_w1, m_ffn_w2, v_mix_pre_g, v_mix_post_g, v_ffn_pre_g, v_ffn_post_g, v_ab_w_in, v_pool_w, v_pool_scale, v_conv_w, v_conv_b, v_conv_ln_g, v_conv_ln_b, v_ab_w_out, v_sc_w_in, v_sc_conv_w, v_sc_w_out, v_ffn_w1, v_ffn_w2):
    x0, target = x[0], loss_target[0]
    T, D = x0.shape
    DP = pool_scale.shape[-1]
    gain = lambda g, layer: g[layer][None, :]

    big = [("ab_w_in", ab_w_in, 0, "col", 4, 67.0), ("ab_w_out", ab_w_out, 0, "row", 2, 44.0),
           ("ffn_w1_0", ffn_w1, 0, "col", 8, 177.0), ("ffn_w2_0", ffn_w2, 0, "row", 8, 177.0),
           ("sc_w_in", sc_w_in, 0, "col", 4, 133.0), ("sc_w_out", sc_w_out, 0, "row", 2, 44.0),
           ("ffn_w1_1", ffn_w1, 1, "col", 8, 177.0), ("ffn_w2_1", ffn_w2, 1, "row", 8, 177.0)]
    kinds = [b[3] for b in big]
    chip = 2 * lax.axis_index("x") + lax.axis_index("y")
    place = jnp.stack([chip, lax.axis_index("c")]).astype(jnp.int32)

    def own_in_zeros(shard, ax):
        full = jnp.zeros(tuple(d * N_CHIPS if i == ax else d for i, d in enumerate(shard.shape)), shard.dtype)
        return lax.dynamic_update_slice_in_dim(full, shard, chip * shard.shape[ax], axis=ax)

    W = [_cast_into_full(w, layer, kind, place, "cast_" + name) for name, w, layer, kind, _, _ in big]
    smalls = [own_in_zeros(pool_w[0], 1), own_in_zeros(conv_w[0], 1), own_in_zeros(sc_conv_w[0], 1)]
    backlog = _Backlog(x0)
    run = backlog.run
    small_weights = []
    backlog.start(_gather_small_job(smalls, [1, 1, 1]), "gather_small", "ici", 6.0, lambda outs, kept: small_weights.extend(outs))
    gather = _GatherStream(backlog, W, kinds, [b[5] for b in big])

    relu_sq = lambda acc: (jnp.maximum(acc, 0.0), jnp.square(jnp.maximum(acc, 0.0)))
    relu_sq_bwd = lambda acc, a: (acc * (2.0 * a.astype(F32)),)

    cast_done = [w[:8, :128] for w in W]
    _Behind.pending += cast_done[1:4]
    h0 = run(_norm_fwd, 12.0, x0, gain(mix_pre_g, 0), "norm_in")
    w_ready = gather.ready(0)
    _Behind.pending += cast_done[4:]
    z0 = run(_matmul, 35.0, h0, w_ready, "nn", "mix0_in")
    while not small_weights:
        backlog.poll(block=True)
    pool_w_full, conv_w_full, sc_conv_w_full = small_weights
    pooled, y_pool = run(_pool_fwd, 23.0, z0, pool_w_full, pool_scale, "pool_fwd")
    a_conv, c_conv = run(_conv_fwd, 25.0, z0, conv_w_full, conv_b, DP, "conv_fwd")
    y_conv = run(_ln_silu_fwd, 10.0, c_conv, conv_ln_g, conv_ln_b, "ln_silu_fwd")
    y0 = jnp.concatenate([y_pool, y_conv], axis=1)
    m0 = run(_matmul, 25.0, y0, gather.ready(1), "nn", "mix0_out")
    x1, h1 = run(_residual_norm, 21.0, x0, m0, gain(mix_post_g, 0), gain(ffn_pre_g, 0), "res_mix0")
    a0, a0sq = run(_matmul, 81.0, h1, gather.ready(2), "nn", "ffn0_up", out_dtypes=(BF16, BF16), epilogue=relu_sq, tn=WIDE_N_TILE)
    f0 = run(_matmul, 84.0, a0sq, gather.ready(3), "nn", "ffn0_down", tk=LONG_K_TILE)
    x2, h2 = run(_residual_norm, 22.0, x1, f0, gain(ffn_post_g, 0), gain(mix_pre_g, 1), "res_ffn0")
    z1 = run(_matmul, 62.0, h2, gather.ready(4), "nn", "mix1_in")
    y1 = run(_short_fwd, 22.0, z1, sc_conv_w_full, "short_fwd")
    m1 = run(_matmul, 25.0, y1, gather.ready(5), "nn", "mix1_out")
    x3, h3 = run(_residual_norm, 21.0, x2, m1, gain(mix_post_g, 1), gain(ffn_pre_g, 1), "res_mix1")
    a1, a1sq = run(_matmul, 81.0, h3, gather.ready(6), "nn", "ffn1_up", out_dtypes=(BF16, BF16), epilogue=relu_sq, tn=WIDE_N_TILE)
    f1 = run(_matmul, 84.0, a1sq, gather.ready(7), "nn", "ffn1_down", tk=LONG_K_TILE)
    w_in0, w_out0, w1_0, w2_0, w_in1, w_out1, w1_1, w2_1 = W

    grads_big = [None] * len(big)


    def reduce_grad(u, g):
        name, _, _, kind, _, cost = big[u]
        _GradStream(backlog, u, name, kind, cost, g, place, grads_big)

    dx, df1, d_ffn_post_1, loss_row = run(_loss_and_last_norm_bwd, 30.0, x3, f1, gain(ffn_post_g, 1), target, "loss")
    reduce_grad(7, run(_matmul, 80.0, a1sq, df1, "tn", "ffn1_down_dw", out_dtypes=(BF16,), tn=WIDE_N_TILE))
    dz = run(_matmul, 82.0, df1, w2_1, "nt", "ffn1_down_dx", out_dtypes=(BF16,), epilogue=relu_sq_bwd, epi=(a1,), tn=WIDE_N_TILE)
    reduce_grad(6, run(_matmul, 80.0, h3, dz, "tn", "ffn1_up_dw", out_dtypes=(BF16,), tn=WIDE_N_TILE))
    dh = run(_matmul, 87.0, dz, w1_1, "nt", "ffn1_up_dx", tk=LONG_K_TILE)
    dx, d_ffn_pre_1, dm1, d_mix_post_1 = run(_norms_bwd, 36.0, dx, dh, x3, gain(ffn_pre_g, 1), m1, gain(mix_post_g, 1), "norms_bwd3")

    reduce_grad(5, run(_matmul, 24.0, y1, dm1, "tn", "mix1_out_dw", out_dtypes=(BF16,), tn=WIDE_N_TILE))
    dy1 = run(_matmul, 25.0, dm1, w_out1, "nt", "mix1_out_dx")
    db, dcg, du, d_sc_conv_w = run(_short_bwd, 41.0, dy1, z1, sc_conv_w_full, "short_bwd")
    dz1 = jnp.concatenate([db, dcg, du], axis=1)
    reduce_grad(4, run(_matmul, 62.0, h2, dz1, "tn", "mix1_in_dw", out_dtypes=(BF16,), tn=WIDE_N_TILE))
    dh = run(_matmul, 68.0, dz1, w_in1, "nt", "mix1_in_dx")
    dx, d_mix_pre_1, df0, d_ffn_post_0 = run(_norms_bwd, 35.0, dx, dh, x2, gain(mix_pre_g, 1), f0, gain(ffn_post_g, 0), "norms_bwd2")

    reduce_grad(3, run(_matmul, 80.0, a0sq, df0, "tn", "ffn0_down_dw", out_dtypes=(BF16,), tn=WIDE_N_TILE))
    dz = run(_matmul, 82.0, df0, w2_0, "nt", "ffn0_down_dx", out_dtypes=(BF16,), epilogue=relu_sq_bwd, epi=(a0,), tn=WIDE_N_TILE)
    reduce_grad(2, run(_matmul, 80.0, h1, dz, "tn", "ffn0_up_dw", out_dtypes=(BF16,), tn=WIDE_N_TILE))
    dh = run(_matmul, 87.0, dz, w1_0, "nt", "ffn0_up_dx", tk=LONG_K_TILE)
    dx, d_ffn_pre_0, dm0, d_mix_post_0 = run(_norms_bwd, 36.0, dx, dh, x1, gain(ffn_pre_g, 0), m0, gain(mix_post_g, 0), "norms_bwd1")

    reduce_grad(1, run(_matmul, 24.0, y0, dm0, "tn", "mix0_out_dw", out_dtypes=(BF16,), tn=WIDE_N_TILE))
    dy0 = run(_matmul, 25.0, dm0, w_out0, "nt", "mix0_out_dx")
    du_pool, d_pool_w, d_pool_scale = run(_pool_bwd, 28.0, dy0, pooled, pool_w_full, pool_scale, "pool_bwd")
    dc, d_ln_g, d_ln_b = run(_ln_silu_bwd, 15.0, dy0, c_conv, conv_ln_g, conv_ln_b, "ln_silu_bwd")
    dv, dgate, d_conv_w, d_conv_b = run(_conv_bwd, 52.0, dc, a_conv, z0, conv_w_full, DP, "conv_bwd")
    dz0 = jnp.concatenate([du_pool, dv, dgate], axis=1)

    small_sums = {}

    def exchange_small(key, arrays, cost):
        slots = _in_own_slot(_pack(arrays, D), place, "small_grads_slot_" + key)
        backlog.start(_small_exchange_job(slots, 1, [0]), "small_grads_" + key, "ici", cost,
                      lambda outs, kept: small_sums.__setitem__(key, _unpack(_sum_slots(outs[0], "small_grads_sum_" + key),
                                                                             [a.shape for a in arrays])))

    exchange_small("most", [d_mix_pre_1, jnp.concatenate([d_mix_post_0, d_mix_post_1], 0),
                            jnp.concatenate([d_ffn_pre_0, d_ffn_pre_1], 0), jnp.concatenate([d_ffn_post_0, d_ffn_post_1], 0),
                            d_pool_scale, d_conv_b, d_ln_g, d_ln_b, d_pool_w, d_conv_w, d_sc_conv_w], 112.0)
    reduce_grad(0, run(_matmul, 34.0, h0, dz0, "tn", "mix0_in_dw", out_dtypes=(BF16,), tn=WIDE_N_TILE))
    dh = run(_matmul, 40.0, dz0, w_in0, "nt", "mix0_in_dx")
    grad_x, d_mix_pre_0 = run(_norms_bwd, 26.0, dx, dh, x0, gain(mix_pre_g, 0), None, None, "norms_bwd0")
    exchange_small("last", [d_mix_pre_0, loss_row], 5.0)

    upd, gr, first = {}, grads_big, {}

    def keep(where, key, outs):
        where[key] = outs
        return outs

    adamw_big = [
        (7, lambda: keep(first, "ffn_w2", _adamw(ffn_w2, gr[7], m_ffn_w2, v_ffn_w2, "adamw_ffn_w2_1", layer=1)), 46.0),
        (6, lambda: keep(first, "ffn_w1", _adamw(ffn_w1, gr[6], m_ffn_w1, v_ffn_w1, "adamw_ffn_w1_1", layer=1)), 46.0),
        (5, lambda: keep(upd, "sc_w_out", _adamw(sc_w_out, gr[5], m_sc_w_out, v_sc_w_out, "adamw_sc_w_out")), 14.0),
        (4, lambda: keep(upd, "sc_w_in", _adamw(sc_w_in, gr[4], m_sc_w_in, v_sc_w_in, "adamw_sc_w_in")), 35.0),
        (3, lambda: keep(upd, "ffn_w2", _adamw(ffn_w2, gr[3], m_ffn_w2, v_ffn_w2, "adamw_ffn_w2_0", layer=0,
                                               carried=first["ffn_w2"])), 46.0),
        (2, lambda: keep(upd, "ffn_w1", _adamw(ffn_w1, gr[2], m_ffn_w1, v_ffn_w1, "adamw_ffn_w1_0", layer=0,
                                               carried=first["ffn_w1"])), 46.0),
        (1, lambda: keep(upd, "ab_w_out", _adamw(ab_w_out, gr[1], m_ab_w_out, v_ab_w_out, "adamw_ab_w_out")), 14.0),
        (0, lambda: keep(upd, "ab_w_in", _adamw(ab_w_in, gr[0], m_ab_w_in, v_ab_w_in, "adamw_ab_w_in")), 19.0)]
    while adamw_big or backlog.flights:
        due = [a for a in adamw_big if gr[a[0]] is not None]
        if due:
            adamw_big.remove(due[0])
            backlog.run(due[0][1], due[0][2])
        else:
            backlog.poll(block=True)

    (g_mix_pre_1, g_mix_post, g_ffn_pre, g_ffn_post, g_pool_scale, g_conv_b, g_ln_g, g_ln_b, g_pool_w_full, g_conv_w_full,
     g_sc_conv_w_full) = small_sums["most"]
    g_mix_pre = jnp.concatenate([small_sums["last"][0], g_mix_pre_1], 0)
    loss = small_sums["last"][1][0, 0]
    own = lambda a, ax: lax.dynamic_slice_in_dim(a, chip * (a.shape[ax] // N_CHIPS), a.shape[ax] // N_CHIPS, axis=ax)
    g_pool_w, g_conv_w, g_sc_conv_w = own(g_pool_w_full, 1), own(g_conv_w_full, 1), own(g_sc_conv_w_full, 1)

    def small_update(w, g, m, v, name):
        shape = w.shape
        as3 = lambda a: a.reshape((1, -1, shape[-1]))
        outs = _adamw(as3(w), g.reshape((-1, shape[-1])), as3(m), as3(v), "adamw_" + name)
        return [o.reshape(shape) for o in outs]

    upd["mix_pre_g"] = small_update(mix_pre_g, g_mix_pre, m_mix_pre_g, v_mix_pre_g, "mix_pre_g")
    upd["mix_post_g"] = small_update(mix_post_g, g_mix_post, m_mix_post_g, v_mix_post_g, "mix_post_g")
    upd["ffn_pre_g"] = small_update(ffn_pre_g, g_ffn_pre, m_ffn_pre_g, v_ffn_pre_g, "ffn_pre_g")
    upd["ffn_post_g"] = small_update(ffn_post_g, g_ffn_post, m_ffn_post_g, v_ffn_post_g, "ffn_post_g")
    upd["pool_w"] = small_update(pool_w, g_pool_w, m_pool_w, v_pool_w, "pool_w")
    upd["pool_scale"] = small_update(pool_scale, g_pool_scale, m_pool_scale, v_pool_scale, "pool_scale")
    upd["conv_w"] = small_update(conv_w, g_conv_w, m_conv_w, v_conv_w, "conv_w")
    upd["conv_b"] = small_update(conv_b, g_conv_b, m_conv_b, v_conv_b, "conv_b")
    upd["conv_ln_g"] = small_update(conv_ln_g, g_ln_g, m_conv_ln_g, v_conv_ln_g, "conv_ln_g")
    upd["conv_ln_b"] = small_update(conv_ln_b, g_ln_b, m_conv_ln_b, v_conv_ln_b, "conv_ln_b")
    upd["sc_conv_w"] = small_update(sc_conv_w, g_sc_conv_w, m_sc_conv_w, v_sc_conv_w, "sc_conv_w")

    order = ["mix_pre_g", "mix_post_g", "ffn_pre_g", "ffn_post_g", "ab_w_in", "pool_w", "pool_scale", "conv_w", "conv_b",
             "conv_ln_g", "conv_ln_b", "ab_w_out", "sc_w_in", "sc_conv_w", "sc_w_out", "ffn_w1", "ffn_w2"]
    out = [loss, grad_x[None]]
    for part in range(4):
        out += [upd[n][part] for n in order]
    return tuple(out)
```

```python
import jax
import jax.numpy as jnp
from jax import lax
from jax.experimental import pallas as pl
from jax.experimental.pallas import tpu as pltpu

F32, BF16 = jnp.float32, jnp.bfloat16
EPS = 1e-6
N_GROUPS = 4
MAX_WINDOW = 16
CONV_K = 31
SHORT_K = 3
CONV_PAD = 32
SHORT_PAD = 8
ADAM_LR, ADAM_B1, ADAM_B2, ADAM_EPS, ADAM_WD, ADAM_STEP = 0.001, 0.9, 0.999, 1e-08, 0.01, 10
N_CHIPS = 4
VMEM_LIMIT_BYTES = 56 * 1024 * 1024
ROW_TILE = 256
CHUNK = 256
LONG_K_TILE = 4096
MESH = pl.DeviceIdType.MESH
HBM = pl.BlockSpec(memory_space=pltpu.HBM)
SEM = pl.BlockSpec(memory_space=pltpu.SEMAPHORE)
ANY = pl.BlockSpec(memory_space=pl.ANY)


def _cp(*sem):
    return pltpu.CompilerParams(dimension_semantics=sem, vmem_limit_bytes=VMEM_LIMIT_BYTES)


def _sigmoid(v):
    return 1.0 / (1.0 + jnp.exp(-v))


class _Behind:
    pending = []


def _call(body, prefetch=None, **kw):
    behind, _Behind.pending = _Behind.pending, []
    single = not isinstance(kw["out_shape"], (list, tuple))
    in_specs, scratch = list(kw["in_specs"]), list(kw.get("scratch_shapes", ()))
    out_shape = [kw["out_shape"]] if single else list(kw["out_shape"])
    out_specs = [kw["out_specs"]] if single else list(kw["out_specs"])
    n_pre = 0 if prefetch is None else 1
    n_own, n_behind = len(in_specs), len(behind)

    def wrapped(*refs):
        body(*refs[:n_pre + n_own], *refs[n_pre + n_own + n_behind:])

    specs = dict(grid=kw["grid"], in_specs=in_specs + [ANY] * n_behind, out_specs=out_specs)
    if prefetch is None:
        specs["scratch_shapes"] = scratch
    else:
        specs = dict(grid_spec=pltpu.PrefetchScalarGridSpec(num_scalar_prefetch=1, scratch_shapes=scratch, **specs))
    aliases = {n_pre + i: o for i, o in kw.get("input_output_aliases", {}).items()}
    call = pl.pallas_call(wrapped, name=kw["name"], out_shape=out_shape, input_output_aliases=aliases,
                          compiler_params=_cp(*kw["sem"]), **specs)

    def run(*args):
        outs = call(*([prefetch] * n_pre), *args, *behind)
        return outs[0] if single else list(outs)

    return run


_DIMS = {"nn": (((1,), (0,)), ((), ())), "nt": (((1,), (1,)), ((), ())), "tn": (((0,), (0,)), ((), ()))}


def _pick(n, cap, step=256):
    if n <= cap:
        return n
    return next(t for t in range(cap - cap % step, 0, -step) if n % t == 0)


def _matmul(a, b, mode, name, out_dtypes=(F32,), epilogue=None, epi=(), tm=1024, tn=1024, tk=2048):
    if mode == "tn":
        (K, M), (K2, N) = a.shape, b.shape
    elif mode == "nt":
        (M, K), (N, K2) = a.shape, b.shape
    else:
        (M, K), (K2, N) = a.shape, b.shape
    assert K == K2
    tm, tn, tk = _pick(M, tm), _pick(N, tn), _pick(K, tk)
    nk = K // tk
    a_spec = pl.BlockSpec((tk, tm), lambda i, j, k: (k, i)) if mode == "tn" else pl.BlockSpec((tm, tk), lambda i, j, k: (i, k))
    b_spec = pl.BlockSpec((tn, tk), lambda i, j, k: (j, k)) if mode == "nt" else pl.BlockSpec((tk, tn), lambda i, j, k: (k, j))
    o_spec = pl.BlockSpec((tm, tn), lambda i, j, k: (i, j))
    n_epi, n_out = len(epi), len(out_dtypes)

    def body(a_ref, b_ref, *rest):
        epi_refs, out_refs, scratch = rest[:n_epi], rest[n_epi:n_epi + n_out], rest[n_epi + n_out:]
        part = lax.dot_general(a_ref[...].astype(BF16), b_ref[...].astype(BF16), _DIMS[mode], preferred_element_type=F32)

        def finish(acc):
            outs = epilogue(acc, *[r[...] for r in epi_refs]) if epilogue else (acc,)
            for o_ref, o in zip(out_refs, outs):
                o_ref[...] = o.astype(o_ref.dtype)

        if nk == 1:
            finish(part)
        else:
            acc_ref = scratch[0]
            k = pl.program_id(2)

            @pl.when(k == 0)
            def _():
                acc_ref[...] = part

            @pl.when(k > 0)
            def _():
                acc_ref[...] += part

            @pl.when(k == nk - 1)
            def _():
                finish(acc_ref[...])

    outs = _call(
        body, name=name, grid=(M // tm, N // tn, nk),
        in_specs=[a_spec, b_spec] + [o_spec] * n_epi, out_specs=[o_spec] * n_out,
        out_shape=[jax.ShapeDtypeStruct((M, N), dt) for dt in out_dtypes],
        scratch_shapes=[pltpu.VMEM((tm, tn), F32)] if nk > 1 else [],
        sem=("parallel", "parallel", "arbitrary"))(a, b, *epi)
    return outs[0] if n_out == 1 else outs


def _rms(x, g):
    r = lax.rsqrt(jnp.mean(x * x, axis=-1, keepdims=True) + EPS)
    return x * r * g


def _rms_bwd(dy, x, g):
    r = lax.rsqrt(jnp.mean(x * x, axis=-1, keepdims=True) + EPS)
    xn = x * r
    dyg = dy * g
    dx = r * (dyg - xn * jnp.mean(dyg * xn, axis=-1, keepdims=True))
    return dx, jnp.sum(dy * xn, axis=0, keepdims=True)


def _rows(d, tr=ROW_TILE):
    return pl.BlockSpec((tr, d), lambda i: (i, 0))


def _vec(d):
    return pl.BlockSpec((1, d), lambda i: (0, 0))


def _accumulate(ref, val):
    @pl.when(pl.program_id(0) == 0)
    def _():
        ref[...] = val

    @pl.when(pl.program_id(0) > 0)
    def _():
        ref[...] += val


def _norm_fwd(x, g, name):
    T, D = x.shape

    def body(x_ref, g_ref, h_ref):
        h_ref[...] = _rms(x_ref[...], g_ref[...]).astype(BF16)

    return _call(body,name=name,grid=(T // ROW_TILE,), in_specs=[_rows(D), _vec(D)], out_specs=_rows(D),
                          out_shape=jax.ShapeDtypeStruct((T, D), BF16), sem=("parallel",))(x, g)


def _residual_norm(x, m, g_post, g_next, name):
    T, D = x.shape

    def body(x_ref, m_ref, gp_ref, gn_ref, xo_ref, h_ref):
        xo = x_ref[...] + _rms(m_ref[...], gp_ref[...])
        xo_ref[...] = xo
        h_ref[...] = _rms(xo, gn_ref[...]).astype(BF16)

    return _call(body,name=name,grid=(T // ROW_TILE,), in_specs=[_rows(D), _rows(D), _vec(D), _vec(D)],
                          out_specs=[_rows(D), _rows(D)],
                          out_shape=[jax.ShapeDtypeStruct((T, D), F32), jax.ShapeDtypeStruct((T, D), BF16)],
                          sem=("parallel",))(x, m, g_post, g_next)


def _loss_and_last_norm_bwd(x, m, g_post, target, name):
    T, D = x.shape

    def body(x_ref, m_ref, gp_ref, t_ref, dx_ref, dm_ref, dg_ref, loss_ref):
        m_val, gp = m_ref[...], gp_ref[...]
        err = x_ref[...] + _rms(m_val, gp) - t_ref[...]
        dx = err * (1.0 / D)
        dx_ref[...] = dx
        dm, dg = _rms_bwd(dx, m_val, gp)
        dm_ref[...] = dm.astype(BF16)
        _accumulate(dg_ref, dg)
        _accumulate(loss_ref, jnp.full((1, 128), 0.5 * jnp.sum(err * err) * (1.0 / D), F32))

    return _call(body,name=name,grid=(T // ROW_TILE,), in_specs=[_rows(D), _rows(D), _vec(D), _rows(D)],
                          out_specs=[_rows(D), _rows(D), _vec(D), _vec(128)],
                          out_shape=[jax.ShapeDtypeStruct((T, D), F32), jax.ShapeDtypeStruct((T, D), BF16),
                                     jax.ShapeDtypeStruct((1, D), F32), jax.ShapeDtypeStruct((1, 128), F32)],
                          sem=("arbitrary",))(x, m, g_post, target)


def _norms_bwd(dx, dh, x_in, g_pre, m_prev, g_post_prev, name):
    T, D = dx.shape
    with_prev = m_prev is not None

    def body(*refs):
        if with_prev:
            dx_ref, dh_ref, x_ref, gq_ref, m_ref, gp_ref, dxo_ref, dgq_ref, dm_ref, dgp_ref = refs
        else:
            dx_ref, dh_ref, x_ref, gq_ref, dxo_ref, dgq_ref = refs
        d_in, dgq = _rms_bwd(dh_ref[...], x_ref[...], gq_ref[...])
        dxo = dx_ref[...] + d_in
        dxo_ref[...] = dxo
        _accumulate(dgq_ref, dgq)
        if with_prev:
            dm, dgp = _rms_bwd(dxo, m_ref[...], gp_ref[...])
            dm_ref[...] = dm.astype(BF16)
            _accumulate(dgp_ref, dgp)

    ins, in_specs = [dx, dh, x_in, g_pre], [_rows(D), _rows(D), _rows(D), _vec(D)]
    out_specs = [_rows(D), _vec(D)]
    out_shape = [jax.ShapeDtypeStruct((T, D), F32), jax.ShapeDtypeStruct((1, D), F32)]
    if with_prev:
        ins += [m_prev, g_post_prev]
        in_specs += [_rows(D), _vec(D)]
        out_specs += [_rows(D), _vec(D)]
        out_shape += [jax.ShapeDtypeStruct((T, D), BF16), jax.ShapeDtypeStruct((1, D), F32)]
    return _call(body,name=name,grid=(T // ROW_TILE,), in_specs=in_specs, out_specs=out_specs, out_shape=out_shape,
                          sem=("arbitrary",))(*ins)


def _window_weights(g):
    w = 2 << g
    return w, [jnp.where(j < w, 1.0, 0.0).astype(F32) for j in range(MAX_WINDOW)]


def _valid_count(r0, rows, w):
    t = (lax.broadcasted_iota(jnp.int32, (rows, 1), 0) + (r0 + 1)).astype(F32)
    return jnp.minimum(t, w.astype(F32))


def _pool_fwd(z, pool_w, pool_scale, name):
    T = z.shape[0]
    PG = pool_w.shape[-1]
    DP = N_GROUPS * PG
    rc = min(CHUNK, T)

    def body(u_ref, pw_ref, sc_ref, pooled_ref, y_ref, pad):
        w, wts = _window_weights(pl.program_id(0))
        pad[pl.ds(0, MAX_WINDOW), :] = jnp.zeros((MAX_WINDOW, PG), F32)
        pad[pl.ds(MAX_WINDOW, T), :] = u_ref[...]
        for r0 in range(0, T, rc):
            acc = jnp.zeros((rc, PG), F32)
            for j in range(MAX_WINDOW):
                acc = acc + wts[j] * pad[pl.ds(MAX_WINDOW + r0 - j, rc), :]
            pooled = acc / _valid_count(r0, rc, w) - u_ref[pl.ds(r0, rc), :]
            pooled_ref[pl.ds(r0, rc), :] = pooled.astype(BF16)
        mixed = jnp.dot(pooled_ref[...], pw_ref[...].astype(BF16), preferred_element_type=F32)
        y_ref[...] = (mixed * sc_ref[...]).astype(BF16)

    col = lambda g: (0, g)
    return _call(
        body, name=name,grid=(N_GROUPS,),
        in_specs=[pl.BlockSpec((T, PG), col), pl.BlockSpec((None, PG, PG), lambda g: (g, 0, 0)), pl.BlockSpec((1, PG), col)],
        out_specs=[pl.BlockSpec((T, PG), col), pl.BlockSpec((T, PG), col)],
        out_shape=[jax.ShapeDtypeStruct((T, DP), BF16), jax.ShapeDtypeStruct((T, DP), BF16)],
        scratch_shapes=[pltpu.VMEM((T + MAX_WINDOW, PG), F32)], sem=("parallel",))(z, pool_w, pool_scale)


def _pool_bwd(dy, pooled, pool_w, pool_scale, name):
    T = dy.shape[0]
    PG = pool_w.shape[-1]
    DP = N_GROUPS * PG
    rc = min(CHUNK, T)

    def body(dy_ref, pooled_ref, pw_ref, sc_ref, du_ref, dpw_ref, dsc_ref, pad, dp_ref):
        w, wts = _window_weights(pl.program_id(0))
        pooled_v, pw = pooled_ref[...], pw_ref[...].astype(BF16)
        dy_v = dy_ref[...]
        mixed = jnp.dot(pooled_v, pw, preferred_element_type=F32)
        dsc_ref[...] = jnp.sum(dy_v * mixed, axis=0, keepdims=True)
        dmixed = (dy_v * sc_ref[...]).astype(BF16)
        dpw_ref[...] = lax.dot_general(pooled_v, dmixed, _DIMS["tn"], preferred_element_type=F32)
        dp_ref[...] = lax.dot_general(dmixed, pw, _DIMS["nt"], preferred_element_type=F32)
        pad[pl.ds(T, MAX_WINDOW), :] = jnp.zeros((MAX_WINDOW, PG), F32)
        for r0 in range(0, T, rc):
            pad[pl.ds(r0, rc), :] = dp_ref[pl.ds(r0, rc), :] / _valid_count(r0, rc, w)
        for r0 in range(0, T, rc):
            acc = jnp.zeros((rc, PG), F32)
            for j in range(MAX_WINDOW):
                acc = acc + wts[j] * pad[pl.ds(r0 + j, rc), :]
            du_ref[pl.ds(r0, rc), :] = (acc - dp_ref[pl.ds(r0, rc), :]).astype(BF16)

    col = lambda g: (0, g)
    return _call(
        body, name=name,grid=(N_GROUPS,),
        in_specs=[pl.BlockSpec((T, PG), col), pl.BlockSpec((T, PG), col), pl.BlockSpec((None, PG, PG), lambda g: (g, 0, 0)),
                  pl.BlockSpec((1, PG), col)],
        out_specs=[pl.BlockSpec((T, PG), col), pl.BlockSpec((None, PG, PG), lambda g: (g, 0, 0)), pl.BlockSpec((1, PG), col)],
        out_shape=[jax.ShapeDtypeStruct((T, DP), BF16), jax.ShapeDtypeStruct((N_GROUPS, PG, PG), F32),
                   jax.ShapeDtypeStruct((1, DP), F32)],
        scratch_shapes=[pltpu.VMEM((T + MAX_WINDOW, PG), F32), pltpu.VMEM((T, PG), F32)],
        sem=("parallel",))(dy, pooled, pool_w, pool_scale)


def _conv_fwd(z, conv_w, conv_b, d_pool, name, tc=128):
    T = z.shape[0]
    DC = conv_w.shape[-1]
    rc = min(CHUNK, T)
    v0, g0 = d_pool // tc, (d_pool + DC) // tc

    def body(v_ref, gt_ref, w_ref, b_ref, a_ref, c_ref, pad):
        pad[pl.ds(0, CONV_PAD), :] = jnp.zeros((CONV_PAD, tc), F32)
        for r0 in range(0, T, rc):
            a = v_ref[pl.ds(r0, rc), :] * _sigmoid(gt_ref[pl.ds(r0, rc), :])
            a_ref[pl.ds(r0, rc), :] = a
            pad[pl.ds(CONV_PAD + r0, rc), :] = a
        for r0 in range(0, T, rc):
            acc = jnp.zeros((rc, tc), F32) + b_ref[...]
            for k in range(CONV_K):
                acc = acc + w_ref[pl.ds(k, 1), :] * pad[pl.ds(CONV_PAD - (CONV_K - 1) + k + r0, rc), :]
            c_ref[pl.ds(r0, rc), :] = acc

    col = lambda j: (0, j)
    return _call(
        body, name=name,grid=(DC // tc,),
        in_specs=[pl.BlockSpec((T, tc), lambda j: (0, v0 + j)), pl.BlockSpec((T, tc), lambda j: (0, g0 + j)),
                  pl.BlockSpec((CONV_K, tc), col), pl.BlockSpec((1, tc), col)],
        out_specs=[pl.BlockSpec((T, tc), col), pl.BlockSpec((T, tc), col)],
        out_shape=[jax.ShapeDtypeStruct((T, DC), F32), jax.ShapeDtypeStruct((T, DC), F32)],
        scratch_shapes=[pltpu.VMEM((T + CONV_PAD, tc), F32)], sem=("parallel",))(z, z, conv_w, conv_b)


def _conv_bwd(dc, a, z, conv_w, d_pool, name, tc=128):
    T, DC = dc.shape
    rc = min(CHUNK, T)
    v0, g0 = d_pool // tc, (d_pool + DC) // tc

    def body(dc_ref, a_ref, v_ref, gt_ref, w_ref, dv_ref, dg_ref, dw_ref, db_ref, apad, dpad):
        apad[pl.ds(0, CONV_PAD), :] = jnp.zeros((CONV_PAD, tc), F32)
        apad[pl.ds(CONV_PAD, T), :] = a_ref[...]
        dpad[pl.ds(0, T), :] = dc_ref[...]
        dpad[pl.ds(T, CONV_PAD), :] = jnp.zeros((CONV_PAD, tc), F32)
        db_ref[...] = jnp.sum(dc_ref[...], axis=0, keepdims=True)
        for k in range(CONV_K):
            acc = jnp.zeros((8, tc), F32)
            for r0 in range(0, T, rc):
                prod = dc_ref[pl.ds(r0, rc), :] * apad[pl.ds(CONV_PAD - (CONV_K - 1) + k + r0, rc), :]
                acc = acc + jnp.sum(prod.reshape(rc // 8, 8, tc), axis=0)
            dw_ref[pl.ds(k, 1), :] = jnp.sum(acc, axis=0, keepdims=True)
        for r0 in range(0, T, rc):
            da = jnp.zeros((rc, tc), F32)
            for k in range(CONV_K):
                da = da + w_ref[pl.ds(k, 1), :] * dpad[pl.ds(r0 + (CONV_K - 1) - k, rc), :]
            sig = _sigmoid(gt_ref[pl.ds(r0, rc), :])
            dv_ref[pl.ds(r0, rc), :] = (da * sig).astype(BF16)
            dg_ref[pl.ds(r0, rc), :] = (da * v_ref[pl.ds(r0, rc), :] * sig * (1.0 - sig)).astype(BF16)

    col = lambda j: (0, j)
    return _call(
        body, name=name,grid=(DC // tc,),
        in_specs=[pl.BlockSpec((T, tc), col), pl.BlockSpec((T, tc), col), pl.BlockSpec((T, tc), lambda j: (0, v0 + j)),
                  pl.BlockSpec((T, tc), lambda j: (0, g0 + j)), pl.BlockSpec((CONV_K, tc), col)],
        out_specs=[pl.BlockSpec((T, tc), col), pl.BlockSpec((T, tc), col), pl.BlockSpec((CONV_K, tc), col),
                   pl.BlockSpec((1, tc), col)],
        out_shape=[jax.ShapeDtypeStruct((T, DC), BF16), jax.ShapeDtypeStruct((T, DC), BF16),
                   jax.ShapeDtypeStruct((CONV_K, DC), F32), jax.ShapeDtypeStruct((1, DC), F32)],
        scratch_shapes=[pltpu.VMEM((T + CONV_PAD, tc), F32), pltpu.VMEM((T + CONV_PAD, tc), F32)],
        sem=("parallel",))(dc, a, z, z, conv_w)


def _layer_norm_parts(c, g, b):
    mu = jnp.mean(c, axis=-1, keepdims=True)
    xc = c - mu
    rstd = lax.rsqrt(jnp.mean(xc * xc, axis=-1, keepdims=True) + EPS)
    xhat = xc * rstd
    return xhat, rstd, xhat * g + b


def _ln_silu_fwd(c, g, b, name):
    T, DC = c.shape

    def body(c_ref, g_ref, b_ref, y_ref):
        _, _, ln = _layer_norm_parts(c_ref[...], g_ref[...], b_ref[...])
        y_ref[...] = (ln * _sigmoid(ln)).astype(BF16)

    return _call(body,name=name,grid=(T // ROW_TILE,), in_specs=[_rows(DC), _vec(DC), _vec(DC)], out_specs=_rows(DC),
                          out_shape=jax.ShapeDtypeStruct((T, DC), BF16), sem=("parallel",))(c, g, b)


def _ln_silu_bwd(dy, c, g, b, name):
    T, DC = c.shape

    def body(dy_ref, c_ref, g_ref, b_ref, dc_ref, dg_ref, db_ref):
        gain = g_ref[...]
        xhat, rstd, ln = _layer_norm_parts(c_ref[...], gain, b_ref[...])
        s = _sigmoid(ln)
        dln = dy_ref[...] * (s * (1.0 + ln * (1.0 - s)))
        _accumulate(dg_ref, jnp.sum(dln * xhat, axis=0, keepdims=True))
        _accumulate(db_ref, jnp.sum(dln, axis=0, keepdims=True))
        dxh = dln * gain
        dc_ref[...] = rstd * (dxh - jnp.mean(dxh, axis=-1, keepdims=True) - xhat * jnp.mean(dxh * xhat, axis=-1, keepdims=True))

    return _call(body,name=name,grid=(T // ROW_TILE,),
                          in_specs=[pl.BlockSpec((ROW_TILE, DC), lambda i: (i, 1)), _rows(DC), _vec(DC), _vec(DC)],
                          out_specs=[_rows(DC), _vec(DC), _vec(DC)],
                          out_shape=[jax.ShapeDtypeStruct((T, DC), F32), jax.ShapeDtypeStruct((1, DC), F32),
                                     jax.ShapeDtypeStruct((1, DC), F32)],
                          sem=("arbitrary",))(dy, c, g, b)


def _short_specs(T, DS, tc):
    n = DS // tc
    return [pl.BlockSpec((T, tc), lambda j: (0, j)), pl.BlockSpec((T, tc), lambda j: (0, n + j)),
            pl.BlockSpec((T, tc), lambda j: (0, 2 * n + j))]


def _short_fwd(z, w, name, tc=256):
    T = z.shape[0]
    DS = w.shape[-1]
    rc = min(CHUNK, T)

    def body(b_ref, cg_ref, u_ref, w_ref, y_ref, pad):
        pad[pl.ds(0, SHORT_PAD), :] = jnp.zeros((SHORT_PAD, tc), F32)
        pad[pl.ds(SHORT_PAD, T), :] = cg_ref[...] * u_ref[...]
        for r0 in range(0, T, rc):
            r = jnp.zeros((rc, tc), F32)
            for k in range(SHORT_K):
                r = r + w_ref[pl.ds(k, 1), :] * pad[pl.ds(SHORT_PAD - (SHORT_K - 1) + k + r0, rc), :]
            y_ref[pl.ds(r0, rc), :] = (b_ref[pl.ds(r0, rc), :] * r).astype(BF16)

    col = lambda j: (0, j)
    return _call(body,name=name,grid=(DS // tc,), in_specs=_short_specs(T, DS, tc) + [pl.BlockSpec((SHORT_K, tc), col)],
                          out_specs=pl.BlockSpec((T, tc), col), out_shape=jax.ShapeDtypeStruct((T, DS), BF16),
                          scratch_shapes=[pltpu.VMEM((T + SHORT_PAD, tc), F32)], sem=("parallel",))(z, z, z, w)


def _short_bwd(dy, z, w, name, tc=256):
    T, DS = dy.shape
    rc = min(CHUNK, T)

    def body(dy_ref, b_ref, cg_ref, u_ref, w_ref, db_ref, dcg_ref, du_ref, dw_ref, qpad, rpad):
        qpad[pl.ds(0, SHORT_PAD), :] = jnp.zeros((SHORT_PAD, tc), F32)
        qpad[pl.ds(SHORT_PAD, T), :] = cg_ref[...] * u_ref[...]
        rpad[pl.ds(0, T), :] = dy_ref[...] * b_ref[...]
        rpad[pl.ds(T, SHORT_PAD), :] = jnp.zeros((SHORT_PAD, tc), F32)
        accs = [jnp.zeros((8, tc), F32) for _ in range(SHORT_K)]
        for r0 in range(0, T, rc):
            r = jnp.zeros((rc, tc), F32)
            dq = jnp.zeros((rc, tc), F32)
            dr = rpad[pl.ds(r0, rc), :]
            for k in range(SHORT_K):
                q_k = qpad[pl.ds(SHORT_PAD - (SHORT_K - 1) + k + r0, rc), :]
                r = r + w_ref[pl.ds(k, 1), :] * q_k
                dq = dq + w_ref[pl.ds(k, 1), :] * rpad[pl.ds(r0 + (SHORT_K - 1) - k, rc), :]
                accs[k] = accs[k] + jnp.sum((dr * q_k).reshape(rc // 8, 8, tc), axis=0)
            db_ref[pl.ds(r0, rc), :] = (dy_ref[pl.ds(r0, rc), :] * r).astype(BF16)
            dcg_ref[pl.ds(r0, rc), :] = (dq * u_ref[pl.ds(r0, rc), :]).astype(BF16)
            du_ref[pl.ds(r0, rc), :] = (dq * cg_ref[pl.ds(r0, rc), :]).astype(BF16)
        for k in range(SHORT_K):
            dw_ref[pl.ds(k, 1), :] = jnp.sum(accs[k], axis=0, keepdims=True)

    col = lambda j: (0, j)
    tile = pl.BlockSpec((T, tc), col)
    return _call(body,name=name,grid=(DS // tc,),
                          in_specs=[tile] + _short_specs(T, DS, tc) + [pl.BlockSpec((SHORT_K, tc), col)],
                          out_specs=[tile, tile, tile, pl.BlockSpec((SHORT_K, tc), col)],
                          out_shape=[jax.ShapeDtypeStruct((T, DS), BF16)] * 3 + [jax.ShapeDtypeStruct((SHORT_K, DS), F32)],
                          scratch_shapes=[pltpu.VMEM((T + SHORT_PAD, tc), F32), pltpu.VMEM((T + SHORT_PAD, tc), F32)],
                          sem=("parallel",))(dy, z, z, z, w)


def _tile_rows(rows, cols, n_bufs):
    budget = VMEM_LIMIT_BYTES * 3 // 4 // (2 * n_bufs * 4 * cols)
    tr = rows
    while tr > budget and tr % 16 == 0:
        tr //= 2
    return tr


def _placed_call(body, name, place, grid, in_specs, out_specs, out_shape, ins):
    return _call(body,prefetch=place, name=name, grid=grid, in_specs=in_specs, out_specs=out_specs, out_shape=out_shape,
                 sem=("parallel",))(*ins)


def _cast_into_full(w, layer, kind, place, name):
    _, R, C = w.shape
    tr = _tile_rows(R, C, 2)
    nb = R // tr
    if kind == "col":
        full, out_spec = (R, C * N_CHIPS), pl.BlockSpec((tr, C), lambda i, s: (i, s[0]))
    else:
        full, out_spec = (R * N_CHIPS, C), pl.BlockSpec((tr, C), lambda i, s: (s[0] * nb + i, 0))

    def body(s_ref, w_ref, o_ref):
        o_ref[...] = w_ref[...].astype(BF16)

    return _placed_call(body, name, place, (nb,), [pl.BlockSpec((None, tr, C), lambda i, s: (layer, i, 0))], out_spec,
                        jax.ShapeDtypeStruct(full, BF16), [w])


def _add_pair(grad, theirs, kind, place, name):
    R, C = grad.shape
    piece_rows = R // 2 if kind == "col" else R // N_CHIPS // 2
    tr = _tile_rows(piece_rows, C, 3)
    nb = piece_rows // tr
    if kind == "col":
        g_spec = pl.BlockSpec((tr, C), lambda i, s: (s[1] * nb + i, 0))
    else:
        g_spec = pl.BlockSpec((tr, C), lambda i, s: ((2 * (i // nb) + s[1]) * nb + i % nb, 0))
    flat = pl.BlockSpec((tr, C), lambda i, s: (i, 0))

    def body(s_ref, a_ref, b_ref, o_ref):
        o_ref[...] = (a_ref[...].astype(F32) + b_ref[...].astype(F32)).astype(BF16)

    return _placed_call(body, name, place, (R // 2 // tr,), [g_spec, flat], flat, jax.ShapeDtypeStruct((R // 2, C), BF16),
                        [grad, theirs])


def _sum_chips(chip_sum, arrived, kind, place, name):
    _, H, W = arrived.shape
    tr = _tile_rows(H, W, 6)
    nb = H // tr
    if kind == "col":
        own_spec = pl.BlockSpec((tr, W), lambda i, s: (i, s[0]))
    else:
        own_spec = pl.BlockSpec((tr, W), lambda i, s: (s[0] * nb + i, 0))

    def body(s_ref, p_ref, r_ref, o_ref):
        acc = p_ref[...].astype(F32)
        for i in range(N_CHIPS - 1):
            acc = acc + r_ref[i].astype(F32)
        o_ref[...] = acc

    return _placed_call(body, name, place, (nb,), [own_spec, pl.BlockSpec((N_CHIPS - 1, tr, W), lambda i, s: (0, i, 0))],
                        pl.BlockSpec((tr, W), lambda i, s: (s[1] * nb + i, 0)), jax.ShapeDtypeStruct((2 * H, W), F32),
                        [chip_sum, arrived])


def _adamw_values(w, g, m, v):
    m = ADAM_B1 * m + (1.0 - ADAM_B1) * g
    v = ADAM_B2 * v + (1.0 - ADAM_B2) * (g * g)
    m_hat = m / (1.0 - ADAM_B1 ** ADAM_STEP)
    v_hat = v / (1.0 - ADAM_B2 ** ADAM_STEP)
    return -ADAM_LR * (m_hat / (jnp.sqrt(v_hat) + ADAM_EPS) + ADAM_WD * w), m, v


def _adamw(w, g, m, v, name, layer=0, carried=None):
    L, R, C = w.shape
    tr = _tile_rows(R, C, 8)

    def body(w_ref, g_ref, m_ref, v_ref, *rest):
        go_ref, d_ref, mo_ref, vo_ref = rest[-4:]
        g_val = g_ref[...]
        d, m_new, v_new = _adamw_values(w_ref[...], g_val, m_ref[...], v_ref[...])
        go_ref[...], d_ref[...], mo_ref[...], vo_ref[...] = g_val, d, m_new, v_new

    lay = pl.BlockSpec((None, tr, C), lambda i: (layer, i, 0))
    ins = [w, g, m, v]
    in_specs = [lay, pl.BlockSpec((tr, C), lambda i: (i, 0)), lay, lay]
    aliases = {}
    if carried is not None:
        ins += list(carried)
        in_specs += [pl.BlockSpec(memory_space=pl.ANY)] * 4
        aliases = {4 + i: i for i in range(4)}
    return _call(body, name=name, grid=(R // tr,), in_specs=in_specs, out_specs=[lay] * 4,
                 out_shape=[jax.ShapeDtypeStruct((L, R, C), F32)] * 4, input_output_aliases=aliases, sem=("parallel",))(*ins)


def _aligned(v, m):
    return v if isinstance(v, int) else pl.multiple_of(v, m)


def _place():
    x, y, c = lax.axis_index("x"), lax.axis_index("y"), lax.axis_index("c")
    other_chips = [(x, 1 - y), (1 - x, y), (1 - x, 1 - y)]
    return x, y, c, 2 * x + y, other_chips


def _chip_index(chip):
    return 2 * chip[0] + chip[1]


def _piece(ref, kind, k, h):
    R, C = ref.shape
    if kind == "col":
        return ref.at[pl.ds(_aligned(h * (R // 2), 16), R // 2), pl.ds(_aligned(k * (C // N_CHIPS), 128), C // N_CHIPS)]
    rs = R // N_CHIPS
    return ref.at[pl.ds(_aligned(k * rs + h * (rs // 2), 16), rs // 2), :]


def _compact_piece(ref, kind, k):
    R2, C = ref.shape
    if kind == "col":
        return ref.at[:, pl.ds(_aligned(k * (C // N_CHIPS), 128), C // N_CHIPS)]
    return ref.at[pl.ds(_aligned(k * (R2 // N_CHIPS), 16), R2 // N_CHIPS), :]


def _half_rows(ref, h):
    R = ref.shape[0]
    return ref.at[pl.ds(_aligned(h * (R // 2), 16), R // 2), :]


class _Copies:
    def __init__(self, send_sems, recv_sems):
        self.send_sems, self.recv_sems = send_sems, recv_sems
        self.n_remote = 0

    def remote(self, src, dst, device):
        k = self.n_remote
        self.n_remote += 1
        return pltpu.make_async_remote_copy(src_ref=src, dst_ref=dst, send_sem=self.send_sems.at[k], recv_sem=self.recv_sems.at[k],
                                            device_id=device, device_id_type=MESH)


class _Job:
    def __init__(self, ins, out_shape, aliases, n_remote, build):
        self.ins, self.out_shape, self.aliases, self.n_remote, self.build = list(ins), list(out_shape), dict(aliases), n_remote, build


class _Flying:
    def __init__(self, job, send_sems, recv_sems, bufs, token):
        self.job, self.send_sems, self.recv_sems, self.bufs, self.token = job, send_sems, recv_sems, bufs, token


def _job_refs(job, buf_refs):
    n_out = len(job.out_shape)
    kept = [i for i in range(len(job.ins)) if i not in job.aliases]
    ins = [buf_refs[job.aliases[i]] if i in job.aliases else buf_refs[n_out + kept.index(i)] for i in range(len(job.ins))]
    return ins, list(buf_refs[:n_out])


def _start_job(job, name, after=()):
    n_in, n_out, n_after = len(job.ins), len(job.out_shape), len(after)
    kept = [i for i in range(n_in) if i not in job.aliases]
    n_bufs = n_out + len(kept)

    def body(*refs):
        in_refs, out_refs = refs[:n_in], refs[n_in + n_after:n_in + n_after + n_out]
        send_sems, recv_sems, token = refs[n_in + n_after + n_bufs:]
        for d in job.build(in_refs, out_refs, _Copies(send_sems, recv_sems)):
            d.start()
        token[...] = jnp.zeros_like(token)

    aliases = dict(job.aliases)
    aliases.update({i: n_out + k for k, i in enumerate(kept)})
    sems = pltpu.SemaphoreType.DMA((job.n_remote,))
    outs = pl.pallas_call(
        body, name=name, in_specs=[HBM] * n_in + [ANY] * n_after,
        out_specs=[HBM] * n_bufs + [SEM, SEM, pl.BlockSpec(memory_space=pltpu.VMEM)],
        out_shape=job.out_shape + [jax.ShapeDtypeStruct(job.ins[i].shape, job.ins[i].dtype) for i in kept]
        + [sems, sems, jax.ShapeDtypeStruct((8, 128), F32)],
        input_output_aliases=aliases,
        compiler_params=pltpu.CompilerParams(has_side_effects=pltpu.SideEffectType.DATAFLOW_SIDE_EFFECTING))(*job.ins, *after)
    return _Flying(job, outs[n_bufs], outs[n_bufs + 1], list(outs[:n_bufs]), outs[n_bufs + 2])


def _wait_job(flying, name, after=()):
    job, n_bufs, n_after = flying.job, len(flying.bufs), len(after)

    def body(*refs):
        in_refs, out_refs = _job_refs(job, refs[:n_bufs])
        send_sems, recv_sems = refs[n_bufs:n_bufs + 2]
        copies = job.build(in_refs, out_refs, _Copies(send_sems, recv_sems))
        for d in copies:
            d.wait_send()
        for d in copies:
            d.wait_recv()

    outs = pl.pallas_call(
        body, name=name, in_specs=[HBM] * n_bufs + [SEM, SEM] + [ANY] * n_after, out_specs=[HBM] * n_bufs,
        out_shape=[jax.ShapeDtypeStruct(b.shape, b.dtype) for b in flying.bufs],
        input_output_aliases={i: i for i in range(n_bufs)},
        compiler_params=pltpu.CompilerParams(has_side_effects=pltpu.SideEffectType.DATAFLOW_SIDE_EFFECTING))(
            *flying.bufs, flying.send_sems, flying.recv_sems, *after)
    return list(outs[:len(job.out_shape)]), list(outs[len(job.out_shape):])


def _in_place(arrays):
    return [jax.ShapeDtypeStruct(a.shape, a.dtype) for a in arrays], {u: u for u in range(len(arrays))}


def _rows_part(ref, part, n_parts):
    h = ref.shape[0] // n_parts
    return ref.at[pl.ds(part * h, h), :]


def _gather_job(full, kind, stage):
    def build(in_refs, out_refs, cp):
        x, y, c, me, (y_nbr, x_nbr, diagonal) = _place()
        (ref,) = out_refs
        sibling = (x, y, 1 - c)
        if stage == 1:
            mine = _piece(ref, kind, me, c)
            return [cp.remote(mine, mine, (*y_nbr, c)), cp.remote(mine, mine, (*x_nbr, c))]
        from_y, from_x = _piece(ref, kind, _chip_index(y_nbr), c), _piece(ref, kind, _chip_index(x_nbr), c)
        if stage == 2:
            relay_0, relay_1 = _rows_part(from_x, 0, 2), _rows_part(from_y, 1, 2)
            return [cp.remote(relay_0, relay_0, (*y_nbr, c)), cp.remote(relay_1, relay_1, (*x_nbr, c)),
                    cp.remote(from_y, from_y, sibling), cp.remote(from_x, from_x, sibling)]
        from_diagonal = _piece(ref, kind, _chip_index(diagonal), c)
        return [cp.remote(from_diagonal, from_diagonal, sibling)]

    return _Job([full], *_in_place([full]), {1: 2, 2: 4, 3: 1}[stage], build)


def _gather_small_job(fulls, axes):
    def build(in_refs, out_refs, cp):
        x, y, c, me, chips = _place()
        copies = []
        for ref, ax in zip(out_refs, axes):
            n = ref.shape[ax] // N_CHIPS
            idx = [slice(None)] * len(ref.shape)
            idx[ax] = pl.ds(_aligned(me * n, n), n)
            mine = ref.at[tuple(idx)]
            copies += [cp.remote(mine, mine, (*chip, c)) for chip in chips]
        return copies

    return _Job(fulls, *_in_place(fulls), 3 * len(fulls), build)


def _exchange_halves_job(grads, kinds):
    def build(in_refs, out_refs, cp):
        x, y, c, me, chips = _place()
        copies = []
        for src, dst, kind in zip(in_refs, out_refs, kinds):
            if kind == "col":
                copies.append(cp.remote(_half_rows(src, 1 - c), dst, (x, y, 1 - c)))
            else:
                copies += [cp.remote(_piece(src, "row", k, 1 - c), _compact_piece(dst, "row", k), (x, y, 1 - c))
                           for k in range(N_CHIPS)]
        return copies

    out_shape = [jax.ShapeDtypeStruct((g.shape[0] // 2, g.shape[1]), g.dtype) for g in grads]
    return _Job(grads, out_shape, {}, sum(1 if k == "col" else N_CHIPS for k in kinds), build)


def _scatter_job(half, kind):
    def build(in_refs, out_refs, cp):
        x, y, c, me, chips = _place()
        (src,), (dst,) = in_refs, out_refs
        return [cp.remote(_compact_piece(src, kind, _chip_index(chip)), dst.at[r], (*chip, c)) for r, chip in enumerate(chips)]

    part_shape = (half.shape[0], half.shape[1] // N_CHIPS) if kind == "col" else (half.shape[0] // N_CHIPS, half.shape[1])
    return _Job([half], [jax.ShapeDtypeStruct((N_CHIPS - 1,) + part_shape, half.dtype)], {}, N_CHIPS - 1, build)


def _share_job(shards):
    def build(in_refs, out_refs, cp):
        x, y, c, me, chips = _place()
        copies = []
        for ref in out_refs:
            mine = _half_rows(ref, c)
            copies.append(cp.remote(mine, mine, (x, y, 1 - c)))
        return copies

    return _Job(shards, *_in_place(shards), len(shards), build)


N_DEVICES = 2 * N_CHIPS


def _small_exchange_job(slots):
    def build(in_refs, out_refs, cp):
        x, y, c, me, chips = _place()
        (ref,) = out_refs
        mine = ref.at[2 * me + c]
        return [cp.remote(mine, mine, (x ^ (p >> 2), y ^ ((p >> 1) & 1), c ^ (p & 1))) for p in range(1, N_DEVICES)]

    return _Job([slots], *_in_place([slots]), N_DEVICES - 1, build)


def _in_own_slot(packed, place, name):
    R, C = packed.shape

    def body(s_ref, p_ref, o_ref):
        o_ref[...] = p_ref[...]

    return _placed_call(body, name, place, (1,), [pl.BlockSpec((R, C), lambda i, s: (0, 0))],
                        pl.BlockSpec((None, R, C), lambda i, s: (2 * s[0] + s[1], 0, 0)),
                        jax.ShapeDtypeStruct((N_DEVICES, R, C), F32), [packed])


def _sum_slots(slots, name):
    n, R, C = slots.shape

    def body(s_ref, o_ref):
        acc = s_ref[0]
        for i in range(1, n):
            acc = acc + s_ref[i]
        o_ref[...] = acc

    return _call(body,name=name, grid=(1,), in_specs=[pl.BlockSpec((n, R, C), lambda i: (0, 0, 0))],
                 out_specs=pl.BlockSpec((R, C), lambda i: (0, 0)), out_shape=jax.ShapeDtypeStruct((R, C), F32),
                 sem=("arbitrary",))(slots)


def _packed_rows(size, width):
    return -(-size // (8 * width)) * 8


def _pack(arrays, width):
    rows = []
    for a in arrays:
        flat = a.reshape(-1)
        n_rows = _packed_rows(flat.shape[0], width)
        rows.append(jnp.pad(flat, (0, n_rows * width - flat.shape[0])).reshape(n_rows, width))
    return jnp.concatenate(rows, axis=0)


def _unpack(packed, shapes):
    out, r0, width = [], 0, packed.shape[1]
    for shape in shapes:
        size = 1
        for d in shape:
            size *= d
        out.append(packed[r0:r0 + _packed_rows(size, width)].reshape(-1)[:size].reshape(shape))
        r0 += _packed_rows(size, width)
    return out


class _Backlog:
    def __init__(self, first):
        self.now, self.free, self.flights, self.last, self.chain = 0.0, {"ici": 0.0, "d2d": 0.0}, [], first, []

    def run(self, fn, us, *args, **kw):
        out = fn(*args, **kw)
        self.now += us
        self.last = out[0] if isinstance(out, (list, tuple)) else out
        self.poll()
        return out

    def start(self, job, name, link, cost, done):
        flying = _start_job(job, name + "_start", self.chain)
        self.chain = [flying.token]
        _Behind.pending.append(flying.token)
        ends = max(self.now, self.free[link]) + cost
        self.free[link] = ends
        self.flights.append((ends + LANDING_SLACK_US, name, flying, done))
        self.flights.sort(key=lambda f: f[0])

    def poll(self, block=False):
        while self.flights and (block or self.flights[0][0] <= self.now):
            ends, name, flying, done = self.flights.pop(0)
            self.now, block = max(self.now, ends), False
            done(*_wait_job(flying, name + "_wait", [self.last] + self.chain))


class _GatherStream:
    def __init__(self, backlog, bufs, kinds, costs):
        self.backlog, self.bufs, self.kinds, self.costs, self.begun, self.complete = backlog, bufs, kinds, costs, 0, set()
        self.begin()

    def begin(self):
        u, self.begun = self.begun, self.begun + 1
        self.backlog.start(_gather_job(self.bufs[u], self.kinds[u], 1), "gather_%d" % u, "ici", 0.5 * self.costs[u],
                           lambda outs, kept: self.arrived(u, outs[0]))

    def arrived(self, u, buf):
        self.bufs[u] = buf
        self.backlog.start(_gather_job(buf, self.kinds[u], 2), "relay_%d" % u, "ici", 0.25 * self.costs[u],
                           lambda outs, kept: self.relayed(u, outs[0]))
        while self.begun <= min(u + (1 if u < STARVED_WEIGHTS else 2), len(self.bufs) - 1):
            self.begin()

    def relayed(self, u, buf):
        self.bufs[u] = buf
        self.backlog.start(_gather_job(buf, self.kinds[u], 3), "handon_%d" % u, "d2d", D2D_SHARE * self.costs[u],
                           lambda outs, kept: self.handed(u, outs[0]))

    def handed(self, u, buf):
        self.bufs[u] = buf
        self.complete.add(u)

    def ready(self, u):
        while u not in self.complete:
            assert self.backlog.flights, "weight %d is not on its way" % u
            self.backlog.poll(block=True)
        return self.bufs[u]


class _GradStream:
    def __init__(self, backlog, u, name, kind, cost, g, place, results):
        self.backlog, self.u, self.name, self.kind, self.cost, self.place, self.results = backlog, u, name, kind, cost, place, results
        backlog.start(_exchange_halves_job([g], [kind]), "to_sibling_" + name, "d2d", D2D_SHARE * cost, self.exchanged)

    def exchanged(self, outs, kept):
        chip_sum = self.backlog.run(_add_pair, SIDE_KERNEL_US, kept[0], outs[0], self.kind, self.place, "chip_sum_" + self.name)
        self.backlog.start(_scatter_job(chip_sum, self.kind), "to_owners_" + self.name, "ici", self.cost, self.scattered)

    def scattered(self, outs, kept):
        reduced = self.backlog.run(_sum_chips, SIDE_KERNEL_US, kept[0], outs[0], self.kind, self.place, "reduce_" + self.name)
        self.backlog.start(_share_job([reduced]), "share_" + self.name, "d2d", D2D_SHARE * self.cost, self.shared)

    def shared(self, outs, kept):
        self.results[self.u] = outs[0]


SIDE_KERNEL_US = 12.0
D2D_SHARE = 0.15
LANDING_SLACK_US = 5.0
STARVED_WEIGHTS = 8


def kernel(x, mix_pre_g, mix_post_g, ffn_pre_g, ffn_post_g, ab_w_in, pool_w, pool_scale, conv_w, conv_b, conv_ln_g, conv_ln_b, ab_w_out, sc_w_in, sc_conv_w, sc_w_out, ffn_w1, ffn_w2, loss_target, m_mix_pre_g, m_mix_post_g, m_ffn_pre_g, m_ffn_post_g, m_ab_w_in, m_pool_w, m_pool_scale, m_conv_w, m_conv_b, m_conv_ln_g, m_conv_ln_b, m_ab_w_out, m_sc_w_in, m_sc_conv_w, m_sc_w_out, m_ffn_w1, m_ffn_w2, v_mix_pre_g, v_mix_post_g, v_ffn_pre_g, v_ffn_post_g, v_ab_w_in, v_pool_w, v_pool_scale, v_conv_w, v_conv_b, v_conv_ln_g, v_conv_ln_b, v_ab_w_out, v_sc_w_in, v_sc_conv_w, v_sc_w_out, v_ffn_w1, v_ffn_w2):
    x0, target = x[0], loss_target[0]
    T, D = x0.shape
    DP = pool_scale.shape[-1]
    gain = lambda g, layer: g[layer][None, :]

    big = [("ab_w_in", ab_w_in, 0, "col", 67.0), ("ab_w_out", ab_w_out, 0, "row", 44.0),
           ("ffn_w1_0", ffn_w1, 0, "col", 177.0), ("ffn_w2_0", ffn_w2, 0, "row", 177.0),
           ("sc_w_in", sc_w_in, 0, "col", 133.0), ("sc_w_out", sc_w_out, 0, "row", 44.0),
           ("ffn_w1_1", ffn_w1, 1, "col", 177.0), ("ffn_w2_1", ffn_w2, 1, "row", 177.0)]
    kinds = [b[3] for b in big]
    chip = 2 * lax.axis_index("x") + lax.axis_index("y")
    place = jnp.stack([chip, lax.axis_index("c")]).astype(jnp.int32)

    def own_in_zeros(shard, ax):
        full = jnp.zeros(tuple(d * N_CHIPS if i == ax else d for i, d in enumerate(shard.shape)), shard.dtype)
        return lax.dynamic_update_slice_in_dim(full, shard, chip * shard.shape[ax], axis=ax)

    W = [_cast_into_full(w, layer, kind, place, "cast_" + name) for name, w, layer, kind, _ in big]
    smalls = [own_in_zeros(pool_w[0], 1), own_in_zeros(conv_w[0], 1), own_in_zeros(sc_conv_w[0], 1)]
    backlog = _Backlog(x0)
    run = backlog.run
    small_weights = []
    backlog.start(_gather_small_job(smalls, [1, 1, 1]), "gather_small", "ici", 6.0, lambda outs, kept: small_weights.extend(outs))
    gather = _GatherStream(backlog, W, kinds, [b[4] for b in big])

    relu_sq = lambda acc: (jnp.maximum(acc, 0.0), jnp.square(jnp.maximum(acc, 0.0)))
    relu_sq_bwd = lambda acc, a: (acc * (2.0 * a.astype(F32)),)

    h0 = run(_norm_fwd, 12.0, x0, gain(mix_pre_g, 0), "norm_in")
    z0 = run(_matmul, 35.0, h0, gather.ready(0), "nn", "mix0_in")
    while not small_weights:
        backlog.poll(block=True)
    pool_w_full, conv_w_full, sc_conv_w_full = small_weights
    pooled, y_pool = run(_pool_fwd, 23.0, z0, pool_w_full, pool_scale, "pool_fwd")
    a_conv, c_conv = run(_conv_fwd, 25.0, z0, conv_w_full, conv_b, DP, "conv_fwd")
    y_conv = run(_ln_silu_fwd, 10.0, c_conv, conv_ln_g, conv_ln_b, "ln_silu_fwd")
    y0 = jnp.concatenate([y_pool, y_conv], axis=1)
    m0 = run(_matmul, 25.0, y0, gather.ready(1), "nn", "mix0_out")
    x1, h1 = run(_residual_norm, 21.0, x0, m0, gain(mix_post_g, 0), gain(ffn_pre_g, 0), "res_mix0")
    a0, a0sq = run(_matmul, 81.0, h1, gather.ready(2), "nn", "ffn0_up", out_dtypes=(BF16, BF16), epilogue=relu_sq)
    f0 = run(_matmul, 84.0, a0sq, gather.ready(3), "nn", "ffn0_down", tk=LONG_K_TILE)
    x2, h2 = run(_residual_norm, 22.0, x1, f0, gain(ffn_post_g, 0), gain(mix_pre_g, 1), "res_ffn0")
    z1 = run(_matmul, 62.0, h2, gather.ready(4), "nn", "mix1_in")
    y1 = run(_short_fwd, 22.0, z1, sc_conv_w_full, "short_fwd")
    m1 = run(_matmul, 25.0, y1, gather.ready(5), "nn", "mix1_out")
    x3, h3 = run(_residual_norm, 21.0, x2, m1, gain(mix_post_g, 1), gain(ffn_pre_g, 1), "res_mix1")
    a1, a1sq = run(_matmul, 81.0, h3, gather.ready(6), "nn", "ffn1_up", out_dtypes=(BF16, BF16), epilogue=relu_sq)
    f1 = run(_matmul, 84.0, a1sq, gather.ready(7), "nn", "ffn1_down", tk=LONG_K_TILE)
    w_in0, w_out0, w1_0, w2_0, w_in1, w_out1, w1_1, w2_1 = W

    grads_big = [None] * len(big)


    def reduce_grad(u, g):
        name, _, _, kind, cost = big[u]
        _GradStream(backlog, u, name, kind, cost, g, place, grads_big)

    dx, df1, d_ffn_post_1, loss_row = run(_loss_and_last_norm_bwd, 30.0, x3, f1, gain(ffn_post_g, 1), target, "loss")
    reduce_grad(7, run(_matmul, 80.0, a1sq, df1, "tn", "ffn1_down_dw", out_dtypes=(BF16,)))
    dz = run(_matmul, 82.0, df1, w2_1, "nt", "ffn1_down_dx", out_dtypes=(BF16,), epilogue=relu_sq_bwd, epi=(a1,))
    reduce_grad(6, run(_matmul, 80.0, h3, dz, "tn", "ffn1_up_dw", out_dtypes=(BF16,)))
    dh = run(_matmul, 87.0, dz, w1_1, "nt", "ffn1_up_dx", tk=LONG_K_TILE)
    dx, d_ffn_pre_1, dm1, d_mix_post_1 = run(_norms_bwd, 36.0, dx, dh, x3, gain(ffn_pre_g, 1), m1, gain(mix_post_g, 1), "norms_bwd3")

    reduce_grad(5, run(_matmul, 24.0, y1, dm1, "tn", "mix1_out_dw", out_dtypes=(BF16,)))
    dy1 = run(_matmul, 25.0, dm1, w_out1, "nt", "mix1_out_dx")
    db, dcg, du, d_sc_conv_w = run(_short_bwd, 41.0, dy1, z1, sc_conv_w_full, "short_bwd")
    dz1 = jnp.concatenate([db, dcg, du], axis=1)
    reduce_grad(4, run(_matmul, 62.0, h2, dz1, "tn", "mix1_in_dw", out_dtypes=(BF16,)))
    dh = run(_matmul, 68.0, dz1, w_in1, "nt", "mix1_in_dx")
    dx, d_mix_pre_1, df0, d_ffn_post_0 = run(_norms_bwd, 35.0, dx, dh, x2, gain(mix_pre_g, 1), f0, gain(ffn_post_g, 0), "norms_bwd2")

    reduce_grad(3, run(_matmul, 80.0, a0sq, df0, "tn", "ffn0_down_dw", out_dtypes=(BF16,)))
    dz = run(_matmul, 82.0, df0, w2_0, "nt", "ffn0_down_dx", out_dtypes=(BF16,), epilogue=relu_sq_bwd, epi=(a0,))
    reduce_grad(2, run(_matmul, 80.0, h1, dz, "tn", "ffn0_up_dw", out_dtypes=(BF16,)))
    dh = run(_matmul, 87.0, dz, w1_0, "nt", "ffn0_up_dx", tk=LONG_K_TILE)
    dx, d_ffn_pre_0, dm0, d_mix_post_0 = run(_norms_bwd, 36.0, dx, dh, x1, gain(ffn_pre_g, 0), m0, gain(mix_post_g, 0), "norms_bwd1")

    reduce_grad(1, run(_matmul, 24.0, y0, dm0, "tn", "mix0_out_dw", out_dtypes=(BF16,)))
    dy0 = run(_matmul, 25.0, dm0, w_out0, "nt", "mix0_out_dx")
    du_pool, d_pool_w, d_pool_scale = run(_pool_bwd, 28.0, dy0, pooled, pool_w_full, pool_scale, "pool_bwd")
    dc, d_ln_g, d_ln_b = run(_ln_silu_bwd, 15.0, dy0, c_conv, conv_ln_g, conv_ln_b, "ln_silu_bwd")
    dv, dgate, d_conv_w, d_conv_b = run(_conv_bwd, 52.0, dc, a_conv, z0, conv_w_full, DP, "conv_bwd")
    dz0 = jnp.concatenate([du_pool, dv, dgate], axis=1)

    small_sums = {}

    def exchange_small(key, arrays, cost):
        slots = _in_own_slot(_pack(arrays, D), place, "small_grads_slot_" + key)
        backlog.start(_small_exchange_job(slots), "small_grads_" + key, "ici", cost,
                      lambda outs, kept: small_sums.__setitem__(key, _unpack(_sum_slots(outs[0], "small_grads_sum_" + key),
                                                                             [a.shape for a in arrays])))

    exchange_small("most", [d_mix_pre_1, jnp.concatenate([d_mix_post_0, d_mix_post_1], 0),
                            jnp.concatenate([d_ffn_pre_0, d_ffn_pre_1], 0), jnp.concatenate([d_ffn_post_0, d_ffn_post_1], 0),
                            d_pool_scale, d_conv_b, d_ln_g, d_ln_b, d_pool_w, d_conv_w, d_sc_conv_w], 112.0)
    reduce_grad(0, run(_matmul, 34.0, h0, dz0, "tn", "mix0_in_dw", out_dtypes=(BF16,)))
    dh = run(_matmul, 40.0, dz0, w_in0, "nt", "mix0_in_dx")
    grad_x, d_mix_pre_0 = run(_norms_bwd, 26.0, dx, dh, x0, gain(mix_pre_g, 0), None, None, "norms_bwd0")
    exchange_small("last", [d_mix_pre_0, loss_row], 5.0)

    upd, gr, first = {}, grads_big, {}

    def keep(where, key, outs):
        where[key] = outs
        return outs

    adamw_big = [
        (7, lambda: keep(first, "ffn_w2", _adamw(ffn_w2, gr[7], m_ffn_w2, v_ffn_w2, "adamw_ffn_w2_1", layer=1)), 46.0),
        (6, lambda: keep(first, "ffn_w1", _adamw(ffn_w1, gr[6], m_ffn_w1, v_ffn_w1, "adamw_ffn_w1_1", layer=1)), 46.0),
        (5, lambda: keep(upd, "sc_w_out", _adamw(sc_w_out, gr[5], m_sc_w_out, v_sc_w_out, "adamw_sc_w_out")), 14.0),
        (4, lambda: keep(upd, "sc_w_in", _adamw(sc_w_in, gr[4], m_sc_w_in, v_sc_w_in, "adamw_sc_w_in")), 35.0),
        (3, lambda: keep(upd, "ffn_w2", _adamw(ffn_w2, gr[3], m_ffn_w2, v_ffn_w2, "adamw_ffn_w2_0", layer=0,
                                               carried=first["ffn_w2"])), 46.0),
        (2, lambda: keep(upd, "ffn_w1", _adamw(ffn_w1, gr[2], m_ffn_w1, v_ffn_w1, "adamw_ffn_w1_0", layer=0,
                                               carried=first["ffn_w1"])), 46.0),
        (1, lambda: keep(upd, "ab_w_out", _adamw(ab_w_out, gr[1], m_ab_w_out, v_ab_w_out, "adamw_ab_w_out")), 14.0),
        (0, lambda: keep(upd, "ab_w_in", _adamw(ab_w_in, gr[0], m_ab_w_in, v_ab_w_in, "adamw_ab_w_in")), 19.0)]
    while adamw_big or backlog.flights:
        due = [a for a in adamw_big if gr[a[0]] is not None]
        if due:
            adamw_big.remove(due[0])
            backlog.run(due[0][1], due[0][2])
        else:
            backlog.poll(block=True)

    (g_mix_pre_1, g_mix_post, g_ffn_pre, g_ffn_post, g_pool_scale, g_conv_b, g_ln_g, g_ln_b, g_pool_w_full, g_conv_w_full,
     g_sc_conv_w_full) = small_sums["most"]
    g_mix_pre = jnp.concatenate([small_sums["last"][0], g_mix_pre_1], 0)
    loss = small_sums["last"][1][0, 0]
    own = lambda a, ax: lax.dynamic_slice_in_dim(a, chip * (a.shape[ax] // N_CHIPS), a.shape[ax] // N_CHIPS, axis=ax)
    g_pool_w, g_conv_w, g_sc_conv_w = own(g_pool_w_full, 1), own(g_conv_w_full, 1), own(g_sc_conv_w_full, 1)

    def small_update(w, g, m, v, name):
        shape = w.shape
        as3 = lambda a: a.reshape((1, -1, shape[-1]))
        outs = _adamw(as3(w), g.reshape((-1, shape[-1])), as3(m), as3(v), "adamw_" + name)
        return [o.reshape(shape) for o in outs]

    upd["mix_pre_g"] = small_update(mix_pre_g, g_mix_pre, m_mix_pre_g, v_mix_pre_g, "mix_pre_g")
    upd["mix_post_g"] = small_update(mix_post_g, g_mix_post, m_mix_post_g, v_mix_post_g, "mix_post_g")
    upd["ffn_pre_g"] = small_update(ffn_pre_g, g_ffn_pre, m_ffn_pre_g, v_ffn_pre_g, "ffn_pre_g")
    upd["ffn_post_g"] = small_update(ffn_post_g, g_ffn_post, m_ffn_post_g, v_ffn_post_g, "ffn_post_g")
    upd["pool_w"] = small_update(pool_w, g_pool_w, m_pool_w, v_pool_w, "pool_w")
    upd["pool_scale"] = small_update(pool_scale, g_pool_scale, m_pool_scale, v_pool_scale, "pool_scale")
    upd["conv_w"] = small_update(conv_w, g_conv_w, m_conv_w, v_conv_w, "conv_w")
    upd["conv_b"] = small_update(conv_b, g_conv_b, m_conv_b, v_conv_b, "conv_b")
    upd["conv_ln_g"] = small_update(conv_ln_g, g_ln_g, m_conv_ln_g, v_conv_ln_g, "conv_ln_g")
    upd["conv_ln_b"] = small_update(conv_ln_b, g_ln_b, m_conv_ln_b, v_conv_ln_b, "conv_ln_b")
    upd["sc_conv_w"] = small_update(sc_conv_w, g_sc_conv_w, m_sc_conv_w, v_sc_conv_w, "sc_conv_w")

    order = ["mix_pre_g", "mix_post_g", "ffn_pre_g", "ffn_post_g", "ab_w_in", "pool_w", "pool_scale", "conv_w", "conv_b",
             "conv_ln_g", "conv_ln_b", "ab_w_out", "sc_w_in", "sc_conv_w", "sc_w_out", "ffn_w1", "ffn_w2"]
    out = [loss, grad_x[None]]
    for part in range(4):
        out += [upd[n][part] for n in order]
    return tuple(out)
```

```python
import jax
import jax.numpy as jnp
from jax import lax
from jax.experimental import pallas as pl
from jax.experimental.pallas import tpu as pltpu

F32, BF16 = jnp.float32, jnp.bfloat16
EPS = 1e-6
N_GROUPS = 4
MAX_WINDOW = 16
CONV_K = 31
SHORT_K = 3
CONV_PAD = 32
SHORT_PAD = 8
ADAM_LR, ADAM_B1, ADAM_B2, ADAM_EPS, ADAM_WD, ADAM_STEP = 0.001, 0.9, 0.999, 1e-08, 0.01, 10
N_CHIPS = 4
VMEM_LIMIT_BYTES = 56 * 1024 * 1024
ROW_TILE = 256
CHUNK = 256
LONG_K_TILE = 4096
MESH = pl.DeviceIdType.MESH
HBM = pl.BlockSpec(memory_space=pltpu.HBM)
SEM = pl.BlockSpec(memory_space=pltpu.SEMAPHORE)
ANY = pl.BlockSpec(memory_space=pl.ANY)


def _cp(*sem):
    return pltpu.CompilerParams(dimension_semantics=sem, vmem_limit_bytes=VMEM_LIMIT_BYTES)


def _sigmoid(v):
    return 1.0 / (1.0 + jnp.exp(-v))


class _Behind:
    pending = []


def _call(body, prefetch=None, **kw):
    behind, _Behind.pending = _Behind.pending, []
    single = not isinstance(kw["out_shape"], (list, tuple))
    in_specs, scratch = list(kw["in_specs"]), list(kw.get("scratch_shapes", ()))
    out_shape = [kw["out_shape"]] if single else list(kw["out_shape"])
    out_specs = [kw["out_specs"]] if single else list(kw["out_specs"])
    n_pre = 0 if prefetch is None else 1
    n_own, n_behind = len(in_specs), len(behind)

    def wrapped(*refs):
        body(*refs[:n_pre + n_own], *refs[n_pre + n_own + n_behind:])

    specs = dict(grid=kw["grid"], in_specs=in_specs + [ANY] * n_behind, out_specs=out_specs)
    if prefetch is None:
        specs["scratch_shapes"] = scratch
    else:
        specs = dict(grid_spec=pltpu.PrefetchScalarGridSpec(num_scalar_prefetch=1, scratch_shapes=scratch, **specs))
    aliases = {n_pre + i: o for i, o in kw.get("input_output_aliases", {}).items()}
    call = pl.pallas_call(wrapped, name=kw["name"], out_shape=out_shape, input_output_aliases=aliases,
                          compiler_params=_cp(*kw["sem"]), **specs)

    def run(*args):
        outs = call(*([prefetch] * n_pre), *args, *behind)
        return outs[0] if single else list(outs)

    return run


_DIMS = {"nn": (((1,), (0,)), ((), ())), "nt": (((1,), (1,)), ((), ())), "tn": (((0,), (0,)), ((), ()))}


def _pick(n, cap, step=256):
    if n <= cap:
        return n
    return next(t for t in range(cap - cap % step, 0, -step) if n % t == 0)


def _matmul(a, b, mode, name, out_dtypes=(F32,), epilogue=None, epi=(), tm=1024, tn=1024, tk=2048):
    if mode == "tn":
        (K, M), (K2, N) = a.shape, b.shape
    elif mode == "nt":
        (M, K), (N, K2) = a.shape, b.shape
    else:
        (M, K), (K2, N) = a.shape, b.shape
    assert K == K2
    tm, tn, tk = _pick(M, tm), _pick(N, tn), _pick(K, tk)
    nk = K // tk
    a_spec = pl.BlockSpec((tk, tm), lambda i, j, k: (k, i)) if mode == "tn" else pl.BlockSpec((tm, tk), lambda i, j, k: (i, k))
    b_spec = pl.BlockSpec((tn, tk), lambda i, j, k: (j, k)) if mode == "nt" else pl.BlockSpec((tk, tn), lambda i, j, k: (k, j))
    o_spec = pl.BlockSpec((tm, tn), lambda i, j, k: (i, j))
    n_epi, n_out = len(epi), len(out_dtypes)

    def body(a_ref, b_ref, *rest):
        epi_refs, out_refs, scratch = rest[:n_epi], rest[n_epi:n_epi + n_out], rest[n_epi + n_out:]
        part = lax.dot_general(a_ref[...].astype(BF16), b_ref[...].astype(BF16), _DIMS[mode], preferred_element_type=F32)

        def finish(acc):
            outs = epilogue(acc, *[r[...] for r in epi_refs]) if epilogue else (acc,)
            for o_ref, o in zip(out_refs, outs):
                o_ref[...] = o.astype(o_ref.dtype)

        if nk == 1:
            finish(part)
        else:
            acc_ref = scratch[0]
            k = pl.program_id(2)

            @pl.when(k == 0)
            def _():
                acc_ref[...] = part

            @pl.when(k > 0)
            def _():
                acc_ref[...] += part

            @pl.when(k == nk - 1)
            def _():
                finish(acc_ref[...])

    outs = _call(
        body, name=name, grid=(M // tm, N // tn, nk),
        in_specs=[a_spec, b_spec] + [o_spec] * n_epi, out_specs=[o_spec] * n_out,
        out_shape=[jax.ShapeDtypeStruct((M, N), dt) for dt in out_dtypes],
        scratch_shapes=[pltpu.VMEM((tm, tn), F32)] if nk > 1 else [],
        sem=("parallel", "parallel", "arbitrary"))(a, b, *epi)
    return outs[0] if n_out == 1 else outs


def _rms(x, g):
    r = lax.rsqrt(jnp.mean(x * x, axis=-1, keepdims=True) + EPS)
    return x * r * g


def _rms_bwd(dy, x, g):
    r = lax.rsqrt(jnp.mean(x * x, axis=-1, keepdims=True) + EPS)
    xn = x * r
    dyg = dy * g
    dx = r * (dyg - xn * jnp.mean(dyg * xn, axis=-1, keepdims=True))
    return dx, jnp.sum(dy * xn, axis=0, keepdims=True)


def _rows(d, tr=ROW_TILE):
    return pl.BlockSpec((tr, d), lambda i: (i, 0))


def _vec(d):
    return pl.BlockSpec((1, d), lambda i: (0, 0))


def _accumulate(ref, val):
    @pl.when(pl.program_id(0) == 0)
    def _():
        ref[...] = val

    @pl.when(pl.program_id(0) > 0)
    def _():
        ref[...] += val


def _norm_fwd(x, g, name):
    T, D = x.shape

    def body(x_ref, g_ref, h_ref):
        h_ref[...] = _rms(x_ref[...], g_ref[...]).astype(BF16)

    return _call(body,name=name,grid=(T // ROW_TILE,), in_specs=[_rows(D), _vec(D)], out_specs=_rows(D),
                          out_shape=jax.ShapeDtypeStruct((T, D), BF16), sem=("parallel",))(x, g)


def _residual_norm(x, m, g_post, g_next, name):
    T, D = x.shape

    def body(x_ref, m_ref, gp_ref, gn_ref, xo_ref, h_ref):
        xo = x_ref[...] + _rms(m_ref[...], gp_ref[...])
        xo_ref[...] = xo
        h_ref[...] = _rms(xo, gn_ref[...]).astype(BF16)

    return _call(body,name=name,grid=(T // ROW_TILE,), in_specs=[_rows(D), _rows(D), _vec(D), _vec(D)],
                          out_specs=[_rows(D), _rows(D)],
                          out_shape=[jax.ShapeDtypeStruct((T, D), F32), jax.ShapeDtypeStruct((T, D), BF16)],
                          sem=("parallel",))(x, m, g_post, g_next)


def _loss_and_last_norm_bwd(x, m, g_post, target, name):
    T, D = x.shape

    def body(x_ref, m_ref, gp_ref, t_ref, dx_ref, dm_ref, dg_ref, loss_ref):
        m_val, gp = m_ref[...], gp_ref[...]
        err = x_ref[...] + _rms(m_val, gp) - t_ref[...]
        dx = err * (1.0 / D)
        dx_ref[...] = dx
        dm, dg = _rms_bwd(dx, m_val, gp)
        dm_ref[...] = dm.astype(BF16)
        _accumulate(dg_ref, dg)
        _accumulate(loss_ref, jnp.full((1, 128), 0.5 * jnp.sum(err * err) * (1.0 / D), F32))

    return _call(body,name=name,grid=(T // ROW_TILE,), in_specs=[_rows(D), _rows(D), _vec(D), _rows(D)],
                          out_specs=[_rows(D), _rows(D), _vec(D), _vec(128)],
                          out_shape=[jax.ShapeDtypeStruct((T, D), F32), jax.ShapeDtypeStruct((T, D), BF16),
                                     jax.ShapeDtypeStruct((1, D), F32), jax.ShapeDtypeStruct((1, 128), F32)],
                          sem=("arbitrary",))(x, m, g_post, target)


def _norms_bwd(dx, dh, x_in, g_pre, m_prev, g_post_prev, name):
    T, D = dx.shape
    with_prev = m_prev is not None

    def body(*refs):
        if with_prev:
            dx_ref, dh_ref, x_ref, gq_ref, m_ref, gp_ref, dxo_ref, dgq_ref, dm_ref, dgp_ref = refs
        else:
            dx_ref, dh_ref, x_ref, gq_ref, dxo_ref, dgq_ref = refs
        d_in, dgq = _rms_bwd(dh_ref[...], x_ref[...], gq_ref[...])
        dxo = dx_ref[...] + d_in
        dxo_ref[...] = dxo
        _accumulate(dgq_ref, dgq)
        if with_prev:
            dm, dgp = _rms_bwd(dxo, m_ref[...], gp_ref[...])
            dm_ref[...] = dm.astype(BF16)
            _accumulate(dgp_ref, dgp)

    ins, in_specs = [dx, dh, x_in, g_pre], [_rows(D), _rows(D), _rows(D), _vec(D)]
    out_specs = [_rows(D), _vec(D)]
    out_shape = [jax.ShapeDtypeStruct((T, D), F32), jax.ShapeDtypeStruct((1, D), F32)]
    if with_prev:
        ins += [m_prev, g_post_prev]
        in_specs += [_rows(D), _vec(D)]
        out_specs += [_rows(D), _vec(D)]
        out_shape += [jax.ShapeDtypeStruct((T, D), BF16), jax.ShapeDtypeStruct((1, D), F32)]
    return _call(body,name=name,grid=(T // ROW_TILE,), in_specs=in_specs, out_specs=out_specs, out_shape=out_shape,
                          sem=("arbitrary",))(*ins)


def _window_weights(g):
    w = 2 << g
    return w, [jnp.where(j < w, 1.0, 0.0).astype(F32) for j in range(MAX_WINDOW)]


def _valid_count(r0, rows, w):
    t = (lax.broadcasted_iota(jnp.int32, (rows, 1), 0) + (r0 + 1)).astype(F32)
    return jnp.minimum(t, w.astype(F32))


def _pool_fwd(z, pool_w, pool_scale, name):
    T = z.shape[0]
    PG = pool_w.shape[-1]
    DP = N_GROUPS * PG
    rc = min(CHUNK, T)

    def body(u_ref, pw_ref, sc_ref, pooled_ref, y_ref, pad):
        w, wts = _window_weights(pl.program_id(0))
        pad[pl.ds(0, MAX_WINDOW), :] = jnp.zeros((MAX_WINDOW, PG), F32)
        pad[pl.ds(MAX_WINDOW, T), :] = u_ref[...]
        for r0 in range(0, T, rc):
            acc = jnp.zeros((rc, PG), F32)
            for j in range(MAX_WINDOW):
                acc = acc + wts[j] * pad[pl.ds(MAX_WINDOW + r0 - j, rc), :]
            pooled = acc / _valid_count(r0, rc, w) - u_ref[pl.ds(r0, rc), :]
            pooled_ref[pl.ds(r0, rc), :] = pooled.astype(BF16)
        mixed = jnp.dot(pooled_ref[...], pw_ref[...].astype(BF16), preferred_element_type=F32)
        y_ref[...] = (mixed * sc_ref[...]).astype(BF16)

    col = lambda g: (0, g)
    return _call(
        body, name=name,grid=(N_GROUPS,),
        in_specs=[pl.BlockSpec((T, PG), col), pl.BlockSpec((None, PG, PG), lambda g: (g, 0, 0)), pl.BlockSpec((1, PG), col)],
        out_specs=[pl.BlockSpec((T, PG), col), pl.BlockSpec((T, PG), col)],
        out_shape=[jax.ShapeDtypeStruct((T, DP), BF16), jax.ShapeDtypeStruct((T, DP), BF16)],
        scratch_shapes=[pltpu.VMEM((T + MAX_WINDOW, PG), F32)], sem=("parallel",))(z, pool_w, pool_scale)


def _pool_bwd(dy, pooled, pool_w, pool_scale, name):
    T = dy.shape[0]
    PG = pool_w.shape[-1]
    DP = N_GROUPS * PG
    rc = min(CHUNK, T)

    def body(dy_ref, pooled_ref, pw_ref, sc_ref, du_ref, dpw_ref, dsc_ref, pad, dp_ref):
        w, wts = _window_weights(pl.program_id(0))
        pooled_v, pw = pooled_ref[...], pw_ref[...].astype(BF16)
        dy_v = dy_ref[...]
        mixed = jnp.dot(pooled_v, pw, preferred_element_type=F32)
        dsc_ref[...] = jnp.sum(dy_v * mixed, axis=0, keepdims=True)
        dmixed = (dy_v * sc_ref[...]).astype(BF16)
        dpw_ref[...] = lax.dot_general(pooled_v, dmixed, _DIMS["tn"], preferred_element_type=F32)
        dp_ref[...] = lax.dot_general(dmixed, pw, _DIMS["nt"], preferred_element_type=F32)
        pad[pl.ds(T, MAX_WINDOW), :] = jnp.zeros((MAX_WINDOW, PG), F32)
        for r0 in range(0, T, rc):
            pad[pl.ds(r0, rc), :] = dp_ref[pl.ds(r0, rc), :] / _valid_count(r0, rc, w)
        for r0 in range(0, T, rc):
            acc = jnp.zeros((rc, PG), F32)
            for j in range(MAX_WINDOW):
                acc = acc + wts[j] * pad[pl.ds(r0 + j, rc), :]
            du_ref[pl.ds(r0, rc), :] = (acc - dp_ref[pl.ds(r0, rc), :]).astype(BF16)

    col = lambda g: (0, g)
    return _call(
        body, name=name,grid=(N_GROUPS,),
        in_specs=[pl.BlockSpec((T, PG), col), pl.BlockSpec((T, PG), col), pl.BlockSpec((None, PG, PG), lambda g: (g, 0, 0)),
                  pl.BlockSpec((1, PG), col)],
        out_specs=[pl.BlockSpec((T, PG), col), pl.BlockSpec((None, PG, PG), lambda g: (g, 0, 0)), pl.BlockSpec((1, PG), col)],
        out_shape=[jax.ShapeDtypeStruct((T, DP), BF16), jax.ShapeDtypeStruct((N_GROUPS, PG, PG), F32),
                   jax.ShapeDtypeStruct((1, DP), F32)],
        scratch_shapes=[pltpu.VMEM((T + MAX_WINDOW, PG), F32), pltpu.VMEM((T, PG), F32)],
        sem=("parallel",))(dy, pooled, pool_w, pool_scale)


def _conv_fwd(z, conv_w, conv_b, d_pool, name, tc=128):
    T = z.shape[0]
    DC = conv_w.shape[-1]
    rc = min(CHUNK, T)
    v0, g0 = d_pool // tc, (d_pool + DC) // tc

    def body(v_ref, gt_ref, w_ref, b_ref, a_ref, c_ref, pad):
        pad[pl.ds(0, CONV_PAD), :] = jnp.zeros((CONV_PAD, tc), F32)
        for r0 in range(0, T, rc):
            a = v_ref[pl.ds(r0, rc), :] * _sigmoid(gt_ref[pl.ds(r0, rc), :])
            a_ref[pl.ds(r0, rc), :] = a
            pad[pl.ds(CONV_PAD + r0, rc), :] = a
        for r0 in range(0, T, rc):
            acc = jnp.zeros((rc, tc), F32) + b_ref[...]
            for k in range(CONV_K):
                acc = acc + w_ref[pl.ds(k, 1), :] * pad[pl.ds(CONV_PAD - (CONV_K - 1) + k + r0, rc), :]
            c_ref[pl.ds(r0, rc), :] = acc

    col = lambda j: (0, j)
    return _call(
        body, name=name,grid=(DC // tc,),
        in_specs=[pl.BlockSpec((T, tc), lambda j: (0, v0 + j)), pl.BlockSpec((T, tc), lambda j: (0, g0 + j)),
                  pl.BlockSpec((CONV_K, tc), col), pl.BlockSpec((1, tc), col)],
        out_specs=[pl.BlockSpec((T, tc), col), pl.BlockSpec((T, tc), col)],
        out_shape=[jax.ShapeDtypeStruct((T, DC), F32), jax.ShapeDtypeStruct((T, DC), F32)],
        scratch_shapes=[pltpu.VMEM((T + CONV_PAD, tc), F32)], sem=("parallel",))(z, z, conv_w, conv_b)


def _conv_bwd(dc, a, z, conv_w, d_pool, name, tc=128):
    T, DC = dc.shape
    rc = min(CHUNK, T)
    v0, g0 = d_pool // tc, (d_pool + DC) // tc

    def body(dc_ref, a_ref, v_ref, gt_ref, w_ref, dv_ref, dg_ref, dw_ref, db_ref, apad, dpad):
        apad[pl.ds(0, CONV_PAD), :] = jnp.zeros((CONV_PAD, tc), F32)
        apad[pl.ds(CONV_PAD, T), :] = a_ref[...]
        dpad[pl.ds(0, T), :] = dc_ref[...]
        dpad[pl.ds(T, CONV_PAD), :] = jnp.zeros((CONV_PAD, tc), F32)
        db_ref[...] = jnp.sum(dc_ref[...], axis=0, keepdims=True)
        for k in range(CONV_K):
            acc = jnp.zeros((8, tc), F32)
            for r0 in range(0, T, rc):
                prod = dc_ref[pl.ds(r0, rc), :] * apad[pl.ds(CONV_PAD - (CONV_K - 1) + k + r0, rc), :]
                acc = acc + jnp.sum(prod.reshape(rc // 8, 8, tc), axis=0)
            dw_ref[pl.ds(k, 1), :] = jnp.sum(acc, axis=0, keepdims=True)
        for r0 in range(0, T, rc):
            da = jnp.zeros((rc, tc), F32)
            for k in range(CONV_K):
                da = da + w_ref[pl.ds(k, 1), :] * dpad[pl.ds(r0 + (CONV_K - 1) - k, rc), :]
            sig = _sigmoid(gt_ref[pl.ds(r0, rc), :])
            dv_ref[pl.ds(r0, rc), :] = (da * sig).astype(BF16)
            dg_ref[pl.ds(r0, rc), :] = (da * v_ref[pl.ds(r0, rc), :] * sig * (1.0 - sig)).astype(BF16)

    col = lambda j: (0, j)
    return _call(
        body, name=name,grid=(DC // tc,),
        in_specs=[pl.BlockSpec((T, tc), col), pl.BlockSpec((T, tc), col), pl.BlockSpec((T, tc), lambda j: (0, v0 + j)),
                  pl.BlockSpec((T, tc), lambda j: (0, g0 + j)), pl.BlockSpec((CONV_K, tc), col)],
        out_specs=[pl.BlockSpec((T, tc), col), pl.BlockSpec((T, tc), col), pl.BlockSpec((CONV_K, tc), col),
                   pl.BlockSpec((1, tc), col)],
        out_shape=[jax.ShapeDtypeStruct((T, DC), BF16), jax.ShapeDtypeStruct((T, DC), BF16),
                   jax.ShapeDtypeStruct((CONV_K, DC), F32), jax.ShapeDtypeStruct((1, DC), F32)],
        scratch_shapes=[pltpu.VMEM((T + CONV_PAD, tc), F32), pltpu.VMEM((T + CONV_PAD, tc), F32)],
        sem=("parallel",))(dc, a, z, z, conv_w)


def _layer_norm_parts(c, g, b):
    mu = jnp.mean(c, axis=-1, keepdims=True)
    xc = c - mu
    rstd = lax.rsqrt(jnp.mean(xc * xc, axis=-1, keepdims=True) + EPS)
    xhat = xc * rstd
    return xhat, rstd, xhat * g + b


def _ln_silu_fwd(c, g, b, name):
    T, DC = c.shape

    def body(c_ref, g_ref, b_ref, y_ref):
        _, _, ln = _layer_norm_parts(c_ref[...], g_ref[...], b_ref[...])
        y_ref[...] = (ln * _sigmoid(ln)).astype(BF16)

    return _call(body,name=name,grid=(T // ROW_TILE,), in_specs=[_rows(DC), _vec(DC), _vec(DC)], out_specs=_rows(DC),
                          out_shape=jax.ShapeDtypeStruct((T, DC), BF16), sem=("parallel",))(c, g, b)


def _ln_silu_bwd(dy, c, g, b, name):
    T, DC = c.shape

    def body(dy_ref, c_ref, g_ref, b_ref, dc_ref, dg_ref, db_ref):
        gain = g_ref[...]
        xhat, rstd, ln = _layer_norm_parts(c_ref[...], gain, b_ref[...])
        s = _sigmoid(ln)
        dln = dy_ref[...] * (s * (1.0 + ln * (1.0 - s)))
        _accumulate(dg_ref, jnp.sum(dln * xhat, axis=0, keepdims=True))
        _accumulate(db_ref, jnp.sum(dln, axis=0, keepdims=True))
        dxh = dln * gain
        dc_ref[...] = rstd * (dxh - jnp.mean(dxh, axis=-1, keepdims=True) - xhat * jnp.mean(dxh * xhat, axis=-1, keepdims=True))

    return _call(body,name=name,grid=(T // ROW_TILE,),
                          in_specs=[pl.BlockSpec((ROW_TILE, DC), lambda i: (i, 1)), _rows(DC), _vec(DC), _vec(DC)],
                          out_specs=[_rows(DC), _vec(DC), _vec(DC)],
                          out_shape=[jax.ShapeDtypeStruct((T, DC), F32), jax.ShapeDtypeStruct((1, DC), F32),
                                     jax.ShapeDtypeStruct((1, DC), F32)],
                          sem=("arbitrary",))(dy, c, g, b)


def _short_specs(T, DS, tc):
    n = DS // tc
    return [pl.BlockSpec((T, tc), lambda j: (0, j)), pl.BlockSpec((T, tc), lambda j: (0, n + j)),
            pl.BlockSpec((T, tc), lambda j: (0, 2 * n + j))]


def _short_fwd(z, w, name, tc=256):
    T = z.shape[0]
    DS = w.shape[-1]
    rc = min(CHUNK, T)

    def body(b_ref, cg_ref, u_ref, w_ref, y_ref, pad):
        pad[pl.ds(0, SHORT_PAD), :] = jnp.zeros((SHORT_PAD, tc), F32)
        pad[pl.ds(SHORT_PAD, T), :] = cg_ref[...] * u_ref[...]
        for r0 in range(0, T, rc):
            r = jnp.zeros((rc, tc), F32)
            for k in range(SHORT_K):
                r = r + w_ref[pl.ds(k, 1), :] * pad[pl.ds(SHORT_PAD - (SHORT_K - 1) + k + r0, rc), :]
            y_ref[pl.ds(r0, rc), :] = (b_ref[pl.ds(r0, rc), :] * r).astype(BF16)

    col = lambda j: (0, j)
    return _call(body,name=name,grid=(DS // tc,), in_specs=_short_specs(T, DS, tc) + [pl.BlockSpec((SHORT_K, tc), col)],
                          out_specs=pl.BlockSpec((T, tc), col), out_shape=jax.ShapeDtypeStruct((T, DS), BF16),
                          scratch_shapes=[pltpu.VMEM((T + SHORT_PAD, tc), F32)], sem=("parallel",))(z, z, z, w)


def _short_bwd(dy, z, w, name, tc=256):
    T, DS = dy.shape
    rc = min(CHUNK, T)

    def body(dy_ref, b_ref, cg_ref, u_ref, w_ref, db_ref, dcg_ref, du_ref, dw_ref, qpad, rpad):
        qpad[pl.ds(0, SHORT_PAD), :] = jnp.zeros((SHORT_PAD, tc), F32)
        qpad[pl.ds(SHORT_PAD, T), :] = cg_ref[...] * u_ref[...]
        rpad[pl.ds(0, T), :] = dy_ref[...] * b_ref[...]
        rpad[pl.ds(T, SHORT_PAD), :] = jnp.zeros((SHORT_PAD, tc), F32)
        accs = [jnp.zeros((8, tc), F32) for _ in range(SHORT_K)]
        for r0 in range(0, T, rc):
            r = jnp.zeros((rc, tc), F32)
            dq = jnp.zeros((rc, tc), F32)
            dr = rpad[pl.ds(r0, rc), :]
            for k in range(SHORT_K):
                q_k = qpad[pl.ds(SHORT_PAD - (SHORT_K - 1) + k + r0, rc), :]
                r = r + w_ref[pl.ds(k, 1), :] * q_k
                dq = dq + w_ref[pl.ds(k, 1), :] * rpad[pl.ds(r0 + (SHORT_K - 1) - k, rc), :]
                accs[k] = accs[k] + jnp.sum((dr * q_k).reshape(rc // 8, 8, tc), axis=0)
            db_ref[pl.ds(r0, rc), :] = (dy_ref[pl.ds(r0, rc), :] * r).astype(BF16)
            dcg_ref[pl.ds(r0, rc), :] = (dq * u_ref[pl.ds(r0, rc), :]).astype(BF16)
            du_ref[pl.ds(r0, rc), :] = (dq * cg_ref[pl.ds(r0, rc), :]).astype(BF16)
        for k in range(SHORT_K):
            dw_ref[pl.ds(k, 1), :] = jnp.sum(accs[k], axis=0, keepdims=True)

    col = lambda j: (0, j)
    tile = pl.BlockSpec((T, tc), col)
    return _call(body,name=name,grid=(DS // tc,),
                          in_specs=[tile] + _short_specs(T, DS, tc) + [pl.BlockSpec((SHORT_K, tc), col)],
                          out_specs=[tile, tile, tile, pl.BlockSpec((SHORT_K, tc), col)],
                          out_shape=[jax.ShapeDtypeStruct((T, DS), BF16)] * 3 + [jax.ShapeDtypeStruct((SHORT_K, DS), F32)],
                          scratch_shapes=[pltpu.VMEM((T + SHORT_PAD, tc), F32), pltpu.VMEM((T + SHORT_PAD, tc), F32)],
                          sem=("parallel",))(dy, z, z, z, w)


def _tile_rows(rows, cols, n_bufs):
    budget = VMEM_LIMIT_BYTES * 3 // 4 // (2 * n_bufs * 4 * cols)
    tr = rows
    while tr > budget and tr % 16 == 0:
        tr //= 2
    return tr


def _placed_call(body, name, place, grid, in_specs, out_specs, out_shape, ins):
    return _call(body,prefetch=place, name=name, grid=grid, in_specs=in_specs, out_specs=out_specs, out_shape=out_shape,
                 sem=("parallel",))(*ins)


def _cast_into_full(w, layer, kind, place, name):
    _, R, C = w.shape
    tr = _tile_rows(R, C, 2)
    nb = R // tr
    if kind == "col":
        full, out_spec = (R, C * N_CHIPS), pl.BlockSpec((tr, C), lambda i, s: (i, s[0]))
    else:
        full, out_spec = (R * N_CHIPS, C), pl.BlockSpec((tr, C), lambda i, s: (s[0] * nb + i, 0))

    def body(s_ref, w_ref, o_ref):
        o_ref[...] = w_ref[...].astype(BF16)

    return _placed_call(body, name, place, (nb,), [pl.BlockSpec((None, tr, C), lambda i, s: (layer, i, 0))], out_spec,
                        jax.ShapeDtypeStruct(full, BF16), [w])


def _add_pair(grad, theirs, kind, place, name):
    R, C = grad.shape
    piece_rows = R // 2 if kind == "col" else R // N_CHIPS // 2
    tr = _tile_rows(piece_rows, C, 3)
    nb = piece_rows // tr
    if kind == "col":
        g_spec = pl.BlockSpec((tr, C), lambda i, s: (s[1] * nb + i, 0))
    else:
        g_spec = pl.BlockSpec((tr, C), lambda i, s: ((2 * (i // nb) + s[1]) * nb + i % nb, 0))
    flat = pl.BlockSpec((tr, C), lambda i, s: (i, 0))

    def body(s_ref, a_ref, b_ref, o_ref):
        o_ref[...] = (a_ref[...].astype(F32) + b_ref[...].astype(F32)).astype(BF16)

    return _placed_call(body, name, place, (R // 2 // tr,), [g_spec, flat], flat, jax.ShapeDtypeStruct((R // 2, C), BF16),
                        [grad, theirs])


def _sum_chips(chip_sum, arrived, kind, place, name):
    _, H, W = arrived.shape
    tr = _tile_rows(H, W, 6)
    nb = H // tr
    if kind == "col":
        own_spec = pl.BlockSpec((tr, W), lambda i, s: (i, s[0]))
    else:
        own_spec = pl.BlockSpec((tr, W), lambda i, s: (s[0] * nb + i, 0))

    def body(s_ref, p_ref, r_ref, o_ref):
        acc = p_ref[...].astype(F32)
        for i in range(N_CHIPS - 1):
            acc = acc + r_ref[i].astype(F32)
        o_ref[...] = acc

    return _placed_call(body, name, place, (nb,), [own_spec, pl.BlockSpec((N_CHIPS - 1, tr, W), lambda i, s: (0, i, 0))],
                        pl.BlockSpec((tr, W), lambda i, s: (s[1] * nb + i, 0)), jax.ShapeDtypeStruct((2 * H, W), F32),
                        [chip_sum, arrived])


def _adamw_values(w, g, m, v):
    m = ADAM_B1 * m + (1.0 - ADAM_B1) * g
    v = ADAM_B2 * v + (1.0 - ADAM_B2) * (g * g)
    m_hat = m / (1.0 - ADAM_B1 ** ADAM_STEP)
    v_hat = v / (1.0 - ADAM_B2 ** ADAM_STEP)
    return -ADAM_LR * (m_hat / (jnp.sqrt(v_hat) + ADAM_EPS) + ADAM_WD * w), m, v


def _adamw(w, g, m, v, name, layer=0, carried=None):
    L, R, C = w.shape
    tr = _tile_rows(R, C, 8)

    def body(w_ref, g_ref, m_ref, v_ref, *rest):
        go_ref, d_ref, mo_ref, vo_ref = rest[-4:]
        g_val = g_ref[...]
        d, m_new, v_new = _adamw_values(w_ref[...], g_val, m_ref[...], v_ref[...])
        go_ref[...], d_ref[...], mo_ref[...], vo_ref[...] = g_val, d, m_new, v_new

    lay = pl.BlockSpec((None, tr, C), lambda i: (layer, i, 0))
    ins = [w, g, m, v]
    in_specs = [lay, pl.BlockSpec((tr, C), lambda i: (i, 0)), lay, lay]
    aliases = {}
    if carried is not None:
        ins += list(carried)
        in_specs += [pl.BlockSpec(memory_space=pl.ANY)] * 4
        aliases = {4 + i: i for i in range(4)}
    return _call(body, name=name, grid=(R // tr,), in_specs=in_specs, out_specs=[lay] * 4,
                 out_shape=[jax.ShapeDtypeStruct((L, R, C), F32)] * 4, input_output_aliases=aliases, sem=("parallel",))(*ins)


def _aligned(v, m):
    return v if isinstance(v, int) else pl.multiple_of(v, m)


def _place():
    x, y, c = lax.axis_index("x"), lax.axis_index("y"), lax.axis_index("c")
    other_chips = [(x, 1 - y), (1 - x, y), (1 - x, 1 - y)]
    return x, y, c, 2 * x + y, other_chips


def _chip_index(chip):
    return 2 * chip[0] + chip[1]


def _piece(ref, kind, k, h):
    R, C = ref.shape
    if kind == "col":
        return ref.at[pl.ds(_aligned(h * (R // 2), 16), R // 2), pl.ds(_aligned(k * (C // N_CHIPS), 128), C // N_CHIPS)]
    rs = R // N_CHIPS
    return ref.at[pl.ds(_aligned(k * rs + h * (rs // 2), 16), rs // 2), :]


def _compact_piece(ref, kind, k):
    R2, C = ref.shape
    if kind == "col":
        return ref.at[:, pl.ds(_aligned(k * (C // N_CHIPS), 128), C // N_CHIPS)]
    return ref.at[pl.ds(_aligned(k * (R2 // N_CHIPS), 16), R2 // N_CHIPS), :]


def _half_rows(ref, h):
    R = ref.shape[0]
    return ref.at[pl.ds(_aligned(h * (R // 2), 16), R // 2), :]


class _Copies:
    def __init__(self, send_sems, recv_sems):
        self.send_sems, self.recv_sems = send_sems, recv_sems
        self.n_remote = 0

    def remote(self, src, dst, device):
        k = self.n_remote
        self.n_remote += 1
        return pltpu.make_async_remote_copy(src_ref=src, dst_ref=dst, send_sem=self.send_sems.at[k], recv_sem=self.recv_sems.at[k],
                                            device_id=device, device_id_type=MESH)


class _Job:
    def __init__(self, ins, out_shape, aliases, n_remote, build):
        self.ins, self.out_shape, self.aliases, self.n_remote, self.build = list(ins), list(out_shape), dict(aliases), n_remote, build


class _Flying:
    def __init__(self, job, send_sems, recv_sems, bufs, token):
        self.job, self.send_sems, self.recv_sems, self.bufs, self.token = job, send_sems, recv_sems, bufs, token


def _job_refs(job, buf_refs):
    n_out = len(job.out_shape)
    kept = [i for i in range(len(job.ins)) if i not in job.aliases]
    ins = [buf_refs[job.aliases[i]] if i in job.aliases else buf_refs[n_out + kept.index(i)] for i in range(len(job.ins))]
    return ins, list(buf_refs[:n_out])


def _start_job(job, name, after=()):
    n_in, n_out, n_after = len(job.ins), len(job.out_shape), len(after)
    kept = [i for i in range(n_in) if i not in job.aliases]
    n_bufs = n_out + len(kept)

    def body(*refs):
        in_refs, out_refs = refs[:n_in], refs[n_in + n_after:n_in + n_after + n_out]
        send_sems, recv_sems, token = refs[n_in + n_after + n_bufs:]
        for d in job.build(in_refs, out_refs, _Copies(send_sems, recv_sems)):
            d.start()
        token[...] = jnp.zeros_like(token)

    aliases = dict(job.aliases)
    aliases.update({i: n_out + k for k, i in enumerate(kept)})
    sems = pltpu.SemaphoreType.DMA((job.n_remote,))
    outs = pl.pallas_call(
        body, name=name, in_specs=[HBM] * n_in + [ANY] * n_after,
        out_specs=[HBM] * n_bufs + [SEM, SEM, pl.BlockSpec(memory_space=pltpu.VMEM)],
        out_shape=job.out_shape + [jax.ShapeDtypeStruct(job.ins[i].shape, job.ins[i].dtype) for i in kept]
        + [sems, sems, jax.ShapeDtypeStruct((8, 128), F32)],
        input_output_aliases=aliases,
        compiler_params=pltpu.CompilerParams(has_side_effects=pltpu.SideEffectType.DATAFLOW_SIDE_EFFECTING))(*job.ins, *after)
    return _Flying(job, outs[n_bufs], outs[n_bufs + 1], list(outs[:n_bufs]), outs[n_bufs + 2])


def _wait_job(flying, name, after=()):
    job, n_bufs, n_after = flying.job, len(flying.bufs), len(after)

    def body(*refs):
        in_refs, out_refs = _job_refs(job, refs[:n_bufs])
        send_sems, recv_sems = refs[n_bufs:n_bufs + 2]
        copies = job.build(in_refs, out_refs, _Copies(send_sems, recv_sems))
        for d in copies:
            d.wait_send()
        for d in copies:
            d.wait_recv()

    outs = pl.pallas_call(
        body, name=name, in_specs=[HBM] * n_bufs + [SEM, SEM] + [ANY] * n_after, out_specs=[HBM] * n_bufs,
        out_shape=[jax.ShapeDtypeStruct(b.shape, b.dtype) for b in flying.bufs],
        input_output_aliases={i: i for i in range(n_bufs)},
        compiler_params=pltpu.CompilerParams(has_side_effects=pltpu.SideEffectType.DATAFLOW_SIDE_EFFECTING))(
            *flying.bufs, flying.send_sems, flying.recv_sems, *after)
    return list(outs[:len(job.out_shape)]), list(outs[len(job.out_shape):])


def _in_place(arrays):
    return [jax.ShapeDtypeStruct(a.shape, a.dtype) for a in arrays], {u: u for u in range(len(arrays))}


def _rows_part(ref, part, n_parts):
    h = ref.shape[0] // n_parts
    return ref.at[pl.ds(part * h, h), :]


def _gather_job(full, kind, stage):
    def build(in_refs, out_refs, cp):
        x, y, c, me, (y_nbr, x_nbr, diagonal) = _place()
        (ref,) = out_refs
        sibling = (x, y, 1 - c)
        if stage == 1:
            mine = _piece(ref, kind, me, c)
            return [cp.remote(mine, mine, (*y_nbr, c)), cp.remote(mine, mine, (*x_nbr, c))]
        from_y, from_x = _piece(ref, kind, _chip_index(y_nbr), c), _piece(ref, kind, _chip_index(x_nbr), c)
        if stage == 2:
            relay_0, relay_1 = _rows_part(from_x, 0, 2), _rows_part(from_y, 1, 2)
            return [cp.remote(relay_0, relay_0, (*y_nbr, c)), cp.remote(relay_1, relay_1, (*x_nbr, c)),
                    cp.remote(from_y, from_y, sibling), cp.remote(from_x, from_x, sibling)]
        from_diagonal = _piece(ref, kind, _chip_index(diagonal), c)
        return [cp.remote(from_diagonal, from_diagonal, sibling)]

    return _Job([full], *_in_place([full]), {1: 2, 2: 4, 3: 1}[stage], build)


def _gather_small_job(fulls, axes):
    def build(in_refs, out_refs, cp):
        x, y, c, me, chips = _place()
        copies = []
        for ref, ax in zip(out_refs, axes):
            n = ref.shape[ax] // N_CHIPS
            idx = [slice(None)] * len(ref.shape)
            idx[ax] = pl.ds(_aligned(me * n, n), n)
            mine = ref.at[tuple(idx)]
            copies += [cp.remote(mine, mine, (*chip, c)) for chip in chips]
        return copies

    return _Job(fulls, *_in_place(fulls), 3 * len(fulls), build)


def _exchange_halves_job(grads, kinds):
    def build(in_refs, out_refs, cp):
        x, y, c, me, chips = _place()
        copies = []
        for src, dst, kind in zip(in_refs, out_refs, kinds):
            if kind == "col":
                copies.append(cp.remote(_half_rows(src, 1 - c), dst, (x, y, 1 - c)))
            else:
                copies += [cp.remote(_piece(src, "row", k, 1 - c), _compact_piece(dst, "row", k), (x, y, 1 - c))
                           for k in range(N_CHIPS)]
        return copies

    out_shape = [jax.ShapeDtypeStruct((g.shape[0] // 2, g.shape[1]), g.dtype) for g in grads]
    return _Job(grads, out_shape, {}, sum(1 if k == "col" else N_CHIPS for k in kinds), build)


def _scatter_job(half, kind):
    def build(in_refs, out_refs, cp):
        x, y, c, me, chips = _place()
        (src,), (dst,) = in_refs, out_refs
        return [cp.remote(_compact_piece(src, kind, _chip_index(chip)), dst.at[r], (*chip, c)) for r, chip in enumerate(chips)]

    part_shape = (half.shape[0], half.shape[1] // N_CHIPS) if kind == "col" else (half.shape[0] // N_CHIPS, half.shape[1])
    return _Job([half], [jax.ShapeDtypeStruct((N_CHIPS - 1,) + part_shape, half.dtype)], {}, N_CHIPS - 1, build)


def _share_job(shards):
    def build(in_refs, out_refs, cp):
        x, y, c, me, chips = _place()
        copies = []
        for ref in out_refs:
            mine = _half_rows(ref, c)
            copies.append(cp.remote(mine, mine, (x, y, 1 - c)))
        return copies

    return _Job(shards, *_in_place(shards), len(shards), build)


N_DEVICES = 2 * N_CHIPS


def _small_exchange_job(slots):
    def build(in_refs, out_refs, cp):
        x, y, c, me, chips = _place()
        (ref,) = out_refs
        mine = ref.at[2 * me + c]
        return [cp.remote(mine, mine, (x ^ (p >> 2), y ^ ((p >> 1) & 1), c ^ (p & 1))) for p in range(1, N_DEVICES)]

    return _Job([slots], *_in_place([slots]), N_DEVICES - 1, build)


def _in_own_slot(packed, place, name):
    R, C = packed.shape

    def body(s_ref, p_ref, o_ref):
        o_ref[...] = p_ref[...]

    return _placed_call(body, name, place, (1,), [pl.BlockSpec((R, C), lambda i, s: (0, 0))],
                        pl.BlockSpec((None, R, C), lambda i, s: (2 * s[0] + s[1], 0, 0)),
                        jax.ShapeDtypeStruct((N_DEVICES, R, C), F32), [packed])


def _sum_slots(slots, name):
    n, R, C = slots.shape

    def body(s_ref, o_ref):
        acc = s_ref[0]
        for i in range(1, n):
            acc = acc + s_ref[i]
        o_ref[...] = acc

    return _call(body,name=name, grid=(1,), in_specs=[pl.BlockSpec((n, R, C), lambda i: (0, 0, 0))],
                 out_specs=pl.BlockSpec((R, C), lambda i: (0, 0)), out_shape=jax.ShapeDtypeStruct((R, C), F32),
                 sem=("arbitrary",))(slots)


def _packed_rows(size, width):
    return -(-size // (8 * width)) * 8


def _pack(arrays, width):
    rows = []
    for a in arrays:
        flat = a.reshape(-1)
        n_rows = _packed_rows(flat.shape[0], width)
        rows.append(jnp.pad(flat, (0, n_rows * width - flat.shape[0])).reshape(n_rows, width))
    return jnp.concatenate(rows, axis=0)


def _unpack(packed, shapes):
    out, r0, width = [], 0, packed.shape[1]
    for shape in shapes:
        size = 1
        for d in shape:
            size *= d
        out.append(packed[r0:r0 + _packed_rows(size, width)].reshape(-1)[:size].reshape(shape))
        r0 += _packed_rows(size, width)
    return out


class _Backlog:
    def __init__(self, first):
        self.now, self.free, self.flights, self.last, self.chain = 0.0, {"ici": 0.0, "d2d": 0.0}, [], first, []

    def run(self, fn, us, *args, **kw):
        out = fn(*args, **kw)
        self.now += us
        self.last = out[0] if isinstance(out, (list, tuple)) else out
        self.poll()
        return out

    def start(self, job, name, link, cost, done):
        flying = _start_job(job, name + "_start", self.chain)
        self.chain = [flying.token]
        _Behind.pending.append(flying.token)
        ends = max(self.now, self.free[link]) + cost
        self.free[link] = ends
        self.flights.append((ends + LANDING_SLACK_US, name, flying, done))
        self.flights.sort(key=lambda f: f[0])

    def poll(self, block=False):
        while self.flights and (block or self.flights[0][0] <= self.now):
            ends, name, flying, done = self.flights.pop(0)
            self.now, block = max(self.now, ends), False
            done(*_wait_job(flying, name + "_wait", [self.last] + self.chain))


class _GatherStream:
    def __init__(self, backlog, bufs, kinds, costs):
        self.backlog, self.bufs, self.kinds, self.costs, self.begun, self.complete = backlog, bufs, kinds, costs, 0, set()
        self.begin()

    def begin(self):
        u, self.begun = self.begun, self.begun + 1
        self.backlog.start(_gather_job(self.bufs[u], self.kinds[u], 1), "gather_%d" % u, "ici", 0.5 * self.costs[u],
                           lambda outs, kept: self.arrived(u, outs[0]))

    def arrived(self, u, buf):
        self.bufs[u] = buf
        self.backlog.start(_gather_job(buf, self.kinds[u], 2), "relay_%d" % u, "ici", 0.25 * self.costs[u],
                           lambda outs, kept: self.relayed(u, outs[0]))
        while self.begun <= min(u + (1 if u < STARVED_WEIGHTS else 2), len(self.bufs) - 1):
            self.begin()

    def relayed(self, u, buf):
        self.bufs[u] = buf
        self.backlog.start(_gather_job(buf, self.kinds[u], 3), "handon_%d" % u, "d2d", D2D_SHARE * self.costs[u],
                           lambda outs, kept: self.handed(u, outs[0]))

    def handed(self, u, buf):
        self.bufs[u] = buf
        self.complete.add(u)

    def ready(self, u):
        while u not in self.complete:
            assert self.backlog.flights, "weight %d is not on its way" % u
            self.backlog.poll(block=True)
        return self.bufs[u]


class _GradStream:
    def __init__(self, backlog, u, name, kind, cost, g, place, results):
        self.backlog, self.u, self.name, self.kind, self.cost, self.place, self.results = backlog, u, name, kind, cost, place, results
        backlog.start(_exchange_halves_job([g], [kind]), "to_sibling_" + name, "d2d", D2D_SHARE * cost, self.exchanged)

    def exchanged(self, outs, kept):
        chip_sum = self.backlog.run(_add_pair, SIDE_KERNEL_US, kept[0], outs[0], self.kind, self.place, "chip_sum_" + self.name)
        self.backlog.start(_scatter_job(chip_sum, self.kind), "to_owners_" + self.name, "ici", self.cost, self.scattered)

    def scattered(self, outs, kept):
        reduced = self.backlog.run(_sum_chips, SIDE_KERNEL_US, kept[0], outs[0], self.kind, self.place, "reduce_" + self.name)
        self.backlog.start(_share_job([reduced]), "share_" + self.name, "d2d", D2D_SHARE * self.cost, self.shared)

    def shared(self, outs, kept):
        self.results[self.u] = outs[0]


SIDE_KERNEL_US = 12.0
D2D_SHARE = 0.15
LANDING_SLACK_US = 5.0
STARVED_WEIGHTS = 8


def kernel(x, mix_pre_g, mix_post_g, ffn_pre_g, ffn_post_g, ab_w_in, pool_w, pool_scale, conv_w, conv_b, conv_ln_g, conv_ln_b, ab_w_out, sc_w_in, sc_conv_w, sc_w_out, ffn_w1, ffn_w2, loss_target, m_mix_pre_g, m_mix_post_g, m_ffn_pre_g, m_ffn_post_g, m_ab_w_in, m_pool_w, m_pool_scale, m_conv_w, m_conv_b, m_conv_ln_g, m_conv_ln_b, m_ab_w_out, m_sc_w_in, m_sc_conv_w, m_sc_w_out, m_ffn_w1, m_ffn_w2, v_mix_pre_g, v_mix_post_g, v_ffn_pre_g, v_ffn_post_g, v_ab_w_in, v_pool_w, v_pool_scale, v_conv_w, v_conv_b, v_conv_ln_g, v_conv_ln_b, v_ab_w_out, v_sc_w_in, v_sc_conv_w, v_sc_w_out, v_ffn_w1, v_ffn_w2):
    x0, target = x[0], loss_target[0]
    T, D = x0.shape
    DP = pool_scale.shape[-1]
    gain = lambda g, layer: g[layer][None, :]

    big = [("ab_w_in", ab_w_in, 0, "col", 67.0), ("ab_w_out", ab_w_out, 0, "row", 44.0),
           ("ffn_w1_0", ffn_w1, 0, "col", 177.0), ("ffn_w2_0", ffn_w2, 0, "row", 177.0),
           ("sc_w_in", sc_w_in, 0, "col", 133.0), ("sc_w_out", sc_w_out, 0, "row", 44.0),
           ("ffn_w1_1", ffn_w1, 1, "col", 177.0), ("ffn_w2_1", ffn_w2, 1, "row", 177.0)]
    kinds = [b[3] for b in big]
    chip = 2 * lax.axis_index("x") + lax.axis_index("y")
    place = jnp.stack([chip, lax.axis_index("c")]).astype(jnp.int32)

    def own_in_zeros(shard, ax):
        full = jnp.zeros(tuple(d * N_CHIPS if i == ax else d for i, d in enumerate(shard.shape)), shard.dtype)
        return lax.dynamic_update_slice_in_dim(full, shard, chip * shard.shape[ax], axis=ax)

    W = [_cast_into_full(w, layer, kind, place, "cast_" + name) for name, w, layer, kind, _ in big]
    smalls = [own_in_zeros(pool_w[0], 1), own_in_zeros(conv_w[0], 1), own_in_zeros(sc_conv_w[0], 1)]
    backlog = _Backlog(x0)
    run = backlog.run
    small_weights = []
    backlog.start(_gather_small_job(smalls, [1, 1, 1]), "gather_small", "ici", 6.0, lambda outs, kept: small_weights.extend(outs))
    gather = _GatherStream(backlog, W, kinds, [b[4] for b in big])

    relu_sq = lambda acc: (jnp.maximum(acc, 0.0), jnp.square(jnp.maximum(acc, 0.0)))
    relu_sq_bwd = lambda acc, a: (acc * (2.0 * a.astype(F32)),)

    cast_done = [w[:8, :128] for w in W[1:]]
    _Behind.pending += cast_done[:3]
    h0 = run(_norm_fwd, 12.0, x0, gain(mix_pre_g, 0), "norm_in")
    first_weight = gather.ready(0)
    _Behind.pending += cast_done[3:]
    z0 = run(_matmul, 35.0, h0, first_weight, "nn", "mix0_in")
    while not small_weights:
        backlog.poll(block=True)
    pool_w_full, conv_w_full, sc_conv_w_full = small_weights
    pooled, y_pool = run(_pool_fwd, 23.0, z0, pool_w_full, pool_scale, "pool_fwd")
    a_conv, c_conv = run(_conv_fwd, 25.0, z0, conv_w_full, conv_b, DP, "conv_fwd")
    y_conv = run(_ln_silu_fwd, 10.0, c_conv, conv_ln_g, conv_ln_b, "ln_silu_fwd")
    y0 = jnp.concatenate([y_pool, y_conv], axis=1)
    m0 = run(_matmul, 25.0, y0, gather.ready(1), "nn", "mix0_out")
    x1, h1 = run(_residual_norm, 21.0, x0, m0, gain(mix_post_g, 0), gain(ffn_pre_g, 0), "res_mix0")
    a0, a0sq = run(_matmul, 81.0, h1, gather.ready(2), "nn", "ffn0_up", out_dtypes=(BF16, BF16), epilogue=relu_sq)
    f0 = run(_matmul, 84.0, a0sq, gather.ready(3), "nn", "ffn0_down", tk=LONG_K_TILE)
    x2, h2 = run(_residual_norm, 22.0, x1, f0, gain(ffn_post_g, 0), gain(mix_pre_g, 1), "res_ffn0")
    z1 = run(_matmul, 62.0, h2, gather.ready(4), "nn", "mix1_in")
    y1 = run(_short_fwd, 22.0, z1, sc_conv_w_full, "short_fwd")
    m1 = run(_matmul, 25.0, y1, gather.ready(5), "nn", "mix1_out")
    x3, h3 = run(_residual_norm, 21.0, x2, m1, gain(mix_post_g, 1), gain(ffn_pre_g, 1), "res_mix1")
    a1, a1sq = run(_matmul, 81.0, h3, gather.ready(6), "nn", "ffn1_up", out_dtypes=(BF16, BF16), epilogue=relu_sq)
    f1 = run(_matmul, 84.0, a1sq, gather.ready(7), "nn", "ffn1_down", tk=LONG_K_TILE)
    w_in0, w_out0, w1_0, w2_0, w_in1, w_out1, w1_1, w2_1 = W

    grads_big = [None] * len(big)


    def reduce_grad(u, g):
        name, _, _, kind, cost = big[u]
        _GradStream(backlog, u, name, kind, cost, g, place, grads_big)

    dx, df1, d_ffn_post_1, loss_row = run(_loss_and_last_norm_bwd, 30.0, x3, f1, gain(ffn_post_g, 1), target, "loss")
    reduce_grad(7, run(_matmul, 80.0, a1sq, df1, "tn", "ffn1_down_dw", out_dtypes=(BF16,)))
    dz = run(_matmul, 82.0, df1, w2_1, "nt", "ffn1_down_dx", out_dtypes=(BF16,), epilogue=relu_sq_bwd, epi=(a1,))
    reduce_grad(6, run(_matmul, 80.0, h3, dz, "tn", "ffn1_up_dw", out_dtypes=(BF16,)))
    dh = run(_matmul, 87.0, dz, w1_1, "nt", "ffn1_up_dx", tk=LONG_K_TILE)
    dx, d_ffn_pre_1, dm1, d_mix_post_1 = run(_norms_bwd, 36.0, dx, dh, x3, gain(ffn_pre_g, 1), m1, gain(mix_post_g, 1), "norms_bwd3")

    reduce_grad(5, run(_matmul, 24.0, y1, dm1, "tn", "mix1_out_dw", out_dtypes=(BF16,)))
    dy1 = run(_matmul, 25.0, dm1, w_out1, "nt", "mix1_out_dx")
    db, dcg, du, d_sc_conv_w = run(_short_bwd, 41.0, dy1, z1, sc_conv_w_full, "short_bwd")
    dz1 = jnp.concatenate([db, dcg, du], axis=1)
    reduce_grad(4, run(_matmul, 62.0, h2, dz1, "tn", "mix1_in_dw", out_dtypes=(BF16,)))
    dh = run(_matmul, 68.0, dz1, w_in1, "nt", "mix1_in_dx")
    dx, d_mix_pre_1, df0, d_ffn_post_0 = run(_norms_bwd, 35.0, dx, dh, x2, gain(mix_pre_g, 1), f0, gain(ffn_post_g, 0), "norms_bwd2")

    reduce_grad(3, run(_matmul, 80.0, a0sq, df0, "tn", "ffn0_down_dw", out_dtypes=(BF16,)))
    dz = run(_matmul, 82.0, df0, w2_0, "nt", "ffn0_down_dx", out_dtypes=(BF16,), epilogue=relu_sq_bwd, epi=(a0,))
    reduce_grad(2, run(_matmul, 80.0, h1, dz, "tn", "ffn0_up_dw", out_dtypes=(BF16,)))
    dh = run(_matmul, 87.0, dz, w1_0, "nt", "ffn0_up_dx", tk=LONG_K_TILE)
    dx, d_ffn_pre_0, dm0, d_mix_post_0 = run(_norms_bwd, 36.0, dx, dh, x1, gain(ffn_pre_g, 0), m0, gain(mix_post_g, 0), "norms_bwd1")

    reduce_grad(1, run(_matmul, 24.0, y0, dm0, "tn", "mix0_out_dw", out_dtypes=(BF16,)))
    dy0 = run(_matmul, 25.0, dm0, w_out0, "nt", "mix0_out_dx")
    du_pool, d_pool_w, d_pool_scale = run(_pool_bwd, 28.0, dy0, pooled, pool_w_full, pool_scale, "pool_bwd")
    dc, d_ln_g, d_ln_b = run(_ln_silu_bwd, 15.0, dy0, c_conv, conv_ln_g, conv_ln_b, "ln_silu_bwd")
    dv, dgate, d_conv_w, d_conv_b = run(_conv_bwd, 52.0, dc, a_conv, z0, conv_w_full, DP, "conv_bwd")
    dz0 = jnp.concatenate([du_pool, dv, dgate], axis=1)

    small_sums = {}

    def exchange_small(key, arrays, cost):
        slots = _in_own_slot(_pack(arrays, D), place, "small_grads_slot_" + key)
        backlog.start(_small_exchange_job(slots), "small_grads_" + key, "ici", cost,
                      lambda outs, kept: small_sums.__setitem__(key, _unpack(_sum_slots(outs[0], "small_grads_sum_" + key),
                                                                             [a.shape for a in arrays])))

    exchange_small("most", [d_mix_pre_1, jnp.concatenate([d_mix_post_0, d_mix_post_1], 0),
                            jnp.concatenate([d_ffn_pre_0, d_ffn_pre_1], 0), jnp.concatenate([d_ffn_post_0, d_ffn_post_1], 0),
                            d_pool_scale, d_conv_b, d_ln_g, d_ln_b, d_pool_w, d_conv_w, d_sc_conv_w], 112.0)
    reduce_grad(0, run(_matmul, 34.0, h0, dz0, "tn", "mix0_in_dw", out_dtypes=(BF16,)))
    dh = run(_matmul, 40.0, dz0, w_in0, "nt", "mix0_in_dx")
    grad_x, d_mix_pre_0 = run(_norms_bwd, 26.0, dx, dh, x0, gain(mix_pre_g, 0), None, None, "norms_bwd0")
    exchange_small("last", [d_mix_pre_0, loss_row], 5.0)

    upd, gr, first = {}, grads_big, {}

    def keep(where, key, outs):
        where[key] = outs
        return outs

    adamw_big = [
        (7, lambda: keep(first, "ffn_w2", _adamw(ffn_w2, gr[7], m_ffn_w2, v_ffn_w2, "adamw_ffn_w2_1", layer=1)), 46.0),
        (6, lambda: keep(first, "ffn_w1", _adamw(ffn_w1, gr[6], m_ffn_w1, v_ffn_w1, "adamw_ffn_w1_1", layer=1)), 46.0),
        (5, lambda: keep(upd, "sc_w_out", _adamw(sc_w_out, gr[5], m_sc_w_out, v_sc_w_out, "adamw_sc_w_out")), 14.0),
        (4, lambda: keep(upd, "sc_w_in", _adamw(sc_w_in, gr[4], m_sc_w_in, v_sc_w_in, "adamw_sc_w_in")), 35.0),
        (3, lambda: keep(upd, "ffn_w2", _adamw(ffn_w2, gr[3], m_ffn_w2, v_ffn_w2, "adamw_ffn_w2_0", layer=0,
                                               carried=first["ffn_w2"])), 46.0),
        (2, lambda: keep(upd, "ffn_w1", _adamw(ffn_w1, gr[2], m_ffn_w1, v_ffn_w1, "adamw_ffn_w1_0", layer=0,
                                               carried=first["ffn_w1"])), 46.0),
        (1, lambda: keep(upd, "ab_w_out", _adamw(ab_w_out, gr[1], m_ab_w_out, v_ab_w_out, "adamw_ab_w_out")), 14.0),
        (0, lambda: keep(upd, "ab_w_in", _adamw(ab_w_in, gr[0], m_ab_w_in, v_ab_w_in, "adamw_ab_w_in")), 19.0)]
    while adamw_big or backlog.flights:
        due = [a for a in adamw_big if gr[a[0]] is not None]
        if due:
            adamw_big.remove(due[0])
            backlog.run(due[0][1], due[0][2])
        else:
            backlog.poll(block=True)

    (g_mix_pre_1, g_mix_post, g_ffn_pre, g_ffn_post, g_pool_scale, g_conv_b, g_ln_g, g_ln_b, g_pool_w_full, g_conv_w_full,
     g_sc_conv_w_full) = small_sums["most"]
    g_mix_pre = jnp.concatenate([small_sums["last"][0], g_mix_pre_1], 0)
    loss = small_sums["last"][1][0, 0]
    own = lambda a, ax: lax.dynamic_slice_in_dim(a, chip * (a.shape[ax] // N_CHIPS), a.shape[ax] // N_CHIPS, axis=ax)
    g_pool_w, g_conv_w, g_sc_conv_w = own(g_pool_w_full, 1), own(g_conv_w_full, 1), own(g_sc_conv_w_full, 1)

    def small_update(w, g, m, v, name):
        shape = w.shape
        as3 = lambda a: a.reshape((1, -1, shape[-1]))
        outs = _adamw(as3(w), g.reshape((-1, shape[-1])), as3(m), as3(v), "adamw_" + name)
        return [o.reshape(shape) for o in outs]

    upd["mix_pre_g"] = small_update(mix_pre_g, g_mix_pre, m_mix_pre_g, v_mix_pre_g, "mix_pre_g")
    upd["mix_post_g"] = small_update(mix_post_g, g_mix_post, m_mix_post_g, v_mix_post_g, "mix_post_g")
    upd["ffn_pre_g"] = small_update(ffn_pre_g, g_ffn_pre, m_ffn_pre_g, v_ffn_pre_g, "ffn_pre_g")
    upd["ffn_post_g"] = small_update(ffn_post_g, g_ffn_post, m_ffn_post_g, v_ffn_post_g, "ffn_post_g")
    upd["pool_w"] = small_update(pool_w, g_pool_w, m_pool_w, v_pool_w, "pool_w")
    upd["pool_scale"] = small_update(pool_scale, g_pool_scale, m_pool_scale, v_pool_scale, "pool_scale")
    upd["conv_w"] = small_update(conv_w, g_conv_w, m_conv_w, v_conv_w, "conv_w")
    upd["conv_b"] = small_update(conv_b, g_conv_b, m_conv_b, v_conv_b, "conv_b")
    upd["conv_ln_g"] = small_update(conv_ln_g, g_ln_g, m_conv_ln_g, v_conv_ln_g, "conv_ln_g")
    upd["conv_ln_b"] = small_update(conv_ln_b, g_ln_b, m_conv_ln_b, v_conv_ln_b, "conv_ln_b")
    upd["sc_conv_w"] = small_update(sc_conv_w, g_sc_conv_w, m_sc_conv_w, v_sc_conv_w, "sc_conv_w")

    order = ["mix_pre_g", "mix_post_g", "ffn_pre_g", "ffn_post_g", "ab_w_in", "pool_w", "pool_scale", "conv_w", "conv_b",
             "conv_ln_g", "conv_ln_b", "ab_w_out", "sc_w_in", "sc_conv_w", "sc_w_out", "ffn_w1", "ffn_w2"]
    out = [loss, grad_x[None]]
    for part in range(4):
        out += [upd[n][part] for n in order]
    return tuple(out)
```

```python
import jax
import jax.numpy as jnp
from jax import lax
from jax.experimental import pallas as pl
from jax.experimental.pallas import tpu as pltpu

F32, BF16 = jnp.float32, jnp.bfloat16
EPS = 1e-6
N_GROUPS = 4
MAX_WINDOW = 16
CONV_K = 31
SHORT_K = 3
CONV_PAD = 32
SHORT_PAD = 8
ADAM_LR, ADAM_B1, ADAM_B2, ADAM_EPS, ADAM_WD, ADAM_STEP = 0.001, 0.9, 0.999, 1e-08, 0.01, 10
N_CHIPS = 4
VMEM_LIMIT_BYTES = 56 * 1024 * 1024
ROW_TILE = 256
CHUNK = 256
LONG_K_TILE = 4096
MESH = pl.DeviceIdType.MESH
HBM = pl.BlockSpec(memory_space=pltpu.HBM)
SEM = pl.BlockSpec(memory_space=pltpu.SEMAPHORE)
ANY = pl.BlockSpec(memory_space=pl.ANY)


def _cp(*sem):
    return pltpu.CompilerParams(dimension_semantics=sem, vmem_limit_bytes=VMEM_LIMIT_BYTES)


def _sigmoid(v):
    return 1.0 / (1.0 + jnp.exp(-v))


class _Behind:
    pending = []


def _call(body, prefetch=None, **kw):
    behind, _Behind.pending = _Behind.pending, []
    single = not isinstance(kw["out_shape"], (list, tuple))
    in_specs, scratch = list(kw["in_specs"]), list(kw.get("scratch_shapes", ()))
    out_shape = [kw["out_shape"]] if single else list(kw["out_shape"])
    out_specs = [kw["out_specs"]] if single else list(kw["out_specs"])
    n_pre = 0 if prefetch is None else 1
    n_own, n_behind = len(in_specs), len(behind)

    def wrapped(*refs):
        body(*refs[:n_pre + n_own], *refs[n_pre + n_own + n_behind:])

    specs = dict(grid=kw["grid"], in_specs=in_specs + [ANY] * n_behind, out_specs=out_specs)
    if prefetch is None:
        specs["scratch_shapes"] = scratch
    else:
        specs = dict(grid_spec=pltpu.PrefetchScalarGridSpec(num_scalar_prefetch=1, scratch_shapes=scratch, **specs))
    aliases = {n_pre + i: o for i, o in kw.get("input_output_aliases", {}).items()}
    call = pl.pallas_call(wrapped, name=kw["name"], out_shape=out_shape, input_output_aliases=aliases,
                          compiler_params=_cp(*kw["sem"]), **specs)

    def run(*args):
        outs = call(*([prefetch] * n_pre), *args, *behind)
        return outs[0] if single else list(outs)

    return run


_DIMS = {"nn": (((1,), (0,)), ((), ())), "nt": (((1,), (1,)), ((), ())), "tn": (((0,), (0,)), ((), ()))}


def _pick(n, cap, step=256):
    if n <= cap:
        return n
    return next(t for t in range(cap - cap % step, 0, -step) if n % t == 0)


def _matmul(a, b, mode, name, out_dtypes=(F32,), epilogue=None, epi=(), tm=1024, tn=1024, tk=2048):
    if mode == "tn":
        (K, M), (K2, N) = a.shape, b.shape
    elif mode == "nt":
        (M, K), (N, K2) = a.shape, b.shape
    else:
        (M, K), (K2, N) = a.shape, b.shape
    assert K == K2
    tm, tn, tk = _pick(M, tm), _pick(N, tn), _pick(K, tk)
    nk = K // tk
    a_spec = pl.BlockSpec((tk, tm), lambda i, j, k: (k, i)) if mode == "tn" else pl.BlockSpec((tm, tk), lambda i, j, k: (i, k))
    b_spec = pl.BlockSpec((tn, tk), lambda i, j, k: (j, k)) if mode == "nt" else pl.BlockSpec((tk, tn), lambda i, j, k: (k, j))
    o_spec = pl.BlockSpec((tm, tn), lambda i, j, k: (i, j))
    n_epi, n_out = len(epi), len(out_dtypes)

    def body(a_ref, b_ref, *rest):
        epi_refs, out_refs, scratch = rest[:n_epi], rest[n_epi:n_epi + n_out], rest[n_epi + n_out:]
        part = lax.dot_general(a_ref[...].astype(BF16), b_ref[...].astype(BF16), _DIMS[mode], preferred_element_type=F32)

        def finish(acc):
            outs = epilogue(acc, *[r[...] for r in epi_refs]) if epilogue else (acc,)
            for o_ref, o in zip(out_refs, outs):
                o_ref[...] = o.astype(o_ref.dtype)

        if nk == 1:
            finish(part)
        else:
            acc_ref = scratch[0]
            k = pl.program_id(2)

            @pl.when(k == 0)
            def _():
                acc_ref[...] = part

            @pl.when(k > 0)
            def _():
                acc_ref[...] += part

            @pl.when(k == nk - 1)
            def _():
                finish(acc_ref[...])

    outs = _call(
        body, name=name, grid=(M // tm, N // tn, nk),
        in_specs=[a_spec, b_spec] + [o_spec] * n_epi, out_specs=[o_spec] * n_out,
        out_shape=[jax.ShapeDtypeStruct((M, N), dt) for dt in out_dtypes],
        scratch_shapes=[pltpu.VMEM((tm, tn), F32)] if nk > 1 else [],
        sem=("parallel", "parallel", "arbitrary"))(a, b, *epi)
    return outs[0] if n_out == 1 else outs


def _rms(x, g):
    r = lax.rsqrt(jnp.mean(x * x, axis=-1, keepdims=True) + EPS)
    return x * r * g


def _rms_bwd(dy, x, g):
    r = lax.rsqrt(jnp.mean(x * x, axis=-1, keepdims=True) + EPS)
    xn = x * r
    dyg = dy * g
    dx = r * (dyg - xn * jnp.mean(dyg * xn, axis=-1, keepdims=True))
    return dx, jnp.sum(dy * xn, axis=0, keepdims=True)


def _rows(d, tr=ROW_TILE):
    return pl.BlockSpec((tr, d), lambda i: (i, 0))


def _vec(d):
    return pl.BlockSpec((1, d), lambda i: (0, 0))


def _accumulate(ref, val):
    @pl.when(pl.program_id(0) == 0)
    def _():
        ref[...] = val

    @pl.when(pl.program_id(0) > 0)
    def _():
        ref[...] += val


def _norm_fwd(x, g, name):
    T, D = x.shape

    def body(x_ref, g_ref, h_ref):
        h_ref[...] = _rms(x_ref[...], g_ref[...]).astype(BF16)

    return _call(body,name=name,grid=(T // ROW_TILE,), in_specs=[_rows(D), _vec(D)], out_specs=_rows(D),
                          out_shape=jax.ShapeDtypeStruct((T, D), BF16), sem=("parallel",))(x, g)


def _residual_norm(x, m, g_post, g_next, name):
    T, D = x.shape

    def body(x_ref, m_ref, gp_ref, gn_ref, xo_ref, h_ref):
        xo = x_ref[...] + _rms(m_ref[...], gp_ref[...])
        xo_ref[...] = xo
        h_ref[...] = _rms(xo, gn_ref[...]).astype(BF16)

    return _call(body,name=name,grid=(T // ROW_TILE,), in_specs=[_rows(D), _rows(D), _vec(D), _vec(D)],
                          out_specs=[_rows(D), _rows(D)],
                          out_shape=[jax.ShapeDtypeStruct((T, D), F32), jax.ShapeDtypeStruct((T, D), BF16)],
                          sem=("parallel",))(x, m, g_post, g_next)


def _loss_and_last_norm_bwd(x, m, g_post, target, name):
    T, D = x.shape

    def body(x_ref, m_ref, gp_ref, t_ref, dx_ref, dm_ref, dg_ref, loss_ref):
        m_val, gp = m_ref[...], gp_ref[...]
        err = x_ref[...] + _rms(m_val, gp) - t_ref[...]
        dx = err * (1.0 / D)
        dx_ref[...] = dx
        dm, dg = _rms_bwd(dx, m_val, gp)
        dm_ref[...] = dm.astype(BF16)
        _accumulate(dg_ref, dg)
        _accumulate(loss_ref, jnp.full((1, 128), 0.5 * jnp.sum(err * err) * (1.0 / D), F32))

    return _call(body,name=name,grid=(T // ROW_TILE,), in_specs=[_rows(D), _rows(D), _vec(D), _rows(D)],
                          out_specs=[_rows(D), _rows(D), _vec(D), _vec(128)],
                          out_shape=[jax.ShapeDtypeStruct((T, D), F32), jax.ShapeDtypeStruct((T, D), BF16),
                                     jax.ShapeDtypeStruct((1, D), F32), jax.ShapeDtypeStruct((1, 128), F32)],
                          sem=("arbitrary",))(x, m, g_post, target)


def _norms_bwd(dx, dh, x_in, g_pre, m_prev, g_post_prev, name):
    T, D = dx.shape
    with_prev = m_prev is not None

    def body(*refs):
        if with_prev:
            dx_ref, dh_ref, x_ref, gq_ref, m_ref, gp_ref, dxo_ref, dgq_ref, dm_ref, dgp_ref = refs
        else:
            dx_ref, dh_ref, x_ref, gq_ref, dxo_ref, dgq_ref = refs
        d_in, dgq = _rms_bwd(dh_ref[...], x_ref[...], gq_ref[...])
        dxo = dx_ref[...] + d_in
        dxo_ref[...] = dxo
        _accumulate(dgq_ref, dgq)
        if with_prev:
            dm, dgp = _rms_bwd(dxo, m_ref[...], gp_ref[...])
            dm_ref[...] = dm.astype(BF16)
            _accumulate(dgp_ref, dgp)

    ins, in_specs = [dx, dh, x_in, g_pre], [_rows(D), _rows(D), _rows(D), _vec(D)]
    out_specs = [_rows(D), _vec(D)]
    out_shape = [jax.ShapeDtypeStruct((T, D), F32), jax.ShapeDtypeStruct((1, D), F32)]
    if with_prev:
        ins += [m_prev, g_post_prev]
        in_specs += [_rows(D), _vec(D)]
        out_specs += [_rows(D), _vec(D)]
        out_shape += [jax.ShapeDtypeStruct((T, D), BF16), jax.ShapeDtypeStruct((1, D), F32)]
    return _call(body,name=name,grid=(T // ROW_TILE,), in_specs=in_specs, out_specs=out_specs, out_shape=out_shape,
                          sem=("arbitrary",))(*ins)


def _window_weights(g):
    w = 2 << g
    return w, [jnp.where(j < w, 1.0, 0.0).astype(F32) for j in range(MAX_WINDOW)]


def _valid_count(r0, rows, w):
    t = (lax.broadcasted_iota(jnp.int32, (rows, 1), 0) + (r0 + 1)).astype(F32)
    return jnp.minimum(t, w.astype(F32))


def _pool_fwd(z, pool_w, pool_scale, name):
    T = z.shape[0]
    PG = pool_w.shape[-1]
    DP = N_GROUPS * PG
    rc = min(CHUNK, T)

    def body(u_ref, pw_ref, sc_ref, pooled_ref, y_ref, pad):
        w, wts = _window_weights(pl.program_id(0))
        pad[pl.ds(0, MAX_WINDOW), :] = jnp.zeros((MAX_WINDOW, PG), F32)
        pad[pl.ds(MAX_WINDOW, T), :] = u_ref[...]
        for r0 in range(0, T, rc):
            acc = jnp.zeros((rc, PG), F32)
            for j in range(MAX_WINDOW):
                acc = acc + wts[j] * pad[pl.ds(MAX_WINDOW + r0 - j, rc), :]
            pooled = acc / _valid_count(r0, rc, w) - u_ref[pl.ds(r0, rc), :]
            pooled_ref[pl.ds(r0, rc), :] = pooled.astype(BF16)
        mixed = jnp.dot(pooled_ref[...], pw_ref[...].astype(BF16), preferred_element_type=F32)
        y_ref[...] = (mixed * sc_ref[...]).astype(BF16)

    col = lambda g: (0, g)
    return _call(
        body, name=name,grid=(N_GROUPS,),
        in_specs=[pl.BlockSpec((T, PG), col), pl.BlockSpec((None, PG, PG), lambda g: (g, 0, 0)), pl.BlockSpec((1, PG), col)],
        out_specs=[pl.BlockSpec((T, PG), col), pl.BlockSpec((T, PG), col)],
        out_shape=[jax.ShapeDtypeStruct((T, DP), BF16), jax.ShapeDtypeStruct((T, DP), BF16)],
        scratch_shapes=[pltpu.VMEM((T + MAX_WINDOW, PG), F32)], sem=("parallel",))(z, pool_w, pool_scale)


def _pool_bwd(dy, pooled, pool_w, pool_scale, name):
    T = dy.shape[0]
    PG = pool_w.shape[-1]
    DP = N_GROUPS * PG
    rc = min(CHUNK, T)

    def body(dy_ref, pooled_ref, pw_ref, sc_ref, du_ref, dpw_ref, dsc_ref, pad, dp_ref):
        w, wts = _window_weights(pl.program_id(0))
        pooled_v, pw = pooled_ref[...], pw_ref[...].astype(BF16)
        dy_v = dy_ref[...]
        mixed = jnp.dot(pooled_v, pw, preferred_element_type=F32)
        dsc_ref[...] = jnp.sum(dy_v * mixed, axis=0, keepdims=True)
        dmixed = (dy_v * sc_ref[...]).astype(BF16)
        dpw_ref[...] = lax.dot_general(pooled_v, dmixed, _DIMS["tn"], preferred_element_type=F32)
        dp_ref[...] = lax.dot_general(dmixed, pw, _DIMS["nt"], preferred_element_type=F32)
        pad[pl.ds(T, MAX_WINDOW), :] = jnp.zeros((MAX_WINDOW, PG), F32)
        for r0 in range(0, T, rc):
            pad[pl.ds(r0, rc), :] = dp_ref[pl.ds(r0, rc), :] / _valid_count(r0, rc, w)
        for r0 in range(0, T, rc):
            acc = jnp.zeros((rc, PG), F32)
            for j in range(MAX_WINDOW):
                acc = acc + wts[j] * pad[pl.ds(r0 + j, rc), :]
            du_ref[pl.ds(r0, rc), :] = (acc - dp_ref[pl.ds(r0, rc), :]).astype(BF16)

    col = lambda g: (0, g)
    return _call(
        body, name=name,grid=(N_GROUPS,),
        in_specs=[pl.BlockSpec((T, PG), col), pl.BlockSpec((T, PG), col), pl.BlockSpec((None, PG, PG), lambda g: (g, 0, 0)),
                  pl.BlockSpec((1, PG), col)],
        out_specs=[pl.BlockSpec((T, PG), col), pl.BlockSpec((None, PG, PG), lambda g: (g, 0, 0)), pl.BlockSpec((1, PG), col)],
        out_shape=[jax.ShapeDtypeStruct((T, DP), BF16), jax.ShapeDtypeStruct((N_GROUPS, PG, PG), F32),
                   jax.ShapeDtypeStruct((1, DP), F32)],
        scratch_shapes=[pltpu.VMEM((T + MAX_WINDOW, PG), F32), pltpu.VMEM((T, PG), F32)],
        sem=("parallel",))(dy, pooled, pool_w, pool_scale)


def _conv_fwd(z, conv_w, conv_b, d_pool, name, tc=128):
    T = z.shape[0]
    DC = conv_w.shape[-1]
    rc = min(CHUNK, T)
    v0, g0 = d_pool // tc, (d_pool + DC) // tc

    def body(v_ref, gt_ref, w_ref, b_ref, a_ref, c_ref, pad):
        pad[pl.ds(0, CONV_PAD), :] = jnp.zeros((CONV_PAD, tc), F32)
        for r0 in range(0, T, rc):
            a = v_ref[pl.ds(r0, rc), :] * _sigmoid(gt_ref[pl.ds(r0, rc), :])
            a_ref[pl.ds(r0, rc), :] = a
            pad[pl.ds(CONV_PAD + r0, rc), :] = a
        for r0 in range(0, T, rc):
            acc = jnp.zeros((rc, tc), F32) + b_ref[...]
            for k in range(CONV_K):
                acc = acc + w_ref[pl.ds(k, 1), :] * pad[pl.ds(CONV_PAD - (CONV_K - 1) + k + r0, rc), :]
            c_ref[pl.ds(r0, rc), :] = acc

    col = lambda j: (0, j)
    return _call(
        body, name=name,grid=(DC // tc,),
        in_specs=[pl.BlockSpec((T, tc), lambda j: (0, v0 + j)), pl.BlockSpec((T, tc), lambda j: (0, g0 + j)),
                  pl.BlockSpec((CONV_K, tc), col), pl.BlockSpec((1, tc), col)],
        out_specs=[pl.BlockSpec((T, tc), col), pl.BlockSpec((T, tc), col)],
        out_shape=[jax.ShapeDtypeStruct((T, DC), F32), jax.ShapeDtypeStruct((T, DC), F32)],
        scratch_shapes=[pltpu.VMEM((T + CONV_PAD, tc), F32)], sem=("parallel",))(z, z, conv_w, conv_b)


def _conv_bwd(dc, a, z, conv_w, d_pool, name, tc=128):
    T, DC = dc.shape
    rc = min(CHUNK, T)
    v0, g0 = d_pool // tc, (d_pool + DC) // tc

    def body(dc_ref, a_ref, v_ref, gt_ref, w_ref, dv_ref, dg_ref, dw_ref, db_ref, apad, dpad):
        apad[pl.ds(0, CONV_PAD), :] = jnp.zeros((CONV_PAD, tc), F32)
        apad[pl.ds(CONV_PAD, T), :] = a_ref[...]
        dpad[pl.ds(0, T), :] = dc_ref[...]
        dpad[pl.ds(T, CONV_PAD), :] = jnp.zeros((CONV_PAD, tc), F32)
        db_ref[...] = jnp.sum(dc_ref[...], axis=0, keepdims=True)
        for k in range(CONV_K):
            acc = jnp.zeros((8, tc), F32)
            for r0 in range(0, T, rc):
                prod = dc_ref[pl.ds(r0, rc), :] * apad[pl.ds(CONV_PAD - (CONV_K - 1) + k + r0, rc), :]
                acc = acc + jnp.sum(prod.reshape(rc // 8, 8, tc), axis=0)
            dw_ref[pl.ds(k, 1), :] = jnp.sum(acc, axis=0, keepdims=True)
        for r0 in range(0, T, rc):
            da = jnp.zeros((rc, tc), F32)
            for k in range(CONV_K):
                da = da + w_ref[pl.ds(k, 1), :] * dpad[pl.ds(r0 + (CONV_K - 1) - k, rc), :]
            sig = _sigmoid(gt_ref[pl.ds(r0, rc), :])
            dv_ref[pl.ds(r0, rc), :] = (da * sig).astype(BF16)
            dg_ref[pl.ds(r0, rc), :] = (da * v_ref[pl.ds(r0, rc), :] * sig * (1.0 - sig)).astype(BF16)

    col = lambda j: (0, j)
    return _call(
        body, name=name,grid=(DC // tc,),
        in_specs=[pl.BlockSpec((T, tc), col), pl.BlockSpec((T, tc), col), pl.BlockSpec((T, tc), lambda j: (0, v0 + j)),
                  pl.BlockSpec((T, tc), lambda j: (0, g0 + j)), pl.BlockSpec((CONV_K, tc), col)],
        out_specs=[pl.BlockSpec((T, tc), col), pl.BlockSpec((T, tc), col), pl.BlockSpec((CONV_K, tc), col),
                   pl.BlockSpec((1, tc), col)],
        out_shape=[jax.ShapeDtypeStruct((T, DC), BF16), jax.ShapeDtypeStruct((T, DC), BF16),
                   jax.ShapeDtypeStruct((CONV_K, DC), F32), jax.ShapeDtypeStruct((1, DC), F32)],
        scratch_shapes=[pltpu.VMEM((T + CONV_PAD, tc), F32), pltpu.VMEM((T + CONV_PAD, tc), F32)],
        sem=("parallel",))(dc, a, z, z, conv_w)


def _layer_norm_parts(c, g, b):
    mu = jnp.mean(c, axis=-1, keepdims=True)
    xc = c - mu
    rstd = lax.rsqrt(jnp.mean(xc * xc, axis=-1, keepdims=True) + EPS)
    xhat = xc * rstd
    return xhat, rstd, xhat * g + b


def _ln_silu_fwd(c, g, b, name):
    T, DC = c.shape

    def body(c_ref, g_ref, b_ref, y_ref):
        _, _, ln = _layer_norm_parts(c_ref[...], g_ref[...], b_ref[...])
        y_ref[...] = (ln * _sigmoid(ln)).astype(BF16)

    return _call(body,name=name,grid=(T // ROW_TILE,), in_specs=[_rows(DC), _vec(DC), _vec(DC)], out_specs=_rows(DC),
                          out_shape=jax.ShapeDtypeStruct((T, DC), BF16), sem=("parallel",))(c, g, b)


def _ln_silu_bwd(dy, c, g, b, name):
    T, DC = c.shape

    def body(dy_ref, c_ref, g_ref, b_ref, dc_ref, dg_ref, db_ref):
        gain = g_ref[...]
        xhat, rstd, ln = _layer_norm_parts(c_ref[...], gain, b_ref[...])
        s = _sigmoid(ln)
        dln = dy_ref[...] * (s * (1.0 + ln * (1.0 - s)))
        _accumulate(dg_ref, jnp.sum(dln * xhat, axis=0, keepdims=True))
        _accumulate(db_ref, jnp.sum(dln, axis=0, keepdims=True))
        dxh = dln * gain
        dc_ref[...] = rstd * (dxh - jnp.mean(dxh, axis=-1, keepdims=True) - xhat * jnp.mean(dxh * xhat, axis=-1, keepdims=True))

    return _call(body,name=name,grid=(T // ROW_TILE,),
                          in_specs=[pl.BlockSpec((ROW_TILE, DC), lambda i: (i, 1)), _rows(DC), _vec(DC), _vec(DC)],
                          out_specs=[_rows(DC), _vec(DC), _vec(DC)],
                          out_shape=[jax.ShapeDtypeStruct((T, DC), F32), jax.ShapeDtypeStruct((1, DC), F32),
                                     jax.ShapeDtypeStruct((1, DC), F32)],
                          sem=("arbitrary",))(dy, c, g, b)


def _short_specs(T, DS, tc):
    n = DS // tc
    return [pl.BlockSpec((T, tc), lambda j: (0, j)), pl.BlockSpec((T, tc), lambda j: (0, n + j)),
            pl.BlockSpec((T, tc), lambda j: (0, 2 * n + j))]


def _short_fwd(z, w, name, tc=256):
    T = z.shape[0]
    DS = w.shape[-1]
    rc = min(CHUNK, T)

    def body(b_ref, cg_ref, u_ref, w_ref, y_ref, pad):
        pad[pl.ds(0, SHORT_PAD), :] = jnp.zeros((SHORT_PAD, tc), F32)
        pad[pl.ds(SHORT_PAD, T), :] = cg_ref[...] * u_ref[...]
        for r0 in range(0, T, rc):
            r = jnp.zeros((rc, tc), F32)
            for k in range(SHORT_K):
                r = r + w_ref[pl.ds(k, 1), :] * pad[pl.ds(SHORT_PAD - (SHORT_K - 1) + k + r0, rc), :]
            y_ref[pl.ds(r0, rc), :] = (b_ref[pl.ds(r0, rc), :] * r).astype(BF16)

    col = lambda j: (0, j)
    return _call(body,name=name,grid=(DS // tc,), in_specs=_short_specs(T, DS, tc) + [pl.BlockSpec((SHORT_K, tc), col)],
                          out_specs=pl.BlockSpec((T, tc), col), out_shape=jax.ShapeDtypeStruct((T, DS), BF16),
                          scratch_shapes=[pltpu.VMEM((T + SHORT_PAD, tc), F32)], sem=("parallel",))(z, z, z, w)


def _short_bwd(dy, z, w, name, tc=256):
    T, DS = dy.shape
    rc = min(CHUNK, T)

    def body(dy_ref, b_ref, cg_ref, u_ref, w_ref, db_ref, dcg_ref, du_ref, dw_ref, qpad, rpad):
        qpad[pl.ds(0, SHORT_PAD), :] = jnp.zeros((SHORT_PAD, tc), F32)
        qpad[pl.ds(SHORT_PAD, T), :] = cg_ref[...] * u_ref[...]
        rpad[pl.ds(0, T), :] = dy_ref[...] * b_ref[...]
        rpad[pl.ds(T, SHORT_PAD), :] = jnp.zeros((SHORT_PAD, tc), F32)
        accs = [jnp.zeros((8, tc), F32) for _ in range(SHORT_K)]
        for r0 in range(0, T, rc):
            r = jnp.zeros((rc, tc), F32)
            dq = jnp.zeros((rc, tc), F32)
            dr = rpad[pl.ds(r0, rc), :]
            for k in range(SHORT_K):
                q_k = qpad[pl.ds(SHORT_PAD - (SHORT_K - 1) + k + r0, rc), :]
                r = r + w_ref[pl.ds(k, 1), :] * q_k
                dq = dq + w_ref[pl.ds(k, 1), :] * rpad[pl.ds(r0 + (SHORT_K - 1) - k, rc), :]
                accs[k] = accs[k] + jnp.sum((dr * q_k).reshape(rc // 8, 8, tc), axis=0)
            db_ref[pl.ds(r0, rc), :] = (dy_ref[pl.ds(r0, rc), :] * r).astype(BF16)
            dcg_ref[pl.ds(r0, rc), :] = (dq * u_ref[pl.ds(r0, rc), :]).astype(BF16)
            du_ref[pl.ds(r0, rc), :] = (dq * cg_ref[pl.ds(r0, rc), :]).astype(BF16)
        for k in range(SHORT_K):
            dw_ref[pl.ds(k, 1), :] = jnp.sum(accs[k], axis=0, keepdims=True)

    col = lambda j: (0, j)
    tile = pl.BlockSpec((T, tc), col)
    return _call(body,name=name,grid=(DS // tc,),
                          in_specs=[tile] + _short_specs(T, DS, tc) + [pl.BlockSpec((SHORT_K, tc), col)],
                          out_specs=[tile, tile, tile, pl.BlockSpec((SHORT_K, tc), col)],
                          out_shape=[jax.ShapeDtypeStruct((T, DS), BF16)] * 3 + [jax.ShapeDtypeStruct((SHORT_K, DS), F32)],
                          scratch_shapes=[pltpu.VMEM((T + SHORT_PAD, tc), F32), pltpu.VMEM((T + SHORT_PAD, tc), F32)],
                          sem=("parallel",))(dy, z, z, z, w)


def _tile_rows(rows, cols, n_bufs):
    budget = VMEM_LIMIT_BYTES * 3 // 4 // (2 * n_bufs * 4 * cols)
    tr = rows
    while tr > budget and tr % 16 == 0:
        tr //= 2
    return tr


def _placed_call(body, name, place, grid, in_specs, out_specs, out_shape, ins):
    return _call(body,prefetch=place, name=name, grid=grid, in_specs=in_specs, out_specs=out_specs, out_shape=out_shape,
                 sem=("parallel",))(*ins)


def _cast_into_full(w, layer, kind, place, name):
    _, R, C = w.shape
    tr = _tile_rows(R, C, 2)
    nb = R // tr
    if kind == "col":
        full, out_spec = (R, C * N_CHIPS), pl.BlockSpec((tr, C), lambda i, s: (i, s[0]))
    else:
        full, out_spec = (R * N_CHIPS, C), pl.BlockSpec((tr, C), lambda i, s: (s[0] * nb + i, 0))

    def body(s_ref, w_ref, o_ref):
        o_ref[...] = w_ref[...].astype(BF16)

    return _placed_call(body, name, place, (nb,), [pl.BlockSpec((None, tr, C), lambda i, s: (layer, i, 0))], out_spec,
                        jax.ShapeDtypeStruct(full, BF16), [w])


def _add_pair(grad, theirs, kind, place, name):
    R, C = grad.shape
    piece_rows = R // 2 if kind == "col" else R // N_CHIPS // 2
    tr = _tile_rows(piece_rows, C, 3)
    nb = piece_rows // tr
    if kind == "col":
        g_spec = pl.BlockSpec((tr, C), lambda i, s: (s[1] * nb + i, 0))
    else:
        g_spec = pl.BlockSpec((tr, C), lambda i, s: ((2 * (i // nb) + s[1]) * nb + i % nb, 0))
    flat = pl.BlockSpec((tr, C), lambda i, s: (i, 0))

    def body(s_ref, a_ref, b_ref, o_ref):
        o_ref[...] = (a_ref[...].astype(F32) + b_ref[...].astype(F32)).astype(BF16)

    return _placed_call(body, name, place, (R // 2 // tr,), [g_spec, flat], flat, jax.ShapeDtypeStruct((R // 2, C), BF16),
                        [grad, theirs])


def _sum_chips(chip_sum, arrived, kind, place, name):
    _, H, W = arrived.shape
    tr = _tile_rows(H, W, 6)
    nb = H // tr
    if kind == "col":
        own_spec = pl.BlockSpec((tr, W), lambda i, s: (i, s[0]))
    else:
        own_spec = pl.BlockSpec((tr, W), lambda i, s: (s[0] * nb + i, 0))

    def body(s_ref, p_ref, r_ref, o_ref):
        acc = p_ref[...].astype(F32)
        for i in range(N_CHIPS - 1):
            acc = acc + r_ref[i].astype(F32)
        o_ref[...] = acc

    return _placed_call(body, name, place, (nb,), [own_spec, pl.BlockSpec((N_CHIPS - 1, tr, W), lambda i, s: (0, i, 0))],
                        pl.BlockSpec((tr, W), lambda i, s: (s[1] * nb + i, 0)), jax.ShapeDtypeStruct((2 * H, W), F32),
                        [chip_sum, arrived])


def _adamw_values(w, g, m, v):
    m = ADAM_B1 * m + (1.0 - ADAM_B1) * g
    v = ADAM_B2 * v + (1.0 - ADAM_B2) * (g * g)
    m_hat = m / (1.0 - ADAM_B1 ** ADAM_STEP)
    v_hat = v / (1.0 - ADAM_B2 ** ADAM_STEP)
    return -ADAM_LR * (m_hat / (jnp.sqrt(v_hat) + ADAM_EPS) + ADAM_WD * w), m, v


def _adamw(w, g, m, v, name, layer=0, carried=None):
    L, R, C = w.shape
    tr = _tile_rows(R, C, 8)

    def body(w_ref, g_ref, m_ref, v_ref, *rest):
        go_ref, d_ref, mo_ref, vo_ref = rest[-4:]
        g_val = g_ref[...]
        d, m_new, v_new = _adamw_values(w_ref[...], g_val, m_ref[...], v_ref[...])
        go_ref[...], d_ref[...], mo_ref[...], vo_ref[...] = g_val, d, m_new, v_new

    lay = pl.BlockSpec((None, tr, C), lambda i: (layer, i, 0))
    ins = [w, g, m, v]
    in_specs = [lay, pl.BlockSpec((tr, C), lambda i: (i, 0)), lay, lay]
    aliases = {}
    if carried is not None:
        ins += list(carried)
        in_specs += [pl.BlockSpec(memory_space=pl.ANY)] * 4
        aliases = {4 + i: i for i in range(4)}
    return _call(body, name=name, grid=(R // tr,), in_specs=in_specs, out_specs=[lay] * 4,
                 out_shape=[jax.ShapeDtypeStruct((L, R, C), F32)] * 4, input_output_aliases=aliases, sem=("parallel",))(*ins)


def _aligned(v, m):
    return v if isinstance(v, int) else pl.multiple_of(v, m)


def _place():
    x, y, c = lax.axis_index("x"), lax.axis_index("y"), lax.axis_index("c")
    other_chips = [(x, 1 - y), (1 - x, y), (1 - x, 1 - y)]
    return x, y, c, 2 * x + y, other_chips


def _chip_index(chip):
    return 2 * chip[0] + chip[1]


def _piece(ref, kind, k, h):
    R, C = ref.shape
    if kind == "col":
        return ref.at[pl.ds(_aligned(h * (R // 2), 16), R // 2), pl.ds(_aligned(k * (C // N_CHIPS), 128), C // N_CHIPS)]
    rs = R // N_CHIPS
    return ref.at[pl.ds(_aligned(k * rs + h * (rs // 2), 16), rs // 2), :]


def _compact_piece(ref, kind, k):
    R2, C = ref.shape
    if kind == "col":
        return ref.at[:, pl.ds(_aligned(k * (C // N_CHIPS), 128), C // N_CHIPS)]
    return ref.at[pl.ds(_aligned(k * (R2 // N_CHIPS), 16), R2 // N_CHIPS), :]


def _half_rows(ref, h):
    R = ref.shape[0]
    return ref.at[pl.ds(_aligned(h * (R // 2), 16), R // 2), :]


class _Copies:
    def __init__(self, send_sems, recv_sems):
        self.send_sems, self.recv_sems = send_sems, recv_sems
        self.n_remote = 0

    def remote(self, src, dst, device):
        k = self.n_remote
        self.n_remote += 1
        return pltpu.make_async_remote_copy(src_ref=src, dst_ref=dst, send_sem=self.send_sems.at[k], recv_sem=self.recv_sems.at[k],
                                            device_id=device, device_id_type=MESH)


class _Job:
    def __init__(self, ins, out_shape, aliases, n_remote, build):
        self.ins, self.out_shape, self.aliases, self.n_remote, self.build = list(ins), list(out_shape), dict(aliases), n_remote, build


class _Flying:
    def __init__(self, job, send_sems, recv_sems, bufs, token):
        self.job, self.send_sems, self.recv_sems, self.bufs, self.token = job, send_sems, recv_sems, bufs, token


def _job_refs(job, buf_refs):
    n_out = len(job.out_shape)
    kept = [i for i in range(len(job.ins)) if i not in job.aliases]
    ins = [buf_refs[job.aliases[i]] if i in job.aliases else buf_refs[n_out + kept.index(i)] for i in range(len(job.ins))]
    return ins, list(buf_refs[:n_out])


def _start_job(job, name, after=()):
    n_in, n_out, n_after = len(job.ins), len(job.out_shape), len(after)
    kept = [i for i in range(n_in) if i not in job.aliases]
    n_bufs = n_out + len(kept)

    def body(*refs):
        in_refs, out_refs = refs[:n_in], refs[n_in + n_after:n_in + n_after + n_out]
        send_sems, recv_sems, token = refs[n_in + n_after + n_bufs:]
        for d in job.build(in_refs, out_refs, _Copies(send_sems, recv_sems)):
            d.start()
        token[...] = jnp.zeros_like(token)

    aliases = dict(job.aliases)
    aliases.update({i: n_out + k for k, i in enumerate(kept)})
    sems = pltpu.SemaphoreType.DMA((job.n_remote,))
    outs = pl.pallas_call(
        body, name=name, in_specs=[HBM] * n_in + [ANY] * n_after,
        out_specs=[HBM] * n_bufs + [SEM, SEM, pl.BlockSpec(memory_space=pltpu.VMEM)],
        out_shape=job.out_shape + [jax.ShapeDtypeStruct(job.ins[i].shape, job.ins[i].dtype) for i in kept]
        + [sems, sems, jax.ShapeDtypeStruct((8, 128), F32)],
        input_output_aliases=aliases,
        compiler_params=pltpu.CompilerParams(has_side_effects=pltpu.SideEffectType.DATAFLOW_SIDE_EFFECTING))(*job.ins, *after)
    return _Flying(job, outs[n_bufs], outs[n_bufs + 1], list(outs[:n_bufs]), outs[n_bufs + 2])


def _wait_job(flying, name, after=()):
    job, n_bufs, n_after = flying.job, len(flying.bufs), len(after)

    def body(*refs):
        in_refs, out_refs = _job_refs(job, refs[:n_bufs])
        send_sems, recv_sems = refs[n_bufs:n_bufs + 2]
        copies = job.build(in_refs, out_refs, _Copies(send_sems, recv_sems))
        for d in copies:
            d.wait_send()
        for d in copies:
            d.wait_recv()

    outs = pl.pallas_call(
        body, name=name, in_specs=[HBM] * n_bufs + [SEM, SEM] + [ANY] * n_after, out_specs=[HBM] * n_bufs,
        out_shape=[jax.ShapeDtypeStruct(b.shape, b.dtype) for b in flying.bufs],
        input_output_aliases={i: i for i in range(n_bufs)},
        compiler_params=pltpu.CompilerParams(has_side_effects=pltpu.SideEffectType.DATAFLOW_SIDE_EFFECTING))(
            *flying.bufs, flying.send_sems, flying.recv_sems, *after)
    return list(outs[:len(job.out_shape)]), list(outs[len(job.out_shape):])


def _in_place(arrays):
    return [jax.ShapeDtypeStruct(a.shape, a.dtype) for a in arrays], {u: u for u in range(len(arrays))}


def _rows_part(ref, part, n_parts):
    h = ref.shape[0] // n_parts
    return ref.at[pl.ds(part * h, h), :]


def _gather_job(full, kind, stage):
    def build(in_refs, out_refs, cp):
        x, y, c, me, (y_nbr, x_nbr, diagonal) = _place()
        (ref,) = out_refs
        sibling = (x, y, 1 - c)
        if stage == 1:
            mine = _piece(ref, kind, me, c)
            return [cp.remote(mine, mine, (*y_nbr, c)), cp.remote(mine, mine, (*x_nbr, c))]
        from_y, from_x = _piece(ref, kind, _chip_index(y_nbr), c), _piece(ref, kind, _chip_index(x_nbr), c)
        if stage == 2:
            relay_0, relay_1 = _rows_part(from_x, 0, 2), _rows_part(from_y, 1, 2)
            return [cp.remote(relay_0, relay_0, (*y_nbr, c)), cp.remote(relay_1, relay_1, (*x_nbr, c)),
                    cp.remote(from_y, from_y, sibling), cp.remote(from_x, from_x, sibling)]
        from_diagonal = _piece(ref, kind, _chip_index(diagonal), c)
        return [cp.remote(from_diagonal, from_diagonal, sibling)]

    return _Job([full], *_in_place([full]), {1: 2, 2: 4, 3: 1}[stage], build)


def _gather_small_job(fulls, axes):
    def build(in_refs, out_refs, cp):
        x, y, c, me, chips = _place()
        copies = []
        for ref, ax in zip(out_refs, axes):
            n = ref.shape[ax] // N_CHIPS
            idx = [slice(None)] * len(ref.shape)
            idx[ax] = pl.ds(_aligned(me * n, n), n)
            mine = ref.at[tuple(idx)]
            copies += [cp.remote(mine, mine, (*chip, c)) for chip in chips]
        return copies

    return _Job(fulls, *_in_place(fulls), 3 * len(fulls), build)


def _exchange_halves_job(grads, kinds):
    def build(in_refs, out_refs, cp):
        x, y, c, me, chips = _place()
        copies = []
        for src, dst, kind in zip(in_refs, out_refs, kinds):
            if kind == "col":
                copies.append(cp.remote(_half_rows(src, 1 - c), dst, (x, y, 1 - c)))
            else:
                copies += [cp.remote(_piece(src, "row", k, 1 - c), _compact_piece(dst, "row", k), (x, y, 1 - c))
                           for k in range(N_CHIPS)]
        return copies

    out_shape = [jax.ShapeDtypeStruct((g.shape[0] // 2, g.shape[1]), g.dtype) for g in grads]
    return _Job(grads, out_shape, {}, sum(1 if k == "col" else N_CHIPS for k in kinds), build)


def _scatter_job(half, kind):
    def build(in_refs, out_refs, cp):
        x, y, c, me, chips = _place()
        (src,), (dst,) = in_refs, out_refs
        return [cp.remote(_compact_piece(src, kind, _chip_index(chip)), dst.at[r], (*chip, c)) for r, chip in enumerate(chips)]

    part_shape = (half.shape[0], half.shape[1] // N_CHIPS) if kind == "col" else (half.shape[0] // N_CHIPS, half.shape[1])
    return _Job([half], [jax.ShapeDtypeStruct((N_CHIPS - 1,) + part_shape, half.dtype)], {}, N_CHIPS - 1, build)


def _share_job(shards):
    def build(in_refs, out_refs, cp):
        x, y, c, me, chips = _place()
        copies = []
        for ref in out_refs:
            mine = _half_rows(ref, c)
            copies.append(cp.remote(mine, mine, (x, y, 1 - c)))
        return copies

    return _Job(shards, *_in_place(shards), len(shards), build)


N_DEVICES = 2 * N_CHIPS


def _small_exchange_job(slots):
    def build(in_refs, out_refs, cp):
        x, y, c, me, chips = _place()
        (ref,) = out_refs
        mine = ref.at[2 * me + c]
        return [cp.remote(mine, mine, (x ^ (p >> 2), y ^ ((p >> 1) & 1), c ^ (p & 1))) for p in range(1, N_DEVICES)]

    return _Job([slots], *_in_place([slots]), N_DEVICES - 1, build)


def _in_own_slot(packed, place, name):
    R, C = packed.shape

    def body(s_ref, p_ref, o_ref):
        o_ref[...] = p_ref[...]

    return _placed_call(body, name, place, (1,), [pl.BlockSpec((R, C), lambda i, s: (0, 0))],
                        pl.BlockSpec((None, R, C), lambda i, s: (2 * s[0] + s[1], 0, 0)),
                        jax.ShapeDtypeStruct((N_DEVICES, R, C), F32), [packed])


def _sum_slots(slots, name):
    n, R, C = slots.shape

    def body(s_ref, o_ref):
        acc = s_ref[0]
        for i in range(1, n):
            acc = acc + s_ref[i]
        o_ref[...] = acc

    return _call(body,name=name, grid=(1,), in_specs=[pl.BlockSpec((n, R, C), lambda i: (0, 0, 0))],
                 out_specs=pl.BlockSpec((R, C), lambda i: (0, 0)), out_shape=jax.ShapeDtypeStruct((R, C), F32),
                 sem=("arbitrary",))(slots)


def _packed_rows(size, width):
    return -(-size // (8 * width)) * 8


def _pack(arrays, width):
    rows = []
    for a in arrays:
        flat = a.reshape(-1)
        n_rows = _packed_rows(flat.shape[0], width)
        rows.append(jnp.pad(flat, (0, n_rows * width - flat.shape[0])).reshape(n_rows, width))
    return jnp.concatenate(rows, axis=0)


def _unpack(packed, shapes):
    out, r0, width = [], 0, packed.shape[1]
    for shape in shapes:
        size = 1
        for d in shape:
            size *= d
        out.append(packed[r0:r0 + _packed_rows(size, width)].reshape(-1)[:size].reshape(shape))
        r0 += _packed_rows(size, width)
    return out


class _Backlog:
    def __init__(self, first):
        self.now, self.free, self.flights, self.last, self.chain = 0.0, {"ici": 0.0, "d2d": 0.0}, [], first, []

    def run(self, fn, us, *args, **kw):
        out = fn(*args, **kw)
        self.now += us
        self.last = out[0] if isinstance(out, (list, tuple)) else out
        self.poll()
        return out

    def start(self, job, name, link, cost, done):
        flying = _start_job(job, name + "_start", self.chain)
        self.chain = [flying.token]
        _Behind.pending.append(flying.token)
        ends = max(self.now, self.free[link]) + cost
        self.free[link] = ends
        self.flights.append((ends + LANDING_SLACK_US, name, flying, done))
        self.flights.sort(key=lambda f: f[0])

    def poll(self, block=False):
        while self.flights and (block or self.flights[0][0] <= self.now):
            ends, name, flying, done = self.flights.pop(0)
            self.now, block = max(self.now, ends), False
            done(*_wait_job(flying, name + "_wait", [self.last] + self.chain))


class _GatherStream:
    def __init__(self, backlog, bufs, kinds, costs):
        self.backlog, self.bufs, self.kinds, self.costs, self.begun, self.complete = backlog, bufs, kinds, costs, 0, set()
        self.begin()

    def begin(self):
        u, self.begun = self.begun, self.begun + 1
        self.backlog.start(_gather_job(self.bufs[u], self.kinds[u], 1), "gather_%d" % u, "ici", 0.5 * self.costs[u],
                           lambda outs, kept: self.arrived(u, outs[0]))

    def arrived(self, u, buf):
        self.bufs[u] = buf
        self.backlog.start(_gather_job(buf, self.kinds[u], 2), "relay_%d" % u, "ici", 0.25 * self.costs[u],
                           lambda outs, kept: self.relayed(u, outs[0]))
        while self.begun <= min(u + GATHER_WINDOW[u], len(self.bufs) - 1):
            self.begin()

    def relayed(self, u, buf):
        self.bufs[u] = buf
        self.backlog.start(_gather_job(buf, self.kinds[u], 3), "handon_%d" % u, "d2d", D2D_SHARE * self.costs[u],
                           lambda outs, kept: self.handed(u, outs[0]))

    def handed(self, u, buf):
        self.bufs[u] = buf
        self.complete.add(u)

    def ready(self, u):
        while u not in self.complete:
            assert self.backlog.flights, "weight %d is not on its way" % u
            self.backlog.poll(block=True)
        return self.bufs[u]


class _GradStream:
    def __init__(self, backlog, u, name, kind, cost, g, place, results):
        self.backlog, self.u, self.name, self.kind, self.cost, self.place, self.results = backlog, u, name, kind, cost, place, results
        backlog.start(_exchange_halves_job([g], [kind]), "to_sibling_" + name, "d2d", D2D_SHARE * cost, self.exchanged)

    def exchanged(self, outs, kept):
        chip_sum = self.backlog.run(_add_pair, SIDE_KERNEL_US, kept[0], outs[0], self.kind, self.place, "chip_sum_" + self.name)
        self.backlog.start(_scatter_job(chip_sum, self.kind), "to_owners_" + self.name, "ici", self.cost, self.scattered)

    def scattered(self, outs, kept):
        reduced = self.backlog.run(_sum_chips, SIDE_KERNEL_US, kept[0], outs[0], self.kind, self.place, "reduce_" + self.name)
        self.backlog.start(_share_job([reduced]), "share_" + self.name, "d2d", D2D_SHARE * self.cost, self.shared)

    def shared(self, outs, kept):
        self.results[self.u] = outs[0]


SIDE_KERNEL_US = 12.0
D2D_SHARE = 0.15
LANDING_SLACK_US = 5.0
GATHER_WINDOW = (1, 1, 1, 2, 2, 1, 1, 1)


def kernel(x, mix_pre_g, mix_post_g, ffn_pre_g, ffn_post_g, ab_w_in, pool_w, pool_scale, conv_w, conv_b, conv_ln_g, conv_ln_b, ab_w_out, sc_w_in, sc_conv_w, sc_w_out, ffn_w1, ffn_w2, loss_target, m_mix_pre_g, m_mix_post_g, m_ffn_pre_g, m_ffn_post_g, m_ab_w_in, m_pool_w, m_pool_scale, m_conv_w, m_conv_b, m_conv_ln_g, m_conv_ln_b, m_ab_w_out, m_sc_w_in, m_sc_conv_w, m_sc_w_out, m_ffn_w1, m_ffn_w2, v_mix_pre_g, v_mix_post_g, v_ffn_pre_g, v_ffn_post_g, v_ab_w_in, v_pool_w, v_pool_scale, v_conv_w, v_conv_b, v_conv_ln_g, v_conv_ln_b, v_ab_w_out, v_sc_w_in, v_sc_conv_w, v_sc_w_out, v_ffn_w1, v_ffn_w2):
    x0, target = x[0], loss_target[0]
    T, D = x0.shape
    DP = pool_scale.shape[-1]
    gain = lambda g, layer: g[layer][None, :]

    big = [("ab_w_in", ab_w_in, 0, "col", 67.0), ("ab_w_out", ab_w_out, 0, "row", 44.0),
           ("ffn_w1_0", ffn_w1, 0, "col", 177.0), ("ffn_w2_0", ffn_w2, 0, "row", 177.0),
           ("sc_w_in", sc_w_in, 0, "col", 133.0), ("sc_w_out", sc_w_out, 0, "row", 44.0),
           ("ffn_w1_1", ffn_w1, 1, "col", 177.0), ("ffn_w2_1", ffn_w2, 1, "row", 177.0)]
    kinds = [b[3] for b in big]
    chip = 2 * lax.axis_index("x") + lax.axis_index("y")
    place = jnp.stack([chip, lax.axis_index("c")]).astype(jnp.int32)

    def own_in_zeros(shard, ax):
        full = jnp.zeros(tuple(d * N_CHIPS if i == ax else d for i, d in enumerate(shard.shape)), shard.dtype)
        return lax.dynamic_update_slice_in_dim(full, shard, chip * shard.shape[ax], axis=ax)

    W = [_cast_into_full(w, layer, kind, place, "cast_" + name) for name, w, layer, kind, _ in big]
    smalls = [own_in_zeros(pool_w[0], 1), own_in_zeros(conv_w[0], 1), own_in_zeros(sc_conv_w[0], 1)]
    backlog = _Backlog(x0)
    run = backlog.run
    small_weights = []
    backlog.start(_gather_small_job(smalls, [1, 1, 1]), "gather_small", "ici", 6.0, lambda outs, kept: small_weights.extend(outs))
    gather = _GatherStream(backlog, W, kinds, [b[4] for b in big])

    relu_sq = lambda acc: (jnp.maximum(acc, 0.0), jnp.square(jnp.maximum(acc, 0.0)))
    relu_sq_bwd = lambda acc, a: (acc * (2.0 * a.astype(F32)),)

    h0 = run(_norm_fwd, 12.0, x0, gain(mix_pre_g, 0), "norm_in")
    z0 = run(_matmul, 35.0, h0, gather.ready(0), "nn", "mix0_in")
    while not small_weights:
        backlog.poll(block=True)
    pool_w_full, conv_w_full, sc_conv_w_full = small_weights
    pooled, y_pool = run(_pool_fwd, 23.0, z0, pool_w_full, pool_scale, "pool_fwd")
    a_conv, c_conv = run(_conv_fwd, 25.0, z0, conv_w_full, conv_b, DP, "conv_fwd")
    y_conv = run(_ln_silu_fwd, 10.0, c_conv, conv_ln_g, conv_ln_b, "ln_silu_fwd")
    y0 = jnp.concatenate([y_pool, y_conv], axis=1)
    m0 = run(_matmul, 25.0, y0, gather.ready(1), "nn", "mix0_out")
    x1, h1 = run(_residual_norm, 21.0, x0, m0, gain(mix_post_g, 0), gain(ffn_pre_g, 0), "res_mix0")
    a0, a0sq = run(_matmul, 81.0, h1, gather.ready(2), "nn", "ffn0_up", out_dtypes=(BF16, BF16), epilogue=relu_sq)
    f0 = run(_matmul, 84.0, a0sq, gather.ready(3), "nn", "ffn0_down", tk=LONG_K_TILE)
    x2, h2 = run(_residual_norm, 22.0, x1, f0, gain(ffn_post_g, 0), gain(mix_pre_g, 1), "res_ffn0")
    z1 = run(_matmul, 62.0, h2, gather.ready(4), "nn", "mix1_in")
    y1 = run(_short_fwd, 22.0, z1, sc_conv_w_full, "short_fwd")
    m1 = run(_matmul, 25.0, y1, gather.ready(5), "nn", "mix1_out")
    x3, h3 = run(_residual_norm, 21.0, x2, m1, gain(mix_post_g, 1), gain(ffn_pre_g, 1), "res_mix1")
    a1, a1sq = run(_matmul, 81.0, h3, gather.ready(6), "nn", "ffn1_up", out_dtypes=(BF16, BF16), epilogue=relu_sq)
    f1 = run(_matmul, 84.0, a1sq, gather.ready(7), "nn", "ffn1_down", tk=LONG_K_TILE)
    w_in0, w_out0, w1_0, w2_0, w_in1, w_out1, w1_1, w2_1 = W

    grads_big = [None] * len(big)


    def reduce_grad(u, g):
        name, _, _, kind, cost = big[u]
        _GradStream(backlog, u, name, kind, cost, g, place, grads_big)

    dx, df1, d_ffn_post_1, loss_row = run(_loss_and_last_norm_bwd, 30.0, x3, f1, gain(ffn_post_g, 1), target, "loss")
    reduce_grad(7, run(_matmul, 80.0, a1sq, df1, "tn", "ffn1_down_dw", out_dtypes=(BF16,)))
    dz = run(_matmul, 82.0, df1, w2_1, "nt", "ffn1_down_dx", out_dtypes=(BF16,), epilogue=relu_sq_bwd, epi=(a1,))
    reduce_grad(6, run(_matmul, 80.0, h3, dz, "tn", "ffn1_up_dw", out_dtypes=(BF16,)))
    dh = run(_matmul, 87.0, dz, w1_1, "nt", "ffn1_up_dx", tk=LONG_K_TILE)
    dx, d_ffn_pre_1, dm1, d_mix_post_1 = run(_norms_bwd, 36.0, dx, dh, x3, gain(ffn_pre_g, 1), m1, gain(mix_post_g, 1), "norms_bwd3")

    reduce_grad(5, run(_matmul, 24.0, y1, dm1, "tn", "mix1_out_dw", out_dtypes=(BF16,)))
    dy1 = run(_matmul, 25.0, dm1, w_out1, "nt", "mix1_out_dx")
    db, dcg, du, d_sc_conv_w = run(_short_bwd, 41.0, dy1, z1, sc_conv_w_full, "short_bwd")
    dz1 = jnp.concatenate([db, dcg, du], axis=1)
    reduce_grad(4, run(_matmul, 62.0, h2, dz1, "tn", "mix1_in_dw", out_dtypes=(BF16,)))
    dh = run(_matmul, 68.0, dz1, w_in1, "nt", "mix1_in_dx")
    dx, d_mix_pre_1, df0, d_ffn_post_0 = run(_norms_bwd, 35.0, dx, dh, x2, gain(mix_pre_g, 1), f0, gain(ffn_post_g, 0), "norms_bwd2")

    reduce_grad(3, run(_matmul, 80.0, a0sq, df0, "tn", "ffn0_down_dw", out_dtypes=(BF16,)))
    dz = run(_matmul, 82.0, df0, w2_0, "nt", "ffn0_down_dx", out_dtypes=(BF16,), epilogue=relu_sq_bwd, epi=(a0,))
    reduce_grad(2, run(_matmul, 80.0, h1, dz, "tn", "ffn0_up_dw", out_dtypes=(BF16,)))
    dh = run(_matmul, 87.0, dz, w1_0, "nt", "ffn0_up_dx", tk=LONG_K_TILE)
    dx, d_ffn_pre_0, dm0, d_mix_post_0 = run(_norms_bwd, 36.0, dx, dh, x1, gain(ffn_pre_g, 0), m0, gain(mix_post_g, 0), "norms_bwd1")

    reduce_grad(1, run(_matmul, 24.0, y0, dm0, "tn", "mix0_out_dw", out_dtypes=(BF16,)))
    dy0 = run(_matmul, 25.0, dm0, w_out0, "nt", "mix0_out_dx")
    du_pool, d_pool_w, d_pool_scale = run(_pool_bwd, 28.0, dy0, pooled, pool_w_full, pool_scale, "pool_bwd")
    dc, d_ln_g, d_ln_b = run(_ln_silu_bwd, 15.0, dy0, c_conv, conv_ln_g, conv_ln_b, "ln_silu_bwd")
    dv, dgate, d_conv_w, d_conv_b = run(_conv_bwd, 52.0, dc, a_conv, z0, conv_w_full, DP, "conv_bwd")
    dz0 = jnp.concatenate([du_pool, dv, dgate], axis=1)

    small_sums = {}

    def exchange_small(key, arrays, cost):
        slots = _in_own_slot(_pack(arrays, D), place, "small_grads_slot_" + key)
        backlog.start(_small_exchange_job(slots), "small_grads_" + key, "ici", cost,
                      lambda outs, kept: small_sums.__setitem__(key, _unpack(_sum_slots(outs[0], "small_grads_sum_" + key),
                                                                             [a.shape for a in arrays])))

    exchange_small("most", [d_mix_pre_1, jnp.concatenate([d_mix_post_0, d_mix_post_1], 0),
                            jnp.concatenate([d_ffn_pre_0, d_ffn_pre_1], 0), jnp.concatenate([d_ffn_post_0, d_ffn_post_1], 0),
                            d_pool_scale, d_conv_b, d_ln_g, d_ln_b, d_pool_w, d_conv_w, d_sc_conv_w], 112.0)
    reduce_grad(0, run(_matmul, 34.0, h0, dz0, "tn", "mix0_in_dw", out_dtypes=(BF16,)))
    dh = run(_matmul, 40.0, dz0, w_in0, "nt", "mix0_in_dx")
    grad_x, d_mix_pre_0 = run(_norms_bwd, 26.0, dx, dh, x0, gain(mix_pre_g, 0), None, None, "norms_bwd0")
    exchange_small("last", [d_mix_pre_0, loss_row], 5.0)

    upd, gr, first = {}, grads_big, {}

    def keep(where, key, outs):
        where[key] = outs
        return outs

    adamw_big = [
        (7, lambda: keep(first, "ffn_w2", _adamw(ffn_w2, gr[7], m_ffn_w2, v_ffn_w2, "adamw_ffn_w2_1", layer=1)), 46.0),
        (6, lambda: keep(first, "ffn_w1", _adamw(ffn_w1, gr[6], m_ffn_w1, v_ffn_w1, "adamw_ffn_w1_1", layer=1)), 46.0),
        (5, lambda: keep(upd, "sc_w_out", _adamw(sc_w_out, gr[5], m_sc_w_out, v_sc_w_out, "adamw_sc_w_out")), 14.0),
        (4, lambda: keep(upd, "sc_w_in", _adamw(sc_w_in, gr[4], m_sc_w_in, v_sc_w_in, "adamw_sc_w_in")), 35.0),
        (3, lambda: keep(upd, "ffn_w2", _adamw(ffn_w2, gr[3], m_ffn_w2, v_ffn_w2, "adamw_ffn_w2_0", layer=0,
                                               carried=first["ffn_w2"])), 46.0),
        (2, lambda: keep(upd, "ffn_w1", _adamw(ffn_w1, gr[2], m_ffn_w1, v_ffn_w1, "adamw_ffn_w1_0", layer=0,
                                               carried=first["ffn_w1"])), 46.0),
        (1, lambda: keep(upd, "ab_w_out", _adamw(ab_w_out, gr[1], m_ab_w_out, v_ab_w_out, "adamw_ab_w_out")), 14.0),
        (0, lambda: keep(upd, "ab_w_in", _adamw(ab_w_in, gr[0], m_ab_w_in, v_ab_w_in, "adamw_ab_w_in")), 19.0)]
    while adamw_big or backlog.flights:
        due = [a for a in adamw_big if gr[a[0]] is not None]
        if due:
            adamw_big.remove(due[0])
            backlog.run(due[0][1], due[0][2])
        else:
            backlog.poll(block=True)

    (g_mix_pre_1, g_mix_post, g_ffn_pre, g_ffn_post, g_pool_scale, g_conv_b, g_ln_g, g_ln_b, g_pool_w_full, g_conv_w_full,
     g_sc_conv_w_full) = small_sums["most"]
    g_mix_pre = jnp.concatenate([small_sums["last"][0], g_mix_pre_1], 0)
    loss = small_sums["last"][1][0, 0]
    own = lambda a, ax: lax.dynamic_slice_in_dim(a, chip * (a.shape[ax] // N_CHIPS), a.shape[ax] // N_CHIPS, axis=ax)
    g_pool_w, g_conv_w, g_sc_conv_w = own(g_pool_w_full, 1), own(g_conv_w_full, 1), own(g_sc_conv_w_full, 1)

    def small_update(w, g, m, v, name):
        shape = w.shape
        as3 = lambda a: a.reshape((1, -1, shape[-1]))
        outs = _adamw(as3(w), g.reshape((-1, shape[-1])), as3(m), as3(v), "adamw_" + name)
        return [o.reshape(shape) for o in outs]

    upd["mix_pre_g"] = small_update(mix_pre_g, g_mix_pre, m_mix_pre_g, v_mix_pre_g, "mix_pre_g")
    upd["mix_post_g"] = small_update(mix_post_g, g_mix_post, m_mix_post_g, v_mix_post_g, "mix_post_g")
    upd["ffn_pre_g"] = small_update(ffn_pre_g, g_ffn_pre, m_ffn_pre_g, v_ffn_pre_g, "ffn_pre_g")
    upd["ffn_post_g"] = small_update(ffn_post_g, g_ffn_post, m_ffn_post_g, v_ffn_post_g, "ffn_post_g")
    upd["pool_w"] = small_update(pool_w, g_pool_w, m_pool_w, v_pool_w, "pool_w")
    upd["pool_scale"] = small_update(pool_scale, g_pool_scale, m_pool_scale, v_pool_scale, "pool_scale")
    upd["conv_w"] = small_update(conv_w, g_conv_w, m_conv_w, v_conv_w, "conv_w")
    upd["conv_b"] = small_update(conv_b, g_conv_b, m_conv_b, v_conv_b, "conv_b")
    upd["conv_ln_g"] = small_update(conv_ln_g, g_ln_g, m_conv_ln_g, v_conv_ln_g, "conv_ln_g")
    upd["conv_ln_b"] = small_update(conv_ln_b, g_ln_b, m_conv_ln_b, v_conv_ln_b, "conv_ln_b")
    upd["sc_conv_w"] = small_update(sc_conv_w, g_sc_conv_w, m_sc_conv_w, v_sc_conv_w, "sc_conv_w")

    order = ["mix_pre_g", "mix_post_g", "ffn_pre_g", "ffn_post_g", "ab_w_in", "pool_w", "pool_scale", "conv_w", "conv_b",
             "conv_ln_g", "conv_ln_b", "ab_w_out", "sc_w_in", "sc_conv_w", "sc_w_out", "ffn_w1", "ffn_w2"]
    out = [loss, grad_x[None]]
    for part in range(4):
        out += [upd[n][part] for n in order]
    return tuple(out)
```

```python
import jax
import jax.numpy as jnp
from jax import lax
from jax.experimental import pallas as pl
from jax.experimental.pallas import tpu as pltpu

F32, BF16 = jnp.float32, jnp.bfloat16
EPS = 1e-6
N_GROUPS = 4
MAX_WINDOW = 16
CONV_K = 31
SHORT_K = 3
CONV_PAD = 32
SHORT_PAD = 8
ADAM_LR, ADAM_B1, ADAM_B2, ADAM_EPS, ADAM_WD, ADAM_STEP = 0.001, 0.9, 0.999, 1e-08, 0.01, 10
N_CHIPS = 4
VMEM_LIMIT_BYTES = 56 * 1024 * 1024
ROW_TILE = 256
CHUNK = 256
LONG_K_TILE = 4096
MESH = pl.DeviceIdType.MESH
HBM = pl.BlockSpec(memory_space=pltpu.HBM)
SEM = pl.BlockSpec(memory_space=pltpu.SEMAPHORE)
ANY = pl.BlockSpec(memory_space=pl.ANY)


def _cp(*sem):
    return pltpu.CompilerParams(dimension_semantics=sem, vmem_limit_bytes=VMEM_LIMIT_BYTES)


def _sigmoid(v):
    return 1.0 / (1.0 + jnp.exp(-v))


class _Behind:
    pending = []


def _call(body, prefetch=None, **kw):
    behind, _Behind.pending = _Behind.pending, []
    single = not isinstance(kw["out_shape"], (list, tuple))
    in_specs, scratch = list(kw["in_specs"]), list(kw.get("scratch_shapes", ()))
    out_shape = [kw["out_shape"]] if single else list(kw["out_shape"])
    out_specs = [kw["out_specs"]] if single else list(kw["out_specs"])
    n_pre = 0 if prefetch is None else 1
    n_own, n_behind = len(in_specs), len(behind)

    def wrapped(*refs):
        body(*refs[:n_pre + n_own], *refs[n_pre + n_own + n_behind:])

    specs = dict(grid=kw["grid"], in_specs=in_specs + [ANY] * n_behind, out_specs=out_specs)
    if prefetch is None:
        specs["scratch_shapes"] = scratch
    else:
        specs = dict(grid_spec=pltpu.PrefetchScalarGridSpec(num_scalar_prefetch=1, scratch_shapes=scratch, **specs))
    aliases = {n_pre + i: o for i, o in kw.get("input_output_aliases", {}).items()}
    call = pl.pallas_call(wrapped, name=kw["name"], out_shape=out_shape, input_output_aliases=aliases,
                          compiler_params=_cp(*kw["sem"]), **specs)

    def run(*args):
        outs = call(*([prefetch] * n_pre), *args, *behind)
        return outs[0] if single else list(outs)

    return run


_DIMS = {"nn": (((1,), (0,)), ((), ())), "nt": (((1,), (1,)), ((), ())), "tn": (((0,), (0,)), ((), ()))}


def _pick(n, cap, step=256):
    if n <= cap:
        return n
    return next(t for t in range(cap - cap % step, 0, -step) if n % t == 0)


def _matmul(a, b, mode, name, out_dtypes=(F32,), epilogue=None, epi=(), tm=1024, tn=1024, tk=2048):
    if mode == "tn":
        (K, M), (K2, N) = a.shape, b.shape
    elif mode == "nt":
        (M, K), (N, K2) = a.shape, b.shape
    else:
        (M, K), (K2, N) = a.shape, b.shape
    assert K == K2
    tm, tn, tk = _pick(M, tm), _pick(N, tn), _pick(K, tk)
    nk = K // tk
    a_spec = pl.BlockSpec((tk, tm), lambda i, j, k: (k, i)) if mode == "tn" else pl.BlockSpec((tm, tk), lambda i, j, k: (i, k))
    b_spec = pl.BlockSpec((tn, tk), lambda i, j, k: (j, k)) if mode == "nt" else pl.BlockSpec((tk, tn), lambda i, j, k: (k, j))
    o_spec = pl.BlockSpec((tm, tn), lambda i, j, k: (i, j))
    n_epi, n_out = len(epi), len(out_dtypes)

    def body(a_ref, b_ref, *rest):
        epi_refs, out_refs, scratch = rest[:n_epi], rest[n_epi:n_epi + n_out], rest[n_epi + n_out:]
        part = lax.dot_general(a_ref[...].astype(BF16), b_ref[...].astype(BF16), _DIMS[mode], preferred_element_type=F32)

        def finish(acc):
            outs = epilogue(acc, *[r[...] for r in epi_refs]) if epilogue else (acc,)
            for o_ref, o in zip(out_refs, outs):
                o_ref[...] = o.astype(o_ref.dtype)

        if nk == 1:
            finish(part)
        else:
            acc_ref = scratch[0]
            k = pl.program_id(2)

            @pl.when(k == 0)
            def _():
                acc_ref[...] = part

            @pl.when(k > 0)
            def _():
                acc_ref[...] += part

            @pl.when(k == nk - 1)
            def _():
                finish(acc_ref[...])

    outs = _call(
        body, name=name, grid=(M // tm, N // tn, nk),
        in_specs=[a_spec, b_spec] + [o_spec] * n_epi, out_specs=[o_spec] * n_out,
        out_shape=[jax.ShapeDtypeStruct((M, N), dt) for dt in out_dtypes],
        scratch_shapes=[pltpu.VMEM((tm, tn), F32)] if nk > 1 else [],
        sem=("parallel", "parallel", "arbitrary"))(a, b, *epi)
    return outs[0] if n_out == 1 else outs


def _matmul_by_chip_rows(a, b, place, name, rest=None, tm=1024, tn=1024):
    (M, K), N = a.shape, b.shape[1]
    tm, tn, tk = _pick(M, tm), _pick(N, tn), K // N_CHIPS
    nk = N_CHIPS - 1 if rest is None else 1

    def block(k, s):
        diagonal = N_CHIPS - 1 - s[0]
        return k + (k >= diagonal).astype(jnp.int32) if rest is None else diagonal

    def body(s_ref, a_ref, b_ref, *refs):
        o_ref, acc_ref = refs[-2:]
        part = jnp.dot(a_ref[...], b_ref[...], preferred_element_type=F32)
        k = pl.program_id(2)

        @pl.when(k == 0)
        def _():
            acc_ref[...] = part if rest is None else part + refs[0][...]

        @pl.when(k > 0)
        def _():
            acc_ref[...] += part

        @pl.when(k == nk - 1)
        def _():
            o_ref[...] = acc_ref[...]

    tile = pl.BlockSpec((tm, tn), lambda i, j, k, s: (i, j))
    in_specs = [pl.BlockSpec((tm, tk), lambda i, j, k, s: (i, block(k, s))), pl.BlockSpec((tk, tn), lambda i, j, k, s: (block(k, s), j))]
    return _call(body, prefetch=place, name=name, grid=(M // tm, N // tn, nk), in_specs=in_specs + ([] if rest is None else [tile]),
                 out_specs=tile, out_shape=jax.ShapeDtypeStruct((M, N), F32), scratch_shapes=[pltpu.VMEM((tm, tn), F32)],
                 sem=("parallel", "parallel", "arbitrary"))(a, b, *([] if rest is None else [rest]))


def _rms(x, g):
    r = lax.rsqrt(jnp.mean(x * x, axis=-1, keepdims=True) + EPS)
    return x * r * g


def _rms_bwd(dy, x, g):
    r = lax.rsqrt(jnp.mean(x * x, axis=-1, keepdims=True) + EPS)
    xn = x * r
    dyg = dy * g
    dx = r * (dyg - xn * jnp.mean(dyg * xn, axis=-1, keepdims=True))
    return dx, jnp.sum(dy * xn, axis=0, keepdims=True)


def _rows(d, tr=ROW_TILE):
    return pl.BlockSpec((tr, d), lambda i: (i, 0))


def _vec(d):
    return pl.BlockSpec((1, d), lambda i: (0, 0))


def _accumulate(ref, val):
    @pl.when(pl.program_id(0) == 0)
    def _():
        ref[...] = val

    @pl.when(pl.program_id(0) > 0)
    def _():
        ref[...] += val


def _norm_fwd(x, g, name):
    T, D = x.shape

    def body(x_ref, g_ref, h_ref):
        h_ref[...] = _rms(x_ref[...], g_ref[...]).astype(BF16)

    return _call(body,name=name,grid=(T // ROW_TILE,), in_specs=[_rows(D), _vec(D)], out_specs=_rows(D),
                          out_shape=jax.ShapeDtypeStruct((T, D), BF16), sem=("parallel",))(x, g)


def _residual_norm(x, m, g_post, g_next, name):
    T, D = x.shape

    def body(x_ref, m_ref, gp_ref, gn_ref, xo_ref, h_ref):
        xo = x_ref[...] + _rms(m_ref[...], gp_ref[...])
        xo_ref[...] = xo
        h_ref[...] = _rms(xo, gn_ref[...]).astype(BF16)

    return _call(body,name=name,grid=(T // ROW_TILE,), in_specs=[_rows(D), _rows(D), _vec(D), _vec(D)],
                          out_specs=[_rows(D), _rows(D)],
                          out_shape=[jax.ShapeDtypeStruct((T, D), F32), jax.ShapeDtypeStruct((T, D), BF16)],
                          sem=("parallel",))(x, m, g_post, g_next)


def _loss_and_last_norm_bwd(x, m, g_post, target, name):
    T, D = x.shape

    def body(x_ref, m_ref, gp_ref, t_ref, dx_ref, dm_ref, dg_ref, loss_ref):
        m_val, gp = m_ref[...], gp_ref[...]
        err = x_ref[...] + _rms(m_val, gp) - t_ref[...]
        dx = err * (1.0 / D)
        dx_ref[...] = dx
        dm, dg = _rms_bwd(dx, m_val, gp)
        dm_ref[...] = dm.astype(BF16)
        _accumulate(dg_ref, dg)
        _accumulate(loss_ref, jnp.full((1, 128), 0.5 * jnp.sum(err * err) * (1.0 / D), F32))

    return _call(body,name=name,grid=(T // ROW_TILE,), in_specs=[_rows(D), _rows(D), _vec(D), _rows(D)],
                          out_specs=[_rows(D), _rows(D), _vec(D), _vec(128)],
                          out_shape=[jax.ShapeDtypeStruct((T, D), F32), jax.ShapeDtypeStruct((T, D), BF16),
                                     jax.ShapeDtypeStruct((1, D), F32), jax.ShapeDtypeStruct((1, 128), F32)],
                          sem=("arbitrary",))(x, m, g_post, target)


def _norms_bwd(dx, dh, x_in, g_pre, m_prev, g_post_prev, name):
    T, D = dx.shape
    with_prev = m_prev is not None

    def body(*refs):
        if with_prev:
            dx_ref, dh_ref, x_ref, gq_ref, m_ref, gp_ref, dxo_ref, dgq_ref, dm_ref, dgp_ref = refs
        else:
            dx_ref, dh_ref, x_ref, gq_ref, dxo_ref, dgq_ref = refs
        d_in, dgq = _rms_bwd(dh_ref[...], x_ref[...], gq_ref[...])
        dxo = dx_ref[...] + d_in
        dxo_ref[...] = dxo
        _accumulate(dgq_ref, dgq)
        if with_prev:
            dm, dgp = _rms_bwd(dxo, m_ref[...], gp_ref[...])
            dm_ref[...] = dm.astype(BF16)
            _accumulate(dgp_ref, dgp)

    ins, in_specs = [dx, dh, x_in, g_pre], [_rows(D), _rows(D), _rows(D), _vec(D)]
    out_specs = [_rows(D), _vec(D)]
    out_shape = [jax.ShapeDtypeStruct((T, D), F32), jax.ShapeDtypeStruct((1, D), F32)]
    if with_prev:
        ins += [m_prev, g_post_prev]
        in_specs += [_rows(D), _vec(D)]
        out_specs += [_rows(D), _vec(D)]
        out_shape += [jax.ShapeDtypeStruct((T, D), BF16), jax.ShapeDtypeStruct((1, D), F32)]
    return _call(body,name=name,grid=(T // ROW_TILE,), in_specs=in_specs, out_specs=out_specs, out_shape=out_shape,
                          sem=("arbitrary",))(*ins)


def _window_weights(g):
    w = 2 << g
    return w, [jnp.where(j < w, 1.0, 0.0).astype(F32) for j in range(MAX_WINDOW)]


def _valid_count(r0, rows, w):
    t = (lax.broadcasted_iota(jnp.int32, (rows, 1), 0) + (r0 + 1)).astype(F32)
    return jnp.minimum(t, w.astype(F32))


def _pool_fwd(z, pool_w, pool_scale, name):
    T = z.shape[0]
    PG = pool_w.shape[-1]
    DP = N_GROUPS * PG
    rc = min(CHUNK, T)

    def body(u_ref, pw_ref, sc_ref, pooled_ref, y_ref, pad):
        w, wts = _window_weights(pl.program_id(0))
        pad[pl.ds(0, MAX_WINDOW), :] = jnp.zeros((MAX_WINDOW, PG), F32)
        pad[pl.ds(MAX_WINDOW, T), :] = u_ref[...]
        for r0 in range(0, T, rc):
            acc = jnp.zeros((rc, PG), F32)
            for j in range(MAX_WINDOW):
                acc = acc + wts[j] * pad[pl.ds(MAX_WINDOW + r0 - j, rc), :]
            pooled = acc / _valid_count(r0, rc, w) - u_ref[pl.ds(r0, rc), :]
            pooled_ref[pl.ds(r0, rc), :] = pooled.astype(BF16)
        mixed = jnp.dot(pooled_ref[...], pw_ref[...].astype(BF16), preferred_element_type=F32)
        y_ref[...] = (mixed * sc_ref[...]).astype(BF16)

    col = lambda g: (0, g)
    return _call(
        body, name=name,grid=(N_GROUPS,),
        in_specs=[pl.BlockSpec((T, PG), col), pl.BlockSpec((None, PG, PG), lambda g: (g, 0, 0)), pl.BlockSpec((1, PG), col)],
        out_specs=[pl.BlockSpec((T, PG), col), pl.BlockSpec((T, PG), col)],
        out_shape=[jax.ShapeDtypeStruct((T, DP), BF16), jax.ShapeDtypeStruct((T, DP), BF16)],
        scratch_shapes=[pltpu.VMEM((T + MAX_WINDOW, PG), F32)], sem=("parallel",))(z, pool_w, pool_scale)


def _pool_bwd(dy, pooled, pool_w, pool_scale, name):
    T = dy.shape[0]
    PG = pool_w.shape[-1]
    DP = N_GROUPS * PG
    rc = min(CHUNK, T)

    def body(dy_ref, pooled_ref, pw_ref, sc_ref, du_ref, dpw_ref, dsc_ref, pad, dp_ref):
        w, wts = _window_weights(pl.program_id(0))
        pooled_v, pw = pooled_ref[...], pw_ref[...].astype(BF16)
        dy_v = dy_ref[...]
        mixed = jnp.dot(pooled_v, pw, preferred_element_type=F32)
        dsc_ref[...] = jnp.sum(dy_v * mixed, axis=0, keepdims=True)
        dmixed = (dy_v * sc_ref[...]).astype(BF16)
        dpw_ref[...] = lax.dot_general(pooled_v, dmixed, _DIMS["tn"], preferred_element_type=F32)
        dp_ref[...] = lax.dot_general(dmixed, pw, _DIMS["nt"], preferred_element_type=F32)
        pad[pl.ds(T, MAX_WINDOW), :] = jnp.zeros((MAX_WINDOW, PG), F32)
        for r0 in range(0, T, rc):
            pad[pl.ds(r0, rc), :] = dp_ref[pl.ds(r0, rc), :] / _valid_count(r0, rc, w)
        for r0 in range(0, T, rc):
            acc = jnp.zeros((rc, PG), F32)
            for j in range(MAX_WINDOW):
                acc = acc + wts[j] * pad[pl.ds(r0 + j, rc), :]
            du_ref[pl.ds(r0, rc), :] = (acc - dp_ref[pl.ds(r0, rc), :]).astype(BF16)

    col = lambda g: (0, g)
    return _call(
        body, name=name,grid=(N_GROUPS,),
        in_specs=[pl.BlockSpec((T, PG), col), pl.BlockSpec((T, PG), col), pl.BlockSpec((None, PG, PG), lambda g: (g, 0, 0)),
                  pl.BlockSpec((1, PG), col)],
        out_specs=[pl.BlockSpec((T, PG), col), pl.BlockSpec((None, PG, PG), lambda g: (g, 0, 0)), pl.BlockSpec((1, PG), col)],
        out_shape=[jax.ShapeDtypeStruct((T, DP), BF16), jax.ShapeDtypeStruct((N_GROUPS, PG, PG), F32),
                   jax.ShapeDtypeStruct((1, DP), F32)],
        scratch_shapes=[pltpu.VMEM((T + MAX_WINDOW, PG), F32), pltpu.VMEM((T, PG), F32)],
        sem=("parallel",))(dy, pooled, pool_w, pool_scale)


def _conv_fwd(z, conv_w, conv_b, d_pool, name, tc=128):
    T = z.shape[0]
    DC = conv_w.shape[-1]
    rc = min(CHUNK, T)
    v0, g0 = d_pool // tc, (d_pool + DC) // tc

    def body(v_ref, gt_ref, w_ref, b_ref, a_ref, c_ref, pad):
        pad[pl.ds(0, CONV_PAD), :] = jnp.zeros((CONV_PAD, tc), F32)
        for r0 in range(0, T, rc):
            a = v_ref[pl.ds(r0, rc), :] * _sigmoid(gt_ref[pl.ds(r0, rc), :])
            a_ref[pl.ds(r0, rc), :] = a
            pad[pl.ds(CONV_PAD + r0, rc), :] = a
        for r0 in range(0, T, rc):
            acc = jnp.zeros((rc, tc), F32) + b_ref[...]
            for k in range(CONV_K):
                acc = acc + w_ref[pl.ds(k, 1), :] * pad[pl.ds(CONV_PAD - (CONV_K - 1) + k + r0, rc), :]
            c_ref[pl.ds(r0, rc), :] = acc

    col = lambda j: (0, j)
    return _call(
        body, name=name,grid=(DC // tc,),
        in_specs=[pl.BlockSpec((T, tc), lambda j: (0, v0 + j)), pl.BlockSpec((T, tc), lambda j: (0, g0 + j)),
                  pl.BlockSpec((CONV_K, tc), col), pl.BlockSpec((1, tc), col)],
        out_specs=[pl.BlockSpec((T, tc), col), pl.BlockSpec((T, tc), col)],
        out_shape=[jax.ShapeDtypeStruct((T, DC), F32), jax.ShapeDtypeStruct((T, DC), F32)],
        scratch_shapes=[pltpu.VMEM((T + CONV_PAD, tc), F32)], sem=("parallel",))(z, z, conv_w, conv_b)


def _conv_bwd(dc, a, z, conv_w, d_pool, name, tc=128):
    T, DC = dc.shape
    rc = min(CHUNK, T)
    v0, g0 = d_pool // tc, (d_pool + DC) // tc

    def body(dc_ref, a_ref, v_ref, gt_ref, w_ref, dv_ref, dg_ref, dw_ref, db_ref, apad, dpad):
        apad[pl.ds(0, CONV_PAD), :] = jnp.zeros((CONV_PAD, tc), F32)
        apad[pl.ds(CONV_PAD, T), :] = a_ref[...]
        dpad[pl.ds(0, T), :] = dc_ref[...]
        dpad[pl.ds(T, CONV_PAD), :] = jnp.zeros((CONV_PAD, tc), F32)
        db_ref[...] = jnp.sum(dc_ref[...], axis=0, keepdims=True)
        for k in range(CONV_K):
            acc = jnp.zeros((8, tc), F32)
            for r0 in range(0, T, rc):
                prod = dc_ref[pl.ds(r0, rc), :] * apad[pl.ds(CONV_PAD - (CONV_K - 1) + k + r0, rc), :]
                acc = acc + jnp.sum(prod.reshape(rc // 8, 8, tc), axis=0)
            dw_ref[pl.ds(k, 1), :] = jnp.sum(acc, axis=0, keepdims=True)
        for r0 in range(0, T, rc):
            da = jnp.zeros((rc, tc), F32)
            for k in range(CONV_K):
                da = da + w_ref[pl.ds(k, 1), :] * dpad[pl.ds(r0 + (CONV_K - 1) - k, rc), :]
            sig = _sigmoid(gt_ref[pl.ds(r0, rc), :])
            dv_ref[pl.ds(r0, rc), :] = (da * sig).astype(BF16)
            dg_ref[pl.ds(r0, rc), :] = (da * v_ref[pl.ds(r0, rc), :] * sig * (1.0 - sig)).astype(BF16)

    col = lambda j: (0, j)
    return _call(
        body, name=name,grid=(DC // tc,),
        in_specs=[pl.BlockSpec((T, tc), col), pl.BlockSpec((T, tc), col), pl.BlockSpec((T, tc), lambda j: (0, v0 + j)),
                  pl.BlockSpec((T, tc), lambda j: (0, g0 + j)), pl.BlockSpec((CONV_K, tc), col)],
        out_specs=[pl.BlockSpec((T, tc), col), pl.BlockSpec((T, tc), col), pl.BlockSpec((CONV_K, tc), col),
                   pl.BlockSpec((1, tc), col)],
        out_shape=[jax.ShapeDtypeStruct((T, DC), BF16), jax.ShapeDtypeStruct((T, DC), BF16),
                   jax.ShapeDtypeStruct((CONV_K, DC), F32), jax.ShapeDtypeStruct((1, DC), F32)],
        scratch_shapes=[pltpu.VMEM((T + CONV_PAD, tc), F32), pltpu.VMEM((T + CONV_PAD, tc), F32)],
        sem=("parallel",))(dc, a, z, z, conv_w)


def _layer_norm_parts(c, g, b):
    mu = jnp.mean(c, axis=-1, keepdims=True)
    xc = c - mu
    rstd = lax.rsqrt(jnp.mean(xc * xc, axis=-1, keepdims=True) + EPS)
    xhat = xc * rstd
    return xhat, rstd, xhat * g + b


def _ln_silu_fwd(c, g, b, name):
    T, DC = c.shape

    def body(c_ref, g_ref, b_ref, y_ref):
        _, _, ln = _layer_norm_parts(c_ref[...], g_ref[...], b_ref[...])
        y_ref[...] = (ln * _sigmoid(ln)).astype(BF16)

    return _call(body,name=name,grid=(T // ROW_TILE,), in_specs=[_rows(DC), _vec(DC), _vec(DC)], out_specs=_rows(DC),
                          out_shape=jax.ShapeDtypeStruct((T, DC), BF16), sem=("parallel",))(c, g, b)


def _ln_silu_bwd(dy, c, g, b, name):
    T, DC = c.shape

    def body(dy_ref, c_ref, g_ref, b_ref, dc_ref, dg_ref, db_ref):
        gain = g_ref[...]
        xhat, rstd, ln = _layer_norm_parts(c_ref[...], gain, b_ref[...])
        s = _sigmoid(ln)
        dln = dy_ref[...] * (s * (1.0 + ln * (1.0 - s)))
        _accumulate(dg_ref, jnp.sum(dln * xhat, axis=0, keepdims=True))
        _accumulate(db_ref, jnp.sum(dln, axis=0, keepdims=True))
        dxh = dln * gain
        dc_ref[...] = rstd * (dxh - jnp.mean(dxh, axis=-1, keepdims=True) - xhat * jnp.mean(dxh * xhat, axis=-1, keepdims=True))

    return _call(body,name=name,grid=(T // ROW_TILE,),
                          in_specs=[pl.BlockSpec((ROW_TILE, DC), lambda i: (i, 1)), _rows(DC), _vec(DC), _vec(DC)],
                          out_specs=[_rows(DC), _vec(DC), _vec(DC)],
                          out_shape=[jax.ShapeDtypeStruct((T, DC), F32), jax.ShapeDtypeStruct((1, DC), F32),
                                     jax.ShapeDtypeStruct((1, DC), F32)],
                          sem=("arbitrary",))(dy, c, g, b)


def _short_specs(T, DS, tc):
    n = DS // tc
    return [pl.BlockSpec((T, tc), lambda j: (0, j)), pl.BlockSpec((T, tc), lambda j: (0, n + j)),
            pl.BlockSpec((T, tc), lambda j: (0, 2 * n + j))]


def _short_fwd(z, w, name, tc=256):
    T = z.shape[0]
    DS = w.shape[-1]
    rc = min(CHUNK, T)

    def body(b_ref, cg_ref, u_ref, w_ref, y_ref, pad):
        pad[pl.ds(0, SHORT_PAD), :] = jnp.zeros((SHORT_PAD, tc), F32)
        pad[pl.ds(SHORT_PAD, T), :] = cg_ref[...] * u_ref[...]
        for r0 in range(0, T, rc):
            r = jnp.zeros((rc, tc), F32)
            for k in range(SHORT_K):
                r = r + w_ref[pl.ds(k, 1), :] * pad[pl.ds(SHORT_PAD - (SHORT_K - 1) + k + r0, rc), :]
            y_ref[pl.ds(r0, rc), :] = (b_ref[pl.ds(r0, rc), :] * r).astype(BF16)

    col = lambda j: (0, j)
    return _call(body,name=name,grid=(DS // tc,), in_specs=_short_specs(T, DS, tc) + [pl.BlockSpec((SHORT_K, tc), col)],
                          out_specs=pl.BlockSpec((T, tc), col), out_shape=jax.ShapeDtypeStruct((T, DS), BF16),
                          scratch_shapes=[pltpu.VMEM((T + SHORT_PAD, tc), F32)], sem=("parallel",))(z, z, z, w)


def _short_bwd(dy, z, w, name, tc=256):
    T, DS = dy.shape
    rc = min(CHUNK, T)

    def body(dy_ref, b_ref, cg_ref, u_ref, w_ref, db_ref, dcg_ref, du_ref, dw_ref, qpad, rpad):
        qpad[pl.ds(0, SHORT_PAD), :] = jnp.zeros((SHORT_PAD, tc), F32)
        qpad[pl.ds(SHORT_PAD, T), :] = cg_ref[...] * u_ref[...]
        rpad[pl.ds(0, T), :] = dy_ref[...] * b_ref[...]
        rpad[pl.ds(T, SHORT_PAD), :] = jnp.zeros((SHORT_PAD, tc), F32)
        accs = [jnp.zeros((8, tc), F32) for _ in range(SHORT_K)]
        for r0 in range(0, T, rc):
            r = jnp.zeros((rc, tc), F32)
            dq = jnp.zeros((rc, tc), F32)
            dr = rpad[pl.ds(r0, rc), :]
            for k in range(SHORT_K):
                q_k = qpad[pl.ds(SHORT_PAD - (SHORT_K - 1) + k + r0, rc), :]
                r = r + w_ref[pl.ds(k, 1), :] * q_k
                dq = dq + w_ref[pl.ds(k, 1), :] * rpad[pl.ds(r0 + (SHORT_K - 1) - k, rc), :]
                accs[k] = accs[k] + jnp.sum((dr * q_k).reshape(rc // 8, 8, tc), axis=0)
            db_ref[pl.ds(r0, rc), :] = (dy_ref[pl.ds(r0, rc), :] * r).astype(BF16)
            dcg_ref[pl.ds(r0, rc), :] = (dq * u_ref[pl.ds(r0, rc), :]).astype(BF16)
            du_ref[pl.ds(r0, rc), :] = (dq * cg_ref[pl.ds(r0, rc), :]).astype(BF16)
        for k in range(SHORT_K):
            dw_ref[pl.ds(k, 1), :] = jnp.sum(accs[k], axis=0, keepdims=True)

    col = lambda j: (0, j)
    tile = pl.BlockSpec((T, tc), col)
    return _call(body,name=name,grid=(DS // tc,),
                          in_specs=[tile] + _short_specs(T, DS, tc) + [pl.BlockSpec((SHORT_K, tc), col)],
                          out_specs=[tile, tile, tile, pl.BlockSpec((SHORT_K, tc), col)],
                          out_shape=[jax.ShapeDtypeStruct((T, DS), BF16)] * 3 + [jax.ShapeDtypeStruct((SHORT_K, DS), F32)],
                          scratch_shapes=[pltpu.VMEM((T + SHORT_PAD, tc), F32), pltpu.VMEM((T + SHORT_PAD, tc), F32)],
                          sem=("parallel",))(dy, z, z, z, w)


def _tile_rows(rows, cols, n_bufs):
    budget = VMEM_LIMIT_BYTES * 3 // 4 // (2 * n_bufs * 4 * cols)
    tr = rows
    while tr > budget and tr % 16 == 0:
        tr //= 2
    return tr


def _placed_call(body, name, place, grid, in_specs, out_specs, out_shape, ins):
    return _call(body,prefetch=place, name=name, grid=grid, in_specs=in_specs, out_specs=out_specs, out_shape=out_shape,
                 sem=("parallel",))(*ins)


def _cast_into_full(w, layer, kind, place, name):
    _, R, C = w.shape
    tr = _tile_rows(R, C, 2)
    nb = R // tr
    if kind == "col":
        full, out_spec = (R, C * N_CHIPS), pl.BlockSpec((tr, C), lambda i, s: (i, s[0]))
    else:
        full, out_spec = (R * N_CHIPS, C), pl.BlockSpec((tr, C), lambda i, s: (s[0] * nb + i, 0))

    def body(s_ref, w_ref, o_ref):
        o_ref[...] = w_ref[...].astype(BF16)

    return _placed_call(body, name, place, (nb,), [pl.BlockSpec((None, tr, C), lambda i, s: (layer, i, 0))], out_spec,
                        jax.ShapeDtypeStruct(full, BF16), [w])


def _add_pair(grad, theirs, kind, place, name):
    R, C = grad.shape
    piece_rows = R // 2 if kind == "col" else R // N_CHIPS // 2
    tr = _tile_rows(piece_rows, C, 3)
    nb = piece_rows // tr
    if kind == "col":
        g_spec = pl.BlockSpec((tr, C), lambda i, s: (s[1] * nb + i, 0))
    else:
        g_spec = pl.BlockSpec((tr, C), lambda i, s: ((2 * (i // nb) + s[1]) * nb + i % nb, 0))
    flat = pl.BlockSpec((tr, C), lambda i, s: (i, 0))

    def body(s_ref, a_ref, b_ref, o_ref):
        o_ref[...] = (a_ref[...].astype(F32) + b_ref[...].astype(F32)).astype(BF16)

    return _placed_call(body, name, place, (R // 2 // tr,), [g_spec, flat], flat, jax.ShapeDtypeStruct((R // 2, C), BF16),
                        [grad, theirs])


def _sum_chips(chip_sum, arrived, kind, place, name):
    _, H, W = arrived.shape
    tr = _tile_rows(H, W, 6)
    nb = H // tr
    if kind == "col":
        own_spec = pl.BlockSpec((tr, W), lambda i, s: (i, s[0]))
    else:
        own_spec = pl.BlockSpec((tr, W), lambda i, s: (s[0] * nb + i, 0))

    def body(s_ref, p_ref, r_ref, o_ref):
        acc = p_ref[...].astype(F32)
        for i in range(N_CHIPS - 1):
            acc = acc + r_ref[i].astype(F32)
        o_ref[...] = acc

    return _placed_call(body, name, place, (nb,), [own_spec, pl.BlockSpec((N_CHIPS - 1, tr, W), lambda i, s: (0, i, 0))],
                        pl.BlockSpec((tr, W), lambda i, s: (s[1] * nb + i, 0)), jax.ShapeDtypeStruct((2 * H, W), F32),
                        [chip_sum, arrived])


def _adamw_values(w, g, m, v):
    m = ADAM_B1 * m + (1.0 - ADAM_B1) * g
    v = ADAM_B2 * v + (1.0 - ADAM_B2) * (g * g)
    m_hat = m / (1.0 - ADAM_B1 ** ADAM_STEP)
    v_hat = v / (1.0 - ADAM_B2 ** ADAM_STEP)
    return -ADAM_LR * (m_hat / (jnp.sqrt(v_hat) + ADAM_EPS) + ADAM_WD * w), m, v


def _adamw(w, g, m, v, name, layer=0, carried=None):
    L, R, C = w.shape
    tr = _tile_rows(R, C, 8)

    def body(w_ref, g_ref, m_ref, v_ref, *rest):
        go_ref, d_ref, mo_ref, vo_ref = rest[-4:]
        g_val = g_ref[...]
        d, m_new, v_new = _adamw_values(w_ref[...], g_val, m_ref[...], v_ref[...])
        go_ref[...], d_ref[...], mo_ref[...], vo_ref[...] = g_val, d, m_new, v_new

    lay = pl.BlockSpec((None, tr, C), lambda i: (layer, i, 0))
    ins = [w, g, m, v]
    in_specs = [lay, pl.BlockSpec((tr, C), lambda i: (i, 0)), lay, lay]
    aliases = {}
    if carried is not None:
        ins += list(carried)
        in_specs += [pl.BlockSpec(memory_space=pl.ANY)] * 4
        aliases = {4 + i: i for i in range(4)}
    return _call(body, name=name, grid=(R // tr,), in_specs=in_specs, out_specs=[lay] * 4,
                 out_shape=[jax.ShapeDtypeStruct((L, R, C), F32)] * 4, input_output_aliases=aliases, sem=("parallel",))(*ins)


def _aligned(v, m):
    return v if isinstance(v, int) else pl.multiple_of(v, m)


def _place():
    x, y, c = lax.axis_index("x"), lax.axis_index("y"), lax.axis_index("c")
    other_chips = [(x, 1 - y), (1 - x, y), (1 - x, 1 - y)]
    return x, y, c, 2 * x + y, other_chips


def _chip_index(chip):
    return 2 * chip[0] + chip[1]


def _piece(ref, kind, k, h):
    R, C = ref.shape
    if kind == "col":
        return ref.at[pl.ds(_aligned(h * (R // 2), 16), R // 2), pl.ds(_aligned(k * (C // N_CHIPS), 128), C // N_CHIPS)]
    rs = R // N_CHIPS
    return ref.at[pl.ds(_aligned(k * rs + h * (rs // 2), 16), rs // 2), :]


def _compact_piece(ref, kind, k):
    R2, C = ref.shape
    if kind == "col":
        return ref.at[:, pl.ds(_aligned(k * (C // N_CHIPS), 128), C // N_CHIPS)]
    return ref.at[pl.ds(_aligned(k * (R2 // N_CHIPS), 16), R2 // N_CHIPS), :]


def _half_rows(ref, h):
    R = ref.shape[0]
    return ref.at[pl.ds(_aligned(h * (R // 2), 16), R // 2), :]


class _Copies:
    def __init__(self, send_sems, recv_sems):
        self.send_sems, self.recv_sems = send_sems, recv_sems
        self.n_remote = 0

    def remote(self, src, dst, device):
        k = self.n_remote
        self.n_remote += 1
        return pltpu.make_async_remote_copy(src_ref=src, dst_ref=dst, send_sem=self.send_sems.at[k], recv_sem=self.recv_sems.at[k],
                                            device_id=device, device_id_type=MESH)


class _Job:
    def __init__(self, ins, out_shape, aliases, n_remote, build):
        self.ins, self.out_shape, self.aliases, self.n_remote, self.build = list(ins), list(out_shape), dict(aliases), n_remote, build


class _Flying:
    def __init__(self, job, send_sems, recv_sems, bufs, token):
        self.job, self.send_sems, self.recv_sems, self.bufs, self.token = job, send_sems, recv_sems, bufs, token


def _job_refs(job, buf_refs):
    n_out = len(job.out_shape)
    kept = [i for i in range(len(job.ins)) if i not in job.aliases]
    ins = [buf_refs[job.aliases[i]] if i in job.aliases else buf_refs[n_out + kept.index(i)] for i in range(len(job.ins))]
    return ins, list(buf_refs[:n_out])


def _start_job(job, name, after=()):
    n_in, n_out, n_after = len(job.ins), len(job.out_shape), len(after)
    kept = [i for i in range(n_in) if i not in job.aliases]
    n_bufs = n_out + len(kept)

    def body(*refs):
        in_refs, out_refs = refs[:n_in], refs[n_in + n_after:n_in + n_after + n_out]
        send_sems, recv_sems, token = refs[n_in + n_after + n_bufs:]
        for d in job.build(in_refs, out_refs, _Copies(send_sems, recv_sems)):
            d.start()
        token[...] = jnp.zeros_like(token)

    aliases = dict(job.aliases)
    aliases.update({i: n_out + k for k, i in enumerate(kept)})
    sems = pltpu.SemaphoreType.DMA((job.n_remote,))
    outs = pl.pallas_call(
        body, name=name, in_specs=[HBM] * n_in + [ANY] * n_after,
        out_specs=[HBM] * n_bufs + [SEM, SEM, pl.BlockSpec(memory_space=pltpu.VMEM)],
        out_shape=job.out_shape + [jax.ShapeDtypeStruct(job.ins[i].shape, job.ins[i].dtype) for i in kept]
        + [sems, sems, jax.ShapeDtypeStruct((8, 128), F32)],
        input_output_aliases=aliases,
        compiler_params=pltpu.CompilerParams(has_side_effects=pltpu.SideEffectType.DATAFLOW_SIDE_EFFECTING))(*job.ins, *after)
    return _Flying(job, outs[n_bufs], outs[n_bufs + 1], list(outs[:n_bufs]), outs[n_bufs + 2])


def _wait_job(flying, name, after=()):
    job, n_bufs, n_after = flying.job, len(flying.bufs), len(after)

    def body(*refs):
        in_refs, out_refs = _job_refs(job, refs[:n_bufs])
        send_sems, recv_sems = refs[n_bufs:n_bufs + 2]
        copies = job.build(in_refs, out_refs, _Copies(send_sems, recv_sems))
        for d in copies:
            d.wait_send()
        for d in copies:
            d.wait_recv()

    outs = pl.pallas_call(
        body, name=name, in_specs=[HBM] * n_bufs + [SEM, SEM] + [ANY] * n_after, out_specs=[HBM] * n_bufs,
        out_shape=[jax.ShapeDtypeStruct(b.shape, b.dtype) for b in flying.bufs],
        input_output_aliases={i: i for i in range(n_bufs)},
        compiler_params=pltpu.CompilerParams(has_side_effects=pltpu.SideEffectType.DATAFLOW_SIDE_EFFECTING))(
            *flying.bufs, flying.send_sems, flying.recv_sems, *after)
    return list(outs[:len(job.out_shape)]), list(outs[len(job.out_shape):])


def _in_place(arrays):
    return [jax.ShapeDtypeStruct(a.shape, a.dtype) for a in arrays], {u: u for u in range(len(arrays))}


def _rows_part(ref, part, n_parts):
    h = ref.shape[0] // n_parts
    return ref.at[pl.ds(part * h, h), :]


def _gather_job(full, kind, stage):
    def build(in_refs, out_refs, cp):
        x, y, c, me, (y_nbr, x_nbr, diagonal) = _place()
        (ref,) = out_refs
        sibling = (x, y, 1 - c)
        if stage == 1:
            mine = _piece(ref, kind, me, c)
            return [cp.remote(mine, mine, (*y_nbr, c)), cp.remote(mine, mine, (*x_nbr, c))]
        from_y, from_x = _piece(ref, kind, _chip_index(y_nbr), c), _piece(ref, kind, _chip_index(x_nbr), c)
        copies = []
        if stage in (2, "relay"):
            relay_0, relay_1 = _rows_part(from_x, 0, 2), _rows_part(from_y, 1, 2)
            copies += [cp.remote(relay_0, relay_0, (*y_nbr, c)), cp.remote(relay_1, relay_1, (*x_nbr, c))]
        if stage in (2, "direct"):
            copies += [cp.remote(from_y, from_y, sibling), cp.remote(from_x, from_x, sibling)]
        if stage == 3:
            from_diagonal = _piece(ref, kind, _chip_index(diagonal), c)
            copies.append(cp.remote(from_diagonal, from_diagonal, sibling))
        return copies

    return _Job([full], *_in_place([full]), {1: 2, 2: 4, "relay": 2, "direct": 2, 3: 1}[stage], build)


def _gather_small_job(fulls, axes):
    def build(in_refs, out_refs, cp):
        x, y, c, me, chips = _place()
        copies = []
        for ref, ax in zip(out_refs, axes):
            n = ref.shape[ax] // N_CHIPS
            idx = [slice(None)] * len(ref.shape)
            idx[ax] = pl.ds(_aligned(me * n, n), n)
            mine = ref.at[tuple(idx)]
            copies += [cp.remote(mine, mine, (*chip, c)) for chip in chips]
        return copies

    return _Job(fulls, *_in_place(fulls), 3 * len(fulls), build)


def _exchange_halves_job(grads, kinds):
    def build(in_refs, out_refs, cp):
        x, y, c, me, chips = _place()
        copies = []
        for src, dst, kind in zip(in_refs, out_refs, kinds):
            if kind == "col":
                copies.append(cp.remote(_half_rows(src, 1 - c), dst, (x, y, 1 - c)))
            else:
                copies += [cp.remote(_piece(src, "row", k, 1 - c), _compact_piece(dst, "row", k), (x, y, 1 - c))
                           for k in range(N_CHIPS)]
        return copies

    out_shape = [jax.ShapeDtypeStruct((g.shape[0] // 2, g.shape[1]), g.dtype) for g in grads]
    return _Job(grads, out_shape, {}, sum(1 if k == "col" else N_CHIPS for k in kinds), build)


def _scatter_job(half, kind):
    def build(in_refs, out_refs, cp):
        x, y, c, me, chips = _place()
        (src,), (dst,) = in_refs, out_refs
        return [cp.remote(_compact_piece(src, kind, _chip_index(chip)), dst.at[r], (*chip, c)) for r, chip in enumerate(chips)]

    part_shape = (half.shape[0], half.shape[1] // N_CHIPS) if kind == "col" else (half.shape[0] // N_CHIPS, half.shape[1])
    return _Job([half], [jax.ShapeDtypeStruct((N_CHIPS - 1,) + part_shape, half.dtype)], {}, N_CHIPS - 1, build)


def _share_job(shards):
    def build(in_refs, out_refs, cp):
        x, y, c, me, chips = _place()
        copies = []
        for ref in out_refs:
            mine = _half_rows(ref, c)
            copies.append(cp.remote(mine, mine, (x, y, 1 - c)))
        return copies

    return _Job(shards, *_in_place(shards), len(shards), build)


N_DEVICES = 2 * N_CHIPS


def _small_exchange_job(slots):
    def build(in_refs, out_refs, cp):
        x, y, c, me, chips = _place()
        (ref,) = out_refs
        mine = ref.at[2 * me + c]
        return [cp.remote(mine, mine, (x ^ (p >> 2), y ^ ((p >> 1) & 1), c ^ (p & 1))) for p in range(1, N_DEVICES)]

    return _Job([slots], *_in_place([slots]), N_DEVICES - 1, build)


def _in_own_slot(packed, place, name):
    R, C = packed.shape

    def body(s_ref, p_ref, o_ref):
        o_ref[...] = p_ref[...]

    return _placed_call(body, name, place, (1,), [pl.BlockSpec((R, C), lambda i, s: (0, 0))],
                        pl.BlockSpec((None, R, C), lambda i, s: (2 * s[0] + s[1], 0, 0)),
                        jax.ShapeDtypeStruct((N_DEVICES, R, C), F32), [packed])


def _sum_slots(slots, name):
    n, R, C = slots.shape

    def body(s_ref, o_ref):
        acc = s_ref[0]
        for i in range(1, n):
            acc = acc + s_ref[i]
        o_ref[...] = acc

    return _call(body,name=name, grid=(1,), in_specs=[pl.BlockSpec((n, R, C), lambda i: (0, 0, 0))],
                 out_specs=pl.BlockSpec((R, C), lambda i: (0, 0)), out_shape=jax.ShapeDtypeStruct((R, C), F32),
                 sem=("arbitrary",))(slots)


def _packed_rows(size, width):
    return -(-size // (8 * width)) * 8


def _pack(arrays, width):
    rows = []
    for a in arrays:
        flat = a.reshape(-1)
        n_rows = _packed_rows(flat.shape[0], width)
        rows.append(jnp.pad(flat, (0, n_rows * width - flat.shape[0])).reshape(n_rows, width))
    return jnp.concatenate(rows, axis=0)


def _unpack(packed, shapes):
    out, r0, width = [], 0, packed.shape[1]
    for shape in shapes:
        size = 1
        for d in shape:
            size *= d
        out.append(packed[r0:r0 + _packed_rows(size, width)].reshape(-1)[:size].reshape(shape))
        r0 += _packed_rows(size, width)
    return out


class _Backlog:
    def __init__(self, first):
        self.now, self.free, self.flights, self.last, self.chain = 0.0, {"ici": 0.0, "d2d": 0.0}, [], first, []

    def run(self, fn, us, *args, **kw):
        out = fn(*args, **kw)
        self.now += us
        self.last = out[0] if isinstance(out, (list, tuple)) else out
        self.poll()
        return out

    def start(self, job, name, link, cost, done):
        flying = _start_job(job, name + "_start", self.chain)
        self.chain = [flying.token]
        _Behind.pending.append(flying.token)
        ends = max(self.now, self.free[link]) + cost
        self.free[link] = ends
        self.flights.append((ends + LANDING_SLACK_US, name, flying, done))
        self.flights.sort(key=lambda f: f[0])
        return flying

    def poll(self, block=False):
        while self.flights and (block or self.flights[0][0] <= self.now):
            ends, name, flying, done = self.flights.pop(0)
            self.now, block = max(self.now, ends), False
            done(*_wait_job(flying, name + "_wait", [self.last] + self.chain))


class _GatherStream:
    def __init__(self, backlog, bufs, kinds, costs, early=()):
        self.backlog, self.bufs, self.kinds, self.costs, self.begun, self.complete = backlog, bufs, kinds, costs, 0, set()
        self.early, self.relays, self.near_complete = set(early), {}, set()
        self.begin()

    def begin(self):
        u, self.begun = self.begun, self.begun + 1
        self.backlog.start(_gather_job(self.bufs[u], self.kinds[u], 1), "gather_%d" % u, "ici", 0.5 * self.costs[u],
                           lambda outs, kept: self.arrived(u, outs[0]))

    def arrived(self, u, buf):
        self.bufs[u] = buf
        if u in self.early:
            self.relays[u] = self.backlog.start(_gather_job(buf, self.kinds[u], "relay"), "relay_%d" % u, "ici",
                                                0.25 * self.costs[u], lambda outs, kept: self.relayed(u, outs[0]))
            self.backlog.start(_gather_job(self.relays[u].bufs[0], self.kinds[u], "direct"), "direct_%d" % u, "d2d",
                               0.5 * D2D_SHARE * self.costs[u], lambda outs, kept: self.near(u, outs[0]))
        else:
            self.backlog.start(_gather_job(buf, self.kinds[u], 2), "relay_%d" % u, "ici", 0.25 * self.costs[u],
                               lambda outs, kept: self.relayed(u, outs[0]))
        while self.begun <= min(u + GATHER_WINDOW[u], len(self.bufs) - 1):
            self.begin()

    def near(self, u, buf):
        self.bufs[u] = self.relays[u].bufs[0] = buf
        self.near_complete.add(u)

    def nearly_ready(self, u):
        while u not in self.near_complete:
            assert self.backlog.flights, "weight %d is not on its way" % u
            self.backlog.poll(block=True)
        return self.bufs[u]

    def relayed(self, u, buf):
        assert u not in self.early or u in self.near_complete, "the relay of weight %d is waited for before its sibling copies" % u
        self.bufs[u] = buf
        self.backlog.start(_gather_job(buf, self.kinds[u], 3), "handon_%d" % u, "d2d", D2D_SHARE * self.costs[u],
                           lambda outs, kept: self.handed(u, outs[0]))

    def handed(self, u, buf):
        self.bufs[u] = buf
        self.complete.add(u)

    def ready(self, u):
        while u not in self.complete:
            assert self.backlog.flights, "weight %d is not on its way" % u
            self.backlog.poll(block=True)
        return self.bufs[u]


class _GradStream:
    def __init__(self, backlog, u, name, kind, cost, g, place, results):
        self.backlog, self.u, self.name, self.kind, self.cost, self.place, self.results = backlog, u, name, kind, cost, place, results
        backlog.start(_exchange_halves_job([g], [kind]), "to_sibling_" + name, "d2d", D2D_SHARE * cost, self.exchanged)

    def exchanged(self, outs, kept):
        chip_sum = self.backlog.run(_add_pair, SIDE_KERNEL_US, kept[0], outs[0], self.kind, self.place, "chip_sum_" + self.name)
        self.backlog.start(_scatter_job(chip_sum, self.kind), "to_owners_" + self.name, "ici", self.cost, self.scattered)

    def scattered(self, outs, kept):
        reduced = self.backlog.run(_sum_chips, SIDE_KERNEL_US, kept[0], outs[0], self.kind, self.place, "reduce_" + self.name)
        self.backlog.start(_share_job([reduced]), "share_" + self.name, "d2d", D2D_SHARE * self.cost, self.shared)

    def shared(self, outs, kept):
        self.results[self.u] = outs[0]


SIDE_KERNEL_US = 12.0
D2D_SHARE = 0.15
LANDING_SLACK_US = 5.0
GATHER_WINDOW = (1, 1, 1, 2, 2, 1, 1, 1)


def kernel(x, mix_pre_g, mix_post_g, ffn_pre_g, ffn_post_g, ab_w_in, pool_w, pool_scale, conv_w, conv_b, conv_ln_g, conv_ln_b, ab_w_out, sc_w_in, sc_conv_w, sc_w_out, ffn_w1, ffn_w2, loss_target, m_mix_pre_g, m_mix_post_g, m_ffn_pre_g, m_ffn_post_g, m_ab_w_in, m_pool_w, m_pool_scale, m_conv_w, m_conv_b, m_conv_ln_g, m_conv_ln_b, m_ab_w_out, m_sc_w_in, m_sc_conv_w, m_sc_w_out, m_ffn_w1, m_ffn_w2, v_mix_pre_g, v_mix_post_g, v_ffn_pre_g, v_ffn_post_g, v_ab_w_in, v_pool_w, v_pool_scale, v_conv_w, v_conv_b, v_conv_ln_g, v_conv_ln_b, v_ab_w_out, v_sc_w_in, v_sc_conv_w, v_sc_w_out, v_ffn_w1, v_ffn_w2):
    x0, target = x[0], loss_target[0]
    T, D = x0.shape
    DP = pool_scale.shape[-1]
    gain = lambda g, layer: g[layer][None, :]

    big = [("ab_w_in", ab_w_in, 0, "col", 67.0), ("ab_w_out", ab_w_out, 0, "row", 44.0),
           ("ffn_w1_0", ffn_w1, 0, "col", 177.0), ("ffn_w2_0", ffn_w2, 0, "row", 177.0),
           ("sc_w_in", sc_w_in, 0, "col", 133.0), ("sc_w_out", sc_w_out, 0, "row", 44.0),
           ("ffn_w1_1", ffn_w1, 1, "col", 177.0), ("ffn_w2_1", ffn_w2, 1, "row", 177.0)]
    kinds = [b[3] for b in big]
    chip = 2 * lax.axis_index("x") + lax.axis_index("y")
    place = jnp.stack([chip, lax.axis_index("c")]).astype(jnp.int32)

    def own_in_zeros(shard, ax):
        full = jnp.zeros(tuple(d * N_CHIPS if i == ax else d for i, d in enumerate(shard.shape)), shard.dtype)
        return lax.dynamic_update_slice_in_dim(full, shard, chip * shard.shape[ax], axis=ax)

    W = [_cast_into_full(w, layer, kind, place, "cast_" + name) for name, w, layer, kind, _ in big]
    smalls = [own_in_zeros(pool_w[0], 1), own_in_zeros(conv_w[0], 1), own_in_zeros(sc_conv_w[0], 1)]
    backlog = _Backlog(x0)
    run = backlog.run
    small_weights = []
    backlog.start(_gather_small_job(smalls, [1, 1, 1]), "gather_small", "ici", 6.0, lambda outs, kept: small_weights.extend(outs))
    gather = _GatherStream(backlog, W, kinds, [b[4] for b in big], early=(3, 7))

    relu_sq = lambda acc: (jnp.maximum(acc, 0.0), jnp.square(jnp.maximum(acc, 0.0)))
    relu_sq_bwd = lambda acc, a: (acc * (2.0 * a.astype(F32)),)

    h0 = run(_norm_fwd, 12.0, x0, gain(mix_pre_g, 0), "norm_in")
    z0 = run(_matmul, 35.0, h0, gather.ready(0), "nn", "mix0_in")
    while not small_weights:
        backlog.poll(block=True)
    pool_w_full, conv_w_full, sc_conv_w_full = small_weights
    pooled, y_pool = run(_pool_fwd, 23.0, z0, pool_w_full, pool_scale, "pool_fwd")
    a_conv, c_conv = run(_conv_fwd, 25.0, z0, conv_w_full, conv_b, DP, "conv_fwd")
    y_conv = run(_ln_silu_fwd, 10.0, c_conv, conv_ln_g, conv_ln_b, "ln_silu_fwd")
    y0 = jnp.concatenate([y_pool, y_conv], axis=1)
    m0 = run(_matmul, 25.0, y0, gather.ready(1), "nn", "mix0_out")
    x1, h1 = run(_residual_norm, 21.0, x0, m0, gain(mix_post_g, 0), gain(ffn_pre_g, 0), "res_mix0")
    a0, a0sq = run(_matmul, 81.0, h1, gather.ready(2), "nn", "ffn0_up", out_dtypes=(BF16, BF16), epilogue=relu_sq)
    f0 = run(_matmul_by_chip_rows, 63.0, a0sq, gather.nearly_ready(3), place, "ffn0_down_near")
    f0 = run(_matmul_by_chip_rows, 21.0, a0sq, gather.ready(3), place, "ffn0_down", rest=f0)
    x2, h2 = run(_residual_norm, 22.0, x1, f0, gain(ffn_post_g, 0), gain(mix_pre_g, 1), "res_ffn0")
    z1 = run(_matmul, 62.0, h2, gather.ready(4), "nn", "mix1_in")
    y1 = run(_short_fwd, 22.0, z1, sc_conv_w_full, "short_fwd")
    m1 = run(_matmul, 25.0, y1, gather.ready(5), "nn", "mix1_out")
    x3, h3 = run(_residual_norm, 21.0, x2, m1, gain(mix_post_g, 1), gain(ffn_pre_g, 1), "res_mix1")
    a1, a1sq = run(_matmul, 81.0, h3, gather.ready(6), "nn", "ffn1_up", out_dtypes=(BF16, BF16), epilogue=relu_sq)
    f1 = run(_matmul_by_chip_rows, 63.0, a1sq, gather.nearly_ready(7), place, "ffn1_down_near")
    f1 = run(_matmul_by_chip_rows, 21.0, a1sq, gather.ready(7), place, "ffn1_down", rest=f1)
    w_in0, w_out0, w1_0, w2_0, w_in1, w_out1, w1_1, w2_1 = W

    grads_big = [None] * len(big)


    def reduce_grad(u, g):
        name, _, _, kind, cost = big[u]
        _GradStream(backlog, u, name, kind, cost, g, place, grads_big)

    dx, df1, d_ffn_post_1, loss_row = run(_loss_and_last_norm_bwd, 30.0, x3, f1, gain(ffn_post_g, 1), target, "loss")
    reduce_grad(7, run(_matmul, 80.0, a1sq, df1, "tn", "ffn1_down_dw", out_dtypes=(BF16,)))
    dz = run(_matmul, 82.0, df1, w2_1, "nt", "ffn1_down_dx", out_dtypes=(BF16,), epilogue=relu_sq_bwd, epi=(a1,))
    reduce_grad(6, run(_matmul, 80.0, h3, dz, "tn", "ffn1_up_dw", out_dtypes=(BF16,)))
    dh = run(_matmul, 87.0, dz, w1_1, "nt", "ffn1_up_dx", tk=LONG_K_TILE)
    dx, d_ffn_pre_1, dm1, d_mix_post_1 = run(_norms_bwd, 36.0, dx, dh, x3, gain(ffn_pre_g, 1), m1, gain(mix_post_g, 1), "norms_bwd3")

    reduce_grad(5, run(_matmul, 24.0, y1, dm1, "tn", "mix1_out_dw", out_dtypes=(BF16,)))
    dy1 = run(_matmul, 25.0, dm1, w_out1, "nt", "mix1_out_dx")
    db, dcg, du, d_sc_conv_w = run(_short_bwd, 41.0, dy1, z1, sc_conv_w_full, "short_bwd")
    dz1 = jnp.concatenate([db, dcg, du], axis=1)
    reduce_grad(4, run(_matmul, 62.0, h2, dz1, "tn", "mix1_in_dw", out_dtypes=(BF16,)))
    dh = run(_matmul, 68.0, dz1, w_in1, "nt", "mix1_in_dx")
    dx, d_mix_pre_1, df0, d_ffn_post_0 = run(_norms_bwd, 35.0, dx, dh, x2, gain(mix_pre_g, 1), f0, gain(ffn_post_g, 0), "norms_bwd2")

    reduce_grad(3, run(_matmul, 80.0, a0sq, df0, "tn", "ffn0_down_dw", out_dtypes=(BF16,)))
    dz = run(_matmul, 82.0, df0, w2_0, "nt", "ffn0_down_dx", out_dtypes=(BF16,), epilogue=relu_sq_bwd, epi=(a0,))
    reduce_grad(2, run(_matmul, 80.0, h1, dz, "tn", "ffn0_up_dw", out_dtypes=(BF16,)))
    dh = run(_matmul, 87.0, dz, w1_0, "nt", "ffn0_up_dx", tk=LONG_K_TILE)
    dx, d_ffn_pre_0, dm0, d_mix_post_0 = run(_norms_bwd, 36.0, dx, dh, x1, gain(ffn_pre_g, 0), m0, gain(mix_post_g, 0), "norms_bwd1")

    reduce_grad(1, run(_matmul, 24.0, y0, dm0, "tn", "mix0_out_dw", out_dtypes=(BF16,)))
    dy0 = run(_matmul, 25.0, dm0, w_out0, "nt", "mix0_out_dx")
    du_pool, d_pool_w, d_pool_scale = run(_pool_bwd, 28.0, dy0, pooled, pool_w_full, pool_scale, "pool_bwd")
    dc, d_ln_g, d_ln_b = run(_ln_silu_bwd, 15.0, dy0, c_conv, conv_ln_g, conv_ln_b, "ln_silu_bwd")
    dv, dgate, d_conv_w, d_conv_b = run(_conv_bwd, 52.0, dc, a_conv, z0, conv_w_full, DP, "conv_bwd")
    dz0 = jnp.concatenate([du_pool, dv, dgate], axis=1)

    small_sums = {}

    def exchange_small(key, arrays, cost):
        slots = _in_own_slot(_pack(arrays, D), place, "small_grads_slot_" + key)
        backlog.start(_small_exchange_job(slots), "small_grads_" + key, "ici", cost,
                      lambda outs, kept: small_sums.__setitem__(key, _unpack(_sum_slots(outs[0], "small_grads_sum_" + key),
                                                                             [a.shape for a in arrays])))

    exchange_small("most", [d_mix_pre_1, jnp.concatenate([d_mix_post_0, d_mix_post_1], 0),
                            jnp.concatenate([d_ffn_pre_0, d_ffn_pre_1], 0), jnp.concatenate([d_ffn_post_0, d_ffn_post_1], 0),
                            d_pool_scale, d_conv_b, d_ln_g, d_ln_b, d_pool_w, d_conv_w, d_sc_conv_w], 112.0)
    reduce_grad(0, run(_matmul, 34.0, h0, dz0, "tn", "mix0_in_dw", out_dtypes=(BF16,)))
    dh = run(_matmul, 40.0, dz0, w_in0, "nt", "mix0_in_dx")
    grad_x, d_mix_pre_0 = run(_norms_bwd, 26.0, dx, dh, x0, gain(mix_pre_g, 0), None, None, "norms_bwd0")
    exchange_small("last", [d_mix_pre_0, loss_row], 5.0)

    upd, gr, first = {}, grads_big, {}

    def keep(where, key, outs):
        where[key] = outs
        return outs

    adamw_big = [
        (7, lambda: keep(first, "ffn_w2", _adamw(ffn_w2, gr[7], m_ffn_w2, v_ffn_w2, "adamw_ffn_w2_1", layer=1)), 46.0),
        (6, lambda: keep(first, "ffn_w1", _adamw(ffn_w1, gr[6], m_ffn_w1, v_ffn_w1, "adamw_ffn_w1_1", layer=1)), 46.0),
        (5, lambda: keep(upd, "sc_w_out", _adamw(sc_w_out, gr[5], m_sc_w_out, v_sc_w_out, "adamw_sc_w_out")), 14.0),
        (4, lambda: keep(upd, "sc_w_in", _adamw(sc_w_in, gr[4], m_sc_w_in, v_sc_w_in, "adamw_sc_w_in")), 35.0),
        (3, lambda: keep(upd, "ffn_w2", _adamw(ffn_w2, gr[3], m_ffn_w2, v_ffn_w2, "adamw_ffn_w2_0", layer=0,
                                               carried=first["ffn_w2"])), 46.0),
        (2, lambda: keep(upd, "ffn_w1", _adamw(ffn_w1, gr[2], m_ffn_w1, v_ffn_w1, "adamw_ffn_w1_0", layer=0,
                                               carried=first["ffn_w1"])), 46.0),
        (1, lambda: keep(upd, "ab_w_out", _adamw(ab_w_out, gr[1], m_ab_w_out, v_ab_w_out, "adamw_ab_w_out")), 14.0),
        (0, lambda: keep(upd, "ab_w_in", _adamw(ab_w_in, gr[0], m_ab_w_in, v_ab_w_in, "adamw_ab_w_in")), 19.0)]
    while adamw_big or backlog.flights:
        due = [a for a in adamw_big if gr[a[0]] is not None]
        if due:
            adamw_big.remove(due[0])
            backlog.run(due[0][1], due[0][2])
        else:
            backlog.poll(block=True)

    (g_mix_pre_1, g_mix_post, g_ffn_pre, g_ffn_post, g_pool_scale, g_conv_b, g_ln_g, g_ln_b, g_pool_w_full, g_conv_w_full,
     g_sc_conv_w_full) = small_sums["most"]
    g_mix_pre = jnp.concatenate([small_sums["last"][0], g_mix_pre_1], 0)
    loss = small_sums["last"][1][0, 0]
    own = lambda a, ax: lax.dynamic_slice_in_dim(a, chip * (a.shape[ax] // N_CHIPS), a.shape[ax] // N_CHIPS, axis=ax)
    g_pool_w, g_conv_w, g_sc_conv_w = own(g_pool_w_full, 1), own(g_conv_w_full, 1), own(g_sc_conv_w_full, 1)

    def small_update(w, g, m, v, name):
        shape = w.shape
        as3 = lambda a: a.reshape((1, -1, shape[-1]))
        outs = _adamw(as3(w), g.reshape((-1, shape[-1])), as3(m), as3(v), "adamw_" + name)
        return [o.reshape(shape) for o in outs]

    upd["mix_pre_g"] = small_update(mix_pre_g, g_mix_pre, m_mix_pre_g, v_mix_pre_g, "mix_pre_g")
    upd["mix_post_g"] = small_update(mix_post_g, g_mix_post, m_mix_post_g, v_mix_post_g, "mix_post_g")
    upd["ffn_pre_g"] = small_update(ffn_pre_g, g_ffn_pre, m_ffn_pre_g, v_ffn_pre_g, "ffn_pre_g")
    upd["ffn_post_g"] = small_update(ffn_post_g, g_ffn_post, m_ffn_post_g, v_ffn_post_g, "ffn_post_g")
    upd["pool_w"] = small_update(pool_w, g_pool_w, m_pool_w, v_pool_w, "pool_w")
    upd["pool_scale"] = small_update(pool_scale, g_pool_scale, m_pool_scale, v_pool_scale, "pool_scale")
    upd["conv_w"] = small_update(conv_w, g_conv_w, m_conv_w, v_conv_w, "conv_w")
    upd["conv_b"] = small_update(conv_b, g_conv_b, m_conv_b, v_conv_b, "conv_b")
    upd["conv_ln_g"] = small_update(conv_ln_g, g_ln_g, m_conv_ln_g, v_conv_ln_g, "conv_ln_g")
    upd["conv_ln_b"] = small_update(conv_ln_b, g_ln_b, m_conv_ln_b, v_conv_ln_b, "conv_ln_b")
    upd["sc_conv_w"] = small_update(sc_conv_w, g_sc_conv_w, m_sc_conv_w, v_sc_conv_w, "sc_conv_w")

    order = ["mix_pre_g", "mix_post_g", "ffn_pre_g", "ffn_post_g", "ab_w_in", "pool_w", "pool_scale", "conv_w", "conv_b",
             "conv_ln_g", "conv_ln_b", "ab_w_out", "sc_w_in", "sc_conv_w", "sc_w_out", "ffn_w1", "ffn_w2"]
    out = [loss, grad_x[None]]
    for part in range(4):
        out += [upd[n][part] for n in order]
    return tuple(out)
```

```python
import jax
import jax.numpy as jnp
from jax import lax
from jax.experimental import pallas as pl
from jax.experimental.pallas import tpu as pltpu

F32, BF16 = jnp.float32, jnp.bfloat16
EPS = 1e-6
N_GROUPS = 4
MAX_WINDOW = 16
CONV_K = 31
SHORT_K = 3
CONV_PAD = 32
SHORT_PAD = 8
ADAM_LR, ADAM_B1, ADAM_B2, ADAM_EPS, ADAM_WD, ADAM_STEP = 0.001, 0.9, 0.999, 1e-08, 0.01, 10
N_CHIPS = 4
VMEM_LIMIT_BYTES = 56 * 1024 * 1024
ROW_TILE = 256
CHUNK = 256
LONG_K_TILE = 4096
MESH = pl.DeviceIdType.MESH
HBM = pl.BlockSpec(memory_space=pltpu.HBM)
SEM = pl.BlockSpec(memory_space=pltpu.SEMAPHORE)
ANY = pl.BlockSpec(memory_space=pl.ANY)


def _cp(*sem):
    return pltpu.CompilerParams(dimension_semantics=sem, vmem_limit_bytes=VMEM_LIMIT_BYTES)


def _sigmoid(v):
    return 1.0 / (1.0 + jnp.exp(-v))


class _Behind:
    pending = []


def _call(body, prefetch=None, **kw):
    behind, _Behind.pending = _Behind.pending, []
    single = not isinstance(kw["out_shape"], (list, tuple))
    in_specs, scratch = list(kw["in_specs"]), list(kw.get("scratch_shapes", ()))
    out_shape = [kw["out_shape"]] if single else list(kw["out_shape"])
    out_specs = [kw["out_specs"]] if single else list(kw["out_specs"])
    n_pre = 0 if prefetch is None else 1
    n_own, n_behind = len(in_specs), len(behind)

    def wrapped(*refs):
        body(*refs[:n_pre + n_own], *refs[n_pre + n_own + n_behind:])

    specs = dict(grid=kw["grid"], in_specs=in_specs + [ANY] * n_behind, out_specs=out_specs)
    if prefetch is None:
        specs["scratch_shapes"] = scratch
    else:
        specs = dict(grid_spec=pltpu.PrefetchScalarGridSpec(num_scalar_prefetch=1, scratch_shapes=scratch, **specs))
    aliases = {n_pre + i: o for i, o in kw.get("input_output_aliases", {}).items()}
    call = pl.pallas_call(wrapped, name=kw["name"], out_shape=out_shape, input_output_aliases=aliases,
                          compiler_params=_cp(*kw["sem"]), **specs)

    def run(*args):
        outs = call(*([prefetch] * n_pre), *args, *behind)
        return outs[0] if single else list(outs)

    return run


_DIMS = {"nn": (((1,), (0,)), ((), ())), "nt": (((1,), (1,)), ((), ())), "tn": (((0,), (0,)), ((), ()))}


def _pick(n, cap, step=256):
    if n <= cap:
        return n
    return next(t for t in range(cap - cap % step, 0, -step) if n % t == 0)


def _matmul(a, b, mode, name, out_dtypes=(F32,), epilogue=None, epi=(), tm=1024, tn=1024, tk=2048):
    if mode == "tn":
        (K, M), (K2, N) = a.shape, b.shape
    elif mode == "nt":
        (M, K), (N, K2) = a.shape, b.shape
    else:
        (M, K), (K2, N) = a.shape, b.shape
    assert K == K2
    tm, tn, tk = _pick(M, tm), _pick(N, tn), _pick(K, tk)
    nk = K // tk
    a_spec = pl.BlockSpec((tk, tm), lambda i, j, k: (k, i)) if mode == "tn" else pl.BlockSpec((tm, tk), lambda i, j, k: (i, k))
    b_spec = pl.BlockSpec((tn, tk), lambda i, j, k: (j, k)) if mode == "nt" else pl.BlockSpec((tk, tn), lambda i, j, k: (k, j))
    o_spec = pl.BlockSpec((tm, tn), lambda i, j, k: (i, j))
    n_epi, n_out = len(epi), len(out_dtypes)

    def body(a_ref, b_ref, *rest):
        epi_refs, out_refs, scratch = rest[:n_epi], rest[n_epi:n_epi + n_out], rest[n_epi + n_out:]
        part = lax.dot_general(a_ref[...].astype(BF16), b_ref[...].astype(BF16), _DIMS[mode], preferred_element_type=F32)

        def finish(acc):
            outs = epilogue(acc, *[r[...] for r in epi_refs]) if epilogue else (acc,)
            for o_ref, o in zip(out_refs, outs):
                o_ref[...] = o.astype(o_ref.dtype)

        if nk == 1:
            finish(part)
        else:
            acc_ref = scratch[0]
            k = pl.program_id(2)

            @pl.when(k == 0)
            def _():
                acc_ref[...] = part

            @pl.when(k > 0)
            def _():
                acc_ref[...] += part

            @pl.when(k == nk - 1)
            def _():
                finish(acc_ref[...])

    outs = _call(
        body, name=name, grid=(M // tm, N // tn, nk),
        in_specs=[a_spec, b_spec] + [o_spec] * n_epi, out_specs=[o_spec] * n_out,
        out_shape=[jax.ShapeDtypeStruct((M, N), dt) for dt in out_dtypes],
        scratch_shapes=[pltpu.VMEM((tm, tn), F32)] if nk > 1 else [],
        sem=("parallel", "parallel", "arbitrary"))(a, b, *epi)
    return outs[0] if n_out == 1 else outs


def _matmul_by_chip_rows(a, b, place, name, rest=None, tm=1024, tn=1024):
    (M, K), N = a.shape, b.shape[1]
    tm, tn, tk = _pick(M, tm), _pick(N, tn), K // N_CHIPS
    nk = N_CHIPS - 1 if rest is None else 1

    def block(k, s):
        diagonal = N_CHIPS - 1 - s[0]
        return k + (k >= diagonal).astype(jnp.int32) if rest is None else diagonal

    def body(s_ref, a_ref, b_ref, *refs):
        o_ref, acc_ref = refs[-2:]
        part = jnp.dot(a_ref[...], b_ref[...], preferred_element_type=F32)
        k = pl.program_id(2)

        @pl.when(k == 0)
        def _():
            acc_ref[...] = part if rest is None else part + refs[0][...]

        @pl.when(k > 0)
        def _():
            acc_ref[...] += part

        @pl.when(k == nk - 1)
        def _():
            o_ref[...] = acc_ref[...]

    tile = pl.BlockSpec((tm, tn), lambda i, j, k, s: (i, j))
    in_specs = [pl.BlockSpec((tm, tk), lambda i, j, k, s: (i, block(k, s))), pl.BlockSpec((tk, tn), lambda i, j, k, s: (block(k, s), j))]
    return _call(body, prefetch=place, name=name, grid=(M // tm, N // tn, nk), in_specs=in_specs + ([] if rest is None else [tile]),
                 out_specs=tile, out_shape=jax.ShapeDtypeStruct((M, N), F32), scratch_shapes=[pltpu.VMEM((tm, tn), F32)],
                 sem=("parallel", "parallel", "arbitrary"))(a, b, *([] if rest is None else [rest]))


def _matmul_by_chip_cols(a, b, place, name, out_dtypes, epilogue, begun=None, tm=1024, tn=1024):
    (M, K), N = a.shape, b.shape[1]
    shard = N // N_CHIPS
    tm, tn = _pick(M, tm), _pick(shard, tn)
    per, n_out = shard // tn, len(out_dtypes)

    def col(j, s):
        diagonal, blk = N_CHIPS - 1 - s[0], j // per
        blk = blk + (blk >= diagonal).astype(jnp.int32) if begun is None else diagonal
        return blk * per + j % per

    def body(s_ref, a_ref, b_ref, *refs):
        outs = epilogue(jnp.dot(a_ref[...], b_ref[...], preferred_element_type=F32))
        for o_ref, o in zip(refs[-n_out:], outs):
            o_ref[...] = o.astype(o_ref.dtype)

    tile = pl.BlockSpec((tm, tn), lambda i, j, s: (i, col(j, s)))
    in_specs = [pl.BlockSpec((tm, K), lambda i, j, s: (i, 0)), pl.BlockSpec((K, tn), lambda i, j, s: (0, col(j, s)))]
    carried = [] if begun is None else list(begun)
    return _call(body, prefetch=place, name=name, grid=(M // tm, (N_CHIPS - 1 if begun is None else 1) * per),
                 in_specs=in_specs + [ANY] * len(carried), out_specs=[tile] * n_out,
                 out_shape=[jax.ShapeDtypeStruct((M, N), dt) for dt in out_dtypes],
                 input_output_aliases={2 + i: i for i in range(len(carried))}, sem=("parallel", "parallel"))(a, b, *carried)


def _rms(x, g):
    r = lax.rsqrt(jnp.mean(x * x, axis=-1, keepdims=True) + EPS)
    return x * r * g


def _rms_bwd(dy, x, g):
    r = lax.rsqrt(jnp.mean(x * x, axis=-1, keepdims=True) + EPS)
    xn = x * r
    dyg = dy * g
    dx = r * (dyg - xn * jnp.mean(dyg * xn, axis=-1, keepdims=True))
    return dx, jnp.sum(dy * xn, axis=0, keepdims=True)


def _rows(d, tr=ROW_TILE):
    return pl.BlockSpec((tr, d), lambda i: (i, 0))


def _vec(d):
    return pl.BlockSpec((1, d), lambda i: (0, 0))


def _accumulate(ref, val):
    @pl.when(pl.program_id(0) == 0)
    def _():
        ref[...] = val

    @pl.when(pl.program_id(0) > 0)
    def _():
        ref[...] += val


def _norm_fwd(x, g, name):
    T, D = x.shape

    def body(x_ref, g_ref, h_ref):
        h_ref[...] = _rms(x_ref[...], g_ref[...]).astype(BF16)

    return _call(body,name=name,grid=(T // ROW_TILE,), in_specs=[_rows(D), _vec(D)], out_specs=_rows(D),
                          out_shape=jax.ShapeDtypeStruct((T, D), BF16), sem=("parallel",))(x, g)


def _residual_norm(x, m, g_post, g_next, name):
    T, D = x.shape

    def body(x_ref, m_ref, gp_ref, gn_ref, xo_ref, h_ref):
        xo = x_ref[...] + _rms(m_ref[...], gp_ref[...])
        xo_ref[...] = xo
        h_ref[...] = _rms(xo, gn_ref[...]).astype(BF16)

    return _call(body,name=name,grid=(T // ROW_TILE,), in_specs=[_rows(D), _rows(D), _vec(D), _vec(D)],
                          out_specs=[_rows(D), _rows(D)],
                          out_shape=[jax.ShapeDtypeStruct((T, D), F32), jax.ShapeDtypeStruct((T, D), BF16)],
                          sem=("parallel",))(x, m, g_post, g_next)


def _loss_and_last_norm_bwd(x, m, g_post, target, name):
    T, D = x.shape

    def body(x_ref, m_ref, gp_ref, t_ref, dx_ref, dm_ref, dg_ref, loss_ref):
        m_val, gp = m_ref[...], gp_ref[...]
        err = x_ref[...] + _rms(m_val, gp) - t_ref[...]
        dx = err * (1.0 / D)
        dx_ref[...] = dx
        dm, dg = _rms_bwd(dx, m_val, gp)
        dm_ref[...] = dm.astype(BF16)
        _accumulate(dg_ref, dg)
        _accumulate(loss_ref, jnp.full((1, 128), 0.5 * jnp.sum(err * err) * (1.0 / D), F32))

    return _call(body,name=name,grid=(T // ROW_TILE,), in_specs=[_rows(D), _rows(D), _vec(D), _rows(D)],
                          out_specs=[_rows(D), _rows(D), _vec(D), _vec(128)],
                          out_shape=[jax.ShapeDtypeStruct((T, D), F32), jax.ShapeDtypeStruct((T, D), BF16),
                                     jax.ShapeDtypeStruct((1, D), F32), jax.ShapeDtypeStruct((1, 128), F32)],
                          sem=("arbitrary",))(x, m, g_post, target)


def _norms_bwd(dx, dh, x_in, g_pre, m_prev, g_post_prev, name):
    T, D = dx.shape
    with_prev = m_prev is not None

    def body(*refs):
        if with_prev:
            dx_ref, dh_ref, x_ref, gq_ref, m_ref, gp_ref, dxo_ref, dgq_ref, dm_ref, dgp_ref = refs
        else:
            dx_ref, dh_ref, x_ref, gq_ref, dxo_ref, dgq_ref = refs
        d_in, dgq = _rms_bwd(dh_ref[...], x_ref[...], gq_ref[...])
        dxo = dx_ref[...] + d_in
        dxo_ref[...] = dxo
        _accumulate(dgq_ref, dgq)
        if with_prev:
            dm, dgp = _rms_bwd(dxo, m_ref[...], gp_ref[...])
            dm_ref[...] = dm.astype(BF16)
            _accumulate(dgp_ref, dgp)

    ins, in_specs = [dx, dh, x_in, g_pre], [_rows(D), _rows(D), _rows(D), _vec(D)]
    out_specs = [_rows(D), _vec(D)]
    out_shape = [jax.ShapeDtypeStruct((T, D), F32), jax.ShapeDtypeStruct((1, D), F32)]
    if with_prev:
        ins += [m_prev, g_post_prev]
        in_specs += [_rows(D), _vec(D)]
        out_specs += [_rows(D), _vec(D)]
        out_shape += [jax.ShapeDtypeStruct((T, D), BF16), jax.ShapeDtypeStruct((1, D), F32)]
    return _call(body,name=name,grid=(T // ROW_TILE,), in_specs=in_specs, out_specs=out_specs, out_shape=out_shape,
                          sem=("arbitrary",))(*ins)


def _window_weights(g):
    w = 2 << g
    return w, [jnp.where(j < w, 1.0, 0.0).astype(F32) for j in range(MAX_WINDOW)]


def _valid_count(r0, rows, w):
    t = (lax.broadcasted_iota(jnp.int32, (rows, 1), 0) + (r0 + 1)).astype(F32)
    return jnp.minimum(t, w.astype(F32))


def _pool_fwd(z, pool_w, pool_scale, name):
    T = z.shape[0]
    PG = pool_w.shape[-1]
    DP = N_GROUPS * PG
    rc = min(CHUNK, T)

    def body(u_ref, pw_ref, sc_ref, pooled_ref, y_ref, pad):
        w, wts = _window_weights(pl.program_id(0))
        pad[pl.ds(0, MAX_WINDOW), :] = jnp.zeros((MAX_WINDOW, PG), F32)
        pad[pl.ds(MAX_WINDOW, T), :] = u_ref[...]
        for r0 in range(0, T, rc):
            acc = jnp.zeros((rc, PG), F32)
            for j in range(MAX_WINDOW):
                acc = acc + wts[j] * pad[pl.ds(MAX_WINDOW + r0 - j, rc), :]
            pooled = acc / _valid_count(r0, rc, w) - u_ref[pl.ds(r0, rc), :]
            pooled_ref[pl.ds(r0, rc), :] = pooled.astype(BF16)
        mixed = jnp.dot(pooled_ref[...], pw_ref[...].astype(BF16), preferred_element_type=F32)
        y_ref[...] = (mixed * sc_ref[...]).astype(BF16)

    col = lambda g: (0, g)
    return _call(
        body, name=name,grid=(N_GROUPS,),
        in_specs=[pl.BlockSpec((T, PG), col), pl.BlockSpec((None, PG, PG), lambda g: (g, 0, 0)), pl.BlockSpec((1, PG), col)],
        out_specs=[pl.BlockSpec((T, PG), col), pl.BlockSpec((T, PG), col)],
        out_shape=[jax.ShapeDtypeStruct((T, DP), BF16), jax.ShapeDtypeStruct((T, DP), BF16)],
        scratch_shapes=[pltpu.VMEM((T + MAX_WINDOW, PG), F32)], sem=("parallel",))(z, pool_w, pool_scale)


def _pool_bwd(dy, pooled, pool_w, pool_scale, name):
    T = dy.shape[0]
    PG = pool_w.shape[-1]
    DP = N_GROUPS * PG
    rc = min(CHUNK, T)

    def body(dy_ref, pooled_ref, pw_ref, sc_ref, du_ref, dpw_ref, dsc_ref, pad, dp_ref):
        w, wts = _window_weights(pl.program_id(0))
        pooled_v, pw = pooled_ref[...], pw_ref[...].astype(BF16)
        dy_v = dy_ref[...]
        mixed = jnp.dot(pooled_v, pw, preferred_element_type=F32)
        dsc_ref[...] = jnp.sum(dy_v * mixed, axis=0, keepdims=True)
        dmixed = (dy_v * sc_ref[...]).astype(BF16)
        dpw_ref[...] = lax.dot_general(pooled_v, dmixed, _DIMS["tn"], preferred_element_type=F32)
        dp_ref[...] = lax.dot_general(dmixed, pw, _DIMS["nt"], preferred_element_type=F32)
        pad[pl.ds(T, MAX_WINDOW), :] = jnp.zeros((MAX_WINDOW, PG), F32)
        for r0 in range(0, T, rc):
            pad[pl.ds(r0, rc), :] = dp_ref[pl.ds(r0, rc), :] / _valid_count(r0, rc, w)
        for r0 in range(0, T, rc):
            acc = jnp.zeros((rc, PG), F32)
            for j in range(MAX_WINDOW):
                acc = acc + wts[j] * pad[pl.ds(r0 + j, rc), :]
            du_ref[pl.ds(r0, rc), :] = (acc - dp_ref[pl.ds(r0, rc), :]).astype(BF16)

    col = lambda g: (0, g)
    return _call(
        body, name=name,grid=(N_GROUPS,),
        in_specs=[pl.BlockSpec((T, PG), col), pl.BlockSpec((T, PG), col), pl.BlockSpec((None, PG, PG), lambda g: (g, 0, 0)),
                  pl.BlockSpec((1, PG), col)],
        out_specs=[pl.BlockSpec((T, PG), col), pl.BlockSpec((None, PG, PG), lambda g: (g, 0, 0)), pl.BlockSpec((1, PG), col)],
        out_shape=[jax.ShapeDtypeStruct((T, DP), BF16), jax.ShapeDtypeStruct((N_GROUPS, PG, PG), F32),
                   jax.ShapeDtypeStruct((1, DP), F32)],
        scratch_shapes=[pltpu.VMEM((T + MAX_WINDOW, PG), F32), pltpu.VMEM((T, PG), F32)],
        sem=("parallel",))(dy, pooled, pool_w, pool_scale)


def _conv_fwd(z, conv_w, conv_b, d_pool, name, tc=128):
    T = z.shape[0]
    DC = conv_w.shape[-1]
    rc = min(CHUNK, T)
    v0, g0 = d_pool // tc, (d_pool + DC) // tc

    def body(v_ref, gt_ref, w_ref, b_ref, a_ref, c_ref, pad):
        pad[pl.ds(0, CONV_PAD), :] = jnp.zeros((CONV_PAD, tc), F32)
        for r0 in range(0, T, rc):
            a = v_ref[pl.ds(r0, rc), :] * _sigmoid(gt_ref[pl.ds(r0, rc), :])
            a_ref[pl.ds(r0, rc), :] = a
            pad[pl.ds(CONV_PAD + r0, rc), :] = a
        for r0 in range(0, T, rc):
            acc = jnp.zeros((rc, tc), F32) + b_ref[...]
            for k in range(CONV_K):
                acc = acc + w_ref[pl.ds(k, 1), :] * pad[pl.ds(CONV_PAD - (CONV_K - 1) + k + r0, rc), :]
            c_ref[pl.ds(r0, rc), :] = acc

    col = lambda j: (0, j)
    return _call(
        body, name=name,grid=(DC // tc,),
        in_specs=[pl.BlockSpec((T, tc), lambda j: (0, v0 + j)), pl.BlockSpec((T, tc), lambda j: (0, g0 + j)),
                  pl.BlockSpec((CONV_K, tc), col), pl.BlockSpec((1, tc), col)],
        out_specs=[pl.BlockSpec((T, tc), col), pl.BlockSpec((T, tc), col)],
        out_shape=[jax.ShapeDtypeStruct((T, DC), F32), jax.ShapeDtypeStruct((T, DC), F32)],
        scratch_shapes=[pltpu.VMEM((T + CONV_PAD, tc), F32)], sem=("parallel",))(z, z, conv_w, conv_b)


def _conv_bwd(dc, a, z, conv_w, d_pool, name, tc=128):
    T, DC = dc.shape
    rc = min(CHUNK, T)
    v0, g0 = d_pool // tc, (d_pool + DC) // tc

    def body(dc_ref, a_ref, v_ref, gt_ref, w_ref, dv_ref, dg_ref, dw_ref, db_ref, apad, dpad):
        apad[pl.ds(0, CONV_PAD), :] = jnp.zeros((CONV_PAD, tc), F32)
        apad[pl.ds(CONV_PAD, T), :] = a_ref[...]
        dpad[pl.ds(0, T), :] = dc_ref[...]
        dpad[pl.ds(T, CONV_PAD), :] = jnp.zeros((CONV_PAD, tc), F32)
        db_ref[...] = jnp.sum(dc_ref[...], axis=0, keepdims=True)
        for k in range(CONV_K):
            acc = jnp.zeros((8, tc), F32)
            for r0 in range(0, T, rc):
                prod = dc_ref[pl.ds(r0, rc), :] * apad[pl.ds(CONV_PAD - (CONV_K - 1) + k + r0, rc), :]
                acc = acc + jnp.sum(prod.reshape(rc // 8, 8, tc), axis=0)
            dw_ref[pl.ds(k, 1), :] = jnp.sum(acc, axis=0, keepdims=True)
        for r0 in range(0, T, rc):
            da = jnp.zeros((rc, tc), F32)
            for k in range(CONV_K):
                da = da + w_ref[pl.ds(k, 1), :] * dpad[pl.ds(r0 + (CONV_K - 1) - k, rc), :]
            sig = _sigmoid(gt_ref[pl.ds(r0, rc), :])
            dv_ref[pl.ds(r0, rc), :] = (da * sig).astype(BF16)
            dg_ref[pl.ds(r0, rc), :] = (da * v_ref[pl.ds(r0, rc), :] * sig * (1.0 - sig)).astype(BF16)

    col = lambda j: (0, j)
    return _call(
        body, name=name,grid=(DC // tc,),
        in_specs=[pl.BlockSpec((T, tc), col), pl.BlockSpec((T, tc), col), pl.BlockSpec((T, tc), lambda j: (0, v0 + j)),
                  pl.BlockSpec((T, tc), lambda j: (0, g0 + j)), pl.BlockSpec((CONV_K, tc), col)],
        out_specs=[pl.BlockSpec((T, tc), col), pl.BlockSpec((T, tc), col), pl.BlockSpec((CONV_K, tc), col),
                   pl.BlockSpec((1, tc), col)],
        out_shape=[jax.ShapeDtypeStruct((T, DC), BF16), jax.ShapeDtypeStruct((T, DC), BF16),
                   jax.ShapeDtypeStruct((CONV_K, DC), F32), jax.ShapeDtypeStruct((1, DC), F32)],
        scratch_shapes=[pltpu.VMEM((T + CONV_PAD, tc), F32), pltpu.VMEM((T + CONV_PAD, tc), F32)],
        sem=("parallel",))(dc, a, z, z, conv_w)


def _layer_norm_parts(c, g, b):
    mu = jnp.mean(c, axis=-1, keepdims=True)
    xc = c - mu
    rstd = lax.rsqrt(jnp.mean(xc * xc, axis=-1, keepdims=True) + EPS)
    xhat = xc * rstd
    return xhat, rstd, xhat * g + b


def _ln_silu_fwd(c, g, b, name):
    T, DC = c.shape

    def body(c_ref, g_ref, b_ref, y_ref):
        _, _, ln = _layer_norm_parts(c_ref[...], g_ref[...], b_ref[...])
        y_ref[...] = (ln * _sigmoid(ln)).astype(BF16)

    return _call(body,name=name,grid=(T // ROW_TILE,), in_specs=[_rows(DC), _vec(DC), _vec(DC)], out_specs=_rows(DC),
                          out_shape=jax.ShapeDtypeStruct((T, DC), BF16), sem=("parallel",))(c, g, b)


def _ln_silu_bwd(dy, c, g, b, name):
    T, DC = c.shape

    def body(dy_ref, c_ref, g_ref, b_ref, dc_ref, dg_ref, db_ref):
        gain = g_ref[...]
        xhat, rstd, ln = _layer_norm_parts(c_ref[...], gain, b_ref[...])
        s = _sigmoid(ln)
        dln = dy_ref[...] * (s * (1.0 + ln * (1.0 - s)))
        _accumulate(dg_ref, jnp.sum(dln * xhat, axis=0, keepdims=True))
        _accumulate(db_ref, jnp.sum(dln, axis=0, keepdims=True))
        dxh = dln * gain
        dc_ref[...] = rstd * (dxh - jnp.mean(dxh, axis=-1, keepdims=True) - xhat * jnp.mean(dxh * xhat, axis=-1, keepdims=True))

    return _call(body,name=name,grid=(T // ROW_TILE,),
                          in_specs=[pl.BlockSpec((ROW_TILE, DC), lambda i: (i, 1)), _rows(DC), _vec(DC), _vec(DC)],
                          out_specs=[_rows(DC), _vec(DC), _vec(DC)],
                          out_shape=[jax.ShapeDtypeStruct((T, DC), F32), jax.ShapeDtypeStruct((1, DC), F32),
                                     jax.ShapeDtypeStruct((1, DC), F32)],
                          sem=("arbitrary",))(dy, c, g, b)


def _short_specs(T, DS, tc):
    n = DS // tc
    return [pl.BlockSpec((T, tc), lambda j: (0, j)), pl.BlockSpec((T, tc), lambda j: (0, n + j)),
            pl.BlockSpec((T, tc), lambda j: (0, 2 * n + j))]


def _short_fwd(z, w, name, tc=256):
    T = z.shape[0]
    DS = w.shape[-1]
    rc = min(CHUNK, T)

    def body(b_ref, cg_ref, u_ref, w_ref, y_ref, pad):
        pad[pl.ds(0, SHORT_PAD), :] = jnp.zeros((SHORT_PAD, tc), F32)
        pad[pl.ds(SHORT_PAD, T), :] = cg_ref[...] * u_ref[...]
        for r0 in range(0, T, rc):
            r = jnp.zeros((rc, tc), F32)
            for k in range(SHORT_K):
                r = r + w_ref[pl.ds(k, 1), :] * pad[pl.ds(SHORT_PAD - (SHORT_K - 1) + k + r0, rc), :]
            y_ref[pl.ds(r0, rc), :] = (b_ref[pl.ds(r0, rc), :] * r).astype(BF16)

    col = lambda j: (0, j)
    return _call(body,name=name,grid=(DS // tc,), in_specs=_short_specs(T, DS, tc) + [pl.BlockSpec((SHORT_K, tc), col)],
                          out_specs=pl.BlockSpec((T, tc), col), out_shape=jax.ShapeDtypeStruct((T, DS), BF16),
                          scratch_shapes=[pltpu.VMEM((T + SHORT_PAD, tc), F32)], sem=("parallel",))(z, z, z, w)


def _short_bwd(dy, z, w, name, tc=256):
    T, DS = dy.shape
    rc = min(CHUNK, T)

    def body(dy_ref, b_ref, cg_ref, u_ref, w_ref, db_ref, dcg_ref, du_ref, dw_ref, qpad, rpad):
        qpad[pl.ds(0, SHORT_PAD), :] = jnp.zeros((SHORT_PAD, tc), F32)
        qpad[pl.ds(SHORT_PAD, T), :] = cg_ref[...] * u_ref[...]
        rpad[pl.ds(0, T), :] = dy_ref[...] * b_ref[...]
        rpad[pl.ds(T, SHORT_PAD), :] = jnp.zeros((SHORT_PAD, tc), F32)
        accs = [jnp.zeros((8, tc), F32) for _ in range(SHORT_K)]
        for r0 in range(0, T, rc):
            r = jnp.zeros((rc, tc), F32)
            dq = jnp.zeros((rc, tc), F32)
            dr = rpad[pl.ds(r0, rc), :]
            for k in range(SHORT_K):
                q_k = qpad[pl.ds(SHORT_PAD - (SHORT_K - 1) + k + r0, rc), :]
                r = r + w_ref[pl.ds(k, 1), :] * q_k
                dq = dq + w_ref[pl.ds(k, 1), :] * rpad[pl.ds(r0 + (SHORT_K - 1) - k, rc), :]
                accs[k] = accs[k] + jnp.sum((dr * q_k).reshape(rc // 8, 8, tc), axis=0)
            db_ref[pl.ds(r0, rc), :] = (dy_ref[pl.ds(r0, rc), :] * r).astype(BF16)
            dcg_ref[pl.ds(r0, rc), :] = (dq * u_ref[pl.ds(r0, rc), :]).astype(BF16)
            du_ref[pl.ds(r0, rc), :] = (dq * cg_ref[pl.ds(r0, rc), :]).astype(BF16)
        for k in range(SHORT_K):
            dw_ref[pl.ds(k, 1), :] = jnp.sum(accs[k], axis=0, keepdims=True)

    col = lambda j: (0, j)
    tile = pl.BlockSpec((T, tc), col)
    return _call(body,name=name,grid=(DS // tc,),
                          in_specs=[tile] + _short_specs(T, DS, tc) + [pl.BlockSpec((SHORT_K, tc), col)],
                          out_specs=[tile, tile, tile, pl.BlockSpec((SHORT_K, tc), col)],
                          out_shape=[jax.ShapeDtypeStruct((T, DS), BF16)] * 3 + [jax.ShapeDtypeStruct((SHORT_K, DS), F32)],
                          scratch_shapes=[pltpu.VMEM((T + SHORT_PAD, tc), F32), pltpu.VMEM((T + SHORT_PAD, tc), F32)],
                          sem=("parallel",))(dy, z, z, z, w)


def _tile_rows(rows, cols, n_bufs):
    budget = VMEM_LIMIT_BYTES * 3 // 4 // (2 * n_bufs * 4 * cols)
    tr = rows
    while tr > budget and tr % 16 == 0:
        tr //= 2
    return tr


def _placed_call(body, name, place, grid, in_specs, out_specs, out_shape, ins):
    return _call(body,prefetch=place, name=name, grid=grid, in_specs=in_specs, out_specs=out_specs, out_shape=out_shape,
                 sem=("parallel",))(*ins)


def _cast_into_full(w, layer, kind, place, name):
    _, R, C = w.shape
    tr = _tile_rows(R, C, 2)
    nb = R // tr
    if kind == "col":
        full, out_spec = (R, C * N_CHIPS), pl.BlockSpec((tr, C), lambda i, s: (i, s[0]))
    else:
        full, out_spec = (R * N_CHIPS, C), pl.BlockSpec((tr, C), lambda i, s: (s[0] * nb + i, 0))

    def body(s_ref, w_ref, o_ref):
        o_ref[...] = w_ref[...].astype(BF16)

    return _placed_call(body, name, place, (nb,), [pl.BlockSpec((None, tr, C), lambda i, s: (layer, i, 0))], out_spec,
                        jax.ShapeDtypeStruct(full, BF16), [w])


def _add_pair(grad, theirs, kind, place, name):
    R, C = grad.shape
    piece_rows = R // 2 if kind == "col" else R // N_CHIPS // 2
    tr = _tile_rows(piece_rows, C, 3)
    nb = piece_rows // tr
    if kind == "col":
        g_spec = pl.BlockSpec((tr, C), lambda i, s: (s[1] * nb + i, 0))
    else:
        g_spec = pl.BlockSpec((tr, C), lambda i, s: ((2 * (i // nb) + s[1]) * nb + i % nb, 0))
    flat = pl.BlockSpec((tr, C), lambda i, s: (i, 0))

    def body(s_ref, a_ref, b_ref, o_ref):
        o_ref[...] = (a_ref[...].astype(F32) + b_ref[...].astype(F32)).astype(BF16)

    return _placed_call(body, name, place, (R // 2 // tr,), [g_spec, flat], flat, jax.ShapeDtypeStruct((R // 2, C), BF16),
                        [grad, theirs])


def _sum_chips(chip_sum, arrived, kind, place, name):
    _, H, W = arrived.shape
    tr = _tile_rows(H, W, 6)
    nb = H // tr
    if kind == "col":
        own_spec = pl.BlockSpec((tr, W), lambda i, s: (i, s[0]))
    else:
        own_spec = pl.BlockSpec((tr, W), lambda i, s: (s[0] * nb + i, 0))

    def body(s_ref, p_ref, r_ref, o_ref):
        acc = p_ref[...].astype(F32)
        for i in range(N_CHIPS - 1):
            acc = acc + r_ref[i].astype(F32)
        o_ref[...] = acc

    return _placed_call(body, name, place, (nb,), [own_spec, pl.BlockSpec((N_CHIPS - 1, tr, W), lambda i, s: (0, i, 0))],
                        pl.BlockSpec((tr, W), lambda i, s: (s[1] * nb + i, 0)), jax.ShapeDtypeStruct((2 * H, W), F32),
                        [chip_sum, arrived])


def _adamw_values(w, g, m, v):
    m = ADAM_B1 * m + (1.0 - ADAM_B1) * g
    v = ADAM_B2 * v + (1.0 - ADAM_B2) * (g * g)
    m_hat = m / (1.0 - ADAM_B1 ** ADAM_STEP)
    v_hat = v / (1.0 - ADAM_B2 ** ADAM_STEP)
    return -ADAM_LR * (m_hat / (jnp.sqrt(v_hat) + ADAM_EPS) + ADAM_WD * w), m, v


def _adamw(w, g, m, v, name, layer=0, carried=None):
    L, R, C = w.shape
    tr = _tile_rows(R, C, 8)

    def body(w_ref, g_ref, m_ref, v_ref, *rest):
        go_ref, d_ref, mo_ref, vo_ref = rest[-4:]
        g_val = g_ref[...]
        d, m_new, v_new = _adamw_values(w_ref[...], g_val, m_ref[...], v_ref[...])
        go_ref[...], d_ref[...], mo_ref[...], vo_ref[...] = g_val, d, m_new, v_new

    lay = pl.BlockSpec((None, tr, C), lambda i: (layer, i, 0))
    ins = [w, g, m, v]
    in_specs = [lay, pl.BlockSpec((tr, C), lambda i: (i, 0)), lay, lay]
    aliases = {}
    if carried is not None:
        ins += list(carried)
        in_specs += [pl.BlockSpec(memory_space=pl.ANY)] * 4
        aliases = {4 + i: i for i in range(4)}
    return _call(body, name=name, grid=(R // tr,), in_specs=in_specs, out_specs=[lay] * 4,
                 out_shape=[jax.ShapeDtypeStruct((L, R, C), F32)] * 4, input_output_aliases=aliases, sem=("parallel",))(*ins)


def _aligned(v, m):
    return v if isinstance(v, int) else pl.multiple_of(v, m)


def _place():
    x, y, c = lax.axis_index("x"), lax.axis_index("y"), lax.axis_index("c")
    other_chips = [(x, 1 - y), (1 - x, y), (1 - x, 1 - y)]
    return x, y, c, 2 * x + y, other_chips


def _chip_index(chip):
    return 2 * chip[0] + chip[1]


def _piece(ref, kind, k, h):
    R, C = ref.shape
    if kind == "col":
        return ref.at[pl.ds(_aligned(h * (R // 2), 16), R // 2), pl.ds(_aligned(k * (C // N_CHIPS), 128), C // N_CHIPS)]
    rs = R // N_CHIPS
    return ref.at[pl.ds(_aligned(k * rs + h * (rs // 2), 16), rs // 2), :]


def _compact_piece(ref, kind, k):
    R2, C = ref.shape
    if kind == "col":
        return ref.at[:, pl.ds(_aligned(k * (C // N_CHIPS), 128), C // N_CHIPS)]
    return ref.at[pl.ds(_aligned(k * (R2 // N_CHIPS), 16), R2 // N_CHIPS), :]


def _half_rows(ref, h):
    R = ref.shape[0]
    return ref.at[pl.ds(_aligned(h * (R // 2), 16), R // 2), :]


class _Copies:
    def __init__(self, send_sems, recv_sems):
        self.send_sems, self.recv_sems = send_sems, recv_sems
        self.n_remote = 0

    def remote(self, src, dst, device):
        k = self.n_remote
        self.n_remote += 1
        return pltpu.make_async_remote_copy(src_ref=src, dst_ref=dst, send_sem=self.send_sems.at[k], recv_sem=self.recv_sems.at[k],
                                            device_id=device, device_id_type=MESH)


class _Job:
    def __init__(self, ins, out_shape, aliases, n_remote, build):
        self.ins, self.out_shape, self.aliases, self.n_remote, self.build = list(ins), list(out_shape), dict(aliases), n_remote, build


class _Flying:
    def __init__(self, job, send_sems, recv_sems, bufs, token):
        self.job, self.send_sems, self.recv_sems, self.bufs, self.token = job, send_sems, recv_sems, bufs, token


def _job_refs(job, buf_refs):
    n_out = len(job.out_shape)
    kept = [i for i in range(len(job.ins)) if i not in job.aliases]
    ins = [buf_refs[job.aliases[i]] if i in job.aliases else buf_refs[n_out + kept.index(i)] for i in range(len(job.ins))]
    return ins, list(buf_refs[:n_out])


def _start_job(job, name, after=()):
    n_in, n_out, n_after = len(job.ins), len(job.out_shape), len(after)
    kept = [i for i in range(n_in) if i not in job.aliases]
    n_bufs = n_out + len(kept)

    def body(*refs):
        in_refs, out_refs = refs[:n_in], refs[n_in + n_after:n_in + n_after + n_out]
        send_sems, recv_sems, token = refs[n_in + n_after + n_bufs:]
        for d in job.build(in_refs, out_refs, _Copies(send_sems, recv_sems)):
            d.start()
        token[...] = jnp.zeros_like(token)

    aliases = dict(job.aliases)
    aliases.update({i: n_out + k for k, i in enumerate(kept)})
    sems = pltpu.SemaphoreType.DMA((job.n_remote,))
    outs = pl.pallas_call(
        body, name=name, in_specs=[HBM] * n_in + [ANY] * n_after,
        out_specs=[HBM] * n_bufs + [SEM, SEM, pl.BlockSpec(memory_space=pltpu.VMEM)],
        out_shape=job.out_shape + [jax.ShapeDtypeStruct(job.ins[i].shape, job.ins[i].dtype) for i in kept]
        + [sems, sems, jax.ShapeDtypeStruct((8, 128), F32)],
        input_output_aliases=aliases,
        compiler_params=pltpu.CompilerParams(has_side_effects=pltpu.SideEffectType.DATAFLOW_SIDE_EFFECTING))(*job.ins, *after)
    return _Flying(job, outs[n_bufs], outs[n_bufs + 1], list(outs[:n_bufs]), outs[n_bufs + 2])


def _wait_job(flying, name, after=()):
    job, n_bufs, n_after = flying.job, len(flying.bufs), len(after)

    def body(*refs):
        in_refs, out_refs = _job_refs(job, refs[:n_bufs])
        send_sems, recv_sems = refs[n_bufs:n_bufs + 2]
        copies = job.build(in_refs, out_refs, _Copies(send_sems, recv_sems))
        for d in copies:
            d.wait_send()
        for d in copies:
            d.wait_recv()

    outs = pl.pallas_call(
        body, name=name, in_specs=[HBM] * n_bufs + [SEM, SEM] + [ANY] * n_after, out_specs=[HBM] * n_bufs,
        out_shape=[jax.ShapeDtypeStruct(b.shape, b.dtype) for b in flying.bufs],
        input_output_aliases={i: i for i in range(n_bufs)},
        compiler_params=pltpu.CompilerParams(has_side_effects=pltpu.SideEffectType.DATAFLOW_SIDE_EFFECTING))(
            *flying.bufs, flying.send_sems, flying.recv_sems, *after)
    return list(outs[:len(job.out_shape)]), list(outs[len(job.out_shape):])


def _in_place(arrays):
    return [jax.ShapeDtypeStruct(a.shape, a.dtype) for a in arrays], {u: u for u in range(len(arrays))}


def _rows_part(ref, part, n_parts):
    h = ref.shape[0] // n_parts
    return ref.at[pl.ds(part * h, h), :]


def _gather_job(full, kind, stage):
    def build(in_refs, out_refs, cp):
        x, y, c, me, (y_nbr, x_nbr, diagonal) = _place()
        (ref,) = out_refs
        sibling = (x, y, 1 - c)
        if stage == 1:
            mine = _piece(ref, kind, me, c)
            return [cp.remote(mine, mine, (*y_nbr, c)), cp.remote(mine, mine, (*x_nbr, c))]
        from_y, from_x = _piece(ref, kind, _chip_index(y_nbr), c), _piece(ref, kind, _chip_index(x_nbr), c)
        copies = []
        if stage in (2, "relay"):
            relay_0, relay_1 = _rows_part(from_x, 0, 2), _rows_part(from_y, 1, 2)
            copies += [cp.remote(relay_0, relay_0, (*y_nbr, c)), cp.remote(relay_1, relay_1, (*x_nbr, c))]
        if stage in (2, "direct"):
            copies += [cp.remote(from_y, from_y, sibling), cp.remote(from_x, from_x, sibling)]
        if stage == 3:
            from_diagonal = _piece(ref, kind, _chip_index(diagonal), c)
            copies.append(cp.remote(from_diagonal, from_diagonal, sibling))
        return copies

    return _Job([full], *_in_place([full]), {1: 2, 2: 4, "relay": 2, "direct": 2, 3: 1}[stage], build)


def _gather_small_job(fulls, axes):
    def build(in_refs, out_refs, cp):
        x, y, c, me, chips = _place()
        copies = []
        for ref, ax in zip(out_refs, axes):
            n = ref.shape[ax] // N_CHIPS
            idx = [slice(None)] * len(ref.shape)
            idx[ax] = pl.ds(_aligned(me * n, n), n)
            mine = ref.at[tuple(idx)]
            copies += [cp.remote(mine, mine, (*chip, c)) for chip in chips]
        return copies

    return _Job(fulls, *_in_place(fulls), 3 * len(fulls), build)


def _exchange_halves_job(grads, kinds):
    def build(in_refs, out_refs, cp):
        x, y, c, me, chips = _place()
        copies = []
        for src, dst, kind in zip(in_refs, out_refs, kinds):
            if kind == "col":
                copies.append(cp.remote(_half_rows(src, 1 - c), dst, (x, y, 1 - c)))
            else:
                copies += [cp.remote(_piece(src, "row", k, 1 - c), _compact_piece(dst, "row", k), (x, y, 1 - c))
                           for k in range(N_CHIPS)]
        return copies

    out_shape = [jax.ShapeDtypeStruct((g.shape[0] // 2, g.shape[1]), g.dtype) for g in grads]
    return _Job(grads, out_shape, {}, sum(1 if k == "col" else N_CHIPS for k in kinds), build)


def _scatter_job(half, kind):
    def build(in_refs, out_refs, cp):
        x, y, c, me, chips = _place()
        (src,), (dst,) = in_refs, out_refs
        return [cp.remote(_compact_piece(src, kind, _chip_index(chip)), dst.at[r], (*chip, c)) for r, chip in enumerate(chips)]

    part_shape = (half.shape[0], half.shape[1] // N_CHIPS) if kind == "col" else (half.shape[0] // N_CHIPS, half.shape[1])
    return _Job([half], [jax.ShapeDtypeStruct((N_CHIPS - 1,) + part_shape, half.dtype)], {}, N_CHIPS - 1, build)


def _share_job(shards):
    def build(in_refs, out_refs, cp):
        x, y, c, me, chips = _place()
        copies = []
        for ref in out_refs:
            mine = _half_rows(ref, c)
            copies.append(cp.remote(mine, mine, (x, y, 1 - c)))
        return copies

    return _Job(shards, *_in_place(shards), len(shards), build)


N_DEVICES = 2 * N_CHIPS


def _small_exchange_job(slots):
    def build(in_refs, out_refs, cp):
        x, y, c, me, chips = _place()
        (ref,) = out_refs
        mine = ref.at[2 * me + c]
        return [cp.remote(mine, mine, (x ^ (p >> 2), y ^ ((p >> 1) & 1), c ^ (p & 1))) for p in range(1, N_DEVICES)]

    return _Job([slots], *_in_place([slots]), N_DEVICES - 1, build)


def _in_own_slot(packed, place, name):
    R, C = packed.shape

    def body(s_ref, p_ref, o_ref):
        o_ref[...] = p_ref[...]

    return _placed_call(body, name, place, (1,), [pl.BlockSpec((R, C), lambda i, s: (0, 0))],
                        pl.BlockSpec((None, R, C), lambda i, s: (2 * s[0] + s[1], 0, 0)),
                        jax.ShapeDtypeStruct((N_DEVICES, R, C), F32), [packed])


def _sum_slots(slots, name):
    n, R, C = slots.shape

    def body(s_ref, o_ref):
        acc = s_ref[0]
        for i in range(1, n):
            acc = acc + s_ref[i]
        o_ref[...] = acc

    return _call(body,name=name, grid=(1,), in_specs=[pl.BlockSpec((n, R, C), lambda i: (0, 0, 0))],
                 out_specs=pl.BlockSpec((R, C), lambda i: (0, 0)), out_shape=jax.ShapeDtypeStruct((R, C), F32),
                 sem=("arbitrary",))(slots)


def _packed_rows(size, width):
    return -(-size // (8 * width)) * 8


def _pack(arrays, width):
    rows = []
    for a in arrays:
        flat = a.reshape(-1)
        n_rows = _packed_rows(flat.shape[0], width)
        rows.append(jnp.pad(flat, (0, n_rows * width - flat.shape[0])).reshape(n_rows, width))
    return jnp.concatenate(rows, axis=0)


def _unpack(packed, shapes):
    out, r0, width = [], 0, packed.shape[1]
    for shape in shapes:
        size = 1
        for d in shape:
            size *= d
        out.append(packed[r0:r0 + _packed_rows(size, width)].reshape(-1)[:size].reshape(shape))
        r0 += _packed_rows(size, width)
    return out


class _Backlog:
    def __init__(self, first):
        self.now, self.free, self.flights, self.last, self.chain = 0.0, {"ici": 0.0, "d2d": 0.0}, [], first, []

    def run(self, fn, us, *args, **kw):
        out = fn(*args, **kw)
        self.now += us
        self.last = out[0] if isinstance(out, (list, tuple)) else out
        self.poll()
        return out

    def start(self, job, name, link, cost, done):
        flying = _start_job(job, name + "_start", self.chain)
        self.chain = [flying.token]
        _Behind.pending.append(flying.token)
        ends = max(self.now, self.free[link]) + cost
        self.free[link] = ends
        self.flights.append((ends + LANDING_SLACK_US, name, flying, done))
        self.flights.sort(key=lambda f: f[0])
        return flying

    def poll(self, block=False):
        while self.flights and (block or self.flights[0][0] <= self.now):
            ends, name, flying, done = self.flights.pop(0)
            self.now, block = max(self.now, ends), False
            done(*_wait_job(flying, name + "_wait", [self.last] + self.chain))


class _GatherStream:
    def __init__(self, backlog, bufs, kinds, costs, early=()):
        self.backlog, self.bufs, self.kinds, self.costs, self.begun, self.complete = backlog, bufs, kinds, costs, 0, set()
        self.early, self.relays, self.near_complete = set(early), {}, set()
        self.begin()

    def begin(self):
        u, self.begun = self.begun, self.begun + 1
        self.backlog.start(_gather_job(self.bufs[u], self.kinds[u], 1), "gather_%d" % u, "ici", 0.5 * self.costs[u],
                           lambda outs, kept: self.arrived(u, outs[0]))

    def arrived(self, u, buf):
        self.bufs[u] = buf
        if u in self.early:
            self.relays[u] = self.backlog.start(_gather_job(buf, self.kinds[u], "relay"), "relay_%d" % u, "ici",
                                                0.25 * self.costs[u], lambda outs, kept: self.relayed(u, outs[0]))
            self.backlog.start(_gather_job(self.relays[u].bufs[0], self.kinds[u], "direct"), "direct_%d" % u, "d2d",
                               0.5 * D2D_SHARE * self.costs[u], lambda outs, kept: self.near(u, outs[0]))
        else:
            self.backlog.start(_gather_job(buf, self.kinds[u], 2), "relay_%d" % u, "ici", 0.25 * self.costs[u],
                               lambda outs, kept: self.relayed(u, outs[0]))
        while self.begun <= min(u + GATHER_WINDOW[u], len(self.bufs) - 1):
            self.begin()

    def near(self, u, buf):
        self.bufs[u] = self.relays[u].bufs[0] = buf
        self.near_complete.add(u)

    def nearly_ready(self, u):
        while u not in self.near_complete:
            assert self.backlog.flights, "weight %d is not on its way" % u
            self.backlog.poll(block=True)
        return self.bufs[u]

    def relayed(self, u, buf):
        assert u not in self.early or u in self.near_complete, "the relay of weight %d is waited for before its sibling copies" % u
        self.bufs[u] = buf
        self.backlog.start(_gather_job(buf, self.kinds[u], 3), "handon_%d" % u, "d2d", D2D_SHARE * self.costs[u],
                           lambda outs, kept: self.handed(u, outs[0]))

    def handed(self, u, buf):
        self.bufs[u] = buf
        self.complete.add(u)

    def ready(self, u):
        while u not in self.complete:
            assert self.backlog.flights, "weight %d is not on its way" % u
            self.backlog.poll(block=True)
        return self.bufs[u]


class _GradStream:
    def __init__(self, backlog, u, name, kind, cost, g, place, results):
        self.backlog, self.u, self.name, self.kind, self.cost, self.place, self.results = backlog, u, name, kind, cost, place, results
        backlog.start(_exchange_halves_job([g], [kind]), "to_sibling_" + name, "d2d", D2D_SHARE * cost, self.exchanged)

    def exchanged(self, outs, kept):
        chip_sum = self.backlog.run(_add_pair, SIDE_KERNEL_US, kept[0], outs[0], self.kind, self.place, "chip_sum_" + self.name)
        self.backlog.start(_scatter_job(chip_sum, self.kind), "to_owners_" + self.name, "ici", self.cost, self.scattered)

    def scattered(self, outs, kept):
        reduced = self.backlog.run(_sum_chips, SIDE_KERNEL_US, kept[0], outs[0], self.kind, self.place, "reduce_" + self.name)
        self.backlog.start(_share_job([reduced]), "share_" + self.name, "d2d", D2D_SHARE * self.cost, self.shared)

    def shared(self, outs, kept):
        self.results[self.u] = outs[0]


SIDE_KERNEL_US = 12.0
D2D_SHARE = 0.15
LANDING_SLACK_US = 5.0
GATHER_WINDOW = (1, 1, 1, 2, 2, 1, 1, 1)


def kernel(x, mix_pre_g, mix_post_g, ffn_pre_g, ffn_post_g, ab_w_in, pool_w, pool_scale, conv_w, conv_b, conv_ln_g, conv_ln_b, ab_w_out, sc_w_in, sc_conv_w, sc_w_out, ffn_w1, ffn_w2, loss_target, m_mix_pre_g, m_mix_post_g, m_ffn_pre_g, m_ffn_post_g, m_ab_w_in, m_pool_w, m_pool_scale, m_conv_w, m_conv_b, m_conv_ln_g, m_conv_ln_b, m_ab_w_out, m_sc_w_in, m_sc_conv_w, m_sc_w_out, m_ffn_w1, m_ffn_w2, v_mix_pre_g, v_mix_post_g, v_ffn_pre_g, v_ffn_post_g, v_ab_w_in, v_pool_w, v_pool_scale, v_conv_w, v_conv_b, v_conv_ln_g, v_conv_ln_b, v_ab_w_out, v_sc_w_in, v_sc_conv_w, v_sc_w_out, v_ffn_w1, v_ffn_w2):
    x0, target = x[0], loss_target[0]
    T, D = x0.shape
    DP = pool_scale.shape[-1]
    gain = lambda g, layer: g[layer][None, :]

    big = [("ab_w_in", ab_w_in, 0, "col", 67.0), ("ab_w_out", ab_w_out, 0, "row", 44.0),
           ("ffn_w1_0", ffn_w1, 0, "col", 177.0), ("ffn_w2_0", ffn_w2, 0, "row", 177.0),
           ("sc_w_in", sc_w_in, 0, "col", 133.0), ("sc_w_out", sc_w_out, 0, "row", 44.0),
           ("ffn_w1_1", ffn_w1, 1, "col", 177.0), ("ffn_w2_1", ffn_w2, 1, "row", 177.0)]
    kinds = [b[3] for b in big]
    chip = 2 * lax.axis_index("x") + lax.axis_index("y")
    place = jnp.stack([chip, lax.axis_index("c")]).astype(jnp.int32)

    def own_in_zeros(shard, ax):
        full = jnp.zeros(tuple(d * N_CHIPS if i == ax else d for i, d in enumerate(shard.shape)), shard.dtype)
        return lax.dynamic_update_slice_in_dim(full, shard, chip * shard.shape[ax], axis=ax)

    W = [_cast_into_full(w, layer, kind, place, "cast_" + name) for name, w, layer, kind, _ in big]
    smalls = [own_in_zeros(pool_w[0], 1), own_in_zeros(conv_w[0], 1), own_in_zeros(sc_conv_w[0], 1)]
    backlog = _Backlog(x0)
    run = backlog.run
    small_weights = []
    backlog.start(_gather_small_job(smalls, [1, 1, 1]), "gather_small", "ici", 6.0, lambda outs, kept: small_weights.extend(outs))
    gather = _GatherStream(backlog, W, kinds, [b[4] for b in big], early=(2, 3, 6, 7))

    relu_sq = lambda acc: (jnp.maximum(acc, 0.0), jnp.square(jnp.maximum(acc, 0.0)))
    relu_sq_bwd = lambda acc, a: (acc * (2.0 * a.astype(F32)),)

    h0 = run(_norm_fwd, 12.0, x0, gain(mix_pre_g, 0), "norm_in")
    z0 = run(_matmul, 35.0, h0, gather.ready(0), "nn", "mix0_in")
    while not small_weights:
        backlog.poll(block=True)
    pool_w_full, conv_w_full, sc_conv_w_full = small_weights
    pooled, y_pool = run(_pool_fwd, 23.0, z0, pool_w_full, pool_scale, "pool_fwd")
    a_conv, c_conv = run(_conv_fwd, 25.0, z0, conv_w_full, conv_b, DP, "conv_fwd")
    y_conv = run(_ln_silu_fwd, 10.0, c_conv, conv_ln_g, conv_ln_b, "ln_silu_fwd")
    y0 = jnp.concatenate([y_pool, y_conv], axis=1)
    m0 = run(_matmul, 25.0, y0, gather.ready(1), "nn", "mix0_out")
    x1, h1 = run(_residual_norm, 21.0, x0, m0, gain(mix_post_g, 0), gain(ffn_pre_g, 0), "res_mix0")
    begun = run(_matmul_by_chip_cols, 61.0, h1, gather.nearly_ready(2), place, "ffn0_up_near", (BF16, BF16), relu_sq)
    a0, a0sq = run(_matmul_by_chip_cols, 20.0, h1, gather.ready(2), place, "ffn0_up", (BF16, BF16), relu_sq, begun=begun)
    f0 = run(_matmul_by_chip_rows, 63.0, a0sq, gather.nearly_ready(3), place, "ffn0_down_near")
    f0 = run(_matmul_by_chip_rows, 21.0, a0sq, gather.ready(3), place, "ffn0_down", rest=f0)
    x2, h2 = run(_residual_norm, 22.0, x1, f0, gain(ffn_post_g, 0), gain(mix_pre_g, 1), "res_ffn0")
    z1 = run(_matmul, 62.0, h2, gather.ready(4), "nn", "mix1_in")
    y1 = run(_short_fwd, 22.0, z1, sc_conv_w_full, "short_fwd")
    m1 = run(_matmul, 25.0, y1, gather.ready(5), "nn", "mix1_out")
    x3, h3 = run(_residual_norm, 21.0, x2, m1, gain(mix_post_g, 1), gain(ffn_pre_g, 1), "res_mix1")
    begun = run(_matmul_by_chip_cols, 61.0, h3, gather.nearly_ready(6), place, "ffn1_up_near", (BF16, BF16), relu_sq)
    a1, a1sq = run(_matmul_by_chip_cols, 20.0, h3, gather.ready(6), place, "ffn1_up", (BF16, BF16), relu_sq, begun=begun)
    f1 = run(_matmul_by_chip_rows, 63.0, a1sq, gather.nearly_ready(7), place, "ffn1_down_near")
    f1 = run(_matmul_by_chip_rows, 21.0, a1sq, gather.ready(7), place, "ffn1_down", rest=f1)
    w_in0, w_out0, w1_0, w2_0, w_in1, w_out1, w1_1, w2_1 = W

    grads_big = [None] * len(big)


    def reduce_grad(u, g):
        name, _, _, kind, cost = big[u]
        _GradStream(backlog, u, name, kind, cost, g, place, grads_big)

    dx, df1, d_ffn_post_1, loss_row = run(_loss_and_last_norm_bwd, 30.0, x3, f1, gain(ffn_post_g, 1), target, "loss")
    reduce_grad(7, run(_matmul, 80.0, a1sq, df1, "tn", "ffn1_down_dw", out_dtypes=(BF16,)))
    dz = run(_matmul, 82.0, df1, w2_1, "nt", "ffn1_down_dx", out_dtypes=(BF16,), epilogue=relu_sq_bwd, epi=(a1,))
    reduce_grad(6, run(_matmul, 80.0, h3, dz, "tn", "ffn1_up_dw", out_dtypes=(BF16,)))
    dh = run(_matmul, 87.0, dz, w1_1, "nt", "ffn1_up_dx", tk=LONG_K_TILE)
    dx, d_ffn_pre_1, dm1, d_mix_post_1 = run(_norms_bwd, 36.0, dx, dh, x3, gain(ffn_pre_g, 1), m1, gain(mix_post_g, 1), "norms_bwd3")

    reduce_grad(5, run(_matmul, 24.0, y1, dm1, "tn", "mix1_out_dw", out_dtypes=(BF16,)))
    dy1 = run(_matmul, 25.0, dm1, w_out1, "nt", "mix1_out_dx")
    db, dcg, du, d_sc_conv_w = run(_short_bwd, 41.0, dy1, z1, sc_conv_w_full, "short_bwd")
    dz1 = jnp.concatenate([db, dcg, du], axis=1)
    reduce_grad(4, run(_matmul, 62.0, h2, dz1, "tn", "mix1_in_dw", out_dtypes=(BF16,)))
    dh = run(_matmul, 68.0, dz1, w_in1, "nt", "mix1_in_dx")
    dx, d_mix_pre_1, df0, d_ffn_post_0 = run(_norms_bwd, 35.0, dx, dh, x2, gain(mix_pre_g, 1), f0, gain(ffn_post_g, 0), "norms_bwd2")

    reduce_grad(3, run(_matmul, 80.0, a0sq, df0, "tn", "ffn0_down_dw", out_dtypes=(BF16,)))
    dz = run(_matmul, 82.0, df0, w2_0, "nt", "ffn0_down_dx", out_dtypes=(BF16,), epilogue=relu_sq_bwd, epi=(a0,))
    reduce_grad(2, run(_matmul, 80.0, h1, dz, "tn", "ffn0_up_dw", out_dtypes=(BF16,)))
    dh = run(_matmul, 87.0, dz, w1_0, "nt", "ffn0_up_dx", tk=LONG_K_TILE)
    dx, d_ffn_pre_0, dm0, d_mix_post_0 = run(_norms_bwd, 36.0, dx, dh, x1, gain(ffn_pre_g, 0), m0, gain(mix_post_g, 0), "norms_bwd1")

    reduce_grad(1, run(_matmul, 24.0, y0, dm0, "tn", "mix0_out_dw", out_dtypes=(BF16,)))
    dy0 = run(_matmul, 25.0, dm0, w_out0, "nt", "mix0_out_dx")
    du_pool, d_pool_w, d_pool_scale = run(_pool_bwd, 28.0, dy0, pooled, pool_w_full, pool_scale, "pool_bwd")
    dc, d_ln_g, d_ln_b = run(_ln_silu_bwd, 15.0, dy0, c_conv, conv_ln_g, conv_ln_b, "ln_silu_bwd")
    dv, dgate, d_conv_w, d_conv_b = run(_conv_bwd, 52.0, dc, a_conv, z0, conv_w_full, DP, "conv_bwd")
    dz0 = jnp.concatenate([du_pool, dv, dgate], axis=1)

    small_sums = {}

    def exchange_small(key, arrays, cost):
        slots = _in_own_slot(_pack(arrays, D), place, "small_grads_slot_" + key)
        backlog.start(_small_exchange_job(slots), "small_grads_" + key, "ici", cost,
                      lambda outs, kept: small_sums.__setitem__(key, _unpack(_sum_slots(outs[0], "small_grads_sum_" + key),
                                                                             [a.shape for a in arrays])))

    exchange_small("most", [d_mix_pre_1, jnp.concatenate([d_mix_post_0, d_mix_post_1], 0),
                            jnp.concatenate([d_ffn_pre_0, d_ffn_pre_1], 0), jnp.concatenate([d_ffn_post_0, d_ffn_post_1], 0),
                            d_pool_scale, d_conv_b, d_ln_g, d_ln_b, d_pool_w, d_conv_w, d_sc_conv_w], 112.0)
    reduce_grad(0, run(_matmul, 34.0, h0, dz0, "tn", "mix0_in_dw", out_dtypes=(BF16,)))
    dh = run(_matmul, 40.0, dz0, w_in0, "nt", "mix0_in_dx")
    grad_x, d_mix_pre_0 = run(_norms_bwd, 26.0, dx, dh, x0, gain(mix_pre_g, 0), None, None, "norms_bwd0")
    exchange_small("last", [d_mix_pre_0, loss_row], 5.0)

    upd, gr, first = {}, grads_big, {}

    def keep(where, key, outs):
        where[key] = outs
        return outs

    adamw_big = [
        (7, lambda: keep(first, "ffn_w2", _adamw(ffn_w2, gr[7], m_ffn_w2, v_ffn_w2, "adamw_ffn_w2_1", layer=1)), 46.0),
        (6, lambda: keep(first, "ffn_w1", _adamw(ffn_w1, gr[6], m_ffn_w1, v_ffn_w1, "adamw_ffn_w1_1", layer=1)), 46.0),
        (5, lambda: keep(upd, "sc_w_out", _adamw(sc_w_out, gr[5], m_sc_w_out, v_sc_w_out, "adamw_sc_w_out")), 14.0),
        (4, lambda: keep(upd, "sc_w_in", _adamw(sc_w_in, gr[4], m_sc_w_in, v_sc_w_in, "adamw_sc_w_in")), 35.0),
        (3, lambda: keep(upd, "ffn_w2", _adamw(ffn_w2, gr[3], m_ffn_w2, v_ffn_w2, "adamw_ffn_w2_0", layer=0,
                                               carried=first["ffn_w2"])), 46.0),
        (2, lambda: keep(upd, "ffn_w1", _adamw(ffn_w1, gr[2], m_ffn_w1, v_ffn_w1, "adamw_ffn_w1_0", layer=0,
                                               carried=first["ffn_w1"])), 46.0),
        (1, lambda: keep(upd, "ab_w_out", _adamw(ab_w_out, gr[1], m_ab_w_out, v_ab_w_out, "adamw_ab_w_out")), 14.0),
        (0, lambda: keep(upd, "ab_w_in", _adamw(ab_w_in, gr[0], m_ab_w_in, v_ab_w_in, "adamw_ab_w_in")), 19.0)]
    while adamw_big or backlog.flights:
        due = [a for a in adamw_big if gr[a[0]] is not None]
        if due:
            adamw_big.remove(due[0])
            backlog.run(due[0][1], due[0][2])
        else:
            backlog.poll(block=True)

    (g_mix_pre_1, g_mix_post, g_ffn_pre, g_ffn_post, g_pool_scale, g_conv_b, g_ln_g, g_ln_b, g_pool_w_full, g_conv_w_full,
     g_sc_conv_w_full) = small_sums["most"]
    g_mix_pre = jnp.concatenate([small_sums["last"][0], g_mix_pre_1], 0)
    loss = small_sums["last"][1][0, 0]
    own = lambda a, ax: lax.dynamic_slice_in_dim(a, chip * (a.shape[ax] // N_CHIPS), a.shape[ax] // N_CHIPS, axis=ax)
    g_pool_w, g_conv_w, g_sc_conv_w = own(g_pool_w_full, 1), own(g_conv_w_full, 1), own(g_sc_conv_w_full, 1)

    def small_update(w, g, m, v, name):
        shape = w.shape
        as3 = lambda a: a.reshape((1, -1, shape[-1]))
        outs = _adamw(as3(w), g.reshape((-1, shape[-1])), as3(m), as3(v), "adamw_" + name)
        return [o.reshape(shape) for o in outs]

    upd["mix_pre_g"] = small_update(mix_pre_g, g_mix_pre, m_mix_pre_g, v_mix_pre_g, "mix_pre_g")
    upd["mix_post_g"] = small_update(mix_post_g, g_mix_post, m_mix_post_g, v_mix_post_g, "mix_post_g")
    upd["ffn_pre_g"] = small_update(ffn_pre_g, g_ffn_pre, m_ffn_pre_g, v_ffn_pre_g, "ffn_pre_g")
    upd["ffn_post_g"] = small_update(ffn_post_g, g_ffn_post, m_ffn_post_g, v_ffn_post_g, "ffn_post_g")
    upd["pool_w"] = small_update(pool_w, g_pool_w, m_pool_w, v_pool_w, "pool_w")
    upd["pool_scale"] = small_update(pool_scale, g_pool_scale, m_pool_scale, v_pool_scale, "pool_scale")
    upd["conv_w"] = small_update(conv_w, g_conv_w, m_conv_w, v_conv_w, "conv_w")
    upd["conv_b"] = small_update(conv_b, g_conv_b, m_conv_b, v_conv_b, "conv_b")
    upd["conv_ln_g"] = small_update(conv_ln_g, g_ln_g, m_conv_ln_g, v_conv_ln_g, "conv_ln_g")
    upd["conv_ln_b"] = small_update(conv_ln_b, g_ln_b, m_conv_ln_b, v_conv_ln_b, "conv_ln_b")
    upd["sc_conv_w"] = small_update(sc_conv_w, g_sc_conv_w, m_sc_conv_w, v_sc_conv_w, "sc_conv_w")

    order = ["mix_pre_g", "mix_post_g", "ffn_pre_g", "ffn_post_g", "ab_w_in", "pool_w", "pool_scale", "conv_w", "conv_b",
             "conv_ln_g", "conv_ln_b", "ab_w_out", "sc_w_in", "sc_conv_w", "sc_w_out", "ffn_w1", "ffn_w2"]
    out = [loss, grad_x[None]]
    for part in range(4):
        out += [upd[n][part] for n in order]
    return tuple(out)
```

```python
import jax
import jax.numpy as jnp
from jax import lax
from jax.experimental import pallas as pl
from jax.experimental.pallas import tpu as pltpu

F32, BF16 = jnp.float32, jnp.bfloat16
EPS = 1e-6
N_GROUPS = 4
MAX_WINDOW = 16
CONV_K = 31
SHORT_K = 3
CONV_PAD = 32
SHORT_PAD = 8
ADAM_LR, ADAM_B1, ADAM_B2, ADAM_EPS, ADAM_WD, ADAM_STEP = 0.001, 0.9, 0.999, 1e-08, 0.01, 10
N_CHIPS = 4
VMEM_LIMIT_BYTES = 56 * 1024 * 1024
ROW_TILE = 256
CHUNK = 256
LONG_K_TILE = 4096
MESH = pl.DeviceIdType.MESH
HBM = pl.BlockSpec(memory_space=pltpu.HBM)
SEM = pl.BlockSpec(memory_space=pltpu.SEMAPHORE)
ANY = pl.BlockSpec(memory_space=pl.ANY)


def _cp(*sem):
    return pltpu.CompilerParams(dimension_semantics=sem, vmem_limit_bytes=VMEM_LIMIT_BYTES)


def _sigmoid(v):
    return 1.0 / (1.0 + jnp.exp(-v))


class _Behind:
    pending = []


def _call(body, prefetch=None, **kw):
    behind, _Behind.pending = _Behind.pending, []
    single = not isinstance(kw["out_shape"], (list, tuple))
    in_specs, scratch = list(kw["in_specs"]), list(kw.get("scratch_shapes", ()))
    out_shape = [kw["out_shape"]] if single else list(kw["out_shape"])
    out_specs = [kw["out_specs"]] if single else list(kw["out_specs"])
    n_pre = 0 if prefetch is None else 1
    n_own, n_behind = len(in_specs), len(behind)

    def wrapped(*refs):
        body(*refs[:n_pre + n_own], *refs[n_pre + n_own + n_behind:])

    specs = dict(grid=kw["grid"], in_specs=in_specs + [ANY] * n_behind, out_specs=out_specs)
    if prefetch is None:
        specs["scratch_shapes"] = scratch
    else:
        specs = dict(grid_spec=pltpu.PrefetchScalarGridSpec(num_scalar_prefetch=1, scratch_shapes=scratch, **specs))
    aliases = {n_pre + i: o for i, o in kw.get("input_output_aliases", {}).items()}
    call = pl.pallas_call(wrapped, name=kw["name"], out_shape=out_shape, input_output_aliases=aliases,
                          compiler_params=_cp(*kw["sem"]), **specs)

    def run(*args):
        outs = call(*([prefetch] * n_pre), *args, *behind)
        return outs[0] if single else list(outs)

    return run


_DIMS = {"nn": (((1,), (0,)), ((), ())), "nt": (((1,), (1,)), ((), ())), "tn": (((0,), (0,)), ((), ()))}


def _pick(n, cap, step=256):
    if n <= cap:
        return n
    return next(t for t in range(cap - cap % step, 0, -step) if n % t == 0)


def _matmul(a, b, mode, name, out_dtypes=(F32,), epilogue=None, epi=(), tm=1024, tn=1024, tk=2048):
    if mode == "tn":
        (K, M), (K2, N) = a.shape, b.shape
    elif mode == "nt":
        (M, K), (N, K2) = a.shape, b.shape
    else:
        (M, K), (K2, N) = a.shape, b.shape
    assert K == K2
    tm, tn, tk = _pick(M, tm), _pick(N, tn), _pick(K, tk)
    nk = K // tk
    a_spec = pl.BlockSpec((tk, tm), lambda i, j, k: (k, i)) if mode == "tn" else pl.BlockSpec((tm, tk), lambda i, j, k: (i, k))
    b_spec = pl.BlockSpec((tn, tk), lambda i, j, k: (j, k)) if mode == "nt" else pl.BlockSpec((tk, tn), lambda i, j, k: (k, j))
    o_spec = pl.BlockSpec((tm, tn), lambda i, j, k: (i, j))
    n_epi, n_out = len(epi), len(out_dtypes)

    def body(a_ref, b_ref, *rest):
        epi_refs, out_refs, scratch = rest[:n_epi], rest[n_epi:n_epi + n_out], rest[n_epi + n_out:]
        part = lax.dot_general(a_ref[...].astype(BF16), b_ref[...].astype(BF16), _DIMS[mode], preferred_element_type=F32)

        def finish(acc):
            outs = epilogue(acc, *[r[...] for r in epi_refs]) if epilogue else (acc,)
            for o_ref, o in zip(out_refs, outs):
                o_ref[...] = o.astype(o_ref.dtype)

        if nk == 1:
            finish(part)
        else:
            acc_ref = scratch[0]
            k = pl.program_id(2)

            @pl.when(k == 0)
            def _():
                acc_ref[...] = part

            @pl.when(k > 0)
            def _():
                acc_ref[...] += part

            @pl.when(k == nk - 1)
            def _():
                finish(acc_ref[...])

    outs = _call(
        body, name=name, grid=(M // tm, N // tn, nk),
        in_specs=[a_spec, b_spec] + [o_spec] * n_epi, out_specs=[o_spec] * n_out,
        out_shape=[jax.ShapeDtypeStruct((M, N), dt) for dt in out_dtypes],
        scratch_shapes=[pltpu.VMEM((tm, tn), F32)] if nk > 1 else [],
        sem=("parallel", "parallel", "arbitrary"))(a, b, *epi)
    return outs[0] if n_out == 1 else outs


def _matmul_by_chip_rows(a, b, place, name, rest=None, tm=1024, tn=1024):
    (M, K), N = a.shape, b.shape[1]
    tm, tn, tk = _pick(M, tm), _pick(N, tn), K // N_CHIPS
    nk = N_CHIPS - 1 if rest is None else 1

    def block(k, s):
        diagonal = N_CHIPS - 1 - s[0]
        return k + (k >= diagonal).astype(jnp.int32) if rest is None else diagonal

    def body(s_ref, a_ref, b_ref, *refs):
        o_ref, acc_ref = refs[-2:]
        part = jnp.dot(a_ref[...], b_ref[...], preferred_element_type=F32)
        k = pl.program_id(2)

        @pl.when(k == 0)
        def _():
            acc_ref[...] = part if rest is None else part + refs[0][...]

        @pl.when(k > 0)
        def _():
            acc_ref[...] += part

        @pl.when(k == nk - 1)
        def _():
            o_ref[...] = acc_ref[...]

    tile = pl.BlockSpec((tm, tn), lambda i, j, k, s: (i, j))
    in_specs = [pl.BlockSpec((tm, tk), lambda i, j, k, s: (i, block(k, s))), pl.BlockSpec((tk, tn), lambda i, j, k, s: (block(k, s), j))]
    return _call(body, prefetch=place, name=name, grid=(M // tm, N // tn, nk), in_specs=in_specs + ([] if rest is None else [tile]),
                 out_specs=tile, out_shape=jax.ShapeDtypeStruct((M, N), F32), scratch_shapes=[pltpu.VMEM((tm, tn), F32)],
                 sem=("parallel", "parallel", "arbitrary"))(a, b, *([] if rest is None else [rest]))


def _rms(x, g):
    r = lax.rsqrt(jnp.mean(x * x, axis=-1, keepdims=True) + EPS)
    return x * r * g


def _rms_bwd(dy, x, g):
    r = lax.rsqrt(jnp.mean(x * x, axis=-1, keepdims=True) + EPS)
    xn = x * r
    dyg = dy * g
    dx = r * (dyg - xn * jnp.mean(dyg * xn, axis=-1, keepdims=True))
    return dx, jnp.sum(dy * xn, axis=0, keepdims=True)


def _rows(d, tr=ROW_TILE):
    return pl.BlockSpec((tr, d), lambda i: (i, 0))


def _vec(d):
    return pl.BlockSpec((1, d), lambda i: (0, 0))


def _accumulate(ref, val):
    @pl.when(pl.program_id(0) == 0)
    def _():
        ref[...] = val

    @pl.when(pl.program_id(0) > 0)
    def _():
        ref[...] += val


def _norm_fwd(x, g, name):
    T, D = x.shape

    def body(x_ref, g_ref, h_ref):
        h_ref[...] = _rms(x_ref[...], g_ref[...]).astype(BF16)

    return _call(body,name=name,grid=(T // ROW_TILE,), in_specs=[_rows(D), _vec(D)], out_specs=_rows(D),
                          out_shape=jax.ShapeDtypeStruct((T, D), BF16), sem=("parallel",))(x, g)


def _residual_norm(x, m, g_post, g_next, name):
    T, D = x.shape

    def body(x_ref, m_ref, gp_ref, gn_ref, xo_ref, h_ref):
        xo = x_ref[...] + _rms(m_ref[...], gp_ref[...])
        xo_ref[...] = xo
        h_ref[...] = _rms(xo, gn_ref[...]).astype(BF16)

    return _call(body,name=name,grid=(T // ROW_TILE,), in_specs=[_rows(D), _rows(D), _vec(D), _vec(D)],
                          out_specs=[_rows(D), _rows(D)],
                          out_shape=[jax.ShapeDtypeStruct((T, D), F32), jax.ShapeDtypeStruct((T, D), BF16)],
                          sem=("parallel",))(x, m, g_post, g_next)


def _loss_and_last_norm_bwd(x, m, g_post, target, name):
    T, D = x.shape

    def body(x_ref, m_ref, gp_ref, t_ref, dx_ref, dm_ref, dg_ref, loss_ref):
        m_val, gp = m_ref[...], gp_ref[...]
        err = x_ref[...] + _rms(m_val, gp) - t_ref[...]
        dx = err * (1.0 / D)
        dx_ref[...] = dx
        dm, dg = _rms_bwd(dx, m_val, gp)
        dm_ref[...] = dm.astype(BF16)
        _accumulate(dg_ref, dg)
        _accumulate(loss_ref, jnp.full((1, 128), 0.5 * jnp.sum(err * err) * (1.0 / D), F32))

    return _call(body,name=name,grid=(T // ROW_TILE,), in_specs=[_rows(D), _rows(D), _vec(D), _rows(D)],
                          out_specs=[_rows(D), _rows(D), _vec(D), _vec(128)],
                          out_shape=[jax.ShapeDtypeStruct((T, D), F32), jax.ShapeDtypeStruct((T, D), BF16),
                                     jax.ShapeDtypeStruct((1, D), F32), jax.ShapeDtypeStruct((1, 128), F32)],
                          sem=("arbitrary",))(x, m, g_post, target)


def _norms_bwd(dx, dh, x_in, g_pre, m_prev, g_post_prev, name):
    T, D = dx.shape
    with_prev = m_prev is not None

    def body(*refs):
        if with_prev:
            dx_ref, dh_ref, x_ref, gq_ref, m_ref, gp_ref, dxo_ref, dgq_ref, dm_ref, dgp_ref = refs
        else:
            dx_ref, dh_ref, x_ref, gq_ref, dxo_ref, dgq_ref = refs
        d_in, dgq = _rms_bwd(dh_ref[...], x_ref[...], gq_ref[...])
        dxo = dx_ref[...] + d_in
        dxo_ref[...] = dxo
        _accumulate(dgq_ref, dgq)
        if with_prev:
            dm, dgp = _rms_bwd(dxo, m_ref[...], gp_ref[...])
            dm_ref[...] = dm.astype(BF16)
            _accumulate(dgp_ref, dgp)

    ins, in_specs = [dx, dh, x_in, g_pre], [_rows(D), _rows(D), _rows(D), _vec(D)]
    out_specs = [_rows(D), _vec(D)]
    out_shape = [jax.ShapeDtypeStruct((T, D), F32), jax.ShapeDtypeStruct((1, D), F32)]
    if with_prev:
        ins += [m_prev, g_post_prev]
        in_specs += [_rows(D), _vec(D)]
        out_specs += [_rows(D), _vec(D)]
        out_shape += [jax.ShapeDtypeStruct((T, D), BF16), jax.ShapeDtypeStruct((1, D), F32)]
    return _call(body,name=name,grid=(T // ROW_TILE,), in_specs=in_specs, out_specs=out_specs, out_shape=out_shape,
                          sem=("arbitrary",))(*ins)


def _window_weights(g):
    w = 2 << g
    return w, [jnp.where(j < w, 1.0, 0.0).astype(F32) for j in range(MAX_WINDOW)]


def _valid_count(r0, rows, w):
    t = (lax.broadcasted_iota(jnp.int32, (rows, 1), 0) + (r0 + 1)).astype(F32)
    return jnp.minimum(t, w.astype(F32))


def _pool_fwd(z, pool_w, pool_scale, name):
    T = z.shape[0]
    PG = pool_w.shape[-1]
    DP = N_GROUPS * PG
    rc = min(CHUNK, T)

    def body(u_ref, pw_ref, sc_ref, pooled_ref, y_ref, pad):
        w, wts = _window_weights(pl.program_id(0))
        pad[pl.ds(0, MAX_WINDOW), :] = jnp.zeros((MAX_WINDOW, PG), F32)
        pad[pl.ds(MAX_WINDOW, T), :] = u_ref[...]
        for r0 in range(0, T, rc):
            acc = jnp.zeros((rc, PG), F32)
            for j in range(MAX_WINDOW):
                acc = acc + wts[j] * pad[pl.ds(MAX_WINDOW + r0 - j, rc), :]
            pooled = acc / _valid_count(r0, rc, w) - u_ref[pl.ds(r0, rc), :]
            pooled_ref[pl.ds(r0, rc), :] = pooled.astype(BF16)
        mixed = jnp.dot(pooled_ref[...], pw_ref[...].astype(BF16), preferred_element_type=F32)
        y_ref[...] = (mixed * sc_ref[...]).astype(BF16)

    col = lambda g: (0, g)
    return _call(
        body, name=name,grid=(N_GROUPS,),
        in_specs=[pl.BlockSpec((T, PG), col), pl.BlockSpec((None, PG, PG), lambda g: (g, 0, 0)), pl.BlockSpec((1, PG), col)],
        out_specs=[pl.BlockSpec((T, PG), col), pl.BlockSpec((T, PG), col)],
        out_shape=[jax.ShapeDtypeStruct((T, DP), BF16), jax.ShapeDtypeStruct((T, DP), BF16)],
        scratch_shapes=[pltpu.VMEM((T + MAX_WINDOW, PG), F32)], sem=("parallel",))(z, pool_w, pool_scale)


def _pool_bwd(dy, pooled, pool_w, pool_scale, name):
    T = dy.shape[0]
    PG = pool_w.shape[-1]
    DP = N_GROUPS * PG
    rc = min(CHUNK, T)

    def body(dy_ref, pooled_ref, pw_ref, sc_ref, du_ref, dpw_ref, dsc_ref, pad, dp_ref):
        w, wts = _window_weights(pl.program_id(0))
        pooled_v, pw = pooled_ref[...], pw_ref[...].astype(BF16)
        dy_v = dy_ref[...]
        mixed = jnp.dot(pooled_v, pw, preferred_element_type=F32)
        dsc_ref[...] = jnp.sum(dy_v * mixed, axis=0, keepdims=True)
        dmixed = (dy_v * sc_ref[...]).astype(BF16)
        dpw_ref[...] = lax.dot_general(pooled_v, dmixed, _DIMS["tn"], preferred_element_type=F32)
        dp_ref[...] = lax.dot_general(dmixed, pw, _DIMS["nt"], preferred_element_type=F32)
        pad[pl.ds(T, MAX_WINDOW), :] = jnp.zeros((MAX_WINDOW, PG), F32)
        for r0 in range(0, T, rc):
            pad[pl.ds(r0, rc), :] = dp_ref[pl.ds(r0, rc), :] / _valid_count(r0, rc, w)
        for r0 in range(0, T, rc):
            acc = jnp.zeros((rc, PG), F32)
            for j in range(MAX_WINDOW):
                acc = acc + wts[j] * pad[pl.ds(r0 + j, rc), :]
            du_ref[pl.ds(r0, rc), :] = (acc - dp_ref[pl.ds(r0, rc), :]).astype(BF16)

    col = lambda g: (0, g)
    return _call(
        body, name=name,grid=(N_GROUPS,),
        in_specs=[pl.BlockSpec((T, PG), col), pl.BlockSpec((T, PG), col), pl.BlockSpec((None, PG, PG), lambda g: (g, 0, 0)),
                  pl.BlockSpec((1, PG), col)],
        out_specs=[pl.BlockSpec((T, PG), col), pl.BlockSpec((None, PG, PG), lambda g: (g, 0, 0)), pl.BlockSpec((1, PG), col)],
        out_shape=[jax.ShapeDtypeStruct((T, DP), BF16), jax.ShapeDtypeStruct((N_GROUPS, PG, PG), F32),
                   jax.ShapeDtypeStruct((1, DP), F32)],
        scratch_shapes=[pltpu.VMEM((T + MAX_WINDOW, PG), F32), pltpu.VMEM((T, PG), F32)],
        sem=("parallel",))(dy, pooled, pool_w, pool_scale)


def _conv_fwd(z, conv_w, conv_b, d_pool, name, tc=128):
    T = z.shape[0]
    DC = conv_w.shape[-1]
    rc = min(CHUNK, T)
    v0, g0 = d_pool // tc, (d_pool + DC) // tc

    def body(v_ref, gt_ref, w_ref, b_ref, a_ref, c_ref, pad):
        pad[pl.ds(0, CONV_PAD), :] = jnp.zeros((CONV_PAD, tc), F32)
        for r0 in range(0, T, rc):
            a = v_ref[pl.ds(r0, rc), :] * _sigmoid(gt_ref[pl.ds(r0, rc), :])
            a_ref[pl.ds(r0, rc), :] = a
            pad[pl.ds(CONV_PAD + r0, rc), :] = a
        for r0 in range(0, T, rc):
            acc = jnp.zeros((rc, tc), F32) + b_ref[...]
            for k in range(CONV_K):
                acc = acc + w_ref[pl.ds(k, 1), :] * pad[pl.ds(CONV_PAD - (CONV_K - 1) + k + r0, rc), :]
            c_ref[pl.ds(r0, rc), :] = acc

    col = lambda j: (0, j)
    return _call(
        body, name=name,grid=(DC // tc,),
        in_specs=[pl.BlockSpec((T, tc), lambda j: (0, v0 + j)), pl.BlockSpec((T, tc), lambda j: (0, g0 + j)),
                  pl.BlockSpec((CONV_K, tc), col), pl.BlockSpec((1, tc), col)],
        out_specs=[pl.BlockSpec((T, tc), col), pl.BlockSpec((T, tc), col)],
        out_shape=[jax.ShapeDtypeStruct((T, DC), F32), jax.ShapeDtypeStruct((T, DC), F32)],
        scratch_shapes=[pltpu.VMEM((T + CONV_PAD, tc), F32)], sem=("parallel",))(z, z, conv_w, conv_b)


def _conv_bwd(dc, a, z, conv_w, d_pool, name, tc=128):
    T, DC = dc.shape
    rc = min(CHUNK, T)
    v0, g0 = d_pool // tc, (d_pool + DC) // tc

    def body(dc_ref, a_ref, v_ref, gt_ref, w_ref, dv_ref, dg_ref, dw_ref, db_ref, apad, dpad):
        apad[pl.ds(0, CONV_PAD), :] = jnp.zeros((CONV_PAD, tc), F32)
        apad[pl.ds(CONV_PAD, T), :] = a_ref[...]
        dpad[pl.ds(0, T), :] = dc_ref[...]
        dpad[pl.ds(T, CONV_PAD), :] = jnp.zeros((CONV_PAD, tc), F32)
        db_ref[...] = jnp.sum(dc_ref[...], axis=0, keepdims=True)
        for k in range(CONV_K):
            acc = jnp.zeros((8, tc), F32)
            for r0 in range(0, T, rc):
                prod = dc_ref[pl.ds(r0, rc), :] * apad[pl.ds(CONV_PAD - (CONV_K - 1) + k + r0, rc), :]
                acc = acc + jnp.sum(prod.reshape(rc // 8, 8, tc), axis=0)
            dw_ref[pl.ds(k, 1), :] = jnp.sum(acc, axis=0, keepdims=True)
        for r0 in range(0, T, rc):
            da = jnp.zeros((rc, tc), F32)
            for k in range(CONV_K):
                da = da + w_ref[pl.ds(k, 1), :] * dpad[pl.ds(r0 + (CONV_K - 1) - k, rc), :]
            sig = _sigmoid(gt_ref[pl.ds(r0, rc), :])
            dv_ref[pl.ds(r0, rc), :] = (da * sig).astype(BF16)
            dg_ref[pl.ds(r0, rc), :] = (da * v_ref[pl.ds(r0, rc), :] * sig * (1.0 - sig)).astype(BF16)

    col = lambda j: (0, j)
    return _call(
        body, name=name,grid=(DC // tc,),
        in_specs=[pl.BlockSpec((T, tc), col), pl.BlockSpec((T, tc), col), pl.BlockSpec((T, tc), lambda j: (0, v0 + j)),
                  pl.BlockSpec((T, tc), lambda j: (0, g0 + j)), pl.BlockSpec((CONV_K, tc), col)],
        out_specs=[pl.BlockSpec((T, tc), col), pl.BlockSpec((T, tc), col), pl.BlockSpec((CONV_K, tc), col),
                   pl.BlockSpec((1, tc), col)],
        out_shape=[jax.ShapeDtypeStruct((T, DC), BF16), jax.ShapeDtypeStruct((T, DC), BF16),
                   jax.ShapeDtypeStruct((CONV_K, DC), F32), jax.ShapeDtypeStruct((1, DC), F32)],
        scratch_shapes=[pltpu.VMEM((T + CONV_PAD, tc), F32), pltpu.VMEM((T + CONV_PAD, tc), F32)],
        sem=("parallel",))(dc, a, z, z, conv_w)


def _layer_norm_parts(c, g, b):
    mu = jnp.mean(c, axis=-1, keepdims=True)
    xc = c - mu
    rstd = lax.rsqrt(jnp.mean(xc * xc, axis=-1, keepdims=True) + EPS)
    xhat = xc * rstd
    return xhat, rstd, xhat * g + b


def _ln_silu_fwd(c, g, b, name):
    T, DC = c.shape

    def body(c_ref, g_ref, b_ref, y_ref):
        _, _, ln = _layer_norm_parts(c_ref[...], g_ref[...], b_ref[...])
        y_ref[...] = (ln * _sigmoid(ln)).astype(BF16)

    return _call(body,name=name,grid=(T // ROW_TILE,), in_specs=[_rows(DC), _vec(DC), _vec(DC)], out_specs=_rows(DC),
                          out_shape=jax.ShapeDtypeStruct((T, DC), BF16), sem=("parallel",))(c, g, b)


def _ln_silu_bwd(dy, c, g, b, name):
    T, DC = c.shape

    def body(dy_ref, c_ref, g_ref, b_ref, dc_ref, dg_ref, db_ref):
        gain = g_ref[...]
        xhat, rstd, ln = _layer_norm_parts(c_ref[...], gain, b_ref[...])
        s = _sigmoid(ln)
        dln = dy_ref[...] * (s * (1.0 + ln * (1.0 - s)))
        _accumulate(dg_ref, jnp.sum(dln * xhat, axis=0, keepdims=True))
        _accumulate(db_ref, jnp.sum(dln, axis=0, keepdims=True))
        dxh = dln * gain
        dc_ref[...] = rstd * (dxh - jnp.mean(dxh, axis=-1, keepdims=True) - xhat * jnp.mean(dxh * xhat, axis=-1, keepdims=True))

    return _call(body,name=name,grid=(T // ROW_TILE,),
                          in_specs=[pl.BlockSpec((ROW_TILE, DC), lambda i: (i, 1)), _rows(DC), _vec(DC), _vec(DC)],
                          out_specs=[_rows(DC), _vec(DC), _vec(DC)],
                          out_shape=[jax.ShapeDtypeStruct((T, DC), F32), jax.ShapeDtypeStruct((1, DC), F32),
                                     jax.ShapeDtypeStruct((1, DC), F32)],
                          sem=("arbitrary",))(dy, c, g, b)


def _short_specs(T, DS, tc):
    n = DS // tc
    return [pl.BlockSpec((T, tc), lambda j: (0, j)), pl.BlockSpec((T, tc), lambda j: (0, n + j)),
            pl.BlockSpec((T, tc), lambda j: (0, 2 * n + j))]


def _short_fwd(z, w, name, tc=256):
    T = z.shape[0]
    DS = w.shape[-1]
    rc = min(CHUNK, T)

    def body(b_ref, cg_ref, u_ref, w_ref, y_ref, pad):
        pad[pl.ds(0, SHORT_PAD), :] = jnp.zeros((SHORT_PAD, tc), F32)
        pad[pl.ds(SHORT_PAD, T), :] = cg_ref[...] * u_ref[...]
        for r0 in range(0, T, rc):
            r = jnp.zeros((rc, tc), F32)
            for k in range(SHORT_K):
                r = r + w_ref[pl.ds(k, 1), :] * pad[pl.ds(SHORT_PAD - (SHORT_K - 1) + k + r0, rc), :]
            y_ref[pl.ds(r0, rc), :] = (b_ref[pl.ds(r0, rc), :] * r).astype(BF16)

    col = lambda j: (0, j)
    return _call(body,name=name,grid=(DS // tc,), in_specs=_short_specs(T, DS, tc) + [pl.BlockSpec((SHORT_K, tc), col)],
                          out_specs=pl.BlockSpec((T, tc), col), out_shape=jax.ShapeDtypeStruct((T, DS), BF16),
                          scratch_shapes=[pltpu.VMEM((T + SHORT_PAD, tc), F32)], sem=("parallel",))(z, z, z, w)


def _short_bwd(dy, z, w, name, tc=256):
    T, DS = dy.shape
    rc = min(CHUNK, T)

    def body(dy_ref, b_ref, cg_ref, u_ref, w_ref, db_ref, dcg_ref, du_ref, dw_ref, qpad, rpad):
        qpad[pl.ds(0, SHORT_PAD), :] = jnp.zeros((SHORT_PAD, tc), F32)
        qpad[pl.ds(SHORT_PAD, T), :] = cg_ref[...] * u_ref[...]
        rpad[pl.ds(0, T), :] = dy_ref[...] * b_ref[...]
        rpad[pl.ds(T, SHORT_PAD), :] = jnp.zeros((SHORT_PAD, tc), F32)
        accs = [jnp.zeros((8, tc), F32) for _ in range(SHORT_K)]
        for r0 in range(0, T, rc):
            r = jnp.zeros((rc, tc), F32)
            dq = jnp.zeros((rc, tc), F32)
            dr = rpad[pl.ds(r0, rc), :]
            for k in range(SHORT_K):
                q_k = qpad[pl.ds(SHORT_PAD - (SHORT_K - 1) + k + r0, rc), :]
                r = r + w_ref[pl.ds(k, 1), :] * q_k
                dq = dq + w_ref[pl.ds(k, 1), :] * rpad[pl.ds(r0 + (SHORT_K - 1) - k, rc), :]
                accs[k] = accs[k] + jnp.sum((dr * q_k).reshape(rc // 8, 8, tc), axis=0)
            db_ref[pl.ds(r0, rc), :] = (dy_ref[pl.ds(r0, rc), :] * r).astype(BF16)
            dcg_ref[pl.ds(r0, rc), :] = (dq * u_ref[pl.ds(r0, rc), :]).astype(BF16)
            du_ref[pl.ds(r0, rc), :] = (dq * cg_ref[pl.ds(r0, rc), :]).astype(BF16)
        for k in range(SHORT_K):
            dw_ref[pl.ds(k, 1), :] = jnp.sum(accs[k], axis=0, keepdims=True)

    col = lambda j: (0, j)
    tile = pl.BlockSpec((T, tc), col)
    return _call(body,name=name,grid=(DS // tc,),
                          in_specs=[tile] + _short_specs(T, DS, tc) + [pl.BlockSpec((SHORT_K, tc), col)],
                          out_specs=[tile, tile, tile, pl.BlockSpec((SHORT_K, tc), col)],
                          out_shape=[jax.ShapeDtypeStruct((T, DS), BF16)] * 3 + [jax.ShapeDtypeStruct((SHORT_K, DS), F32)],
                          scratch_shapes=[pltpu.VMEM((T + SHORT_PAD, tc), F32), pltpu.VMEM((T + SHORT_PAD, tc), F32)],
                          sem=("parallel",))(dy, z, z, z, w)


def _tile_rows(rows, cols, n_bufs):
    budget = VMEM_LIMIT_BYTES * 3 // 4 // (2 * n_bufs * 4 * cols)
    tr = rows
    while tr > budget and tr % 16 == 0:
        tr //= 2
    return tr


def _placed_call(body, name, place, grid, in_specs, out_specs, out_shape, ins):
    return _call(body,prefetch=place, name=name, grid=grid, in_specs=in_specs, out_specs=out_specs, out_shape=out_shape,
                 sem=("parallel",))(*ins)


def _cast_into_full(w, layer, kind, place, name):
    _, R, C = w.shape
    tr = _tile_rows(R, C, 2)
    nb = R // tr
    if kind == "col":
        full, out_spec = (R, C * N_CHIPS), pl.BlockSpec((tr, C), lambda i, s: (i, s[0]))
    else:
        full, out_spec = (R * N_CHIPS, C), pl.BlockSpec((tr, C), lambda i, s: (s[0] * nb + i, 0))

    def body(s_ref, w_ref, o_ref):
        o_ref[...] = w_ref[...].astype(BF16)

    return _placed_call(body, name, place, (nb,), [pl.BlockSpec((None, tr, C), lambda i, s: (layer, i, 0))], out_spec,
                        jax.ShapeDtypeStruct(full, BF16), [w])


def _add_pair(grad, theirs, kind, place, name):
    R, C = grad.shape
    piece_rows = R // 2 if kind == "col" else R // N_CHIPS // 2
    tr = _tile_rows(piece_rows, C, 3)
    nb = piece_rows // tr
    if kind == "col":
        g_spec = pl.BlockSpec((tr, C), lambda i, s: (s[1] * nb + i, 0))
    else:
        g_spec = pl.BlockSpec((tr, C), lambda i, s: ((2 * (i // nb) + s[1]) * nb + i % nb, 0))
    flat = pl.BlockSpec((tr, C), lambda i, s: (i, 0))

    def body(s_ref, a_ref, b_ref, o_ref):
        o_ref[...] = (a_ref[...].astype(F32) + b_ref[...].astype(F32)).astype(BF16)

    return _placed_call(body, name, place, (R // 2 // tr,), [g_spec, flat], flat, jax.ShapeDtypeStruct((R // 2, C), BF16),
                        [grad, theirs])


def _sum_chips(chip_sum, arrived, kind, place, name):
    _, H, W = arrived.shape
    tr = _tile_rows(H, W, 6)
    nb = H // tr
    if kind == "col":
        own_spec = pl.BlockSpec((tr, W), lambda i, s: (i, s[0]))
    else:
        own_spec = pl.BlockSpec((tr, W), lambda i, s: (s[0] * nb + i, 0))

    def body(s_ref, p_ref, r_ref, o_ref):
        acc = p_ref[...].astype(F32)
        for i in range(N_CHIPS - 1):
            acc = acc + r_ref[i].astype(F32)
        o_ref[...] = acc

    return _placed_call(body, name, place, (nb,), [own_spec, pl.BlockSpec((N_CHIPS - 1, tr, W), lambda i, s: (0, i, 0))],
                        pl.BlockSpec((tr, W), lambda i, s: (s[1] * nb + i, 0)), jax.ShapeDtypeStruct((2 * H, W), F32),
                        [chip_sum, arrived])


def _adamw_values(w, g, m, v):
    m = ADAM_B1 * m + (1.0 - ADAM_B1) * g
    v = ADAM_B2 * v + (1.0 - ADAM_B2) * (g * g)
    m_hat = m / (1.0 - ADAM_B1 ** ADAM_STEP)
    v_hat = v / (1.0 - ADAM_B2 ** ADAM_STEP)
    return -ADAM_LR * (m_hat / (jnp.sqrt(v_hat) + ADAM_EPS) + ADAM_WD * w), m, v


def _adamw(w, g, m, v, name, layer=0, carried=None):
    L, R, C = w.shape
    tr = _tile_rows(R, C, 8)

    def body(w_ref, g_ref, m_ref, v_ref, *rest):
        go_ref, d_ref, mo_ref, vo_ref = rest[-4:]
        g_val = g_ref[...]
        d, m_new, v_new = _adamw_values(w_ref[...], g_val, m_ref[...], v_ref[...])
        go_ref[...], d_ref[...], mo_ref[...], vo_ref[...] = g_val, d, m_new, v_new

    lay = pl.BlockSpec((None, tr, C), lambda i: (layer, i, 0))
    ins = [w, g, m, v]
    in_specs = [lay, pl.BlockSpec((tr, C), lambda i: (i, 0)), lay, lay]
    aliases = {}
    if carried is not None:
        ins += list(carried)
        in_specs += [pl.BlockSpec(memory_space=pl.ANY)] * 4
        aliases = {4 + i: i for i in range(4)}
    return _call(body, name=name, grid=(R // tr,), in_specs=in_specs, out_specs=[lay] * 4,
                 out_shape=[jax.ShapeDtypeStruct((L, R, C), F32)] * 4, input_output_aliases=aliases, sem=("parallel",))(*ins)


def _aligned(v, m):
    return v if isinstance(v, int) else pl.multiple_of(v, m)


def _place():
    x, y, c = lax.axis_index("x"), lax.axis_index("y"), lax.axis_index("c")
    other_chips = [(x, 1 - y), (1 - x, y), (1 - x, 1 - y)]
    return x, y, c, 2 * x + y, other_chips


def _chip_index(chip):
    return 2 * chip[0] + chip[1]


def _piece(ref, kind, k, h):
    R, C = ref.shape
    if kind == "col":
        return ref.at[pl.ds(_aligned(h * (R // 2), 16), R // 2), pl.ds(_aligned(k * (C // N_CHIPS), 128), C // N_CHIPS)]
    rs = R // N_CHIPS
    return ref.at[pl.ds(_aligned(k * rs + h * (rs // 2), 16), rs // 2), :]


def _compact_piece(ref, kind, k):
    R2, C = ref.shape
    if kind == "col":
        return ref.at[:, pl.ds(_aligned(k * (C // N_CHIPS), 128), C // N_CHIPS)]
    return ref.at[pl.ds(_aligned(k * (R2 // N_CHIPS), 16), R2 // N_CHIPS), :]


def _half_rows(ref, h):
    R = ref.shape[0]
    return ref.at[pl.ds(_aligned(h * (R // 2), 16), R // 2), :]


class _Copies:
    def __init__(self, send_sems, recv_sems):
        self.send_sems, self.recv_sems = send_sems, recv_sems
        self.n_remote = 0

    def remote(self, src, dst, device):
        k = self.n_remote
        self.n_remote += 1
        return pltpu.make_async_remote_copy(src_ref=src, dst_ref=dst, send_sem=self.send_sems.at[k], recv_sem=self.recv_sems.at[k],
                                            device_id=device, device_id_type=MESH)


class _Job:
    def __init__(self, ins, out_shape, aliases, n_remote, build):
        self.ins, self.out_shape, self.aliases, self.n_remote, self.build = list(ins), list(out_shape), dict(aliases), n_remote, build


class _Flying:
    def __init__(self, job, send_sems, recv_sems, bufs, token):
        self.job, self.send_sems, self.recv_sems, self.bufs, self.token = job, send_sems, recv_sems, bufs, token


def _job_refs(job, buf_refs):
    n_out = len(job.out_shape)
    kept = [i for i in range(len(job.ins)) if i not in job.aliases]
    ins = [buf_refs[job.aliases[i]] if i in job.aliases else buf_refs[n_out + kept.index(i)] for i in range(len(job.ins))]
    return ins, list(buf_refs[:n_out])


def _start_job(job, name, after=()):
    n_in, n_out, n_after = len(job.ins), len(job.out_shape), len(after)
    kept = [i for i in range(n_in) if i not in job.aliases]
    n_bufs = n_out + len(kept)

    def body(*refs):
        in_refs, out_refs = refs[:n_in], refs[n_in + n_after:n_in + n_after + n_out]
        send_sems, recv_sems, token = refs[n_in + n_after + n_bufs:]
        for d in job.build(in_refs, out_refs, _Copies(send_sems, recv_sems)):
            d.start()
        token[...] = jnp.zeros_like(token)

    aliases = dict(job.aliases)
    aliases.update({i: n_out + k for k, i in enumerate(kept)})
    sems = pltpu.SemaphoreType.DMA((job.n_remote,))
    outs = pl.pallas_call(
        body, name=name, in_specs=[HBM] * n_in + [ANY] * n_after,
        out_specs=[HBM] * n_bufs + [SEM, SEM, pl.BlockSpec(memory_space=pltpu.VMEM)],
        out_shape=job.out_shape + [jax.ShapeDtypeStruct(job.ins[i].shape, job.ins[i].dtype) for i in kept]
        + [sems, sems, jax.ShapeDtypeStruct((8, 128), F32)],
        input_output_aliases=aliases,
        compiler_params=pltpu.CompilerParams(has_side_effects=pltpu.SideEffectType.DATAFLOW_SIDE_EFFECTING))(*job.ins, *after)
    return _Flying(job, outs[n_bufs], outs[n_bufs + 1], list(outs[:n_bufs]), outs[n_bufs + 2])


def _wait_job(flying, name, after=()):
    job, n_bufs, n_after = flying.job, len(flying.bufs), len(after)

    def body(*refs):
        in_refs, out_refs = _job_refs(job, refs[:n_bufs])
        send_sems, recv_sems = refs[n_bufs:n_bufs + 2]
        copies = job.build(in_refs, out_refs, _Copies(send_sems, recv_sems))
        for d in copies:
            d.wait_send()
        for d in copies:
            d.wait_recv()

    outs = pl.pallas_call(
        body, name=name, in_specs=[HBM] * n_bufs + [SEM, SEM] + [ANY] * n_after, out_specs=[HBM] * n_bufs,
        out_shape=[jax.ShapeDtypeStruct(b.shape, b.dtype) for b in flying.bufs],
        input_output_aliases={i: i for i in range(n_bufs)},
        compiler_params=pltpu.CompilerParams(has_side_effects=pltpu.SideEffectType.DATAFLOW_SIDE_EFFECTING))(
            *flying.bufs, flying.send_sems, flying.recv_sems, *after)
    return list(outs[:len(job.out_shape)]), list(outs[len(job.out_shape):])


def _in_place(arrays):
    return [jax.ShapeDtypeStruct(a.shape, a.dtype) for a in arrays], {u: u for u in range(len(arrays))}


def _rows_part(ref, part, n_parts):
    h = ref.shape[0] // n_parts
    return ref.at[pl.ds(part * h, h), :]


def _gather_job(full, kind, stage):
    def build(in_refs, out_refs, cp):
        x, y, c, me, (y_nbr, x_nbr, diagonal) = _place()
        (ref,) = out_refs
        sibling = (x, y, 1 - c)
        if stage == 1:
            mine = _piece(ref, kind, me, c)
            return [cp.remote(mine, mine, (*y_nbr, c)), cp.remote(mine, mine, (*x_nbr, c))]
        from_y, from_x = _piece(ref, kind, _chip_index(y_nbr), c), _piece(ref, kind, _chip_index(x_nbr), c)
        copies = []
        if stage in (2, "relay"):
            relay_0, relay_1 = _rows_part(from_x, 0, 2), _rows_part(from_y, 1, 2)
            copies += [cp.remote(relay_0, relay_0, (*y_nbr, c)), cp.remote(relay_1, relay_1, (*x_nbr, c))]
        if stage in (2, "direct"):
            copies += [cp.remote(from_y, from_y, sibling), cp.remote(from_x, from_x, sibling)]
        if stage == 3:
            from_diagonal = _piece(ref, kind, _chip_index(diagonal), c)
            copies.append(cp.remote(from_diagonal, from_diagonal, sibling))
        return copies

    return _Job([full], *_in_place([full]), {1: 2, 2: 4, "relay": 2, "direct": 2, 3: 1}[stage], build)


def _gather_small_job(fulls, axes):
    def build(in_refs, out_refs, cp):
        x, y, c, me, chips = _place()
        copies = []
        for ref, ax in zip(out_refs, axes):
            n = ref.shape[ax] // N_CHIPS
            idx = [slice(None)] * len(ref.shape)
            idx[ax] = pl.ds(_aligned(me * n, n), n)
            mine = ref.at[tuple(idx)]
            copies += [cp.remote(mine, mine, (*chip, c)) for chip in chips]
        return copies

    return _Job(fulls, *_in_place(fulls), 3 * len(fulls), build)


def _exchange_halves_job(grads, kinds):
    def build(in_refs, out_refs, cp):
        x, y, c, me, chips = _place()
        copies = []
        for src, dst, kind in zip(in_refs, out_refs, kinds):
            if kind == "col":
                copies.append(cp.remote(_half_rows(src, 1 - c), dst, (x, y, 1 - c)))
            else:
                copies += [cp.remote(_piece(src, "row", k, 1 - c), _compact_piece(dst, "row", k), (x, y, 1 - c))
                           for k in range(N_CHIPS)]
        return copies

    out_shape = [jax.ShapeDtypeStruct((g.shape[0] // 2, g.shape[1]), g.dtype) for g in grads]
    return _Job(grads, out_shape, {}, sum(1 if k == "col" else N_CHIPS for k in kinds), build)


def _scatter_job(half, kind):
    def build(in_refs, out_refs, cp):
        x, y, c, me, chips = _place()
        (src,), (dst,) = in_refs, out_refs
        return [cp.remote(_compact_piece(src, kind, _chip_index(chip)), dst.at[r], (*chip, c)) for r, chip in enumerate(chips)]

    part_shape = (half.shape[0], half.shape[1] // N_CHIPS) if kind == "col" else (half.shape[0] // N_CHIPS, half.shape[1])
    return _Job([half], [jax.ShapeDtypeStruct((N_CHIPS - 1,) + part_shape, half.dtype)], {}, N_CHIPS - 1, build)


def _share_job(shards):
    def build(in_refs, out_refs, cp):
        x, y, c, me, chips = _place()
        copies = []
        for ref in out_refs:
            mine = _half_rows(ref, c)
            copies.append(cp.remote(mine, mine, (x, y, 1 - c)))
        return copies

    return _Job(shards, *_in_place(shards), len(shards), build)


N_DEVICES = 2 * N_CHIPS


def _small_exchange_job(slots):
    def build(in_refs, out_refs, cp):
        x, y, c, me, chips = _place()
        (ref,) = out_refs
        mine = ref.at[2 * me + c]
        return [cp.remote(mine, mine, (x ^ (p >> 2), y ^ ((p >> 1) & 1), c ^ (p & 1))) for p in range(1, N_DEVICES)]

    return _Job([slots], *_in_place([slots]), N_DEVICES - 1, build)


def _in_own_slot(packed, place, name):
    R, C = packed.shape

    def body(s_ref, p_ref, o_ref):
        o_ref[...] = p_ref[...]

    return _placed_call(body, name, place, (1,), [pl.BlockSpec((R, C), lambda i, s: (0, 0))],
                        pl.BlockSpec((None, R, C), lambda i, s: (2 * s[0] + s[1], 0, 0)),
                        jax.ShapeDtypeStruct((N_DEVICES, R, C), F32), [packed])


def _sum_slots(slots, name):
    n, R, C = slots.shape

    def body(s_ref, o_ref):
        acc = s_ref[0]
        for i in range(1, n):
            acc = acc + s_ref[i]
        o_ref[...] = acc

    return _call(body,name=name, grid=(1,), in_specs=[pl.BlockSpec((n, R, C), lambda i: (0, 0, 0))],
                 out_specs=pl.BlockSpec((R, C), lambda i: (0, 0)), out_shape=jax.ShapeDtypeStruct((R, C), F32),
                 sem=("arbitrary",))(slots)


def _packed_rows(size, width):
    return -(-size // (8 * width)) * 8


def _pack(arrays, width):
    rows = []
    for a in arrays:
        flat = a.reshape(-1)
        n_rows = _packed_rows(flat.shape[0], width)
        rows.append(jnp.pad(flat, (0, n_rows * width - flat.shape[0])).reshape(n_rows, width))
    return jnp.concatenate(rows, axis=0)


def _unpack(packed, shapes):
    out, r0, width = [], 0, packed.shape[1]
    for shape in shapes:
        size = 1
        for d in shape:
            size *= d
        out.append(packed[r0:r0 + _packed_rows(size, width)].reshape(-1)[:size].reshape(shape))
        r0 += _packed_rows(size, width)
    return out


class _Backlog:
    def __init__(self, first):
        self.now, self.free, self.flights, self.last, self.chain = 0.0, {"ici": 0.0, "d2d": 0.0}, [], first, []

    def run(self, fn, us, *args, **kw):
        out = fn(*args, **kw)
        self.now += us
        self.last = out[0] if isinstance(out, (list, tuple)) else out
        self.poll()
        return out

    def start(self, job, name, link, cost, done):
        flying = _start_job(job, name + "_start", self.chain)
        self.chain = [flying.token]
        _Behind.pending.append(flying.token)
        ends = max(self.now, self.free[link]) + cost
        self.free[link] = ends
        self.flights.append((ends + LANDING_SLACK_US, name, flying, done))
        self.flights.sort(key=lambda f: f[0])
        return flying

    def poll(self, block=False):
        while self.flights and (block or self.flights[0][0] <= self.now):
            ends, name, flying, done = self.flights.pop(0)
            self.now, block = max(self.now, ends), False
            done(*_wait_job(flying, name + "_wait", [self.last] + self.chain))


class _GatherStream:
    def __init__(self, backlog, bufs, kinds, costs, early=()):
        self.backlog, self.bufs, self.kinds, self.costs, self.begun, self.complete = backlog, bufs, kinds, costs, 0, set()
        self.early, self.relays, self.near_complete = set(early), {}, set()
        self.begin()

    def begin(self):
        u, self.begun = self.begun, self.begun + 1
        self.backlog.start(_gather_job(self.bufs[u], self.kinds[u], 1), "gather_%d" % u, "ici", 0.5 * self.costs[u],
                           lambda outs, kept: self.arrived(u, outs[0]))

    def arrived(self, u, buf):
        self.bufs[u] = buf
        if u in self.early:
            self.relays[u] = self.backlog.start(_gather_job(buf, self.kinds[u], "relay"), "relay_%d" % u, "ici",
                                                0.25 * self.costs[u], lambda outs, kept: self.relayed(u, outs[0]))
            self.backlog.start(_gather_job(self.relays[u].bufs[0], self.kinds[u], "direct"), "direct_%d" % u, "d2d",
                               0.5 * D2D_SHARE * self.costs[u], lambda outs, kept: self.near(u, outs[0]))
        else:
            self.backlog.start(_gather_job(buf, self.kinds[u], 2), "relay_%d" % u, "ici", 0.25 * self.costs[u],
                               lambda outs, kept: self.relayed(u, outs[0]))
        while self.begun <= min(u + GATHER_WINDOW[u], len(self.bufs) - 1):
            self.begin()

    def near(self, u, buf):
        self.bufs[u] = self.relays[u].bufs[0] = buf
        self.near_complete.add(u)

    def nearly_ready(self, u):
        while u not in self.near_complete:
            assert self.backlog.flights, "weight %d is not on its way" % u
            self.backlog.poll(block=True)
        return self.bufs[u]

    def relayed(self, u, buf):
        assert u not in self.early or u in self.near_complete, "the relay of weight %d is waited for before its sibling copies" % u
        self.bufs[u] = buf
        self.backlog.start(_gather_job(buf, self.kinds[u], 3), "handon_%d" % u, "d2d", D2D_SHARE * self.costs[u],
                           lambda outs, kept: self.handed(u, outs[0]))

    def handed(self, u, buf):
        self.bufs[u] = buf
        self.complete.add(u)

    def ready(self, u):
        while u not in self.complete:
            assert self.backlog.flights, "weight %d is not on its way" % u
            self.backlog.poll(block=True)
        return self.bufs[u]


class _GradStream:
    def __init__(self, backlog, u, name, kind, cost, g, place, results):
        self.backlog, self.u, self.name, self.kind, self.cost, self.place, self.results = backlog, u, name, kind, cost, place, results
        backlog.start(_exchange_halves_job([g], [kind]), "to_sibling_" + name, "d2d", D2D_SHARE * cost, self.exchanged)

    def exchanged(self, outs, kept):
        chip_sum = self.backlog.run(_add_pair, SIDE_KERNEL_US, kept[0], outs[0], self.kind, self.place, "chip_sum_" + self.name)
        self.backlog.start(_scatter_job(chip_sum, self.kind), "to_owners_" + self.name, "ici", self.cost, self.scattered)

    def scattered(self, outs, kept):
        reduced = self.backlog.run(_sum_chips, SIDE_KERNEL_US, kept[0], outs[0], self.kind, self.place, "reduce_" + self.name)
        self.backlog.start(_share_job([reduced]), "share_" + self.name, "d2d", D2D_SHARE * self.cost, self.shared)

    def shared(self, outs, kept):
        self.results[self.u] = outs[0]


SIDE_KERNEL_US = 12.0
D2D_SHARE = 0.15
LANDING_SLACK_US = 0.0
GATHER_WINDOW = (1, 1, 1, 2, 2, 1, 1, 1)


def kernel(x, mix_pre_g, mix_post_g, ffn_pre_g, ffn_post_g, ab_w_in, pool_w, pool_scale, conv_w, conv_b, conv_ln_g, conv_ln_b, ab_w_out, sc_w_in, sc_conv_w, sc_w_out, ffn_w1, ffn_w2, loss_target, m_mix_pre_g, m_mix_post_g, m_ffn_pre_g, m_ffn_post_g, m_ab_w_in, m_pool_w, m_pool_scale, m_conv_w, m_conv_b, m_conv_ln_g, m_conv_ln_b, m_ab_w_out, m_sc_w_in, m_sc_conv_w, m_sc_w_out, m_ffn_w1, m_ffn_w2, v_mix_pre_g, v_mix_post_g, v_ffn_pre_g, v_ffn_post_g, v_ab_w_in, v_pool_w, v_pool_scale, v_conv_w, v_conv_b, v_conv_ln_g, v_conv_ln_b, v_ab_w_out, v_sc_w_in, v_sc_conv_w, v_sc_w_out, v_ffn_w1, v_ffn_w2):
    x0, target = x[0], loss_target[0]
    T, D = x0.shape
    DP = pool_scale.shape[-1]
    gain = lambda g, layer: g[layer][None, :]

    big = [("ab_w_in", ab_w_in, 0, "col", 67.0), ("ab_w_out", ab_w_out, 0, "row", 44.0),
           ("ffn_w1_0", ffn_w1, 0, "col", 177.0), ("ffn_w2_0", ffn_w2, 0, "row", 177.0),
           ("sc_w_in", sc_w_in, 0, "col", 133.0), ("sc_w_out", sc_w_out, 0, "row", 44.0),
           ("ffn_w1_1", ffn_w1, 1, "col", 177.0), ("ffn_w2_1", ffn_w2, 1, "row", 177.0)]
    kinds = [b[3] for b in big]
    chip = 2 * lax.axis_index("x") + lax.axis_index("y")
    place = jnp.stack([chip, lax.axis_index("c")]).astype(jnp.int32)

    def own_in_zeros(shard, ax):
        full = jnp.zeros(tuple(d * N_CHIPS if i == ax else d for i, d in enumerate(shard.shape)), shard.dtype)
        return lax.dynamic_update_slice_in_dim(full, shard, chip * shard.shape[ax], axis=ax)

    W = [_cast_into_full(w, layer, kind, place, "cast_" + name) for name, w, layer, kind, _ in big]
    smalls = [own_in_zeros(pool_w[0], 1), own_in_zeros(conv_w[0], 1), own_in_zeros(sc_conv_w[0], 1)]
    backlog = _Backlog(x0)
    run = backlog.run
    small_weights = []
    backlog.start(_gather_small_job(smalls, [1, 1, 1]), "gather_small", "ici", 6.0, lambda outs, kept: small_weights.extend(outs))
    gather = _GatherStream(backlog, W, kinds, [b[4] for b in big], early=(3, 7))

    relu_sq = lambda acc: (jnp.maximum(acc, 0.0), jnp.square(jnp.maximum(acc, 0.0)))
    relu_sq_bwd = lambda acc, a: (acc * (2.0 * a.astype(F32)),)

    h0 = run(_norm_fwd, 12.0, x0, gain(mix_pre_g, 0), "norm_in")
    z0 = run(_matmul, 35.0, h0, gather.ready(0), "nn", "mix0_in")
    while not small_weights:
        backlog.poll(block=True)
    pool_w_full, conv_w_full, sc_conv_w_full = small_weights
    pooled, y_pool = run(_pool_fwd, 23.0, z0, pool_w_full, pool_scale, "pool_fwd")
    a_conv, c_conv = run(_conv_fwd, 25.0, z0, conv_w_full, conv_b, DP, "conv_fwd")
    y_conv = run(_ln_silu_fwd, 10.0, c_conv, conv_ln_g, conv_ln_b, "ln_silu_fwd")
    y0 = jnp.concatenate([y_pool, y_conv], axis=1)
    m0 = run(_matmul, 25.0, y0, gather.ready(1), "nn", "mix0_out")
    x1, h1 = run(_residual_norm, 21.0, x0, m0, gain(mix_post_g, 0), gain(ffn_pre_g, 0), "res_mix0")
    a0, a0sq = run(_matmul, 81.0, h1, gather.ready(2), "nn", "ffn0_up", out_dtypes=(BF16, BF16), epilogue=relu_sq)
    f0 = run(_matmul_by_chip_rows, 63.0, a0sq, gather.nearly_ready(3), place, "ffn0_down_near")
    f0 = run(_matmul_by_chip_rows, 21.0, a0sq, gather.ready(3), place, "ffn0_down", rest=f0)
    x2, h2 = run(_residual_norm, 22.0, x1, f0, gain(ffn_post_g, 0), gain(mix_pre_g, 1), "res_ffn0")
    z1 = run(_matmul, 62.0, h2, gather.ready(4), "nn", "mix1_in")
    y1 = run(_short_fwd, 22.0, z1, sc_conv_w_full, "short_fwd")
    m1 = run(_matmul, 25.0, y1, gather.ready(5), "nn", "mix1_out")
    x3, h3 = run(_residual_norm, 21.0, x2, m1, gain(mix_post_g, 1), gain(ffn_pre_g, 1), "res_mix1")
    a1, a1sq = run(_matmul, 81.0, h3, gather.ready(6), "nn", "ffn1_up", out_dtypes=(BF16, BF16), epilogue=relu_sq)
    f1 = run(_matmul_by_chip_rows, 63.0, a1sq, gather.nearly_ready(7), place, "ffn1_down_near")
    f1 = run(_matmul_by_chip_rows, 21.0, a1sq, gather.ready(7), place, "ffn1_down", rest=f1)
    w_in0, w_out0, w1_0, w2_0, w_in1, w_out1, w1_1, w2_1 = W

    grads_big = [None] * len(big)


    def reduce_grad(u, g):
        name, _, _, kind, cost = big[u]
        _GradStream(backlog, u, name, kind, cost, g, place, grads_big)

    dx, df1, d_ffn_post_1, loss_row = run(_loss_and_last_norm_bwd, 30.0, x3, f1, gain(ffn_post_g, 1), target, "loss")
    reduce_grad(7, run(_matmul, 80.0, a1sq, df1, "tn", "ffn1_down_dw", out_dtypes=(BF16,)))
    dz = run(_matmul, 82.0, df1, w2_1, "nt", "ffn1_down_dx", out_dtypes=(BF16,), epilogue=relu_sq_bwd, epi=(a1,))
    reduce_grad(6, run(_matmul, 80.0, h3, dz, "tn", "ffn1_up_dw", out_dtypes=(BF16,)))
    dh = run(_matmul, 87.0, dz, w1_1, "nt", "ffn1_up_dx", tk=LONG_K_TILE)
    dx, d_ffn_pre_1, dm1, d_mix_post_1 = run(_norms_bwd, 36.0, dx, dh, x3, gain(ffn_pre_g, 1), m1, gain(mix_post_g, 1), "norms_bwd3")

    reduce_grad(5, run(_matmul, 24.0, y1, dm1, "tn", "mix1_out_dw", out_dtypes=(BF16,)))
    dy1 = run(_matmul, 25.0, dm1, w_out1, "nt", "mix1_out_dx")
    db, dcg, du, d_sc_conv_w = run(_short_bwd, 41.0, dy1, z1, sc_conv_w_full, "short_bwd")
    dz1 = jnp.concatenate([db, dcg, du], axis=1)
    reduce_grad(4, run(_matmul, 62.0, h2, dz1, "tn", "mix1_in_dw", out_dtypes=(BF16,)))
    dh = run(_matmul, 68.0, dz1, w_in1, "nt", "mix1_in_dx")
    dx, d_mix_pre_1, df0, d_ffn_post_0 = run(_norms_bwd, 35.0, dx, dh, x2, gain(mix_pre_g, 1), f0, gain(ffn_post_g, 0), "norms_bwd2")

    reduce_grad(3, run(_matmul, 80.0, a0sq, df0, "tn", "ffn0_down_dw", out_dtypes=(BF16,)))
    dz = run(_matmul, 82.0, df0, w2_0, "nt", "ffn0_down_dx", out_dtypes=(BF16,), epilogue=relu_sq_bwd, epi=(a0,))
    reduce_grad(2, run(_matmul, 80.0, h1, dz, "tn", "ffn0_up_dw", out_dtypes=(BF16,)))
    dh = run(_matmul, 87.0, dz, w1_0, "nt", "ffn0_up_dx", tk=LONG_K_TILE)
    dx, d_ffn_pre_0, dm0, d_mix_post_0 = run(_norms_bwd, 36.0, dx, dh, x1, gain(ffn_pre_g, 0), m0, gain(mix_post_g, 0), "norms_bwd1")

    reduce_grad(1, run(_matmul, 24.0, y0, dm0, "tn", "mix0_out_dw", out_dtypes=(BF16,)))
    dy0 = run(_matmul, 25.0, dm0, w_out0, "nt", "mix0_out_dx")
    du_pool, d_pool_w, d_pool_scale = run(_pool_bwd, 28.0, dy0, pooled, pool_w_full, pool_scale, "pool_bwd")
    dc, d_ln_g, d_ln_b = run(_ln_silu_bwd, 15.0, dy0, c_conv, conv_ln_g, conv_ln_b, "ln_silu_bwd")
    dv, dgate, d_conv_w, d_conv_b = run(_conv_bwd, 52.0, dc, a_conv, z0, conv_w_full, DP, "conv_bwd")
    dz0 = jnp.concatenate([du_pool, dv, dgate], axis=1)

    small_sums = {}

    def exchange_small(key, arrays, cost):
        slots = _in_own_slot(_pack(arrays, D), place, "small_grads_slot_" + key)
        backlog.start(_small_exchange_job(slots), "small_grads_" + key, "ici", cost,
                      lambda outs, kept: small_sums.__setitem__(key, _unpack(_sum_slots(outs[0], "small_grads_sum_" + key),
                                                                             [a.shape for a in arrays])))

    exchange_small("most", [d_mix_pre_1, jnp.concatenate([d_mix_post_0, d_mix_post_1], 0),
                            jnp.concatenate([d_ffn_pre_0, d_ffn_pre_1], 0), jnp.concatenate([d_ffn_post_0, d_ffn_post_1], 0),
                            d_pool_scale, d_conv_b, d_ln_g, d_ln_b, d_pool_w, d_conv_w, d_sc_conv_w], 112.0)
    reduce_grad(0, run(_matmul, 34.0, h0, dz0, "tn", "mix0_in_dw", out_dtypes=(BF16,)))
    dh = run(_matmul, 40.0, dz0, w_in0, "nt", "mix0_in_dx")
    grad_x, d_mix_pre_0 = run(_norms_bwd, 26.0, dx, dh, x0, gain(mix_pre_g, 0), None, None, "norms_bwd0")
    exchange_small("last", [d_mix_pre_0, loss_row], 5.0)

    upd, gr, first = {}, grads_big, {}

    def keep(where, key, outs):
        where[key] = outs
        return outs

    adamw_big = [
        (7, lambda: keep(first, "ffn_w2", _adamw(ffn_w2, gr[7], m_ffn_w2, v_ffn_w2, "adamw_ffn_w2_1", layer=1)), 46.0),
        (6, lambda: keep(first, "ffn_w1", _adamw(ffn_w1, gr[6], m_ffn_w1, v_ffn_w1, "adamw_ffn_w1_1", layer=1)), 46.0),
        (5, lambda: keep(upd, "sc_w_out", _adamw(sc_w_out, gr[5], m_sc_w_out, v_sc_w_out, "adamw_sc_w_out")), 14.0),
        (4, lambda: keep(upd, "sc_w_in", _adamw(sc_w_in, gr[4], m_sc_w_in, v_sc_w_in, "adamw_sc_w_in")), 35.0),
        (3, lambda: keep(upd, "ffn_w2", _adamw(ffn_w2, gr[3], m_ffn_w2, v_ffn_w2, "adamw_ffn_w2_0", layer=0,
                                               carried=first["ffn_w2"])), 46.0),
        (2, lambda: keep(upd, "ffn_w1", _adamw(ffn_w1, gr[2], m_ffn_w1, v_ffn_w1, "adamw_ffn_w1_0", layer=0,
                                               carried=first["ffn_w1"])), 46.0),
        (1, lambda: keep(upd, "ab_w_out", _adamw(ab_w_out, gr[1], m_ab_w_out, v_ab_w_out, "adamw_ab_w_out")), 14.0),
        (0, lambda: keep(upd, "ab_w_in", _adamw(ab_w_in, gr[0], m_ab_w_in, v_ab_w_in, "adamw_ab_w_in")), 19.0)]
    while adamw_big or backlog.flights:
        due = [a for a in adamw_big if gr[a[0]] is not None]
        if due:
            adamw_big.remove(due[0])
            backlog.run(due[0][1], due[0][2])
        else:
            backlog.poll(block=True)

    (g_mix_pre_1, g_mix_post, g_ffn_pre, g_ffn_post, g_pool_scale, g_conv_b, g_ln_g, g_ln_b, g_pool_w_full, g_conv_w_full,
     g_sc_conv_w_full) = small_sums["most"]
    g_mix_pre = jnp.concatenate([small_sums["last"][0], g_mix_pre_1], 0)
    loss = small_sums["last"][1][0, 0]
    own = lambda a, ax: lax.dynamic_slice_in_dim(a, chip * (a.shape[ax] // N_CHIPS), a.shape[ax] // N_CHIPS, axis=ax)
    g_pool_w, g_conv_w, g_sc_conv_w = own(g_pool_w_full, 1), own(g_conv_w_full, 1), own(g_sc_conv_w_full, 1)

    def small_update(w, g, m, v, name):
        shape = w.shape
        as3 = lambda a: a.reshape((1, -1, shape[-1]))
        outs = _adamw(as3(w), g.reshape((-1, shape[-1])), as3(m), as3(v), "adamw_" + name)
        return [o.reshape(shape) for o in outs]

    upd["mix_pre_g"] = small_update(mix_pre_g, g_mix_pre, m_mix_pre_g, v_mix_pre_g, "mix_pre_g")
    upd["mix_post_g"] = small_update(mix_post_g, g_mix_post, m_mix_post_g, v_mix_post_g, "mix_post_g")
    upd["ffn_pre_g"] = small_update(ffn_pre_g, g_ffn_pre, m_ffn_pre_g, v_ffn_pre_g, "ffn_pre_g")
    upd["ffn_post_g"] = small_update(ffn_post_g, g_ffn_post, m_ffn_post_g, v_ffn_post_g, "ffn_post_g")
    upd["pool_w"] = small_update(pool_w, g_pool_w, m_pool_w, v_pool_w, "pool_w")
    upd["pool_scale"] = small_update(pool_scale, g_pool_scale, m_pool_scale, v_pool_scale, "pool_scale")
    upd["conv_w"] = small_update(conv_w, g_conv_w, m_conv_w, v_conv_w, "conv_w")
    upd["conv_b"] = small_update(conv_b, g_conv_b, m_conv_b, v_conv_b, "conv_b")
    upd["conv_ln_g"] = small_update(conv_ln_g, g_ln_g, m_conv_ln_g, v_conv_ln_g, "conv_ln_g")
    upd["conv_ln_b"] = small_update(conv_ln_b, g_ln_b, m_conv_ln_b, v_conv_ln_b, "conv_ln_b")
    upd["sc_conv_w"] = small_update(sc_conv_w, g_sc_conv_w, m_sc_conv_w, v_sc_conv_w, "sc_conv_w")

    order = ["mix_pre_g", "mix_post_g", "ffn_pre_g", "ffn_post_g", "ab_w_in", "pool_w", "pool_scale", "conv_w", "conv_b",
             "conv_ln_g", "conv_ln_b", "ab_w_out", "sc_w_in", "sc_conv_w", "sc_w_out", "ffn_w1", "ffn_w2"]
    out = [loss, grad_x[None]]
    for part in range(4):
        out += [upd[n][part] for n in order]
    return tuple(out)
```

```python
import jax
import jax.numpy as jnp
from jax import lax
from jax.experimental import pallas as pl
from jax.experimental.pallas import tpu as pltpu

F32, BF16 = jnp.float32, jnp.bfloat16
EPS = 1e-6
N_GROUPS = 4
MAX_WINDOW = 16
CONV_K = 31
SHORT_K = 3
CONV_PAD = 32
SHORT_PAD = 8
ADAM_LR, ADAM_B1, ADAM_B2, ADAM_EPS, ADAM_WD, ADAM_STEP = 0.001, 0.9, 0.999, 1e-08, 0.01, 10
N_CHIPS = 4
VMEM_LIMIT_BYTES = 56 * 1024 * 1024
ROW_TILE = 256
CHUNK = 256
LONG_K_TILE = 4096
MESH = pl.DeviceIdType.MESH
HBM = pl.BlockSpec(memory_space=pltpu.HBM)
SEM = pl.BlockSpec(memory_space=pltpu.SEMAPHORE)
ANY = pl.BlockSpec(memory_space=pl.ANY)


def _cp(*sem):
    return pltpu.CompilerParams(dimension_semantics=sem, vmem_limit_bytes=VMEM_LIMIT_BYTES)


def _sigmoid(v):
    return 1.0 / (1.0 + jnp.exp(-v))


class _Behind:
    pending = []


def _call(body, prefetch=None, **kw):
    behind, _Behind.pending = _Behind.pending, []
    single = not isinstance(kw["out_shape"], (list, tuple))
    in_specs, scratch = list(kw["in_specs"]), list(kw.get("scratch_shapes", ()))
    out_shape = [kw["out_shape"]] if single else list(kw["out_shape"])
    out_specs = [kw["out_specs"]] if single else list(kw["out_specs"])
    n_pre = 0 if prefetch is None else 1
    n_own, n_behind = len(in_specs), len(behind)

    def wrapped(*refs):
        body(*refs[:n_pre + n_own], *refs[n_pre + n_own + n_behind:])

    specs = dict(grid=kw["grid"], in_specs=in_specs + [ANY] * n_behind, out_specs=out_specs)
    if prefetch is None:
        specs["scratch_shapes"] = scratch
    else:
        specs = dict(grid_spec=pltpu.PrefetchScalarGridSpec(num_scalar_prefetch=1, scratch_shapes=scratch, **specs))
    aliases = {n_pre + i: o for i, o in kw.get("input_output_aliases", {}).items()}
    call = pl.pallas_call(wrapped, name=kw["name"], out_shape=out_shape, input_output_aliases=aliases,
                          compiler_params=_cp(*kw["sem"]), **specs)

    def run(*args):
        outs = call(*([prefetch] * n_pre), *args, *behind)
        return outs[0] if single else list(outs)

    return run


_DIMS = {"nn": (((1,), (0,)), ((), ())), "nt": (((1,), (1,)), ((), ())), "tn": (((0,), (0,)), ((), ()))}


def _pick(n, cap, step=256):
    if n <= cap:
        return n
    return next(t for t in range(cap - cap % step, 0, -step) if n % t == 0)


def _matmul(a, b, mode, name, out_dtypes=(F32,), epilogue=None, epi=(), stacked=0, tm=1024, tn=1024, tk=2048):
    if mode == "tn":
        (K, M), (K2, N) = a.shape, (b.shape if not stacked else (b.shape[1], stacked * b.shape[2]))
    elif mode == "nt":
        (M, K), (N, K2) = (a.shape if not stacked else (a.shape[1], stacked * a.shape[2])), b.shape
    else:
        (M, K), (K2, N) = a.shape, b.shape
    assert K == K2
    tm = _pick(M, tm)
    tn = _pick(N // stacked if stacked and mode == "tn" else N, tn)
    tk = _pick(K // stacked if stacked and mode == "nt" else K, tk)
    nk = K // tk
    a_spec = pl.BlockSpec((tk, tm), lambda i, j, k: (k, i)) if mode == "tn" else pl.BlockSpec((tm, tk), lambda i, j, k: (i, k))
    b_spec = pl.BlockSpec((tn, tk), lambda i, j, k: (j, k)) if mode == "nt" else pl.BlockSpec((tk, tn), lambda i, j, k: (k, j))
    if stacked and mode == "tn":
        per = N // stacked // tn
        b_spec = pl.BlockSpec((None, tk, tn), lambda i, j, k: (j // per, k, j % per))
    if stacked and mode == "nt":
        per = K // stacked // tk
        a_spec = pl.BlockSpec((None, tm, tk), lambda i, j, k: (k // per, i, k % per))
    o_spec = pl.BlockSpec((tm, tn), lambda i, j, k: (i, j))
    n_epi, n_out = len(epi), len(out_dtypes)

    def body(a_ref, b_ref, *rest):
        epi_refs, out_refs, scratch = rest[:n_epi], rest[n_epi:n_epi + n_out], rest[n_epi + n_out:]
        part = lax.dot_general(a_ref[...].astype(BF16), b_ref[...].astype(BF16), _DIMS[mode], preferred_element_type=F32)

        def finish(acc):
            outs = epilogue(acc, *[r[...] for r in epi_refs]) if epilogue else (acc,)
            for o_ref, o in zip(out_refs, outs):
                o_ref[...] = o.astype(o_ref.dtype)

        if nk == 1:
            finish(part)
        else:
            acc_ref = scratch[0]
            k = pl.program_id(2)

            @pl.when(k == 0)
            def _():
                acc_ref[...] = part

            @pl.when(k > 0)
            def _():
                acc_ref[...] += part

            @pl.when(k == nk - 1)
            def _():
                finish(acc_ref[...])

    outs = _call(
        body, name=name, grid=(M // tm, N // tn, nk),
        in_specs=[a_spec, b_spec] + [o_spec] * n_epi, out_specs=[o_spec] * n_out,
        out_shape=[jax.ShapeDtypeStruct((M, N), dt) for dt in out_dtypes],
        scratch_shapes=[pltpu.VMEM((tm, tn), F32)] if nk > 1 else [],
        sem=("parallel", "parallel", "arbitrary"))(a, b, *epi)
    return outs[0] if n_out == 1 else outs


def _matmul_by_chip_rows(a, b, place, name, rest=None, tm=1024, tn=1024):
    (M, K), N = a.shape, b.shape[1]
    tm, tn, tk = _pick(M, tm), _pick(N, tn), K // N_CHIPS
    nk = N_CHIPS - 1 if rest is None else 1

    def block(k, s):
        diagonal = N_CHIPS - 1 - s[0]
        return k + (k >= diagonal).astype(jnp.int32) if rest is None else diagonal

    def body(s_ref, a_ref, b_ref, *refs):
        o_ref, acc_ref = refs[-2:]
        part = jnp.dot(a_ref[...], b_ref[...], preferred_element_type=F32)
        k = pl.program_id(2)

        @pl.when(k == 0)
        def _():
            acc_ref[...] = part if rest is None else part + refs[0][...]

        @pl.when(k > 0)
        def _():
            acc_ref[...] += part

        @pl.when(k == nk - 1)
        def _():
            o_ref[...] = acc_ref[...]

    tile = pl.BlockSpec((tm, tn), lambda i, j, k, s: (i, j))
    in_specs = [pl.BlockSpec((tm, tk), lambda i, j, k, s: (i, block(k, s))), pl.BlockSpec((tk, tn), lambda i, j, k, s: (block(k, s), j))]
    return _call(body, prefetch=place, name=name, grid=(M // tm, N // tn, nk), in_specs=in_specs + ([] if rest is None else [tile]),
                 out_specs=tile, out_shape=jax.ShapeDtypeStruct((M, N), F32), scratch_shapes=[pltpu.VMEM((tm, tn), F32)],
                 sem=("parallel", "parallel", "arbitrary"))(a, b, *([] if rest is None else [rest]))


def _rms(x, g):
    r = lax.rsqrt(jnp.mean(x * x, axis=-1, keepdims=True) + EPS)
    return x * r * g


def _rms_bwd(dy, x, g):
    r = lax.rsqrt(jnp.mean(x * x, axis=-1, keepdims=True) + EPS)
    xn = x * r
    dyg = dy * g
    dx = r * (dyg - xn * jnp.mean(dyg * xn, axis=-1, keepdims=True))
    return dx, jnp.sum(dy * xn, axis=0, keepdims=True)


def _rows(d, tr=ROW_TILE):
    return pl.BlockSpec((tr, d), lambda i: (i, 0))


def _vec(d):
    return pl.BlockSpec((1, d), lambda i: (0, 0))


def _accumulate(ref, val):
    @pl.when(pl.program_id(0) == 0)
    def _():
        ref[...] = val

    @pl.when(pl.program_id(0) > 0)
    def _():
        ref[...] += val


def _norm_fwd(x, g, name):
    T, D = x.shape

    def body(x_ref, g_ref, h_ref):
        h_ref[...] = _rms(x_ref[...], g_ref[...]).astype(BF16)

    return _call(body,name=name,grid=(T // ROW_TILE,), in_specs=[_rows(D), _vec(D)], out_specs=_rows(D),
                          out_shape=jax.ShapeDtypeStruct((T, D), BF16), sem=("parallel",))(x, g)


def _residual_norm(x, m, g_post, g_next, name):
    T, D = x.shape

    def body(x_ref, m_ref, gp_ref, gn_ref, xo_ref, h_ref):
        xo = x_ref[...] + _rms(m_ref[...], gp_ref[...])
        xo_ref[...] = xo
        h_ref[...] = _rms(xo, gn_ref[...]).astype(BF16)

    return _call(body,name=name,grid=(T // ROW_TILE,), in_specs=[_rows(D), _rows(D), _vec(D), _vec(D)],
                          out_specs=[_rows(D), _rows(D)],
                          out_shape=[jax.ShapeDtypeStruct((T, D), F32), jax.ShapeDtypeStruct((T, D), BF16)],
                          sem=("parallel",))(x, m, g_post, g_next)


def _loss_and_last_norm_bwd(x, m, g_post, target, name):
    T, D = x.shape

    def body(x_ref, m_ref, gp_ref, t_ref, dx_ref, dm_ref, dg_ref, loss_ref):
        m_val, gp = m_ref[...], gp_ref[...]
        err = x_ref[...] + _rms(m_val, gp) - t_ref[...]
        dx = err * (1.0 / D)
        dx_ref[...] = dx
        dm, dg = _rms_bwd(dx, m_val, gp)
        dm_ref[...] = dm.astype(BF16)
        _accumulate(dg_ref, dg)
        _accumulate(loss_ref, jnp.full((1, 128), 0.5 * jnp.sum(err * err) * (1.0 / D), F32))

    return _call(body,name=name,grid=(T // ROW_TILE,), in_specs=[_rows(D), _rows(D), _vec(D), _rows(D)],
                          out_specs=[_rows(D), _rows(D), _vec(D), _vec(128)],
                          out_shape=[jax.ShapeDtypeStruct((T, D), F32), jax.ShapeDtypeStruct((T, D), BF16),
                                     jax.ShapeDtypeStruct((1, D), F32), jax.ShapeDtypeStruct((1, 128), F32)],
                          sem=("arbitrary",))(x, m, g_post, target)


def _norms_bwd(dx, dh, x_in, g_pre, m_prev, g_post_prev, name):
    T, D = dx.shape
    with_prev = m_prev is not None

    def body(*refs):
        if with_prev:
            dx_ref, dh_ref, x_ref, gq_ref, m_ref, gp_ref, dxo_ref, dgq_ref, dm_ref, dgp_ref = refs
        else:
            dx_ref, dh_ref, x_ref, gq_ref, dxo_ref, dgq_ref = refs
        d_in, dgq = _rms_bwd(dh_ref[...], x_ref[...], gq_ref[...])
        dxo = dx_ref[...] + d_in
        dxo_ref[...] = dxo
        _accumulate(dgq_ref, dgq)
        if with_prev:
            dm, dgp = _rms_bwd(dxo, m_ref[...], gp_ref[...])
            dm_ref[...] = dm.astype(BF16)
            _accumulate(dgp_ref, dgp)

    ins, in_specs = [dx, dh, x_in, g_pre], [_rows(D), _rows(D), _rows(D), _vec(D)]
    out_specs = [_rows(D), _vec(D)]
    out_shape = [jax.ShapeDtypeStruct((T, D), F32), jax.ShapeDtypeStruct((1, D), F32)]
    if with_prev:
        ins += [m_prev, g_post_prev]
        in_specs += [_rows(D), _vec(D)]
        out_specs += [_rows(D), _vec(D)]
        out_shape += [jax.ShapeDtypeStruct((T, D), BF16), jax.ShapeDtypeStruct((1, D), F32)]
    return _call(body,name=name,grid=(T // ROW_TILE,), in_specs=in_specs, out_specs=out_specs, out_shape=out_shape,
                          sem=("arbitrary",))(*ins)


def _window_weights(g):
    w = 2 << g
    return w, [jnp.where(j < w, 1.0, 0.0).astype(F32) for j in range(MAX_WINDOW)]


def _valid_count(r0, rows, w):
    t = (lax.broadcasted_iota(jnp.int32, (rows, 1), 0) + (r0 + 1)).astype(F32)
    return jnp.minimum(t, w.astype(F32))


def _pool_fwd(z, pool_w, pool_scale, name):
    T = z.shape[0]
    PG = pool_w.shape[-1]
    DP = N_GROUPS * PG
    rc = min(CHUNK, T)

    def body(u_ref, pw_ref, sc_ref, pooled_ref, y_ref, pad):
        w, wts = _window_weights(pl.program_id(0))
        pad[pl.ds(0, MAX_WINDOW), :] = jnp.zeros((MAX_WINDOW, PG), F32)
        pad[pl.ds(MAX_WINDOW, T), :] = u_ref[...]
        for r0 in range(0, T, rc):
            acc = jnp.zeros((rc, PG), F32)
            for j in range(MAX_WINDOW):
                acc = acc + wts[j] * pad[pl.ds(MAX_WINDOW + r0 - j, rc), :]
            pooled = acc / _valid_count(r0, rc, w) - u_ref[pl.ds(r0, rc), :]
            pooled_ref[pl.ds(r0, rc), :] = pooled.astype(BF16)
        mixed = jnp.dot(pooled_ref[...], pw_ref[...].astype(BF16), preferred_element_type=F32)
        y_ref[...] = (mixed * sc_ref[...]).astype(BF16)

    col = lambda g: (0, g)
    return _call(
        body, name=name,grid=(N_GROUPS,),
        in_specs=[pl.BlockSpec((T, PG), col), pl.BlockSpec((None, PG, PG), lambda g: (g, 0, 0)), pl.BlockSpec((1, PG), col)],
        out_specs=[pl.BlockSpec((T, PG), col), pl.BlockSpec((T, PG), col)],
        out_shape=[jax.ShapeDtypeStruct((T, DP), BF16), jax.ShapeDtypeStruct((T, DP), BF16)],
        scratch_shapes=[pltpu.VMEM((T + MAX_WINDOW, PG), F32)], sem=("parallel",))(z, pool_w, pool_scale)


def _pool_bwd(dy, pooled, pool_w, pool_scale, name):
    T = dy.shape[0]
    PG = pool_w.shape[-1]
    DP = N_GROUPS * PG
    rc = min(CHUNK, T)

    def body(dy_ref, pooled_ref, pw_ref, sc_ref, du_ref, dpw_ref, dsc_ref, pad, dp_ref):
        w, wts = _window_weights(pl.program_id(0))
        pooled_v, pw = pooled_ref[...], pw_ref[...].astype(BF16)
        dy_v = dy_ref[...]
        mixed = jnp.dot(pooled_v, pw, preferred_element_type=F32)
        dsc_ref[...] = jnp.sum(dy_v * mixed, axis=0, keepdims=True)
        dmixed = (dy_v * sc_ref[...]).astype(BF16)
        dpw_ref[...] = lax.dot_general(pooled_v, dmixed, _DIMS["tn"], preferred_element_type=F32)
        dp_ref[...] = lax.dot_general(dmixed, pw, _DIMS["nt"], preferred_element_type=F32)
        pad[pl.ds(T, MAX_WINDOW), :] = jnp.zeros((MAX_WINDOW, PG), F32)
        for r0 in range(0, T, rc):
            pad[pl.ds(r0, rc), :] = dp_ref[pl.ds(r0, rc), :] / _valid_count(r0, rc, w)
        for r0 in range(0, T, rc):
            acc = jnp.zeros((rc, PG), F32)
            for j in range(MAX_WINDOW):
                acc = acc + wts[j] * pad[pl.ds(r0 + j, rc), :]
            du_ref[pl.ds(r0, rc), :] = (acc - dp_ref[pl.ds(r0, rc), :]).astype(BF16)

    col = lambda g: (0, g)
    return _call(
        body, name=name,grid=(N_GROUPS,),
        in_specs=[pl.BlockSpec((T, PG), col), pl.BlockSpec((T, PG), col), pl.BlockSpec((None, PG, PG), lambda g: (g, 0, 0)),
                  pl.BlockSpec((1, PG), col)],
        out_specs=[pl.BlockSpec((T, PG), col), pl.BlockSpec((None, PG, PG), lambda g: (g, 0, 0)), pl.BlockSpec((1, PG), col)],
        out_shape=[jax.ShapeDtypeStruct((T, DP), BF16), jax.ShapeDtypeStruct((N_GROUPS, PG, PG), F32),
                   jax.ShapeDtypeStruct((1, DP), F32)],
        scratch_shapes=[pltpu.VMEM((T + MAX_WINDOW, PG), F32), pltpu.VMEM((T, PG), F32)],
        sem=("parallel",))(dy, pooled, pool_w, pool_scale)


def _conv_fwd(z, conv_w, conv_b, d_pool, name, tc=128):
    T = z.shape[0]
    DC = conv_w.shape[-1]
    rc = min(CHUNK, T)
    v0, g0 = d_pool // tc, (d_pool + DC) // tc

    def body(v_ref, gt_ref, w_ref, b_ref, a_ref, c_ref, pad):
        pad[pl.ds(0, CONV_PAD), :] = jnp.zeros((CONV_PAD, tc), F32)
        for r0 in range(0, T, rc):
            a = v_ref[pl.ds(r0, rc), :] * _sigmoid(gt_ref[pl.ds(r0, rc), :])
            a_ref[pl.ds(r0, rc), :] = a
            pad[pl.ds(CONV_PAD + r0, rc), :] = a
        for r0 in range(0, T, rc):
            acc = jnp.zeros((rc, tc), F32) + b_ref[...]
            for k in range(CONV_K):
                acc = acc + w_ref[pl.ds(k, 1), :] * pad[pl.ds(CONV_PAD - (CONV_K - 1) + k + r0, rc), :]
            c_ref[pl.ds(r0, rc), :] = acc

    col = lambda j: (0, j)
    return _call(
        body, name=name,grid=(DC // tc,),
        in_specs=[pl.BlockSpec((T, tc), lambda j: (0, v0 + j)), pl.BlockSpec((T, tc), lambda j: (0, g0 + j)),
                  pl.BlockSpec((CONV_K, tc), col), pl.BlockSpec((1, tc), col)],
        out_specs=[pl.BlockSpec((T, tc), col), pl.BlockSpec((T, tc), col)],
        out_shape=[jax.ShapeDtypeStruct((T, DC), F32), jax.ShapeDtypeStruct((T, DC), F32)],
        scratch_shapes=[pltpu.VMEM((T + CONV_PAD, tc), F32)], sem=("parallel",))(z, z, conv_w, conv_b)


def _conv_bwd(dc, a, z, conv_w, d_pool, name, tc=128):
    T, DC = dc.shape
    rc = min(CHUNK, T)
    v0, g0 = d_pool // tc, (d_pool + DC) // tc

    def body(dc_ref, a_ref, v_ref, gt_ref, w_ref, dv_ref, dg_ref, dw_ref, db_ref, apad, dpad):
        apad[pl.ds(0, CONV_PAD), :] = jnp.zeros((CONV_PAD, tc), F32)
        apad[pl.ds(CONV_PAD, T), :] = a_ref[...]
        dpad[pl.ds(0, T), :] = dc_ref[...]
        dpad[pl.ds(T, CONV_PAD), :] = jnp.zeros((CONV_PAD, tc), F32)
        db_ref[...] = jnp.sum(dc_ref[...], axis=0, keepdims=True)
        for k in range(CONV_K):
            acc = jnp.zeros((8, tc), F32)
            for r0 in range(0, T, rc):
                prod = dc_ref[pl.ds(r0, rc), :] * apad[pl.ds(CONV_PAD - (CONV_K - 1) + k + r0, rc), :]
                acc = acc + jnp.sum(prod.reshape(rc // 8, 8, tc), axis=0)
            dw_ref[pl.ds(k, 1), :] = jnp.sum(acc, axis=0, keepdims=True)
        for r0 in range(0, T, rc):
            da = jnp.zeros((rc, tc), F32)
            for k in range(CONV_K):
                da = da + w_ref[pl.ds(k, 1), :] * dpad[pl.ds(r0 + (CONV_K - 1) - k, rc), :]
            sig = _sigmoid(gt_ref[pl.ds(r0, rc), :])
            dv_ref[pl.ds(r0, rc), :] = (da * sig).astype(BF16)
            dg_ref[pl.ds(r0, rc), :] = (da * v_ref[pl.ds(r0, rc), :] * sig * (1.0 - sig)).astype(BF16)

    col = lambda j: (0, j)
    return _call(
        body, name=name,grid=(DC // tc,),
        in_specs=[pl.BlockSpec((T, tc), col), pl.BlockSpec((T, tc), col), pl.BlockSpec((T, tc), lambda j: (0, v0 + j)),
                  pl.BlockSpec((T, tc), lambda j: (0, g0 + j)), pl.BlockSpec((CONV_K, tc), col)],
        out_specs=[pl.BlockSpec((T, tc), col), pl.BlockSpec((T, tc), col), pl.BlockSpec((CONV_K, tc), col),
                   pl.BlockSpec((1, tc), col)],
        out_shape=[jax.ShapeDtypeStruct((T, DC), BF16), jax.ShapeDtypeStruct((T, DC), BF16),
                   jax.ShapeDtypeStruct((CONV_K, DC), F32), jax.ShapeDtypeStruct((1, DC), F32)],
        scratch_shapes=[pltpu.VMEM((T + CONV_PAD, tc), F32), pltpu.VMEM((T + CONV_PAD, tc), F32)],
        sem=("parallel",))(dc, a, z, z, conv_w)


def _layer_norm_parts(c, g, b):
    mu = jnp.mean(c, axis=-1, keepdims=True)
    xc = c - mu
    rstd = lax.rsqrt(jnp.mean(xc * xc, axis=-1, keepdims=True) + EPS)
    xhat = xc * rstd
    return xhat, rstd, xhat * g + b


def _ln_silu_fwd(c, g, b, name):
    T, DC = c.shape

    def body(c_ref, g_ref, b_ref, y_ref):
        _, _, ln = _layer_norm_parts(c_ref[...], g_ref[...], b_ref[...])
        y_ref[...] = (ln * _sigmoid(ln)).astype(BF16)

    return _call(body,name=name,grid=(T // ROW_TILE,), in_specs=[_rows(DC), _vec(DC), _vec(DC)], out_specs=_rows(DC),
                          out_shape=jax.ShapeDtypeStruct((T, DC), BF16), sem=("parallel",))(c, g, b)


def _ln_silu_bwd(dy, c, g, b, name):
    T, DC = c.shape

    def body(dy_ref, c_ref, g_ref, b_ref, dc_ref, dg_ref, db_ref):
        gain = g_ref[...]
        xhat, rstd, ln = _layer_norm_parts(c_ref[...], gain, b_ref[...])
        s = _sigmoid(ln)
        dln = dy_ref[...] * (s * (1.0 + ln * (1.0 - s)))
        _accumulate(dg_ref, jnp.sum(dln * xhat, axis=0, keepdims=True))
        _accumulate(db_ref, jnp.sum(dln, axis=0, keepdims=True))
        dxh = dln * gain
        dc_ref[...] = rstd * (dxh - jnp.mean(dxh, axis=-1, keepdims=True) - xhat * jnp.mean(dxh * xhat, axis=-1, keepdims=True))

    return _call(body,name=name,grid=(T // ROW_TILE,),
                          in_specs=[pl.BlockSpec((ROW_TILE, DC), lambda i: (i, 1)), _rows(DC), _vec(DC), _vec(DC)],
                          out_specs=[_rows(DC), _vec(DC), _vec(DC)],
                          out_shape=[jax.ShapeDtypeStruct((T, DC), F32), jax.ShapeDtypeStruct((1, DC), F32),
                                     jax.ShapeDtypeStruct((1, DC), F32)],
                          sem=("arbitrary",))(dy, c, g, b)


def _short_specs(T, DS, tc):
    n = DS // tc
    return [pl.BlockSpec((T, tc), lambda j: (0, j)), pl.BlockSpec((T, tc), lambda j: (0, n + j)),
            pl.BlockSpec((T, tc), lambda j: (0, 2 * n + j))]


def _short_fwd(z, w, name, tc=256):
    T = z.shape[0]
    DS = w.shape[-1]
    rc = min(CHUNK, T)

    def body(b_ref, cg_ref, u_ref, w_ref, y_ref, pad):
        pad[pl.ds(0, SHORT_PAD), :] = jnp.zeros((SHORT_PAD, tc), F32)
        pad[pl.ds(SHORT_PAD, T), :] = cg_ref[...] * u_ref[...]
        for r0 in range(0, T, rc):
            r = jnp.zeros((rc, tc), F32)
            for k in range(SHORT_K):
                r = r + w_ref[pl.ds(k, 1), :] * pad[pl.ds(SHORT_PAD - (SHORT_K - 1) + k + r0, rc), :]
            y_ref[pl.ds(r0, rc), :] = (b_ref[pl.ds(r0, rc), :] * r).astype(BF16)

    col = lambda j: (0, j)
    return _call(body,name=name,grid=(DS // tc,), in_specs=_short_specs(T, DS, tc) + [pl.BlockSpec((SHORT_K, tc), col)],
                          out_specs=pl.BlockSpec((T, tc), col), out_shape=jax.ShapeDtypeStruct((T, DS), BF16),
                          scratch_shapes=[pltpu.VMEM((T + SHORT_PAD, tc), F32)], sem=("parallel",))(z, z, z, w)


def _short_bwd(dy, z, w, name, tc=256):
    T, DS = dy.shape
    rc = min(CHUNK, T)

    def body(dy_ref, b_ref, cg_ref, u_ref, w_ref, dz_ref, dw_ref, qpad, rpad):
        qpad[pl.ds(0, SHORT_PAD), :] = jnp.zeros((SHORT_PAD, tc), F32)
        qpad[pl.ds(SHORT_PAD, T), :] = cg_ref[...] * u_ref[...]
        rpad[pl.ds(0, T), :] = dy_ref[...] * b_ref[...]
        rpad[pl.ds(T, SHORT_PAD), :] = jnp.zeros((SHORT_PAD, tc), F32)
        accs = [jnp.zeros((8, tc), F32) for _ in range(SHORT_K)]
        for r0 in range(0, T, rc):
            r = jnp.zeros((rc, tc), F32)
            dq = jnp.zeros((rc, tc), F32)
            dr = rpad[pl.ds(r0, rc), :]
            for k in range(SHORT_K):
                q_k = qpad[pl.ds(SHORT_PAD - (SHORT_K - 1) + k + r0, rc), :]
                r = r + w_ref[pl.ds(k, 1), :] * q_k
                dq = dq + w_ref[pl.ds(k, 1), :] * rpad[pl.ds(r0 + (SHORT_K - 1) - k, rc), :]
                accs[k] = accs[k] + jnp.sum((dr * q_k).reshape(rc // 8, 8, tc), axis=0)
            dz_ref[0, pl.ds(r0, rc), :] = (dy_ref[pl.ds(r0, rc), :] * r).astype(BF16)
            dz_ref[1, pl.ds(r0, rc), :] = (dq * u_ref[pl.ds(r0, rc), :]).astype(BF16)
            dz_ref[2, pl.ds(r0, rc), :] = (dq * cg_ref[pl.ds(r0, rc), :]).astype(BF16)
        for k in range(SHORT_K):
            dw_ref[pl.ds(k, 1), :] = jnp.sum(accs[k], axis=0, keepdims=True)

    col = lambda j: (0, j)
    tile = pl.BlockSpec((T, tc), col)
    return _call(body,name=name,grid=(DS // tc,),
                          in_specs=[tile] + _short_specs(T, DS, tc) + [pl.BlockSpec((SHORT_K, tc), col)],
                          out_specs=[pl.BlockSpec((3, T, tc), lambda j: (0, 0, j)), pl.BlockSpec((SHORT_K, tc), col)],
                          out_shape=[jax.ShapeDtypeStruct((3, T, DS), BF16), jax.ShapeDtypeStruct((SHORT_K, DS), F32)],
                          scratch_shapes=[pltpu.VMEM((T + SHORT_PAD, tc), F32), pltpu.VMEM((T + SHORT_PAD, tc), F32)],
                          sem=("parallel",))(dy, z, z, z, w)


def _tile_rows(rows, cols, n_bufs):
    budget = VMEM_LIMIT_BYTES * 3 // 4 // (2 * n_bufs * 4 * cols)
    tr = rows
    while tr > budget and tr % 16 == 0:
        tr //= 2
    return tr


def _placed_call(body, name, place, grid, in_specs, out_specs, out_shape, ins):
    return _call(body,prefetch=place, name=name, grid=grid, in_specs=in_specs, out_specs=out_specs, out_shape=out_shape,
                 sem=("parallel",))(*ins)


def _cast_into_full(w, layer, kind, place, name):
    _, R, C = w.shape
    tr = _tile_rows(R, C, 2)
    nb = R // tr
    if kind == "col":
        full, out_spec = (R, C * N_CHIPS), pl.BlockSpec((tr, C), lambda i, s: (i, s[0]))
    else:
        full, out_spec = (R * N_CHIPS, C), pl.BlockSpec((tr, C), lambda i, s: (s[0] * nb + i, 0))

    def body(s_ref, w_ref, o_ref):
        o_ref[...] = w_ref[...].astype(BF16)

    return _placed_call(body, name, place, (nb,), [pl.BlockSpec((None, tr, C), lambda i, s: (layer, i, 0))], out_spec,
                        jax.ShapeDtypeStruct(full, BF16), [w])


def _add_pair(grad, theirs, kind, place, name):
    R, C = grad.shape
    piece_rows = R // 2 if kind == "col" else R // N_CHIPS // 2
    tr = _tile_rows(piece_rows, C, 3)
    nb = piece_rows // tr
    if kind == "col":
        g_spec = pl.BlockSpec((tr, C), lambda i, s: (s[1] * nb + i, 0))
    else:
        g_spec = pl.BlockSpec((tr, C), lambda i, s: ((2 * (i // nb) + s[1]) * nb + i % nb, 0))
    flat = pl.BlockSpec((tr, C), lambda i, s: (i, 0))

    def body(s_ref, a_ref, b_ref, o_ref):
        o_ref[...] = (a_ref[...].astype(F32) + b_ref[...].astype(F32)).astype(BF16)

    return _placed_call(body, name, place, (R // 2 // tr,), [g_spec, flat], flat, jax.ShapeDtypeStruct((R // 2, C), BF16),
                        [grad, theirs])


def _sum_chips(chip_sum, arrived, kind, place, name):
    _, H, W = arrived.shape
    tr = _tile_rows(H, W, 6)
    nb = H // tr
    if kind == "col":
        own_spec = pl.BlockSpec((tr, W), lambda i, s: (i, s[0]))
    else:
        own_spec = pl.BlockSpec((tr, W), lambda i, s: (s[0] * nb + i, 0))

    def body(s_ref, p_ref, r_ref, o_ref):
        acc = p_ref[...].astype(F32)
        for i in range(N_CHIPS - 1):
            acc = acc + r_ref[i].astype(F32)
        o_ref[...] = acc

    return _placed_call(body, name, place, (nb,), [own_spec, pl.BlockSpec((N_CHIPS - 1, tr, W), lambda i, s: (0, i, 0))],
                        pl.BlockSpec((tr, W), lambda i, s: (s[1] * nb + i, 0)), jax.ShapeDtypeStruct((2 * H, W), F32),
                        [chip_sum, arrived])


def _adamw_values(w, g, m, v):
    m = ADAM_B1 * m + (1.0 - ADAM_B1) * g
    v = ADAM_B2 * v + (1.0 - ADAM_B2) * (g * g)
    m_hat = m / (1.0 - ADAM_B1 ** ADAM_STEP)
    v_hat = v / (1.0 - ADAM_B2 ** ADAM_STEP)
    return -ADAM_LR * (m_hat / (jnp.sqrt(v_hat) + ADAM_EPS) + ADAM_WD * w), m, v


def _adamw(w, g, m, v, name, layer=0, carried=None):
    L, R, C = w.shape
    tr = _tile_rows(R, C, 8)

    def body(w_ref, g_ref, m_ref, v_ref, *rest):
        go_ref, d_ref, mo_ref, vo_ref = rest[-4:]
        g_val = g_ref[...]
        d, m_new, v_new = _adamw_values(w_ref[...], g_val, m_ref[...], v_ref[...])
        go_ref[...], d_ref[...], mo_ref[...], vo_ref[...] = g_val, d, m_new, v_new

    lay = pl.BlockSpec((None, tr, C), lambda i: (layer, i, 0))
    ins = [w, g, m, v]
    in_specs = [lay, pl.BlockSpec((tr, C), lambda i: (i, 0)), lay, lay]
    aliases = {}
    if carried is not None:
        ins += list(carried)
        in_specs += [pl.BlockSpec(memory_space=pl.ANY)] * 4
        aliases = {4 + i: i for i in range(4)}
    return _call(body, name=name, grid=(R // tr,), in_specs=in_specs, out_specs=[lay] * 4,
                 out_shape=[jax.ShapeDtypeStruct((L, R, C), F32)] * 4, input_output_aliases=aliases, sem=("parallel",))(*ins)


def _aligned(v, m):
    return v if isinstance(v, int) else pl.multiple_of(v, m)


def _place():
    x, y, c = lax.axis_index("x"), lax.axis_index("y"), lax.axis_index("c")
    other_chips = [(x, 1 - y), (1 - x, y), (1 - x, 1 - y)]
    return x, y, c, 2 * x + y, other_chips


def _chip_index(chip):
    return 2 * chip[0] + chip[1]


def _piece(ref, kind, k, h):
    R, C = ref.shape
    if kind == "col":
        return ref.at[pl.ds(_aligned(h * (R // 2), 16), R // 2), pl.ds(_aligned(k * (C // N_CHIPS), 128), C // N_CHIPS)]
    rs = R // N_CHIPS
    return ref.at[pl.ds(_aligned(k * rs + h * (rs // 2), 16), rs // 2), :]


def _compact_piece(ref, kind, k):
    R2, C = ref.shape
    if kind == "col":
        return ref.at[:, pl.ds(_aligned(k * (C // N_CHIPS), 128), C // N_CHIPS)]
    return ref.at[pl.ds(_aligned(k * (R2 // N_CHIPS), 16), R2 // N_CHIPS), :]


def _half_rows(ref, h):
    R = ref.shape[0]
    return ref.at[pl.ds(_aligned(h * (R // 2), 16), R // 2), :]


class _Copies:
    def __init__(self, send_sems, recv_sems):
        self.send_sems, self.recv_sems = send_sems, recv_sems
        self.n_remote = 0

    def remote(self, src, dst, device):
        k = self.n_remote
        self.n_remote += 1
        return pltpu.make_async_remote_copy(src_ref=src, dst_ref=dst, send_sem=self.send_sems.at[k], recv_sem=self.recv_sems.at[k],
                                            device_id=device, device_id_type=MESH)


class _Job:
    def __init__(self, ins, out_shape, aliases, n_remote, build):
        self.ins, self.out_shape, self.aliases, self.n_remote, self.build = list(ins), list(out_shape), dict(aliases), n_remote, build


class _Flying:
    def __init__(self, job, send_sems, recv_sems, bufs, token):
        self.job, self.send_sems, self.recv_sems, self.bufs, self.token = job, send_sems, recv_sems, bufs, token


def _job_refs(job, buf_refs):
    n_out = len(job.out_shape)
    kept = [i for i in range(len(job.ins)) if i not in job.aliases]
    ins = [buf_refs[job.aliases[i]] if i in job.aliases else buf_refs[n_out + kept.index(i)] for i in range(len(job.ins))]
    return ins, list(buf_refs[:n_out])


def _start_job(job, name, after=()):
    n_in, n_out, n_after = len(job.ins), len(job.out_shape), len(after)
    kept = [i for i in range(n_in) if i not in job.aliases]
    n_bufs = n_out + len(kept)

    def body(*refs):
        in_refs, out_refs = refs[:n_in], refs[n_in + n_after:n_in + n_after + n_out]
        send_sems, recv_sems, token = refs[n_in + n_after + n_bufs:]
        for d in job.build(in_refs, out_refs, _Copies(send_sems, recv_sems)):
            d.start()
        token[...] = jnp.zeros_like(token)

    aliases = dict(job.aliases)
    aliases.update({i: n_out + k for k, i in enumerate(kept)})
    sems = pltpu.SemaphoreType.DMA((job.n_remote,))
    outs = pl.pallas_call(
        body, name=name, in_specs=[HBM] * n_in + [ANY] * n_after,
        out_specs=[HBM] * n_bufs + [SEM, SEM, pl.BlockSpec(memory_space=pltpu.VMEM)],
        out_shape=job.out_shape + [jax.ShapeDtypeStruct(job.ins[i].shape, job.ins[i].dtype) for i in kept]
        + [sems, sems, jax.ShapeDtypeStruct((8, 128), F32)],
        input_output_aliases=aliases,
        compiler_params=pltpu.CompilerParams(has_side_effects=pltpu.SideEffectType.DATAFLOW_SIDE_EFFECTING))(*job.ins, *after)
    return _Flying(job, outs[n_bufs], outs[n_bufs + 1], list(outs[:n_bufs]), outs[n_bufs + 2])


def _wait_job(flying, name, after=()):
    job, n_bufs, n_after = flying.job, len(flying.bufs), len(after)

    def body(*refs):
        in_refs, out_refs = _job_refs(job, refs[:n_bufs])
        send_sems, recv_sems = refs[n_bufs:n_bufs + 2]
        copies = job.build(in_refs, out_refs, _Copies(send_sems, recv_sems))
        for d in copies:
            d.wait_send()
        for d in copies:
            d.wait_recv()

    outs = pl.pallas_call(
        body, name=name, in_specs=[HBM] * n_bufs + [SEM, SEM] + [ANY] * n_after, out_specs=[HBM] * n_bufs,
        out_shape=[jax.ShapeDtypeStruct(b.shape, b.dtype) for b in flying.bufs],
        input_output_aliases={i: i for i in range(n_bufs)},
        compiler_params=pltpu.CompilerParams(has_side_effects=pltpu.SideEffectType.DATAFLOW_SIDE_EFFECTING))(
            *flying.bufs, flying.send_sems, flying.recv_sems, *after)
    return list(outs[:len(job.out_shape)]), list(outs[len(job.out_shape):])


def _in_place(arrays):
    return [jax.ShapeDtypeStruct(a.shape, a.dtype) for a in arrays], {u: u for u in range(len(arrays))}


def _rows_part(ref, part, n_parts):
    h = ref.shape[0] // n_parts
    return ref.at[pl.ds(part * h, h), :]


def _gather_job(full, kind, stage):
    def build(in_refs, out_refs, cp):
        x, y, c, me, (y_nbr, x_nbr, diagonal) = _place()
        (ref,) = out_refs
        sibling = (x, y, 1 - c)
        if stage == 1:
            mine = _piece(ref, kind, me, c)
            return [cp.remote(mine, mine, (*y_nbr, c)), cp.remote(mine, mine, (*x_nbr, c))]
        from_y, from_x = _piece(ref, kind, _chip_index(y_nbr), c), _piece(ref, kind, _chip_index(x_nbr), c)
        copies = []
        if stage in (2, "relay"):
            relay_0, relay_1 = _rows_part(from_x, 0, 2), _rows_part(from_y, 1, 2)
            copies += [cp.remote(relay_0, relay_0, (*y_nbr, c)), cp.remote(relay_1, relay_1, (*x_nbr, c))]
        if stage in (2, "direct"):
            copies += [cp.remote(from_y, from_y, sibling), cp.remote(from_x, from_x, sibling)]
        if stage == 3:
            from_diagonal = _piece(ref, kind, _chip_index(diagonal), c)
            copies.append(cp.remote(from_diagonal, from_diagonal, sibling))
        return copies

    return _Job([full], *_in_place([full]), {1: 2, 2: 4, "relay": 2, "direct": 2, 3: 1}[stage], build)


def _gather_small_job(fulls, axes):
    def build(in_refs, out_refs, cp):
        x, y, c, me, chips = _place()
        copies = []
        for ref, ax in zip(out_refs, axes):
            n = ref.shape[ax] // N_CHIPS
            idx = [slice(None)] * len(ref.shape)
            idx[ax] = pl.ds(_aligned(me * n, n), n)
            mine = ref.at[tuple(idx)]
            copies += [cp.remote(mine, mine, (*chip, c)) for chip in chips]
        return copies

    return _Job(fulls, *_in_place(fulls), 3 * len(fulls), build)


def _exchange_halves_job(grads, kinds):
    def build(in_refs, out_refs, cp):
        x, y, c, me, chips = _place()
        copies = []
        for src, dst, kind in zip(in_refs, out_refs, kinds):
            if kind == "col":
                copies.append(cp.remote(_half_rows(src, 1 - c), dst, (x, y, 1 - c)))
            else:
                copies += [cp.remote(_piece(src, "row", k, 1 - c), _compact_piece(dst, "row", k), (x, y, 1 - c))
                           for k in range(N_CHIPS)]
        return copies

    out_shape = [jax.ShapeDtypeStruct((g.shape[0] // 2, g.shape[1]), g.dtype) for g in grads]
    return _Job(grads, out_shape, {}, sum(1 if k == "col" else N_CHIPS for k in kinds), build)


def _scatter_job(half, kind):
    def build(in_refs, out_refs, cp):
        x, y, c, me, chips = _place()
        (src,), (dst,) = in_refs, out_refs
        return [cp.remote(_compact_piece(src, kind, _chip_index(chip)), dst.at[r], (*chip, c)) for r, chip in enumerate(chips)]

    part_shape = (half.shape[0], half.shape[1] // N_CHIPS) if kind == "col" else (half.shape[0] // N_CHIPS, half.shape[1])
    return _Job([half], [jax.ShapeDtypeStruct((N_CHIPS - 1,) + part_shape, half.dtype)], {}, N_CHIPS - 1, build)


def _share_job(shards):
    def build(in_refs, out_refs, cp):
        x, y, c, me, chips = _place()
        copies = []
        for ref in out_refs:
            mine = _half_rows(ref, c)
            copies.append(cp.remote(mine, mine, (x, y, 1 - c)))
        return copies

    return _Job(shards, *_in_place(shards), len(shards), build)


N_DEVICES = 2 * N_CHIPS


def _small_exchange_job(slots):
    def build(in_refs, out_refs, cp):
        x, y, c, me, chips = _place()
        (ref,) = out_refs
        mine = ref.at[2 * me + c]
        return [cp.remote(mine, mine, (x ^ (p >> 2), y ^ ((p >> 1) & 1), c ^ (p & 1))) for p in range(1, N_DEVICES)]

    return _Job([slots], *_in_place([slots]), N_DEVICES - 1, build)


def _in_own_slot(packed, place, name):
    R, C = packed.shape

    def body(s_ref, p_ref, o_ref):
        o_ref[...] = p_ref[...]

    return _placed_call(body, name, place, (1,), [pl.BlockSpec((R, C), lambda i, s: (0, 0))],
                        pl.BlockSpec((None, R, C), lambda i, s: (2 * s[0] + s[1], 0, 0)),
                        jax.ShapeDtypeStruct((N_DEVICES, R, C), F32), [packed])


def _sum_slots(slots, name):
    n, R, C = slots.shape

    def body(s_ref, o_ref):
        acc = s_ref[0]
        for i in range(1, n):
            acc = acc + s_ref[i]
        o_ref[...] = acc

    return _call(body,name=name, grid=(1,), in_specs=[pl.BlockSpec((n, R, C), lambda i: (0, 0, 0))],
                 out_specs=pl.BlockSpec((R, C), lambda i: (0, 0)), out_shape=jax.ShapeDtypeStruct((R, C), F32),
                 sem=("arbitrary",))(slots)


def _packed_rows(size, width):
    return -(-size // (8 * width)) * 8


def _pack(arrays, width):
    rows = []
    for a in arrays:
        flat = a.reshape(-1)
        n_rows = _packed_rows(flat.shape[0], width)
        rows.append(jnp.pad(flat, (0, n_rows * width - flat.shape[0])).reshape(n_rows, width))
    return jnp.concatenate(rows, axis=0)


def _unpack(packed, shapes):
    out, r0, width = [], 0, packed.shape[1]
    for shape in shapes:
        size = 1
        for d in shape:
            size *= d
        out.append(packed[r0:r0 + _packed_rows(size, width)].reshape(-1)[:size].reshape(shape))
        r0 += _packed_rows(size, width)
    return out


class _Backlog:
    def __init__(self, first):
        self.now, self.free, self.flights, self.last, self.chain = 0.0, {"ici": 0.0, "d2d": 0.0}, [], first, []

    def run(self, fn, us, *args, **kw):
        out = fn(*args, **kw)
        self.now += us
        self.last = out[0] if isinstance(out, (list, tuple)) else out
        self.poll()
        return out

    def start(self, job, name, link, cost, done):
        flying = _start_job(job, name + "_start", self.chain)
        self.chain = [flying.token]
        _Behind.pending.append(flying.token)
        ends = max(self.now, self.free[link]) + cost
        self.free[link] = ends
        self.flights.append((ends + LANDING_SLACK_US, name, flying, done))
        self.flights.sort(key=lambda f: f[0])
        return flying

    def poll(self, block=False):
        while self.flights and (block or self.flights[0][0] <= self.now):
            ends, name, flying, done = self.flights.pop(0)
            self.now, block = max(self.now, ends), False
            done(*_wait_job(flying, name + "_wait", [self.last] + self.chain))


class _GatherStream:
    def __init__(self, backlog, bufs, kinds, costs, early=()):
        self.backlog, self.bufs, self.kinds, self.costs, self.begun, self.complete = backlog, bufs, kinds, costs, 0, set()
        self.early, self.relays, self.near_complete = set(early), {}, set()
        self.begin()

    def begin(self):
        u, self.begun = self.begun, self.begun + 1
        self.backlog.start(_gather_job(self.bufs[u], self.kinds[u], 1), "gather_%d" % u, "ici", 0.5 * self.costs[u],
                           lambda outs, kept: self.arrived(u, outs[0]))

    def arrived(self, u, buf):
        self.bufs[u] = buf
        if u in self.early:
            self.relays[u] = self.backlog.start(_gather_job(buf, self.kinds[u], "relay"), "relay_%d" % u, "ici",
                                                0.25 * self.costs[u], lambda outs, kept: self.relayed(u, outs[0]))
            self.backlog.start(_gather_job(self.relays[u].bufs[0], self.kinds[u], "direct"), "direct_%d" % u, "d2d",
                               0.5 * D2D_SHARE * self.costs[u], lambda outs, kept: self.near(u, outs[0]))
        else:
            self.backlog.start(_gather_job(buf, self.kinds[u], 2), "relay_%d" % u, "ici", 0.25 * self.costs[u],
                               lambda outs, kept: self.relayed(u, outs[0]))
        while self.begun <= min(u + GATHER_WINDOW[u], len(self.bufs) - 1):
            self.begin()

    def near(self, u, buf):
        self.bufs[u] = self.relays[u].bufs[0] = buf
        self.near_complete.add(u)

    def nearly_ready(self, u):
        while u not in self.near_complete:
            assert self.backlog.flights, "weight %d is not on its way" % u
            self.backlog.poll(block=True)
        return self.bufs[u]

    def relayed(self, u, buf):
        assert u not in self.early or u in self.near_complete, "the relay of weight %d is waited for before its sibling copies" % u
        self.bufs[u] = buf
        self.backlog.start(_gather_job(buf, self.kinds[u], 3), "handon_%d" % u, "d2d", D2D_SHARE * self.costs[u],
                           lambda outs, kept: self.handed(u, outs[0]))

    def handed(self, u, buf):
        self.bufs[u] = buf
        self.complete.add(u)

    def ready(self, u):
        while u not in self.complete:
            assert self.backlog.flights, "weight %d is not on its way" % u
            self.backlog.poll(block=True)
        return self.bufs[u]


class _GradStream:
    def __init__(self, backlog, u, name, kind, cost, g, place, results):
        self.backlog, self.u, self.name, self.kind, self.cost, self.place, self.results = backlog, u, name, kind, cost, place, results
        backlog.start(_exchange_halves_job([g], [kind]), "to_sibling_" + name, "d2d", D2D_SHARE * cost, self.exchanged)

    def exchanged(self, outs, kept):
        chip_sum = self.backlog.run(_add_pair, SIDE_KERNEL_US, kept[0], outs[0], self.kind, self.place, "chip_sum_" + self.name)
        self.backlog.start(_scatter_job(chip_sum, self.kind), "to_owners_" + self.name, "ici", self.cost, self.scattered)

    def scattered(self, outs, kept):
        reduced = self.backlog.run(_sum_chips, SIDE_KERNEL_US, kept[0], outs[0], self.kind, self.place, "reduce_" + self.name)
        self.backlog.start(_share_job([reduced]), "share_" + self.name, "d2d", D2D_SHARE * self.cost, self.shared)

    def shared(self, outs, kept):
        self.results[self.u] = outs[0]


SIDE_KERNEL_US = 12.0
D2D_SHARE = 0.15
LANDING_SLACK_US = 0.0
GATHER_WINDOW = (1, 1, 1, 2, 2, 1, 1, 1)


def kernel(x, mix_pre_g, mix_post_g, ffn_pre_g, ffn_post_g, ab_w_in, pool_w, pool_scale, conv_w, conv_b, conv_ln_g, conv_ln_b, ab_w_out, sc_w_in, sc_conv_w, sc_w_out, ffn_w1, ffn_w2, loss_target, m_mix_pre_g, m_mix_post_g, m_ffn_pre_g, m_ffn_post_g, m_ab_w_in, m_pool_w, m_pool_scale, m_conv_w, m_conv_b, m_conv_ln_g, m_conv_ln_b, m_ab_w_out, m_sc_w_in, m_sc_conv_w, m_sc_w_out, m_ffn_w1, m_ffn_w2, v_mix_pre_g, v_mix_post_g, v_ffn_pre_g, v_ffn_post_g, v_ab_w_in, v_pool_w, v_pool_scale, v_conv_w, v_conv_b, v_conv_ln_g, v_conv_ln_b, v_ab_w_out, v_sc_w_in, v_sc_conv_w, v_sc_w_out, v_ffn_w1, v_ffn_w2):
    x0, target = x[0], loss_target[0]
    T, D = x0.shape
    DP = pool_scale.shape[-1]
    gain = lambda g, layer: g[layer][None, :]

    big = [("ab_w_in", ab_w_in, 0, "col", 67.0), ("ab_w_out", ab_w_out, 0, "row", 44.0),
           ("ffn_w1_0", ffn_w1, 0, "col", 177.0), ("ffn_w2_0", ffn_w2, 0, "row", 177.0),
           ("sc_w_in", sc_w_in, 0, "col", 133.0), ("sc_w_out", sc_w_out, 0, "row", 44.0),
           ("ffn_w1_1", ffn_w1, 1, "col", 177.0), ("ffn_w2_1", ffn_w2, 1, "row", 177.0)]
    kinds = [b[3] for b in big]
    chip = 2 * lax.axis_index("x") + lax.axis_index("y")
    place = jnp.stack([chip, lax.axis_index("c")]).astype(jnp.int32)

    def own_in_zeros(shard, ax):
        full = jnp.zeros(tuple(d * N_CHIPS if i == ax else d for i, d in enumerate(shard.shape)), shard.dtype)
        return lax.dynamic_update_slice_in_dim(full, shard, chip * shard.shape[ax], axis=ax)

    W = [_cast_into_full(w, layer, kind, place, "cast_" + name) for name, w, layer, kind, _ in big]
    smalls = [own_in_zeros(pool_w[0], 1), own_in_zeros(conv_w[0], 1), own_in_zeros(sc_conv_w[0], 1)]
    backlog = _Backlog(x0)
    run = backlog.run
    small_weights = []
    backlog.start(_gather_small_job(smalls, [1, 1, 1]), "gather_small", "ici", 6.0, lambda outs, kept: small_weights.extend(outs))
    gather = _GatherStream(backlog, W, kinds, [b[4] for b in big], early=(3, 7))

    relu_sq = lambda acc: (jnp.maximum(acc, 0.0), jnp.square(jnp.maximum(acc, 0.0)))
    relu_sq_bwd = lambda acc, a: (acc * (2.0 * a.astype(F32)),)

    h0 = run(_norm_fwd, 12.0, x0, gain(mix_pre_g, 0), "norm_in")
    z0 = run(_matmul, 35.0, h0, gather.ready(0), "nn", "mix0_in")
    while not small_weights:
        backlog.poll(block=True)
    pool_w_full, conv_w_full, sc_conv_w_full = small_weights
    pooled, y_pool = run(_pool_fwd, 23.0, z0, pool_w_full, pool_scale, "pool_fwd")
    a_conv, c_conv = run(_conv_fwd, 25.0, z0, conv_w_full, conv_b, DP, "conv_fwd")
    y_conv = run(_ln_silu_fwd, 10.0, c_conv, conv_ln_g, conv_ln_b, "ln_silu_fwd")
    y0 = jnp.concatenate([y_pool, y_conv], axis=1)
    m0 = run(_matmul, 25.0, y0, gather.ready(1), "nn", "mix0_out")
    x1, h1 = run(_residual_norm, 21.0, x0, m0, gain(mix_post_g, 0), gain(ffn_pre_g, 0), "res_mix0")
    a0, a0sq = run(_matmul, 81.0, h1, gather.ready(2), "nn", "ffn0_up", out_dtypes=(BF16, BF16), epilogue=relu_sq)
    f0 = run(_matmul_by_chip_rows, 63.0, a0sq, gather.nearly_ready(3), place, "ffn0_down_near")
    f0 = run(_matmul_by_chip_rows, 21.0, a0sq, gather.ready(3), place, "ffn0_down", rest=f0)
    x2, h2 = run(_residual_norm, 22.0, x1, f0, gain(ffn_post_g, 0), gain(mix_pre_g, 1), "res_ffn0")
    z1 = run(_matmul, 62.0, h2, gather.ready(4), "nn", "mix1_in")
    y1 = run(_short_fwd, 22.0, z1, sc_conv_w_full, "short_fwd")
    m1 = run(_matmul, 25.0, y1, gather.ready(5), "nn", "mix1_out")
    x3, h3 = run(_residual_norm, 21.0, x2, m1, gain(mix_post_g, 1), gain(ffn_pre_g, 1), "res_mix1")
    a1, a1sq = run(_matmul, 81.0, h3, gather.ready(6), "nn", "ffn1_up", out_dtypes=(BF16, BF16), epilogue=relu_sq)
    f1 = run(_matmul_by_chip_rows, 63.0, a1sq, gather.nearly_ready(7), place, "ffn1_down_near")
    f1 = run(_matmul_by_chip_rows, 21.0, a1sq, gather.ready(7), place, "ffn1_down", rest=f1)
    w_in0, w_out0, w1_0, w2_0, w_in1, w_out1, w1_1, w2_1 = W

    grads_big = [None] * len(big)


    def reduce_grad(u, g):
        name, _, _, kind, cost = big[u]
        _GradStream(backlog, u, name, kind, cost, g, place, grads_big)

    dx, df1, d_ffn_post_1, loss_row = run(_loss_and_last_norm_bwd, 30.0, x3, f1, gain(ffn_post_g, 1), target, "loss")
    reduce_grad(7, run(_matmul, 80.0, a1sq, df1, "tn", "ffn1_down_dw", out_dtypes=(BF16,)))
    dz = run(_matmul, 82.0, df1, w2_1, "nt", "ffn1_down_dx", out_dtypes=(BF16,), epilogue=relu_sq_bwd, epi=(a1,))
    reduce_grad(6, run(_matmul, 80.0, h3, dz, "tn", "ffn1_up_dw", out_dtypes=(BF16,)))
    dh = run(_matmul, 87.0, dz, w1_1, "nt", "ffn1_up_dx", tk=LONG_K_TILE)
    dx, d_ffn_pre_1, dm1, d_mix_post_1 = run(_norms_bwd, 36.0, dx, dh, x3, gain(ffn_pre_g, 1), m1, gain(mix_post_g, 1), "norms_bwd3")

    reduce_grad(5, run(_matmul, 24.0, y1, dm1, "tn", "mix1_out_dw", out_dtypes=(BF16,)))
    dy1 = run(_matmul, 25.0, dm1, w_out1, "nt", "mix1_out_dx")
    dz1, d_sc_conv_w = run(_short_bwd, 41.0, dy1, z1, sc_conv_w_full, "short_bwd")
    reduce_grad(4, run(_matmul, 62.0, h2, dz1, "tn", "mix1_in_dw", out_dtypes=(BF16,), stacked=3))
    dh = run(_matmul, 68.0, dz1, w_in1, "nt", "mix1_in_dx", stacked=3)
    dx, d_mix_pre_1, df0, d_ffn_post_0 = run(_norms_bwd, 35.0, dx, dh, x2, gain(mix_pre_g, 1), f0, gain(ffn_post_g, 0), "norms_bwd2")

    reduce_grad(3, run(_matmul, 80.0, a0sq, df0, "tn", "ffn0_down_dw", out_dtypes=(BF16,)))
    dz = run(_matmul, 82.0, df0, w2_0, "nt", "ffn0_down_dx", out_dtypes=(BF16,), epilogue=relu_sq_bwd, epi=(a0,))
    reduce_grad(2, run(_matmul, 80.0, h1, dz, "tn", "ffn0_up_dw", out_dtypes=(BF16,)))
    dh = run(_matmul, 87.0, dz, w1_0, "nt", "ffn0_up_dx", tk=LONG_K_TILE)
    dx, d_ffn_pre_0, dm0, d_mix_post_0 = run(_norms_bwd, 36.0, dx, dh, x1, gain(ffn_pre_g, 0), m0, gain(mix_post_g, 0), "norms_bwd1")

    reduce_grad(1, run(_matmul, 24.0, y0, dm0, "tn", "mix0_out_dw", out_dtypes=(BF16,)))
    dy0 = run(_matmul, 25.0, dm0, w_out0, "nt", "mix0_out_dx")
    du_pool, d_pool_w, d_pool_scale = run(_pool_bwd, 28.0, dy0, pooled, pool_w_full, pool_scale, "pool_bwd")
    dc, d_ln_g, d_ln_b = run(_ln_silu_bwd, 15.0, dy0, c_conv, conv_ln_g, conv_ln_b, "ln_silu_bwd")
    dv, dgate, d_conv_w, d_conv_b = run(_conv_bwd, 52.0, dc, a_conv, z0, conv_w_full, DP, "conv_bwd")
    dz0 = jnp.concatenate([du_pool, dv, dgate], axis=1)

    small_sums = {}

    def exchange_small(key, arrays, cost):
        slots = _in_own_slot(_pack(arrays, D), place, "small_grads_slot_" + key)
        backlog.start(_small_exchange_job(slots), "small_grads_" + key, "ici", cost,
                      lambda outs, kept: small_sums.__setitem__(key, _unpack(_sum_slots(outs[0], "small_grads_sum_" + key),
                                                                             [a.shape for a in arrays])))

    exchange_small("most", [d_mix_pre_1, jnp.concatenate([d_mix_post_0, d_mix_post_1], 0),
                            jnp.concatenate([d_ffn_pre_0, d_ffn_pre_1], 0), jnp.concatenate([d_ffn_post_0, d_ffn_post_1], 0),
                            d_pool_scale, d_conv_b, d_ln_g, d_ln_b, d_pool_w, d_conv_w, d_sc_conv_w], 112.0)
    reduce_grad(0, run(_matmul, 34.0, h0, dz0, "tn", "mix0_in_dw", out_dtypes=(BF16,)))
    dh = run(_matmul, 40.0, dz0, w_in0, "nt", "mix0_in_dx")
    grad_x, d_mix_pre_0 = run(_norms_bwd, 26.0, dx, dh, x0, gain(mix_pre_g, 0), None, None, "norms_bwd0")
    exchange_small("last", [d_mix_pre_0, loss_row], 5.0)

    upd, gr, first = {}, grads_big, {}

    def keep(where, key, outs):
        where[key] = outs
        return outs

    adamw_big = [
        (7, lambda: keep(first, "ffn_w2", _adamw(ffn_w2, gr[7], m_ffn_w2, v_ffn_w2, "adamw_ffn_w2_1", layer=1)), 46.0),
        (6, lambda: keep(first, "ffn_w1", _adamw(ffn_w1, gr[6], m_ffn_w1, v_ffn_w1, "adamw_ffn_w1_1", layer=1)), 46.0),
        (5, lambda: keep(upd, "sc_w_out", _adamw(sc_w_out, gr[5], m_sc_w_out, v_sc_w_out, "adamw_sc_w_out")), 14.0),
        (4, lambda: keep(upd, "sc_w_in", _adamw(sc_w_in, gr[4], m_sc_w_in, v_sc_w_in, "adamw_sc_w_in")), 35.0),
        (3, lambda: keep(upd, "ffn_w2", _adamw(ffn_w2, gr[3], m_ffn_w2, v_ffn_w2, "adamw_ffn_w2_0", layer=0,
                                               carried=first["ffn_w2"])), 46.0),
        (2, lambda: keep(upd, "ffn_w1", _adamw(ffn_w1, gr[2], m_ffn_w1, v_ffn_w1, "adamw_ffn_w1_0", layer=0,
                                               carried=first["ffn_w1"])), 46.0),
        (1, lambda: keep(upd, "ab_w_out", _adamw(ab_w_out, gr[1], m_ab_w_out, v_ab_w_out, "adamw_ab_w_out")), 14.0),
        (0, lambda: keep(upd, "ab_w_in", _adamw(ab_w_in, gr[0], m_ab_w_in, v_ab_w_in, "adamw_ab_w_in")), 19.0)]
    while adamw_big or backlog.flights:
        due = [a for a in adamw_big if gr[a[0]] is not None]
        if due:
            adamw_big.remove(due[0])
            backlog.run(due[0][1], due[0][2])
        else:
            backlog.poll(block=True)

    (g_mix_pre_1, g_mix_post, g_ffn_pre, g_ffn_post, g_pool_scale, g_conv_b, g_ln_g, g_ln_b, g_pool_w_full, g_conv_w_full,
     g_sc_conv_w_full) = small_sums["most"]
    g_mix_pre = jnp.concatenate([small_sums["last"][0], g_mix_pre_1], 0)
    loss = small_sums["last"][1][0, 0]
    own = lambda a, ax: lax.dynamic_slice_in_dim(a, chip * (a.shape[ax] // N_CHIPS), a.shape[ax] // N_CHIPS, axis=ax)
    g_pool_w, g_conv_w, g_sc_conv_w = own(g_pool_w_full, 1), own(g_conv_w_full, 1), own(g_sc_conv_w_full, 1)

    def small_update(w, g, m, v, name):
        shape = w.shape
        as3 = lambda a: a.reshape((1, -1, shape[-1]))
        outs = _adamw(as3(w), g.reshape((-1, shape[-1])), as3(m), as3(v), "adamw_" + name)
        return [o.reshape(shape) for o in outs]

    upd["mix_pre_g"] = small_update(mix_pre_g, g_mix_pre, m_mix_pre_g, v_mix_pre_g, "mix_pre_g")
    upd["mix_post_g"] = small_update(mix_post_g, g_mix_post, m_mix_post_g, v_mix_post_g, "mix_post_g")
    upd["ffn_pre_g"] = small_update(ffn_pre_g, g_ffn_pre, m_ffn_pre_g, v_ffn_pre_g, "ffn_pre_g")
    upd["ffn_post_g"] = small_update(ffn_post_g, g_ffn_post, m_ffn_post_g, v_ffn_post_g, "ffn_post_g")
    upd["pool_w"] = small_update(pool_w, g_pool_w, m_pool_w, v_pool_w, "pool_w")
    upd["pool_scale"] = small_update(pool_scale, g_pool_scale, m_pool_scale, v_pool_scale, "pool_scale")
    upd["conv_w"] = small_update(conv_w, g_conv_w, m_conv_w, v_conv_w, "conv_w")
    upd["conv_b"] = small_update(conv_b, g_conv_b, m_conv_b, v_conv_b, "conv_b")
    upd["conv_ln_g"] = small_update(conv_ln_g, g_ln_g, m_conv_ln_g, v_conv_ln_g, "conv_ln_g")
    upd["conv_ln_b"] = small_update(conv_ln_b, g_ln_b, m_conv_ln_b, v_conv_ln_b, "conv_ln_b")
    upd["sc_conv_w"] = small_update(sc_conv_w, g_sc_conv_w, m_sc_conv_w, v_sc_conv_w, "sc_conv_w")

    order = ["mix_pre_g", "mix_post_g", "ffn_pre_g", "ffn_post_g", "ab_w_in", "pool_w", "pool_scale", "conv_w", "conv_b",
             "conv_ln_g", "conv_ln_b", "ab_w_out", "sc_w_in", "sc_conv_w", "sc_w_out", "ffn_w1", "ffn_w2"]
    out = [loss, grad_x[None]]
    for part in range(4):
        out += [upd[n][part] for n in order]
    return tuple(out)
```

```python
import jax
import jax.numpy as jnp
from jax import lax
from jax.experimental import pallas as pl
from jax.experimental.pallas import tpu as pltpu

F32, BF16 = jnp.float32, jnp.bfloat16
EPS = 1e-6
N_GROUPS = 4
MAX_WINDOW = 16
CONV_K = 31
SHORT_K = 3
CONV_PAD = 32
SHORT_PAD = 8
ADAM_LR, ADAM_B1, ADAM_B2, ADAM_EPS, ADAM_WD, ADAM_STEP = 0.001, 0.9, 0.999, 1e-08, 0.01, 10
N_CHIPS = 4
VMEM_LIMIT_BYTES = 56 * 1024 * 1024
ROW_TILE = 256
CHUNK = 256
LONG_K_TILE = 4096
MESH = pl.DeviceIdType.MESH
HBM = pl.BlockSpec(memory_space=pltpu.HBM)
SEM = pl.BlockSpec(memory_space=pltpu.SEMAPHORE)
ANY = pl.BlockSpec(memory_space=pl.ANY)


def _cp(*sem):
    return pltpu.CompilerParams(dimension_semantics=sem, vmem_limit_bytes=VMEM_LIMIT_BYTES)


def _sigmoid(v):
    return 1.0 / (1.0 + jnp.exp(-v))


class _Behind:
    pending = []


def _call(body, prefetch=None, **kw):
    behind, _Behind.pending = _Behind.pending, []
    single = not isinstance(kw["out_shape"], (list, tuple))
    in_specs, scratch = list(kw["in_specs"]), list(kw.get("scratch_shapes", ()))
    out_shape = [kw["out_shape"]] if single else list(kw["out_shape"])
    out_specs = [kw["out_specs"]] if single else list(kw["out_specs"])
    n_pre = 0 if prefetch is None else 1
    n_own, n_behind = len(in_specs), len(behind)

    def wrapped(*refs):
        body(*refs[:n_pre + n_own], *refs[n_pre + n_own + n_behind:])

    specs = dict(grid=kw["grid"], in_specs=in_specs + [ANY] * n_behind, out_specs=out_specs)
    if prefetch is None:
        specs["scratch_shapes"] = scratch
    else:
        specs = dict(grid_spec=pltpu.PrefetchScalarGridSpec(num_scalar_prefetch=1, scratch_shapes=scratch, **specs))
    aliases = {n_pre + i: o for i, o in kw.get("input_output_aliases", {}).items()}
    call = pl.pallas_call(wrapped, name=kw["name"], out_shape=out_shape, input_output_aliases=aliases,
                          compiler_params=_cp(*kw["sem"]), **specs)

    def run(*args):
        outs = call(*([prefetch] * n_pre), *args, *behind)
        return outs[0] if single else list(outs)

    return run


_DIMS = {"nn": (((1,), (0,)), ((), ())), "nt": (((1,), (1,)), ((), ())), "tn": (((0,), (0,)), ((), ()))}


def _pick(n, cap, step=256):
    if n <= cap:
        return n
    return next(t for t in range(cap - cap % step, 0, -step) if n % t == 0)


def _matmul(a, b, mode, name, out_dtypes=(F32,), epilogue=None, epi=(), stacked=0, tm=1024, tn=1024, tk=2048):
    if mode == "tn":
        (K, M), (K2, N) = a.shape, (b.shape if not stacked else (b.shape[1], stacked * b.shape[2]))
    elif mode == "nt":
        (M, K), (N, K2) = (a.shape if not stacked else (a.shape[1], stacked * a.shape[2])), b.shape
    else:
        (M, K), (K2, N) = a.shape, b.shape
    assert K == K2
    tm = _pick(M, tm)
    tn = _pick(N // stacked if stacked and mode == "tn" else N, tn)
    tk = _pick(K // stacked if stacked and mode == "nt" else K, tk)
    nk = K // tk
    a_spec = pl.BlockSpec((tk, tm), lambda i, j, k: (k, i)) if mode == "tn" else pl.BlockSpec((tm, tk), lambda i, j, k: (i, k))
    b_spec = pl.BlockSpec((tn, tk), lambda i, j, k: (j, k)) if mode == "nt" else pl.BlockSpec((tk, tn), lambda i, j, k: (k, j))
    if stacked and mode == "tn":
        per = N // stacked // tn
        b_spec = pl.BlockSpec((None, tk, tn), lambda i, j, k: (j // per, k, j % per))
    if stacked and mode == "nt":
        per = K // stacked // tk
        a_spec = pl.BlockSpec((None, tm, tk), lambda i, j, k: (k // per, i, k % per))
    o_spec = pl.BlockSpec((tm, tn), lambda i, j, k: (i, j))
    n_epi, n_out = len(epi), len(out_dtypes)

    def body(a_ref, b_ref, *rest):
        epi_refs, out_refs, scratch = rest[:n_epi], rest[n_epi:n_epi + n_out], rest[n_epi + n_out:]
        part = lax.dot_general(a_ref[...].astype(BF16), b_ref[...].astype(BF16), _DIMS[mode], preferred_element_type=F32)

        def finish(acc):
            outs = epilogue(acc, *[r[...] for r in epi_refs]) if epilogue else (acc,)
            for o_ref, o in zip(out_refs, outs):
                o_ref[...] = o.astype(o_ref.dtype)

        if nk == 1:
            finish(part)
        else:
            acc_ref = scratch[0]
            k = pl.program_id(2)

            @pl.when(k == 0)
            def _():
                acc_ref[...] = part

            @pl.when(k > 0)
            def _():
                acc_ref[...] += part

            @pl.when(k == nk - 1)
            def _():
                finish(acc_ref[...])

    outs = _call(
        body, name=name, grid=(M // tm, N // tn, nk),
        in_specs=[a_spec, b_spec] + [o_spec] * n_epi, out_specs=[o_spec] * n_out,
        out_shape=[jax.ShapeDtypeStruct((M, N), dt) for dt in out_dtypes],
        scratch_shapes=[pltpu.VMEM((tm, tn), F32)] if nk > 1 else [],
        sem=("parallel", "parallel", "arbitrary"))(a, b, *epi)
    return outs[0] if n_out == 1 else outs


def _matmul_by_chip_rows(a, b, place, name, rest=None, tm=1024, tn=1024):
    (M, K), N = a.shape, b.shape[1]
    tm, tn, tk = _pick(M, tm), _pick(N, tn), K // N_CHIPS
    nk = N_CHIPS - 1 if rest is None else 1

    def block(k, s):
        diagonal = N_CHIPS - 1 - s[0]
        return k + (k >= diagonal).astype(jnp.int32) if rest is None else diagonal

    def body(s_ref, a_ref, b_ref, *refs):
        o_ref, acc_ref = refs[-2:]
        part = jnp.dot(a_ref[...], b_ref[...], preferred_element_type=F32)
        k = pl.program_id(2)

        @pl.when(k == 0)
        def _():
            acc_ref[...] = part if rest is None else part + refs[0][...]

        @pl.when(k > 0)
        def _():
            acc_ref[...] += part

        @pl.when(k == nk - 1)
        def _():
            o_ref[...] = acc_ref[...]

    tile = pl.BlockSpec((tm, tn), lambda i, j, k, s: (i, j))
    in_specs = [pl.BlockSpec((tm, tk), lambda i, j, k, s: (i, block(k, s))), pl.BlockSpec((tk, tn), lambda i, j, k, s: (block(k, s), j))]
    return _call(body, prefetch=place, name=name, grid=(M // tm, N // tn, nk), in_specs=in_specs + ([] if rest is None else [tile]),
                 out_specs=tile, out_shape=jax.ShapeDtypeStruct((M, N), F32), scratch_shapes=[pltpu.VMEM((tm, tn), F32)],
                 sem=("parallel", "parallel", "arbitrary"))(a, b, *([] if rest is None else [rest]))


def _rms(x, g):
    r = lax.rsqrt(jnp.mean(x * x, axis=-1, keepdims=True) + EPS)
    return x * r * g


def _rms_bwd(dy, x, g):
    r = lax.rsqrt(jnp.mean(x * x, axis=-1, keepdims=True) + EPS)
    xn = x * r
    dyg = dy * g
    dx = r * (dyg - xn * jnp.mean(dyg * xn, axis=-1, keepdims=True))
    return dx, jnp.sum(dy * xn, axis=0, keepdims=True)


def _rows(d, tr=ROW_TILE):
    return pl.BlockSpec((tr, d), lambda i: (i, 0))


def _vec(d):
    return pl.BlockSpec((1, d), lambda i: (0, 0))


def _accumulate(ref, val):
    @pl.when(pl.program_id(0) == 0)
    def _():
        ref[...] = val

    @pl.when(pl.program_id(0) > 0)
    def _():
        ref[...] += val


def _norm_fwd(x, g, name):
    T, D = x.shape

    def body(x_ref, g_ref, h_ref):
        h_ref[...] = _rms(x_ref[...], g_ref[...]).astype(BF16)

    return _call(body,name=name,grid=(T // ROW_TILE,), in_specs=[_rows(D), _vec(D)], out_specs=_rows(D),
                          out_shape=jax.ShapeDtypeStruct((T, D), BF16), sem=("parallel",))(x, g)


def _residual_norm(x, m, g_post, g_next, name):
    T, D = x.shape

    def body(x_ref, m_ref, gp_ref, gn_ref, xo_ref, h_ref):
        xo = x_ref[...] + _rms(m_ref[...], gp_ref[...])
        xo_ref[...] = xo
        h_ref[...] = _rms(xo, gn_ref[...]).astype(BF16)

    return _call(body,name=name,grid=(T // ROW_TILE,), in_specs=[_rows(D), _rows(D), _vec(D), _vec(D)],
                          out_specs=[_rows(D), _rows(D)],
                          out_shape=[jax.ShapeDtypeStruct((T, D), F32), jax.ShapeDtypeStruct((T, D), BF16)],
                          sem=("parallel",))(x, m, g_post, g_next)


def _loss_and_last_norm_bwd(x, m, g_post, target, name):
    T, D = x.shape

    def body(x_ref, m_ref, gp_ref, t_ref, dx_ref, dm_ref, dg_ref, loss_ref):
        m_val, gp = m_ref[...], gp_ref[...]
        err = x_ref[...] + _rms(m_val, gp) - t_ref[...]
        dx = err * (1.0 / D)
        dx_ref[...] = dx
        dm, dg = _rms_bwd(dx, m_val, gp)
        dm_ref[...] = dm.astype(BF16)
        _accumulate(dg_ref, dg)
        _accumulate(loss_ref, jnp.full((1, 128), 0.5 * jnp.sum(err * err) * (1.0 / D), F32))

    return _call(body,name=name,grid=(T // ROW_TILE,), in_specs=[_rows(D), _rows(D), _vec(D), _rows(D)],
                          out_specs=[_rows(D), _rows(D), _vec(D), _vec(128)],
                          out_shape=[jax.ShapeDtypeStruct((T, D), F32), jax.ShapeDtypeStruct((T, D), BF16),
                                     jax.ShapeDtypeStruct((1, D), F32), jax.ShapeDtypeStruct((1, 128), F32)],
                          sem=("arbitrary",))(x, m, g_post, target)


def _norms_bwd(dx, dh, x_in, g_pre, m_prev, g_post_prev, name):
    T, D = dx.shape
    with_prev = m_prev is not None

    def body(*refs):
        if with_prev:
            dx_ref, dh_ref, x_ref, gq_ref, m_ref, gp_ref, dxo_ref, dgq_ref, dm_ref, dgp_ref = refs
        else:
            dx_ref, dh_ref, x_ref, gq_ref, dxo_ref, dgq_ref = refs
        d_in, dgq = _rms_bwd(dh_ref[...], x_ref[...], gq_ref[...])
        dxo = dx_ref[...] + d_in
        dxo_ref[...] = dxo
        _accumulate(dgq_ref, dgq)
        if with_prev:
            dm, dgp = _rms_bwd(dxo, m_ref[...], gp_ref[...])
            dm_ref[...] = dm.astype(BF16)
            _accumulate(dgp_ref, dgp)

    ins, in_specs = [dx, dh, x_in, g_pre], [_rows(D), _rows(D), _rows(D), _vec(D)]
    out_specs = [_rows(D), _vec(D)]
    out_shape = [jax.ShapeDtypeStruct((T, D), F32), jax.ShapeDtypeStruct((1, D), F32)]
    if with_prev:
        ins += [m_prev, g_post_prev]
        in_specs += [_rows(D), _vec(D)]
        out_specs += [_rows(D), _vec(D)]
        out_shape += [jax.ShapeDtypeStruct((T, D), BF16), jax.ShapeDtypeStruct((1, D), F32)]
    return _call(body,name=name,grid=(T // ROW_TILE,), in_specs=in_specs, out_specs=out_specs, out_shape=out_shape,
                          sem=("arbitrary",))(*ins)


def _window_weights(g):
    w = 2 << g
    return w, [jnp.where(j < w, 1.0, 0.0).astype(F32) for j in range(MAX_WINDOW)]


def _valid_count(r0, rows, w):
    t = (lax.broadcasted_iota(jnp.int32, (rows, 1), 0) + (r0 + 1)).astype(F32)
    return jnp.minimum(t, w.astype(F32))


def _pool_fwd(z, pool_w, pool_scale, name):
    T = z.shape[0]
    PG = pool_w.shape[-1]
    DP = N_GROUPS * PG
    rc = min(CHUNK, T)

    def body(u_ref, pw_ref, sc_ref, pooled_ref, y_ref, pad):
        w, wts = _window_weights(pl.program_id(0))
        pad[pl.ds(0, MAX_WINDOW), :] = jnp.zeros((MAX_WINDOW, PG), F32)
        pad[pl.ds(MAX_WINDOW, T), :] = u_ref[...]
        for r0 in range(0, T, rc):
            acc = jnp.zeros((rc, PG), F32)
            for j in range(MAX_WINDOW):
                acc = acc + wts[j] * pad[pl.ds(MAX_WINDOW + r0 - j, rc), :]
            pooled = acc / _valid_count(r0, rc, w) - u_ref[pl.ds(r0, rc), :]
            pooled_ref[pl.ds(r0, rc), :] = pooled.astype(BF16)
        mixed = jnp.dot(pooled_ref[...], pw_ref[...].astype(BF16), preferred_element_type=F32)
        y_ref[...] = (mixed * sc_ref[...]).astype(BF16)

    col = lambda g: (0, g)
    return _call(
        body, name=name,grid=(N_GROUPS,),
        in_specs=[pl.BlockSpec((T, PG), col), pl.BlockSpec((None, PG, PG), lambda g: (g, 0, 0)), pl.BlockSpec((1, PG), col)],
        out_specs=[pl.BlockSpec((T, PG), col), pl.BlockSpec((T, PG), col)],
        out_shape=[jax.ShapeDtypeStruct((T, DP), BF16), jax.ShapeDtypeStruct((T, DP), BF16)],
        scratch_shapes=[pltpu.VMEM((T + MAX_WINDOW, PG), F32)], sem=("parallel",))(z, pool_w, pool_scale)


def _pool_bwd(dy, pooled, pool_w, pool_scale, name):
    T = dy.shape[0]
    PG = pool_w.shape[-1]
    DP = N_GROUPS * PG
    rc = min(CHUNK, T)

    def body(dy_ref, pooled_ref, pw_ref, sc_ref, du_ref, dpw_ref, dsc_ref, pad, dp_ref):
        w, wts = _window_weights(pl.program_id(0))
        pooled_v, pw = pooled_ref[...], pw_ref[...].astype(BF16)
        dy_v = dy_ref[...]
        mixed = jnp.dot(pooled_v, pw, preferred_element_type=F32)
        dsc_ref[...] = jnp.sum(dy_v * mixed, axis=0, keepdims=True)
        dmixed = (dy_v * sc_ref[...]).astype(BF16)
        dpw_ref[...] = lax.dot_general(pooled_v, dmixed, _DIMS["tn"], preferred_element_type=F32)
        dp_ref[...] = lax.dot_general(dmixed, pw, _DIMS["nt"], preferred_element_type=F32)
        pad[pl.ds(T, MAX_WINDOW), :] = jnp.zeros((MAX_WINDOW, PG), F32)
        for r0 in range(0, T, rc):
            pad[pl.ds(r0, rc), :] = dp_ref[pl.ds(r0, rc), :] / _valid_count(r0, rc, w)
        for r0 in range(0, T, rc):
            acc = jnp.zeros((rc, PG), F32)
            for j in range(MAX_WINDOW):
                acc = acc + wts[j] * pad[pl.ds(r0 + j, rc), :]
            du_ref[pl.ds(r0, rc), :] = (acc - dp_ref[pl.ds(r0, rc), :]).astype(BF16)

    col = lambda g: (0, g)
    return _call(
        body, name=name,grid=(N_GROUPS,),
        in_specs=[pl.BlockSpec((T, PG), col), pl.BlockSpec((T, PG), col), pl.BlockSpec((None, PG, PG), lambda g: (g, 0, 0)),
                  pl.BlockSpec((1, PG), col)],
        out_specs=[pl.BlockSpec((T, PG), col), pl.BlockSpec((None, PG, PG), lambda g: (g, 0, 0)), pl.BlockSpec((1, PG), col)],
        out_shape=[jax.ShapeDtypeStruct((T, DP), BF16), jax.ShapeDtypeStruct((N_GROUPS, PG, PG), F32),
                   jax.ShapeDtypeStruct((1, DP), F32)],
        scratch_shapes=[pltpu.VMEM((T + MAX_WINDOW, PG), F32), pltpu.VMEM((T, PG), F32)],
        sem=("parallel",))(dy, pooled, pool_w, pool_scale)


def _conv_fwd(z, conv_w, conv_b, d_pool, name, tc=128):
    T = z.shape[0]
    DC = conv_w.shape[-1]
    rc = min(CHUNK, T)
    v0, g0 = d_pool // tc, (d_pool + DC) // tc

    def body(v_ref, gt_ref, w_ref, b_ref, a_ref, c_ref, pad):
        pad[pl.ds(0, CONV_PAD), :] = jnp.zeros((CONV_PAD, tc), F32)
        for r0 in range(0, T, rc):
            a = v_ref[pl.ds(r0, rc), :] * _sigmoid(gt_ref[pl.ds(r0, rc), :])
            a_ref[pl.ds(r0, rc), :] = a
            pad[pl.ds(CONV_PAD + r0, rc), :] = a
        for r0 in range(0, T, rc):
            acc = jnp.zeros((rc, tc), F32) + b_ref[...]
            for k in range(CONV_K):
                acc = acc + w_ref[pl.ds(k, 1), :] * pad[pl.ds(CONV_PAD - (CONV_K - 1) + k + r0, rc), :]
            c_ref[pl.ds(r0, rc), :] = acc

    col = lambda j: (0, j)
    return _call(
        body, name=name,grid=(DC // tc,),
        in_specs=[pl.BlockSpec((T, tc), lambda j: (0, v0 + j)), pl.BlockSpec((T, tc), lambda j: (0, g0 + j)),
                  pl.BlockSpec((CONV_K, tc), col), pl.BlockSpec((1, tc), col)],
        out_specs=[pl.BlockSpec((T, tc), col), pl.BlockSpec((T, tc), col)],
        out_shape=[jax.ShapeDtypeStruct((T, DC), F32), jax.ShapeDtypeStruct((T, DC), F32)],
        scratch_shapes=[pltpu.VMEM((T + CONV_PAD, tc), F32)], sem=("parallel",))(z, z, conv_w, conv_b)


def _conv_bwd(dc, a, z, conv_w, d_pool, du_pool, name, tc=128):
    T, DC = dc.shape
    assert du_pool.shape == (T, DC)
    rc = min(CHUNK, T)
    v0, g0 = d_pool // tc, (d_pool + DC) // tc

    def body(dc_ref, a_ref, v_ref, gt_ref, w_ref, du_ref, dz_ref, dw_ref, db_ref, apad, dpad):
        dz_ref[0] = du_ref[...]
        apad[pl.ds(0, CONV_PAD), :] = jnp.zeros((CONV_PAD, tc), F32)
        apad[pl.ds(CONV_PAD, T), :] = a_ref[...]
        dpad[pl.ds(0, T), :] = dc_ref[...]
        dpad[pl.ds(T, CONV_PAD), :] = jnp.zeros((CONV_PAD, tc), F32)
        db_ref[...] = jnp.sum(dc_ref[...], axis=0, keepdims=True)
        for k in range(CONV_K):
            acc = jnp.zeros((8, tc), F32)
            for r0 in range(0, T, rc):
                prod = dc_ref[pl.ds(r0, rc), :] * apad[pl.ds(CONV_PAD - (CONV_K - 1) + k + r0, rc), :]
                acc = acc + jnp.sum(prod.reshape(rc // 8, 8, tc), axis=0)
            dw_ref[pl.ds(k, 1), :] = jnp.sum(acc, axis=0, keepdims=True)
        for r0 in range(0, T, rc):
            da = jnp.zeros((rc, tc), F32)
            for k in range(CONV_K):
                da = da + w_ref[pl.ds(k, 1), :] * dpad[pl.ds(r0 + (CONV_K - 1) - k, rc), :]
            sig = _sigmoid(gt_ref[pl.ds(r0, rc), :])
            dz_ref[1, pl.ds(r0, rc), :] = (da * sig).astype(BF16)
            dz_ref[2, pl.ds(r0, rc), :] = (da * v_ref[pl.ds(r0, rc), :] * sig * (1.0 - sig)).astype(BF16)

    col = lambda j: (0, j)
    return _call(
        body, name=name,grid=(DC // tc,),
        in_specs=[pl.BlockSpec((T, tc), col), pl.BlockSpec((T, tc), col), pl.BlockSpec((T, tc), lambda j: (0, v0 + j)),
                  pl.BlockSpec((T, tc), lambda j: (0, g0 + j)), pl.BlockSpec((CONV_K, tc), col), pl.BlockSpec((T, tc), col)],
        out_specs=[pl.BlockSpec((3, T, tc), lambda j: (0, 0, j)), pl.BlockSpec((CONV_K, tc), col), pl.BlockSpec((1, tc), col)],
        out_shape=[jax.ShapeDtypeStruct((3, T, DC), BF16), jax.ShapeDtypeStruct((CONV_K, DC), F32),
                   jax.ShapeDtypeStruct((1, DC), F32)],
        scratch_shapes=[pltpu.VMEM((T + CONV_PAD, tc), F32), pltpu.VMEM((T + CONV_PAD, tc), F32)],
        sem=("parallel",))(dc, a, z, z, conv_w, du_pool)


def _layer_norm_parts(c, g, b):
    mu = jnp.mean(c, axis=-1, keepdims=True)
    xc = c - mu
    rstd = lax.rsqrt(jnp.mean(xc * xc, axis=-1, keepdims=True) + EPS)
    xhat = xc * rstd
    return xhat, rstd, xhat * g + b


def _ln_silu_fwd(c, g, b, name):
    T, DC = c.shape

    def body(c_ref, g_ref, b_ref, y_ref):
        _, _, ln = _layer_norm_parts(c_ref[...], g_ref[...], b_ref[...])
        y_ref[...] = (ln * _sigmoid(ln)).astype(BF16)

    return _call(body,name=name,grid=(T // ROW_TILE,), in_specs=[_rows(DC), _vec(DC), _vec(DC)], out_specs=_rows(DC),
                          out_shape=jax.ShapeDtypeStruct((T, DC), BF16), sem=("parallel",))(c, g, b)


def _ln_silu_bwd(dy, c, g, b, name):
    T, DC = c.shape

    def body(dy_ref, c_ref, g_ref, b_ref, dc_ref, dg_ref, db_ref):
        gain = g_ref[...]
        xhat, rstd, ln = _layer_norm_parts(c_ref[...], gain, b_ref[...])
        s = _sigmoid(ln)
        dln = dy_ref[...] * (s * (1.0 + ln * (1.0 - s)))
        _accumulate(dg_ref, jnp.sum(dln * xhat, axis=0, keepdims=True))
        _accumulate(db_ref, jnp.sum(dln, axis=0, keepdims=True))
        dxh = dln * gain
        dc_ref[...] = rstd * (dxh - jnp.mean(dxh, axis=-1, keepdims=True) - xhat * jnp.mean(dxh * xhat, axis=-1, keepdims=True))

    return _call(body,name=name,grid=(T // ROW_TILE,),
                          in_specs=[pl.BlockSpec((ROW_TILE, DC), lambda i: (i, 1)), _rows(DC), _vec(DC), _vec(DC)],
                          out_specs=[_rows(DC), _vec(DC), _vec(DC)],
                          out_shape=[jax.ShapeDtypeStruct((T, DC), F32), jax.ShapeDtypeStruct((1, DC), F32),
                                     jax.ShapeDtypeStruct((1, DC), F32)],
                          sem=("arbitrary",))(dy, c, g, b)


def _short_specs(T, DS, tc):
    n = DS // tc
    return [pl.BlockSpec((T, tc), lambda j: (0, j)), pl.BlockSpec((T, tc), lambda j: (0, n + j)),
            pl.BlockSpec((T, tc), lambda j: (0, 2 * n + j))]


def _short_fwd(z, w, name, tc=256):
    T = z.shape[0]
    DS = w.shape[-1]
    rc = min(CHUNK, T)

    def body(b_ref, cg_ref, u_ref, w_ref, y_ref, pad):
        pad[pl.ds(0, SHORT_PAD), :] = jnp.zeros((SHORT_PAD, tc), F32)
        pad[pl.ds(SHORT_PAD, T), :] = cg_ref[...] * u_ref[...]
        for r0 in range(0, T, rc):
            r = jnp.zeros((rc, tc), F32)
            for k in range(SHORT_K):
                r = r + w_ref[pl.ds(k, 1), :] * pad[pl.ds(SHORT_PAD - (SHORT_K - 1) + k + r0, rc), :]
            y_ref[pl.ds(r0, rc), :] = (b_ref[pl.ds(r0, rc), :] * r).astype(BF16)

    col = lambda j: (0, j)
    return _call(body,name=name,grid=(DS // tc,), in_specs=_short_specs(T, DS, tc) + [pl.BlockSpec((SHORT_K, tc), col)],
                          out_specs=pl.BlockSpec((T, tc), col), out_shape=jax.ShapeDtypeStruct((T, DS), BF16),
                          scratch_shapes=[pltpu.VMEM((T + SHORT_PAD, tc), F32)], sem=("parallel",))(z, z, z, w)


def _short_bwd(dy, z, w, name, tc=256):
    T, DS = dy.shape
    rc = min(CHUNK, T)

    def body(dy_ref, b_ref, cg_ref, u_ref, w_ref, dz_ref, dw_ref, qpad, rpad):
        qpad[pl.ds(0, SHORT_PAD), :] = jnp.zeros((SHORT_PAD, tc), F32)
        qpad[pl.ds(SHORT_PAD, T), :] = cg_ref[...] * u_ref[...]
        rpad[pl.ds(0, T), :] = dy_ref[...] * b_ref[...]
        rpad[pl.ds(T, SHORT_PAD), :] = jnp.zeros((SHORT_PAD, tc), F32)
        accs = [jnp.zeros((8, tc), F32) for _ in range(SHORT_K)]
        for r0 in range(0, T, rc):
            r = jnp.zeros((rc, tc), F32)
            dq = jnp.zeros((rc, tc), F32)
            dr = rpad[pl.ds(r0, rc), :]
            for k in range(SHORT_K):
                q_k = qpad[pl.ds(SHORT_PAD - (SHORT_K - 1) + k + r0, rc), :]
                r = r + w_ref[pl.ds(k, 1), :] * q_k
                dq = dq + w_ref[pl.ds(k, 1), :] * rpad[pl.ds(r0 + (SHORT_K - 1) - k, rc), :]
                accs[k] = accs[k] + jnp.sum((dr * q_k).reshape(rc // 8, 8, tc), axis=0)
            dz_ref[0, pl.ds(r0, rc), :] = (dy_ref[pl.ds(r0, rc), :] * r).astype(BF16)
            dz_ref[1, pl.ds(r0, rc), :] = (dq * u_ref[pl.ds(r0, rc), :]).astype(BF16)
            dz_ref[2, pl.ds(r0, rc), :] = (dq * cg_ref[pl.ds(r0, rc), :]).astype(BF16)
        for k in range(SHORT_K):
            dw_ref[pl.ds(k, 1), :] = jnp.sum(accs[k], axis=0, keepdims=True)

    col = lambda j: (0, j)
    tile = pl.BlockSpec((T, tc), col)
    return _call(body,name=name,grid=(DS // tc,),
                          in_specs=[tile] + _short_specs(T, DS, tc) + [pl.BlockSpec((SHORT_K, tc), col)],
                          out_specs=[pl.BlockSpec((3, T, tc), lambda j: (0, 0, j)), pl.BlockSpec((SHORT_K, tc), col)],
                          out_shape=[jax.ShapeDtypeStruct((3, T, DS), BF16), jax.ShapeDtypeStruct((SHORT_K, DS), F32)],
                          scratch_shapes=[pltpu.VMEM((T + SHORT_PAD, tc), F32), pltpu.VMEM((T + SHORT_PAD, tc), F32)],
                          sem=("parallel",))(dy, z, z, z, w)


def _tile_rows(rows, cols, n_bufs):
    budget = VMEM_LIMIT_BYTES * 3 // 4 // (2 * n_bufs * 4 * cols)
    tr = rows
    while tr > budget and tr % 16 == 0:
        tr //= 2
    return tr


def _placed_call(body, name, place, grid, in_specs, out_specs, out_shape, ins):
    return _call(body,prefetch=place, name=name, grid=grid, in_specs=in_specs, out_specs=out_specs, out_shape=out_shape,
                 sem=("parallel",))(*ins)


def _cast_into_full(w, layer, kind, place, name):
    _, R, C = w.shape
    tr = _tile_rows(R, C, 2)
    nb = R // tr
    if kind == "col":
        full, out_spec = (R, C * N_CHIPS), pl.BlockSpec((tr, C), lambda i, s: (i, s[0]))
    else:
        full, out_spec = (R * N_CHIPS, C), pl.BlockSpec((tr, C), lambda i, s: (s[0] * nb + i, 0))

    def body(s_ref, w_ref, o_ref):
        o_ref[...] = w_ref[...].astype(BF16)

    return _placed_call(body, name, place, (nb,), [pl.BlockSpec((None, tr, C), lambda i, s: (layer, i, 0))], out_spec,
                        jax.ShapeDtypeStruct(full, BF16), [w])


def _add_pair(grad, theirs, kind, place, name):
    R, C = grad.shape
    piece_rows = R // 2 if kind == "col" else R // N_CHIPS // 2
    tr = _tile_rows(piece_rows, C, 3)
    nb = piece_rows // tr
    if kind == "col":
        g_spec = pl.BlockSpec((tr, C), lambda i, s: (s[1] * nb + i, 0))
    else:
        g_spec = pl.BlockSpec((tr, C), lambda i, s: ((2 * (i // nb) + s[1]) * nb + i % nb, 0))
    flat = pl.BlockSpec((tr, C), lambda i, s: (i, 0))

    def body(s_ref, a_ref, b_ref, o_ref):
        o_ref[...] = (a_ref[...].astype(F32) + b_ref[...].astype(F32)).astype(BF16)

    return _placed_call(body, name, place, (R // 2 // tr,), [g_spec, flat], flat, jax.ShapeDtypeStruct((R // 2, C), BF16),
                        [grad, theirs])


def _sum_chips(chip_sum, arrived, kind, place, name):
    _, H, W = arrived.shape
    tr = _tile_rows(H, W, 6)
    nb = H // tr
    if kind == "col":
        own_spec = pl.BlockSpec((tr, W), lambda i, s: (i, s[0]))
    else:
        own_spec = pl.BlockSpec((tr, W), lambda i, s: (s[0] * nb + i, 0))

    def body(s_ref, p_ref, r_ref, o_ref):
        acc = p_ref[...].astype(F32)
        for i in range(N_CHIPS - 1):
            acc = acc + r_ref[i].astype(F32)
        o_ref[...] = acc

    return _placed_call(body, name, place, (nb,), [own_spec, pl.BlockSpec((N_CHIPS - 1, tr, W), lambda i, s: (0, i, 0))],
                        pl.BlockSpec((tr, W), lambda i, s: (s[1] * nb + i, 0)), jax.ShapeDtypeStruct((2 * H, W), F32),
                        [chip_sum, arrived])


def _adamw_values(w, g, m, v):
    m = ADAM_B1 * m + (1.0 - ADAM_B1) * g
    v = ADAM_B2 * v + (1.0 - ADAM_B2) * (g * g)
    m_hat = m / (1.0 - ADAM_B1 ** ADAM_STEP)
    v_hat = v / (1.0 - ADAM_B2 ** ADAM_STEP)
    return -ADAM_LR * (m_hat / (jnp.sqrt(v_hat) + ADAM_EPS) + ADAM_WD * w), m, v


def _adamw(w, g, m, v, name, layer=0, carried=None):
    L, R, C = w.shape
    tr = _tile_rows(R, C, 8)

    def body(w_ref, g_ref, m_ref, v_ref, *rest):
        go_ref, d_ref, mo_ref, vo_ref = rest[-4:]
        g_val = g_ref[...]
        d, m_new, v_new = _adamw_values(w_ref[...], g_val, m_ref[...], v_ref[...])
        go_ref[...], d_ref[...], mo_ref[...], vo_ref[...] = g_val, d, m_new, v_new

    lay = pl.BlockSpec((None, tr, C), lambda i: (layer, i, 0))
    ins = [w, g, m, v]
    in_specs = [lay, pl.BlockSpec((tr, C), lambda i: (i, 0)), lay, lay]
    aliases = {}
    if carried is not None:
        ins += list(carried)
        in_specs += [pl.BlockSpec(memory_space=pl.ANY)] * 4
        aliases = {4 + i: i for i in range(4)}
    return _call(body, name=name, grid=(R // tr,), in_specs=in_specs, out_specs=[lay] * 4,
                 out_shape=[jax.ShapeDtypeStruct((L, R, C), F32)] * 4, input_output_aliases=aliases, sem=("parallel",))(*ins)


def _aligned(v, m):
    return v if isinstance(v, int) else pl.multiple_of(v, m)


def _place():
    x, y, c = lax.axis_index("x"), lax.axis_index("y"), lax.axis_index("c")
    other_chips = [(x, 1 - y), (1 - x, y), (1 - x, 1 - y)]
    return x, y, c, 2 * x + y, other_chips


def _chip_index(chip):
    return 2 * chip[0] + chip[1]


def _piece(ref, kind, k, h):
    R, C = ref.shape
    if kind == "col":
        return ref.at[pl.ds(_aligned(h * (R // 2), 16), R // 2), pl.ds(_aligned(k * (C // N_CHIPS), 128), C // N_CHIPS)]
    rs = R // N_CHIPS
    return ref.at[pl.ds(_aligned(k * rs + h * (rs // 2), 16), rs // 2), :]


def _compact_piece(ref, kind, k):
    R2, C = ref.shape
    if kind == "col":
        return ref.at[:, pl.ds(_aligned(k * (C // N_CHIPS), 128), C // N_CHIPS)]
    return ref.at[pl.ds(_aligned(k * (R2 // N_CHIPS), 16), R2 // N_CHIPS), :]


def _half_rows(ref, h):
    R = ref.shape[0]
    return ref.at[pl.ds(_aligned(h * (R // 2), 16), R // 2), :]


class _Copies:
    def __init__(self, send_sems, recv_sems):
        self.send_sems, self.recv_sems = send_sems, recv_sems
        self.n_remote = 0

    def remote(self, src, dst, device):
        k = self.n_remote
        self.n_remote += 1
        return pltpu.make_async_remote_copy(src_ref=src, dst_ref=dst, send_sem=self.send_sems.at[k], recv_sem=self.recv_sems.at[k],
                                            device_id=device, device_id_type=MESH)


class _Job:
    def __init__(self, ins, out_shape, aliases, n_remote, build):
        self.ins, self.out_shape, self.aliases, self.n_remote, self.build = list(ins), list(out_shape), dict(aliases), n_remote, build


class _Flying:
    def __init__(self, job, send_sems, recv_sems, bufs, token):
        self.job, self.send_sems, self.recv_sems, self.bufs, self.token = job, send_sems, recv_sems, bufs, token


def _job_refs(job, buf_refs):
    n_out = len(job.out_shape)
    kept = [i for i in range(len(job.ins)) if i not in job.aliases]
    ins = [buf_refs[job.aliases[i]] if i in job.aliases else buf_refs[n_out + kept.index(i)] for i in range(len(job.ins))]
    return ins, list(buf_refs[:n_out])


def _start_job(job, name, after=()):
    n_in, n_out, n_after = len(job.ins), len(job.out_shape), len(after)
    kept = [i for i in range(n_in) if i not in job.aliases]
    n_bufs = n_out + len(kept)

    def body(*refs):
        in_refs, out_refs = refs[:n_in], refs[n_in + n_after:n_in + n_after + n_out]
        send_sems, recv_sems, token = refs[n_in + n_after + n_bufs:]
        for d in job.build(in_refs, out_refs, _Copies(send_sems, recv_sems)):
            d.start()
        token[...] = jnp.zeros_like(token)

    aliases = dict(job.aliases)
    aliases.update({i: n_out + k for k, i in enumerate(kept)})
    sems = pltpu.SemaphoreType.DMA((job.n_remote,))
    outs = pl.pallas_call(
        body, name=name, in_specs=[HBM] * n_in + [ANY] * n_after,
        out_specs=[HBM] * n_bufs + [SEM, SEM, pl.BlockSpec(memory_space=pltpu.VMEM)],
        out_shape=job.out_shape + [jax.ShapeDtypeStruct(job.ins[i].shape, job.ins[i].dtype) for i in kept]
        + [sems, sems, jax.ShapeDtypeStruct((8, 128), F32)],
        input_output_aliases=aliases,
        compiler_params=pltpu.CompilerParams(has_side_effects=pltpu.SideEffectType.DATAFLOW_SIDE_EFFECTING))(*job.ins, *after)
    return _Flying(job, outs[n_bufs], outs[n_bufs + 1], list(outs[:n_bufs]), outs[n_bufs + 2])


def _wait_job(flying, name, after=()):
    job, n_bufs, n_after = flying.job, len(flying.bufs), len(after)

    def body(*refs):
        in_refs, out_refs = _job_refs(job, refs[:n_bufs])
        send_sems, recv_sems = refs[n_bufs:n_bufs + 2]
        copies = job.build(in_refs, out_refs, _Copies(send_sems, recv_sems))
        for d in copies:
            d.wait_send()
        for d in copies:
            d.wait_recv()

    outs = pl.pallas_call(
        body, name=name, in_specs=[HBM] * n_bufs + [SEM, SEM] + [ANY] * n_after, out_specs=[HBM] * n_bufs,
        out_shape=[jax.ShapeDtypeStruct(b.shape, b.dtype) for b in flying.bufs],
        input_output_aliases={i: i for i in range(n_bufs)},
        compiler_params=pltpu.CompilerParams(has_side_effects=pltpu.SideEffectType.DATAFLOW_SIDE_EFFECTING))(
            *flying.bufs, flying.send_sems, flying.recv_sems, *after)
    return list(outs[:len(job.out_shape)]), list(outs[len(job.out_shape):])


def _in_place(arrays):
    return [jax.ShapeDtypeStruct(a.shape, a.dtype) for a in arrays], {u: u for u in range(len(arrays))}


def _rows_part(ref, part, n_parts):
    h = ref.shape[0] // n_parts
    return ref.at[pl.ds(part * h, h), :]


def _gather_job(full, kind, stage):
    def build(in_refs, out_refs, cp):
        x, y, c, me, (y_nbr, x_nbr, diagonal) = _place()
        (ref,) = out_refs
        sibling = (x, y, 1 - c)
        if stage == 1:
            mine = _piece(ref, kind, me, c)
            return [cp.remote(mine, mine, (*y_nbr, c)), cp.remote(mine, mine, (*x_nbr, c))]
        from_y, from_x = _piece(ref, kind, _chip_index(y_nbr), c), _piece(ref, kind, _chip_index(x_nbr), c)
        copies = []
        if stage in (2, "relay"):
            relay_0, relay_1 = _rows_part(from_x, 0, 2), _rows_part(from_y, 1, 2)
            copies += [cp.remote(relay_0, relay_0, (*y_nbr, c)), cp.remote(relay_1, relay_1, (*x_nbr, c))]
        if stage in (2, "direct"):
            copies += [cp.remote(from_y, from_y, sibling), cp.remote(from_x, from_x, sibling)]
        if stage == 3:
            from_diagonal = _piece(ref, kind, _chip_index(diagonal), c)
            copies.append(cp.remote(from_diagonal, from_diagonal, sibling))
        return copies

    return _Job([full], *_in_place([full]), {1: 2, 2: 4, "relay": 2, "direct": 2, 3: 1}[stage], build)


def _gather_small_job(fulls, axes):
    def build(in_refs, out_refs, cp):
        x, y, c, me, chips = _place()
        copies = []
        for ref, ax in zip(out_refs, axes):
            n = ref.shape[ax] // N_CHIPS
            idx = [slice(None)] * len(ref.shape)
            idx[ax] = pl.ds(_aligned(me * n, n), n)
            mine = ref.at[tuple(idx)]
            copies += [cp.remote(mine, mine, (*chip, c)) for chip in chips]
        return copies

    return _Job(fulls, *_in_place(fulls), 3 * len(fulls), build)


def _exchange_halves_job(grads, kinds):
    def build(in_refs, out_refs, cp):
        x, y, c, me, chips = _place()
        copies = []
        for src, dst, kind in zip(in_refs, out_refs, kinds):
            if kind == "col":
                copies.append(cp.remote(_half_rows(src, 1 - c), dst, (x, y, 1 - c)))
            else:
                copies += [cp.remote(_piece(src, "row", k, 1 - c), _compact_piece(dst, "row", k), (x, y, 1 - c))
                           for k in range(N_CHIPS)]
        return copies

    out_shape = [jax.ShapeDtypeStruct((g.shape[0] // 2, g.shape[1]), g.dtype) for g in grads]
    return _Job(grads, out_shape, {}, sum(1 if k == "col" else N_CHIPS for k in kinds), build)


def _scatter_job(half, kind):
    def build(in_refs, out_refs, cp):
        x, y, c, me, chips = _place()
        (src,), (dst,) = in_refs, out_refs
        return [cp.remote(_compact_piece(src, kind, _chip_index(chip)), dst.at[r], (*chip, c)) for r, chip in enumerate(chips)]

    part_shape = (half.shape[0], half.shape[1] // N_CHIPS) if kind == "col" else (half.shape[0] // N_CHIPS, half.shape[1])
    return _Job([half], [jax.ShapeDtypeStruct((N_CHIPS - 1,) + part_shape, half.dtype)], {}, N_CHIPS - 1, build)


def _share_job(shards):
    def build(in_refs, out_refs, cp):
        x, y, c, me, chips = _place()
        copies = []
        for ref in out_refs:
            mine = _half_rows(ref, c)
            copies.append(cp.remote(mine, mine, (x, y, 1 - c)))
        return copies

    return _Job(shards, *_in_place(shards), len(shards), build)


N_DEVICES = 2 * N_CHIPS


def _small_exchange_job(slots):
    def build(in_refs, out_refs, cp):
        x, y, c, me, chips = _place()
        (ref,) = out_refs
        mine = ref.at[2 * me + c]
        return [cp.remote(mine, mine, (x ^ (p >> 2), y ^ ((p >> 1) & 1), c ^ (p & 1))) for p in range(1, N_DEVICES)]

    return _Job([slots], *_in_place([slots]), N_DEVICES - 1, build)


def _in_own_slot(packed, place, name):
    R, C = packed.shape

    def body(s_ref, p_ref, o_ref):
        o_ref[...] = p_ref[...]

    return _placed_call(body, name, place, (1,), [pl.BlockSpec((R, C), lambda i, s: (0, 0))],
                        pl.BlockSpec((None, R, C), lambda i, s: (2 * s[0] + s[1], 0, 0)),
                        jax.ShapeDtypeStruct((N_DEVICES, R, C), F32), [packed])


def _sum_slots(slots, name):
    n, R, C = slots.shape

    def body(s_ref, o_ref):
        acc = s_ref[0]
        for i in range(1, n):
            acc = acc + s_ref[i]
        o_ref[...] = acc

    return _call(body,name=name, grid=(1,), in_specs=[pl.BlockSpec((n, R, C), lambda i: (0, 0, 0))],
                 out_specs=pl.BlockSpec((R, C), lambda i: (0, 0)), out_shape=jax.ShapeDtypeStruct((R, C), F32),
                 sem=("arbitrary",))(slots)


def _packed_rows(size, width):
    return -(-size // (8 * width)) * 8


def _pack(arrays, width):
    rows = []
    for a in arrays:
        flat = a.reshape(-1)
        n_rows = _packed_rows(flat.shape[0], width)
        rows.append(jnp.pad(flat, (0, n_rows * width - flat.shape[0])).reshape(n_rows, width))
    return jnp.concatenate(rows, axis=0)


def _unpack(packed, shapes):
    out, r0, width = [], 0, packed.shape[1]
    for shape in shapes:
        size = 1
        for d in shape:
            size *= d
        out.append(packed[r0:r0 + _packed_rows(size, width)].reshape(-1)[:size].reshape(shape))
        r0 += _packed_rows(size, width)
    return out


class _Backlog:
    def __init__(self, first):
        self.now, self.free, self.flights, self.last, self.chain = 0.0, {"ici": 0.0, "d2d": 0.0}, [], first, []

    def run(self, fn, us, *args, **kw):
        out = fn(*args, **kw)
        self.now += us
        self.last = out[0] if isinstance(out, (list, tuple)) else out
        self.poll()
        return out

    def start(self, job, name, link, cost, done):
        flying = _start_job(job, name + "_start", self.chain)
        self.chain = [flying.token]
        _Behind.pending.append(flying.token)
        ends = max(self.now, self.free[link]) + cost
        self.free[link] = ends
        self.flights.append((ends + LANDING_SLACK_US, name, flying, done))
        self.flights.sort(key=lambda f: f[0])
        return flying

    def poll(self, block=False):
        while self.flights and (block or self.flights[0][0] <= self.now):
            ends, name, flying, done = self.flights.pop(0)
            self.now, block = max(self.now, ends), False
            done(*_wait_job(flying, name + "_wait", [self.last] + self.chain))


class _GatherStream:
    def __init__(self, backlog, bufs, kinds, costs, early=()):
        self.backlog, self.bufs, self.kinds, self.costs, self.begun, self.complete = backlog, bufs, kinds, costs, 0, set()
        self.early, self.relays, self.near_complete = set(early), {}, set()
        self.begin()

    def begin(self):
        u, self.begun = self.begun, self.begun + 1
        self.backlog.start(_gather_job(self.bufs[u], self.kinds[u], 1), "gather_%d" % u, "ici", 0.5 * self.costs[u],
                           lambda outs, kept: self.arrived(u, outs[0]))

    def arrived(self, u, buf):
        self.bufs[u] = buf
        if u in self.early:
            self.relays[u] = self.backlog.start(_gather_job(buf, self.kinds[u], "relay"), "relay_%d" % u, "ici",
                                                0.25 * self.costs[u], lambda outs, kept: self.relayed(u, outs[0]))
            self.backlog.start(_gather_job(self.relays[u].bufs[0], self.kinds[u], "direct"), "direct_%d" % u, "d2d",
                               0.5 * D2D_SHARE * self.costs[u], lambda outs, kept: self.near(u, outs[0]))
        else:
            self.backlog.start(_gather_job(buf, self.kinds[u], 2), "relay_%d" % u, "ici", 0.25 * self.costs[u],
                               lambda outs, kept: self.relayed(u, outs[0]))
        while self.begun <= min(u + GATHER_WINDOW[u], len(self.bufs) - 1):
            self.begin()

    def near(self, u, buf):
        self.bufs[u] = self.relays[u].bufs[0] = buf
        self.near_complete.add(u)

    def nearly_ready(self, u):
        while u not in self.near_complete:
            assert self.backlog.flights, "weight %d is not on its way" % u
            self.backlog.poll(block=True)
        return self.bufs[u]

    def relayed(self, u, buf):
        assert u not in self.early or u in self.near_complete, "the relay of weight %d is waited for before its sibling copies" % u
        self.bufs[u] = buf
        self.backlog.start(_gather_job(buf, self.kinds[u], 3), "handon_%d" % u, "d2d", D2D_SHARE * self.costs[u],
                           lambda outs, kept: self.handed(u, outs[0]))

    def handed(self, u, buf):
        self.bufs[u] = buf
        self.complete.add(u)

    def ready(self, u):
        while u not in self.complete:
            assert self.backlog.flights, "weight %d is not on its way" % u
            self.backlog.poll(block=True)
        return self.bufs[u]


class _GradStream:
    def __init__(self, backlog, u, name, kind, cost, g, place, results):
        self.backlog, self.u, self.name, self.kind, self.cost, self.place, self.results = backlog, u, name, kind, cost, place, results
        backlog.start(_exchange_halves_job([g], [kind]), "to_sibling_" + name, "d2d", D2D_SHARE * cost, self.exchanged)

    def exchanged(self, outs, kept):
        chip_sum = self.backlog.run(_add_pair, SIDE_KERNEL_US, kept[0], outs[0], self.kind, self.place, "chip_sum_" + self.name)
        self.backlog.start(_scatter_job(chip_sum, self.kind), "to_owners_" + self.name, "ici", self.cost, self.scattered)

    def scattered(self, outs, kept):
        reduced = self.backlog.run(_sum_chips, SIDE_KERNEL_US, kept[0], outs[0], self.kind, self.place, "reduce_" + self.name)
        self.backlog.start(_share_job([reduced]), "share_" + self.name, "d2d", D2D_SHARE * self.cost, self.shared)

    def shared(self, outs, kept):
        self.results[self.u] = outs[0]


SIDE_KERNEL_US = 12.0
D2D_SHARE = 0.15
LANDING_SLACK_US = 0.0
GATHER_WINDOW = (1, 1, 1, 2, 2, 1, 1, 1)


def kernel(x, mix_pre_g, mix_post_g, ffn_pre_g, ffn_post_g, ab_w_in, pool_w, pool_scale, conv_w, conv_b, conv_ln_g, conv_ln_b, ab_w_out, sc_w_in, sc_conv_w, sc_w_out, ffn_w1, ffn_w2, loss_target, m_mix_pre_g, m_mix_post_g, m_ffn_pre_g, m_ffn_post_g, m_ab_w_in, m_pool_w, m_pool_scale, m_conv_w, m_conv_b, m_conv_ln_g, m_conv_ln_b, m_ab_w_out, m_sc_w_in, m_sc_conv_w, m_sc_w_out, m_ffn_w1, m_ffn_w2, v_mix_pre_g, v_mix_post_g, v_ffn_pre_g, v_ffn_post_g, v_ab_w_in, v_pool_w, v_pool_scale, v_conv_w, v_conv_b, v_conv_ln_g, v_conv_ln_b, v_ab_w_out, v_sc_w_in, v_sc_conv_w, v_sc_w_out, v_ffn_w1, v_ffn_w2):
    x0, target = x[0], loss_target[0]
    T, D = x0.shape
    DP = pool_scale.shape[-1]
    gain = lambda g, layer: g[layer][None, :]

    big = [("ab_w_in", ab_w_in, 0, "col", 67.0), ("ab_w_out", ab_w_out, 0, "row", 44.0),
           ("ffn_w1_0", ffn_w1, 0, "col", 177.0), ("ffn_w2_0", ffn_w2, 0, "row", 177.0),
           ("sc_w_in", sc_w_in, 0, "col", 133.0), ("sc_w_out", sc_w_out, 0, "row", 44.0),
           ("ffn_w1_1", ffn_w1, 1, "col", 177.0), ("ffn_w2_1", ffn_w2, 1, "row", 177.0)]
    kinds = [b[3] for b in big]
    chip = 2 * lax.axis_index("x") + lax.axis_index("y")
    place = jnp.stack([chip, lax.axis_index("c")]).astype(jnp.int32)

    def own_in_zeros(shard, ax):
        full = jnp.zeros(tuple(d * N_CHIPS if i == ax else d for i, d in enumerate(shard.shape)), shard.dtype)
        return lax.dynamic_update_slice_in_dim(full, shard, chip * shard.shape[ax], axis=ax)

    W = [_cast_into_full(w, layer, kind, place, "cast_" + name) for name, w, layer, kind, _ in big]
    smalls = [own_in_zeros(pool_w[0], 1), own_in_zeros(conv_w[0], 1), own_in_zeros(sc_conv_w[0], 1)]
    backlog = _Backlog(x0)
    run = backlog.run
    small_weights = []
    backlog.start(_gather_small_job(smalls, [1, 1, 1]), "gather_small", "ici", 6.0, lambda outs, kept: small_weights.extend(outs))
    gather = _GatherStream(backlog, W, kinds, [b[4] for b in big], early=(3, 7))

    relu_sq = lambda acc: (jnp.maximum(acc, 0.0), jnp.square(jnp.maximum(acc, 0.0)))
    relu_sq_bwd = lambda acc, a: (acc * (2.0 * a.astype(F32)),)

    h0 = run(_norm_fwd, 12.0, x0, gain(mix_pre_g, 0), "norm_in")
    z0 = run(_matmul, 35.0, h0, gather.ready(0), "nn", "mix0_in")
    while not small_weights:
        backlog.poll(block=True)
    pool_w_full, conv_w_full, sc_conv_w_full = small_weights
    pooled, y_pool = run(_pool_fwd, 23.0, z0, pool_w_full, pool_scale, "pool_fwd")
    a_conv, c_conv = run(_conv_fwd, 25.0, z0, conv_w_full, conv_b, DP, "conv_fwd")
    y_conv = run(_ln_silu_fwd, 10.0, c_conv, conv_ln_g, conv_ln_b, "ln_silu_fwd")
    y0 = jnp.concatenate([y_pool, y_conv], axis=1)
    m0 = run(_matmul, 25.0, y0, gather.ready(1), "nn", "mix0_out")
    x1, h1 = run(_residual_norm, 21.0, x0, m0, gain(mix_post_g, 0), gain(ffn_pre_g, 0), "res_mix0")
    a0, a0sq = run(_matmul, 81.0, h1, gather.ready(2), "nn", "ffn0_up", out_dtypes=(BF16, BF16), epilogue=relu_sq)
    f0 = run(_matmul_by_chip_rows, 63.0, a0sq, gather.nearly_ready(3), place, "ffn0_down_near")
    f0 = run(_matmul_by_chip_rows, 21.0, a0sq, gather.ready(3), place, "ffn0_down", rest=f0)
    x2, h2 = run(_residual_norm, 22.0, x1, f0, gain(ffn_post_g, 0), gain(mix_pre_g, 1), "res_ffn0")
    z1 = run(_matmul, 62.0, h2, gather.ready(4), "nn", "mix1_in")
    y1 = run(_short_fwd, 22.0, z1, sc_conv_w_full, "short_fwd")
    m1 = run(_matmul, 25.0, y1, gather.ready(5), "nn", "mix1_out")
    x3, h3 = run(_residual_norm, 21.0, x2, m1, gain(mix_post_g, 1), gain(ffn_pre_g, 1), "res_mix1")
    a1, a1sq = run(_matmul, 81.0, h3, gather.ready(6), "nn", "ffn1_up", out_dtypes=(BF16, BF16), epilogue=relu_sq)
    f1 = run(_matmul_by_chip_rows, 63.0, a1sq, gather.nearly_ready(7), place, "ffn1_down_near")
    f1 = run(_matmul_by_chip_rows, 21.0, a1sq, gather.ready(7), place, "ffn1_down", rest=f1)
    w_in0, w_out0, w1_0, w2_0, w_in1, w_out1, w1_1, w2_1 = W

    grads_big = [None] * len(big)


    def reduce_grad(u, g):
        name, _, _, kind, cost = big[u]
        _GradStream(backlog, u, name, kind, cost, g, place, grads_big)

    dx, df1, d_ffn_post_1, loss_row = run(_loss_and_last_norm_bwd, 30.0, x3, f1, gain(ffn_post_g, 1), target, "loss")
    reduce_grad(7, run(_matmul, 80.0, a1sq, df1, "tn", "ffn1_down_dw", out_dtypes=(BF16,)))
    dz = run(_matmul, 82.0, df1, w2_1, "nt", "ffn1_down_dx", out_dtypes=(BF16,), epilogue=relu_sq_bwd, epi=(a1,))
    reduce_grad(6, run(_matmul, 80.0, h3, dz, "tn", "ffn1_up_dw", out_dtypes=(BF16,)))
    dh = run(_matmul, 87.0, dz, w1_1, "nt", "ffn1_up_dx", tk=LONG_K_TILE)
    dx, d_ffn_pre_1, dm1, d_mix_post_1 = run(_norms_bwd, 36.0, dx, dh, x3, gain(ffn_pre_g, 1), m1, gain(mix_post_g, 1), "norms_bwd3")

    reduce_grad(5, run(_matmul, 24.0, y1, dm1, "tn", "mix1_out_dw", out_dtypes=(BF16,)))
    dy1 = run(_matmul, 25.0, dm1, w_out1, "nt", "mix1_out_dx")
    dz1, d_sc_conv_w = run(_short_bwd, 41.0, dy1, z1, sc_conv_w_full, "short_bwd")
    reduce_grad(4, run(_matmul, 62.0, h2, dz1, "tn", "mix1_in_dw", out_dtypes=(BF16,), stacked=3))
    dh = run(_matmul, 68.0, dz1, w_in1, "nt", "mix1_in_dx", stacked=3)
    dx, d_mix_pre_1, df0, d_ffn_post_0 = run(_norms_bwd, 35.0, dx, dh, x2, gain(mix_pre_g, 1), f0, gain(ffn_post_g, 0), "norms_bwd2")

    reduce_grad(3, run(_matmul, 80.0, a0sq, df0, "tn", "ffn0_down_dw", out_dtypes=(BF16,)))
    dz = run(_matmul, 82.0, df0, w2_0, "nt", "ffn0_down_dx", out_dtypes=(BF16,), epilogue=relu_sq_bwd, epi=(a0,))
    reduce_grad(2, run(_matmul, 80.0, h1, dz, "tn", "ffn0_up_dw", out_dtypes=(BF16,)))
    dh = run(_matmul, 87.0, dz, w1_0, "nt", "ffn0_up_dx", tk=LONG_K_TILE)
    dx, d_ffn_pre_0, dm0, d_mix_post_0 = run(_norms_bwd, 36.0, dx, dh, x1, gain(ffn_pre_g, 0), m0, gain(mix_post_g, 0), "norms_bwd1")

    reduce_grad(1, run(_matmul, 24.0, y0, dm0, "tn", "mix0_out_dw", out_dtypes=(BF16,)))
    dy0 = run(_matmul, 25.0, dm0, w_out0, "nt", "mix0_out_dx")
    du_pool, d_pool_w, d_pool_scale = run(_pool_bwd, 28.0, dy0, pooled, pool_w_full, pool_scale, "pool_bwd")
    dc, d_ln_g, d_ln_b = run(_ln_silu_bwd, 15.0, dy0, c_conv, conv_ln_g, conv_ln_b, "ln_silu_bwd")
    dz0, d_conv_w, d_conv_b = run(_conv_bwd, 52.0, dc, a_conv, z0, conv_w_full, DP, du_pool, "conv_bwd")

    small_sums = {}

    def exchange_small(key, arrays, cost):
        slots = _in_own_slot(_pack(arrays, D), place, "small_grads_slot_" + key)
        backlog.start(_small_exchange_job(slots), "small_grads_" + key, "ici", cost,
                      lambda outs, kept: small_sums.__setitem__(key, _unpack(_sum_slots(outs[0], "small_grads_sum_" + key),
                                                                             [a.shape for a in arrays])))

    exchange_small("most", [d_mix_pre_1, jnp.concatenate([d_mix_post_0, d_mix_post_1], 0),
                            jnp.concatenate([d_ffn_pre_0, d_ffn_pre_1], 0), jnp.concatenate([d_ffn_post_0, d_ffn_post_1], 0),
                            d_pool_scale, d_conv_b, d_ln_g, d_ln_b, d_pool_w, d_conv_w, d_sc_conv_w], 112.0)
    reduce_grad(0, run(_matmul, 34.0, h0, dz0, "tn", "mix0_in_dw", out_dtypes=(BF16,), stacked=3))
    dh = run(_matmul, 40.0, dz0, w_in0, "nt", "mix0_in_dx", stacked=3)
    grad_x, d_mix_pre_0 = run(_norms_bwd, 26.0, dx, dh, x0, gain(mix_pre_g, 0), None, None, "norms_bwd0")
    exchange_small("last", [d_mix_pre_0, loss_row], 5.0)

    upd, gr, first = {}, grads_big, {}

    def keep(where, key, outs):
        where[key] = outs
        return outs

    adamw_big = [
        (7, lambda: keep(first, "ffn_w2", _adamw(ffn_w2, gr[7], m_ffn_w2, v_ffn_w2, "adamw_ffn_w2_1", layer=1)), 46.0),
        (6, lambda: keep(first, "ffn_w1", _adamw(ffn_w1, gr[6], m_ffn_w1, v_ffn_w1, "adamw_ffn_w1_1", layer=1)), 46.0),
        (5, lambda: keep(upd, "sc_w_out", _adamw(sc_w_out, gr[5], m_sc_w_out, v_sc_w_out, "adamw_sc_w_out")), 14.0),
        (4, lambda: keep(upd, "sc_w_in", _adamw(sc_w_in, gr[4], m_sc_w_in, v_sc_w_in, "adamw_sc_w_in")), 35.0),
        (3, lambda: keep(upd, "ffn_w2", _adamw(ffn_w2, gr[3], m_ffn_w2, v_ffn_w2, "adamw_ffn_w2_0", layer=0,
                                               carried=first["ffn_w2"])), 46.0),
        (2, lambda: keep(upd, "ffn_w1", _adamw(ffn_w1, gr[2], m_ffn_w1, v_ffn_w1, "adamw_ffn_w1_0", layer=0,
                                               carried=first["ffn_w1"])), 46.0),
        (1, lambda: keep(upd, "ab_w_out", _adamw(ab_w_out, gr[1], m_ab_w_out, v_ab_w_out, "adamw_ab_w_out")), 14.0),
        (0, lambda: keep(upd, "ab_w_in", _adamw(ab_w_in, gr[0], m_ab_w_in, v_ab_w_in, "adamw_ab_w_in")), 19.0)]
    while adamw_big or backlog.flights:
        due = [a for a in adamw_big if gr[a[0]] is not None]
        if due:
            adamw_big.remove(due[0])
            backlog.run(due[0][1], due[0][2])
        else:
            backlog.poll(block=True)

    (g_mix_pre_1, g_mix_post, g_ffn_pre, g_ffn_post, g_pool_scale, g_conv_b, g_ln_g, g_ln_b, g_pool_w_full, g_conv_w_full,
     g_sc_conv_w_full) = small_sums["most"]
    g_mix_pre = jnp.concatenate([small_sums["last"][0], g_mix_pre_1], 0)
    loss = small_sums["last"][1][0, 0]
    own = lambda a, ax: lax.dynamic_slice_in_dim(a, chip * (a.shape[ax] // N_CHIPS), a.shape[ax] // N_CHIPS, axis=ax)
    g_pool_w, g_conv_w, g_sc_conv_w = own(g_pool_w_full, 1), own(g_conv_w_full, 1), own(g_sc_conv_w_full, 1)

    def small_update(w, g, m, v, name):
        shape = w.shape
        as3 = lambda a: a.reshape((1, -1, shape[-1]))
        outs = _adamw(as3(w), g.reshape((-1, shape[-1])), as3(m), as3(v), "adamw_" + name)
        return [o.reshape(shape) for o in outs]

    upd["mix_pre_g"] = small_update(mix_pre_g, g_mix_pre, m_mix_pre_g, v_mix_pre_g, "mix_pre_g")
    upd["mix_post_g"] = small_update(mix_post_g, g_mix_post, m_mix_post_g, v_mix_post_g, "mix_post_g")
    upd["ffn_pre_g"] = small_update(ffn_pre_g, g_ffn_pre, m_ffn_pre_g, v_ffn_pre_g, "ffn_pre_g")
    upd["ffn_post_g"] = small_update(ffn_post_g, g_ffn_post, m_ffn_post_g, v_ffn_post_g, "ffn_post_g")
    upd["pool_w"] = small_update(pool_w, g_pool_w, m_pool_w, v_pool_w, "pool_w")
    upd["pool_scale"] = small_update(pool_scale, g_pool_scale, m_pool_scale, v_pool_scale, "pool_scale")
    upd["conv_w"] = small_update(conv_w, g_conv_w, m_conv_w, v_conv_w, "conv_w")
    upd["conv_b"] = small_update(conv_b, g_conv_b, m_conv_b, v_conv_b, "conv_b")
    upd["conv_ln_g"] = small_update(conv_ln_g, g_ln_g, m_conv_ln_g, v_conv_ln_g, "conv_ln_g")
    upd["conv_ln_b"] = small_update(conv_ln_b, g_ln_b, m_conv_ln_b, v_conv_ln_b, "conv_ln_b")
    upd["sc_conv_w"] = small_update(sc_conv_w, g_sc_conv_w, m_sc_conv_w, v_sc_conv_w, "sc_conv_w")

    order = ["mix_pre_g", "mix_post_g", "ffn_pre_g", "ffn_post_g", "ab_w_in", "pool_w", "pool_scale", "conv_w", "conv_b",
             "conv_ln_g", "conv_ln_b", "ab_w_out", "sc_w_in", "sc_conv_w", "sc_w_out", "ffn_w1", "ffn_w2"]
    out = [loss, grad_x[None]]
    for part in range(4):
        out += [upd[n][part] for n in order]
    return tuple(out)
```

```python
import jax
import jax.numpy as jnp
from jax import lax
from jax.experimental import pallas as pl
from jax.experimental.pallas import tpu as pltpu

F32, BF16 = jnp.float32, jnp.bfloat16
EPS = 1e-6
N_GROUPS = 4
MAX_WINDOW = 16
CONV_K = 31
SHORT_K = 3
CONV_PAD = 32
SHORT_PAD = 8
ADAM_LR, ADAM_B1, ADAM_B2, ADAM_EPS, ADAM_WD, ADAM_STEP = 0.001, 0.9, 0.999, 1e-08, 0.01, 10
N_CHIPS = 4
VMEM_LIMIT_BYTES = 56 * 1024 * 1024
ROW_TILE = 256
CHUNK = 256
LONG_K_TILE = 4096
MESH = pl.DeviceIdType.MESH
HBM = pl.BlockSpec(memory_space=pltpu.HBM)
SEM = pl.BlockSpec(memory_space=pltpu.SEMAPHORE)
ANY = pl.BlockSpec(memory_space=pl.ANY)


def _cp(*sem):
    return pltpu.CompilerParams(dimension_semantics=sem, vmem_limit_bytes=VMEM_LIMIT_BYTES)


def _sigmoid(v):
    return 1.0 / (1.0 + jnp.exp(-v))


class _Behind:
    pending = []


def _call(body, prefetch=None, **kw):
    behind, _Behind.pending = _Behind.pending, []
    single = not isinstance(kw["out_shape"], (list, tuple))
    in_specs, scratch = list(kw["in_specs"]), list(kw.get("scratch_shapes", ()))
    out_shape = [kw["out_shape"]] if single else list(kw["out_shape"])
    out_specs = [kw["out_specs"]] if single else list(kw["out_specs"])
    n_pre = 0 if prefetch is None else 1
    n_own, n_behind = len(in_specs), len(behind)

    def wrapped(*refs):
        body(*refs[:n_pre + n_own], *refs[n_pre + n_own + n_behind:])

    specs = dict(grid=kw["grid"], in_specs=in_specs + [ANY] * n_behind, out_specs=out_specs)
    if prefetch is None:
        specs["scratch_shapes"] = scratch
    else:
        specs = dict(grid_spec=pltpu.PrefetchScalarGridSpec(num_scalar_prefetch=1, scratch_shapes=scratch, **specs))
    aliases = {n_pre + i: o for i, o in kw.get("input_output_aliases", {}).items()}
    call = pl.pallas_call(wrapped, name=kw["name"], out_shape=out_shape, input_output_aliases=aliases,
                          compiler_params=_cp(*kw["sem"]), **specs)

    def run(*args):
        outs = call(*([prefetch] * n_pre), *args, *behind)
        return outs[0] if single else list(outs)

    return run


_DIMS = {"nn": (((1,), (0,)), ((), ())), "nt": (((1,), (1,)), ((), ())), "tn": (((0,), (0,)), ((), ()))}


def _pick(n, cap, step=256):
    if n <= cap:
        return n
    return next(t for t in range(cap - cap % step, 0, -step) if n % t == 0)


def _matmul(a, b, mode, name, out_dtypes=(F32,), epilogue=None, epi=(), stacked=0, tm=1024, tn=1024, tk=2048):
    if mode == "tn":
        (K, M), (K2, N) = a.shape, (b.shape if not stacked else (b.shape[1], stacked * b.shape[2]))
    elif mode == "nt":
        (M, K), (N, K2) = (a.shape if not stacked else (a.shape[1], stacked * a.shape[2])), b.shape
    else:
        (M, K), (K2, N) = a.shape, b.shape
    assert K == K2
    tm = _pick(M, tm)
    tn = _pick(N // stacked if stacked and mode == "tn" else N, tn)
    tk = _pick(K // stacked if stacked and mode == "nt" else K, tk)
    nk = K // tk
    a_spec = pl.BlockSpec((tk, tm), lambda i, j, k: (k, i)) if mode == "tn" else pl.BlockSpec((tm, tk), lambda i, j, k: (i, k))
    b_spec = pl.BlockSpec((tn, tk), lambda i, j, k: (j, k)) if mode == "nt" else pl.BlockSpec((tk, tn), lambda i, j, k: (k, j))
    if stacked and mode == "tn":
        per = N // stacked // tn
        b_spec = pl.BlockSpec((None, tk, tn), lambda i, j, k: (j // per, k, j % per))
    if stacked and mode == "nt":
        per = K // stacked // tk
        a_spec = pl.BlockSpec((None, tm, tk), lambda i, j, k: (k // per, i, k % per))
    o_spec = pl.BlockSpec((tm, tn), lambda i, j, k: (i, j))
    n_epi, n_out = len(epi), len(out_dtypes)

    def body(a_ref, b_ref, *rest):
        epi_refs, out_refs, scratch = rest[:n_epi], rest[n_epi:n_epi + n_out], rest[n_epi + n_out:]
        part = lax.dot_general(a_ref[...].astype(BF16), b_ref[...].astype(BF16), _DIMS[mode], preferred_element_type=F32)

        def finish(acc):
            outs = epilogue(acc, *[r[...] for r in epi_refs]) if epilogue else (acc,)
            for o_ref, o in zip(out_refs, outs):
                o_ref[...] = o.astype(o_ref.dtype)

        if nk == 1:
            finish(part)
        else:
            acc_ref = scratch[0]
            k = pl.program_id(2)

            @pl.when(k == 0)
            def _():
                acc_ref[...] = part

            @pl.when(k > 0)
            def _():
                acc_ref[...] += part

            @pl.when(k == nk - 1)
            def _():
                finish(acc_ref[...])

    outs = _call(
        body, name=name, grid=(M // tm, N // tn, nk),
        in_specs=[a_spec, b_spec] + [o_spec] * n_epi, out_specs=[o_spec] * n_out,
        out_shape=[jax.ShapeDtypeStruct((M, N), dt) for dt in out_dtypes],
        scratch_shapes=[pltpu.VMEM((tm, tn), F32)] if nk > 1 else [],
        sem=("parallel", "parallel", "arbitrary"))(a, b, *epi)
    return outs[0] if n_out == 1 else outs


def _matmul_by_chip_rows(a, b, place, name, rest=None, tm=1024, tn=1024):
    (M, K), N = a.shape, b.shape[1]
    tm, tn, tk = _pick(M, tm), _pick(N, tn), K // N_CHIPS
    nk = N_CHIPS - 1 if rest is None else 1

    def block(k, s):
        diagonal = N_CHIPS - 1 - s[0]
        return k + (k >= diagonal).astype(jnp.int32) if rest is None else diagonal

    def body(s_ref, a_ref, b_ref, *refs):
        o_ref, acc_ref = refs[-2:]
        part = jnp.dot(a_ref[...], b_ref[...], preferred_element_type=F32)
        k = pl.program_id(2)

        @pl.when(k == 0)
        def _():
            acc_ref[...] = part if rest is None else part + refs[0][...]

        @pl.when(k > 0)
        def _():
            acc_ref[...] += part

        @pl.when(k == nk - 1)
        def _():
            o_ref[...] = acc_ref[...]

    tile = pl.BlockSpec((tm, tn), lambda i, j, k, s: (i, j))
    in_specs = [pl.BlockSpec((tm, tk), lambda i, j, k, s: (i, block(k, s))), pl.BlockSpec((tk, tn), lambda i, j, k, s: (block(k, s), j))]
    return _call(body, prefetch=place, name=name, grid=(M // tm, N // tn, nk), in_specs=in_specs + ([] if rest is None else [tile]),
                 out_specs=tile, out_shape=jax.ShapeDtypeStruct((M, N), F32), scratch_shapes=[pltpu.VMEM((tm, tn), F32)],
                 sem=("parallel", "parallel", "arbitrary"))(a, b, *([] if rest is None else [rest]))


def _rms(x, g):
    r = lax.rsqrt(jnp.mean(x * x, axis=-1, keepdims=True) + EPS)
    return x * r * g


def _rms_bwd(dy, x, g):
    r = lax.rsqrt(jnp.mean(x * x, axis=-1, keepdims=True) + EPS)
    xn = x * r
    dyg = dy * g
    dx = r * (dyg - xn * jnp.mean(dyg * xn, axis=-1, keepdims=True))
    return dx, jnp.sum(dy * xn, axis=0, keepdims=True)


def _rows(d, tr=ROW_TILE):
    return pl.BlockSpec((tr, d), lambda i: (i, 0))


def _vec(d):
    return pl.BlockSpec((1, d), lambda i: (0, 0))


def _accumulate(ref, val):
    @pl.when(pl.program_id(0) == 0)
    def _():
        ref[...] = val

    @pl.when(pl.program_id(0) > 0)
    def _():
        ref[...] += val


def _norm_fwd(x, g, name):
    T, D = x.shape

    def body(x_ref, g_ref, h_ref):
        h_ref[...] = _rms(x_ref[...], g_ref[...]).astype(BF16)

    return _call(body,name=name,grid=(T // ROW_TILE,), in_specs=[_rows(D), _vec(D)], out_specs=_rows(D),
                          out_shape=jax.ShapeDtypeStruct((T, D), BF16), sem=("parallel",))(x, g)


def _residual_norm(x, m, g_post, g_next, name):
    T, D = x.shape

    def body(x_ref, m_ref, gp_ref, gn_ref, xo_ref, h_ref):
        xo = x_ref[...] + _rms(m_ref[...], gp_ref[...])
        xo_ref[...] = xo
        h_ref[...] = _rms(xo, gn_ref[...]).astype(BF16)

    return _call(body,name=name,grid=(T // ROW_TILE,), in_specs=[_rows(D), _rows(D), _vec(D), _vec(D)],
                          out_specs=[_rows(D), _rows(D)],
                          out_shape=[jax.ShapeDtypeStruct((T, D), F32), jax.ShapeDtypeStruct((T, D), BF16)],
                          sem=("parallel",))(x, m, g_post, g_next)


def _loss_and_last_norm_bwd(x, m, g_post, target, name):
    T, D = x.shape

    def body(x_ref, m_ref, gp_ref, t_ref, dx_ref, dm_ref, dg_ref, loss_ref):
        m_val, gp = m_ref[...], gp_ref[...]
        err = x_ref[...] + _rms(m_val, gp) - t_ref[...]
        dx = err * (1.0 / D)
        dx_ref[...] = dx
        dm, dg = _rms_bwd(dx, m_val, gp)
        dm_ref[...] = dm.astype(BF16)
        _accumulate(dg_ref, dg)
        _accumulate(loss_ref, jnp.full((1, 128), 0.5 * jnp.sum(err * err) * (1.0 / D), F32))

    return _call(body,name=name,grid=(T // ROW_TILE,), in_specs=[_rows(D), _rows(D), _vec(D), _rows(D)],
                          out_specs=[_rows(D), _rows(D), _vec(D), _vec(128)],
                          out_shape=[jax.ShapeDtypeStruct((T, D), F32), jax.ShapeDtypeStruct((T, D), BF16),
                                     jax.ShapeDtypeStruct((1, D), F32), jax.ShapeDtypeStruct((1, 128), F32)],
                          sem=("arbitrary",))(x, m, g_post, target)


def _norms_bwd(dx, dh, x_in, g_pre, m_prev, g_post_prev, name):
    T, D = dx.shape
    with_prev = m_prev is not None

    def body(*refs):
        if with_prev:
            dx_ref, dh_ref, x_ref, gq_ref, m_ref, gp_ref, dxo_ref, dgq_ref, dm_ref, dgp_ref = refs
        else:
            dx_ref, dh_ref, x_ref, gq_ref, dxo_ref, dgq_ref = refs
        d_in, dgq = _rms_bwd(dh_ref[...], x_ref[...], gq_ref[...])
        dxo = dx_ref[...] + d_in
        dxo_ref[...] = dxo
        _accumulate(dgq_ref, dgq)
        if with_prev:
            dm, dgp = _rms_bwd(dxo, m_ref[...], gp_ref[...])
            dm_ref[...] = dm.astype(BF16)
            _accumulate(dgp_ref, dgp)

    ins, in_specs = [dx, dh, x_in, g_pre], [_rows(D), _rows(D), _rows(D), _vec(D)]
    out_specs = [_rows(D), _vec(D)]
    out_shape = [jax.ShapeDtypeStruct((T, D), F32), jax.ShapeDtypeStruct((1, D), F32)]
    if with_prev:
        ins += [m_prev, g_post_prev]
        in_specs += [_rows(D), _vec(D)]
        out_specs += [_rows(D), _vec(D)]
        out_shape += [jax.ShapeDtypeStruct((T, D), BF16), jax.ShapeDtypeStruct((1, D), F32)]
    return _call(body,name=name,grid=(T // ROW_TILE,), in_specs=in_specs, out_specs=out_specs, out_shape=out_shape,
                          sem=("arbitrary",))(*ins)


def _window_weights(g):
    w = 2 << g
    return w, [jnp.where(j < w, 1.0, 0.0).astype(F32) for j in range(MAX_WINDOW)]


def _valid_count(r0, rows, w):
    t = (lax.broadcasted_iota(jnp.int32, (rows, 1), 0) + (r0 + 1)).astype(F32)
    return jnp.minimum(t, w.astype(F32))


def _pool_fwd(z, pool_w, pool_scale, name):
    T = z.shape[0]
    PG = pool_w.shape[-1]
    DP = N_GROUPS * PG
    rc = min(CHUNK, T)

    def body(u_ref, pw_ref, sc_ref, pooled_ref, y_ref, pad):
        w, wts = _window_weights(pl.program_id(0))
        pad[pl.ds(0, MAX_WINDOW), :] = jnp.zeros((MAX_WINDOW, PG), F32)
        pad[pl.ds(MAX_WINDOW, T), :] = u_ref[...]
        for r0 in range(0, T, rc):
            acc = jnp.zeros((rc, PG), F32)
            for j in range(MAX_WINDOW):
                acc = acc + wts[j] * pad[pl.ds(MAX_WINDOW + r0 - j, rc), :]
            pooled = acc / _valid_count(r0, rc, w) - u_ref[pl.ds(r0, rc), :]
            pooled_ref[pl.ds(r0, rc), :] = pooled.astype(BF16)
        mixed = jnp.dot(pooled_ref[...], pw_ref[...].astype(BF16), preferred_element_type=F32)
        y_ref[...] = (mixed * sc_ref[...]).astype(BF16)

    col = lambda g: (0, g)
    return _call(
        body, name=name,grid=(N_GROUPS,),
        in_specs=[pl.BlockSpec((T, PG), col), pl.BlockSpec((None, PG, PG), lambda g: (g, 0, 0)), pl.BlockSpec((1, PG), col)],
        out_specs=[pl.BlockSpec((T, PG), col), pl.BlockSpec((T, PG), col)],
        out_shape=[jax.ShapeDtypeStruct((T, DP), BF16), jax.ShapeDtypeStruct((T, DP), BF16)],
        scratch_shapes=[pltpu.VMEM((T + MAX_WINDOW, PG), F32)], sem=("parallel",))(z, pool_w, pool_scale)


def _pool_bwd(dy, pooled, pool_w, pool_scale, name):
    T = dy.shape[0]
    PG = pool_w.shape[-1]
    DP = N_GROUPS * PG
    rc = min(CHUNK, T)

    def body(dy_ref, pooled_ref, pw_ref, sc_ref, du_ref, dpw_ref, dsc_ref, pad, dp_ref):
        w, wts = _window_weights(pl.program_id(0))
        pooled_v, pw = pooled_ref[...], pw_ref[...].astype(BF16)
        dy_v = dy_ref[...]
        mixed = jnp.dot(pooled_v, pw, preferred_element_type=F32)
        dsc_ref[...] = jnp.sum(dy_v * mixed, axis=0, keepdims=True)
        dmixed = (dy_v * sc_ref[...]).astype(BF16)
        dpw_ref[...] = lax.dot_general(pooled_v, dmixed, _DIMS["tn"], preferred_element_type=F32)
        dp_ref[...] = lax.dot_general(dmixed, pw, _DIMS["nt"], preferred_element_type=F32)
        pad[pl.ds(T, MAX_WINDOW), :] = jnp.zeros((MAX_WINDOW, PG), F32)
        for r0 in range(0, T, rc):
            pad[pl.ds(r0, rc), :] = dp_ref[pl.ds(r0, rc), :] / _valid_count(r0, rc, w)
        for r0 in range(0, T, rc):
            acc = jnp.zeros((rc, PG), F32)
            for j in range(MAX_WINDOW):
                acc = acc + wts[j] * pad[pl.ds(r0 + j, rc), :]
            du_ref[pl.ds(r0, rc), :] = (acc - dp_ref[pl.ds(r0, rc), :]).astype(BF16)

    col = lambda g: (0, g)
    return _call(
        body, name=name,grid=(N_GROUPS,),
        in_specs=[pl.BlockSpec((T, PG), col), pl.BlockSpec((T, PG), col), pl.BlockSpec((None, PG, PG), lambda g: (g, 0, 0)),
                  pl.BlockSpec((1, PG), col)],
        out_specs=[pl.BlockSpec((T, PG), col), pl.BlockSpec((None, PG, PG), lambda g: (g, 0, 0)), pl.BlockSpec((1, PG), col)],
        out_shape=[jax.ShapeDtypeStruct((T, DP), BF16), jax.ShapeDtypeStruct((N_GROUPS, PG, PG), F32),
                   jax.ShapeDtypeStruct((1, DP), F32)],
        scratch_shapes=[pltpu.VMEM((T + MAX_WINDOW, PG), F32), pltpu.VMEM((T, PG), F32)],
        sem=("parallel",))(dy, pooled, pool_w, pool_scale)


def _conv_fwd(z, conv_w, conv_b, d_pool, name, tc=128):
    T = z.shape[0]
    DC = conv_w.shape[-1]
    rc = min(CHUNK, T)
    v0, g0 = d_pool // tc, (d_pool + DC) // tc

    def body(v_ref, gt_ref, w_ref, b_ref, a_ref, c_ref, pad):
        pad[pl.ds(0, CONV_PAD), :] = jnp.zeros((CONV_PAD, tc), F32)
        for r0 in range(0, T, rc):
            a = v_ref[pl.ds(r0, rc), :] * _sigmoid(gt_ref[pl.ds(r0, rc), :])
            a_ref[pl.ds(r0, rc), :] = a
            pad[pl.ds(CONV_PAD + r0, rc), :] = a
        for r0 in range(0, T, rc):
            acc = jnp.zeros((rc, tc), F32) + b_ref[...]
            for k in range(CONV_K):
                acc = acc + w_ref[pl.ds(k, 1), :] * pad[pl.ds(CONV_PAD - (CONV_K - 1) + k + r0, rc), :]
            c_ref[pl.ds(r0, rc), :] = acc

    col = lambda j: (0, j)
    return _call(
        body, name=name,grid=(DC // tc,),
        in_specs=[pl.BlockSpec((T, tc), lambda j: (0, v0 + j)), pl.BlockSpec((T, tc), lambda j: (0, g0 + j)),
                  pl.BlockSpec((CONV_K, tc), col), pl.BlockSpec((1, tc), col)],
        out_specs=[pl.BlockSpec((T, tc), col), pl.BlockSpec((T, tc), col)],
        out_shape=[jax.ShapeDtypeStruct((T, DC), F32), jax.ShapeDtypeStruct((T, DC), F32)],
        scratch_shapes=[pltpu.VMEM((T + CONV_PAD, tc), F32)], sem=("parallel",))(z, z, conv_w, conv_b)


def _conv_bwd(dc, a, z, conv_w, d_pool, name, tc=128):
    T, DC = dc.shape
    rc = min(CHUNK, T)
    v0, g0 = d_pool // tc, (d_pool + DC) // tc

    def body(dc_ref, a_ref, v_ref, gt_ref, w_ref, dv_ref, dg_ref, dw_ref, db_ref, apad, dpad):
        apad[pl.ds(0, CONV_PAD), :] = jnp.zeros((CONV_PAD, tc), F32)
        apad[pl.ds(CONV_PAD, T), :] = a_ref[...]
        dpad[pl.ds(0, T), :] = dc_ref[...]
        dpad[pl.ds(T, CONV_PAD), :] = jnp.zeros((CONV_PAD, tc), F32)
        db_ref[...] = jnp.sum(dc_ref[...], axis=0, keepdims=True)
        for k in range(CONV_K):
            acc = jnp.zeros((8, tc), F32)
            for r0 in range(0, T, rc):
                prod = dc_ref[pl.ds(r0, rc), :] * apad[pl.ds(CONV_PAD - (CONV_K - 1) + k + r0, rc), :]
                acc = acc + jnp.sum(prod.reshape(rc // 8, 8, tc), axis=0)
            dw_ref[pl.ds(k, 1), :] = jnp.sum(acc, axis=0, keepdims=True)
        for r0 in range(0, T, rc):
            da = jnp.zeros((rc, tc), F32)
            for k in range(CONV_K):
                da = da + w_ref[pl.ds(k, 1), :] * dpad[pl.ds(r0 + (CONV_K - 1) - k, rc), :]
            sig = _sigmoid(gt_ref[pl.ds(r0, rc), :])
            dv_ref[pl.ds(r0, rc), :] = (da * sig).astype(BF16)
            dg_ref[pl.ds(r0, rc), :] = (da * v_ref[pl.ds(r0, rc), :] * sig * (1.0 - sig)).astype(BF16)

    col = lambda j: (0, j)
    return _call(
        body, name=name,grid=(DC // tc,),
        in_specs=[pl.BlockSpec((T, tc), col), pl.BlockSpec((T, tc), col), pl.BlockSpec((T, tc), lambda j: (0, v0 + j)),
                  pl.BlockSpec((T, tc), lambda j: (0, g0 + j)), pl.BlockSpec((CONV_K, tc), col)],
        out_specs=[pl.BlockSpec((T, tc), col), pl.BlockSpec((T, tc), col), pl.BlockSpec((CONV_K, tc), col),
                   pl.BlockSpec((1, tc), col)],
        out_shape=[jax.ShapeDtypeStruct((T, DC), BF16), jax.ShapeDtypeStruct((T, DC), BF16),
                   jax.ShapeDtypeStruct((CONV_K, DC), F32), jax.ShapeDtypeStruct((1, DC), F32)],
        scratch_shapes=[pltpu.VMEM((T + CONV_PAD, tc), F32), pltpu.VMEM((T + CONV_PAD, tc), F32)],
        sem=("parallel",))(dc, a, z, z, conv_w)


def _layer_norm_parts(c, g, b):
    mu = jnp.mean(c, axis=-1, keepdims=True)
    xc = c - mu
    rstd = lax.rsqrt(jnp.mean(xc * xc, axis=-1, keepdims=True) + EPS)
    xhat = xc * rstd
    return xhat, rstd, xhat * g + b


def _ln_silu_fwd(c, g, b, name):
    T, DC = c.shape

    def body(c_ref, g_ref, b_ref, y_ref):
        _, _, ln = _layer_norm_parts(c_ref[...], g_ref[...], b_ref[...])
        y_ref[...] = (ln * _sigmoid(ln)).astype(BF16)

    return _call(body,name=name,grid=(T // ROW_TILE,), in_specs=[_rows(DC), _vec(DC), _vec(DC)], out_specs=_rows(DC),
                          out_shape=jax.ShapeDtypeStruct((T, DC), BF16), sem=("parallel",))(c, g, b)


def _ln_silu_bwd(dy, c, g, b, name):
    T, DC = c.shape

    def body(dy_ref, c_ref, g_ref, b_ref, dc_ref, dg_ref, db_ref):
        gain = g_ref[...]
        xhat, rstd, ln = _layer_norm_parts(c_ref[...], gain, b_ref[...])
        s = _sigmoid(ln)
        dln = dy_ref[...] * (s * (1.0 + ln * (1.0 - s)))
        _accumulate(dg_ref, jnp.sum(dln * xhat, axis=0, keepdims=True))
        _accumulate(db_ref, jnp.sum(dln, axis=0, keepdims=True))
        dxh = dln * gain
        dc_ref[...] = rstd * (dxh - jnp.mean(dxh, axis=-1, keepdims=True) - xhat * jnp.mean(dxh * xhat, axis=-1, keepdims=True))

    return _call(body,name=name,grid=(T // ROW_TILE,),
                          in_specs=[pl.BlockSpec((ROW_TILE, DC), lambda i: (i, 1)), _rows(DC), _vec(DC), _vec(DC)],
                          out_specs=[_rows(DC), _vec(DC), _vec(DC)],
                          out_shape=[jax.ShapeDtypeStruct((T, DC), F32), jax.ShapeDtypeStruct((1, DC), F32),
                                     jax.ShapeDtypeStruct((1, DC), F32)],
                          sem=("arbitrary",))(dy, c, g, b)


def _short_specs(T, DS, tc):
    n = DS // tc
    return [pl.BlockSpec((T, tc), lambda j: (0, j)), pl.BlockSpec((T, tc), lambda j: (0, n + j)),
            pl.BlockSpec((T, tc), lambda j: (0, 2 * n + j))]


def _short_fwd(z, w, name, tc=256):
    T = z.shape[0]
    DS = w.shape[-1]
    rc = min(CHUNK, T)

    def body(b_ref, cg_ref, u_ref, w_ref, y_ref, pad):
        pad[pl.ds(0, SHORT_PAD), :] = jnp.zeros((SHORT_PAD, tc), F32)
        pad[pl.ds(SHORT_PAD, T), :] = cg_ref[...] * u_ref[...]
        for r0 in range(0, T, rc):
            r = jnp.zeros((rc, tc), F32)
            for k in range(SHORT_K):
                r = r + w_ref[pl.ds(k, 1), :] * pad[pl.ds(SHORT_PAD - (SHORT_K - 1) + k + r0, rc), :]
            y_ref[pl.ds(r0, rc), :] = (b_ref[pl.ds(r0, rc), :] * r).astype(BF16)

    col = lambda j: (0, j)
    return _call(body,name=name,grid=(DS // tc,), in_specs=_short_specs(T, DS, tc) + [pl.BlockSpec((SHORT_K, tc), col)],
                          out_specs=pl.BlockSpec((T, tc), col), out_shape=jax.ShapeDtypeStruct((T, DS), BF16),
                          scratch_shapes=[pltpu.VMEM((T + SHORT_PAD, tc), F32)], sem=("parallel",))(z, z, z, w)


def _short_bwd(dy, z, w, name, tc=256):
    T, DS = dy.shape
    rc = min(CHUNK, T)

    def body(dy_ref, b_ref, cg_ref, u_ref, w_ref, dz_ref, dw_ref, qpad, rpad):
        qpad[pl.ds(0, SHORT_PAD), :] = jnp.zeros((SHORT_PAD, tc), F32)
        qpad[pl.ds(SHORT_PAD, T), :] = cg_ref[...] * u_ref[...]
        rpad[pl.ds(0, T), :] = dy_ref[...] * b_ref[...]
        rpad[pl.ds(T, SHORT_PAD), :] = jnp.zeros((SHORT_PAD, tc), F32)
        accs = [jnp.zeros((8, tc), F32) for _ in range(SHORT_K)]
        for r0 in range(0, T, rc):
            r = jnp.zeros((rc, tc), F32)
            dq = jnp.zeros((rc, tc), F32)
            dr = rpad[pl.ds(r0, rc), :]
            for k in range(SHORT_K):
                q_k = qpad[pl.ds(SHORT_PAD - (SHORT_K - 1) + k + r0, rc), :]
                r = r + w_ref[pl.ds(k, 1), :] * q_k
                dq = dq + w_ref[pl.ds(k, 1), :] * rpad[pl.ds(r0 + (SHORT_K - 1) - k, rc), :]
                accs[k] = accs[k] + jnp.sum((dr * q_k).reshape(rc // 8, 8, tc), axis=0)
            dz_ref[0, pl.ds(r0, rc), :] = (dy_ref[pl.ds(r0, rc), :] * r).astype(BF16)
            dz_ref[1, pl.ds(r0, rc), :] = (dq * u_ref[pl.ds(r0, rc), :]).astype(BF16)
            dz_ref[2, pl.ds(r0, rc), :] = (dq * cg_ref[pl.ds(r0, rc), :]).astype(BF16)
        for k in range(SHORT_K):
            dw_ref[pl.ds(k, 1), :] = jnp.sum(accs[k], axis=0, keepdims=True)

    col = lambda j: (0, j)
    tile = pl.BlockSpec((T, tc), col)
    return _call(body,name=name,grid=(DS // tc,),
                          in_specs=[tile] + _short_specs(T, DS, tc) + [pl.BlockSpec((SHORT_K, tc), col)],
                          out_specs=[pl.BlockSpec((3, T, tc), lambda j: (0, 0, j)), pl.BlockSpec((SHORT_K, tc), col)],
                          out_shape=[jax.ShapeDtypeStruct((3, T, DS), BF16), jax.ShapeDtypeStruct((SHORT_K, DS), F32)],
                          scratch_shapes=[pltpu.VMEM((T + SHORT_PAD, tc), F32), pltpu.VMEM((T + SHORT_PAD, tc), F32)],
                          sem=("parallel",))(dy, z, z, z, w)


def _tile_rows(rows, cols, n_bufs):
    budget = VMEM_LIMIT_BYTES * 3 // 4 // (2 * n_bufs * 4 * cols)
    tr = rows
    while tr > budget and tr % 16 == 0:
        tr //= 2
    return tr


def _placed_call(body, name, place, grid, in_specs, out_specs, out_shape, ins):
    return _call(body,prefetch=place, name=name, grid=grid, in_specs=in_specs, out_specs=out_specs, out_shape=out_shape,
                 sem=("parallel",))(*ins)


def _cast_into_full(w, layer, kind, place, name):
    _, R, C = w.shape
    tr = _tile_rows(R, C, 2)
    nb = R // tr
    if kind == "col":
        full, out_spec = (R, C * N_CHIPS), pl.BlockSpec((tr, C), lambda i, s: (i, s[0]))
    else:
        full, out_spec = (R * N_CHIPS, C), pl.BlockSpec((tr, C), lambda i, s: (s[0] * nb + i, 0))

    def body(s_ref, w_ref, o_ref):
        o_ref[...] = w_ref[...].astype(BF16)

    return _placed_call(body, name, place, (nb,), [pl.BlockSpec((None, tr, C), lambda i, s: (layer, i, 0))], out_spec,
                        jax.ShapeDtypeStruct(full, BF16), [w])


def _add_pair(grad, theirs, kind, place, name):
    R, C = grad.shape
    piece_rows = R // 2 if kind == "col" else R // N_CHIPS // 2
    tr = _tile_rows(piece_rows, C, 3)
    nb = piece_rows // tr
    if kind == "col":
        g_spec = pl.BlockSpec((tr, C), lambda i, s: (s[1] * nb + i, 0))
    else:
        g_spec = pl.BlockSpec((tr, C), lambda i, s: ((2 * (i // nb) + s[1]) * nb + i % nb, 0))
    flat = pl.BlockSpec((tr, C), lambda i, s: (i, 0))

    def body(s_ref, a_ref, b_ref, o_ref):
        o_ref[...] = (a_ref[...].astype(F32) + b_ref[...].astype(F32)).astype(BF16)

    return _placed_call(body, name, place, (R // 2 // tr,), [g_spec, flat], flat, jax.ShapeDtypeStruct((R // 2, C), BF16),
                        [grad, theirs])


def _sum_chips(chip_sum, arrived, kind, place, name):
    _, H, W = arrived.shape
    tr = _tile_rows(H, W, 6)
    nb = H // tr
    if kind == "col":
        own_spec = pl.BlockSpec((tr, W), lambda i, s: (i, s[0]))
    else:
        own_spec = pl.BlockSpec((tr, W), lambda i, s: (s[0] * nb + i, 0))

    def body(s_ref, p_ref, r_ref, o_ref):
        acc = p_ref[...].astype(F32)
        for i in range(N_CHIPS - 1):
            acc = acc + r_ref[i].astype(F32)
        o_ref[...] = acc

    return _placed_call(body, name, place, (nb,), [own_spec, pl.BlockSpec((N_CHIPS - 1, tr, W), lambda i, s: (0, i, 0))],
                        pl.BlockSpec((tr, W), lambda i, s: (s[1] * nb + i, 0)), jax.ShapeDtypeStruct((2 * H, W), F32),
                        [chip_sum, arrived])


def _adamw_values(w, g, m, v):
    m = ADAM_B1 * m + (1.0 - ADAM_B1) * g
    v = ADAM_B2 * v + (1.0 - ADAM_B2) * (g * g)
    m_hat = m / (1.0 - ADAM_B1 ** ADAM_STEP)
    v_hat = v / (1.0 - ADAM_B2 ** ADAM_STEP)
    return -ADAM_LR * (m_hat / (jnp.sqrt(v_hat) + ADAM_EPS) + ADAM_WD * w), m, v


def _adamw(w, g, m, v, name, layer=0, carried=None):
    L, R, C = w.shape
    tr = _tile_rows(R, C, 8)

    def body(w_ref, g_ref, m_ref, v_ref, *rest):
        go_ref, d_ref, mo_ref, vo_ref = rest[-4:]
        g_val = g_ref[...]
        d, m_new, v_new = _adamw_values(w_ref[...], g_val, m_ref[...], v_ref[...])
        go_ref[...], d_ref[...], mo_ref[...], vo_ref[...] = g_val, d, m_new, v_new

    lay = pl.BlockSpec((None, tr, C), lambda i: (layer, i, 0))
    ins = [w, g, m, v]
    in_specs = [lay, pl.BlockSpec((tr, C), lambda i: (i, 0)), lay, lay]
    aliases = {}
    if carried is not None:
        ins += list(carried)
        in_specs += [pl.BlockSpec(memory_space=pl.ANY)] * 4
        aliases = {4 + i: i for i in range(4)}
    return _call(body, name=name, grid=(R // tr,), in_specs=in_specs, out_specs=[lay] * 4,
                 out_shape=[jax.ShapeDtypeStruct((L, R, C), F32)] * 4, input_output_aliases=aliases, sem=("parallel",))(*ins)


def _aligned(v, m):
    return v if isinstance(v, int) else pl.multiple_of(v, m)


def _place():
    x, y, c = lax.axis_index("x"), lax.axis_index("y"), lax.axis_index("c")
    other_chips = [(x, 1 - y), (1 - x, y), (1 - x, 1 - y)]
    return x, y, c, 2 * x + y, other_chips


def _chip_index(chip):
    return 2 * chip[0] + chip[1]


def _piece(ref, kind, k, h):
    R, C = ref.shape
    if kind == "col":
        return ref.at[pl.ds(_aligned(h * (R // 2), 16), R // 2), pl.ds(_aligned(k * (C // N_CHIPS), 128), C // N_CHIPS)]
    rs = R // N_CHIPS
    return ref.at[pl.ds(_aligned(k * rs + h * (rs // 2), 16), rs // 2), :]


def _compact_piece(ref, kind, k):
    R2, C = ref.shape
    if kind == "col":
        return ref.at[:, pl.ds(_aligned(k * (C // N_CHIPS), 128), C // N_CHIPS)]
    return ref.at[pl.ds(_aligned(k * (R2 // N_CHIPS), 16), R2 // N_CHIPS), :]


def _half_rows(ref, h):
    R = ref.shape[0]
    return ref.at[pl.ds(_aligned(h * (R // 2), 16), R // 2), :]


class _Copies:
    def __init__(self, send_sems, recv_sems):
        self.send_sems, self.recv_sems = send_sems, recv_sems
        self.n_remote = 0

    def remote(self, src, dst, device):
        k = self.n_remote
        self.n_remote += 1
        return pltpu.make_async_remote_copy(src_ref=src, dst_ref=dst, send_sem=self.send_sems.at[k], recv_sem=self.recv_sems.at[k],
                                            device_id=device, device_id_type=MESH)


class _Job:
    def __init__(self, ins, out_shape, aliases, n_remote, build, sibling_only=False):
        self.ins, self.out_shape, self.aliases, self.n_remote, self.build = list(ins), list(out_shape), dict(aliases), n_remote, build
        self.sibling_only = sibling_only


class _Flying:
    def __init__(self, job, send_sems, recv_sems, bufs, token):
        self.job, self.send_sems, self.recv_sems, self.bufs, self.token = job, send_sems, recv_sems, bufs, token


def _job_refs(job, buf_refs):
    n_out = len(job.out_shape)
    kept = [i for i in range(len(job.ins)) if i not in job.aliases]
    ins = [buf_refs[job.aliases[i]] if i in job.aliases else buf_refs[n_out + kept.index(i)] for i in range(len(job.ins))]
    return ins, list(buf_refs[:n_out])


SIBLING_BARRIER_ID = 1


def _start_job(job, name, after=()):
    n_in, n_out, n_after = len(job.ins), len(job.out_shape), len(after)
    kept = [i for i in range(n_in) if i not in job.aliases]
    n_bufs = n_out + len(kept)

    def body(*refs):
        in_refs, out_refs = refs[:n_in], refs[n_in + n_after:n_in + n_after + n_out]
        send_sems, recv_sems, token = refs[n_in + n_after + n_bufs:]
        if job.sibling_only:
            barrier = pltpu.get_barrier_semaphore()
            pl.semaphore_signal(barrier, inc=1, device_id=(lax.axis_index("x"), lax.axis_index("y"), 1 - lax.axis_index("c")),
                                device_id_type=MESH)
            pl.semaphore_wait(barrier, 1)
        for d in job.build(in_refs, out_refs, _Copies(send_sems, recv_sems)):
            d.start()
        token[...] = jnp.zeros_like(token)

    aliases = dict(job.aliases)
    aliases.update({i: n_out + k for k, i in enumerate(kept)})
    sems = pltpu.SemaphoreType.DMA((job.n_remote,))
    outs = pl.pallas_call(
        body, name=name, in_specs=[HBM] * n_in + [ANY] * n_after,
        out_specs=[HBM] * n_bufs + [SEM, SEM, pl.BlockSpec(memory_space=pltpu.VMEM)],
        out_shape=job.out_shape + [jax.ShapeDtypeStruct(job.ins[i].shape, job.ins[i].dtype) for i in kept]
        + [sems, sems, jax.ShapeDtypeStruct((8, 128), F32)],
        input_output_aliases=aliases,
        compiler_params=pltpu.CompilerParams(has_side_effects=pltpu.SideEffectType.DATAFLOW_SIDE_EFFECTING,
                                             collective_id=SIBLING_BARRIER_ID if job.sibling_only else None))(*job.ins, *after)
    return _Flying(job, outs[n_bufs], outs[n_bufs + 1], list(outs[:n_bufs]), outs[n_bufs + 2])


def _wait_job(flying, name, after=()):
    job, n_bufs, n_after = flying.job, len(flying.bufs), len(after)

    def body(*refs):
        in_refs, out_refs = _job_refs(job, refs[:n_bufs])
        send_sems, recv_sems = refs[n_bufs:n_bufs + 2]
        copies = job.build(in_refs, out_refs, _Copies(send_sems, recv_sems))
        for d in copies:
            d.wait_send()
        for d in copies:
            d.wait_recv()

    outs = pl.pallas_call(
        body, name=name, in_specs=[HBM] * n_bufs + [SEM, SEM] + [ANY] * n_after, out_specs=[HBM] * n_bufs,
        out_shape=[jax.ShapeDtypeStruct(b.shape, b.dtype) for b in flying.bufs],
        input_output_aliases={i: i for i in range(n_bufs)},
        compiler_params=pltpu.CompilerParams(has_side_effects=pltpu.SideEffectType.DATAFLOW_SIDE_EFFECTING))(
            *flying.bufs, flying.send_sems, flying.recv_sems, *after)
    return list(outs[:len(job.out_shape)]), list(outs[len(job.out_shape):])


def _in_place(arrays):
    return [jax.ShapeDtypeStruct(a.shape, a.dtype) for a in arrays], {u: u for u in range(len(arrays))}


def _rows_part(ref, part, n_parts):
    h = ref.shape[0] // n_parts
    return ref.at[pl.ds(part * h, h), :]


def _gather_job(full, kind, stage):
    def build(in_refs, out_refs, cp):
        x, y, c, me, (y_nbr, x_nbr, diagonal) = _place()
        (ref,) = out_refs
        sibling = (x, y, 1 - c)
        if stage == 1:
            mine = _piece(ref, kind, me, c)
            return [cp.remote(mine, mine, (*y_nbr, c)), cp.remote(mine, mine, (*x_nbr, c))]
        from_y, from_x = _piece(ref, kind, _chip_index(y_nbr), c), _piece(ref, kind, _chip_index(x_nbr), c)
        copies = []
        if stage in (2, "relay"):
            relay_0, relay_1 = _rows_part(from_x, 0, 2), _rows_part(from_y, 1, 2)
            copies += [cp.remote(relay_0, relay_0, (*y_nbr, c)), cp.remote(relay_1, relay_1, (*x_nbr, c))]
        if stage in (2, "direct"):
            copies += [cp.remote(from_y, from_y, sibling), cp.remote(from_x, from_x, sibling)]
        if stage == 3:
            from_diagonal = _piece(ref, kind, _chip_index(diagonal), c)
            copies.append(cp.remote(from_diagonal, from_diagonal, sibling))
        return copies

    return _Job([full], *_in_place([full]), {1: 2, 2: 4, "relay": 2, "direct": 2, 3: 1}[stage], build,
                sibling_only=stage in ("direct", 3))


def _gather_small_job(fulls, axes):
    def build(in_refs, out_refs, cp):
        x, y, c, me, chips = _place()
        copies = []
        for ref, ax in zip(out_refs, axes):
            n = ref.shape[ax] // N_CHIPS
            idx = [slice(None)] * len(ref.shape)
            idx[ax] = pl.ds(_aligned(me * n, n), n)
            mine = ref.at[tuple(idx)]
            copies += [cp.remote(mine, mine, (*chip, c)) for chip in chips]
        return copies

    return _Job(fulls, *_in_place(fulls), 3 * len(fulls), build)


def _exchange_halves_job(grads, kinds):
    def build(in_refs, out_refs, cp):
        x, y, c, me, chips = _place()
        copies = []
        for src, dst, kind in zip(in_refs, out_refs, kinds):
            if kind == "col":
                copies.append(cp.remote(_half_rows(src, 1 - c), dst, (x, y, 1 - c)))
            else:
                copies += [cp.remote(_piece(src, "row", k, 1 - c), _compact_piece(dst, "row", k), (x, y, 1 - c))
                           for k in range(N_CHIPS)]
        return copies

    out_shape = [jax.ShapeDtypeStruct((g.shape[0] // 2, g.shape[1]), g.dtype) for g in grads]
    return _Job(grads, out_shape, {}, sum(1 if k == "col" else N_CHIPS for k in kinds), build, sibling_only=True)


def _scatter_job(half, kind):
    def build(in_refs, out_refs, cp):
        x, y, c, me, chips = _place()
        (src,), (dst,) = in_refs, out_refs
        return [cp.remote(_compact_piece(src, kind, _chip_index(chip)), dst.at[r], (*chip, c)) for r, chip in enumerate(chips)]

    part_shape = (half.shape[0], half.shape[1] // N_CHIPS) if kind == "col" else (half.shape[0] // N_CHIPS, half.shape[1])
    return _Job([half], [jax.ShapeDtypeStruct((N_CHIPS - 1,) + part_shape, half.dtype)], {}, N_CHIPS - 1, build)


def _share_job(shards):
    def build(in_refs, out_refs, cp):
        x, y, c, me, chips = _place()
        copies = []
        for ref in out_refs:
            mine = _half_rows(ref, c)
            copies.append(cp.remote(mine, mine, (x, y, 1 - c)))
        return copies

    return _Job(shards, *_in_place(shards), len(shards), build, sibling_only=True)


N_DEVICES = 2 * N_CHIPS


def _small_exchange_job(slots):
    def build(in_refs, out_refs, cp):
        x, y, c, me, chips = _place()
        (ref,) = out_refs
        mine = ref.at[2 * me + c]
        return [cp.remote(mine, mine, (x ^ (p >> 2), y ^ ((p >> 1) & 1), c ^ (p & 1))) for p in range(1, N_DEVICES)]

    return _Job([slots], *_in_place([slots]), N_DEVICES - 1, build)


def _in_own_slot(packed, place, name):
    R, C = packed.shape

    def body(s_ref, p_ref, o_ref):
        o_ref[...] = p_ref[...]

    return _placed_call(body, name, place, (1,), [pl.BlockSpec((R, C), lambda i, s: (0, 0))],
                        pl.BlockSpec((None, R, C), lambda i, s: (2 * s[0] + s[1], 0, 0)),
                        jax.ShapeDtypeStruct((N_DEVICES, R, C), F32), [packed])


def _sum_slots(slots, name):
    n, R, C = slots.shape

    def body(s_ref, o_ref):
        acc = s_ref[0]
        for i in range(1, n):
            acc = acc + s_ref[i]
        o_ref[...] = acc

    return _call(body,name=name, grid=(1,), in_specs=[pl.BlockSpec((n, R, C), lambda i: (0, 0, 0))],
                 out_specs=pl.BlockSpec((R, C), lambda i: (0, 0)), out_shape=jax.ShapeDtypeStruct((R, C), F32),
                 sem=("arbitrary",))(slots)


def _packed_rows(size, width):
    return -(-size // (8 * width)) * 8


def _pack(arrays, width):
    rows = []
    for a in arrays:
        flat = a.reshape(-1)
        n_rows = _packed_rows(flat.shape[0], width)
        rows.append(jnp.pad(flat, (0, n_rows * width - flat.shape[0])).reshape(n_rows, width))
    return jnp.concatenate(rows, axis=0)


def _unpack(packed, shapes):
    out, r0, width = [], 0, packed.shape[1]
    for shape in shapes:
        size = 1
        for d in shape:
            size *= d
        out.append(packed[r0:r0 + _packed_rows(size, width)].reshape(-1)[:size].reshape(shape))
        r0 += _packed_rows(size, width)
    return out


class _Backlog:
    def __init__(self, first):
        self.now, self.free, self.flights, self.last, self.chain = 0.0, {"ici": 0.0, "d2d": 0.0}, [], first, []

    def run(self, fn, us, *args, **kw):
        out = fn(*args, **kw)
        self.now += us
        self.last = out[0] if isinstance(out, (list, tuple)) else out
        self.poll()
        return out

    def start(self, job, name, link, cost, done):
        flying = _start_job(job, name + "_start", self.chain)
        self.chain = [flying.token]
        _Behind.pending.append(flying.token)
        ends = max(self.now, self.free[link]) + cost
        self.free[link] = ends
        self.flights.append((ends + LANDING_SLACK_US, name, flying, done))
        self.flights.sort(key=lambda f: f[0])
        return flying

    def poll(self, block=False):
        while self.flights and (block or self.flights[0][0] <= self.now):
            ends, name, flying, done = self.flights.pop(0)
            self.now, block = max(self.now, ends), False
            done(*_wait_job(flying, name + "_wait", [self.last] + self.chain))


class _GatherStream:
    def __init__(self, backlog, bufs, kinds, costs, early=()):
        self.backlog, self.bufs, self.kinds, self.costs, self.begun, self.complete = backlog, bufs, kinds, costs, 0, set()
        self.early, self.relays, self.near_complete = set(early), {}, set()
        self.begin()

    def begin(self):
        u, self.begun = self.begun, self.begun + 1
        self.backlog.start(_gather_job(self.bufs[u], self.kinds[u], 1), "gather_%d" % u, "ici", 0.5 * self.costs[u],
                           lambda outs, kept: self.arrived(u, outs[0]))

    def arrived(self, u, buf):
        self.bufs[u] = buf
        if u in self.early:
            self.relays[u] = self.backlog.start(_gather_job(buf, self.kinds[u], "relay"), "relay_%d" % u, "ici",
                                                0.25 * self.costs[u], lambda outs, kept: self.relayed(u, outs[0]))
            self.backlog.start(_gather_job(self.relays[u].bufs[0], self.kinds[u], "direct"), "direct_%d" % u, "d2d",
                               0.5 * D2D_SHARE * self.costs[u], lambda outs, kept: self.near(u, outs[0]))
        else:
            self.backlog.start(_gather_job(buf, self.kinds[u], 2), "relay_%d" % u, "ici", 0.25 * self.costs[u],
                               lambda outs, kept: self.relayed(u, outs[0]))
        while self.begun <= min(u + GATHER_WINDOW[u], len(self.bufs) - 1):
            self.begin()

    def near(self, u, buf):
        self.bufs[u] = self.relays[u].bufs[0] = buf
        self.near_complete.add(u)

    def nearly_ready(self, u):
        while u not in self.near_complete:
            assert self.backlog.flights, "weight %d is not on its way" % u
            self.backlog.poll(block=True)
        return self.bufs[u]

    def relayed(self, u, buf):
        assert u not in self.early or u in self.near_complete, "the relay of weight %d is waited for before its sibling copies" % u
        self.bufs[u] = buf
        self.backlog.start(_gather_job(buf, self.kinds[u], 3), "handon_%d" % u, "d2d", D2D_SHARE * self.costs[u],
                           lambda outs, kept: self.handed(u, outs[0]))

    def handed(self, u, buf):
        self.bufs[u] = buf
        self.complete.add(u)

    def ready(self, u):
        while u not in self.complete:
            assert self.backlog.flights, "weight %d is not on its way" % u
            self.backlog.poll(block=True)
        return self.bufs[u]


class _GradStream:
    def __init__(self, backlog, u, name, kind, cost, g, place, results):
        self.backlog, self.u, self.name, self.kind, self.cost, self.place, self.results = backlog, u, name, kind, cost, place, results
        backlog.start(_exchange_halves_job([g], [kind]), "to_sibling_" + name, "d2d", D2D_SHARE * cost, self.exchanged)

    def exchanged(self, outs, kept):
        chip_sum = self.backlog.run(_add_pair, SIDE_KERNEL_US, kept[0], outs[0], self.kind, self.place, "chip_sum_" + self.name)
        self.backlog.start(_scatter_job(chip_sum, self.kind), "to_owners_" + self.name, "ici", self.cost, self.scattered)

    def scattered(self, outs, kept):
        reduced = self.backlog.run(_sum_chips, SIDE_KERNEL_US, kept[0], outs[0], self.kind, self.place, "reduce_" + self.name)
        self.backlog.start(_share_job([reduced]), "share_" + self.name, "d2d", D2D_SHARE * self.cost, self.shared)

    def shared(self, outs, kept):
        self.results[self.u] = outs[0]


SIDE_KERNEL_US = 12.0
D2D_SHARE = 0.15
LANDING_SLACK_US = 0.0
GATHER_WINDOW = (1, 1, 1, 2, 2, 1, 1, 1)


def kernel(x, mix_pre_g, mix_post_g, ffn_pre_g, ffn_post_g, ab_w_in, pool_w, pool_scale, conv_w, conv_b, conv_ln_g, conv_ln_b, ab_w_out, sc_w_in, sc_conv_w, sc_w_out, ffn_w1, ffn_w2, loss_target, m_mix_pre_g, m_mix_post_g, m_ffn_pre_g, m_ffn_post_g, m_ab_w_in, m_pool_w, m_pool_scale, m_conv_w, m_conv_b, m_conv_ln_g, m_conv_ln_b, m_ab_w_out, m_sc_w_in, m_sc_conv_w, m_sc_w_out, m_ffn_w1, m_ffn_w2, v_mix_pre_g, v_mix_post_g, v_ffn_pre_g, v_ffn_post_g, v_ab_w_in, v_pool_w, v_pool_scale, v_conv_w, v_conv_b, v_conv_ln_g, v_conv_ln_b, v_ab_w_out, v_sc_w_in, v_sc_conv_w, v_sc_w_out, v_ffn_w1, v_ffn_w2):
    x0, target = x[0], loss_target[0]
    T, D = x0.shape
    DP = pool_scale.shape[-1]
    gain = lambda g, layer: g[layer][None, :]

    big = [("ab_w_in", ab_w_in, 0, "col", 67.0), ("ab_w_out", ab_w_out, 0, "row", 44.0),
           ("ffn_w1_0", ffn_w1, 0, "col", 177.0), ("ffn_w2_0", ffn_w2, 0, "row", 177.0),
           ("sc_w_in", sc_w_in, 0, "col", 133.0), ("sc_w_out", sc_w_out, 0, "row", 44.0),
           ("ffn_w1_1", ffn_w1, 1, "col", 177.0), ("ffn_w2_1", ffn_w2, 1, "row", 177.0)]
    kinds = [b[3] for b in big]
    chip = 2 * lax.axis_index("x") + lax.axis_index("y")
    place = jnp.stack([chip, lax.axis_index("c")]).astype(jnp.int32)

    def own_in_zeros(shard, ax):
        full = jnp.zeros(tuple(d * N_CHIPS if i == ax else d for i, d in enumerate(shard.shape)), shard.dtype)
        return lax.dynamic_update_slice_in_dim(full, shard, chip * shard.shape[ax], axis=ax)

    W = [_cast_into_full(w, layer, kind, place, "cast_" + name) for name, w, layer, kind, _ in big]
    smalls = [own_in_zeros(pool_w[0], 1), own_in_zeros(conv_w[0], 1), own_in_zeros(sc_conv_w[0], 1)]
    backlog = _Backlog(x0)
    run = backlog.run
    small_weights = []
    backlog.start(_gather_small_job(smalls, [1, 1, 1]), "gather_small", "ici", 6.0, lambda outs, kept: small_weights.extend(outs))
    gather = _GatherStream(backlog, W, kinds, [b[4] for b in big], early=(3, 7))

    relu_sq = lambda acc: (jnp.maximum(acc, 0.0), jnp.square(jnp.maximum(acc, 0.0)))
    relu_sq_bwd = lambda acc, a: (acc * (2.0 * a.astype(F32)),)

    h0 = run(_norm_fwd, 12.0, x0, gain(mix_pre_g, 0), "norm_in")
    z0 = run(_matmul, 35.0, h0, gather.ready(0), "nn", "mix0_in")
    while not small_weights:
        backlog.poll(block=True)
    pool_w_full, conv_w_full, sc_conv_w_full = small_weights
    pooled, y_pool = run(_pool_fwd, 23.0, z0, pool_w_full, pool_scale, "pool_fwd")
    a_conv, c_conv = run(_conv_fwd, 25.0, z0, conv_w_full, conv_b, DP, "conv_fwd")
    y_conv = run(_ln_silu_fwd, 10.0, c_conv, conv_ln_g, conv_ln_b, "ln_silu_fwd")
    y0 = jnp.concatenate([y_pool, y_conv], axis=1)
    m0 = run(_matmul, 25.0, y0, gather.ready(1), "nn", "mix0_out")
    x1, h1 = run(_residual_norm, 21.0, x0, m0, gain(mix_post_g, 0), gain(ffn_pre_g, 0), "res_mix0")
    a0, a0sq = run(_matmul, 81.0, h1, gather.ready(2), "nn", "ffn0_up", out_dtypes=(BF16, BF16), epilogue=relu_sq)
    f0 = run(_matmul_by_chip_rows, 63.0, a0sq, gather.nearly_ready(3), place, "ffn0_down_near")
    f0 = run(_matmul_by_chip_rows, 21.0, a0sq, gather.ready(3), place, "ffn0_down", rest=f0)
    x2, h2 = run(_residual_norm, 22.0, x1, f0, gain(ffn_post_g, 0), gain(mix_pre_g, 1), "res_ffn0")
    z1 = run(_matmul, 62.0, h2, gather.ready(4), "nn", "mix1_in")
    y1 = run(_short_fwd, 22.0, z1, sc_conv_w_full, "short_fwd")
    m1 = run(_matmul, 25.0, y1, gather.ready(5), "nn", "mix1_out")
    x3, h3 = run(_residual_norm, 21.0, x2, m1, gain(mix_post_g, 1), gain(ffn_pre_g, 1), "res_mix1")
    a1, a1sq = run(_matmul, 81.0, h3, gather.ready(6), "nn", "ffn1_up", out_dtypes=(BF16, BF16), epilogue=relu_sq)
    f1 = run(_matmul_by_chip_rows, 63.0, a1sq, gather.nearly_ready(7), place, "ffn1_down_near")
    f1 = run(_matmul_by_chip_rows, 21.0, a1sq, gather.ready(7), place, "ffn1_down", rest=f1)
    w_in0, w_out0, w1_0, w2_0, w_in1, w_out1, w1_1, w2_1 = W

    grads_big = [None] * len(big)


    def reduce_grad(u, g):
        name, _, _, kind, cost = big[u]
        _GradStream(backlog, u, name, kind, cost, g, place, grads_big)

    dx, df1, d_ffn_post_1, loss_row = run(_loss_and_last_norm_bwd, 30.0, x3, f1, gain(ffn_post_g, 1), target, "loss")
    reduce_grad(7, run(_matmul, 80.0, a1sq, df1, "tn", "ffn1_down_dw", out_dtypes=(BF16,)))
    dz = run(_matmul, 82.0, df1, w2_1, "nt", "ffn1_down_dx", out_dtypes=(BF16,), epilogue=relu_sq_bwd, epi=(a1,))
    reduce_grad(6, run(_matmul, 80.0, h3, dz, "tn", "ffn1_up_dw", out_dtypes=(BF16,)))
    dh = run(_matmul, 87.0, dz, w1_1, "nt", "ffn1_up_dx", tk=LONG_K_TILE)
    dx, d_ffn_pre_1, dm1, d_mix_post_1 = run(_norms_bwd, 36.0, dx, dh, x3, gain(ffn_pre_g, 1), m1, gain(mix_post_g, 1), "norms_bwd3")

    reduce_grad(5, run(_matmul, 24.0, y1, dm1, "tn", "mix1_out_dw", out_dtypes=(BF16,)))
    dy1 = run(_matmul, 25.0, dm1, w_out1, "nt", "mix1_out_dx")
    dz1, d_sc_conv_w = run(_short_bwd, 41.0, dy1, z1, sc_conv_w_full, "short_bwd")
    reduce_grad(4, run(_matmul, 62.0, h2, dz1, "tn", "mix1_in_dw", out_dtypes=(BF16,), stacked=3))
    dh = run(_matmul, 68.0, dz1, w_in1, "nt", "mix1_in_dx", stacked=3)
    dx, d_mix_pre_1, df0, d_ffn_post_0 = run(_norms_bwd, 35.0, dx, dh, x2, gain(mix_pre_g, 1), f0, gain(ffn_post_g, 0), "norms_bwd2")

    reduce_grad(3, run(_matmul, 80.0, a0sq, df0, "tn", "ffn0_down_dw", out_dtypes=(BF16,)))
    dz = run(_matmul, 82.0, df0, w2_0, "nt", "ffn0_down_dx", out_dtypes=(BF16,), epilogue=relu_sq_bwd, epi=(a0,))
    reduce_grad(2, run(_matmul, 80.0, h1, dz, "tn", "ffn0_up_dw", out_dtypes=(BF16,)))
    dh = run(_matmul, 87.0, dz, w1_0, "nt", "ffn0_up_dx", tk=LONG_K_TILE)
    dx, d_ffn_pre_0, dm0, d_mix_post_0 = run(_norms_bwd, 36.0, dx, dh, x1, gain(ffn_pre_g, 0), m0, gain(mix_post_g, 0), "norms_bwd1")

    reduce_grad(1, run(_matmul, 24.0, y0, dm0, "tn", "mix0_out_dw", out_dtypes=(BF16,)))
    dy0 = run(_matmul, 25.0, dm0, w_out0, "nt", "mix0_out_dx")
    du_pool, d_pool_w, d_pool_scale = run(_pool_bwd, 28.0, dy0, pooled, pool_w_full, pool_scale, "pool_bwd")
    dc, d_ln_g, d_ln_b = run(_ln_silu_bwd, 15.0, dy0, c_conv, conv_ln_g, conv_ln_b, "ln_silu_bwd")
    dv, dgate, d_conv_w, d_conv_b = run(_conv_bwd, 52.0, dc, a_conv, z0, conv_w_full, DP, "conv_bwd")
    dz0 = jnp.concatenate([du_pool, dv, dgate], axis=1)

    small_sums = {}

    def exchange_small(key, arrays, cost):
        slots = _in_own_slot(_pack(arrays, D), place, "small_grads_slot_" + key)
        backlog.start(_small_exchange_job(slots), "small_grads_" + key, "ici", cost,
                      lambda outs, kept: small_sums.__setitem__(key, _unpack(_sum_slots(outs[0], "small_grads_sum_" + key),
                                                                             [a.shape for a in arrays])))

    exchange_small("most", [d_mix_pre_1, jnp.concatenate([d_mix_post_0, d_mix_post_1], 0),
                            jnp.concatenate([d_ffn_pre_0, d_ffn_pre_1], 0), jnp.concatenate([d_ffn_post_0, d_ffn_post_1], 0),
                            d_pool_scale, d_conv_b, d_ln_g, d_ln_b, d_pool_w, d_conv_w, d_sc_conv_w], 112.0)
    reduce_grad(0, run(_matmul, 34.0, h0, dz0, "tn", "mix0_in_dw", out_dtypes=(BF16,)))
    dh = run(_matmul, 40.0, dz0, w_in0, "nt", "mix0_in_dx")
    grad_x, d_mix_pre_0 = run(_norms_bwd, 26.0, dx, dh, x0, gain(mix_pre_g, 0), None, None, "norms_bwd0")
    exchange_small("last", [d_mix_pre_0, loss_row], 5.0)

    upd, gr, first = {}, grads_big, {}

    def keep(where, key, outs):
        where[key] = outs
        return outs

    adamw_big = [
        (7, lambda: keep(first, "ffn_w2", _adamw(ffn_w2, gr[7], m_ffn_w2, v_ffn_w2, "adamw_ffn_w2_1", layer=1)), 46.0),
        (6, lambda: keep(first, "ffn_w1", _adamw(ffn_w1, gr[6], m_ffn_w1, v_ffn_w1, "adamw_ffn_w1_1", layer=1)), 46.0),
        (5, lambda: keep(upd, "sc_w_out", _adamw(sc_w_out, gr[5], m_sc_w_out, v_sc_w_out, "adamw_sc_w_out")), 14.0),
        (4, lambda: keep(upd, "sc_w_in", _adamw(sc_w_in, gr[4], m_sc_w_in, v_sc_w_in, "adamw_sc_w_in")), 35.0),
        (3, lambda: keep(upd, "ffn_w2", _adamw(ffn_w2, gr[3], m_ffn_w2, v_ffn_w2, "adamw_ffn_w2_0", layer=0,
                                               carried=first["ffn_w2"])), 46.0),
        (2, lambda: keep(upd, "ffn_w1", _adamw(ffn_w1, gr[2], m_ffn_w1, v_ffn_w1, "adamw_ffn_w1_0", layer=0,
                                               carried=first["ffn_w1"])), 46.0),
        (1, lambda: keep(upd, "ab_w_out", _adamw(ab_w_out, gr[1], m_ab_w_out, v_ab_w_out, "adamw_ab_w_out")), 14.0),
        (0, lambda: keep(upd, "ab_w_in", _adamw(ab_w_in, gr[0], m_ab_w_in, v_ab_w_in, "adamw_ab_w_in")), 19.0)]
    while adamw_big or backlog.flights:
        due = [a for a in adamw_big if gr[a[0]] is not None]
        if due:
            adamw_big.remove(due[0])
            backlog.run(due[0][1], due[0][2])
        else:
            backlog.poll(block=True)

    (g_mix_pre_1, g_mix_post, g_ffn_pre, g_ffn_post, g_pool_scale, g_conv_b, g_ln_g, g_ln_b, g_pool_w_full, g_conv_w_full,
     g_sc_conv_w_full) = small_sums["most"]
    g_mix_pre = jnp.concatenate([small_sums["last"][0], g_mix_pre_1], 0)
    loss = small_sums["last"][1][0, 0]
    own = lambda a, ax: lax.dynamic_slice_in_dim(a, chip * (a.shape[ax] // N_CHIPS), a.shape[ax] // N_CHIPS, axis=ax)
    g_pool_w, g_conv_w, g_sc_conv_w = own(g_pool_w_full, 1), own(g_conv_w_full, 1), own(g_sc_conv_w_full, 1)

    def small_update(w, g, m, v, name):
        shape = w.shape
        as3 = lambda a: a.reshape((1, -1, shape[-1]))
        outs = _adamw(as3(w), g.reshape((-1, shape[-1])), as3(m), as3(v), "adamw_" + name)
        return [o.reshape(shape) for o in outs]

    upd["mix_pre_g"] = small_update(mix_pre_g, g_mix_pre, m_mix_pre_g, v_mix_pre_g, "mix_pre_g")
    upd["mix_post_g"] = small_update(mix_post_g, g_mix_post, m_mix_post_g, v_mix_post_g, "mix_post_g")
    upd["ffn_pre_g"] = small_update(ffn_pre_g, g_ffn_pre, m_ffn_pre_g, v_ffn_pre_g, "ffn_pre_g")
    upd["ffn_post_g"] = small_update(ffn_post_g, g_ffn_post, m_ffn_post_g, v_ffn_post_g, "ffn_post_g")
    upd["pool_w"] = small_update(pool_w, g_pool_w, m_pool_w, v_pool_w, "pool_w")
    upd["pool_scale"] = small_update(pool_scale, g_pool_scale, m_pool_scale, v_pool_scale, "pool_scale")
    upd["conv_w"] = small_update(conv_w, g_conv_w, m_conv_w, v_conv_w, "conv_w")
    upd["conv_b"] = small_update(conv_b, g_conv_b, m_conv_b, v_conv_b, "conv_b")
    upd["conv_ln_g"] = small_update(conv_ln_g, g_ln_g, m_conv_ln_g, v_conv_ln_g, "conv_ln_g")
    upd["conv_ln_b"] = small_update(conv_ln_b, g_ln_b, m_conv_ln_b, v_conv_ln_b, "conv_ln_b")
    upd["sc_conv_w"] = small_update(sc_conv_w, g_sc_conv_w, m_sc_conv_w, v_sc_conv_w, "sc_conv_w")

    order = ["mix_pre_g", "mix_post_g", "ffn_pre_g", "ffn_post_g", "ab_w_in", "pool_w", "pool_scale", "conv_w", "conv_b",
             "conv_ln_g", "conv_ln_b", "ab_w_out", "sc_w_in", "sc_conv_w", "sc_w_out", "ffn_w1", "ffn_w2"]
    out = [loss, grad_x[None]]
    for part in range(4):
        out += [upd[n][part] for n in order]
    return tuple(out)
```

```python
import jax
import jax.numpy as jnp
from jax import lax
from jax.experimental import pallas as pl
from jax.experimental.pallas import tpu as pltpu

F32, BF16 = jnp.float32, jnp.bfloat16
EPS = 1e-6
N_GROUPS = 4
MAX_WINDOW = 16
CONV_K = 31
SHORT_K = 3
CONV_PAD = 32
SHORT_PAD = 8
ADAM_LR, ADAM_B1, ADAM_B2, ADAM_EPS, ADAM_WD, ADAM_STEP = 0.001, 0.9, 0.999, 1e-08, 0.01, 10
N_CHIPS = 4
VMEM_LIMIT_BYTES = 56 * 1024 * 1024
ROW_TILE = 256
CHUNK = 256
LONG_K_TILE = 4096
MESH = pl.DeviceIdType.MESH
HBM = pl.BlockSpec(memory_space=pltpu.HBM)
SEM = pl.BlockSpec(memory_space=pltpu.SEMAPHORE)
ANY = pl.BlockSpec(memory_space=pl.ANY)


def _cp(*sem):
    return pltpu.CompilerParams(dimension_semantics=sem, vmem_limit_bytes=VMEM_LIMIT_BYTES)


def _sigmoid(v):
    return 1.0 / (1.0 + jnp.exp(-v))


class _Behind:
    pending = []


def _call(body, prefetch=None, **kw):
    behind, _Behind.pending = _Behind.pending, []
    single = not isinstance(kw["out_shape"], (list, tuple))
    in_specs, scratch = list(kw["in_specs"]), list(kw.get("scratch_shapes", ()))
    out_shape = [kw["out_shape"]] if single else list(kw["out_shape"])
    out_specs = [kw["out_specs"]] if single else list(kw["out_specs"])
    n_pre = 0 if prefetch is None else 1
    n_own, n_behind = len(in_specs), len(behind)

    def wrapped(*refs):
        body(*refs[:n_pre + n_own], *refs[n_pre + n_own + n_behind:])

    specs = dict(grid=kw["grid"], in_specs=in_specs + [ANY] * n_behind, out_specs=out_specs)
    if prefetch is None:
        specs["scratch_shapes"] = scratch
    else:
        specs = dict(grid_spec=pltpu.PrefetchScalarGridSpec(num_scalar_prefetch=1, scratch_shapes=scratch, **specs))
    aliases = {n_pre + i: o for i, o in kw.get("input_output_aliases", {}).items()}
    call = pl.pallas_call(wrapped, name=kw["name"], out_shape=out_shape, input_output_aliases=aliases,
                          compiler_params=_cp(*kw["sem"]), **specs)

    def run(*args):
        outs = call(*([prefetch] * n_pre), *args, *behind)
        return outs[0] if single else list(outs)

    return run


_DIMS = {"nn": (((1,), (0,)), ((), ())), "nt": (((1,), (1,)), ((), ())), "tn": (((0,), (0,)), ((), ()))}


def _pick(n, cap, step=256):
    if n <= cap:
        return n
    return next(t for t in range(cap - cap % step, 0, -step) if n % t == 0)


def _matmul(a, b, mode, name, out_dtypes=(F32,), epilogue=None, epi=(), stacked=0, tm=1024, tn=1024, tk=2048):
    if mode == "tn":
        (K, M), (K2, N) = a.shape, (b.shape if not stacked else (b.shape[1], stacked * b.shape[2]))
    elif mode == "nt":
        (M, K), (N, K2) = (a.shape if not stacked else (a.shape[1], stacked * a.shape[2])), b.shape
    else:
        (M, K), (K2, N) = a.shape, b.shape
    assert K == K2
    tm = _pick(M, tm)
    tn = _pick(N // stacked if stacked and mode == "tn" else N, tn)
    tk = _pick(K // stacked if stacked and mode == "nt" else K, tk)
    nk = K // tk
    a_spec = pl.BlockSpec((tk, tm), lambda i, j, k: (k, i)) if mode == "tn" else pl.BlockSpec((tm, tk), lambda i, j, k: (i, k))
    b_spec = pl.BlockSpec((tn, tk), lambda i, j, k: (j, k)) if mode == "nt" else pl.BlockSpec((tk, tn), lambda i, j, k: (k, j))
    if stacked and mode == "tn":
        per = N // stacked // tn
        b_spec = pl.BlockSpec((None, tk, tn), lambda i, j, k: (j // per, k, j % per))
    if stacked and mode == "nt":
        per = K // stacked // tk
        a_spec = pl.BlockSpec((None, tm, tk), lambda i, j, k: (k // per, i, k % per))
    o_spec = pl.BlockSpec((tm, tn), lambda i, j, k: (i, j))
    n_epi, n_out = len(epi), len(out_dtypes)

    def body(a_ref, b_ref, *rest):
        epi_refs, out_refs, scratch = rest[:n_epi], rest[n_epi:n_epi + n_out], rest[n_epi + n_out:]
        part = lax.dot_general(a_ref[...].astype(BF16), b_ref[...].astype(BF16), _DIMS[mode], preferred_element_type=F32)

        def finish(acc):
            outs = epilogue(acc, *[r[...] for r in epi_refs]) if epilogue else (acc,)
            for o_ref, o in zip(out_refs, outs):
                o_ref[...] = o.astype(o_ref.dtype)

        if nk == 1:
            finish(part)
        else:
            acc_ref = scratch[0]
            k = pl.program_id(2)

            @pl.when(k == 0)
            def _():
                acc_ref[...] = part

            @pl.when(k > 0)
            def _():
                acc_ref[...] += part

            @pl.when(k == nk - 1)
            def _():
                finish(acc_ref[...])

    outs = _call(
        body, name=name, grid=(M // tm, N // tn, nk),
        in_specs=[a_spec, b_spec] + [o_spec] * n_epi, out_specs=[o_spec] * n_out,
        out_shape=[jax.ShapeDtypeStruct((M, N), dt) for dt in out_dtypes],
        scratch_shapes=[pltpu.VMEM((tm, tn), F32)] if nk > 1 else [],
        sem=("parallel", "parallel", "arbitrary"))(a, b, *epi)
    return outs[0] if n_out == 1 else outs


def _matmul_by_chip_rows(a, b, place, name, rest=None, tm=1024, tn=1024):
    (M, K), N = a.shape, b.shape[1]
    tm, tn, tk = _pick(M, tm), _pick(N, tn), K // N_CHIPS
    nk = N_CHIPS - 1 if rest is None else 1

    def block(k, s):
        diagonal = N_CHIPS - 1 - s[0]
        return k + (k >= diagonal).astype(jnp.int32) if rest is None else diagonal

    def body(s_ref, a_ref, b_ref, *refs):
        o_ref, acc_ref = refs[-2:]
        part = jnp.dot(a_ref[...], b_ref[...], preferred_element_type=F32)
        k = pl.program_id(2)

        @pl.when(k == 0)
        def _():
            acc_ref[...] = part if rest is None else part + refs[0][...]

        @pl.when(k > 0)
        def _():
            acc_ref[...] += part

        @pl.when(k == nk - 1)
        def _():
            o_ref[...] = acc_ref[...]

    tile = pl.BlockSpec((tm, tn), lambda i, j, k, s: (i, j))
    in_specs = [pl.BlockSpec((tm, tk), lambda i, j, k, s: (i, block(k, s))), pl.BlockSpec((tk, tn), lambda i, j, k, s: (block(k, s), j))]
    return _call(body, prefetch=place, name=name, grid=(M // tm, N // tn, nk), in_specs=in_specs + ([] if rest is None else [tile]),
                 out_specs=tile, out_shape=jax.ShapeDtypeStruct((M, N), F32), scratch_shapes=[pltpu.VMEM((tm, tn), F32)],
                 sem=("parallel", "parallel", "arbitrary"))(a, b, *([] if rest is None else [rest]))


def _rms(x, g):
    r = lax.rsqrt(jnp.mean(x * x, axis=-1, keepdims=True) + EPS)
    return x * r * g


def _rms_bwd(dy, x, g):
    r = lax.rsqrt(jnp.mean(x * x, axis=-1, keepdims=True) + EPS)
    xn = x * r
    dyg = dy * g
    dx = r * (dyg - xn * jnp.mean(dyg * xn, axis=-1, keepdims=True))
    return dx, jnp.sum(dy * xn, axis=0, keepdims=True)


def _rows(d, tr=ROW_TILE):
    return pl.BlockSpec((tr, d), lambda i: (i, 0))


def _vec(d):
    return pl.BlockSpec((1, d), lambda i: (0, 0))


def _accumulate(ref, val):
    @pl.when(pl.program_id(0) == 0)
    def _():
        ref[...] = val

    @pl.when(pl.program_id(0) > 0)
    def _():
        ref[...] += val


def _norm_fwd(x, g, name):
    T, D = x.shape

    def body(x_ref, g_ref, h_ref):
        h_ref[...] = _rms(x_ref[...], g_ref[...]).astype(BF16)

    return _call(body,name=name,grid=(T // ROW_TILE,), in_specs=[_rows(D), _vec(D)], out_specs=_rows(D),
                          out_shape=jax.ShapeDtypeStruct((T, D), BF16), sem=("parallel",))(x, g)


def _residual_norm(x, m, g_post, g_next, name):
    T, D = x.shape

    def body(x_ref, m_ref, gp_ref, gn_ref, xo_ref, h_ref):
        xo = x_ref[...] + _rms(m_ref[...], gp_ref[...])
        xo_ref[...] = xo
        h_ref[...] = _rms(xo, gn_ref[...]).astype(BF16)

    return _call(body,name=name,grid=(T // ROW_TILE,), in_specs=[_rows(D), _rows(D), _vec(D), _vec(D)],
                          out_specs=[_rows(D), _rows(D)],
                          out_shape=[jax.ShapeDtypeStruct((T, D), F32), jax.ShapeDtypeStruct((T, D), BF16)],
                          sem=("parallel",))(x, m, g_post, g_next)


def _loss_and_last_norm_bwd(x, m, g_post, target, name):
    T, D = x.shape

    def body(x_ref, m_ref, gp_ref, t_ref, dx_ref, dm_ref, dg_ref, loss_ref):
        m_val, gp = m_ref[...], gp_ref[...]
        err = x_ref[...] + _rms(m_val, gp) - t_ref[...]
        dx = err * (1.0 / D)
        dx_ref[...] = dx
        dm, dg = _rms_bwd(dx, m_val, gp)
        dm_ref[...] = dm.astype(BF16)
        _accumulate(dg_ref, dg)
        _accumulate(loss_ref, jnp.full((1, 128), 0.5 * jnp.sum(err * err) * (1.0 / D), F32))

    return _call(body,name=name,grid=(T // ROW_TILE,), in_specs=[_rows(D), _rows(D), _vec(D), _rows(D)],
                          out_specs=[_rows(D), _rows(D), _vec(D), _vec(128)],
                          out_shape=[jax.ShapeDtypeStruct((T, D), F32), jax.ShapeDtypeStruct((T, D), BF16),
                                     jax.ShapeDtypeStruct((1, D), F32), jax.ShapeDtypeStruct((1, 128), F32)],
                          sem=("arbitrary",))(x, m, g_post, target)


def _norms_bwd(dx, dh, x_in, g_pre, m_prev, g_post_prev, name):
    T, D = dx.shape
    with_prev = m_prev is not None

    def body(*refs):
        if with_prev:
            dx_ref, dh_ref, x_ref, gq_ref, m_ref, gp_ref, dxo_ref, dgq_ref, dm_ref, dgp_ref = refs
        else:
            dx_ref, dh_ref, x_ref, gq_ref, dxo_ref, dgq_ref = refs
        d_in, dgq = _rms_bwd(dh_ref[...], x_ref[...], gq_ref[...])
        dxo = dx_ref[...] + d_in
        dxo_ref[...] = dxo
        _accumulate(dgq_ref, dgq)
        if with_prev:
            dm, dgp = _rms_bwd(dxo, m_ref[...], gp_ref[...])
            dm_ref[...] = dm.astype(BF16)
            _accumulate(dgp_ref, dgp)

    ins, in_specs = [dx, dh, x_in, g_pre], [_rows(D), _rows(D), _rows(D), _vec(D)]
    out_specs = [_rows(D), _vec(D)]
    out_shape = [jax.ShapeDtypeStruct((T, D), F32), jax.ShapeDtypeStruct((1, D), F32)]
    if with_prev:
        ins += [m_prev, g_post_prev]
        in_specs += [_rows(D), _vec(D)]
        out_specs += [_rows(D), _vec(D)]
        out_shape += [jax.ShapeDtypeStruct((T, D), BF16), jax.ShapeDtypeStruct((1, D), F32)]
    return _call(body,name=name,grid=(T // ROW_TILE,), in_specs=in_specs, out_specs=out_specs, out_shape=out_shape,
                          sem=("arbitrary",))(*ins)


def _window_weights(g):
    w = 2 << g
    return w, [jnp.where(j < w, 1.0, 0.0).astype(F32) for j in range(MAX_WINDOW)]


def _valid_count(r0, rows, w):
    t = (lax.broadcasted_iota(jnp.int32, (rows, 1), 0) + (r0 + 1)).astype(F32)
    return jnp.minimum(t, w.astype(F32))


def _pool_fwd(z, pool_w, pool_scale, name):
    T = z.shape[0]
    PG = pool_w.shape[-1]
    DP = N_GROUPS * PG
    rc = min(CHUNK, T)

    def body(u_ref, pw_ref, sc_ref, pooled_ref, y_ref, pad):
        w, wts = _window_weights(pl.program_id(0))
        pad[pl.ds(0, MAX_WINDOW), :] = jnp.zeros((MAX_WINDOW, PG), F32)
        pad[pl.ds(MAX_WINDOW, T), :] = u_ref[...]
        for r0 in range(0, T, rc):
            acc = jnp.zeros((rc, PG), F32)
            for j in range(MAX_WINDOW):
                acc = acc + wts[j] * pad[pl.ds(MAX_WINDOW + r0 - j, rc), :]
            pooled = acc / _valid_count(r0, rc, w) - u_ref[pl.ds(r0, rc), :]
            pooled_ref[pl.ds(r0, rc), :] = pooled.astype(BF16)
        mixed = jnp.dot(pooled_ref[...], pw_ref[...].astype(BF16), preferred_element_type=F32)
        y_ref[...] = (mixed * sc_ref[...]).astype(BF16)

    col = lambda g: (0, g)
    return _call(
        body, name=name,grid=(N_GROUPS,),
        in_specs=[pl.BlockSpec((T, PG), col), pl.BlockSpec((None, PG, PG), lambda g: (g, 0, 0)), pl.BlockSpec((1, PG), col)],
        out_specs=[pl.BlockSpec((T, PG), col), pl.BlockSpec((T, PG), col)],
        out_shape=[jax.ShapeDtypeStruct((T, DP), BF16), jax.ShapeDtypeStruct((T, DP), BF16)],
        scratch_shapes=[pltpu.VMEM((T + MAX_WINDOW, PG), F32)], sem=("parallel",))(z, pool_w, pool_scale)


def _pool_bwd(dy, pooled, pool_w, pool_scale, name):
    T = dy.shape[0]
    PG = pool_w.shape[-1]
    DP = N_GROUPS * PG
    rc = min(CHUNK, T)

    def body(dy_ref, pooled_ref, pw_ref, sc_ref, du_ref, dpw_ref, dsc_ref, pad, dp_ref):
        w, wts = _window_weights(pl.program_id(0))
        pooled_v, pw = pooled_ref[...], pw_ref[...].astype(BF16)
        dy_v = dy_ref[...]
        mixed = jnp.dot(pooled_v, pw, preferred_element_type=F32)
        dsc_ref[...] = jnp.sum(dy_v * mixed, axis=0, keepdims=True)
        dmixed = (dy_v * sc_ref[...]).astype(BF16)
        dpw_ref[...] = lax.dot_general(pooled_v, dmixed, _DIMS["tn"], preferred_element_type=F32)
        dp_ref[...] = lax.dot_general(dmixed, pw, _DIMS["nt"], preferred_element_type=F32)
        pad[pl.ds(T, MAX_WINDOW), :] = jnp.zeros((MAX_WINDOW, PG), F32)
        for r0 in range(0, T, rc):
            pad[pl.ds(r0, rc), :] = dp_ref[pl.ds(r0, rc), :] / _valid_count(r0, rc, w)
        for r0 in range(0, T, rc):
            acc = jnp.zeros((rc, PG), F32)
            for j in range(MAX_WINDOW):
                acc = acc + wts[j] * pad[pl.ds(r0 + j, rc), :]
            du_ref[pl.ds(r0, rc), :] = (acc - dp_ref[pl.ds(r0, rc), :]).astype(BF16)

    col = lambda g: (0, g)
    return _call(
        body, name=name,grid=(N_GROUPS,),
        in_specs=[pl.BlockSpec((T, PG), col), pl.BlockSpec((T, PG), col), pl.BlockSpec((None, PG, PG), lambda g: (g, 0, 0)),
                  pl.BlockSpec((1, PG), col)],
        out_specs=[pl.BlockSpec((T, PG), col), pl.BlockSpec((None, PG, PG), lambda g: (g, 0, 0)), pl.BlockSpec((1, PG), col)],
        out_shape=[jax.ShapeDtypeStruct((T, DP), BF16), jax.ShapeDtypeStruct((N_GROUPS, PG, PG), F32),
                   jax.ShapeDtypeStruct((1, DP), F32)],
        scratch_shapes=[pltpu.VMEM((T + MAX_WINDOW, PG), F32), pltpu.VMEM((T, PG), F32)],
        sem=("parallel",))(dy, pooled, pool_w, pool_scale)


def _conv_fwd(z, conv_w, conv_b, d_pool, name, tc=128):
    T = z.shape[0]
    DC = conv_w.shape[-1]
    rc = min(CHUNK, T)
    v0, g0 = d_pool // tc, (d_pool + DC) // tc

    def body(v_ref, gt_ref, w_ref, b_ref, a_ref, c_ref, pad):
        pad[pl.ds(0, CONV_PAD), :] = jnp.zeros((CONV_PAD, tc), F32)
        for r0 in range(0, T, rc):
            a = v_ref[pl.ds(r0, rc), :] * _sigmoid(gt_ref[pl.ds(r0, rc), :])
            a_ref[pl.ds(r0, rc), :] = a
            pad[pl.ds(CONV_PAD + r0, rc), :] = a
        for r0 in range(0, T, rc):
            acc = jnp.zeros((rc, tc), F32) + b_ref[...]
            for k in range(CONV_K):
                acc = acc + w_ref[pl.ds(k, 1), :] * pad[pl.ds(CONV_PAD - (CONV_K - 1) + k + r0, rc), :]
            c_ref[pl.ds(r0, rc), :] = acc

    col = lambda j: (0, j)
    return _call(
        body, name=name,grid=(DC // tc,),
        in_specs=[pl.BlockSpec((T, tc), lambda j: (0, v0 + j)), pl.BlockSpec((T, tc), lambda j: (0, g0 + j)),
                  pl.BlockSpec((CONV_K, tc), col), pl.BlockSpec((1, tc), col)],
        out_specs=[pl.BlockSpec((T, tc), col), pl.BlockSpec((T, tc), col)],
        out_shape=[jax.ShapeDtypeStruct((T, DC), F32), jax.ShapeDtypeStruct((T, DC), F32)],
        scratch_shapes=[pltpu.VMEM((T + CONV_PAD, tc), F32)], sem=("parallel",))(z, z, conv_w, conv_b)


def _conv_bwd(dc, a, z, conv_w, d_pool, name, tc=128):
    T, DC = dc.shape
    rc = min(CHUNK, T)
    v0, g0 = d_pool // tc, (d_pool + DC) // tc

    def body(dc_ref, a_ref, v_ref, gt_ref, w_ref, dv_ref, dg_ref, dw_ref, db_ref, apad, dpad):
        apad[pl.ds(0, CONV_PAD), :] = jnp.zeros((CONV_PAD, tc), F32)
        apad[pl.ds(CONV_PAD, T), :] = a_ref[...]
        dpad[pl.ds(0, T), :] = dc_ref[...]
        dpad[pl.ds(T, CONV_PAD), :] = jnp.zeros((CONV_PAD, tc), F32)
        db_ref[...] = jnp.sum(dc_ref[...], axis=0, keepdims=True)
        for k in range(CONV_K):
            acc = jnp.zeros((8, tc), F32)
            for r0 in range(0, T, rc):
                prod = dc_ref[pl.ds(r0, rc), :] * apad[pl.ds(CONV_PAD - (CONV_K - 1) + k + r0, rc), :]
                acc = acc + jnp.sum(prod.reshape(rc // 8, 8, tc), axis=0)
            dw_ref[pl.ds(k, 1), :] = jnp.sum(acc, axis=0, keepdims=True)
        for r0 in range(0, T, rc):
            da = jnp.zeros((rc, tc), F32)
            for k in range(CONV_K):
                da = da + w_ref[pl.ds(k, 1), :] * dpad[pl.ds(r0 + (CONV_K - 1) - k, rc), :]
            sig = _sigmoid(gt_ref[pl.ds(r0, rc), :])
            dv_ref[pl.ds(r0, rc), :] = (da * sig).astype(BF16)
            dg_ref[pl.ds(r0, rc), :] = (da * v_ref[pl.ds(r0, rc), :] * sig * (1.0 - sig)).astype(BF16)

    col = lambda j: (0, j)
    return _call(
        body, name=name,grid=(DC // tc,),
        in_specs=[pl.BlockSpec((T, tc), col), pl.BlockSpec((T, tc), col), pl.BlockSpec((T, tc), lambda j: (0, v0 + j)),
                  pl.BlockSpec((T, tc), lambda j: (0, g0 + j)), pl.BlockSpec((CONV_K, tc), col)],
        out_specs=[pl.BlockSpec((T, tc), col), pl.BlockSpec((T, tc), col), pl.BlockSpec((CONV_K, tc), col),
                   pl.BlockSpec((1, tc), col)],
        out_shape=[jax.ShapeDtypeStruct((T, DC), BF16), jax.ShapeDtypeStruct((T, DC), BF16),
                   jax.ShapeDtypeStruct((CONV_K, DC), F32), jax.ShapeDtypeStruct((1, DC), F32)],
        scratch_shapes=[pltpu.VMEM((T + CONV_PAD, tc), F32), pltpu.VMEM((T + CONV_PAD, tc), F32)],
        sem=("parallel",))(dc, a, z, z, conv_w)


def _layer_norm_parts(c, g, b):
    mu = jnp.mean(c, axis=-1, keepdims=True)
    xc = c - mu
    rstd = lax.rsqrt(jnp.mean(xc * xc, axis=-1, keepdims=True) + EPS)
    xhat = xc * rstd
    return xhat, rstd, xhat * g + b


def _ln_silu_fwd(c, g, b, name):
    T, DC = c.shape

    def body(c_ref, g_ref, b_ref, y_ref):
        _, _, ln = _layer_norm_parts(c_ref[...], g_ref[...], b_ref[...])
        y_ref[...] = (ln * _sigmoid(ln)).astype(BF16)

    return _call(body,name=name,grid=(T // ROW_TILE,), in_specs=[_rows(DC), _vec(DC), _vec(DC)], out_specs=_rows(DC),
                          out_shape=jax.ShapeDtypeStruct((T, DC), BF16), sem=("parallel",))(c, g, b)


def _ln_silu_bwd(dy, c, g, b, name):
    T, DC = c.shape

    def body(dy_ref, c_ref, g_ref, b_ref, dc_ref, dg_ref, db_ref):
        gain = g_ref[...]
        xhat, rstd, ln = _layer_norm_parts(c_ref[...], gain, b_ref[...])
        s = _sigmoid(ln)
        dln = dy_ref[...] * (s * (1.0 + ln * (1.0 - s)))
        _accumulate(dg_ref, jnp.sum(dln * xhat, axis=0, keepdims=True))
        _accumulate(db_ref, jnp.sum(dln, axis=0, keepdims=True))
        dxh = dln * gain
        dc_ref[...] = rstd * (dxh - jnp.mean(dxh, axis=-1, keepdims=True) - xhat * jnp.mean(dxh * xhat, axis=-1, keepdims=True))

    return _call(body,name=name,grid=(T // ROW_TILE,),
                          in_specs=[pl.BlockSpec((ROW_TILE, DC), lambda i: (i, 1)), _rows(DC), _vec(DC), _vec(DC)],
                          out_specs=[_rows(DC), _vec(DC), _vec(DC)],
                          out_shape=[jax.ShapeDtypeStruct((T, DC), F32), jax.ShapeDtypeStruct((1, DC), F32),
                                     jax.ShapeDtypeStruct((1, DC), F32)],
                          sem=("arbitrary",))(dy, c, g, b)


def _short_specs(T, DS, tc):
    n = DS // tc
    return [pl.BlockSpec((T, tc), lambda j: (0, j)), pl.BlockSpec((T, tc), lambda j: (0, n + j)),
            pl.BlockSpec((T, tc), lambda j: (0, 2 * n + j))]


def _short_fwd(z, w, name, tc=256):
    T = z.shape[0]
    DS = w.shape[-1]
    rc = min(CHUNK, T)

    def body(b_ref, cg_ref, u_ref, w_ref, y_ref, pad):
        pad[pl.ds(0, SHORT_PAD), :] = jnp.zeros((SHORT_PAD, tc), F32)
        pad[pl.ds(SHORT_PAD, T), :] = cg_ref[...] * u_ref[...]
        for r0 in range(0, T, rc):
            r = jnp.zeros((rc, tc), F32)
            for k in range(SHORT_K):
                r = r + w_ref[pl.ds(k, 1), :] * pad[pl.ds(SHORT_PAD - (SHORT_K - 1) + k + r0, rc), :]
            y_ref[pl.ds(r0, rc), :] = (b_ref[pl.ds(r0, rc), :] * r).astype(BF16)

    col = lambda j: (0, j)
    return _call(body,name=name,grid=(DS // tc,), in_specs=_short_specs(T, DS, tc) + [pl.BlockSpec((SHORT_K, tc), col)],
                          out_specs=pl.BlockSpec((T, tc), col), out_shape=jax.ShapeDtypeStruct((T, DS), BF16),
                          scratch_shapes=[pltpu.VMEM((T + SHORT_PAD, tc), F32)], sem=("parallel",))(z, z, z, w)


def _short_bwd(dy, z, w, name, tc=256):
    T, DS = dy.shape
    rc = min(CHUNK, T)

    def body(dy_ref, b_ref, cg_ref, u_ref, w_ref, dz_ref, dw_ref, qpad, rpad):
        qpad[pl.ds(0, SHORT_PAD), :] = jnp.zeros((SHORT_PAD, tc), F32)
        qpad[pl.ds(SHORT_PAD, T), :] = cg_ref[...] * u_ref[...]
        rpad[pl.ds(0, T), :] = dy_ref[...] * b_ref[...]
        rpad[pl.ds(T, SHORT_PAD), :] = jnp.zeros((SHORT_PAD, tc), F32)
        accs = [jnp.zeros((8, tc), F32) for _ in range(SHORT_K)]
        for r0 in range(0, T, rc):
            r = jnp.zeros((rc, tc), F32)
            dq = jnp.zeros((rc, tc), F32)
            dr = rpad[pl.ds(r0, rc), :]
            for k in range(SHORT_K):
                q_k = qpad[pl.ds(SHORT_PAD - (SHORT_K - 1) + k + r0, rc), :]
                r = r + w_ref[pl.ds(k, 1), :] * q_k
                dq = dq + w_ref[pl.ds(k, 1), :] * rpad[pl.ds(r0 + (SHORT_K - 1) - k, rc), :]
                accs[k] = accs[k] + jnp.sum((dr * q_k).reshape(rc // 8, 8, tc), axis=0)
            dz_ref[0, pl.ds(r0, rc), :] = (dy_ref[pl.ds(r0, rc), :] * r).astype(BF16)
            dz_ref[1, pl.ds(r0, rc), :] = (dq * u_ref[pl.ds(r0, rc), :]).astype(BF16)
            dz_ref[2, pl.ds(r0, rc), :] = (dq * cg_ref[pl.ds(r0, rc), :]).astype(BF16)
        for k in range(SHORT_K):
            dw_ref[pl.ds(k, 1), :] = jnp.sum(accs[k], axis=0, keepdims=True)

    col = lambda j: (0, j)
    tile = pl.BlockSpec((T, tc), col)
    return _call(body,name=name,grid=(DS // tc,),
                          in_specs=[tile] + _short_specs(T, DS, tc) + [pl.BlockSpec((SHORT_K, tc), col)],
                          out_specs=[pl.BlockSpec((3, T, tc), lambda j: (0, 0, j)), pl.BlockSpec((SHORT_K, tc), col)],
                          out_shape=[jax.ShapeDtypeStruct((3, T, DS), BF16), jax.ShapeDtypeStruct((SHORT_K, DS), F32)],
                          scratch_shapes=[pltpu.VMEM((T + SHORT_PAD, tc), F32), pltpu.VMEM((T + SHORT_PAD, tc), F32)],
                          sem=("parallel",))(dy, z, z, z, w)


def _tile_rows(rows, cols, n_bufs):
    budget = VMEM_LIMIT_BYTES * 3 // 4 // (2 * n_bufs * 4 * cols)
    tr = rows
    while tr > budget and tr % 16 == 0:
        tr //= 2
    return tr


def _placed_call(body, name, place, grid, in_specs, out_specs, out_shape, ins):
    return _call(body,prefetch=place, name=name, grid=grid, in_specs=in_specs, out_specs=out_specs, out_shape=out_shape,
                 sem=("parallel",))(*ins)


def _cast_into_full(w, layer, kind, place, name):
    _, R, C = w.shape
    tr = _tile_rows(R, C, 2)
    nb = R // tr
    if kind == "col":
        full, out_spec = (R, C * N_CHIPS), pl.BlockSpec((tr, C), lambda i, s: (i, s[0]))
    else:
        full, out_spec = (R * N_CHIPS, C), pl.BlockSpec((tr, C), lambda i, s: (s[0] * nb + i, 0))

    def body(s_ref, w_ref, o_ref):
        o_ref[...] = w_ref[...].astype(BF16)

    return _placed_call(body, name, place, (nb,), [pl.BlockSpec((None, tr, C), lambda i, s: (layer, i, 0))], out_spec,
                        jax.ShapeDtypeStruct(full, BF16), [w])


def _add_pair(grad, theirs, kind, place, name):
    R, C = grad.shape
    piece_rows = R // 2 if kind == "col" else R // N_CHIPS // 2
    tr = _tile_rows(piece_rows, C, 3)
    nb = piece_rows // tr
    if kind == "col":
        g_spec = pl.BlockSpec((tr, C), lambda i, s: (s[1] * nb + i, 0))
    else:
        g_spec = pl.BlockSpec((tr, C), lambda i, s: ((2 * (i // nb) + s[1]) * nb + i % nb, 0))
    flat = pl.BlockSpec((tr, C), lambda i, s: (i, 0))

    def body(s_ref, a_ref, b_ref, o_ref):
        o_ref[...] = (a_ref[...].astype(F32) + b_ref[...].astype(F32)).astype(BF16)

    return _placed_call(body, name, place, (R // 2 // tr,), [g_spec, flat], flat, jax.ShapeDtypeStruct((R // 2, C), BF16),
                        [grad, theirs])


def _sum_chips(chip_sum, arrived, kind, place, name):
    _, H, W = arrived.shape
    tr = _tile_rows(H, W, 6)
    nb = H // tr
    if kind == "col":
        own_spec = pl.BlockSpec((tr, W), lambda i, s: (i, s[0]))
    else:
        own_spec = pl.BlockSpec((tr, W), lambda i, s: (s[0] * nb + i, 0))

    def body(s_ref, p_ref, r_ref, o_ref):
        acc = p_ref[...].astype(F32)
        for i in range(N_CHIPS - 1):
            acc = acc + r_ref[i].astype(F32)
        o_ref[...] = acc

    return _placed_call(body, name, place, (nb,), [own_spec, pl.BlockSpec((N_CHIPS - 1, tr, W), lambda i, s: (0, i, 0))],
                        pl.BlockSpec((tr, W), lambda i, s: (s[1] * nb + i, 0)), jax.ShapeDtypeStruct((2 * H, W), F32),
                        [chip_sum, arrived])


def _adamw_values(w, g, m, v):
    m = ADAM_B1 * m + (1.0 - ADAM_B1) * g
    v = ADAM_B2 * v + (1.0 - ADAM_B2) * (g * g)
    m_hat = m / (1.0 - ADAM_B1 ** ADAM_STEP)
    v_hat = v / (1.0 - ADAM_B2 ** ADAM_STEP)
    return -ADAM_LR * (m_hat / (jnp.sqrt(v_hat) + ADAM_EPS) + ADAM_WD * w), m, v


def _adamw(w, g, m, v, name, layer=0, carried=None):
    L, R, C = w.shape
    tr = _tile_rows(R, C, 8)

    def body(w_ref, g_ref, m_ref, v_ref, *rest):
        go_ref, d_ref, mo_ref, vo_ref = rest[-4:]
        g_val = g_ref[...]
        d, m_new, v_new = _adamw_values(w_ref[...], g_val, m_ref[...], v_ref[...])
        go_ref[...], d_ref[...], mo_ref[...], vo_ref[...] = g_val, d, m_new, v_new

    lay = pl.BlockSpec((None, tr, C), lambda i: (layer, i, 0))
    ins = [w, g, m, v]
    in_specs = [lay, pl.BlockSpec((tr, C), lambda i: (i, 0)), lay, lay]
    aliases = {}
    if carried is not None:
        ins += list(carried)
        in_specs += [pl.BlockSpec(memory_space=pl.ANY)] * 4
        aliases = {4 + i: i for i in range(4)}
    return _call(body, name=name, grid=(R // tr,), in_specs=in_specs, out_specs=[lay] * 4,
                 out_shape=[jax.ShapeDtypeStruct((L, R, C), F32)] * 4, input_output_aliases=aliases, sem=("parallel",))(*ins)


def _aligned(v, m):
    return v if isinstance(v, int) else pl.multiple_of(v, m)


def _place():
    x, y, c = lax.axis_index("x"), lax.axis_index("y"), lax.axis_index("c")
    other_chips = [(x, 1 - y), (1 - x, y), (1 - x, 1 - y)]
    return x, y, c, 2 * x + y, other_chips


def _chip_index(chip):
    return 2 * chip[0] + chip[1]


def _piece(ref, kind, k, h):
    R, C = ref.shape
    if kind == "col":
        return ref.at[pl.ds(_aligned(h * (R // 2), 16), R // 2), pl.ds(_aligned(k * (C // N_CHIPS), 128), C // N_CHIPS)]
    rs = R // N_CHIPS
    return ref.at[pl.ds(_aligned(k * rs + h * (rs // 2), 16), rs // 2), :]


def _compact_piece(ref, kind, k):
    R2, C = ref.shape
    if kind == "col":
        return ref.at[:, pl.ds(_aligned(k * (C // N_CHIPS), 128), C // N_CHIPS)]
    return ref.at[pl.ds(_aligned(k * (R2 // N_CHIPS), 16), R2 // N_CHIPS), :]


def _half_rows(ref, h):
    R = ref.shape[0]
    return ref.at[pl.ds(_aligned(h * (R // 2), 16), R // 2), :]


class _Copies:
    def __init__(self, send_sems, recv_sems):
        self.send_sems, self.recv_sems = send_sems, recv_sems
        self.n_remote = 0

    def remote(self, src, dst, device):
        k = self.n_remote
        self.n_remote += 1
        return pltpu.make_async_remote_copy(src_ref=src, dst_ref=dst, send_sem=self.send_sems.at[k], recv_sem=self.recv_sems.at[k],
                                            device_id=device, device_id_type=MESH)


class _Job:
    def __init__(self, ins, out_shape, aliases, n_remote, build, sibling_only=False, all_chips=False):
        self.ins, self.out_shape, self.aliases, self.n_remote, self.build = list(ins), list(out_shape), dict(aliases), n_remote, build
        self.sibling_only, self.all_chips = sibling_only, all_chips


class _Flying:
    def __init__(self, job, send_sems, recv_sems, bufs, token):
        self.job, self.send_sems, self.recv_sems, self.bufs, self.token = job, send_sems, recv_sems, bufs, token


def _job_refs(job, buf_refs):
    n_out = len(job.out_shape)
    kept = [i for i in range(len(job.ins)) if i not in job.aliases]
    ins = [buf_refs[job.aliases[i]] if i in job.aliases else buf_refs[n_out + kept.index(i)] for i in range(len(job.ins))]
    return ins, list(buf_refs[:n_out])


SIBLING_BARRIER_ID = 1
CHIPS_BARRIER_IDS = (2, 3)
_CHIPS_ROUNDS = []


def _start_job(job, name, after=()):
    n_in, n_out, n_after = len(job.ins), len(job.out_shape), len(after)
    kept = [i for i in range(n_in) if i not in job.aliases]
    n_bufs = n_out + len(kept)

    def body(*refs):
        in_refs, out_refs = refs[:n_in], refs[n_in + n_after:n_in + n_after + n_out]
        send_sems, recv_sems, token = refs[n_in + n_after + n_bufs:]
        if job.sibling_only:
            barrier = pltpu.get_barrier_semaphore()
            pl.semaphore_signal(barrier, inc=1, device_id=(lax.axis_index("x"), lax.axis_index("y"), 1 - lax.axis_index("c")),
                                device_id_type=MESH)
            pl.semaphore_wait(barrier, 1)
        if job.all_chips:
            x, y, c = lax.axis_index("x"), lax.axis_index("y"), lax.axis_index("c")
            barrier = pltpu.get_barrier_semaphore()
            for p in range(1, N_CHIPS):
                pl.semaphore_signal(barrier, inc=1, device_id=(x ^ (p >> 1), y ^ (p & 1), c), device_id_type=MESH)
            pl.semaphore_wait(barrier, N_CHIPS - 1)
        for d in job.build(in_refs, out_refs, _Copies(send_sems, recv_sems)):
            d.start()
        token[...] = jnp.zeros_like(token)

    aliases = dict(job.aliases)
    aliases.update({i: n_out + k for k, i in enumerate(kept)})
    sems = pltpu.SemaphoreType.DMA((job.n_remote,))
    barrier_id = SIBLING_BARRIER_ID if job.sibling_only else None
    if job.all_chips:
        barrier_id = CHIPS_BARRIER_IDS[len(_CHIPS_ROUNDS) % 2]
        _CHIPS_ROUNDS.append(name)
    outs = pl.pallas_call(
        body, name=name, in_specs=[HBM] * n_in + [ANY] * n_after,
        out_specs=[HBM] * n_bufs + [SEM, SEM, pl.BlockSpec(memory_space=pltpu.VMEM)],
        out_shape=job.out_shape + [jax.ShapeDtypeStruct(job.ins[i].shape, job.ins[i].dtype) for i in kept]
        + [sems, sems, jax.ShapeDtypeStruct((8, 128), F32)],
        input_output_aliases=aliases,
        compiler_params=pltpu.CompilerParams(has_side_effects=pltpu.SideEffectType.DATAFLOW_SIDE_EFFECTING,
                                             collective_id=barrier_id))(*job.ins, *after)
    return _Flying(job, outs[n_bufs], outs[n_bufs + 1], list(outs[:n_bufs]), outs[n_bufs + 2])


def _wait_job(flying, name, after=()):
    job, n_bufs, n_after = flying.job, len(flying.bufs), len(after)

    def body(*refs):
        in_refs, out_refs = _job_refs(job, refs[:n_bufs])
        send_sems, recv_sems = refs[n_bufs:n_bufs + 2]
        copies = job.build(in_refs, out_refs, _Copies(send_sems, recv_sems))
        for d in copies:
            d.wait_send()
        for d in copies:
            d.wait_recv()

    outs = pl.pallas_call(
        body, name=name, in_specs=[HBM] * n_bufs + [SEM, SEM] + [ANY] * n_after, out_specs=[HBM] * n_bufs,
        out_shape=[jax.ShapeDtypeStruct(b.shape, b.dtype) for b in flying.bufs],
        input_output_aliases={i: i for i in range(n_bufs)},
        compiler_params=pltpu.CompilerParams(has_side_effects=pltpu.SideEffectType.DATAFLOW_SIDE_EFFECTING))(
            *flying.bufs, flying.send_sems, flying.recv_sems, *after)
    return list(outs[:len(job.out_shape)]), list(outs[len(job.out_shape):])


def _in_place(arrays):
    return [jax.ShapeDtypeStruct(a.shape, a.dtype) for a in arrays], {u: u for u in range(len(arrays))}


def _rows_part(ref, part, n_parts):
    h = ref.shape[0] // n_parts
    return ref.at[pl.ds(part * h, h), :]


def _gather_job(full, kind, stage):
    def build(in_refs, out_refs, cp):
        x, y, c, me, (y_nbr, x_nbr, diagonal) = _place()
        (ref,) = out_refs
        sibling = (x, y, 1 - c)
        if stage == 1:
            mine = _piece(ref, kind, me, c)
            return [cp.remote(mine, mine, (*y_nbr, c)), cp.remote(mine, mine, (*x_nbr, c))]
        from_y, from_x = _piece(ref, kind, _chip_index(y_nbr), c), _piece(ref, kind, _chip_index(x_nbr), c)
        copies = []
        if stage in (2, "relay"):
            relay_0, relay_1 = _rows_part(from_x, 0, 2), _rows_part(from_y, 1, 2)
            copies += [cp.remote(relay_0, relay_0, (*y_nbr, c)), cp.remote(relay_1, relay_1, (*x_nbr, c))]
        if stage in (2, "direct"):
            copies += [cp.remote(from_y, from_y, sibling), cp.remote(from_x, from_x, sibling)]
        if stage == 3:
            from_diagonal = _piece(ref, kind, _chip_index(diagonal), c)
            copies.append(cp.remote(from_diagonal, from_diagonal, sibling))
        return copies

    return _Job([full], *_in_place([full]), {1: 2, 2: 4, "relay": 2, "direct": 2, 3: 1}[stage], build,
                sibling_only=stage in ("direct", 3))


def _gather_small_job(fulls, axes):
    def build(in_refs, out_refs, cp):
        x, y, c, me, chips = _place()
        copies = []
        for ref, ax in zip(out_refs, axes):
            n = ref.shape[ax] // N_CHIPS
            idx = [slice(None)] * len(ref.shape)
            idx[ax] = pl.ds(_aligned(me * n, n), n)
            mine = ref.at[tuple(idx)]
            copies += [cp.remote(mine, mine, (*chip, c)) for chip in chips]
        return copies

    return _Job(fulls, *_in_place(fulls), 3 * len(fulls), build)


def _exchange_halves_job(grads, kinds):
    def build(in_refs, out_refs, cp):
        x, y, c, me, chips = _place()
        copies = []
        for src, dst, kind in zip(in_refs, out_refs, kinds):
            if kind == "col":
                copies.append(cp.remote(_half_rows(src, 1 - c), dst, (x, y, 1 - c)))
            else:
                copies += [cp.remote(_piece(src, "row", k, 1 - c), _compact_piece(dst, "row", k), (x, y, 1 - c))
                           for k in range(N_CHIPS)]
        return copies

    out_shape = [jax.ShapeDtypeStruct((g.shape[0] // 2, g.shape[1]), g.dtype) for g in grads]
    return _Job(grads, out_shape, {}, sum(1 if k == "col" else N_CHIPS for k in kinds), build, sibling_only=True)


def _scatter_job(half, kind):
    def build(in_refs, out_refs, cp):
        x, y, c, me, chips = _place()
        (src,), (dst,) = in_refs, out_refs
        return [cp.remote(_compact_piece(src, kind, _chip_index(chip)), dst.at[r], (*chip, c)) for r, chip in enumerate(chips)]

    part_shape = (half.shape[0], half.shape[1] // N_CHIPS) if kind == "col" else (half.shape[0] // N_CHIPS, half.shape[1])
    return _Job([half], [jax.ShapeDtypeStruct((N_CHIPS - 1,) + part_shape, half.dtype)], {}, N_CHIPS - 1, build, all_chips=True)


def _share_job(shards):
    def build(in_refs, out_refs, cp):
        x, y, c, me, chips = _place()
        copies = []
        for ref in out_refs:
            mine = _half_rows(ref, c)
            copies.append(cp.remote(mine, mine, (x, y, 1 - c)))
        return copies

    return _Job(shards, *_in_place(shards), len(shards), build, sibling_only=True)


N_DEVICES = 2 * N_CHIPS


def _small_exchange_job(slots):
    def build(in_refs, out_refs, cp):
        x, y, c, me, chips = _place()
        (ref,) = out_refs
        mine = ref.at[2 * me + c]
        return [cp.remote(mine, mine, (x ^ (p >> 2), y ^ ((p >> 1) & 1), c ^ (p & 1))) for p in range(1, N_DEVICES)]

    return _Job([slots], *_in_place([slots]), N_DEVICES - 1, build)


def _in_own_slot(packed, place, name):
    R, C = packed.shape

    def body(s_ref, p_ref, o_ref):
        o_ref[...] = p_ref[...]

    return _placed_call(body, name, place, (1,), [pl.BlockSpec((R, C), lambda i, s: (0, 0))],
                        pl.BlockSpec((None, R, C), lambda i, s: (2 * s[0] + s[1], 0, 0)),
                        jax.ShapeDtypeStruct((N_DEVICES, R, C), F32), [packed])


def _sum_slots(slots, name):
    n, R, C = slots.shape

    def body(s_ref, o_ref):
        acc = s_ref[0]
        for i in range(1, n):
            acc = acc + s_ref[i]
        o_ref[...] = acc

    return _call(body,name=name, grid=(1,), in_specs=[pl.BlockSpec((n, R, C), lambda i: (0, 0, 0))],
                 out_specs=pl.BlockSpec((R, C), lambda i: (0, 0)), out_shape=jax.ShapeDtypeStruct((R, C), F32),
                 sem=("arbitrary",))(slots)


def _packed_rows(size, width):
    return -(-size // (8 * width)) * 8


def _pack(arrays, width):
    rows = []
    for a in arrays:
        flat = a.reshape(-1)
        n_rows = _packed_rows(flat.shape[0], width)
        rows.append(jnp.pad(flat, (0, n_rows * width - flat.shape[0])).reshape(n_rows, width))
    return jnp.concatenate(rows, axis=0)


def _unpack(packed, shapes):
    out, r0, width = [], 0, packed.shape[1]
    for shape in shapes:
        size = 1
        for d in shape:
            size *= d
        out.append(packed[r0:r0 + _packed_rows(size, width)].reshape(-1)[:size].reshape(shape))
        r0 += _packed_rows(size, width)
    return out


class _Backlog:
    def __init__(self, first):
        self.now, self.free, self.flights, self.last, self.chain = 0.0, {"ici": 0.0, "d2d": 0.0}, [], first, []

    def run(self, fn, us, *args, **kw):
        out = fn(*args, **kw)
        self.now += us
        self.last = out[0] if isinstance(out, (list, tuple)) else out
        self.poll()
        return out

    def start(self, job, name, link, cost, done):
        flying = _start_job(job, name + "_start", self.chain)
        self.chain = [flying.token]
        _Behind.pending.append(flying.token)
        ends = max(self.now, self.free[link]) + cost
        self.free[link] = ends
        self.flights.append((ends + LANDING_SLACK_US, name, flying, done))
        self.flights.sort(key=lambda f: f[0])
        return flying

    def poll(self, block=False):
        while self.flights and (block or self.flights[0][0] <= self.now):
            ends, name, flying, done = self.flights.pop(0)
            self.now, block = max(self.now, ends), False
            done(*_wait_job(flying, name + "_wait", [self.last] + self.chain))


class _GatherStream:
    def __init__(self, backlog, bufs, kinds, costs, early=()):
        self.backlog, self.bufs, self.kinds, self.costs, self.begun, self.complete = backlog, bufs, kinds, costs, 0, set()
        self.early, self.relays, self.near_complete = set(early), {}, set()
        self.begin()

    def begin(self):
        u, self.begun = self.begun, self.begun + 1
        self.backlog.start(_gather_job(self.bufs[u], self.kinds[u], 1), "gather_%d" % u, "ici", 0.5 * self.costs[u],
                           lambda outs, kept: self.arrived(u, outs[0]))

    def arrived(self, u, buf):
        self.bufs[u] = buf
        if u in self.early:
            self.relays[u] = self.backlog.start(_gather_job(buf, self.kinds[u], "relay"), "relay_%d" % u, "ici",
                                                0.25 * self.costs[u], lambda outs, kept: self.relayed(u, outs[0]))
            self.backlog.start(_gather_job(self.relays[u].bufs[0], self.kinds[u], "direct"), "direct_%d" % u, "d2d",
                               0.5 * D2D_SHARE * self.costs[u], lambda outs, kept: self.near(u, outs[0]))
        else:
            self.backlog.start(_gather_job(buf, self.kinds[u], 2), "relay_%d" % u, "ici", 0.25 * self.costs[u],
                               lambda outs, kept: self.relayed(u, outs[0]))
        while self.begun <= min(u + GATHER_WINDOW[u], len(self.bufs) - 1):
            self.begin()

    def near(self, u, buf):
        self.bufs[u] = self.relays[u].bufs[0] = buf
        self.near_complete.add(u)

    def nearly_ready(self, u):
        while u not in self.near_complete:
            assert self.backlog.flights, "weight %d is not on its way" % u
            self.backlog.poll(block=True)
        return self.bufs[u]

    def relayed(self, u, buf):
        assert u not in self.early or u in self.near_complete, "the relay of weight %d is waited for before its sibling copies" % u
        self.bufs[u] = buf
        self.backlog.start(_gather_job(buf, self.kinds[u], 3), "handon_%d" % u, "d2d", D2D_SHARE * self.costs[u],
                           lambda outs, kept: self.handed(u, outs[0]))

    def handed(self, u, buf):
        self.bufs[u] = buf
        self.complete.add(u)

    def ready(self, u):
        while u not in self.complete:
            assert self.backlog.flights, "weight %d is not on its way" % u
            self.backlog.poll(block=True)
        return self.bufs[u]


class _GradStream:
    def __init__(self, backlog, u, name, kind, cost, g, place, results):
        self.backlog, self.u, self.name, self.kind, self.cost, self.place, self.results = backlog, u, name, kind, cost, place, results
        backlog.start(_exchange_halves_job([g], [kind]), "to_sibling_" + name, "d2d", D2D_SHARE * cost, self.exchanged)

    def exchanged(self, outs, kept):
        chip_sum = self.backlog.run(_add_pair, SIDE_KERNEL_US, kept[0], outs[0], self.kind, self.place, "chip_sum_" + self.name)
        self.backlog.start(_scatter_job(chip_sum, self.kind), "to_owners_" + self.name, "ici", self.cost, self.scattered)

    def scattered(self, outs, kept):
        reduced = self.backlog.run(_sum_chips, SIDE_KERNEL_US, kept[0], outs[0], self.kind, self.place, "reduce_" + self.name)
        self.backlog.start(_share_job([reduced]), "share_" + self.name, "d2d", D2D_SHARE * self.cost, self.shared)

    def shared(self, outs, kept):
        self.results[self.u] = outs[0]


SIDE_KERNEL_US = 12.0
D2D_SHARE = 0.15
LANDING_SLACK_US = 0.0
GATHER_WINDOW = (1, 1, 1, 2, 2, 1, 1, 1)


def kernel(x, mix_pre_g, mix_post_g, ffn_pre_g, ffn_post_g, ab_w_in, pool_w, pool_scale, conv_w, conv_b, conv_ln_g, conv_ln_b, ab_w_out, sc_w_in, sc_conv_w, sc_w_out, ffn_w1, ffn_w2, loss_target, m_mix_pre_g, m_mix_post_g, m_ffn_pre_g, m_ffn_post_g, m_ab_w_in, m_pool_w, m_pool_scale, m_conv_w, m_conv_b, m_conv_ln_g, m_conv_ln_b, m_ab_w_out, m_sc_w_in, m_sc_conv_w, m_sc_w_out, m_ffn_w1, m_ffn_w2, v_mix_pre_g, v_mix_post_g, v_ffn_pre_g, v_ffn_post_g, v_ab_w_in, v_pool_w, v_pool_scale, v_conv_w, v_conv_b, v_conv_ln_g, v_conv_ln_b, v_ab_w_out, v_sc_w_in, v_sc_conv_w, v_sc_w_out, v_ffn_w1, v_ffn_w2):
    x0, target = x[0], loss_target[0]
    T, D = x0.shape
    DP = pool_scale.shape[-1]
    gain = lambda g, layer: g[layer][None, :]

    big = [("ab_w_in", ab_w_in, 0, "col", 67.0), ("ab_w_out", ab_w_out, 0, "row", 44.0),
           ("ffn_w1_0", ffn_w1, 0, "col", 177.0), ("ffn_w2_0", ffn_w2, 0, "row", 177.0),
           ("sc_w_in", sc_w_in, 0, "col", 133.0), ("sc_w_out", sc_w_out, 0, "row", 44.0),
           ("ffn_w1_1", ffn_w1, 1, "col", 177.0), ("ffn_w2_1", ffn_w2, 1, "row", 177.0)]
    kinds = [b[3] for b in big]
    chip = 2 * lax.axis_index("x") + lax.axis_index("y")
    place = jnp.stack([chip, lax.axis_index("c")]).astype(jnp.int32)

    def own_in_zeros(shard, ax):
        full = jnp.zeros(tuple(d * N_CHIPS if i == ax else d for i, d in enumerate(shard.shape)), shard.dtype)
        return lax.dynamic_update_slice_in_dim(full, shard, chip * shard.shape[ax], axis=ax)

    W = [_cast_into_full(w, layer, kind, place, "cast_" + name) for name, w, layer, kind, _ in big]
    smalls = [own_in_zeros(pool_w[0], 1), own_in_zeros(conv_w[0], 1), own_in_zeros(sc_conv_w[0], 1)]
    backlog = _Backlog(x0)
    run = backlog.run
    small_weights = []
    backlog.start(_gather_small_job(smalls, [1, 1, 1]), "gather_small", "ici", 6.0, lambda outs, kept: small_weights.extend(outs))
    gather = _GatherStream(backlog, W, kinds, [b[4] for b in big], early=(3, 7))

    relu_sq = lambda acc: (jnp.maximum(acc, 0.0), jnp.square(jnp.maximum(acc, 0.0)))
    relu_sq_bwd = lambda acc, a: (acc * (2.0 * a.astype(F32)),)

    h0 = run(_norm_fwd, 12.0, x0, gain(mix_pre_g, 0), "norm_in")
    z0 = run(_matmul, 35.0, h0, gather.ready(0), "nn", "mix0_in")
    while not small_weights:
        backlog.poll(block=True)
    pool_w_full, conv_w_full, sc_conv_w_full = small_weights
    pooled, y_pool = run(_pool_fwd, 23.0, z0, pool_w_full, pool_scale, "pool_fwd")
    a_conv, c_conv = run(_conv_fwd, 25.0, z0, conv_w_full, conv_b, DP, "conv_fwd")
    y_conv = run(_ln_silu_fwd, 10.0, c_conv, conv_ln_g, conv_ln_b, "ln_silu_fwd")
    y0 = jnp.concatenate([y_pool, y_conv], axis=1)
    m0 = run(_matmul, 25.0, y0, gather.ready(1), "nn", "mix0_out")
    x1, h1 = run(_residual_norm, 21.0, x0, m0, gain(mix_post_g, 0), gain(ffn_pre_g, 0), "res_mix0")
    a0, a0sq = run(_matmul, 81.0, h1, gather.ready(2), "nn", "ffn0_up", out_dtypes=(BF16, BF16), epilogue=relu_sq)
    f0 = run(_matmul_by_chip_rows, 63.0, a0sq, gather.nearly_ready(3), place, "ffn0_down_near")
    f0 = run(_matmul_by_chip_rows, 21.0, a0sq, gather.ready(3), place, "ffn0_down", rest=f0)
    x2, h2 = run(_residual_norm, 22.0, x1, f0, gain(ffn_post_g, 0), gain(mix_pre_g, 1), "res_ffn0")
    z1 = run(_matmul, 62.0, h2, gather.ready(4), "nn", "mix1_in")
    y1 = run(_short_fwd, 22.0, z1, sc_conv_w_full, "short_fwd")
    m1 = run(_matmul, 25.0, y1, gather.ready(5), "nn", "mix1_out")
    x3, h3 = run(_residual_norm, 21.0, x2, m1, gain(mix_post_g, 1), gain(ffn_pre_g, 1), "res_mix1")
    a1, a1sq = run(_matmul, 81.0, h3, gather.ready(6), "nn", "ffn1_up", out_dtypes=(BF16, BF16), epilogue=relu_sq)
    f1 = run(_matmul_by_chip_rows, 63.0, a1sq, gather.nearly_ready(7), place, "ffn1_down_near")
    f1 = run(_matmul_by_chip_rows, 21.0, a1sq, gather.ready(7), place, "ffn1_down", rest=f1)
    w_in0, w_out0, w1_0, w2_0, w_in1, w_out1, w1_1, w2_1 = W

    grads_big = [None] * len(big)


    def reduce_grad(u, g):
        name, _, _, kind, cost = big[u]
        _GradStream(backlog, u, name, kind, cost, g, place, grads_big)

    dx, df1, d_ffn_post_1, loss_row = run(_loss_and_last_norm_bwd, 30.0, x3, f1, gain(ffn_post_g, 1), target, "loss")
    reduce_grad(7, run(_matmul, 80.0, a1sq, df1, "tn", "ffn1_down_dw", out_dtypes=(BF16,)))
    dz = run(_matmul, 82.0, df1, w2_1, "nt", "ffn1_down_dx", out_dtypes=(BF16,), epilogue=relu_sq_bwd, epi=(a1,))
    reduce_grad(6, run(_matmul, 80.0, h3, dz, "tn", "ffn1_up_dw", out_dtypes=(BF16,)))
    dh = run(_matmul, 87.0, dz, w1_1, "nt", "ffn1_up_dx", tk=LONG_K_TILE)
    dx, d_ffn_pre_1, dm1, d_mix_post_1 = run(_norms_bwd, 36.0, dx, dh, x3, gain(ffn_pre_g, 1), m1, gain(mix_post_g, 1), "norms_bwd3")

    reduce_grad(5, run(_matmul, 24.0, y1, dm1, "tn", "mix1_out_dw", out_dtypes=(BF16,)))
    dy1 = run(_matmul, 25.0, dm1, w_out1, "nt", "mix1_out_dx")
    dz1, d_sc_conv_w = run(_short_bwd, 41.0, dy1, z1, sc_conv_w_full, "short_bwd")
    reduce_grad(4, run(_matmul, 62.0, h2, dz1, "tn", "mix1_in_dw", out_dtypes=(BF16,), stacked=3))
    dh = run(_matmul, 68.0, dz1, w_in1, "nt", "mix1_in_dx", stacked=3)
    dx, d_mix_pre_1, df0, d_ffn_post_0 = run(_norms_bwd, 35.0, dx, dh, x2, gain(mix_pre_g, 1), f0, gain(ffn_post_g, 0), "norms_bwd2")

    reduce_grad(3, run(_matmul, 80.0, a0sq, df0, "tn", "ffn0_down_dw", out_dtypes=(BF16,)))
    dz = run(_matmul, 82.0, df0, w2_0, "nt", "ffn0_down_dx", out_dtypes=(BF16,), epilogue=relu_sq_bwd, epi=(a0,))
    reduce_grad(2, run(_matmul, 80.0, h1, dz, "tn", "ffn0_up_dw", out_dtypes=(BF16,)))
    dh = run(_matmul, 87.0, dz, w1_0, "nt", "ffn0_up_dx", tk=LONG_K_TILE)
    dx, d_ffn_pre_0, dm0, d_mix_post_0 = run(_norms_bwd, 36.0, dx, dh, x1, gain(ffn_pre_g, 0), m0, gain(mix_post_g, 0), "norms_bwd1")

    reduce_grad(1, run(_matmul, 24.0, y0, dm0, "tn", "mix0_out_dw", out_dtypes=(BF16,)))
    dy0 = run(_matmul, 25.0, dm0, w_out0, "nt", "mix0_out_dx")
    du_pool, d_pool_w, d_pool_scale = run(_pool_bwd, 28.0, dy0, pooled, pool_w_full, pool_scale, "pool_bwd")
    dc, d_ln_g, d_ln_b = run(_ln_silu_bwd, 15.0, dy0, c_conv, conv_ln_g, conv_ln_b, "ln_silu_bwd")
    dv, dgate, d_conv_w, d_conv_b = run(_conv_bwd, 52.0, dc, a_conv, z0, conv_w_full, DP, "conv_bwd")
    dz0 = jnp.concatenate([du_pool, dv, dgate], axis=1)

    small_sums = {}

    def exchange_small(key, arrays, cost):
        slots = _in_own_slot(_pack(arrays, D), place, "small_grads_slot_" + key)
        backlog.start(_small_exchange_job(slots), "small_grads_" + key, "ici", cost,
                      lambda outs, kept: small_sums.__setitem__(key, _unpack(_sum_slots(outs[0], "small_grads_sum_" + key),
                                                                             [a.shape for a in arrays])))

    exchange_small("most", [d_mix_pre_1, jnp.concatenate([d_mix_post_0, d_mix_post_1], 0),
                            jnp.concatenate([d_ffn_pre_0, d_ffn_pre_1], 0), jnp.concatenate([d_ffn_post_0, d_ffn_post_1], 0),
                            d_pool_scale, d_conv_b, d_ln_g, d_ln_b, d_pool_w, d_conv_w, d_sc_conv_w], 112.0)
    reduce_grad(0, run(_matmul, 34.0, h0, dz0, "tn", "mix0_in_dw", out_dtypes=(BF16,)))
    dh = run(_matmul, 40.0, dz0, w_in0, "nt", "mix0_in_dx")
    grad_x, d_mix_pre_0 = run(_norms_bwd, 26.0, dx, dh, x0, gain(mix_pre_g, 0), None, None, "norms_bwd0")
    exchange_small("last", [d_mix_pre_0, loss_row], 5.0)

    upd, gr, first = {}, grads_big, {}

    def keep(where, key, outs):
        where[key] = outs
        return outs

    adamw_big = [
        (7, lambda: keep(first, "ffn_w2", _adamw(ffn_w2, gr[7], m_ffn_w2, v_ffn_w2, "adamw_ffn_w2_1", layer=1)), 46.0),
        (6, lambda: keep(first, "ffn_w1", _adamw(ffn_w1, gr[6], m_ffn_w1, v_ffn_w1, "adamw_ffn_w1_1", layer=1)), 46.0),
        (5, lambda: keep(upd, "sc_w_out", _adamw(sc_w_out, gr[5], m_sc_w_out, v_sc_w_out, "adamw_sc_w_out")), 14.0),
        (4, lambda: keep(upd, "sc_w_in", _adamw(sc_w_in, gr[4], m_sc_w_in, v_sc_w_in, "adamw_sc_w_in")), 35.0),
        (3, lambda: keep(upd, "ffn_w2", _adamw(ffn_w2, gr[3], m_ffn_w2, v_ffn_w2, "adamw_ffn_w2_0", layer=0,
                                               carried=first["ffn_w2"])), 46.0),
        (2, lambda: keep(upd, "ffn_w1", _adamw(ffn_w1, gr[2], m_ffn_w1, v_ffn_w1, "adamw_ffn_w1_0", layer=0,
                                               carried=first["ffn_w1"])), 46.0),
        (1, lambda: keep(upd, "ab_w_out", _adamw(ab_w_out, gr[1], m_ab_w_out, v_ab_w_out, "adamw_ab_w_out")), 14.0),
        (0, lambda: keep(upd, "ab_w_in", _adamw(ab_w_in, gr[0], m_ab_w_in, v_ab_w_in, "adamw_ab_w_in")), 19.0)]
    while adamw_big or backlog.flights:
        due = [a for a in adamw_big if gr[a[0]] is not None]
        if due:
            adamw_big.remove(due[0])
            backlog.run(due[0][1], due[0][2])
        else:
            backlog.poll(block=True)

    (g_mix_pre_1, g_mix_post, g_ffn_pre, g_ffn_post, g_pool_scale, g_conv_b, g_ln_g, g_ln_b, g_pool_w_full, g_conv_w_full,
     g_sc_conv_w_full) = small_sums["most"]
    g_mix_pre = jnp.concatenate([small_sums["last"][0], g_mix_pre_1], 0)
    loss = small_sums["last"][1][0, 0]
    own = lambda a, ax: lax.dynamic_slice_in_dim(a, chip * (a.shape[ax] // N_CHIPS), a.shape[ax] // N_CHIPS, axis=ax)
    g_pool_w, g_conv_w, g_sc_conv_w = own(g_pool_w_full, 1), own(g_conv_w_full, 1), own(g_sc_conv_w_full, 1)

    def small_update(w, g, m, v, name):
        shape = w.shape
        as3 = lambda a: a.reshape((1, -1, shape[-1]))
        outs = _adamw(as3(w), g.reshape((-1, shape[-1])), as3(m), as3(v), "adamw_" + name)
        return [o.reshape(shape) for o in outs]

    upd["mix_pre_g"] = small_update(mix_pre_g, g_mix_pre, m_mix_pre_g, v_mix_pre_g, "mix_pre_g")
    upd["mix_post_g"] = small_update(mix_post_g, g_mix_post, m_mix_post_g, v_mix_post_g, "mix_post_g")
    upd["ffn_pre_g"] = small_update(ffn_pre_g, g_ffn_pre, m_ffn_pre_g, v_ffn_pre_g, "ffn_pre_g")
    upd["ffn_post_g"] = small_update(ffn_post_g, g_ffn_post, m_ffn_post_g, v_ffn_post_g, "ffn_post_g")
    upd["pool_w"] = small_update(pool_w, g_pool_w, m_pool_w, v_pool_w, "pool_w")
    upd["pool_scale"] = small_update(pool_scale, g_pool_scale, m_pool_scale, v_pool_scale, "pool_scale")
    upd["conv_w"] = small_update(conv_w, g_conv_w, m_conv_w, v_conv_w, "conv_w")
    upd["conv_b"] = small_update(conv_b, g_conv_b, m_conv_b, v_conv_b, "conv_b")
    upd["conv_ln_g"] = small_update(conv_ln_g, g_ln_g, m_conv_ln_g, v_conv_ln_g, "conv_ln_g")
    upd["conv_ln_b"] = small_update(conv_ln_b, g_ln_b, m_conv_ln_b, v_conv_ln_b, "conv_ln_b")
    upd["sc_conv_w"] = small_update(sc_conv_w, g_sc_conv_w, m_sc_conv_w, v_sc_conv_w, "sc_conv_w")

    order = ["mix_pre_g", "mix_post_g", "ffn_pre_g", "ffn_post_g", "ab_w_in", "pool_w", "pool_scale", "conv_w", "conv_b",
             "conv_ln_g", "conv_ln_b", "ab_w_out", "sc_w_in", "sc_conv_w", "sc_w_out", "ffn_w1", "ffn_w2"]
    out = [loss, grad_x[None]]
    for part in range(4):
        out += [upd[n][part] for n in order]
    return tuple(out)
```
